```python
import math, functools
import jax, jax.numpy as jnp
from jax import lax
import numpy as np

D_MODEL = 1024
BATCH = 2
SEQ = 8192
DEPTH = 2
DEC_BATCH = 128
DEC_SEQ = 1
PAST_LEN = 2048
PAGE_SIZE = 128

N_EVEN = (DEPTH + 1) // 2
N_ODD = DEPTH // 2
MIX_W = D_MODEL // 2
H_A = 4
DH_A = MIX_W // (2 * H_A)
H_B = 4
DK_B = MIX_W // H_B
DV_B = MIX_W // H_B
H_C = 4
DK_C = MIX_W // H_C
DV_C = MIX_W // H_C
H_D = 4
DK_D = MIX_W // H_D
DV_D = MIX_W // H_D
H_X = 4
DH_X = D_MODEL // H_X
N_MEM = 256
D_FF = ((8 * D_MODEL // 3 + 255) // 256) * 256
CONV_W = 3
CHUNK = 64
Q_BLOCK = 128
ROPE_THETA = 10000.0
NORM_EPS = 1e-5
DN_ALPHA = (2.0 * DEPTH) ** 0.25
DN_BETA = (8.0 * DEPTH) ** -0.25
EV_IN = 7 * MIX_W
OD_IN = 8 * MIX_W + 2 * H_D

kernel_name = "hybrid_diffattn_hgrn2_retnet_mlstm_step"

F32 = jnp.float32


def _split(a, sizes):
    cuts = [int(c) for c in np.cumsum(sizes)[:-1]]
    return jnp.split(a, cuts, axis=-1)


def _heads(a, h):
    B, T, _ = a.shape
    return a.reshape(B, T, h, -1).transpose(0, 2, 1, 3)


def _merge(a):
    B, h, T, d = a.shape
    return a.transpose(0, 2, 1, 3).reshape(B, T, h * d)


def _layernorm(x, g, b):
    xf = x.astype(F32)
    mu = jnp.mean(xf, -1, keepdims=True)
    var = jnp.mean(jnp.square(xf - mu), -1, keepdims=True)
    return ((xf - mu) * lax.rsqrt(var + NORM_EPS) * g.astype(F32) + b.astype(F32)).astype(x.dtype)


def _rmsnorm(x, g):
    xf = x.astype(F32)
    return (xf * lax.rsqrt(jnp.mean(jnp.square(xf), -1, keepdims=True) + NORM_EPS) * g.astype(F32)).astype(x.dtype)


def _groupnorm(x, g):
    xf = x.astype(F32)
    mu = jnp.mean(xf, -1, keepdims=True)
    var = jnp.mean(jnp.square(xf - mu), -1, keepdims=True)
    return ((xf - mu) * lax.rsqrt(var + NORM_EPS) * g.astype(F32)).astype(x.dtype)


def _rope(x, pos):
    d = x.shape[-1]
    inv = ROPE_THETA ** (-jnp.arange(0, d // 2, dtype=F32) * 2.0 / d)
    ang = pos.astype(F32)[:, None] * inv[None, :]
    ang = ang.reshape((ang.shape[0],) + (1,) * (x.ndim - 3) + (d // 2,))
    cos, sin = jnp.cos(ang), jnp.sin(ang)
    x1 = x[..., : d // 2].astype(F32)
    x2 = x[..., d // 2:].astype(F32)
    return jnp.concatenate([x1 * cos - x2 * sin, x2 * cos + x1 * sin], -1).astype(x.dtype)


def _gated_linear_attention(q, k, v, log_f, s0):
    B, H, T, dk = q.shape
    dv = v.shape[-1]
    L = math.gcd(T, CHUNK)
    nc = T // L

    def blocks(a):
        return jnp.moveaxis(a.astype(F32).reshape(B, H, nc, L, a.shape[-1]), 2, 0)

    qc, kc, vc, gc = blocks(q), blocks(k), blocks(v), blocks(log_f)
    b = jnp.cumsum(gc, axis=3)
    b_end = b[:, :, :, -1:, :]
    q_in = qc * jnp.exp(b)
    k_in = kc * jnp.exp(-b)
    k_end = kc * jnp.exp(b_end - b)
    causal = jnp.tril(jnp.ones((L, L), bool))
    attn = jnp.where(causal, jnp.einsum('nbhld,nbhsd->nbhls', q_in, k_in), 0.0)
    o_intra = jnp.einsum('nbhls,nbhse->nbhle', attn, vc)

    def step(S, inp):
        q_i, k_e, v_i, dec = inp
        o_inter = jnp.einsum('bhld,bhde->bhle', q_i, S)
        S = dec[:, :, 0, :, None] * S + jnp.einsum('bhld,bhle->bhde', k_e, v_i)
        return S, o_inter

    s_T, o_inter = lax.scan(step, s0.astype(F32), (q_in, k_end, vc, jnp.exp(b_end)))
    o = jnp.moveaxis(o_intra + o_inter, 0, 2).reshape(B, H, T, dv)
    return o, s_T


def _mlstm(q, k, v, log_i, log_f, c0, n0, m0):
    B, H, T, dk = q.shape
    dv = v.shape[-1]
    L = math.gcd(T, CHUNK)
    nc = T // L

    def blocks(a):
        return jnp.moveaxis(a.astype(F32).reshape((B, H, nc, L) + a.shape[3:]), 2, 0)

    qc, kc, vc, ic, fc = blocks(q), blocks(k), blocks(v), blocks(log_i), blocks(log_f)
    causal = jnp.tril(jnp.ones((L, L), bool))

    def step(carry, inp):
        C, n, m = carry
        q_i, k_i, v_i, li, lf = inp
        b = jnp.cumsum(lf, -1)
        Dm = jnp.where(causal, b[..., :, None] - b[..., None, :] + li[..., None, :], -jnp.inf)
        m_t = jnp.maximum(b + m[..., None], jnp.max(Dm, -1))
        inter = jnp.exp(b + m[..., None] - m_t)
        W = jnp.exp(Dm - m_t[..., None]) * jnp.einsum('bhld,bhsd->bhls', q_i, k_i)
        num = inter[..., None] * jnp.einsum('bhld,bhde->bhle', q_i, C) + jnp.einsum('bhls,bhse->bhle', W, v_i)
        den = inter * jnp.einsum('bhld,bhd->bhl', q_i, n) + jnp.sum(W, -1)
        h = num / jnp.maximum(jnp.abs(den), jnp.exp(-m_t))[..., None]
        m_new = m_t[..., -1]
        c_scale = jnp.exp(b[..., -1] + m - m_new)
        w = jnp.exp(b[..., -1:] - b + li - m_new[..., None])
        C = c_scale[..., None, None] * C + jnp.einsum('bhl,bhld,bhle->bhde', w, k_i, v_i)
        n = c_scale[..., None] * n + jnp.einsum('bhl,bhld->bhd', w, k_i)
        return (C, n, m_new), h

    (C, n, m), h = lax.scan(step, (c0.astype(F32), n0.astype(F32), m0.astype(F32)), (qc, kc, vc, ic, fc))
    h = jnp.moveaxis(h, 0, 2).reshape(B, H, T, dv)
    return h, C, n, m


def _diff_attn(q, k, v, q_pos, k_pos, lam):
    s = jnp.einsum('bqhcd,bkhcd->bhcqk', q, k).astype(F32) * (DH_A ** -0.5)
    s = jnp.where(q_pos[:, None] >= k_pos[None, :], s, -jnp.inf)
    p = jax.nn.softmax(s, axis=-1)
    a = p[:, :, 0] - lam * p[:, :, 1]
    return jnp.einsum('bhqk,bkhe->bqhe', a.astype(v.dtype), v)


def _attend_prompt(q, k, v, lam):
    B, T = q.shape[0], q.shape[1]
    qb_len = math.gcd(T, Q_BLOCK)
    k_pos = jnp.arange(T)

    def block(i):
        start = i * qb_len
        qb = lax.dynamic_slice_in_dim(q, start, qb_len, axis=1)
        return _diff_attn(qb, k, v, start + jnp.arange(qb_len), k_pos, lam)

    o = lax.map(block, jnp.arange(T // qb_len))
    return jnp.moveaxis(o, 0, 1).reshape(B, T, H_A, 2 * DH_A)


def _attend_sample(q, k, v, lam, k_pool, v_pool, page_table):
    DB, Tn = q.shape[0], q.shape[1]
    k_past = k_pool[page_table].reshape(DB, -1, H_A, 2, DH_A)
    v_past = v_pool[page_table].reshape(DB, -1, H_A, 2 * DH_A)
    P = k_past.shape[1]
    k_all = jnp.concatenate([k_past.astype(k.dtype), k], 1)
    v_all = jnp.concatenate([v_past.astype(v.dtype), v], 1)
    return _diff_attn(q, k_all, v_all, P + jnp.arange(Tn), jnp.arange(P + Tn), lam)


def _even_mixer(x, pos, s0, w_in, w_out, lam_p, lam_init, subln_g, lb, b_norm_g, attend):
    B, T, _ = x.shape
    a_q, a_k, a_v, b_q, b_f, b_i, b_g = _split(x @ w_in, [MIX_W] * 7)
    q = _rope(a_q.reshape(B, T, H_A, 2, DH_A), pos)
    k = _rope(a_k.reshape(B, T, H_A, 2, DH_A), pos)
    v = a_v.reshape(B, T, H_A, 2 * DH_A)
    lp = lam_p.astype(F32)
    lam = jnp.exp(jnp.sum(lp[0] * lp[1])) - jnp.exp(jnp.sum(lp[2] * lp[3])) + lam_init
    o_a = _rmsnorm(attend(q, k, v, lam), subln_g) * (1.0 - lam_init)
    o_a = o_a.reshape(B, T, MIX_W).astype(x.dtype)
    f = lb + (1.0 - lb) * jax.nn.sigmoid(b_f.astype(F32))
    o_b, s_T = _gated_linear_attention(_heads(jax.nn.silu(b_q.astype(F32)), H_B), _heads(1.0 - f, H_B),
                                       _heads(b_i, H_B), _heads(jnp.log(f), H_B), s0)
    o_b = (_merge(_rmsnorm(o_b, b_norm_g)) * jax.nn.silu(b_g.astype(F32))).astype(x.dtype)
    y = jnp.concatenate([o_a, o_b], -1) @ w_out
    return y, k, v, s_T


def _odd_mixer(x, pos, sc0, dc0, dn0, dm0, w_in, b_if, w_out, c_norm_g, d_norm_g):
    B, T, _ = x.shape
    c_q, c_k, c_v, c_g, d_q, d_k, d_v, d_o, d_if = _split(x @ w_in, [MIX_W] * 8 + [2 * H_D])
    q = _heads(_rope(c_q.reshape(B, T, H_C, DK_C), pos).reshape(B, T, MIX_W), H_C)
    k = _heads(_rope(c_k.reshape(B, T, H_C, DK_C), pos).reshape(B, T, MIX_W), H_C) * (DK_C ** -0.5)
    log_gamma = jnp.log(1.0 - 2.0 ** (-5.0 - jnp.arange(H_C, dtype=F32)))
    lf = jnp.broadcast_to(log_gamma[None, :, None, None], (B, H_C, T, DK_C))
    o_c, sc = _gated_linear_attention(q, k, _heads(c_v, H_C), lf, sc0)
    o_c = (_merge(_groupnorm(o_c, c_norm_g)) * jax.nn.silu(c_g.astype(F32))).astype(x.dtype)
    gates = d_if.astype(F32) + b_if.astype(F32)
    log_i = gates[..., :H_D].transpose(0, 2, 1)
    log_f = jax.nn.log_sigmoid(gates[..., H_D:]).transpose(0, 2, 1)
    h, dc, dn, dm = _mlstm(_heads(d_q, H_D) * (DK_D ** -0.5), _heads(d_k, H_D), _heads(d_v, H_D),
                           log_i, log_f, dc0, dn0, dm0)
    o_d = (_merge(_groupnorm(h, d_norm_g)) * jax.nn.sigmoid(d_o.astype(F32))).astype(x.dtype)
    y = jnp.concatenate([o_c, o_d], -1) @ w_out
    return y, sc, dc, dn, dm


def _mem_kv(mem, w_kv):
    B, M, _ = mem.shape
    mk, mv = _split(mem @ w_kv, [D_MODEL, D_MODEL])
    return mk.reshape(B, M, H_X, DH_X), mv.reshape(B, M, H_X, DH_X)


def _cross_attn(x, mk, mv, wq, wo):
    B, T, _ = x.shape
    q = (x @ wq).reshape(B, T, H_X, DH_X)
    s = jnp.einsum('bqhd,bkhd->bhqk', q, mk.astype(q.dtype)).astype(F32) * (DH_X ** -0.5)
    p = jax.nn.softmax(s, axis=-1)
    o = jnp.einsum('bhqk,bkhd->bqhd', p.astype(x.dtype), mv.astype(x.dtype)).reshape(B, T, D_MODEL)
    return o @ wo


def _conv_ffn(x, buf, w_up, conv_w, conv_b, w_down):
    T = x.shape[1]
    up, gate = _split(x @ w_up, [D_FF, D_FF])
    cat = jnp.concatenate([buf.astype(up.dtype), up], 1)
    conv = conv_b
    for j in range(CONV_W):
        conv = conv + conv_w[j] * cat[:, j:j + T]
    y = (jax.nn.gelu(conv) * gate) @ w_down
    return y, cat[:, cat.shape[1] - (CONV_W - 1):]


def setup_inputs(seed: int = 0) -> dict:
    key = jax.random.key(seed)
    ks = iter(jax.random.split(key, 48))

    def nrm(shape, scale):
        return jax.random.normal(next(ks), shape, F32) * scale

    n_pages = PAST_LEN // PAGE_SIZE
    n_pool = (DEC_BATCH * n_pages * 5 + 3) // 4
    page_table = jax.random.permutation(next(ks), n_pool)[: DEC_BATCH * n_pages].reshape(DEC_BATCH, n_pages).astype(jnp.int32)
    ones, beta = jnp.ones((MIX_W,), F32), jnp.full((MIX_W,), DN_BETA, F32)
    ev_col = jnp.concatenate([ones, ones, beta, ones, ones, beta, ones])
    od_col = jnp.concatenate([ones, ones, beta, ones, ones, ones, beta, ones, jnp.ones((2 * H_D,), F32)])
    kv_col = jnp.concatenate([jnp.ones((D_MODEL,), F32), jnp.full((D_MODEL,), DN_BETA, F32)])
    f_bias = jnp.linspace(3.0, 6.0, H_D, dtype=F32)[None, :] + nrm((N_ODD, H_D), 0.1)
    return {
        "x_prompt": nrm((BATCH, SEQ, D_MODEL), 1.0),
        "x_sample": nrm((DEC_BATCH, DEC_SEQ, D_MODEL), 1.0),
        "cache_a_k": nrm((N_EVEN, n_pool, PAGE_SIZE, H_A, 2, DH_A), 1.0),
        "cache_a_v": nrm((N_EVEN, n_pool, PAGE_SIZE, H_A, 2 * DH_A), 0.5),
        "state_b": nrm((N_EVEN, DEC_BATCH, H_B, DK_B, DV_B), 0.5),
        "state_c": nrm((N_ODD, DEC_BATCH, H_C, DK_C, DV_C), 0.5),
        "state_d_c": nrm((N_ODD, DEC_BATCH, H_D, DK_D, DV_D), 0.1),
        "state_d_n": nrm((N_ODD, DEC_BATCH, H_D, DK_D), 0.1),
        "state_d_m": nrm((N_ODD, DEC_BATCH, H_D), 1.0),
        "cache_mem_k": nrm((DEPTH, DEC_BATCH, N_MEM, H_X, DH_X), 1.0),
        "cache_mem_v": nrm((DEPTH, DEC_BATCH, N_MEM, H_X, DH_X), 0.5),
        "state_conv": nrm((DEPTH, DEC_BATCH, CONV_W - 1, D_FF), 1.0),
        "page_table": page_table,
        "mem_prompt": nrm((BATCH, N_MEM, D_MODEL), 1.0),
        "ev_w_in": nrm((N_EVEN, D_MODEL, EV_IN), D_MODEL ** -0.5) * ev_col,
        "ev_w_out": nrm((N_EVEN, 2 * MIX_W, D_MODEL), (2 * MIX_W) ** -0.5 * DN_BETA),
        "ev_lam": nrm((N_EVEN, 4, DH_A), 0.1),
        "ev_subln_g": 1.0 + nrm((N_EVEN, 2 * DH_A), 0.02),
        "ev_lb_logits": nrm((N_EVEN + 1, MIX_W), 0.1),
        "ev_b_norm_g": 1.0 + nrm((N_EVEN, DV_B), 0.02),
        "od_w_in": nrm((N_ODD, D_MODEL, OD_IN), D_MODEL ** -0.5) * od_col,
        "od_b_if": jnp.concatenate([nrm((N_ODD, H_D), 0.1), f_bias], -1),
        "od_w_out": nrm((N_ODD, 2 * MIX_W, D_MODEL), (2 * MIX_W) ** -0.5 * DN_BETA),
        "od_c_norm_g": 1.0 + nrm((N_ODD, DV_C), 0.02),
        "od_d_norm_g": 1.0 + nrm((N_ODD, DV_D), 0.02),
        "ln_g": 1.0 + nrm((DEPTH, 3, D_MODEL), 0.02),
        "ln_b": nrm((DEPTH, 3, D_MODEL), 0.02),
        "xa_wq": nrm((DEPTH, D_MODEL, D_MODEL), D_MODEL ** -0.5),
        "xa_wkv": nrm((DEPTH, D_MODEL, 2 * D_MODEL), D_MODEL ** -0.5) * kv_col,
        "xa_wo": nrm((DEPTH, D_MODEL, D_MODEL), D_MODEL ** -0.5 * DN_BETA),
        "ffn_w_up": nrm((DEPTH, D_MODEL, 2 * D_FF), D_MODEL ** -0.5),
        "ffn_conv_w": nrm((DEPTH, CONV_W, D_FF), CONV_W ** -0.5),
        "ffn_conv_b": nrm((DEPTH, D_FF), 0.02),
        "ffn_w_down": nrm((DEPTH, D_FF, D_MODEL), D_FF ** -0.5 * DN_BETA),
    }


def reference(x_prompt, x_sample, cache_a_k, cache_a_v, state_b, state_c, state_d_c, state_d_n, state_d_m,
              cache_mem_k, cache_mem_v, state_conv, page_table, mem_prompt,
              ev_w_in, ev_w_out, ev_lam, ev_subln_g, ev_lb_logits, ev_b_norm_g,
              od_w_in, od_b_if, od_w_out, od_c_norm_g, od_d_norm_g,
              ln_g, ln_b, xa_wq, xa_wkv, xa_wo, ffn_w_up, ffn_conv_w, ffn_conv_b, ffn_w_down):
    B, T_p = x_prompt.shape[0], x_prompt.shape[1]
    T_s = x_sample.shape[1]
    past = page_table.shape[1] * cache_a_k.shape[2]
    pos_p = jnp.arange(T_p)
    pos_s = past + jnp.arange(T_s)
    lb_table = jnp.cumsum(jax.nn.softmax(ev_lb_logits.astype(F32), axis=0), axis=0)

    xp, xs = x_prompt, x_sample
    ak_p, av_p, ak_s, av_s, sb_p, sb_s = [], [], [], [], [], []
    sc_p, sc_s, dc_p, dc_s, dn_p, dn_s, dm_p, dm_s = [], [], [], [], [], [], [], []
    mk_p, mv_p, cv_p, cv_s = [], [], [], []
    for l in range(DEPTH):
        j = l // 2
        if l % 2 == 0:
            lam_init = 0.8 - 0.6 * math.exp(-0.3 * l)
            yp, kp, vp, sbp = _even_mixer(xp, pos_p, jnp.zeros((B, H_B, DK_B, DV_B), F32), ev_w_in[j], ev_w_out[j],
                                          ev_lam[j], lam_init, ev_subln_g[j], lb_table[j], ev_b_norm_g[j],
                                          _attend_prompt)
            attend_s = functools.partial(_attend_sample, k_pool=cache_a_k[j], v_pool=cache_a_v[j], page_table=page_table)
            ys, ks_, vs_, sbs = _even_mixer(xs, pos_s, state_b[j], ev_w_in[j], ev_w_out[j],
                                            ev_lam[j], lam_init, ev_subln_g[j], lb_table[j], ev_b_norm_g[j],
                                            attend_s)
            ak_p.append(kp); av_p.append(vp); ak_s.append(ks_); av_s.append(vs_)
            sb_p.append(sbp); sb_s.append(sbs)
        else:
            yp, scp, dcp, dnp, dmp = _odd_mixer(xp, pos_p, jnp.zeros((B, H_C, DK_C, DV_C), F32),
                                                jnp.zeros((B, H_D, DK_D, DV_D), F32), jnp.zeros((B, H_D, DK_D), F32),
                                                jnp.zeros((B, H_D), F32), od_w_in[j], od_b_if[j], od_w_out[j],
                                                od_c_norm_g[j], od_d_norm_g[j])
            ys, scs, dcs, dns, dms = _odd_mixer(xs, pos_s, state_c[j], state_d_c[j], state_d_n[j], state_d_m[j],
                                                od_w_in[j], od_b_if[j], od_w_out[j], od_c_norm_g[j], od_d_norm_g[j])
            sc_p.append(scp); sc_s.append(scs); dc_p.append(dcp); dc_s.append(dcs)
            dn_p.append(dnp); dn_s.append(dns); dm_p.append(dmp); dm_s.append(dms)
        xp = _layernorm(DN_ALPHA * xp + yp, ln_g[l, 0], ln_b[l, 0])
        xs = _layernorm(DN_ALPHA * xs + ys, ln_g[l, 0], ln_b[l, 0])
        mkp, mvp = _mem_kv(mem_prompt, xa_wkv[l])
        mk_p.append(mkp); mv_p.append(mvp)
        xp = _layernorm(DN_ALPHA * xp + _cross_attn(xp, mkp, mvp, xa_wq[l], xa_wo[l]), ln_g[l, 1], ln_b[l, 1])
        xs = _layernorm(DN_ALPHA * xs + _cross_attn(xs, cache_mem_k[l], cache_mem_v[l], xa_wq[l], xa_wo[l]),
                        ln_g[l, 1], ln_b[l, 1])
        fp, cbp = _conv_ffn(xp, jnp.zeros((B, CONV_W - 1, D_FF), xp.dtype), ffn_w_up[l], ffn_conv_w[l], ffn_conv_b[l], ffn_w_down[l])
        fs, cbs = _conv_ffn(xs, state_conv[l], ffn_w_up[l], ffn_conv_w[l], ffn_conv_b[l], ffn_w_down[l])
        cv_p.append(cbp); cv_s.append(cbs)
        xp = _layernorm(DN_ALPHA * xp + fp, ln_g[l, 2], ln_b[l, 2])
        xs = _layernorm(DN_ALPHA * xs + fs, ln_g[l, 2], ln_b[l, 2])

    return (xp, xs,
            jnp.stack(ak_p), jnp.stack(av_p), jnp.stack(ak_s), jnp.stack(av_s),
            jnp.stack(sb_p), jnp.stack(sb_s),
            jnp.stack(sc_p), jnp.stack(sc_s),
            jnp.stack(dc_p), jnp.stack(dc_s), jnp.stack(dn_p), jnp.stack(dn_s), jnp.stack(dm_p), jnp.stack(dm_s),
            jnp.stack(mk_p), jnp.stack(mv_p),
            jnp.stack(cv_p), jnp.stack(cv_s))
```

```python
import functools
import math

import numpy as np
import jax
import jax.numpy as jnp
from jax import lax
from jax.experimental import pallas as pl
from jax.experimental.pallas import tpu as pltpu

F32 = jnp.float32
BF16 = jnp.bfloat16

D_MODEL = 1024
MIX_W = D_MODEL // 2
N_HEADS = 4
HEAD_W = MIX_W // N_HEADS
DH_A = HEAD_W // 2
DH_X = D_MODEL // N_HEADS
CHUNK = 64
ROPE_THETA = 10000.0
NORM_EPS = 1e-5
DEPTH = 2
DN_ALPHA = (2.0 * DEPTH) ** 0.25
LANES = 128
SUBLANES = 8
VMEM_LIMIT = 56 * 1024 * 1024
NEG_INF = float("-inf")

NT_DIMS = (((1,), (1,)), ((), ()))
TN_DIMS = (((0,), (0,)), ((), ()))


def _cparams(n_axes, vmem=VMEM_LIMIT):
    return pltpu.CompilerParams(dimension_semantics=("arbitrary",) * n_axes, vmem_limit_bytes=vmem)


def _bdot(a, b):
    return jnp.dot(a.astype(BF16), b.astype(BF16), preferred_element_type=F32)


def _bdot_nt(a, b):
    return lax.dot_general(a.astype(BF16), b.astype(BF16), NT_DIMS, preferred_element_type=F32)


def _bdot_tn(a, b):
    return lax.dot_general(a.astype(BF16), b.astype(BF16), TN_DIMS, preferred_element_type=F32)


def _split3(x):
    p1 = x.astype(BF16)
    r1 = x - p1.astype(F32)
    p2 = r1.astype(BF16)
    p3 = (r1 - p2.astype(F32)).astype(BF16)
    return p1, p2, p3


def _cumsum_rows(tri, x):
    p1, p2, p3 = _split3(x)
    d = functools.partial(jnp.dot, preferred_element_type=F32)
    return d(tri, p1) + d(tri, p2) + d(tri, p3)


def _cumsum_lanes(x, triu):
    p1, p2, p3 = _split3(x)
    d = functools.partial(jnp.dot, preferred_element_type=F32)
    return d(p1, triu) + d(p2, triu) + d(p3, triu)


def _tri(L, lower):
    r = lax.broadcasted_iota(jnp.int32, (L, L), 0)
    c = lax.broadcasted_iota(jnp.int32, (L, L), 1)
    return (r >= c) if lower else (r <= c)


def _sigmoid(x):
    return 1.0 / (1.0 + jnp.exp(-x))


def _silu(x):
    return x * _sigmoid(x)


def _log_sigmoid(x):
    return jnp.minimum(x, 0.0) - jnp.log(1.0 + jnp.exp(-jnp.abs(x)))


def _layernorm(z, g, b):
    mu = jnp.mean(z, -1, keepdims=True)
    zc = z - mu
    var = jnp.mean(zc * zc, -1, keepdims=True)
    return zc * lax.rsqrt(var + NORM_EPS) * g + b


def _rmsnorm(x, g):
    return x * lax.rsqrt(jnp.mean(x * x, -1, keepdims=True) + NORM_EPS) * g


def _groupnorm(x, g):
    mu = jnp.mean(x, -1, keepdims=True)
    xc = x - mu
    var = jnp.mean(xc * xc, -1, keepdims=True)
    return xc * lax.rsqrt(var + NORM_EPS) * g


def _rope(x, cos, sin, half):
    outs = []
    for c in range(x.shape[1] // LANES):
        xc = x[:, c * LANES:(c + 1) * LANES]
        if 2 * half == LANES:
            sw = pltpu.roll(xc, half, 1)
        else:
            lane = lax.broadcasted_iota(jnp.int32, xc.shape, 1)
            first = (lane & (2 * half - 1)) < half
            sw = jnp.where(first, pltpu.roll(xc, LANES - half, 1), pltpu.roll(xc, half, 1))
        outs.append(xc * cos + sw * sin)
    return outs[0] if len(outs) == 1 else jnp.concatenate(outs, axis=1)


def _rope_tables(pos, d):
    inv = ROPE_THETA ** (-jnp.arange(0, d // 2, dtype=F32) * 2.0 / d)
    ang = pos.astype(F32)[:, None] * inv[None, :]
    cos, sin = jnp.cos(ang), jnp.sin(ang)
    reps = LANES // d
    cos_t = jnp.tile(jnp.concatenate([cos, cos], -1), (1, reps))
    sin_t = jnp.tile(jnp.concatenate([-sin, sin], -1), (1, reps))
    return cos_t, sin_t


def _col(row):
    return jnp.broadcast_to(row, (LANES, LANES)).T


def _mm_kernel(x_ref, w_ref, o_ref, xb_ref):
    @pl.when(pl.program_id(1) == 0)
    def _():
        xb_ref[...] = x_ref[...].astype(BF16)

    o_ref[...] = jnp.dot(xb_ref[...], w_ref[...], preferred_element_type=F32).astype(o_ref.dtype)


def _matmul(x, w, tm, tn, out_dtype=F32):
    M, K = x.shape
    N = w.shape[1]
    tm, tn = min(tm, M), min(tn, N)
    return pl.pallas_call(
        _mm_kernel,
        out_shape=jax.ShapeDtypeStruct((M, N), out_dtype),
        grid=(M // tm, N // tn),
        in_specs=[pl.BlockSpec((tm, K), lambda i, j: (i, 0)),
                  pl.BlockSpec((K, tn), lambda i, j: (0, j))],
        out_specs=pl.BlockSpec((tm, tn), lambda i, j: (i, j)),
        scratch_shapes=[pltpu.VMEM((tm, K), BF16)],
        compiler_params=_cparams(2),
    )(x, w)


def _proj_ln_kernel(*refs, n_in):
    a_refs, w_refs = refs[:n_in], refs[n_in:2 * n_in]
    x_ref, g_ref, b_ref, o_ref = refs[2 * n_in:]
    y = _bdot(a_refs[0][...], w_refs[0][...])
    for a_ref, w_ref in zip(a_refs[1:], w_refs[1:]):
        y = y + _bdot(a_ref[...], w_ref[...])
    o_ref[...] = _layernorm(DN_ALPHA * x_ref[...] + y, g_ref[...], b_ref[...])


def _proj_ln(acts, weights, x, g, b, tm):
    M = x.shape[0]
    tm = min(tm, M)
    n_in = len(acts)
    in_specs = [pl.BlockSpec((tm, a.shape[1]), lambda i: (i, 0)) for a in acts]
    in_specs += [pl.BlockSpec(w.shape, lambda i: (0, 0)) for w in weights]
    in_specs += [pl.BlockSpec((tm, D_MODEL), lambda i: (i, 0)),
                 pl.BlockSpec((1, D_MODEL), lambda i: (0, 0)),
                 pl.BlockSpec((1, D_MODEL), lambda i: (0, 0))]
    return pl.pallas_call(
        functools.partial(_proj_ln_kernel, n_in=n_in),
        out_shape=jax.ShapeDtypeStruct((M, D_MODEL), F32),
        grid=(M // tm,),
        in_specs=in_specs,
        out_specs=pl.BlockSpec((tm, D_MODEL), lambda i: (i, 0)),
        compiler_params=_cparams(1),
    )(*acts, *weights, x, g.reshape(1, -1), b.reshape(1, -1))


def _ev_prep_kernel(qk_ref, v_ref, cos_ref, sin_ref, qb_ref, k32_ref, kb_ref, vb_ref):
    cos, sin = cos_ref[...], sin_ref[...]
    q = _rope(qk_ref[:, :MIX_W], cos, sin, DH_A // 2)
    k = _rope(qk_ref[:, MIX_W:], cos, sin, DH_A // 2)
    qb_ref[...] = (q * (DH_A ** -0.5)).astype(BF16)
    k32_ref[...] = k
    kb_ref[...] = k.astype(BF16)
    vb_ref[...] = v_ref[...].astype(BF16)


def _ev_prep(h, cos_t, sin_t, tm):
    M = h.shape[0]
    tm = min(tm, M)
    n_tab = cos_t.shape[0] // tm
    tab = pl.BlockSpec((tm, LANES), lambda i: (i % n_tab, 0))
    blk = pl.BlockSpec((tm, MIX_W), lambda i: (i, 0))
    return pl.pallas_call(
        _ev_prep_kernel,
        out_shape=(jax.ShapeDtypeStruct((M, MIX_W), BF16), jax.ShapeDtypeStruct((M, MIX_W), F32),
                   jax.ShapeDtypeStruct((M, MIX_W), BF16), jax.ShapeDtypeStruct((M, MIX_W), BF16)),
        grid=(M // tm,),
        in_specs=[pl.BlockSpec((tm, 2 * MIX_W), lambda i: (i, 0)),
                  pl.BlockSpec((tm, MIX_W), lambda i: (i, 2)), tab, tab],
        out_specs=(blk, blk, blk, blk),
        compiler_params=_cparams(1),
    )(h, h, cos_t, sin_t)


def _lambda(lam_ref, lam_init):
    lp = lam_ref[...]
    s01 = jnp.sum(lp[0:1] * lp[1:2], axis=1, keepdims=True)
    s23 = jnp.sum(lp[2:3] * lp[3:4], axis=1, keepdims=True)
    return jnp.exp(s01) - jnp.exp(s23) + lam_init


def _dattn_kernel(lam_ref, g_ref, q_ref, k_ref, v_ref, o_ref, m_ref, l_ref, acc_ref, *, tq, lam_init):
    i = pl.program_id(2)
    q = q_ref[0]
    lane = lax.broadcasted_iota(jnp.int32, q.shape, 1)
    zero = jnp.zeros_like(q)
    qs = jnp.concatenate([jnp.where(lane < DH_A, q, zero), jnp.where(lane >= DH_A, q, zero)], axis=0)
    m_ref[...] = jnp.full(m_ref.shape, NEG_INF, F32)
    l_ref[...] = jnp.zeros(l_ref.shape, F32)
    acc_ref[...] = jnp.zeros(acc_ref.shape, F32)

    def step(j, masked):
        rows = pl.ds(pl.multiple_of(j * tq, tq), tq)
        kj = k_ref[0, rows, :]
        vj = v_ref[0, rows, :]
        s = lax.dot_general(qs, kj, NT_DIMS, preferred_element_type=F32)
        if masked:
            r = lax.broadcasted_iota(jnp.int32, s.shape, 0)
            c = lax.broadcasted_iota(jnp.int32, s.shape, 1)
            r = jnp.where(r >= tq, r - tq, r)
            s = jnp.where(c <= r, s, NEG_INF)
        m_prev = m_ref[...]
        m_new = jnp.maximum(m_prev, jnp.max(s, axis=1, keepdims=True))
        alpha = jnp.exp(m_prev - m_new)
        p = jnp.exp(s - m_new)
        l_ref[...] = alpha * l_ref[...] + jnp.sum(p, axis=1, keepdims=True)
        acc_ref[...] = alpha * acc_ref[...] + jnp.dot(p.astype(BF16), vj, preferred_element_type=F32)
        m_ref[...] = m_new

    def body(j, carry):
        step(j, False)
        return carry

    lax.fori_loop(0, i, body, 0)
    step(i, True)
    o = acc_ref[...] / l_ref[...]
    lam = _lambda(lam_ref, lam_init)
    d = o[:tq] - lam * o[tq:]
    o_ref[0] = _rmsnorm(d, g_ref[...]) * (1.0 - lam_init)


def _dattn_prompt(qb, kb, vb, lam_p, subln_g, lam_init, tq):
    B, T, _ = qb.shape
    tq = min(tq, T)
    return pl.pallas_call(
        functools.partial(_dattn_kernel, tq=tq, lam_init=lam_init),
        out_shape=jax.ShapeDtypeStruct((B, T, MIX_W), F32),
        grid=(B, N_HEADS, T // tq),
        in_specs=[pl.BlockSpec((4, DH_A), lambda b, h, i: (0, 0)),
                  pl.BlockSpec((1, HEAD_W), lambda b, h, i: (0, 0)),
                  pl.BlockSpec((1, tq, HEAD_W), lambda b, h, i: (b, i, h)),
                  pl.BlockSpec((1, T, HEAD_W), lambda b, h, i: (b, 0, h)),
                  pl.BlockSpec((1, T, HEAD_W), lambda b, h, i: (b, 0, h))],
        out_specs=pl.BlockSpec((1, tq, HEAD_W), lambda b, h, i: (b, i, h)),
        scratch_shapes=[pltpu.VMEM((2 * tq, 1), F32), pltpu.VMEM((2 * tq, 1), F32),
                        pltpu.VMEM((2 * tq, HEAD_W), F32)],
        compiler_params=_cparams(3),
    )(lam_p, subln_g.reshape(1, -1), qb, kb, vb)


def _gla_chunk(q_in, k_in, k_end, v, dec_row, st_ref, h, tril_mask):
    st = st_ref[h]
    attn = jnp.where(tril_mask, _bdot_nt(q_in, k_in), 0.0)
    o = _bdot(attn, v) + _bdot_nt(q_in, st)
    st_ref[h] = st * dec_row + _bdot_tn(v, k_end)
    return o


def _hgrn_kernel(q_ref, f_ref, i_ref, g_ref, lb_ref, ng_ref, o_ref, s_ref, st_ref, *, tt, L):
    t = pl.program_id(1)

    @pl.when(t == 0)
    def _():
        st_ref[...] = jnp.zeros(st_ref.shape, F32)

    tril_mask = _tri(L, True)
    tril = tril_mask.astype(BF16)

    def body(c, carry):
        rows = pl.ds(pl.multiple_of(c * L, L), L)
        for h in range(N_HEADS):
            cols = slice(h * HEAD_W, (h + 1) * HEAD_W)
            lb = lb_ref[:, cols]
            f = lb + (1.0 - lb) * _sigmoid(f_ref[0, rows, cols])
            b = _cumsum_rows(tril, jnp.log(f))
            b_end = b[L - 1:L, :]
            k = 1.0 - f
            q_in = _silu(q_ref[0, rows, cols]) * jnp.exp(b)
            o = _gla_chunk(q_in, k * jnp.exp(-b), k * jnp.exp(b_end - b), i_ref[0, rows, cols],
                           jnp.exp(b_end), st_ref, h, tril_mask)
            o_ref[0, rows, cols] = _rmsnorm(o, ng_ref[...]) * _silu(g_ref[0, rows, cols])
        return carry

    lax.fori_loop(0, tt // L, body, 0)

    @pl.when(t == pl.num_programs(1) - 1)
    def _():
        s_ref[0] = st_ref[...]


def _hgrn_prompt(h3, lb, norm_g, tt):
    B, T, _ = h3.shape
    tt = min(tt, T)
    L = math.gcd(T, CHUNK)
    col = lambda c: pl.BlockSpec((1, tt, MIX_W), lambda b, t: (b, t, c))
    return pl.pallas_call(
        functools.partial(_hgrn_kernel, tt=tt, L=L),
        out_shape=(jax.ShapeDtypeStruct((B, T, MIX_W), F32),
                   jax.ShapeDtypeStruct((B, N_HEADS, HEAD_W, HEAD_W), F32)),
        grid=(B, T // tt),
        in_specs=[col(3), col(4), col(5), col(6),
                  pl.BlockSpec((1, MIX_W), lambda b, t: (0, 0)),
                  pl.BlockSpec((1, HEAD_W), lambda b, t: (0, 0))],
        out_specs=(pl.BlockSpec((1, tt, MIX_W), lambda b, t: (b, t, 0)),
                   pl.BlockSpec((1, N_HEADS, HEAD_W, HEAD_W), lambda b, t: (b, 0, 0, 0))),
        scratch_shapes=[pltpu.VMEM((N_HEADS, HEAD_W, HEAD_W), F32)],
        compiler_params=_cparams(2),
    )(h3, h3, h3, h3, lb.reshape(1, -1), norm_g.reshape(1, -1))


def _log_gamma(h):
    return float(np.log(1.0 - 2.0 ** (-5.0 - h)))


def _od_kernel(cq_ref, ck_ref, cv_ref, cg_ref, dq_ref, dk_ref, dv_ref, do_ref, gc_ref, gr_ref,
               bc_ref, br_ref, cos_ref, sin_ref, cng_ref, dng_ref,
               oc_ref, od_ref, sc_ref, dc_ref, dn_ref, dm_ref,
               st_ref, ct_ref, n_ref, m_ref, *, tt, L):
    t = pl.program_id(1)

    @pl.when(t == 0)
    def _():
        st_ref[...] = jnp.zeros(st_ref.shape, F32)
        ct_ref[...] = jnp.zeros(ct_ref.shape, F32)
        n_ref[...] = jnp.zeros(n_ref.shape, F32)
        m_ref[...] = jnp.zeros(m_ref.shape, F32)

    tril_mask = _tri(L, True)
    tril = tril_mask.astype(BF16)
    triu = _tri(L, False).astype(BF16)
    pos1 = (lax.broadcasted_iota(jnp.int32, (L, HEAD_W), 0) + 1).astype(F32)
    scale = HEAD_W ** -0.5

    def body(c, carry):
        rows = pl.ds(pl.multiple_of(c * L, L), L)
        cos, sin = cos_ref[rows, :], sin_ref[rows, :]
        for h in range(N_HEADS):
            cols = slice(h * HEAD_W, (h + 1) * HEAD_W)
            lg = _log_gamma(h)
            b = pos1 * lg
            q = _rope(cq_ref[0, rows, cols], cos, sin, HEAD_W // 2)
            k = _rope(ck_ref[0, rows, cols], cos, sin, HEAD_W // 2) * scale
            b_end = L * lg
            o = _gla_chunk(q * jnp.exp(b), k * jnp.exp(-b), k * jnp.exp(b_end - b), cv_ref[0, rows, cols],
                           math.exp(b_end), st_ref, h, tril_mask)
            oc_ref[0, rows, cols] = _groupnorm(o, cng_ref[...]) * _silu(cg_ref[0, rows, cols])
            li_c = gc_ref[0, rows, h:h + 1] + bc_ref[:, h:h + 1]
            lf_c = _log_sigmoid(gc_ref[0, rows, N_HEADS + h:N_HEADS + h + 1]
                                + bc_ref[:, N_HEADS + h:N_HEADS + h + 1])
            li_r = gr_ref[0, c, h:h + 1, :] + br_ref[h:h + 1, :]
            lf_r = _log_sigmoid(gr_ref[0, c, N_HEADS + h:N_HEADS + h + 1, :]
                                + br_ref[N_HEADS + h:N_HEADS + h + 1, :])
            bc = _cumsum_rows(tril, jnp.broadcast_to(lf_c, (L, L)))
            br = _cumsum_lanes(jnp.broadcast_to(lf_r, (SUBLANES, L)), triu)[0:1, :]
            bc1 = bc[:, 0:1]
            m_prev = m_ref[h:h + 1, 0:1]
            dm = jnp.where(tril_mask, bc - br + li_r, NEG_INF)
            m_t = jnp.maximum(bc1 + m_prev, jnp.max(dm, axis=1, keepdims=True))
            inter = jnp.exp(bc1 + m_prev - m_t)
            dq = dq_ref[0, rows, cols] * scale
            dk = dk_ref[0, rows, cols]
            dv = dv_ref[0, rows, cols]
            w_mat = jnp.exp(dm - m_t) * _bdot_nt(dq, dk)
            ct = ct_ref[h]
            n_row = n_ref[h:h + 1, :]
            num = inter * _bdot_nt(dq, ct) + _bdot(w_mat, dv)
            den = inter * jnp.sum(dq * n_row, axis=1, keepdims=True) + jnp.sum(w_mat, axis=1, keepdims=True)
            hh = num / jnp.maximum(jnp.abs(den), jnp.exp(-m_t))
            od_ref[0, rows, cols] = _groupnorm(hh, dng_ref[...]) * _sigmoid(do_ref[0, rows, cols])
            m_new = m_t[L - 1:L, :]
            b_last = bc1[L - 1:L, :]
            c_scale = jnp.exp(b_last + m_prev - m_new)
            kw = dk * jnp.exp(b_last - bc1 + li_c - m_new)
            ct_ref[h] = c_scale * ct + _bdot_tn(dv, kw)
            n_ref[h:h + 1, :] = c_scale * n_row + jnp.sum(kw, axis=0, keepdims=True)
            m_ref[h:h + 1, :] = jnp.broadcast_to(m_new, (1, LANES))
        return carry

    lax.fori_loop(0, tt // L, body, 0)

    @pl.when(t == pl.num_programs(1) - 1)
    def _():
        sc_ref[0] = st_ref[...]
        dc_ref[0] = ct_ref[...]
        dn_ref[0] = n_ref[...]
        dm_ref[0] = m_ref[...]


def _od_prompt(h3, gates, b_if, cos_t, sin_t, c_norm_g, d_norm_g, tt):
    B, T, _ = h3.shape
    tt = min(tt, T)
    L = math.gcd(T, CHUNK)
    gates_r = gates.reshape(B, T // L, L, 2 * N_HEADS).transpose(0, 1, 3, 2)
    col = lambda c: pl.BlockSpec((1, tt, MIX_W), lambda b, t: (b, t, c))
    tab = pl.BlockSpec((tt, LANES), lambda b, t: (t, 0))
    vec = pl.BlockSpec((1, HEAD_W), lambda b, t: (0, 0))
    mat_state = pl.BlockSpec((1, N_HEADS, HEAD_W, HEAD_W), lambda b, t: (b, 0, 0, 0))
    row_state = pl.BlockSpec((1, SUBLANES, LANES), lambda b, t: (b, 0, 0))
    out_blk = pl.BlockSpec((1, tt, MIX_W), lambda b, t: (b, t, 0))
    return pl.pallas_call(
        functools.partial(_od_kernel, tt=tt, L=L),
        out_shape=(jax.ShapeDtypeStruct((B, T, MIX_W), F32), jax.ShapeDtypeStruct((B, T, MIX_W), F32),
                   jax.ShapeDtypeStruct((B, N_HEADS, HEAD_W, HEAD_W), F32),
                   jax.ShapeDtypeStruct((B, N_HEADS, HEAD_W, HEAD_W), F32),
                   jax.ShapeDtypeStruct((B, SUBLANES, LANES), F32),
                   jax.ShapeDtypeStruct((B, SUBLANES, LANES), F32)),
        grid=(B, T // tt),
        in_specs=[col(0), col(1), col(2), col(3), col(4), col(5), col(6), col(7),
                  pl.BlockSpec((1, tt, 2 * N_HEADS), lambda b, t: (b, t, 0)),
                  pl.BlockSpec((1, tt // L, 2 * N_HEADS, L), lambda b, t: (b, t, 0, 0)),
                  pl.BlockSpec((1, 2 * N_HEADS), lambda b, t: (0, 0)),
                  pl.BlockSpec((2 * N_HEADS, 1), lambda b, t: (0, 0)),
                  tab, tab, vec, vec],
        out_specs=(out_blk, out_blk, mat_state, mat_state, row_state, row_state),
        scratch_shapes=[pltpu.VMEM((N_HEADS, HEAD_W, HEAD_W), F32), pltpu.VMEM((N_HEADS, HEAD_W, HEAD_W), F32),
                        pltpu.VMEM((SUBLANES, LANES), F32), pltpu.VMEM((SUBLANES, LANES), F32)],
        compiler_params=_cparams(2),
    )(h3, h3, h3, h3, h3, h3, h3, h3, gates, gates_r, b_if.reshape(1, -1), b_if.reshape(-1, 1),
      cos_t, sin_t, c_norm_g.reshape(1, -1), d_norm_g.reshape(1, -1))


def _xattn_kernel(x_ref, wq_ref, mk_ref, mv_ref, wo_ref, g_ref, b_ref, o_ref):
    x = x_ref[...]
    q = _bdot(x, wq_ref[...])
    qb = (q * (DH_X ** -0.5)).astype(BF16)
    outs = []
    for h in range(N_HEADS):
        cols = slice(h * DH_X, (h + 1) * DH_X)
        s = lax.dot_general(qb[:, cols], mk_ref[0, :, cols], NT_DIMS, preferred_element_type=F32)
        p = jnp.exp(s - jnp.max(s, axis=1, keepdims=True))
        l = jnp.sum(p, axis=1, keepdims=True)
        outs.append((jnp.dot(p.astype(BF16), mv_ref[0, :, cols], preferred_element_type=F32) / l).astype(BF16))
    y = jnp.dot(jnp.concatenate(outs, axis=1), wo_ref[...], preferred_element_type=F32)
    o_ref[...] = _layernorm(DN_ALPHA * x + y, g_ref[...], b_ref[...])


def _xattn_prompt(x, wq, mk, mv, wo, g, b, T, tm):
    M = x.shape[0]
    tm = min(tm, T)
    n_mem = mk.shape[1]
    per_b = T // tm
    full = lambda shape: pl.BlockSpec(shape, lambda i: (0,) * len(shape))
    mem = pl.BlockSpec((1, n_mem, D_MODEL), lambda i: (i // per_b, 0, 0))
    return pl.pallas_call(
        _xattn_kernel,
        out_shape=jax.ShapeDtypeStruct((M, D_MODEL), F32),
        grid=(M // tm,),
        in_specs=[pl.BlockSpec((tm, D_MODEL), lambda i: (i, 0)), full((D_MODEL, D_MODEL)), mem, mem,
                  full((D_MODEL, D_MODEL)), full((1, D_MODEL)), full((1, D_MODEL))],
        out_specs=pl.BlockSpec((tm, D_MODEL), lambda i: (i, 0)),
        compiler_params=_cparams(1),
    )(x, wq, mk, mv, wo, g.reshape(1, -1), b.reshape(1, -1))


def _ffn_kernel(x_ref, wu_ref, wg_ref, cw_ref, cb_ref, wd_ref, g_ref, b_ref, o_ref, tail_ref,
                xb_ref, acc_ref, stage_ref, carry_ref, *, tm, per_b):
    i, j = pl.program_id(0), pl.program_id(1)

    @pl.when(j == 0)
    def _():
        xb_ref[...] = x_ref[...].astype(BF16)

    @pl.when(i % per_b == 0)
    def _():
        carry_ref[j] = jnp.zeros(carry_ref.shape[1:], F32)

    xb = xb_ref[...]
    stage_ref[0:SUBLANES, :] = carry_ref[j]
    stage_ref[SUBLANES:, :] = jnp.dot(xb, wu_ref[...], preferred_element_type=F32)
    last = stage_ref[tm:tm + SUBLANES, :]
    carry_ref[j] = last
    tail_ref[0] = last
    conv = (cb_ref[...] + cw_ref[0:1, :] * stage_ref[SUBLANES - 2:SUBLANES - 2 + tm, :]
            + cw_ref[1:2, :] * stage_ref[SUBLANES - 1:SUBLANES - 1 + tm, :]
            + cw_ref[2:3, :] * stage_ref[SUBLANES:, :])
    gate = jnp.dot(xb, wg_ref[...], preferred_element_type=F32)
    y = _bdot(jax.nn.gelu(conv) * gate, wd_ref[...])

    @pl.when(j == 0)
    def _():
        acc_ref[...] = y

    @pl.when(j > 0)
    def _():
        acc_ref[...] = acc_ref[...] + y

    @pl.when(j == pl.num_programs(1) - 1)
    def _():
        o_ref[...] = _layernorm(DN_ALPHA * x_ref[...] + acc_ref[...], g_ref[...], b_ref[...])


def _ffn_prompt(x, w_up, conv_w, conv_b, w_down, g, b, T, tm, tf):
    M = x.shape[0]
    d_ff = w_down.shape[0]
    tm = min(tm, T)
    nf = d_ff // tf
    per_b = T // tm
    return pl.pallas_call(
        functools.partial(_ffn_kernel, tm=tm, per_b=per_b),
        out_shape=(jax.ShapeDtypeStruct((M, D_MODEL), F32),
                   jax.ShapeDtypeStruct((M // tm, SUBLANES, d_ff), F32)),
        grid=(M // tm, nf),
        in_specs=[pl.BlockSpec((tm, D_MODEL), lambda i, j: (i, 0)),
                  pl.BlockSpec((D_MODEL, tf), lambda i, j: (0, j)),
                  pl.BlockSpec((D_MODEL, tf), lambda i, j: (0, nf + j)),
                  pl.BlockSpec((3, tf), lambda i, j: (0, j)),
                  pl.BlockSpec((1, tf), lambda i, j: (0, j)),
                  pl.BlockSpec((tf, D_MODEL), lambda i, j: (j, 0)),
                  pl.BlockSpec((1, D_MODEL), lambda i, j: (0, 0)),
                  pl.BlockSpec((1, D_MODEL), lambda i, j: (0, 0))],
        out_specs=(pl.BlockSpec((tm, D_MODEL), lambda i, j: (i, 0)),
                   pl.BlockSpec((1, SUBLANES, tf), lambda i, j: (i, 0, j))),
        scratch_shapes=[pltpu.VMEM((tm, D_MODEL), BF16), pltpu.VMEM((tm, D_MODEL), F32),
                        pltpu.VMEM((tm + SUBLANES, tf), F32), pltpu.VMEM((nf, SUBLANES, tf), F32)],
        compiler_params=_cparams(2),
    )(x, w_up, w_up, conv_w, conv_b.reshape(1, -1), w_down, g.reshape(1, -1), b.reshape(1, -1))


def _dattn_sample_kernel(*refs, n_pages, page, lam_init):
    pt_ref = refs[0]
    lam_ref, g_ref, q_ref, kn_ref, vn_ref = refs[1:6]
    k_refs = refs[6:6 + n_pages]
    v_refs = refs[6 + n_pages:6 + 2 * n_pages]
    o_ref = refs[6 + 2 * n_pages]
    del pt_ref
    n_rows = 2 * N_HEADS
    q = q_ref[0]
    lane = lax.broadcasted_iota(jnp.int32, (n_rows, MIX_W), 1)
    row = lax.broadcasted_iota(jnp.int32, (n_rows, MIX_W), 0)
    qbd = jnp.where(lax.shift_right_logical(lane, 6) == row, jnp.broadcast_to(q, (n_rows, MIX_W)), 0.0)
    s_new = jnp.sum(qbd * kn_ref[0], axis=1, keepdims=True)
    scores = [_bdot_nt(qbd, k_refs[p][0]) for p in range(n_pages)]
    m = s_new
    for s in scores:
        m = jnp.maximum(m, jnp.max(s, axis=1, keepdims=True))
    p_new = jnp.exp(s_new - m)
    l = p_new
    acc = p_new * vn_ref[0]
    for p in range(n_pages):
        pr = jnp.exp(scores[p] - m)
        l = l + jnp.sum(pr, axis=1, keepdims=True)
        acc = acc + _bdot(pr, v_refs[p][0])
    o = acc / l
    lam = _lambda(lam_ref, lam_init)
    outs = []
    for h in range(N_HEADS):
        cols = slice(h * HEAD_W, (h + 1) * HEAD_W)
        d = o[2 * h:2 * h + 1, cols] - lam * o[2 * h + 1:2 * h + 2, cols]
        outs.append(_rmsnorm(d, g_ref[...]) * (1.0 - lam_init))
    o_ref[0] = jnp.concatenate(outs, axis=1)


def _dattn_sample(q, k_new, v_new, k_pool, v_pool, page_table, lam_p, subln_g, lam_init):
    DB = q.shape[0]
    n_pages = page_table.shape[1]
    page = k_pool.shape[1]
    tok = pl.BlockSpec((1, 1, MIX_W), lambda b, pt: (b, 0, 0))

    def page_spec(p):
        return pl.BlockSpec((1, page, MIX_W), lambda b, pt: (pt[b, p], 0, 0))

    grid_spec = pltpu.PrefetchScalarGridSpec(
        num_scalar_prefetch=1,
        grid=(DB,),
        in_specs=[pl.BlockSpec((4, DH_A), lambda b, pt: (0, 0)),
                  pl.BlockSpec((1, HEAD_W), lambda b, pt: (0, 0)), tok, tok, tok]
                 + [page_spec(p) for p in range(n_pages)] * 2,
        out_specs=tok,
    )
    out = pl.pallas_call(
        functools.partial(_dattn_sample_kernel, n_pages=n_pages, page=page, lam_init=lam_init),
        out_shape=jax.ShapeDtypeStruct((DB, 1, MIX_W), F32),
        grid_spec=grid_spec,
        compiler_params=_cparams(1),
    )(page_table, lam_p, subln_g.reshape(1, -1), q.reshape(DB, 1, MIX_W), k_new.reshape(DB, 1, MIX_W),
      v_new.reshape(DB, 1, MIX_W), *([k_pool] * n_pages), *([v_pool] * n_pages))
    return out.reshape(DB, MIX_W)


def _hgrn_step_kernel(h_ref, s0_ref, lb_ref, ng_ref, o_ref, s_ref, *, bt):
    def body(i, carry):
        row = h_ref[i]
        outs = []
        for h in range(N_HEADS):
            c0 = h * HEAD_W
            cols = lambda blk: slice(blk * MIX_W + c0, blk * MIX_W + c0 + HEAD_W)
            lb = lb_ref[:, c0:c0 + HEAD_W]
            f = lb + (1.0 - lb) * _sigmoid(row[:, cols(4)])
            fc = _col(f)
            s_new = fc * s0_ref[i, h] + (1.0 - fc) * row[:, cols(5)]
            s_ref[i, h] = s_new
            q = jnp.broadcast_to(_silu(row[:, cols(3)]), (SUBLANES, HEAD_W))
            o = _bdot(q, s_new)[0:1, :]
            outs.append(_rmsnorm(o, ng_ref[...]) * _silu(row[:, cols(6)]))
        o_ref[i] = jnp.concatenate(outs, axis=1)
        return carry

    lax.fori_loop(0, bt, body, 0)


def _hgrn_step(h, s0, lb, norm_g, bt):
    DB, W = h.shape
    bt = min(bt, DB)
    state = pl.BlockSpec((bt, N_HEADS, HEAD_W, HEAD_W), lambda i: (i, 0, 0, 0))
    o, s = pl.pallas_call(
        functools.partial(_hgrn_step_kernel, bt=bt),
        out_shape=(jax.ShapeDtypeStruct((DB, 1, MIX_W), F32), jax.ShapeDtypeStruct(s0.shape, F32)),
        grid=(DB // bt,),
        in_specs=[pl.BlockSpec((bt, 1, W), lambda i: (i, 0, 0)), state,
                  pl.BlockSpec((1, MIX_W), lambda i: (0, 0)), pl.BlockSpec((1, HEAD_W), lambda i: (0, 0))],
        out_specs=(pl.BlockSpec((bt, 1, MIX_W), lambda i: (i, 0, 0)), state),
        compiler_params=_cparams(1),
    )(h.reshape(DB, 1, W), s0, lb.reshape(1, -1), norm_g.reshape(1, -1))
    return o.reshape(DB, MIX_W), s


def _od_step_kernel(h_ref, gt_ref, bif_ref, cos_ref, sin_ref, sc0_ref, dc0_ref, dn0_ref, dm0_ref, cng_ref, dng_ref,
                    oc_ref, od_ref, sc_ref, dc_ref, dn_ref, dm_ref, *, bt):
    scale = HEAD_W ** -0.5

    def body(i, carry):
        row = h_ref[i]
        cos, sin = cos_ref[...], sin_ref[...]
        gates = gt_ref[i] + bif_ref[...]
        m_row = dm0_ref[i]
        oc, od, ms = [], [], []
        for h in range(N_HEADS):
            c0 = h * HEAD_W
            cols = lambda blk: slice(blk * MIX_W + c0, blk * MIX_W + c0 + HEAD_W)
            q = _rope(row[:, cols(0)], cos, sin, HEAD_W // 2)
            k = _rope(row[:, cols(1)], cos, sin, HEAD_W // 2) * scale
            s_new = math.exp(_log_gamma(h)) * sc0_ref[i, h] + _col(k) * row[:, cols(2)]
            sc_ref[i, h] = s_new
            o = _bdot(jnp.broadcast_to(q, (SUBLANES, HEAD_W)), s_new)[0:1, :]
            oc.append(_groupnorm(o, cng_ref[...]) * _silu(row[:, cols(3)]))
            li = gates[:, h:h + 1]
            lf = _log_sigmoid(gates[:, N_HEADS + h:N_HEADS + h + 1])
            m_prev = m_row[:, h:h + 1]
            m_new = jnp.maximum(lf + m_prev, li)
            c_scale = jnp.exp(lf + m_prev - m_new)
            kw = row[:, cols(5)] * jnp.exp(li - m_new)
            c_new = c_scale * dc0_ref[i, h] + _col(kw) * row[:, cols(6)]
            n_new = c_scale * dn0_ref[i, h:h + 1, :] + kw
            dc_ref[i, h] = c_new
            dn_ref[i, h:h + 1, :] = n_new
            ms.append(m_new)
            dq = row[:, cols(4)] * scale
            num = _bdot(jnp.broadcast_to(dq, (SUBLANES, HEAD_W)), c_new)[0:1, :]
            den = jnp.sum(dq * n_new, axis=1, keepdims=True)
            hh = num / jnp.maximum(jnp.abs(den), jnp.exp(-m_new))
            od.append(_groupnorm(hh, dng_ref[...]) * _sigmoid(row[:, cols(7)]))
        oc_ref[i] = jnp.concatenate(oc, axis=1)
        od_ref[i] = jnp.concatenate(od, axis=1)
        dm_ref[i] = jnp.concatenate(ms, axis=1)
        return carry

    lax.fori_loop(0, bt, body, 0)


def _od_step(h, gates, b_if, cos_t, sin_t, sc0, dc0, dn0, dm0, c_norm_g, d_norm_g, bt):
    DB, W = h.shape
    bt = min(bt, DB)
    mat = pl.BlockSpec((bt, N_HEADS, HEAD_W, HEAD_W), lambda i: (i, 0, 0, 0))
    nblk = pl.BlockSpec((bt, N_HEADS, HEAD_W), lambda i: (i, 0, 0))
    mblk = pl.BlockSpec((bt, 1, N_HEADS), lambda i: (i, 0, 0))
    vec = pl.BlockSpec((1, LANES), lambda i: (0, 0))
    out = pl.BlockSpec((bt, 1, MIX_W), lambda i: (i, 0, 0))
    b_pad = jnp.zeros((1, LANES), F32).at[0, :2 * N_HEADS].set(b_if)
    oc, od, sc, dc, dn, dm = pl.pallas_call(
        functools.partial(_od_step_kernel, bt=bt),
        out_shape=(jax.ShapeDtypeStruct((DB, 1, MIX_W), F32), jax.ShapeDtypeStruct((DB, 1, MIX_W), F32),
                   jax.ShapeDtypeStruct(sc0.shape, F32), jax.ShapeDtypeStruct(dc0.shape, F32),
                   jax.ShapeDtypeStruct(dn0.shape, F32), jax.ShapeDtypeStruct((DB, 1, N_HEADS), F32)),
        grid=(DB // bt,),
        in_specs=[pl.BlockSpec((bt, 1, W), lambda i: (i, 0, 0)),
                  pl.BlockSpec((bt, 1, LANES), lambda i: (i, 0, 0)), vec, vec, vec,
                  mat, mat, nblk, mblk, vec, vec],
        out_specs=(out, out, mat, mat, nblk, mblk),
        compiler_params=_cparams(1),
    )(h.reshape(DB, 1, W), gates.reshape(DB, 1, LANES), b_pad, cos_t, sin_t, sc0, dc0, dn0,
      dm0.reshape(DB, 1, N_HEADS), c_norm_g.reshape(1, -1), d_norm_g.reshape(1, -1))
    return oc.reshape(DB, MIX_W), od.reshape(DB, MIX_W), sc, dc, dn, dm.reshape(DB, N_HEADS)


def _xattn_sample_kernel(q_ref, mk_ref, mv_ref, o_ref):
    q = q_ref[0] * (DH_X ** -0.5)
    lane = lax.broadcasted_iota(jnp.int32, (SUBLANES, D_MODEL), 1)
    row = lax.broadcasted_iota(jnp.int32, (SUBLANES, D_MODEL), 0)
    qbd = jnp.where(lax.shift_right_logical(lane, 8) == row, jnp.broadcast_to(q, (SUBLANES, D_MODEL)), 0.0)
    s = _bdot_nt(qbd, mk_ref[0])
    p = jnp.exp(s - jnp.max(s, axis=1, keepdims=True))
    l = jnp.sum(p, axis=1, keepdims=True)
    o = _bdot(p, mv_ref[0]) / l
    o_ref[0] = jnp.concatenate([o[h:h + 1, h * DH_X:(h + 1) * DH_X] for h in range(N_HEADS)], axis=1)


def _xattn_sample(q, mk, mv):
    DB, n_mem, _ = mk.shape
    tok = pl.BlockSpec((1, 1, D_MODEL), lambda b: (b, 0, 0))
    mem = pl.BlockSpec((1, n_mem, D_MODEL), lambda b: (b, 0, 0))
    out = pl.pallas_call(
        _xattn_sample_kernel,
        out_shape=jax.ShapeDtypeStruct((DB, 1, D_MODEL), F32),
        grid=(DB,),
        in_specs=[tok, mem, mem],
        out_specs=tok,
        compiler_params=_cparams(1),
    )(q.reshape(DB, 1, D_MODEL), mk, mv)
    return out.reshape(DB, D_MODEL)


def _ffn_sample_kernel(x_ref, up_ref, gate_ref, b0_ref, b1_ref, cw_ref, cb_ref, wd_ref, g_ref, b_ref, o_ref, acc_ref):
    j = pl.program_id(0)
    conv = (cb_ref[...] + cw_ref[0:1, :] * b0_ref[...] + cw_ref[1:2, :] * b1_ref[...]
            + cw_ref[2:3, :] * up_ref[...])
    y = _bdot(jax.nn.gelu(conv) * gate_ref[...], wd_ref[...])

    @pl.when(j == 0)
    def _():
        acc_ref[...] = y

    @pl.when(j > 0)
    def _():
        acc_ref[...] = acc_ref[...] + y

    @pl.when(j == pl.num_programs(0) - 1)
    def _():
        o_ref[...] = _layernorm(DN_ALPHA * x_ref[...] + acc_ref[...], g_ref[...], b_ref[...])


def _ffn_sample(x, ug, buf0, buf1, conv_w, conv_b, w_down, g, b, tf):
    DB = x.shape[0]
    d_ff = w_down.shape[0]
    nf = d_ff // tf
    ff = lambda off: pl.BlockSpec((DB, tf), lambda j: (0, off + j))
    full = lambda shape: pl.BlockSpec(shape, lambda j: (0,) * len(shape))
    return pl.pallas_call(
        _ffn_sample_kernel,
        out_shape=jax.ShapeDtypeStruct((DB, D_MODEL), F32),
        grid=(nf,),
        in_specs=[full((DB, D_MODEL)), ff(0), ff(nf), ff(0), ff(0),
                  pl.BlockSpec((3, tf), lambda j: (0, j)), pl.BlockSpec((1, tf), lambda j: (0, j)),
                  pl.BlockSpec((tf, D_MODEL), lambda j: (j, 0)), full((1, D_MODEL)), full((1, D_MODEL))],
        out_specs=full((DB, D_MODEL)),
        scratch_shapes=[pltpu.VMEM((DB, D_MODEL), F32)],
        compiler_params=_cparams(1),
    )(x, ug, ug, buf0, buf1, conv_w, conv_b.reshape(1, -1), w_down, g.reshape(1, -1), b.reshape(1, -1))


def kernel(x_prompt, x_sample, cache_a_k, cache_a_v, state_b, state_c, state_d_c, state_d_n, state_d_m,
           cache_mem_k, cache_mem_v, state_conv, page_table, mem_prompt,
           ev_w_in, ev_w_out, ev_lam, ev_subln_g, ev_lb_logits, ev_b_norm_g,
           od_w_in, od_b_if, od_w_out, od_c_norm_g, od_d_norm_g,
           ln_g, ln_b, xa_wq, xa_wkv, xa_wo, ffn_w_up, ffn_conv_w, ffn_conv_b, ffn_w_down):
    B, T, _ = x_prompt.shape
    DB, t_s, _ = x_sample.shape
    assert t_s == 1, "the sample group is a single decoding step"
    assert T % CHUNK == 0
    n_pool, page = cache_a_k.shape[1], cache_a_k.shape[2]
    n_pages = page_table.shape[1]
    past = n_pages * page
    n_mem = mem_prompt.shape[1]
    d_ff = ffn_w_down.shape[1]
    tf = d_ff // 2
    N = B * T
    TM = 512

    pos_p = jnp.arange(T)
    pos_s = jnp.full((DB,), past, jnp.int32)
    lb_table = jnp.cumsum(jax.nn.softmax(ev_lb_logits.astype(F32), axis=0), axis=0)

    xp = x_prompt.reshape(N, D_MODEL)
    xs = x_sample.reshape(DB, D_MODEL)
    outs = {k: [] for k in ("ak_p", "av_p", "ak_s", "av_s", "sb_p", "sb_s", "sc_p", "sc_s", "dc_p", "dc_s",
                            "dn_p", "dn_s", "dm_p", "dm_s", "mk_p", "mv_p", "cv_p", "cv_s")}
    swap = lambda s: jnp.swapaxes(s, -1, -2)

    for l in range(DEPTH):
        j = l // 2
        if l % 2 == 0:
            lam_init = 0.8 - 0.6 * math.exp(-0.3 * l)
            w_in = ev_w_in[j].astype(BF16)
            w_out = ev_w_out[j].astype(BF16)
            h = _matmul(xp, w_in, 1024, 512)
            cos_t, sin_t = _rope_tables(pos_p, DH_A)
            qb, k32, kb, vb = _ev_prep(h, cos_t, sin_t, TM)
            h3 = h.reshape(B, T, -1)
            o_a = _dattn_prompt(qb.reshape(B, T, MIX_W), kb.reshape(B, T, MIX_W), vb.reshape(B, T, MIX_W),
                                ev_lam[j], ev_subln_g[j], lam_init, 512)
            o_b, st = _hgrn_prompt(h3, lb_table[j], ev_b_norm_g[j], 512)
            outs["ak_p"].append(k32.reshape(B, T, N_HEADS, 2, DH_A))
            outs["av_p"].append(h3[:, :, 2 * MIX_W:3 * MIX_W].reshape(B, T, N_HEADS, HEAD_W))
            outs["sb_p"].append(swap(st))
            xp = _proj_ln([o_a.reshape(N, MIX_W), o_b.reshape(N, MIX_W)], [w_out[:MIX_W], w_out[MIX_W:]],
                          xp, ln_g[l, 0], ln_b[l, 0], TM)
            hs = _matmul(xs, w_in, DB, 512)
            cos_s, sin_s = _rope_tables(pos_s, DH_A)
            qs, ks32, _, _ = _ev_prep(hs, cos_s, sin_s, DB)
            vs32 = hs[:, 2 * MIX_W:3 * MIX_W]
            oa_s = _dattn_sample(qs.astype(F32), ks32, vs32, cache_a_k[j].reshape(n_pool, page, MIX_W),
                                 cache_a_v[j].reshape(n_pool, page, MIX_W), page_table,
                                 ev_lam[j], ev_subln_g[j], lam_init)
            ob_s, sb_s = _hgrn_step(hs, state_b[j], lb_table[j], ev_b_norm_g[j], 8)
            outs["ak_s"].append(ks32.reshape(DB, 1, N_HEADS, 2, DH_A))
            outs["av_s"].append(vs32.reshape(DB, 1, N_HEADS, HEAD_W))
            outs["sb_s"].append(sb_s)
            xs = _proj_ln([oa_s, ob_s], [w_out[:MIX_W], w_out[MIX_W:]], xs, ln_g[l, 0], ln_b[l, 0], DB)
        else:
            n_main = 8 * MIX_W
            w_in = od_w_in[j][:, :n_main].astype(BF16)
            w_gate = jnp.pad(od_w_in[j][:, n_main:], ((0, 0), (0, LANES - 2 * N_HEADS))).astype(BF16)
            w_out = od_w_out[j].astype(BF16)
            h = _matmul(xp, w_in, 1024, 512)
            gates = _matmul(xp, w_gate, 1024, LANES)[:, :2 * N_HEADS]
            cos_t, sin_t = _rope_tables(pos_p, HEAD_W)
            o_c, o_d, sct, dct, dn, dm = _od_prompt(h.reshape(B, T, -1), gates.reshape(B, T, -1), od_b_if[j],
                                                    cos_t, sin_t, od_c_norm_g[j], od_d_norm_g[j], 512)
            outs["sc_p"].append(swap(sct))
            outs["dc_p"].append(swap(dct))
            outs["dn_p"].append(dn[:, :N_HEADS, :])
            outs["dm_p"].append(dm[:, :N_HEADS, 0])
            xp = _proj_ln([o_c.reshape(N, MIX_W), o_d.reshape(N, MIX_W)], [w_out[:MIX_W], w_out[MIX_W:]],
                          xp, ln_g[l, 0], ln_b[l, 0], TM)
            hs = _matmul(xs, w_in, DB, 512)
            gates_s = _matmul(xs, w_gate, DB, LANES)
            cos_s, sin_s = _rope_tables(pos_s[:1], HEAD_W)
            oc_s, od_s, sc_s, dc_s, dn_s, dm_s = _od_step(hs, gates_s, od_b_if[j], cos_s, sin_s,
                                                          state_c[j], state_d_c[j], state_d_n[j], state_d_m[j],
                                                          od_c_norm_g[j], od_d_norm_g[j], 8)
            outs["sc_s"].append(sc_s)
            outs["dc_s"].append(dc_s)
            outs["dn_s"].append(dn_s)
            outs["dm_s"].append(dm_s)
            xs = _proj_ln([oc_s, od_s], [w_out[:MIX_W], w_out[MIX_W:]], xs, ln_g[l, 0], ln_b[l, 0], DB)

        wq = xa_wq[l].astype(BF16)
        wo = xa_wo[l].astype(BF16)
        mkv = _matmul(mem_prompt.reshape(B * n_mem, D_MODEL), xa_wkv[l].astype(BF16), 512, 512)
        mk, mv = mkv[:, :D_MODEL], mkv[:, D_MODEL:]
        outs["mk_p"].append(mk.reshape(B, n_mem, N_HEADS, DH_X))
        outs["mv_p"].append(mv.reshape(B, n_mem, N_HEADS, DH_X))
        xp = _xattn_prompt(xp, wq, mk.astype(BF16).reshape(B, n_mem, D_MODEL),
                           mv.astype(BF16).reshape(B, n_mem, D_MODEL), wo, ln_g[l, 1], ln_b[l, 1], T, TM)
        q_s = _matmul(xs, wq, DB, 512)
        xo_s = _xattn_sample(q_s, cache_mem_k[l].reshape(DB, n_mem, D_MODEL),
                             cache_mem_v[l].reshape(DB, n_mem, D_MODEL))
        xs = _proj_ln([xo_s], [wo], xs, ln_g[l, 1], ln_b[l, 1], DB)

        w_up = ffn_w_up[l].astype(BF16)
        w_down = ffn_w_down[l].astype(BF16)
        tm_f = min(TM, T)
        xp, tails = _ffn_prompt(xp, w_up, ffn_conv_w[l], ffn_conv_b[l], w_down, ln_g[l, 2], ln_b[l, 2], T, TM, tf)
        tails = tails.reshape(B, T // tm_f, SUBLANES, d_ff)
        outs["cv_p"].append(tails[:, -1, SUBLANES - 2:, :])
        ug_s = _matmul(xs, w_up, DB, tf)
        buf = state_conv[l]
        xs = _ffn_sample(xs, ug_s, buf[:, 0, :], buf[:, 1, :], ffn_conv_w[l], ffn_conv_b[l], w_down,
                         ln_g[l, 2], ln_b[l, 2], tf)
        outs["cv_s"].append(jnp.stack([buf[:, 1, :], ug_s[:, :d_ff]], axis=1))

    st = lambda k: jnp.stack(outs[k])
    return (xp.reshape(B, T, D_MODEL), xs.reshape(DB, 1, D_MODEL),
            st("ak_p"), st("av_p"), st("ak_s"), st("av_s"), st("sb_p"), st("sb_s"),
            st("sc_p"), st("sc_s"), st("dc_p"), st("dc_s"), st("dn_p"), st("dn_s"), st("dm_p"), st("dm_s"),
            st("mk_p"), st("mv_p"), st("cv_p"), st("cv_s"))
```

```python
import functools
import math

import numpy as np
import jax
import jax.numpy as jnp
from jax import lax
from jax.experimental import pallas as pl
from jax.experimental.pallas import tpu as pltpu

F32 = jnp.float32
BF16 = jnp.bfloat16

D_MODEL = 1024
MIX_W = D_MODEL // 2
N_HEADS = 4
HEAD_W = MIX_W // N_HEADS
DH_A = HEAD_W // 2
DH_X = D_MODEL // N_HEADS
CHUNK = 64
ROPE_THETA = 10000.0
NORM_EPS = 1e-5
DEPTH = 2
DN_ALPHA = (2.0 * DEPTH) ** 0.25
LANES = 128
SUBLANES = 8
VMEM_LIMIT = 56 * 1024 * 1024
NEG_INF = float("-inf")

NT_DIMS = (((1,), (1,)), ((), ()))
TN_DIMS = (((0,), (0,)), ((), ()))


def _cparams(n_axes, vmem=VMEM_LIMIT):
    return pltpu.CompilerParams(dimension_semantics=("arbitrary",) * n_axes, vmem_limit_bytes=vmem)


def _bdot(a, b):
    return jnp.dot(a.astype(BF16), b.astype(BF16), preferred_element_type=F32)


def _bdot_nt(a, b):
    return lax.dot_general(a.astype(BF16), b.astype(BF16), NT_DIMS, preferred_element_type=F32)


def _bdot_tn(a, b):
    return lax.dot_general(a.astype(BF16), b.astype(BF16), TN_DIMS, preferred_element_type=F32)


def _split3(x):
    p1 = x.astype(BF16)
    r1 = x - p1.astype(F32)
    p2 = r1.astype(BF16)
    p3 = (r1 - p2.astype(F32)).astype(BF16)
    return p1, p2, p3


def _cumsum_rows(tri, x):
    p1, p2, p3 = _split3(x)
    d = functools.partial(jnp.dot, preferred_element_type=F32)
    return d(tri, p1) + d(tri, p2) + d(tri, p3)


def _cumsum_lanes(x, triu):
    p1, p2, p3 = _split3(x)
    d = functools.partial(jnp.dot, preferred_element_type=F32)
    return d(p1, triu) + d(p2, triu) + d(p3, triu)


def _tri(L, lower):
    r = lax.broadcasted_iota(jnp.int32, (L, L), 0)
    c = lax.broadcasted_iota(jnp.int32, (L, L), 1)
    return (r >= c) if lower else (r <= c)


def _sigmoid(x):
    return 1.0 / (1.0 + jnp.exp(-x))


def _silu(x):
    return x * _sigmoid(x)


def _log_sigmoid(x):
    return jnp.minimum(x, 0.0) - jnp.log(1.0 + jnp.exp(-jnp.abs(x)))


def _layernorm(z, g, b):
    mu = jnp.mean(z, -1, keepdims=True)
    zc = z - mu
    var = jnp.mean(zc * zc, -1, keepdims=True)
    return zc * lax.rsqrt(var + NORM_EPS) * g + b


def _rmsnorm(x, g):
    return x * lax.rsqrt(jnp.mean(x * x, -1, keepdims=True) + NORM_EPS) * g


def _groupnorm(x, g):
    mu = jnp.mean(x, -1, keepdims=True)
    xc = x - mu
    var = jnp.mean(xc * xc, -1, keepdims=True)
    return xc * lax.rsqrt(var + NORM_EPS) * g


def _rope(x, cos, sin, half):
    outs = []
    for c in range(x.shape[1] // LANES):
        xc = x[:, c * LANES:(c + 1) * LANES]
        if 2 * half == LANES:
            sw = pltpu.roll(xc, half, 1)
        else:
            lane = lax.broadcasted_iota(jnp.int32, xc.shape, 1)
            first = (lane & (2 * half - 1)) < half
            sw = jnp.where(first, pltpu.roll(xc, LANES - half, 1), pltpu.roll(xc, half, 1))
        outs.append(xc * cos + sw * sin)
    return outs[0] if len(outs) == 1 else jnp.concatenate(outs, axis=1)


def _rope_tables(pos, d):
    inv = ROPE_THETA ** (-jnp.arange(0, d // 2, dtype=F32) * 2.0 / d)
    ang = pos.astype(F32)[:, None] * inv[None, :]
    cos, sin = jnp.cos(ang), jnp.sin(ang)
    reps = LANES // d
    cos_t = jnp.tile(jnp.concatenate([cos, cos], -1), (1, reps))
    sin_t = jnp.tile(jnp.concatenate([-sin, sin], -1), (1, reps))
    return cos_t, sin_t


def _col(row):
    return jnp.broadcast_to(row, (LANES, LANES)).T


def _mm_kernel(x_ref, w_ref, o_ref, xb_ref):
    @pl.when(pl.program_id(1) == 0)
    def _():
        xb_ref[...] = x_ref[...].astype(BF16)

    o_ref[...] = jnp.dot(xb_ref[...], w_ref[...], preferred_element_type=F32).astype(o_ref.dtype)


def _matmul(x, w, tm, tn, name, out_dtype=F32):
    M, K = x.shape
    N = w.shape[1]
    tm, tn = min(tm, M), min(tn, N)
    return pl.pallas_call(
        _mm_kernel,
        out_shape=jax.ShapeDtypeStruct((M, N), out_dtype),
        grid=(M // tm, N // tn),
        in_specs=[pl.BlockSpec((tm, K), lambda i, j: (i, 0)),
                  pl.BlockSpec((K, tn), lambda i, j: (0, j))],
        out_specs=pl.BlockSpec((tm, tn), lambda i, j: (i, j)),
        scratch_shapes=[pltpu.VMEM((tm, K), BF16)],
        compiler_params=_cparams(2),
        name=name,
    )(x, w)


def _proj_ln_kernel(*refs, n_in):
    a_refs, w_refs = refs[:n_in], refs[n_in:2 * n_in]
    x_ref, g_ref, b_ref, o_ref = refs[2 * n_in:]
    y = _bdot(a_refs[0][...], w_refs[0][...])
    for a_ref, w_ref in zip(a_refs[1:], w_refs[1:]):
        y = y + _bdot(a_ref[...], w_ref[...])
    o_ref[...] = _layernorm(DN_ALPHA * x_ref[...] + y, g_ref[...], b_ref[...])


def _proj_ln(acts, weights, x, g, b, tm, name):
    M = x.shape[0]
    tm = min(tm, M)
    n_in = len(acts)
    in_specs = [pl.BlockSpec((tm, a.shape[1]), lambda i: (i, 0)) for a in acts]
    in_specs += [pl.BlockSpec(w.shape, lambda i: (0, 0)) for w in weights]
    in_specs += [pl.BlockSpec((tm, D_MODEL), lambda i: (i, 0)),
                 pl.BlockSpec((1, D_MODEL), lambda i: (0, 0)),
                 pl.BlockSpec((1, D_MODEL), lambda i: (0, 0))]
    return pl.pallas_call(
        functools.partial(_proj_ln_kernel, n_in=n_in),
        out_shape=jax.ShapeDtypeStruct((M, D_MODEL), F32),
        grid=(M // tm,),
        in_specs=in_specs,
        out_specs=pl.BlockSpec((tm, D_MODEL), lambda i: (i, 0)),
        compiler_params=_cparams(1),
        name=name,
    )(*acts, *weights, x, g.reshape(1, -1), b.reshape(1, -1))


Q_SCALE = DH_A ** -0.5 * math.log2(math.e)


def _ev_prep_prompt_kernel(qk_ref, v_ref, cos_ref, sin_ref, qt_ref, kb_ref, kt_ref, vt_ref):
    cos, sin = cos_ref[...], sin_ref[...]
    q = _rope(qk_ref[0, :, :MIX_W], cos, sin, DH_A // 2)
    k = _rope(qk_ref[0, :, MIX_W:], cos, sin, DH_A // 2)
    qt_ref[0] = (q * Q_SCALE).T.astype(BF16)
    kb_ref[0] = k.astype(BF16)
    kt_ref[0] = k.T
    vt_ref[0] = v_ref[0].T.astype(BF16)


def _ev_prep_prompt(h3, cos_t, sin_t, tm):
    B, T, _ = h3.shape
    tm = min(tm, T)
    tab = pl.BlockSpec((tm, LANES), lambda b, i: (i, 0))
    tr = pl.BlockSpec((1, MIX_W, tm), lambda b, i: (b, 0, i))
    tshape = lambda dt: jax.ShapeDtypeStruct((B, MIX_W, T), dt)
    return pl.pallas_call(
        _ev_prep_prompt_kernel,
        out_shape=(tshape(BF16), jax.ShapeDtypeStruct((B, T, MIX_W), BF16), tshape(F32), tshape(BF16)),
        grid=(B, T // tm),
        in_specs=[pl.BlockSpec((1, tm, 2 * MIX_W), lambda b, i: (b, i, 0)),
                  pl.BlockSpec((1, tm, MIX_W), lambda b, i: (b, i, 2)), tab, tab],
        out_specs=(tr, pl.BlockSpec((1, tm, MIX_W), lambda b, i: (b, i, 0)), tr, tr),
        compiler_params=_cparams(2),
        name="ev_prep_prompt",
    )(h3, h3, cos_t, sin_t)


def _ev_prep_sample_kernel(qk_ref, cos_ref, sin_ref, q_ref, k_ref, kt_ref):
    cos, sin = cos_ref[...], sin_ref[...]
    k = _rope(qk_ref[:, MIX_W:], cos, sin, DH_A // 2)
    q_ref[...] = _rope(qk_ref[:, :MIX_W], cos, sin, DH_A // 2) * Q_SCALE
    k_ref[...] = k
    kt_ref[...] = k.T


def _ev_prep_sample(h, cos_t, sin_t):
    DB = h.shape[0]
    full = lambda shape: pl.BlockSpec(shape, lambda i: (0,) * len(shape))
    return pl.pallas_call(
        _ev_prep_sample_kernel,
        out_shape=(jax.ShapeDtypeStruct((DB, MIX_W), F32), jax.ShapeDtypeStruct((DB, MIX_W), F32),
                   jax.ShapeDtypeStruct((MIX_W, DB), F32)),
        grid=(1,),
        in_specs=[full((DB, 2 * MIX_W)), full((DB, LANES)), full((DB, LANES))],
        out_specs=(full((DB, MIX_W)), full((DB, MIX_W)), full((MIX_W, DB))),
        compiler_params=_cparams(1),
        name="ev_prep_sample",
    )(h, cos_t, sin_t)


def _lambda(lam_ref, lam_init):
    lp = lam_ref[...]
    s01 = jnp.sum(lp[0:1] * lp[1:2], axis=1, keepdims=True)
    s23 = jnp.sum(lp[2:3] * lp[3:4], axis=1, keepdims=True)
    return jnp.exp(s01) - jnp.exp(s23) + lam_init


def _dattn_kernel(lam_ref, g_ref, qt_ref, k_ref, vt_ref, o_ref, m_ref, l_ref, acc_ref, *, tq, lam_init):
    i = pl.program_id(2)
    qt = qt_ref[0]
    sub = lax.broadcasted_iota(jnp.int32, qt.shape, 0)
    zero = jnp.zeros_like(qt)
    q2 = jnp.concatenate([jnp.where(sub < DH_A, qt, zero), jnp.where(sub >= DH_A, qt, zero)], axis=1)
    m_ref[...] = jnp.full(m_ref.shape, NEG_INF, F32)
    l_ref[...] = jnp.zeros(l_ref.shape, F32)
    acc_ref[...] = jnp.zeros(acc_ref.shape, F32)

    def step(j, masked):
        start = pl.multiple_of(j * tq, tq)
        kj = k_ref[0, pl.ds(start, tq), :]
        vtj = vt_ref[0, :, pl.ds(start, tq)]
        s = jnp.dot(kj, q2, preferred_element_type=F32)
        if masked:
            r = lax.broadcasted_iota(jnp.int32, s.shape, 0)
            c = lax.broadcasted_iota(jnp.int32, s.shape, 1)
            c = jnp.where(c >= tq, c - tq, c)
            s = jnp.where(r <= c, s, NEG_INF)
        m_prev = m_ref[...]
        m_new = jnp.maximum(m_prev, jnp.max(s, axis=0, keepdims=True))
        alpha = jnp.exp2(m_prev - m_new)
        p = jnp.exp2(s - m_new)
        l_ref[...] = alpha * l_ref[...] + jnp.sum(p, axis=0, keepdims=True)
        acc_ref[...] = alpha * acc_ref[...] + jnp.dot(vtj, p.astype(BF16), preferred_element_type=F32)
        m_ref[...] = m_new

    def body(j, carry):
        step(j, False)
        return carry

    lax.fori_loop(0, i, body, 0)
    step(i, True)
    ot = acc_ref[...] / l_ref[...]
    lam = _lambda(lam_ref, lam_init)
    d = (ot[:, :tq] - lam * ot[:, tq:]).T
    o_ref[0] = _rmsnorm(d, g_ref[...]) * (1.0 - lam_init)


def _dattn_prompt(qt, kb, vt, lam_p, subln_g, lam_init, tq):
    B, T, _ = kb.shape
    tq = min(tq, T)
    return pl.pallas_call(
        functools.partial(_dattn_kernel, tq=tq, lam_init=lam_init),
        out_shape=jax.ShapeDtypeStruct((B, T, MIX_W), F32),
        grid=(B, N_HEADS, T // tq),
        in_specs=[pl.BlockSpec((4, DH_A), lambda b, h, i: (0, 0)),
                  pl.BlockSpec((1, HEAD_W), lambda b, h, i: (0, 0)),
                  pl.BlockSpec((1, HEAD_W, tq), lambda b, h, i: (b, h, i)),
                  pl.BlockSpec((1, T, HEAD_W), lambda b, h, i: (b, 0, h)),
                  pl.BlockSpec((1, HEAD_W, T), lambda b, h, i: (b, h, 0))],
        out_specs=pl.BlockSpec((1, tq, HEAD_W), lambda b, h, i: (b, i, h)),
        scratch_shapes=[pltpu.VMEM((1, 2 * tq), F32), pltpu.VMEM((1, 2 * tq), F32),
                        pltpu.VMEM((HEAD_W, 2 * tq), F32)],
        compiler_params=_cparams(3),
        name="dattn_prompt",
    )(lam_p, subln_g.reshape(1, -1), qt, kb, vt)


def _gla_chunk(q_in, k_in, k_end, v, dec_row, st_ref, h, tril_mask):
    st = st_ref[h]
    attn = jnp.where(tril_mask, _bdot_nt(q_in, k_in), 0.0)
    o = _bdot(attn, v) + _bdot_nt(q_in, st)
    st_ref[h] = st * dec_row + _bdot_tn(v, k_end)
    return o


def _hgrn_kernel(q_ref, f_ref, i_ref, g_ref, lb_ref, ng_ref, o_ref, s_ref, st_ref, *, tt, L):
    t = pl.program_id(1)

    @pl.when(t == 0)
    def _():
        st_ref[...] = jnp.zeros(st_ref.shape, F32)

    tril_mask = _tri(L, True)
    tril = tril_mask.astype(BF16)

    def body(c, carry):
        rows = pl.ds(pl.multiple_of(c * L, L), L)
        for h in range(N_HEADS):
            cols = slice(h * HEAD_W, (h + 1) * HEAD_W)
            lb = lb_ref[:, cols]
            f = lb + (1.0 - lb) * _sigmoid(f_ref[0, rows, cols])
            b = _cumsum_rows(tril, jnp.log(f))
            b_end = b[L - 1:L, :]
            k = 1.0 - f
            q_in = _silu(q_ref[0, rows, cols]) * jnp.exp(b)
            o = _gla_chunk(q_in, k * jnp.exp(-b), k * jnp.exp(b_end - b), i_ref[0, rows, cols],
                           jnp.exp(b_end), st_ref, h, tril_mask)
            o_ref[0, rows, cols] = _rmsnorm(o, ng_ref[...]) * _silu(g_ref[0, rows, cols])
        return carry

    lax.fori_loop(0, tt // L, body, 0)

    @pl.when(t == pl.num_programs(1) - 1)
    def _():
        s_ref[0] = st_ref[...]


def _hgrn_prompt(h3, lb, norm_g, tt):
    B, T, _ = h3.shape
    tt = min(tt, T)
    L = math.gcd(T, CHUNK)
    col = lambda c: pl.BlockSpec((1, tt, MIX_W), lambda b, t: (b, t, c))
    return pl.pallas_call(
        functools.partial(_hgrn_kernel, tt=tt, L=L),
        out_shape=(jax.ShapeDtypeStruct((B, T, MIX_W), F32),
                   jax.ShapeDtypeStruct((B, N_HEADS, HEAD_W, HEAD_W), F32)),
        grid=(B, T // tt),
        in_specs=[col(3), col(4), col(5), col(6),
                  pl.BlockSpec((1, MIX_W), lambda b, t: (0, 0)),
                  pl.BlockSpec((1, HEAD_W), lambda b, t: (0, 0))],
        out_specs=(pl.BlockSpec((1, tt, MIX_W), lambda b, t: (b, t, 0)),
                   pl.BlockSpec((1, N_HEADS, HEAD_W, HEAD_W), lambda b, t: (b, 0, 0, 0))),
        scratch_shapes=[pltpu.VMEM((N_HEADS, HEAD_W, HEAD_W), F32)],
        compiler_params=_cparams(2),
        name="hgrn_prompt",
    )(h3, h3, h3, h3, lb.reshape(1, -1), norm_g.reshape(1, -1))


def _log_gamma(h):
    return float(np.log(1.0 - 2.0 ** (-5.0 - h)))


def _od_kernel(cq_ref, ck_ref, cv_ref, cg_ref, dq_ref, dk_ref, dv_ref, do_ref, gc_ref, gr_ref,
               bc_ref, br_ref, cos_ref, sin_ref, cng_ref, dng_ref,
               oc_ref, od_ref, sc_ref, dc_ref, dn_ref, dm_ref,
               st_ref, ct_ref, n_ref, m_ref, *, tt, L):
    t = pl.program_id(1)

    @pl.when(t == 0)
    def _():
        st_ref[...] = jnp.zeros(st_ref.shape, F32)
        ct_ref[...] = jnp.zeros(ct_ref.shape, F32)
        n_ref[...] = jnp.zeros(n_ref.shape, F32)
        m_ref[...] = jnp.zeros(m_ref.shape, F32)

    tril_mask = _tri(L, True)
    tril = tril_mask.astype(BF16)
    triu = _tri(L, False).astype(BF16)
    pos1 = (lax.broadcasted_iota(jnp.int32, (L, HEAD_W), 0) + 1).astype(F32)
    scale = HEAD_W ** -0.5

    def body(c, carry):
        rows = pl.ds(pl.multiple_of(c * L, L), L)
        cos, sin = cos_ref[rows, :], sin_ref[rows, :]
        for h in range(N_HEADS):
            cols = slice(h * HEAD_W, (h + 1) * HEAD_W)
            lg = _log_gamma(h)
            b = pos1 * lg
            q = _rope(cq_ref[0, rows, cols], cos, sin, HEAD_W // 2)
            k = _rope(ck_ref[0, rows, cols], cos, sin, HEAD_W // 2) * scale
            b_end = L * lg
            o = _gla_chunk(q * jnp.exp(b), k * jnp.exp(-b), k * jnp.exp(b_end - b), cv_ref[0, rows, cols],
                           math.exp(b_end), st_ref, h, tril_mask)
            oc_ref[0, rows, cols] = _groupnorm(o, cng_ref[...]) * _silu(cg_ref[0, rows, cols])
            li_c = gc_ref[0, rows, h:h + 1] + bc_ref[:, h:h + 1]
            lf_c = _log_sigmoid(gc_ref[0, rows, N_HEADS + h:N_HEADS + h + 1]
                                + bc_ref[:, N_HEADS + h:N_HEADS + h + 1])
            li_r = gr_ref[0, c, h:h + 1, :] + br_ref[h:h + 1, :]
            lf_r = _log_sigmoid(gr_ref[0, c, N_HEADS + h:N_HEADS + h + 1, :]
                                + br_ref[N_HEADS + h:N_HEADS + h + 1, :])
            bc = _cumsum_rows(tril, jnp.broadcast_to(lf_c, (L, L)))
            br = _cumsum_lanes(jnp.broadcast_to(lf_r, (SUBLANES, L)), triu)[0:1, :]
            bc1 = bc[:, 0:1]
            m_prev = m_ref[h:h + 1, 0:1]
            dm = jnp.where(tril_mask, bc - br + li_r, NEG_INF)
            m_t = jnp.maximum(bc1 + m_prev, jnp.max(dm, axis=1, keepdims=True))
            inter = jnp.exp(bc1 + m_prev - m_t)
            dq = dq_ref[0, rows, cols] * scale
            dk = dk_ref[0, rows, cols]
            dv = dv_ref[0, rows, cols]
            w_mat = jnp.exp(dm - m_t) * _bdot_nt(dq, dk)
            ct = ct_ref[h]
            n_row = n_ref[h:h + 1, :]
            num = inter * _bdot_nt(dq, ct) + _bdot(w_mat, dv)
            den = inter * jnp.sum(dq * n_row, axis=1, keepdims=True) + jnp.sum(w_mat, axis=1, keepdims=True)
            hh = num / jnp.maximum(jnp.abs(den), jnp.exp(-m_t))
            od_ref[0, rows, cols] = _groupnorm(hh, dng_ref[...]) * _sigmoid(do_ref[0, rows, cols])
            m_new = m_t[L - 1:L, :]
            b_last = bc1[L - 1:L, :]
            c_scale = jnp.exp(b_last + m_prev - m_new)
            kw = dk * jnp.exp(b_last - bc1 + li_c - m_new)
            ct_ref[h] = c_scale * ct + _bdot_tn(dv, kw)
            n_ref[h:h + 1, :] = c_scale * n_row + jnp.sum(kw, axis=0, keepdims=True)
            m_ref[h:h + 1, :] = jnp.broadcast_to(m_new, (1, LANES))
        return carry

    lax.fori_loop(0, tt // L, body, 0)

    @pl.when(t == pl.num_programs(1) - 1)
    def _():
        sc_ref[0] = st_ref[...]
        dc_ref[0] = ct_ref[...]
        dn_ref[0] = n_ref[...]
        dm_ref[0] = m_ref[...]


def _od_prompt(h3, gates, b_if, cos_t, sin_t, c_norm_g, d_norm_g, tt):
    B, T, _ = h3.shape
    tt = min(tt, T)
    L = math.gcd(T, CHUNK)
    gates_r = gates.reshape(B, T // L, L, 2 * N_HEADS).transpose(0, 1, 3, 2)
    col = lambda c: pl.BlockSpec((1, tt, MIX_W), lambda b, t: (b, t, c))
    tab = pl.BlockSpec((tt, LANES), lambda b, t: (t, 0))
    vec = pl.BlockSpec((1, HEAD_W), lambda b, t: (0, 0))
    mat_state = pl.BlockSpec((1, N_HEADS, HEAD_W, HEAD_W), lambda b, t: (b, 0, 0, 0))
    row_state = pl.BlockSpec((1, SUBLANES, LANES), lambda b, t: (b, 0, 0))
    out_blk = pl.BlockSpec((1, tt, MIX_W), lambda b, t: (b, t, 0))
    return pl.pallas_call(
        functools.partial(_od_kernel, tt=tt, L=L),
        out_shape=(jax.ShapeDtypeStruct((B, T, MIX_W), F32), jax.ShapeDtypeStruct((B, T, MIX_W), F32),
                   jax.ShapeDtypeStruct((B, N_HEADS, HEAD_W, HEAD_W), F32),
                   jax.ShapeDtypeStruct((B, N_HEADS, HEAD_W, HEAD_W), F32),
                   jax.ShapeDtypeStruct((B, SUBLANES, LANES), F32),
                   jax.ShapeDtypeStruct((B, SUBLANES, LANES), F32)),
        grid=(B, T // tt),
        in_specs=[col(0), col(1), col(2), col(3), col(4), col(5), col(6), col(7),
                  pl.BlockSpec((1, tt, 2 * N_HEADS), lambda b, t: (b, t, 0)),
                  pl.BlockSpec((1, tt // L, 2 * N_HEADS, L), lambda b, t: (b, t, 0, 0)),
                  pl.BlockSpec((1, 2 * N_HEADS), lambda b, t: (0, 0)),
                  pl.BlockSpec((2 * N_HEADS, 1), lambda b, t: (0, 0)),
                  tab, tab, vec, vec],
        out_specs=(out_blk, out_blk, mat_state, mat_state, row_state, row_state),
        scratch_shapes=[pltpu.VMEM((N_HEADS, HEAD_W, HEAD_W), F32), pltpu.VMEM((N_HEADS, HEAD_W, HEAD_W), F32),
                        pltpu.VMEM((SUBLANES, LANES), F32), pltpu.VMEM((SUBLANES, LANES), F32)],
        compiler_params=_cparams(2),
        name="od_prompt",
    )(h3, h3, h3, h3, h3, h3, h3, h3, gates, gates_r, b_if.reshape(1, -1), b_if.reshape(-1, 1),
      cos_t, sin_t, c_norm_g.reshape(1, -1), d_norm_g.reshape(1, -1))


def _xattn_kernel(x_ref, wq_ref, mk_ref, mv_ref, wo_ref, g_ref, b_ref, o_ref):
    x = x_ref[...]
    q = _bdot(x, wq_ref[...])
    qb = (q * (DH_X ** -0.5)).astype(BF16)
    outs = []
    for h in range(N_HEADS):
        cols = slice(h * DH_X, (h + 1) * DH_X)
        s = lax.dot_general(qb[:, cols], mk_ref[0, :, cols], NT_DIMS, preferred_element_type=F32)
        p = jnp.exp(s - jnp.max(s, axis=1, keepdims=True))
        l = jnp.sum(p, axis=1, keepdims=True)
        outs.append((jnp.dot(p.astype(BF16), mv_ref[0, :, cols], preferred_element_type=F32) / l).astype(BF16))
    y = jnp.dot(jnp.concatenate(outs, axis=1), wo_ref[...], preferred_element_type=F32)
    o_ref[...] = _layernorm(DN_ALPHA * x + y, g_ref[...], b_ref[...])


def _xattn_prompt(x, wq, mk, mv, wo, g, b, T, tm):
    M = x.shape[0]
    tm = min(tm, T)
    n_mem = mk.shape[1]
    per_b = T // tm
    full = lambda shape: pl.BlockSpec(shape, lambda i: (0,) * len(shape))
    mem = pl.BlockSpec((1, n_mem, D_MODEL), lambda i: (i // per_b, 0, 0))
    return pl.pallas_call(
        _xattn_kernel,
        out_shape=jax.ShapeDtypeStruct((M, D_MODEL), F32),
        grid=(M // tm,),
        in_specs=[pl.BlockSpec((tm, D_MODEL), lambda i: (i, 0)), full((D_MODEL, D_MODEL)), mem, mem,
                  full((D_MODEL, D_MODEL)), full((1, D_MODEL)), full((1, D_MODEL))],
        out_specs=pl.BlockSpec((tm, D_MODEL), lambda i: (i, 0)),
        compiler_params=_cparams(1),
        name="xattn_prompt",
    )(x, wq, mk, mv, wo, g.reshape(1, -1), b.reshape(1, -1))


def _ffn_kernel(x_ref, wu_ref, wg_ref, cw_ref, cb_ref, wd_ref, g_ref, b_ref, o_ref, tail_ref,
                xb_ref, acc_ref, stage_ref, carry_ref, *, tm, per_b):
    i, j = pl.program_id(0), pl.program_id(1)

    @pl.when(j == 0)
    def _():
        xb_ref[...] = x_ref[...].astype(BF16)

    @pl.when(i % per_b == 0)
    def _():
        carry_ref[j] = jnp.zeros(carry_ref.shape[1:], F32)

    xb = xb_ref[...]
    stage_ref[0:SUBLANES, :] = carry_ref[j]
    stage_ref[SUBLANES:, :] = jnp.dot(xb, wu_ref[...], preferred_element_type=F32)
    last = stage_ref[tm:tm + SUBLANES, :]
    carry_ref[j] = last
    tail_ref[0] = last
    conv = (cb_ref[...] + cw_ref[0:1, :] * stage_ref[SUBLANES - 2:SUBLANES - 2 + tm, :]
            + cw_ref[1:2, :] * stage_ref[SUBLANES - 1:SUBLANES - 1 + tm, :]
            + cw_ref[2:3, :] * stage_ref[SUBLANES:, :])
    gate = jnp.dot(xb, wg_ref[...], preferred_element_type=F32)
    y = _bdot(jax.nn.gelu(conv) * gate, wd_ref[...])

    @pl.when(j == 0)
    def _():
        acc_ref[...] = y

    @pl.when(j > 0)
    def _():
        acc_ref[...] = acc_ref[...] + y

    @pl.when(j == pl.num_programs(1) - 1)
    def _():
        o_ref[...] = _layernorm(DN_ALPHA * x_ref[...] + acc_ref[...], g_ref[...], b_ref[...])


def _ffn_prompt(x, w_up, conv_w, conv_b, w_down, g, b, T, tm, tf):
    M = x.shape[0]
    d_ff = w_down.shape[0]
    tm = min(tm, T)
    nf = d_ff // tf
    per_b = T // tm
    return pl.pallas_call(
        functools.partial(_ffn_kernel, tm=tm, per_b=per_b),
        out_shape=(jax.ShapeDtypeStruct((M, D_MODEL), F32),
                   jax.ShapeDtypeStruct((M // tm, SUBLANES, d_ff), F32)),
        grid=(M // tm, nf),
        in_specs=[pl.BlockSpec((tm, D_MODEL), lambda i, j: (i, 0)),
                  pl.BlockSpec((D_MODEL, tf), lambda i, j: (0, j)),
                  pl.BlockSpec((D_MODEL, tf), lambda i, j: (0, nf + j)),
                  pl.BlockSpec((3, tf), lambda i, j: (0, j)),
                  pl.BlockSpec((1, tf), lambda i, j: (0, j)),
                  pl.BlockSpec((tf, D_MODEL), lambda i, j: (j, 0)),
                  pl.BlockSpec((1, D_MODEL), lambda i, j: (0, 0)),
                  pl.BlockSpec((1, D_MODEL), lambda i, j: (0, 0))],
        out_specs=(pl.BlockSpec((tm, D_MODEL), lambda i, j: (i, 0)),
                   pl.BlockSpec((1, SUBLANES, tf), lambda i, j: (i, 0, j))),
        scratch_shapes=[pltpu.VMEM((tm, D_MODEL), BF16), pltpu.VMEM((tm, D_MODEL), F32),
                        pltpu.VMEM((tm + SUBLANES, tf), F32), pltpu.VMEM((nf, SUBLANES, tf), F32)],
        compiler_params=_cparams(2),
        name="ffn_prompt",
    )(x, w_up, w_up, conv_w, conv_b.reshape(1, -1), w_down, g.reshape(1, -1), b.reshape(1, -1))


def _dattn_sample_kernel(*refs, n_pages, page, lam_init):
    pt_ref = refs[0]
    lam_ref, g_ref, q_ref, kn_ref, vn_ref = refs[1:6]
    k_refs = refs[6:6 + n_pages]
    v_refs = refs[6 + n_pages:6 + 2 * n_pages]
    o_ref = refs[6 + 2 * n_pages]
    del pt_ref
    n_rows = 2 * N_HEADS
    q = q_ref[0]
    lane = lax.broadcasted_iota(jnp.int32, (n_rows, MIX_W), 1)
    row = lax.broadcasted_iota(jnp.int32, (n_rows, MIX_W), 0)
    qbd = jnp.where(lax.shift_right_logical(lane, 6) == row, jnp.broadcast_to(q, (n_rows, MIX_W)), 0.0)
    s_new = jnp.sum(qbd * kn_ref[0], axis=1, keepdims=True)
    scores = [_bdot(qbd, k_refs[p][0]) for p in range(n_pages)]
    m = s_new
    for s in scores:
        m = jnp.maximum(m, jnp.max(s, axis=1, keepdims=True))
    p_new = jnp.exp2(s_new - m)
    l = p_new
    vn = vn_ref[0]
    accs = [p_new * vn[:, h * HEAD_W:(h + 1) * HEAD_W] for h in range(N_HEADS)]
    for p in range(n_pages):
        pr = jnp.exp2(scores[p] - m)
        l = l + jnp.sum(pr, axis=1, keepdims=True)
        for h in range(N_HEADS):
            accs[h] = accs[h] + _bdot(pr, v_refs[p][0, pl.ds(h, page, stride=N_HEADS), :])
    lam = _lambda(lam_ref, lam_init)
    outs = []
    for h in range(N_HEADS):
        o = accs[h] / l
        d = o[2 * h:2 * h + 1, :] - lam * o[2 * h + 1:2 * h + 2, :]
        outs.append(_rmsnorm(d, g_ref[...]) * (1.0 - lam_init))
    o_ref[0] = jnp.concatenate(outs, axis=1)


def _dattn_sample(q, k_new, v_new, kt_pool, v_pool, pool_base, page_table, lam_p, subln_g, lam_init):
    DB = q.shape[0]
    n_pages = page_table.shape[1]
    page = kt_pool.shape[2]
    tok = pl.BlockSpec((1, 1, MIX_W), lambda b, pt: (b, 0, 0))

    def k_spec(p):
        return pl.BlockSpec((1, MIX_W, page), lambda b, pt: (pool_base + pt[b, p], 0, 0))

    def v_spec(p):
        return pl.BlockSpec((1, page * N_HEADS, HEAD_W), lambda b, pt: (pool_base + pt[b, p], 0, 0))

    grid_spec = pltpu.PrefetchScalarGridSpec(
        num_scalar_prefetch=1,
        grid=(DB,),
        in_specs=[pl.BlockSpec((4, DH_A), lambda b, pt: (0, 0)),
                  pl.BlockSpec((1, HEAD_W), lambda b, pt: (0, 0)), tok, tok, tok]
                 + [k_spec(p) for p in range(n_pages)] + [v_spec(p) for p in range(n_pages)],
        out_specs=tok,
    )
    out = pl.pallas_call(
        functools.partial(_dattn_sample_kernel, n_pages=n_pages, page=page, lam_init=lam_init),
        out_shape=jax.ShapeDtypeStruct((DB, 1, MIX_W), F32),
        grid_spec=grid_spec,
        compiler_params=_cparams(1),
        name="dattn_sample",
    )(page_table, lam_p, subln_g.reshape(1, -1), q.reshape(DB, 1, MIX_W), k_new.reshape(DB, 1, MIX_W),
      v_new.reshape(DB, 1, MIX_W), *([kt_pool] * n_pages), *([v_pool] * n_pages))
    return out.reshape(DB, MIX_W)


def _hgrn_step_kernel(h_ref, s0_ref, lb_ref, ng_ref, o_ref, s_ref, *, bt):
    def body(i, carry):
        row = h_ref[i]
        outs = []
        for h in range(N_HEADS):
            c0 = h * HEAD_W
            cols = lambda blk: slice(blk * MIX_W + c0, blk * MIX_W + c0 + HEAD_W)
            lb = lb_ref[:, c0:c0 + HEAD_W]
            f = lb + (1.0 - lb) * _sigmoid(row[:, cols(4)])
            fc = _col(f)
            s_new = fc * s0_ref[i, h] + (1.0 - fc) * row[:, cols(5)]
            s_ref[i, h] = s_new
            q = jnp.broadcast_to(_silu(row[:, cols(3)]), (SUBLANES, HEAD_W))
            o = _bdot(q, s_new)[0:1, :]
            outs.append(_rmsnorm(o, ng_ref[...]) * _silu(row[:, cols(6)]))
        o_ref[i] = jnp.concatenate(outs, axis=1)
        return carry

    lax.fori_loop(0, bt, body, 0)


def _hgrn_step(h, s0, lb, norm_g, bt):
    DB, W = h.shape
    bt = min(bt, DB)
    state = pl.BlockSpec((bt, N_HEADS, HEAD_W, HEAD_W), lambda i: (i, 0, 0, 0))
    o, s = pl.pallas_call(
        functools.partial(_hgrn_step_kernel, bt=bt),
        out_shape=(jax.ShapeDtypeStruct((DB, 1, MIX_W), F32), jax.ShapeDtypeStruct(s0.shape, F32)),
        grid=(DB // bt,),
        in_specs=[pl.BlockSpec((bt, 1, W), lambda i: (i, 0, 0)), state,
                  pl.BlockSpec((1, MIX_W), lambda i: (0, 0)), pl.BlockSpec((1, HEAD_W), lambda i: (0, 0))],
        out_specs=(pl.BlockSpec((bt, 1, MIX_W), lambda i: (i, 0, 0)), state),
        compiler_params=_cparams(1),
        name="hgrn_step",
    )(h.reshape(DB, 1, W), s0, lb.reshape(1, -1), norm_g.reshape(1, -1))
    return o.reshape(DB, MIX_W), s


def _od_step_kernel(h_ref, gt_ref, bif_ref, cos_ref, sin_ref, sc0_ref, dc0_ref, dn0_ref, dm0_ref, cng_ref, dng_ref,
                    oc_ref, od_ref, sc_ref, dc_ref, dn_ref, dm_ref, *, bt):
    scale = HEAD_W ** -0.5

    def body(i, carry):
        row = h_ref[i]
        cos, sin = cos_ref[...], sin_ref[...]
        gates = gt_ref[i] + bif_ref[...]
        m_row = dm0_ref[i]
        oc, od, ms = [], [], []
        for h in range(N_HEADS):
            c0 = h * HEAD_W
            cols = lambda blk: slice(blk * MIX_W + c0, blk * MIX_W + c0 + HEAD_W)
            q = _rope(row[:, cols(0)], cos, sin, HEAD_W // 2)
            k = _rope(row[:, cols(1)], cos, sin, HEAD_W // 2) * scale
            s_new = math.exp(_log_gamma(h)) * sc0_ref[i, h] + _col(k) * row[:, cols(2)]
            sc_ref[i, h] = s_new
            o = _bdot(jnp.broadcast_to(q, (SUBLANES, HEAD_W)), s_new)[0:1, :]
            oc.append(_groupnorm(o, cng_ref[...]) * _silu(row[:, cols(3)]))
            li = gates[:, h:h + 1]
            lf = _log_sigmoid(gates[:, N_HEADS + h:N_HEADS + h + 1])
            m_prev = m_row[:, h:h + 1]
            m_new = jnp.maximum(lf + m_prev, li)
            c_scale = jnp.exp(lf + m_prev - m_new)
            kw = row[:, cols(5)] * jnp.exp(li - m_new)
            c_new = c_scale * dc0_ref[i, h] + _col(kw) * row[:, cols(6)]
            n_new = c_scale * dn0_ref[i, h:h + 1, :] + kw
            dc_ref[i, h] = c_new
            dn_ref[i, h:h + 1, :] = n_new
            ms.append(m_new)
            dq = row[:, cols(4)] * scale
            num = _bdot(jnp.broadcast_to(dq, (SUBLANES, HEAD_W)), c_new)[0:1, :]
            den = jnp.sum(dq * n_new, axis=1, keepdims=True)
            hh = num / jnp.maximum(jnp.abs(den), jnp.exp(-m_new))
            od.append(_groupnorm(hh, dng_ref[...]) * _sigmoid(row[:, cols(7)]))
        oc_ref[i] = jnp.concatenate(oc, axis=1)
        od_ref[i] = jnp.concatenate(od, axis=1)
        dm_ref[i] = jnp.concatenate(ms, axis=1)
        return carry

    lax.fori_loop(0, bt, body, 0)


def _od_step(h, gates, b_if, cos_t, sin_t, sc0, dc0, dn0, dm0, c_norm_g, d_norm_g, bt):
    DB, W = h.shape
    bt = min(bt, DB)
    mat = pl.BlockSpec((bt, N_HEADS, HEAD_W, HEAD_W), lambda i: (i, 0, 0, 0))
    nblk = pl.BlockSpec((bt, N_HEADS, HEAD_W), lambda i: (i, 0, 0))
    mblk = pl.BlockSpec((bt, 1, N_HEADS), lambda i: (i, 0, 0))
    vec = pl.BlockSpec((1, LANES), lambda i: (0, 0))
    out = pl.BlockSpec((bt, 1, MIX_W), lambda i: (i, 0, 0))
    b_pad = jnp.zeros((1, LANES), F32).at[0, :2 * N_HEADS].set(b_if)
    oc, od, sc, dc, dn, dm = pl.pallas_call(
        functools.partial(_od_step_kernel, bt=bt),
        out_shape=(jax.ShapeDtypeStruct((DB, 1, MIX_W), F32), jax.ShapeDtypeStruct((DB, 1, MIX_W), F32),
                   jax.ShapeDtypeStruct(sc0.shape, F32), jax.ShapeDtypeStruct(dc0.shape, F32),
                   jax.ShapeDtypeStruct(dn0.shape, F32), jax.ShapeDtypeStruct((DB, 1, N_HEADS), F32)),
        grid=(DB // bt,),
        in_specs=[pl.BlockSpec((bt, 1, W), lambda i: (i, 0, 0)),
                  pl.BlockSpec((bt, 1, LANES), lambda i: (i, 0, 0)), vec, vec, vec,
                  mat, mat, nblk, mblk, vec, vec],
        out_specs=(out, out, mat, mat, nblk, mblk),
        compiler_params=_cparams(1),
        name="od_step",
    )(h.reshape(DB, 1, W), gates.reshape(DB, 1, LANES), b_pad, cos_t, sin_t, sc0, dc0, dn0,
      dm0.reshape(DB, 1, N_HEADS), c_norm_g.reshape(1, -1), d_norm_g.reshape(1, -1))
    return oc.reshape(DB, MIX_W), od.reshape(DB, MIX_W), sc, dc, dn, dm.reshape(DB, N_HEADS)


X_HALVES = DH_X // LANES
X_ROWS = N_HEADS * X_HALVES


def _xattn_sample_kernel(q_ref, mk_ref, mv_ref, o_ref):
    q = q_ref[0] * (DH_X ** -0.5)
    n_cols = mk_ref.shape[1]
    q2 = jnp.concatenate([q[:, h * DH_X + c * LANES:h * DH_X + (c + 1) * LANES]
                          for c in range(X_HALVES) for h in range(N_HEADS)], axis=0)
    s2 = _bdot_nt(q2, mk_ref[0])
    row = lax.broadcasted_iota(jnp.int32, (X_ROWS, n_cols), 0)
    col = lax.broadcasted_iota(jnp.int32, (X_ROWS, n_cols), 1)
    own = (col & (X_ROWS - 1)) == row
    s2 = jnp.where(own, s2, 0.0)
    s4 = s2[:N_HEADS] + pltpu.roll(s2, n_cols - N_HEADS, 1)[N_HEADS:]
    own4 = own[:N_HEADS]
    s4 = jnp.where(own4, s4, NEG_INF)
    p4 = jnp.exp(s4 - jnp.max(s4, axis=1, keepdims=True))
    l4 = jnp.sum(p4, axis=1, keepdims=True)
    p4 = p4 / l4
    p2 = jnp.concatenate([p4, pltpu.roll(p4, N_HEADS, 1)], axis=0)
    o2 = _bdot(p2, mv_ref[0])
    o_ref[0] = jnp.concatenate([o2[c * N_HEADS + h:c * N_HEADS + h + 1, :]
                                for h in range(N_HEADS) for c in range(X_HALVES)], axis=1)


def _xattn_sample(q, mk, mv, base):
    DB = q.shape[0]
    n_rows = mk.shape[1]
    tok = pl.BlockSpec((1, 1, D_MODEL), lambda b: (b, 0, 0))
    mem = pl.BlockSpec((1, n_rows, LANES), lambda b: (base + b, 0, 0))
    out = pl.pallas_call(
        _xattn_sample_kernel,
        out_shape=jax.ShapeDtypeStruct((DB, 1, D_MODEL), F32),
        grid=(DB,),
        in_specs=[tok, mem, mem],
        out_specs=tok,
        compiler_params=_cparams(1),
        name="xattn_sample",
    )(q.reshape(DB, 1, D_MODEL), mk, mv)
    return out.reshape(DB, D_MODEL)


def _mem_rows(cache):
    n_l, DB, n_mem = cache.shape[:3]
    c = cache.reshape(n_l, DB, n_mem, N_HEADS, X_HALVES, LANES).transpose(0, 1, 2, 4, 3, 5)
    return c.reshape(n_l * DB, n_mem * X_ROWS, LANES)


def _ffn_sample_kernel(x_ref, up_ref, gate_ref, b0_ref, b1_ref, cw_ref, cb_ref, wd_ref, g_ref, b_ref, o_ref, acc_ref):
    j = pl.program_id(0)
    conv = (cb_ref[...] + cw_ref[0:1, :] * b0_ref[...] + cw_ref[1:2, :] * b1_ref[...]
            + cw_ref[2:3, :] * up_ref[...])
    y = _bdot(jax.nn.gelu(conv) * gate_ref[...], wd_ref[...])

    @pl.when(j == 0)
    def _():
        acc_ref[...] = y

    @pl.when(j > 0)
    def _():
        acc_ref[...] = acc_ref[...] + y

    @pl.when(j == pl.num_programs(0) - 1)
    def _():
        o_ref[...] = _layernorm(DN_ALPHA * x_ref[...] + acc_ref[...], g_ref[...], b_ref[...])


def _ffn_sample(x, ug, buf0, buf1, conv_w, conv_b, w_down, g, b, tf):
    DB = x.shape[0]
    d_ff = w_down.shape[0]
    nf = d_ff // tf
    ff = lambda off: pl.BlockSpec((DB, tf), lambda j: (0, off + j))
    full = lambda shape: pl.BlockSpec(shape, lambda j: (0,) * len(shape))
    return pl.pallas_call(
        _ffn_sample_kernel,
        out_shape=jax.ShapeDtypeStruct((DB, D_MODEL), F32),
        grid=(nf,),
        in_specs=[full((DB, D_MODEL)), ff(0), ff(nf), ff(0), ff(0),
                  pl.BlockSpec((3, tf), lambda j: (0, j)), pl.BlockSpec((1, tf), lambda j: (0, j)),
                  pl.BlockSpec((tf, D_MODEL), lambda j: (j, 0)), full((1, D_MODEL)), full((1, D_MODEL))],
        out_specs=full((DB, D_MODEL)),
        scratch_shapes=[pltpu.VMEM((DB, D_MODEL), F32)],
        compiler_params=_cparams(1),
        name="ffn_sample",
    )(x, ug, ug, buf0, buf1, conv_w, conv_b.reshape(1, -1), w_down, g.reshape(1, -1), b.reshape(1, -1))


def kernel(x_prompt, x_sample, cache_a_k, cache_a_v, state_b, state_c, state_d_c, state_d_n, state_d_m,
           cache_mem_k, cache_mem_v, state_conv, page_table, mem_prompt,
           ev_w_in, ev_w_out, ev_lam, ev_subln_g, ev_lb_logits, ev_b_norm_g,
           od_w_in, od_b_if, od_w_out, od_c_norm_g, od_d_norm_g,
           ln_g, ln_b, xa_wq, xa_wkv, xa_wo, ffn_w_up, ffn_conv_w, ffn_conv_b, ffn_w_down):
    B, T, _ = x_prompt.shape
    DB, t_s, _ = x_sample.shape
    assert t_s == 1, "the sample group is a single decoding step"
    assert T % CHUNK == 0
    n_pool, page = cache_a_k.shape[1], cache_a_k.shape[2]
    n_pages = page_table.shape[1]
    past = n_pages * page
    n_mem = mem_prompt.shape[1]
    d_ff = ffn_w_down.shape[1]
    tf = d_ff // 2
    N = B * T
    TM = 512

    pos_p = jnp.arange(T)
    pos_s = jnp.full((DB,), past, jnp.int32)
    lb_table = jnp.cumsum(jax.nn.softmax(ev_lb_logits.astype(F32), axis=0), axis=0)

    xp = x_prompt.reshape(N, D_MODEL)
    xs = x_sample.reshape(DB, D_MODEL)
    outs = {k: [] for k in ("ak_p", "av_p", "ak_s", "av_s", "sb_p", "sb_s", "sc_p", "sc_s", "dc_p", "dc_s",
                            "dn_p", "dn_s", "dm_p", "dm_s", "mk_p", "mv_p", "cv_p", "cv_s")}
    swap = lambda s: jnp.swapaxes(s, -1, -2)
    mem_k_rows, mem_v_rows = _mem_rows(cache_mem_k), _mem_rows(cache_mem_v)

    for l in range(DEPTH):
        j = l // 2
        if l % 2 == 0:
            lam_init = 0.8 - 0.6 * math.exp(-0.3 * l)
            w_in = ev_w_in[j].astype(BF16)
            w_out = ev_w_out[j].astype(BF16)
            h = _matmul(xp, w_in, 1024, 512, "ev_in_prompt")
            cos_t, sin_t = _rope_tables(pos_p, DH_A)
            h3 = h.reshape(B, T, -1)
            qt, kb, kt32, vt = _ev_prep_prompt(h3, cos_t, sin_t, TM)
            o_a = _dattn_prompt(qt, kb, vt, ev_lam[j], ev_subln_g[j], lam_init, 512)
            o_b, st = _hgrn_prompt(h3, lb_table[j], ev_b_norm_g[j], 512)
            outs["ak_p"].append(kt32.reshape(B, N_HEADS, 2, DH_A, T).transpose(0, 4, 1, 2, 3))
            outs["av_p"].append(h3[:, :, 2 * MIX_W:3 * MIX_W].reshape(B, T, N_HEADS, HEAD_W))
            outs["sb_p"].append(swap(st))
            xp = _proj_ln([o_a.reshape(N, MIX_W), o_b.reshape(N, MIX_W)], [w_out[:MIX_W], w_out[MIX_W:]],
                          xp, ln_g[l, 0], ln_b[l, 0], TM, "ev_out_prompt")
            hs = _matmul(xs, w_in, DB, 512, "ev_in_sample")
            cos_s, sin_s = _rope_tables(pos_s, DH_A)
            qs, ks32, kts32 = _ev_prep_sample(hs, cos_s, sin_s)
            vs32 = hs[:, 2 * MIX_W:3 * MIX_W]
            kt_pool = cache_a_k.transpose(0, 1, 3, 4, 5, 2).reshape(-1, MIX_W, page)
            v_pool = cache_a_v.reshape(-1, page * N_HEADS, HEAD_W)
            oa_s = _dattn_sample(qs, ks32, vs32, kt_pool, v_pool, j * n_pool, page_table,
                                 ev_lam[j], ev_subln_g[j], lam_init)
            ob_s, sb_s = _hgrn_step(hs, state_b[j], lb_table[j], ev_b_norm_g[j], 8)
            outs["ak_s"].append(kts32.reshape(N_HEADS, 2, DH_A, DB).transpose(3, 0, 1, 2)[:, None])
            outs["av_s"].append(vs32.reshape(DB, 1, N_HEADS, HEAD_W))
            outs["sb_s"].append(sb_s)
            xs = _proj_ln([oa_s, ob_s], [w_out[:MIX_W], w_out[MIX_W:]], xs, ln_g[l, 0], ln_b[l, 0], DB,
                          "ev_out_sample")
        else:
            n_main = 8 * MIX_W
            w_in = od_w_in[j][:, :n_main].astype(BF16)
            w_gate = jnp.pad(od_w_in[j][:, n_main:], ((0, 0), (0, LANES - 2 * N_HEADS))).astype(BF16)
            w_out = od_w_out[j].astype(BF16)
            h = _matmul(xp, w_in, 1024, 512, "od_in_prompt")
            gates = _matmul(xp, w_gate, 1024, LANES, "od_gates_prompt")[:, :2 * N_HEADS]
            cos_t, sin_t = _rope_tables(pos_p, HEAD_W)
            o_c, o_d, sct, dct, dn, dm = _od_prompt(h.reshape(B, T, -1), gates.reshape(B, T, -1), od_b_if[j],
                                                    cos_t, sin_t, od_c_norm_g[j], od_d_norm_g[j], 512)
            outs["sc_p"].append(swap(sct))
            outs["dc_p"].append(swap(dct))
            outs["dn_p"].append(dn[:, :N_HEADS, :])
            outs["dm_p"].append(dm[:, :N_HEADS, 0])
            xp = _proj_ln([o_c.reshape(N, MIX_W), o_d.reshape(N, MIX_W)], [w_out[:MIX_W], w_out[MIX_W:]],
                          xp, ln_g[l, 0], ln_b[l, 0], TM, "od_out_prompt")
            hs = _matmul(xs, w_in, DB, 512, "od_in_sample")
            gates_s = _matmul(xs, w_gate, DB, LANES, "od_gates_sample")
            cos_s, sin_s = _rope_tables(pos_s[:1], HEAD_W)
            oc_s, od_s, sc_s, dc_s, dn_s, dm_s = _od_step(hs, gates_s, od_b_if[j], cos_s, sin_s,
                                                          state_c[j], state_d_c[j], state_d_n[j], state_d_m[j],
                                                          od_c_norm_g[j], od_d_norm_g[j], 8)
            outs["sc_s"].append(sc_s)
            outs["dc_s"].append(dc_s)
            outs["dn_s"].append(dn_s)
            outs["dm_s"].append(dm_s)
            xs = _proj_ln([oc_s, od_s], [w_out[:MIX_W], w_out[MIX_W:]], xs, ln_g[l, 0], ln_b[l, 0], DB,
                          "od_out_sample")

        wq = xa_wq[l].astype(BF16)
        wo = xa_wo[l].astype(BF16)
        mkv = _matmul(mem_prompt.reshape(B * n_mem, D_MODEL), xa_wkv[l].astype(BF16), 512, 512, "mem_kv")
        mk, mv = mkv[:, :D_MODEL], mkv[:, D_MODEL:]
        outs["mk_p"].append(mk.reshape(B, n_mem, N_HEADS, DH_X))
        outs["mv_p"].append(mv.reshape(B, n_mem, N_HEADS, DH_X))
        xp = _xattn_prompt(xp, wq, mk.astype(BF16).reshape(B, n_mem, D_MODEL),
                           mv.astype(BF16).reshape(B, n_mem, D_MODEL), wo, ln_g[l, 1], ln_b[l, 1], T, TM)
        q_s = _matmul(xs, wq, DB, 512, "xattn_q_sample")
        xo_s = _xattn_sample(q_s, mem_k_rows, mem_v_rows, l * DB)
        xs = _proj_ln([xo_s], [wo], xs, ln_g[l, 1], ln_b[l, 1], DB, "xattn_out_sample")

        w_up = ffn_w_up[l].astype(BF16)
        w_down = ffn_w_down[l].astype(BF16)
        tm_f = min(TM, T)
        xp, tails = _ffn_prompt(xp, w_up, ffn_conv_w[l], ffn_conv_b[l], w_down, ln_g[l, 2], ln_b[l, 2], T, TM, tf)
        tails = tails.reshape(B, T // tm_f, SUBLANES, d_ff)
        outs["cv_p"].append(tails[:, -1, SUBLANES - 2:, :])
        ug_s = _matmul(xs, w_up, DB, tf, "ffn_up_sample")
        buf = state_conv[l]
        xs = _ffn_sample(xs, ug_s, buf[:, 0, :], buf[:, 1, :], ffn_conv_w[l], ffn_conv_b[l], w_down,
                         ln_g[l, 2], ln_b[l, 2], tf)
        outs["cv_s"].append(jnp.stack([buf[:, 1, :], ug_s[:, :d_ff]], axis=1))

    st = lambda k: jnp.stack(outs[k])
    return (xp.reshape(B, T, D_MODEL), xs.reshape(DB, 1, D_MODEL),
            st("ak_p"), st("av_p"), st("ak_s"), st("av_s"), st("sb_p"), st("sb_s"),
            st("sc_p"), st("sc_s"), st("dc_p"), st("dc_s"), st("dn_p"), st("dn_s"), st("dm_p"), st("dm_s"),
            st("mk_p"), st("mv_p"), st("cv_p"), st("cv_s"))
```

```python
import functools
import math

import numpy as np
import jax
import jax.numpy as jnp
from jax import lax
from jax.experimental import pallas as pl
from jax.experimental.pallas import tpu as pltpu

F32 = jnp.float32
BF16 = jnp.bfloat16

D_MODEL = 1024
MIX_W = D_MODEL // 2
N_HEADS = 4
HEAD_W = MIX_W // N_HEADS
DH_A = HEAD_W // 2
DH_X = D_MODEL // N_HEADS
CHUNK = 64
ROPE_THETA = 10000.0
NORM_EPS = 1e-5
DEPTH = 2
DN_ALPHA = (2.0 * DEPTH) ** 0.25
LANES = 128
SUBLANES = 8
VMEM_LIMIT = 56 * 1024 * 1024
NEG_INF = float("-inf")

NT_DIMS = (((1,), (1,)), ((), ()))
TN_DIMS = (((0,), (0,)), ((), ()))


def _cparams(n_axes, vmem=VMEM_LIMIT):
    return pltpu.CompilerParams(dimension_semantics=("arbitrary",) * n_axes, vmem_limit_bytes=vmem)


def _bdot(a, b):
    return jnp.dot(a.astype(BF16), b.astype(BF16), preferred_element_type=F32)


def _bdot_nt(a, b):
    return lax.dot_general(a.astype(BF16), b.astype(BF16), NT_DIMS, preferred_element_type=F32)


def _bdot_tn(a, b):
    return lax.dot_general(a.astype(BF16), b.astype(BF16), TN_DIMS, preferred_element_type=F32)


def _split3(x):
    p1 = x.astype(BF16)
    r1 = x - p1.astype(F32)
    p2 = r1.astype(BF16)
    p3 = (r1 - p2.astype(F32)).astype(BF16)
    return p1, p2, p3


def _cumsum_rows(tri, x):
    p1, p2, p3 = _split3(x)
    d = functools.partial(jnp.dot, preferred_element_type=F32)
    return d(tri, p1) + d(tri, p2) + d(tri, p3)


def _cumsum_lanes(x, triu):
    p1, p2, p3 = _split3(x)
    d = functools.partial(jnp.dot, preferred_element_type=F32)
    return d(p1, triu) + d(p2, triu) + d(p3, triu)


def _tri(L, lower):
    r = lax.broadcasted_iota(jnp.int32, (L, L), 0)
    c = lax.broadcasted_iota(jnp.int32, (L, L), 1)
    return (r >= c) if lower else (r <= c)


def _sigmoid(x):
    return 1.0 / (1.0 + jnp.exp(-x))


def _silu(x):
    return x * _sigmoid(x)


def _log_sigmoid(x):
    return jnp.minimum(x, 0.0) - jnp.log(1.0 + jnp.exp(-jnp.abs(x)))


def _layernorm(z, g, b):
    mu = jnp.mean(z, -1, keepdims=True)
    zc = z - mu
    var = jnp.mean(zc * zc, -1, keepdims=True)
    return zc * lax.rsqrt(var + NORM_EPS) * g + b


def _rmsnorm(x, g):
    return x * lax.rsqrt(jnp.mean(x * x, -1, keepdims=True) + NORM_EPS) * g


def _groupnorm(x, g):
    mu = jnp.mean(x, -1, keepdims=True)
    xc = x - mu
    var = jnp.mean(xc * xc, -1, keepdims=True)
    return xc * lax.rsqrt(var + NORM_EPS) * g


def _rope(x, cos, sin, half):
    outs = []
    for c in range(x.shape[1] // LANES):
        xc = x[:, c * LANES:(c + 1) * LANES]
        if 2 * half == LANES:
            sw = pltpu.roll(xc, half, 1)
        else:
            lane = lax.broadcasted_iota(jnp.int32, xc.shape, 1)
            first = (lane & (2 * half - 1)) < half
            sw = jnp.where(first, pltpu.roll(xc, LANES - half, 1), pltpu.roll(xc, half, 1))
        outs.append(xc * cos + sw * sin)
    return outs[0] if len(outs) == 1 else jnp.concatenate(outs, axis=1)


def _rope_tables(pos, d):
    inv = ROPE_THETA ** (-jnp.arange(0, d // 2, dtype=F32) * 2.0 / d)
    ang = pos.astype(F32)[:, None] * inv[None, :]
    cos, sin = jnp.cos(ang), jnp.sin(ang)
    reps = LANES // d
    cos_t = jnp.tile(jnp.concatenate([cos, cos], -1), (1, reps))
    sin_t = jnp.tile(jnp.concatenate([-sin, sin], -1), (1, reps))
    return cos_t, sin_t


def _col(row):
    return jnp.broadcast_to(row, (LANES, LANES)).T


def _mm_kernel(x_ref, w_ref, o_ref, xb_ref):
    @pl.when(pl.program_id(1) == 0)
    def _():
        xb_ref[...] = x_ref[...].astype(BF16)

    o_ref[...] = jnp.dot(xb_ref[...], w_ref[...], preferred_element_type=F32).astype(o_ref.dtype)


def _matmul(x, w, tm, tn, name, out_dtype=F32):
    M, K = x.shape
    N = w.shape[1]
    tm, tn = min(tm, M), min(tn, N)
    return pl.pallas_call(
        _mm_kernel,
        out_shape=jax.ShapeDtypeStruct((M, N), out_dtype),
        grid=(M // tm, N // tn),
        in_specs=[pl.BlockSpec((tm, K), lambda i, j: (i, 0)),
                  pl.BlockSpec((K, tn), lambda i, j: (0, j))],
        out_specs=pl.BlockSpec((tm, tn), lambda i, j: (i, j)),
        scratch_shapes=[pltpu.VMEM((tm, K), BF16)],
        compiler_params=_cparams(2),
        name=name,
    )(x, w)


def _proj_ln_kernel(*refs, n_in):
    a_refs, w_refs = refs[:n_in], refs[n_in:2 * n_in]
    x_ref, g_ref, b_ref, o_ref = refs[2 * n_in:]
    y = _bdot(a_refs[0][...], w_refs[0][...])
    for a_ref, w_ref in zip(a_refs[1:], w_refs[1:]):
        y = y + _bdot(a_ref[...], w_ref[...])
    o_ref[...] = _layernorm(DN_ALPHA * x_ref[...] + y, g_ref[...], b_ref[...])


def _proj_ln(acts, weights, x, g, b, tm, name):
    M = x.shape[0]
    tm = min(tm, M)
    n_in = len(acts)
    in_specs = [pl.BlockSpec((tm, a.shape[1]), lambda i: (i, 0)) for a in acts]
    in_specs += [pl.BlockSpec(w.shape, lambda i: (0, 0)) for w in weights]
    in_specs += [pl.BlockSpec((tm, D_MODEL), lambda i: (i, 0)),
                 pl.BlockSpec((1, D_MODEL), lambda i: (0, 0)),
                 pl.BlockSpec((1, D_MODEL), lambda i: (0, 0))]
    return pl.pallas_call(
        functools.partial(_proj_ln_kernel, n_in=n_in),
        out_shape=jax.ShapeDtypeStruct((M, D_MODEL), F32),
        grid=(M // tm,),
        in_specs=in_specs,
        out_specs=pl.BlockSpec((tm, D_MODEL), lambda i: (i, 0)),
        compiler_params=_cparams(1),
        name=name,
    )(*acts, *weights, x, g.reshape(1, -1), b.reshape(1, -1))


Q_SCALE = DH_A ** -0.5 * math.log2(math.e)


def _ev_prep_prompt_kernel(qk_ref, v_ref, cos_ref, sin_ref, qt_ref, kb_ref, kt_ref, vt_ref):
    cos, sin = cos_ref[...], sin_ref[...]
    q = _rope(qk_ref[0, :, :MIX_W], cos, sin, DH_A // 2)
    k = _rope(qk_ref[0, :, MIX_W:], cos, sin, DH_A // 2)
    qt_ref[0] = (q * Q_SCALE).T.astype(BF16)
    kb_ref[0] = k.astype(BF16)
    kt_ref[0] = k.T
    vt_ref[0] = v_ref[0].T.astype(BF16)


def _ev_prep_prompt(h3, cos_t, sin_t, tm):
    B, T, _ = h3.shape
    tm = min(tm, T)
    tab = pl.BlockSpec((tm, LANES), lambda b, i: (i, 0))
    tr = pl.BlockSpec((1, MIX_W, tm), lambda b, i: (b, 0, i))
    tshape = lambda dt: jax.ShapeDtypeStruct((B, MIX_W, T), dt)
    return pl.pallas_call(
        _ev_prep_prompt_kernel,
        out_shape=(tshape(BF16), jax.ShapeDtypeStruct((B, T, MIX_W), BF16), tshape(F32), tshape(BF16)),
        grid=(B, T // tm),
        in_specs=[pl.BlockSpec((1, tm, 2 * MIX_W), lambda b, i: (b, i, 0)),
                  pl.BlockSpec((1, tm, MIX_W), lambda b, i: (b, i, 2)), tab, tab],
        out_specs=(tr, pl.BlockSpec((1, tm, MIX_W), lambda b, i: (b, i, 0)), tr, tr),
        compiler_params=_cparams(2),
        name="ev_prep_prompt",
    )(h3, h3, cos_t, sin_t)


def _ev_prep_sample_kernel(qk_ref, cos_ref, sin_ref, q_ref, k_ref, kt_ref):
    cos, sin = cos_ref[...], sin_ref[...]
    k = _rope(qk_ref[:, MIX_W:], cos, sin, DH_A // 2)
    q_ref[...] = _rope(qk_ref[:, :MIX_W], cos, sin, DH_A // 2) * Q_SCALE
    k_ref[...] = k
    kt_ref[...] = k.T


def _ev_prep_sample(h, cos_t, sin_t):
    DB = h.shape[0]
    full = lambda shape: pl.BlockSpec(shape, lambda i: (0,) * len(shape))
    return pl.pallas_call(
        _ev_prep_sample_kernel,
        out_shape=(jax.ShapeDtypeStruct((DB, MIX_W), F32), jax.ShapeDtypeStruct((DB, MIX_W), F32),
                   jax.ShapeDtypeStruct((MIX_W, DB), F32)),
        grid=(1,),
        in_specs=[full((DB, 2 * MIX_W)), full((DB, LANES)), full((DB, LANES))],
        out_specs=(full((DB, MIX_W)), full((DB, MIX_W)), full((MIX_W, DB))),
        compiler_params=_cparams(1),
        name="ev_prep_sample",
    )(h, cos_t, sin_t)


def _lambda(lam_ref, lam_init):
    lp = lam_ref[...]
    s01 = jnp.sum(lp[0:1] * lp[1:2], axis=1, keepdims=True)
    s23 = jnp.sum(lp[2:3] * lp[3:4], axis=1, keepdims=True)
    return jnp.exp(s01) - jnp.exp(s23) + lam_init


def _dattn_kernel(lam_ref, g_ref, qt_ref, k_ref, vt_ref, o_ref, m_ref, l_ref, acc_ref, *, tq, lam_init):
    i = pl.program_id(2)
    qt = qt_ref[0]
    sub = lax.broadcasted_iota(jnp.int32, qt.shape, 0)
    zero = jnp.zeros_like(qt)
    q_maps = [jnp.where(sub < DH_A, qt, zero), jnp.where(sub >= DH_A, qt, zero)]
    maps = range(2)
    m_ref[...] = jnp.full(m_ref.shape, NEG_INF, F32)
    l_ref[...] = jnp.zeros(l_ref.shape, F32)
    acc_ref[...] = jnp.zeros(acc_ref.shape, F32)

    def steps(js, masked):
        ks, vts, ss = [], [], []
        for j in js:
            start = pl.multiple_of(j * tq, tq)
            ks.append(k_ref[0, pl.ds(start, tq), :])
            vts.append(vt_ref[0, :, pl.ds(start, tq)])
        for kj in ks:
            ss.append([jnp.dot(kj, q_maps[c], preferred_element_type=F32) for c in maps])
        for n in range(len(js)):
            for c in maps:
                s = ss[n][c]
                if masked:
                    r = lax.broadcasted_iota(jnp.int32, s.shape, 0)
                    col = lax.broadcasted_iota(jnp.int32, s.shape, 1)
                    s = jnp.where(r <= col, s, NEG_INF)
                m_prev = m_ref[c]
                m_new = jnp.maximum(m_prev, jnp.max(s, axis=0, keepdims=True))
                alpha = jnp.exp2(m_prev - m_new)
                p = jnp.exp2(s - m_new)
                l_ref[c] = alpha * l_ref[c] + jnp.sum(p, axis=0, keepdims=True)
                acc_ref[c] = alpha * acc_ref[c] + jnp.dot(vts[n], p.astype(BF16), preferred_element_type=F32)
                m_ref[c] = m_new

    def body(jj, carry):
        steps([2 * jj, 2 * jj + 1], False)
        return carry

    lax.fori_loop(0, lax.shift_right_logical(i, 1), body, 0)

    @pl.when((i & 1) == 1)
    def _():
        steps([i - 1], False)

    steps([i], True)
    lam = _lambda(lam_ref, lam_init)
    d = (acc_ref[0] / l_ref[0] - lam * (acc_ref[1] / l_ref[1])).T
    o_ref[0] = _rmsnorm(d, g_ref[...]) * (1.0 - lam_init)


def _dattn_prompt(qt, kb, vt, lam_p, subln_g, lam_init, tq):
    B, T, _ = kb.shape
    tq = min(tq, T)
    return pl.pallas_call(
        functools.partial(_dattn_kernel, tq=tq, lam_init=lam_init),
        out_shape=jax.ShapeDtypeStruct((B, T, MIX_W), F32),
        grid=(B, N_HEADS, T // tq),
        in_specs=[pl.BlockSpec((4, DH_A), lambda b, h, i: (0, 0)),
                  pl.BlockSpec((1, HEAD_W), lambda b, h, i: (0, 0)),
                  pl.BlockSpec((1, HEAD_W, tq), lambda b, h, i: (b, h, i)),
                  pl.BlockSpec((1, T, HEAD_W), lambda b, h, i: (b, 0, h)),
                  pl.BlockSpec((1, HEAD_W, T), lambda b, h, i: (b, h, 0))],
        out_specs=pl.BlockSpec((1, tq, HEAD_W), lambda b, h, i: (b, i, h)),
        scratch_shapes=[pltpu.VMEM((2, 1, tq), F32), pltpu.VMEM((2, 1, tq), F32),
                        pltpu.VMEM((2, HEAD_W, tq), F32)],
        compiler_params=_cparams(3),
        name="dattn_prompt",
    )(lam_p, subln_g.reshape(1, -1), qt, kb, vt)


def _head(x, h):
    return x[:, h * HEAD_W:(h + 1) * HEAD_W]


def _gla_chunk(q_in, k_in, k_end, v, dec, st_ref, tril_mask):
    q_in, k_in, k_end, v = (x.astype(BF16) for x in (q_in, k_in, k_end, v))
    heads = range(N_HEADS)
    sts = [st_ref[h] for h in heads]
    inter = [_bdot_nt(_head(q_in, h), sts[h]) for h in heads]
    upd = [_bdot_tn(_head(v, h), _head(k_end, h)) for h in heads]
    attn = [jnp.where(tril_mask, _bdot_nt(_head(q_in, h), _head(k_in, h)), 0.0) for h in heads]
    intra = [_bdot(attn[h], _head(v, h)) for h in heads]
    for h in heads:
        st_ref[h] = sts[h] * (dec[h] if isinstance(dec, (list, tuple)) else _head(dec, h)) + upd[h]
    return [intra[h] + inter[h] for h in heads]


def _hgrn_kernel(q_ref, f_ref, i_ref, g_ref, lb_ref, ng_ref, o_ref, s_ref, st_ref, *, tt, L):
    t = pl.program_id(1)

    @pl.when(t == 0)
    def _():
        st_ref[...] = jnp.zeros(st_ref.shape, F32)

    tril_mask = _tri(L, True)
    tril = tril_mask.astype(BF16)

    def body(c, carry):
        rows = pl.ds(pl.multiple_of(c * L, L), L)
        lb = lb_ref[...]
        f = lb + (1.0 - lb) * _sigmoid(f_ref[0, rows, :])
        b = _cumsum_rows(tril, jnp.log(f))
        b_end = b[L - 1:L, :]
        k = 1.0 - f
        q_in = _silu(q_ref[0, rows, :]) * jnp.exp(b)
        o = _gla_chunk(q_in, k * jnp.exp(-b), k * jnp.exp(b_end - b), i_ref[0, rows, :], jnp.exp(b_end),
                       st_ref, tril_mask)
        o = jnp.concatenate([_rmsnorm(o_h, ng_ref[...]) for o_h in o], axis=1)
        o_ref[0, rows, :] = o * _silu(g_ref[0, rows, :])
        return carry

    lax.fori_loop(0, tt // L, body, 0)

    @pl.when(t == pl.num_programs(1) - 1)
    def _():
        s_ref[0] = st_ref[...]


def _hgrn_prompt(h3, lb, norm_g, tt):
    B, T, _ = h3.shape
    tt = min(tt, T)
    L = math.gcd(T, CHUNK)
    col = lambda c: pl.BlockSpec((1, tt, MIX_W), lambda b, t: (b, t, c))
    return pl.pallas_call(
        functools.partial(_hgrn_kernel, tt=tt, L=L),
        out_shape=(jax.ShapeDtypeStruct((B, T, MIX_W), F32),
                   jax.ShapeDtypeStruct((B, N_HEADS, HEAD_W, HEAD_W), F32)),
        grid=(B, T // tt),
        in_specs=[col(3), col(4), col(5), col(6),
                  pl.BlockSpec((1, MIX_W), lambda b, t: (0, 0)),
                  pl.BlockSpec((1, HEAD_W), lambda b, t: (0, 0))],
        out_specs=(pl.BlockSpec((1, tt, MIX_W), lambda b, t: (b, t, 0)),
                   pl.BlockSpec((1, N_HEADS, HEAD_W, HEAD_W), lambda b, t: (b, 0, 0, 0))),
        scratch_shapes=[pltpu.VMEM((N_HEADS, HEAD_W, HEAD_W), F32)],
        compiler_params=_cparams(2),
        name="hgrn_prompt",
    )(h3, h3, h3, h3, lb.reshape(1, -1), norm_g.reshape(1, -1))


def _log_gamma(h):
    return float(np.log(1.0 - 2.0 ** (-5.0 - h)))


def _od_kernel(cq_ref, ck_ref, cv_ref, cg_ref, dq_ref, dk_ref, dv_ref, do_ref, gc_ref, gr_ref,
               bc_ref, br_ref, cos_ref, sin_ref, cng_ref, dng_ref,
               oc_ref, od_ref, sc_ref, dc_ref, dn_ref, dm_ref,
               st_ref, ct_ref, n_ref, m_ref, *, tt, L):
    t = pl.program_id(1)

    @pl.when(t == 0)
    def _():
        st_ref[...] = jnp.zeros(st_ref.shape, F32)
        ct_ref[...] = jnp.zeros(ct_ref.shape, F32)
        n_ref[...] = jnp.zeros(n_ref.shape, F32)
        m_ref[...] = jnp.zeros(m_ref.shape, F32)

    tril_mask = _tri(L, True)
    tril = tril_mask.astype(BF16)
    triu = _tri(L, False).astype(BF16)
    scale = HEAD_W ** -0.5
    heads = range(N_HEADS)
    pos1 = (lax.broadcasted_iota(jnp.int32, (L, HEAD_W), 0) + 1).astype(F32)
    ret_b = jnp.concatenate([pos1 * _log_gamma(h) for h in heads], axis=1)
    ret_b_end = jnp.concatenate([jnp.full((L, HEAD_W), L * _log_gamma(h), F32) for h in heads], axis=1)
    ret_q_dec, ret_k_dec, ret_k_end = jnp.exp(ret_b), jnp.exp(-ret_b), jnp.exp(ret_b_end - ret_b)
    ret_dec = [math.exp(L * _log_gamma(h)) for h in heads]

    def body(c, carry):
        rows = pl.ds(pl.multiple_of(c * L, L), L)
        cos, sin = cos_ref[rows, :], sin_ref[rows, :]
        q = _rope(cq_ref[0, rows, :], cos, sin, HEAD_W // 2)
        k = _rope(ck_ref[0, rows, :], cos, sin, HEAD_W // 2) * scale
        o = _gla_chunk(q * ret_q_dec, k * ret_k_dec, k * ret_k_end, cv_ref[0, rows, :], ret_dec, st_ref, tril_mask)
        o = jnp.concatenate([_groupnorm(o_h, cng_ref[...]) for o_h in o], axis=1)
        oc_ref[0, rows, :] = o * _silu(cg_ref[0, rows, :])
        g_c = gc_ref[0, rows, :] + bc_ref[...]
        g_r = gr_ref[0, c] + br_ref[...]
        lf_c = _log_sigmoid(g_c)
        b_c = _cumsum_rows(tril, jnp.concatenate(
            [jnp.broadcast_to(lf_c[:, N_HEADS + h:N_HEADS + h + 1], (L, HEAD_W)) for h in heads], axis=1))
        b_r = _cumsum_lanes(_log_sigmoid(g_r), triu)
        dq = (dq_ref[0, rows, :] * scale)
        dk = dk_ref[0, rows, :]
        dqb, dkb, dvb = dq.astype(BF16), dk.astype(BF16), dv_ref[0, rows, :].astype(BF16)
        cts = [ct_ref[h] for h in heads]
        q_c = [_bdot_nt(_head(dqb, h), cts[h]) for h in heads]
        q_k = [_bdot_nt(_head(dqb, h), _head(dkb, h)) for h in heads]
        m_prev = [m_ref[h:h + 1, 0:1] for h in heads]
        bc1, m_t, w_mat = [], [], []
        for h in heads:
            bc = _head(b_c, h)[:, :L]
            dm = jnp.where(tril_mask, bc - b_r[N_HEADS + h:N_HEADS + h + 1, :] + g_r[h:h + 1, :], NEG_INF)
            bc1.append(bc[:, 0:1])
            m_t.append(jnp.maximum(bc1[h] + m_prev[h], jnp.max(dm, axis=1, keepdims=True)))
            w_mat.append(jnp.exp(dm - m_t[h]) * q_k[h])
        w_v = [_bdot(w_mat[h], _head(dvb, h)) for h in heads]
        kws, c_scales = [], []
        for h in heads:
            m_new = m_t[h][L - 1:L, :]
            b_last = bc1[h][L - 1:L, :]
            c_scales.append(jnp.exp(b_last + m_prev[h] - m_new))
            kws.append(_head(dk, h) * jnp.exp(b_last - bc1[h] + g_c[:, h:h + 1] - m_new))
            m_ref[h:h + 1, :] = jnp.broadcast_to(m_new, (1, LANES))
        c_upd = [_bdot_tn(_head(dvb, h), kws[h]) for h in heads]
        hs = []
        for h in heads:
            n_row = n_ref[h:h + 1, :]
            inter = jnp.exp(bc1[h] + m_prev[h] - m_t[h])
            num = inter * q_c[h] + w_v[h]
            den = (inter * jnp.sum(_head(dq, h) * n_row, axis=1, keepdims=True)
                   + jnp.sum(w_mat[h], axis=1, keepdims=True))
            hs.append(_groupnorm(num / jnp.maximum(jnp.abs(den), jnp.exp(-m_t[h])), dng_ref[...]))
            ct_ref[h] = c_scales[h] * cts[h] + c_upd[h]
            n_ref[h:h + 1, :] = c_scales[h] * n_row + jnp.sum(kws[h], axis=0, keepdims=True)
        od_ref[0, rows, :] = jnp.concatenate(hs, axis=1) * _sigmoid(do_ref[0, rows, :])
        return carry

    lax.fori_loop(0, tt // L, body, 0)

    @pl.when(t == pl.num_programs(1) - 1)
    def _():
        sc_ref[0] = st_ref[...]
        dc_ref[0] = ct_ref[...]
        dn_ref[0] = n_ref[...]
        dm_ref[0] = m_ref[...]


def _od_prompt(h3, gates, b_if, cos_t, sin_t, c_norm_g, d_norm_g, tt):
    B, T, _ = h3.shape
    tt = min(tt, T)
    L = math.gcd(T, CHUNK)
    gates_r = gates.reshape(B, T // L, L, 2 * N_HEADS).transpose(0, 1, 3, 2)
    col = lambda c: pl.BlockSpec((1, tt, MIX_W), lambda b, t: (b, t, c))
    tab = pl.BlockSpec((tt, LANES), lambda b, t: (t, 0))
    vec = pl.BlockSpec((1, HEAD_W), lambda b, t: (0, 0))
    mat_state = pl.BlockSpec((1, N_HEADS, HEAD_W, HEAD_W), lambda b, t: (b, 0, 0, 0))
    row_state = pl.BlockSpec((1, SUBLANES, LANES), lambda b, t: (b, 0, 0))
    out_blk = pl.BlockSpec((1, tt, MIX_W), lambda b, t: (b, t, 0))
    return pl.pallas_call(
        functools.partial(_od_kernel, tt=tt, L=L),
        out_shape=(jax.ShapeDtypeStruct((B, T, MIX_W), F32), jax.ShapeDtypeStruct((B, T, MIX_W), F32),
                   jax.ShapeDtypeStruct((B, N_HEADS, HEAD_W, HEAD_W), F32),
                   jax.ShapeDtypeStruct((B, N_HEADS, HEAD_W, HEAD_W), F32),
                   jax.ShapeDtypeStruct((B, SUBLANES, LANES), F32),
                   jax.ShapeDtypeStruct((B, SUBLANES, LANES), F32)),
        grid=(B, T // tt),
        in_specs=[col(0), col(1), col(2), col(3), col(4), col(5), col(6), col(7),
                  pl.BlockSpec((1, tt, 2 * N_HEADS), lambda b, t: (b, t, 0)),
                  pl.BlockSpec((1, tt // L, 2 * N_HEADS, L), lambda b, t: (b, t, 0, 0)),
                  pl.BlockSpec((1, 2 * N_HEADS), lambda b, t: (0, 0)),
                  pl.BlockSpec((2 * N_HEADS, 1), lambda b, t: (0, 0)),
                  tab, tab, vec, vec],
        out_specs=(out_blk, out_blk, mat_state, mat_state, row_state, row_state),
        scratch_shapes=[pltpu.VMEM((N_HEADS, HEAD_W, HEAD_W), F32), pltpu.VMEM((N_HEADS, HEAD_W, HEAD_W), F32),
                        pltpu.VMEM((SUBLANES, LANES), F32), pltpu.VMEM((SUBLANES, LANES), F32)],
        compiler_params=_cparams(2),
        name="od_prompt",
    )(h3, h3, h3, h3, h3, h3, h3, h3, gates, gates_r, b_if.reshape(1, -1), b_if.reshape(-1, 1),
      cos_t, sin_t, c_norm_g.reshape(1, -1), d_norm_g.reshape(1, -1))


def _xattn_kernel(x_ref, wq_ref, mk_ref, mv_ref, wo_ref, g_ref, b_ref, o_ref):
    x = x_ref[...]
    q = _bdot(x, wq_ref[...])
    qb = (q * (DH_X ** -0.5)).astype(BF16)
    outs = []
    for h in range(N_HEADS):
        cols = slice(h * DH_X, (h + 1) * DH_X)
        s = lax.dot_general(qb[:, cols], mk_ref[0, :, cols], NT_DIMS, preferred_element_type=F32)
        p = jnp.exp(s - jnp.max(s, axis=1, keepdims=True))
        l = jnp.sum(p, axis=1, keepdims=True)
        outs.append((jnp.dot(p.astype(BF16), mv_ref[0, :, cols], preferred_element_type=F32) / l).astype(BF16))
    y = jnp.dot(jnp.concatenate(outs, axis=1), wo_ref[...], preferred_element_type=F32)
    o_ref[...] = _layernorm(DN_ALPHA * x + y, g_ref[...], b_ref[...])


def _xattn_prompt(x, wq, mk, mv, wo, g, b, T, tm):
    M = x.shape[0]
    tm = min(tm, T)
    n_mem = mk.shape[1]
    per_b = T // tm
    full = lambda shape: pl.BlockSpec(shape, lambda i: (0,) * len(shape))
    mem = pl.BlockSpec((1, n_mem, D_MODEL), lambda i: (i // per_b, 0, 0))
    return pl.pallas_call(
        _xattn_kernel,
        out_shape=jax.ShapeDtypeStruct((M, D_MODEL), F32),
        grid=(M // tm,),
        in_specs=[pl.BlockSpec((tm, D_MODEL), lambda i: (i, 0)), full((D_MODEL, D_MODEL)), mem, mem,
                  full((D_MODEL, D_MODEL)), full((1, D_MODEL)), full((1, D_MODEL))],
        out_specs=pl.BlockSpec((tm, D_MODEL), lambda i: (i, 0)),
        compiler_params=_cparams(1),
        name="xattn_prompt",
    )(x, wq, mk, mv, wo, g.reshape(1, -1), b.reshape(1, -1))


def _ffn_kernel(x_ref, wu_ref, wg_ref, cw_ref, cb_ref, wd_ref, g_ref, b_ref, o_ref, tail_ref,
                xb_ref, acc_ref, stage_ref, carry_ref, *, tm, per_b):
    i, j = pl.program_id(0), pl.program_id(1)

    @pl.when(j == 0)
    def _():
        xb_ref[...] = x_ref[...].astype(BF16)

    @pl.when(i % per_b == 0)
    def _():
        carry_ref[j] = jnp.zeros(carry_ref.shape[1:], F32)

    xb = xb_ref[...]
    stage_ref[0:SUBLANES, :] = carry_ref[j]
    stage_ref[SUBLANES:, :] = jnp.dot(xb, wu_ref[...], preferred_element_type=F32)
    last = stage_ref[tm:tm + SUBLANES, :]
    carry_ref[j] = last
    tail_ref[0] = last
    conv = (cb_ref[...] + cw_ref[0:1, :] * stage_ref[SUBLANES - 2:SUBLANES - 2 + tm, :]
            + cw_ref[1:2, :] * stage_ref[SUBLANES - 1:SUBLANES - 1 + tm, :]
            + cw_ref[2:3, :] * stage_ref[SUBLANES:, :])
    gate = jnp.dot(xb, wg_ref[...], preferred_element_type=F32)
    y = _bdot(jax.nn.gelu(conv) * gate, wd_ref[...])

    @pl.when(j == 0)
    def _():
        acc_ref[...] = y

    @pl.when(j > 0)
    def _():
        acc_ref[...] = acc_ref[...] + y

    @pl.when(j == pl.num_programs(1) - 1)
    def _():
        o_ref[...] = _layernorm(DN_ALPHA * x_ref[...] + acc_ref[...], g_ref[...], b_ref[...])


def _ffn_prompt(x, w_up, conv_w, conv_b, w_down, g, b, T, tm, tf):
    M = x.shape[0]
    d_ff = w_down.shape[0]
    tm = min(tm, T)
    nf = d_ff // tf
    per_b = T // tm
    return pl.pallas_call(
        functools.partial(_ffn_kernel, tm=tm, per_b=per_b),
        out_shape=(jax.ShapeDtypeStruct((M, D_MODEL), F32),
                   jax.ShapeDtypeStruct((M // tm, SUBLANES, d_ff), F32)),
        grid=(M // tm, nf),
        in_specs=[pl.BlockSpec((tm, D_MODEL), lambda i, j: (i, 0)),
                  pl.BlockSpec((D_MODEL, tf), lambda i, j: (0, j)),
                  pl.BlockSpec((D_MODEL, tf), lambda i, j: (0, nf + j)),
                  pl.BlockSpec((3, tf), lambda i, j: (0, j)),
                  pl.BlockSpec((1, tf), lambda i, j: (0, j)),
                  pl.BlockSpec((tf, D_MODEL), lambda i, j: (j, 0)),
                  pl.BlockSpec((1, D_MODEL), lambda i, j: (0, 0)),
                  pl.BlockSpec((1, D_MODEL), lambda i, j: (0, 0))],
        out_specs=(pl.BlockSpec((tm, D_MODEL), lambda i, j: (i, 0)),
                   pl.BlockSpec((1, SUBLANES, tf), lambda i, j: (i, 0, j))),
        scratch_shapes=[pltpu.VMEM((tm, D_MODEL), BF16), pltpu.VMEM((tm, D_MODEL), F32),
                        pltpu.VMEM((tm + SUBLANES, tf), F32), pltpu.VMEM((nf, SUBLANES, tf), F32)],
        compiler_params=_cparams(2),
        name="ffn_prompt",
    )(x, w_up, w_up, conv_w, conv_b.reshape(1, -1), w_down, g.reshape(1, -1), b.reshape(1, -1))


def _dattn_sample_kernel(*refs, n_pages, page, lam_init):
    pt_ref = refs[0]
    lam_ref, g_ref, q_ref, kn_ref, vn_ref = refs[1:6]
    k_refs = refs[6:6 + n_pages]
    v_refs = refs[6 + n_pages:6 + 2 * n_pages]
    o_ref = refs[6 + 2 * n_pages]
    del pt_ref
    n_rows = 2 * N_HEADS
    q = q_ref[0]
    lane = lax.broadcasted_iota(jnp.int32, (n_rows, MIX_W), 1)
    row = lax.broadcasted_iota(jnp.int32, (n_rows, MIX_W), 0)
    qbd = jnp.where(lax.shift_right_logical(lane, 6) == row, jnp.broadcast_to(q, (n_rows, MIX_W)), 0.0)
    s_new = jnp.sum(qbd * kn_ref[0], axis=1, keepdims=True)
    scores = [_bdot(qbd, k_refs[p][0]) for p in range(n_pages)]
    m = s_new
    for s in scores:
        m = jnp.maximum(m, jnp.max(s, axis=1, keepdims=True))
    p_new = jnp.exp2(s_new - m)
    l = p_new
    vn = vn_ref[0]
    accs = [p_new * vn[:, h * HEAD_W:(h + 1) * HEAD_W] for h in range(N_HEADS)]
    for p in range(n_pages):
        pr = jnp.exp2(scores[p] - m)
        l = l + jnp.sum(pr, axis=1, keepdims=True)
        for h in range(N_HEADS):
            accs[h] = accs[h] + _bdot(pr, v_refs[p][0, pl.ds(h, page, stride=N_HEADS), :])
    lam = _lambda(lam_ref, lam_init)
    outs = []
    for h in range(N_HEADS):
        o = accs[h] / l
        d = o[2 * h:2 * h + 1, :] - lam * o[2 * h + 1:2 * h + 2, :]
        outs.append(_rmsnorm(d, g_ref[...]) * (1.0 - lam_init))
    o_ref[0] = jnp.concatenate(outs, axis=1)


def _dattn_sample(q, k_new, v_new, kt_pool, v_pool, pool_base, page_table, lam_p, subln_g, lam_init):
    DB = q.shape[0]
    n_pages = page_table.shape[1]
    page = kt_pool.shape[2]
    tok = pl.BlockSpec((1, 1, MIX_W), lambda b, pt: (b, 0, 0))

    def k_spec(p):
        return pl.BlockSpec((1, MIX_W, page), lambda b, pt: (pool_base + pt[b, p], 0, 0))

    def v_spec(p):
        return pl.BlockSpec((1, page * N_HEADS, HEAD_W), lambda b, pt: (pool_base + pt[b, p], 0, 0))

    grid_spec = pltpu.PrefetchScalarGridSpec(
        num_scalar_prefetch=1,
        grid=(DB,),
        in_specs=[pl.BlockSpec((4, DH_A), lambda b, pt: (0, 0)),
                  pl.BlockSpec((1, HEAD_W), lambda b, pt: (0, 0)), tok, tok, tok]
                 + [k_spec(p) for p in range(n_pages)] + [v_spec(p) for p in range(n_pages)],
        out_specs=tok,
    )
    out = pl.pallas_call(
        functools.partial(_dattn_sample_kernel, n_pages=n_pages, page=page, lam_init=lam_init),
        out_shape=jax.ShapeDtypeStruct((DB, 1, MIX_W), F32),
        grid_spec=grid_spec,
        compiler_params=_cparams(1),
        name="dattn_sample",
    )(page_table, lam_p, subln_g.reshape(1, -1), q.reshape(DB, 1, MIX_W), k_new.reshape(DB, 1, MIX_W),
      v_new.reshape(DB, 1, MIX_W), *([kt_pool] * n_pages), *([v_pool] * n_pages))
    return out.reshape(DB, MIX_W)


def _hgrn_step_kernel(h_ref, s0_ref, lb_ref, ng_ref, o_ref, s_ref, *, bt):
    def body(i, carry):
        row = h_ref[i]
        outs = []
        for h in range(N_HEADS):
            c0 = h * HEAD_W
            cols = lambda blk: slice(blk * MIX_W + c0, blk * MIX_W + c0 + HEAD_W)
            lb = lb_ref[:, c0:c0 + HEAD_W]
            f = lb + (1.0 - lb) * _sigmoid(row[:, cols(4)])
            fc = _col(f)
            s_new = fc * s0_ref[i, h] + (1.0 - fc) * row[:, cols(5)]
            s_ref[i, h] = s_new
            q = jnp.broadcast_to(_silu(row[:, cols(3)]), (SUBLANES, HEAD_W))
            o = _bdot(q, s_new)[0:1, :]
            outs.append(_rmsnorm(o, ng_ref[...]) * _silu(row[:, cols(6)]))
        o_ref[i] = jnp.concatenate(outs, axis=1)
        return carry

    lax.fori_loop(0, bt, body, 0)


def _hgrn_step(h, s0, lb, norm_g, bt):
    DB, W = h.shape
    bt = min(bt, DB)
    state = pl.BlockSpec((bt, N_HEADS, HEAD_W, HEAD_W), lambda i: (i, 0, 0, 0))
    o, s = pl.pallas_call(
        functools.partial(_hgrn_step_kernel, bt=bt),
        out_shape=(jax.ShapeDtypeStruct((DB, 1, MIX_W), F32), jax.ShapeDtypeStruct(s0.shape, F32)),
        grid=(DB // bt,),
        in_specs=[pl.BlockSpec((bt, 1, W), lambda i: (i, 0, 0)), state,
                  pl.BlockSpec((1, MIX_W), lambda i: (0, 0)), pl.BlockSpec((1, HEAD_W), lambda i: (0, 0))],
        out_specs=(pl.BlockSpec((bt, 1, MIX_W), lambda i: (i, 0, 0)), state),
        compiler_params=_cparams(1),
        name="hgrn_step",
    )(h.reshape(DB, 1, W), s0, lb.reshape(1, -1), norm_g.reshape(1, -1))
    return o.reshape(DB, MIX_W), s


def _od_step_kernel(h_ref, gt_ref, bif_ref, cos_ref, sin_ref, sc0_ref, dc0_ref, dn0_ref, dm0_ref, cng_ref, dng_ref,
                    oc_ref, od_ref, sc_ref, dc_ref, dn_ref, dm_ref, *, bt):
    scale = HEAD_W ** -0.5

    def body(i, carry):
        row = h_ref[i]
        cos, sin = cos_ref[...], sin_ref[...]
        gates = gt_ref[i] + bif_ref[...]
        m_row = dm0_ref[i]
        oc, od, ms = [], [], []
        for h in range(N_HEADS):
            c0 = h * HEAD_W
            cols = lambda blk: slice(blk * MIX_W + c0, blk * MIX_W + c0 + HEAD_W)
            q = _rope(row[:, cols(0)], cos, sin, HEAD_W // 2)
            k = _rope(row[:, cols(1)], cos, sin, HEAD_W // 2) * scale
            s_new = math.exp(_log_gamma(h)) * sc0_ref[i, h] + _col(k) * row[:, cols(2)]
            sc_ref[i, h] = s_new
            o = _bdot(jnp.broadcast_to(q, (SUBLANES, HEAD_W)), s_new)[0:1, :]
            oc.append(_groupnorm(o, cng_ref[...]) * _silu(row[:, cols(3)]))
            li = gates[:, h:h + 1]
            lf = _log_sigmoid(gates[:, N_HEADS + h:N_HEADS + h + 1])
            m_prev = m_row[:, h:h + 1]
            m_new = jnp.maximum(lf + m_prev, li)
            c_scale = jnp.exp(lf + m_prev - m_new)
            kw = row[:, cols(5)] * jnp.exp(li - m_new)
            c_new = c_scale * dc0_ref[i, h] + _col(kw) * row[:, cols(6)]
            n_new = c_scale * dn0_ref[i, h:h + 1, :] + kw
            dc_ref[i, h] = c_new
            dn_ref[i, h:h + 1, :] = n_new
            ms.append(m_new)
            dq = row[:, cols(4)] * scale
            num = _bdot(jnp.broadcast_to(dq, (SUBLANES, HEAD_W)), c_new)[0:1, :]
            den = jnp.sum(dq * n_new, axis=1, keepdims=True)
            hh = num / jnp.maximum(jnp.abs(den), jnp.exp(-m_new))
            od.append(_groupnorm(hh, dng_ref[...]) * _sigmoid(row[:, cols(7)]))
        oc_ref[i] = jnp.concatenate(oc, axis=1)
        od_ref[i] = jnp.concatenate(od, axis=1)
        dm_ref[i] = jnp.concatenate(ms, axis=1)
        return carry

    lax.fori_loop(0, bt, body, 0)


def _od_step(h, gates, b_if, cos_t, sin_t, sc0, dc0, dn0, dm0, c_norm_g, d_norm_g, bt):
    DB, W = h.shape
    bt = min(bt, DB)
    mat = pl.BlockSpec((bt, N_HEADS, HEAD_W, HEAD_W), lambda i: (i, 0, 0, 0))
    nblk = pl.BlockSpec((bt, N_HEADS, HEAD_W), lambda i: (i, 0, 0))
    mblk = pl.BlockSpec((bt, 1, N_HEADS), lambda i: (i, 0, 0))
    vec = pl.BlockSpec((1, LANES), lambda i: (0, 0))
    out = pl.BlockSpec((bt, 1, MIX_W), lambda i: (i, 0, 0))
    b_pad = jnp.zeros((1, LANES), F32).at[0, :2 * N_HEADS].set(b_if)
    oc, od, sc, dc, dn, dm = pl.pallas_call(
        functools.partial(_od_step_kernel, bt=bt),
        out_shape=(jax.ShapeDtypeStruct((DB, 1, MIX_W), F32), jax.ShapeDtypeStruct((DB, 1, MIX_W), F32),
                   jax.ShapeDtypeStruct(sc0.shape, F32), jax.ShapeDtypeStruct(dc0.shape, F32),
                   jax.ShapeDtypeStruct(dn0.shape, F32), jax.ShapeDtypeStruct((DB, 1, N_HEADS), F32)),
        grid=(DB // bt,),
        in_specs=[pl.BlockSpec((bt, 1, W), lambda i: (i, 0, 0)),
                  pl.BlockSpec((bt, 1, LANES), lambda i: (i, 0, 0)), vec, vec, vec,
                  mat, mat, nblk, mblk, vec, vec],
        out_specs=(out, out, mat, mat, nblk, mblk),
        compiler_params=_cparams(1),
        name="od_step",
    )(h.reshape(DB, 1, W), gates.reshape(DB, 1, LANES), b_pad, cos_t, sin_t, sc0, dc0, dn0,
      dm0.reshape(DB, 1, N_HEADS), c_norm_g.reshape(1, -1), d_norm_g.reshape(1, -1))
    return oc.reshape(DB, MIX_W), od.reshape(DB, MIX_W), sc, dc, dn, dm.reshape(DB, N_HEADS)


X_HALVES = DH_X // LANES
X_ROWS = N_HEADS * X_HALVES


def _xattn_sample_kernel(q_ref, mk_ref, mv_ref, o_ref):
    q = q_ref[0] * (DH_X ** -0.5)
    n_cols = mk_ref.shape[1]
    q2 = jnp.concatenate([q[:, h * DH_X + c * LANES:h * DH_X + (c + 1) * LANES]
                          for c in range(X_HALVES) for h in range(N_HEADS)], axis=0)
    s2 = _bdot_nt(q2, mk_ref[0])
    row = lax.broadcasted_iota(jnp.int32, (X_ROWS, n_cols), 0)
    col = lax.broadcasted_iota(jnp.int32, (X_ROWS, n_cols), 1)
    own = (col & (X_ROWS - 1)) == row
    s2 = jnp.where(own, s2, 0.0)
    s4 = s2[:N_HEADS] + pltpu.roll(s2, n_cols - N_HEADS, 1)[N_HEADS:]
    own4 = own[:N_HEADS]
    s4 = jnp.where(own4, s4, NEG_INF)
    p4 = jnp.exp(s4 - jnp.max(s4, axis=1, keepdims=True))
    l4 = jnp.sum(p4, axis=1, keepdims=True)
    p4 = p4 / l4
    p2 = jnp.concatenate([p4, pltpu.roll(p4, N_HEADS, 1)], axis=0)
    o2 = _bdot(p2, mv_ref[0])
    o_ref[0] = jnp.concatenate([o2[c * N_HEADS + h:c * N_HEADS + h + 1, :]
                                for h in range(N_HEADS) for c in range(X_HALVES)], axis=1)


def _xattn_sample(q, mk, mv, base):
    DB = q.shape[0]
    n_rows = mk.shape[1]
    tok = pl.BlockSpec((1, 1, D_MODEL), lambda b: (b, 0, 0))
    mem = pl.BlockSpec((1, n_rows, LANES), lambda b: (base + b, 0, 0))
    out = pl.pallas_call(
        _xattn_sample_kernel,
        out_shape=jax.ShapeDtypeStruct((DB, 1, D_MODEL), F32),
        grid=(DB,),
        in_specs=[tok, mem, mem],
        out_specs=tok,
        compiler_params=_cparams(1),
        name="xattn_sample",
    )(q.reshape(DB, 1, D_MODEL), mk, mv)
    return out.reshape(DB, D_MODEL)


def _mem_rows(cache):
    n_l, DB, n_mem = cache.shape[:3]
    c = cache.reshape(n_l, DB, n_mem, N_HEADS, X_HALVES, LANES).transpose(0, 1, 2, 4, 3, 5)
    return c.reshape(n_l * DB, n_mem * X_ROWS, LANES)


def _ffn_sample_kernel(x_ref, up_ref, gate_ref, b0_ref, b1_ref, cw_ref, cb_ref, wd_ref, g_ref, b_ref, o_ref, acc_ref):
    j = pl.program_id(0)
    conv = (cb_ref[...] + cw_ref[0:1, :] * b0_ref[...] + cw_ref[1:2, :] * b1_ref[...]
            + cw_ref[2:3, :] * up_ref[...])
    y = _bdot(jax.nn.gelu(conv) * gate_ref[...], wd_ref[...])

    @pl.when(j == 0)
    def _():
        acc_ref[...] = y

    @pl.when(j > 0)
    def _():
        acc_ref[...] = acc_ref[...] + y

    @pl.when(j == pl.num_programs(0) - 1)
    def _():
        o_ref[...] = _layernorm(DN_ALPHA * x_ref[...] + acc_ref[...], g_ref[...], b_ref[...])


def _ffn_sample(x, ug, buf0, buf1, conv_w, conv_b, w_down, g, b, tf):
    DB = x.shape[0]
    d_ff = w_down.shape[0]
    nf = d_ff // tf
    ff = lambda off: pl.BlockSpec((DB, tf), lambda j: (0, off + j))
    full = lambda shape: pl.BlockSpec(shape, lambda j: (0,) * len(shape))
    return pl.pallas_call(
        _ffn_sample_kernel,
        out_shape=jax.ShapeDtypeStruct((DB, D_MODEL), F32),
        grid=(nf,),
        in_specs=[full((DB, D_MODEL)), ff(0), ff(nf), ff(0), ff(0),
                  pl.BlockSpec((3, tf), lambda j: (0, j)), pl.BlockSpec((1, tf), lambda j: (0, j)),
                  pl.BlockSpec((tf, D_MODEL), lambda j: (j, 0)), full((1, D_MODEL)), full((1, D_MODEL))],
        out_specs=full((DB, D_MODEL)),
        scratch_shapes=[pltpu.VMEM((DB, D_MODEL), F32)],
        compiler_params=_cparams(1),
        name="ffn_sample",
    )(x, ug, ug, buf0, buf1, conv_w, conv_b.reshape(1, -1), w_down, g.reshape(1, -1), b.reshape(1, -1))


def kernel(x_prompt, x_sample, cache_a_k, cache_a_v, state_b, state_c, state_d_c, state_d_n, state_d_m,
           cache_mem_k, cache_mem_v, state_conv, page_table, mem_prompt,
           ev_w_in, ev_w_out, ev_lam, ev_subln_g, ev_lb_logits, ev_b_norm_g,
           od_w_in, od_b_if, od_w_out, od_c_norm_g, od_d_norm_g,
           ln_g, ln_b, xa_wq, xa_wkv, xa_wo, ffn_w_up, ffn_conv_w, ffn_conv_b, ffn_w_down):
    B, T, _ = x_prompt.shape
    DB, t_s, _ = x_sample.shape
    assert t_s == 1, "the sample group is a single decoding step"
    assert T % CHUNK == 0
    n_pool, page = cache_a_k.shape[1], cache_a_k.shape[2]
    n_pages = page_table.shape[1]
    past = n_pages * page
    n_mem = mem_prompt.shape[1]
    d_ff = ffn_w_down.shape[1]
    tf = d_ff // 2
    N = B * T
    TM = 512

    pos_p = jnp.arange(T)
    pos_s = jnp.full((DB,), past, jnp.int32)
    lb_table = jnp.cumsum(jax.nn.softmax(ev_lb_logits.astype(F32), axis=0), axis=0)

    xp = x_prompt.reshape(N, D_MODEL)
    xs = x_sample.reshape(DB, D_MODEL)
    outs = {k: [] for k in ("ak_p", "av_p", "ak_s", "av_s", "sb_p", "sb_s", "sc_p", "sc_s", "dc_p", "dc_s",
                            "dn_p", "dn_s", "dm_p", "dm_s", "mk_p", "mv_p", "cv_p", "cv_s")}
    swap = lambda s: jnp.swapaxes(s, -1, -2)
    mem_k_rows, mem_v_rows = _mem_rows(cache_mem_k), _mem_rows(cache_mem_v)

    for l in range(DEPTH):
        j = l // 2
        if l % 2 == 0:
            lam_init = 0.8 - 0.6 * math.exp(-0.3 * l)
            w_in = ev_w_in[j].astype(BF16)
            w_out = ev_w_out[j].astype(BF16)
            h = _matmul(xp, w_in, TM, w_in.shape[1], "ev_in_prompt")
            cos_t, sin_t = _rope_tables(pos_p, DH_A)
            h3 = h.reshape(B, T, -1)
            qt, kb, kt32, vt = _ev_prep_prompt(h3, cos_t, sin_t, TM)
            o_a = _dattn_prompt(qt, kb, vt, ev_lam[j], ev_subln_g[j], lam_init, 512)
            o_b, st = _hgrn_prompt(h3, lb_table[j], ev_b_norm_g[j], 512)
            outs["ak_p"].append(kt32.reshape(B, N_HEADS, 2, DH_A, T).transpose(0, 4, 1, 2, 3))
            outs["av_p"].append(h3[:, :, 2 * MIX_W:3 * MIX_W].reshape(B, T, N_HEADS, HEAD_W))
            outs["sb_p"].append(swap(st))
            xp = _proj_ln([o_a.reshape(N, MIX_W), o_b.reshape(N, MIX_W)], [w_out[:MIX_W], w_out[MIX_W:]],
                          xp, ln_g[l, 0], ln_b[l, 0], TM, "ev_out_prompt")
            hs = _matmul(xs, w_in, DB, 512, "ev_in_sample")
            cos_s, sin_s = _rope_tables(pos_s, DH_A)
            qs, ks32, kts32 = _ev_prep_sample(hs, cos_s, sin_s)
            vs32 = hs[:, 2 * MIX_W:3 * MIX_W]
            kt_pool = cache_a_k.transpose(0, 1, 3, 4, 5, 2).reshape(-1, MIX_W, page)
            v_pool = cache_a_v.reshape(-1, page * N_HEADS, HEAD_W)
            oa_s = _dattn_sample(qs, ks32, vs32, kt_pool, v_pool, j * n_pool, page_table,
                                 ev_lam[j], ev_subln_g[j], lam_init)
            ob_s, sb_s = _hgrn_step(hs, state_b[j], lb_table[j], ev_b_norm_g[j], 8)
            outs["ak_s"].append(kts32.reshape(N_HEADS, 2, DH_A, DB).transpose(3, 0, 1, 2)[:, None])
            outs["av_s"].append(vs32.reshape(DB, 1, N_HEADS, HEAD_W))
            outs["sb_s"].append(sb_s)
            xs = _proj_ln([oa_s, ob_s], [w_out[:MIX_W], w_out[MIX_W:]], xs, ln_g[l, 0], ln_b[l, 0], DB,
                          "ev_out_sample")
        else:
            n_main = 8 * MIX_W
            w_in = od_w_in[j][:, :n_main].astype(BF16)
            w_gate = jnp.pad(od_w_in[j][:, n_main:], ((0, 0), (0, LANES - 2 * N_HEADS))).astype(BF16)
            w_out = od_w_out[j].astype(BF16)
            h = _matmul(xp, w_in, TM, w_in.shape[1], "od_in_prompt")
            gates = _matmul(xp, w_gate, 1024, LANES, "od_gates_prompt")[:, :2 * N_HEADS]
            cos_t, sin_t = _rope_tables(pos_p, HEAD_W)
            o_c, o_d, sct, dct, dn, dm = _od_prompt(h.reshape(B, T, -1), gates.reshape(B, T, -1), od_b_if[j],
                                                    cos_t, sin_t, od_c_norm_g[j], od_d_norm_g[j], 512)
            outs["sc_p"].append(swap(sct))
            outs["dc_p"].append(swap(dct))
            outs["dn_p"].append(dn[:, :N_HEADS, :])
            outs["dm_p"].append(dm[:, :N_HEADS, 0])
            xp = _proj_ln([o_c.reshape(N, MIX_W), o_d.reshape(N, MIX_W)], [w_out[:MIX_W], w_out[MIX_W:]],
                          xp, ln_g[l, 0], ln_b[l, 0], TM, "od_out_prompt")
            hs = _matmul(xs, w_in, DB, 512, "od_in_sample")
            gates_s = _matmul(xs, w_gate, DB, LANES, "od_gates_sample")
            cos_s, sin_s = _rope_tables(pos_s[:1], HEAD_W)
            oc_s, od_s, sc_s, dc_s, dn_s, dm_s = _od_step(hs, gates_s, od_b_if[j], cos_s, sin_s,
                                                          state_c[j], state_d_c[j], state_d_n[j], state_d_m[j],
                                                          od_c_norm_g[j], od_d_norm_g[j], 8)
            outs["sc_s"].append(sc_s)
            outs["dc_s"].append(dc_s)
            outs["dn_s"].append(dn_s)
            outs["dm_s"].append(dm_s)
            xs = _proj_ln([oc_s, od_s], [w_out[:MIX_W], w_out[MIX_W:]], xs, ln_g[l, 0], ln_b[l, 0], DB,
                          "od_out_sample")

        wq = xa_wq[l].astype(BF16)
        wo = xa_wo[l].astype(BF16)
        mkv = _matmul(mem_prompt.reshape(B * n_mem, D_MODEL), xa_wkv[l].astype(BF16), 512, 512, "mem_kv")
        mk, mv = mkv[:, :D_MODEL], mkv[:, D_MODEL:]
        outs["mk_p"].append(mk.reshape(B, n_mem, N_HEADS, DH_X))
        outs["mv_p"].append(mv.reshape(B, n_mem, N_HEADS, DH_X))
        xp = _xattn_prompt(xp, wq, mk.astype(BF16).reshape(B, n_mem, D_MODEL),
                           mv.astype(BF16).reshape(B, n_mem, D_MODEL), wo, ln_g[l, 1], ln_b[l, 1], T, TM)
        q_s = _matmul(xs, wq, DB, 512, "xattn_q_sample")
        xo_s = _xattn_sample(q_s, mem_k_rows, mem_v_rows, l * DB)
        xs = _proj_ln([xo_s], [wo], xs, ln_g[l, 1], ln_b[l, 1], DB, "xattn_out_sample")

        w_up = ffn_w_up[l].astype(BF16)
        w_down = ffn_w_down[l].astype(BF16)
        tm_f = min(TM, T)
        xp, tails = _ffn_prompt(xp, w_up, ffn_conv_w[l], ffn_conv_b[l], w_down, ln_g[l, 2], ln_b[l, 2], T, TM, tf)
        tails = tails.reshape(B, T // tm_f, SUBLANES, d_ff)
        outs["cv_p"].append(tails[:, -1, SUBLANES - 2:, :])
        ug_s = _matmul(xs, w_up, DB, tf, "ffn_up_sample")
        buf = state_conv[l]
        xs = _ffn_sample(xs, ug_s, buf[:, 0, :], buf[:, 1, :], ffn_conv_w[l], ffn_conv_b[l], w_down,
                         ln_g[l, 2], ln_b[l, 2], tf)
        outs["cv_s"].append(jnp.stack([buf[:, 1, :], ug_s[:, :d_ff]], axis=1))

    st = lambda k: jnp.stack(outs[k])
    return (xp.reshape(B, T, D_MODEL), xs.reshape(DB, 1, D_MODEL),
            st("ak_p"), st("av_p"), st("ak_s"), st("av_s"), st("sb_p"), st("sb_s"),
            st("sc_p"), st("sc_s"), st("dc_p"), st("dc_s"), st("dn_p"), st("dn_s"), st("dm_p"), st("dm_s"),
            st("mk_p"), st("mv_p"), st("cv_p"), st("cv_s"))
```

```python
import functools
import math

import numpy as np
import jax
import jax.numpy as jnp
from jax import lax
from jax.experimental import pallas as pl
from jax.experimental.pallas import tpu as pltpu

F32 = jnp.float32
BF16 = jnp.bfloat16

D_MODEL = 1024
MIX_W = D_MODEL // 2
N_HEADS = 4
HEAD_W = MIX_W // N_HEADS
DH_A = HEAD_W // 2
DH_X = D_MODEL // N_HEADS
CHUNK = 64
ROPE_THETA = 10000.0
NORM_EPS = 1e-5
DEPTH = 2
DN_ALPHA = (2.0 * DEPTH) ** 0.25
LANES = 128
SUBLANES = 8
VMEM_LIMIT = 56 * 1024 * 1024
NEG_INF = float("-inf")

NT_DIMS = (((1,), (1,)), ((), ()))
TN_DIMS = (((0,), (0,)), ((), ()))


def _cparams(n_axes, vmem=VMEM_LIMIT):
    return pltpu.CompilerParams(dimension_semantics=("arbitrary",) * n_axes, vmem_limit_bytes=vmem)


def _bdot(a, b):
    return jnp.dot(a.astype(BF16), b.astype(BF16), preferred_element_type=F32)


def _bdot_nt(a, b):
    return lax.dot_general(a.astype(BF16), b.astype(BF16), NT_DIMS, preferred_element_type=F32)


def _bdot_tn(a, b):
    return lax.dot_general(a.astype(BF16), b.astype(BF16), TN_DIMS, preferred_element_type=F32)


def _split3(x):
    p1 = x.astype(BF16)
    r1 = x - p1.astype(F32)
    p2 = r1.astype(BF16)
    p3 = (r1 - p2.astype(F32)).astype(BF16)
    return p1, p2, p3


def _cumsum_rows(tri, x):
    p1, p2, p3 = _split3(x)
    d = functools.partial(jnp.dot, preferred_element_type=F32)
    return d(tri, p1) + d(tri, p2) + d(tri, p3)


def _cumsum_lanes(x, triu):
    p1, p2, p3 = _split3(x)
    d = functools.partial(jnp.dot, preferred_element_type=F32)
    return d(p1, triu) + d(p2, triu) + d(p3, triu)


def _tri(L, lower):
    r = lax.broadcasted_iota(jnp.int32, (L, L), 0)
    c = lax.broadcasted_iota(jnp.int32, (L, L), 1)
    return (r >= c) if lower else (r <= c)


def _sigmoid(x):
    return 1.0 / (1.0 + jnp.exp(-x))


def _silu(x):
    return x * _sigmoid(x)


def _log_sigmoid(x):
    return jnp.minimum(x, 0.0) - jnp.log(1.0 + jnp.exp(-jnp.abs(x)))


def _layernorm(z, g, b):
    mu = jnp.mean(z, -1, keepdims=True)
    zc = z - mu
    var = jnp.mean(zc * zc, -1, keepdims=True)
    return zc * lax.rsqrt(var + NORM_EPS) * g + b


def _rmsnorm(x, g):
    return x * lax.rsqrt(jnp.mean(x * x, -1, keepdims=True) + NORM_EPS) * g


def _groupnorm(x, g):
    mu = jnp.mean(x, -1, keepdims=True)
    xc = x - mu
    var = jnp.mean(xc * xc, -1, keepdims=True)
    return xc * lax.rsqrt(var + NORM_EPS) * g


def _rope(x, cos, sin, half):
    outs = []
    for c in range(x.shape[1] // LANES):
        xc = x[:, c * LANES:(c + 1) * LANES]
        if 2 * half == LANES:
            sw = pltpu.roll(xc, half, 1)
        else:
            lane = lax.broadcasted_iota(jnp.int32, xc.shape, 1)
            first = (lane & (2 * half - 1)) < half
            sw = jnp.where(first, pltpu.roll(xc, LANES - half, 1), pltpu.roll(xc, half, 1))
        outs.append(xc * cos + sw * sin)
    return outs[0] if len(outs) == 1 else jnp.concatenate(outs, axis=1)


def _rope_tables(pos, d):
    inv = ROPE_THETA ** (-jnp.arange(0, d // 2, dtype=F32) * 2.0 / d)
    ang = pos.astype(F32)[:, None] * inv[None, :]
    cos, sin = jnp.cos(ang), jnp.sin(ang)
    reps = LANES // d
    cos_t = jnp.tile(jnp.concatenate([cos, cos], -1), (1, reps))
    sin_t = jnp.tile(jnp.concatenate([-sin, sin], -1), (1, reps))
    return cos_t, sin_t


def _col(row):
    return jnp.broadcast_to(row, (LANES, LANES)).T


def _mm_kernel(x_ref, w_ref, o_ref, xb_ref):
    @pl.when(pl.program_id(1) == 0)
    def _():
        xb_ref[...] = x_ref[...].astype(BF16)

    o_ref[...] = jnp.dot(xb_ref[...], w_ref[...], preferred_element_type=F32).astype(o_ref.dtype)


def _matmul(x, w, tm, tn, name, out_dtype=F32):
    M, K = x.shape
    N = w.shape[1]
    tm, tn = min(tm, M), min(tn, N)
    return pl.pallas_call(
        _mm_kernel,
        out_shape=jax.ShapeDtypeStruct((M, N), out_dtype),
        grid=(M // tm, N // tn),
        in_specs=[pl.BlockSpec((tm, K), lambda i, j: (i, 0)),
                  pl.BlockSpec((K, tn), lambda i, j: (0, j))],
        out_specs=pl.BlockSpec((tm, tn), lambda i, j: (i, j)),
        scratch_shapes=[pltpu.VMEM((tm, K), BF16)],
        compiler_params=_cparams(2),
        name=name,
    )(x, w)


def _proj_ln_kernel(*refs, n_in):
    a_refs, w_refs = refs[:n_in], refs[n_in:2 * n_in]
    x_ref, g_ref, b_ref, o_ref = refs[2 * n_in:]
    y = _bdot(a_refs[0][...], w_refs[0][...])
    for a_ref, w_ref in zip(a_refs[1:], w_refs[1:]):
        y = y + _bdot(a_ref[...], w_ref[...])
    o_ref[...] = _layernorm(DN_ALPHA * x_ref[...] + y, g_ref[...], b_ref[...])


def _proj_ln(acts, weights, x, g, b, tm, name):
    M = x.shape[0]
    tm = min(tm, M)
    n_in = len(acts)
    in_specs = [pl.BlockSpec((tm, a.shape[1]), lambda i: (i, 0)) for a in acts]
    in_specs += [pl.BlockSpec(w.shape, lambda i: (0, 0)) for w in weights]
    in_specs += [pl.BlockSpec((tm, D_MODEL), lambda i: (i, 0)),
                 pl.BlockSpec((1, D_MODEL), lambda i: (0, 0)),
                 pl.BlockSpec((1, D_MODEL), lambda i: (0, 0))]
    return pl.pallas_call(
        functools.partial(_proj_ln_kernel, n_in=n_in),
        out_shape=jax.ShapeDtypeStruct((M, D_MODEL), F32),
        grid=(M // tm,),
        in_specs=in_specs,
        out_specs=pl.BlockSpec((tm, D_MODEL), lambda i: (i, 0)),
        compiler_params=_cparams(1),
        name=name,
    )(*acts, *weights, x, g.reshape(1, -1), b.reshape(1, -1))


Q_SCALE = DH_A ** -0.5 * math.log2(math.e)
VT_ROWS = HEAD_W + 16


def _ev_prep_prompt_kernel(qk_ref, v_ref, cos_ref, sin_ref, qt_ref, kb_ref, kt_ref, vt_ref, vr_ref):
    cos, sin = cos_ref[...], sin_ref[...]
    q = _rope(qk_ref[0, :, :MIX_W], cos, sin, DH_A // 2)
    k = _rope(qk_ref[0, :, MIX_W:], cos, sin, DH_A // 2)
    qt_ref[0] = (q * Q_SCALE).T.astype(BF16)
    kb_ref[0] = k.astype(BF16)
    kt_ref[0] = k.T
    v = v_ref[0]
    tm = v.shape[0]
    vt = v.T.astype(BF16)
    ones = jnp.ones((VT_ROWS - HEAD_W, tm), BF16)
    for h in range(N_HEADS):
        vt_ref[0, h, :HEAD_W, :] = vt[h * HEAD_W:(h + 1) * HEAD_W]
        vt_ref[0, h, HEAD_W:, :] = ones
        vr_ref[0, pl.ds(h, tm, stride=N_HEADS), :] = _head(v, h)


def _ev_prep_prompt(h3, cos_t, sin_t, tm):
    B, T, _ = h3.shape
    tm = min(tm, T)
    tab = pl.BlockSpec((tm, LANES), lambda b, i: (i, 0))
    tr = pl.BlockSpec((1, MIX_W, tm), lambda b, i: (b, 0, i))
    tshape = lambda dt: jax.ShapeDtypeStruct((B, MIX_W, T), dt)
    return pl.pallas_call(
        _ev_prep_prompt_kernel,
        out_shape=(tshape(BF16), jax.ShapeDtypeStruct((B, T, MIX_W), BF16), tshape(F32),
                   jax.ShapeDtypeStruct((B, N_HEADS, VT_ROWS, T), BF16),
                   jax.ShapeDtypeStruct((B, T * N_HEADS, HEAD_W), F32)),
        grid=(B, T // tm),
        in_specs=[pl.BlockSpec((1, tm, 2 * MIX_W), lambda b, i: (b, i, 0)),
                  pl.BlockSpec((1, tm, MIX_W), lambda b, i: (b, i, 2)), tab, tab],
        out_specs=(tr, pl.BlockSpec((1, tm, MIX_W), lambda b, i: (b, i, 0)), tr,
                   pl.BlockSpec((1, N_HEADS, VT_ROWS, tm), lambda b, i: (b, 0, 0, i)),
                   pl.BlockSpec((1, tm * N_HEADS, HEAD_W), lambda b, i: (b, i, 0))),
        compiler_params=_cparams(2),
        name="ev_prep_prompt",
    )(h3, h3, cos_t, sin_t)


def _ev_prep_sample_kernel(qk_ref, cos_ref, sin_ref, q_ref, k_ref, kt_ref):
    cos, sin = cos_ref[...], sin_ref[...]
    k = _rope(qk_ref[:, MIX_W:], cos, sin, DH_A // 2)
    q_ref[...] = _rope(qk_ref[:, :MIX_W], cos, sin, DH_A // 2) * Q_SCALE
    k_ref[...] = k
    kt_ref[...] = k.T


def _ev_prep_sample(h, cos_t, sin_t):
    DB = h.shape[0]
    full = lambda shape: pl.BlockSpec(shape, lambda i: (0,) * len(shape))
    return pl.pallas_call(
        _ev_prep_sample_kernel,
        out_shape=(jax.ShapeDtypeStruct((DB, MIX_W), F32), jax.ShapeDtypeStruct((DB, MIX_W), F32),
                   jax.ShapeDtypeStruct((MIX_W, DB), F32)),
        grid=(1,),
        in_specs=[full((DB, 2 * MIX_W)), full((DB, LANES)), full((DB, LANES))],
        out_specs=(full((DB, MIX_W)), full((DB, MIX_W)), full((MIX_W, DB))),
        compiler_params=_cparams(1),
        name="ev_prep_sample",
    )(h, cos_t, sin_t)


def _lambda(lam_ref, lam_init):
    lp = lam_ref[...]
    s01 = jnp.sum(lp[0:1] * lp[1:2], axis=1, keepdims=True)
    s23 = jnp.sum(lp[2:3] * lp[3:4], axis=1, keepdims=True)
    return jnp.exp(s01) - jnp.exp(s23) + lam_init


def _dattn_kernel(lam_ref, g_ref, qt_ref, k_ref, vt_ref, o_ref, m_ref, acc_ref, *, tq, lam_init):
    i = pl.program_id(2)
    qt = qt_ref[0]
    sub = lax.broadcasted_iota(jnp.int32, qt.shape, 0)
    zero = jnp.zeros_like(qt)
    q_maps = [jnp.where(sub < DH_A, qt, zero), jnp.where(sub >= DH_A, qt, zero)]
    maps = range(2)
    m_ref[...] = jnp.full(m_ref.shape, NEG_INF, F32)
    acc_ref[...] = jnp.zeros(acc_ref.shape, F32)

    def steps(js, masked):
        ks, vts, ss = [], [], []
        for j in js:
            start = pl.multiple_of(j * tq, tq)
            ks.append(k_ref[0, pl.ds(start, tq), :])
            vts.append(vt_ref[0, 0, :, pl.ds(start, tq)])
        for kj in ks:
            ss.append([jnp.dot(kj, q_maps[c], preferred_element_type=F32) for c in maps])
        for n in range(len(js)):
            for c in maps:
                s = ss[n][c]
                if masked:
                    r = lax.broadcasted_iota(jnp.int32, s.shape, 0)
                    col = lax.broadcasted_iota(jnp.int32, s.shape, 1)
                    s = jnp.where(r <= col, s, NEG_INF)
                m_prev = m_ref[c]
                m_new = jnp.maximum(m_prev, jnp.max(s, axis=0, keepdims=True))
                alpha = jnp.exp2(m_prev - m_new)
                p = jnp.exp2(s - m_new)
                acc_ref[c] = alpha * acc_ref[c] + jnp.dot(vts[n], p.astype(BF16), preferred_element_type=F32)
                m_ref[c] = m_new

    def body(jj, carry):
        steps([2 * jj, 2 * jj + 1], False)
        return carry

    lax.fori_loop(0, lax.shift_right_logical(i, 1), body, 0)

    @pl.when((i & 1) == 1)
    def _():
        steps([i - 1], False)

    steps([i], True)
    lam = _lambda(lam_ref, lam_init)
    o = [acc_ref[c, :HEAD_W, :] / acc_ref[c, HEAD_W:HEAD_W + 1, :] for c in maps]
    d = (o[0] - lam * o[1]).T
    o_ref[0] = _rmsnorm(d, g_ref[...]) * (1.0 - lam_init)


def _dattn_prompt(qt, kb, vt, lam_p, subln_g, lam_init, tq):
    B, T, _ = kb.shape
    tq = min(tq, T)
    return pl.pallas_call(
        functools.partial(_dattn_kernel, tq=tq, lam_init=lam_init),
        out_shape=jax.ShapeDtypeStruct((B, T, MIX_W), F32),
        grid=(B, N_HEADS, T // tq),
        in_specs=[pl.BlockSpec((4, DH_A), lambda b, h, i: (0, 0)),
                  pl.BlockSpec((1, HEAD_W), lambda b, h, i: (0, 0)),
                  pl.BlockSpec((1, HEAD_W, tq), lambda b, h, i: (b, h, i)),
                  pl.BlockSpec((1, T, HEAD_W), lambda b, h, i: (b, 0, h)),
                  pl.BlockSpec((1, 1, VT_ROWS, T), lambda b, h, i: (b, h, 0, 0))],
        out_specs=pl.BlockSpec((1, tq, HEAD_W), lambda b, h, i: (b, i, h)),
        scratch_shapes=[pltpu.VMEM((2, 1, tq), F32), pltpu.VMEM((2, VT_ROWS, tq), F32)],
        compiler_params=_cparams(3),
        name="dattn_prompt",
    )(lam_p, subln_g.reshape(1, -1), qt, kb, vt)


def _head(x, h):
    return x[:, h * HEAD_W:(h + 1) * HEAD_W]


def _gla_chunk(q_in, k_in, k_end, v, dec, st_ref, tril_mask):
    q_in, k_in, k_end, v = (x.astype(BF16) for x in (q_in, k_in, k_end, v))
    heads = range(N_HEADS)
    sts = [st_ref[h] for h in heads]
    inter = [_bdot_nt(_head(q_in, h), sts[h]) for h in heads]
    upd = [_bdot_tn(_head(v, h), _head(k_end, h)) for h in heads]
    attn = [jnp.where(tril_mask, _bdot_nt(_head(q_in, h), _head(k_in, h)), 0.0) for h in heads]
    intra = [_bdot(attn[h], _head(v, h)) for h in heads]
    for h in heads:
        st_ref[h] = sts[h] * (dec[h] if isinstance(dec, (list, tuple)) else _head(dec, h)) + upd[h]
    return [intra[h] + inter[h] for h in heads]


def _hgrn_kernel(q_ref, f_ref, i_ref, g_ref, lb_ref, ng_ref, o_ref, s_ref, st_ref, *, tt, L):
    t = pl.program_id(1)

    @pl.when(t == 0)
    def _():
        st_ref[...] = jnp.zeros(st_ref.shape, F32)

    tril_mask = _tri(L, True)
    tril = tril_mask.astype(BF16)

    def body(c, carry):
        rows = pl.ds(pl.multiple_of(c * L, L), L)
        lb = lb_ref[...]
        f = lb + (1.0 - lb) * _sigmoid(f_ref[0, rows, :])
        b = _cumsum_rows(tril, jnp.log(f))
        b_end = b[L - 1:L, :]
        k = 1.0 - f
        q_in = _silu(q_ref[0, rows, :]) * jnp.exp(b)
        o = _gla_chunk(q_in, k * jnp.exp(-b), k * jnp.exp(b_end - b), i_ref[0, rows, :], jnp.exp(b_end),
                       st_ref, tril_mask)
        o = jnp.concatenate([_rmsnorm(o_h, ng_ref[...]) for o_h in o], axis=1)
        o_ref[0, rows, :] = o * _silu(g_ref[0, rows, :])
        return carry

    lax.fori_loop(0, tt // L, body, 0, unroll=2)

    @pl.when(t == pl.num_programs(1) - 1)
    def _():
        s_ref[0] = st_ref[...]


def _hgrn_prompt(h3, lb, norm_g, tt):
    B, T, _ = h3.shape
    tt = min(tt, T)
    L = math.gcd(T, CHUNK)
    col = lambda c: pl.BlockSpec((1, tt, MIX_W), lambda b, t: (b, t, c))
    return pl.pallas_call(
        functools.partial(_hgrn_kernel, tt=tt, L=L),
        out_shape=(jax.ShapeDtypeStruct((B, T, MIX_W), F32),
                   jax.ShapeDtypeStruct((B, N_HEADS, HEAD_W, HEAD_W), F32)),
        grid=(B, T // tt),
        in_specs=[col(3), col(4), col(5), col(6),
                  pl.BlockSpec((1, MIX_W), lambda b, t: (0, 0)),
                  pl.BlockSpec((1, HEAD_W), lambda b, t: (0, 0))],
        out_specs=(pl.BlockSpec((1, tt, MIX_W), lambda b, t: (b, t, 0)),
                   pl.BlockSpec((1, N_HEADS, HEAD_W, HEAD_W), lambda b, t: (b, 0, 0, 0))),
        scratch_shapes=[pltpu.VMEM((N_HEADS, HEAD_W, HEAD_W), F32)],
        compiler_params=_cparams(2),
        name="hgrn_prompt",
    )(h3, h3, h3, h3, lb.reshape(1, -1), norm_g.reshape(1, -1))


def _log_gamma(h):
    return float(np.log(1.0 - 2.0 ** (-5.0 - h)))


def _od_kernel(cq_ref, ck_ref, cv_ref, cg_ref, dq_ref, dk_ref, dv_ref, do_ref, gc_ref, gr_ref,
               bc_ref, br_ref, cos_ref, sin_ref, cng_ref, dng_ref,
               oc_ref, od_ref, sc_ref, dc_ref, dn_ref, dm_ref,
               st_ref, ct_ref, n_ref, m_ref, *, tt, L):
    t = pl.program_id(1)

    @pl.when(t == 0)
    def _():
        st_ref[...] = jnp.zeros(st_ref.shape, F32)
        ct_ref[...] = jnp.zeros(ct_ref.shape, F32)
        n_ref[...] = jnp.zeros(n_ref.shape, F32)
        m_ref[...] = jnp.zeros(m_ref.shape, F32)

    tril_mask = _tri(L, True)
    tril = tril_mask.astype(BF16)
    triu = _tri(L, False).astype(BF16)
    scale = HEAD_W ** -0.5
    heads = range(N_HEADS)
    pos1 = (lax.broadcasted_iota(jnp.int32, (L, HEAD_W), 0) + 1).astype(F32)
    ret_b = jnp.concatenate([pos1 * _log_gamma(h) for h in heads], axis=1)
    ret_b_end = jnp.concatenate([jnp.full((L, HEAD_W), L * _log_gamma(h), F32) for h in heads], axis=1)
    ret_q_dec, ret_k_dec, ret_k_end = jnp.exp(ret_b), jnp.exp(-ret_b), jnp.exp(ret_b_end - ret_b)
    ret_dec = [math.exp(L * _log_gamma(h)) for h in heads]

    def body(c, carry):
        rows = pl.ds(pl.multiple_of(c * L, L), L)
        cos, sin = cos_ref[rows, :], sin_ref[rows, :]
        q = _rope(cq_ref[0, rows, :], cos, sin, HEAD_W // 2)
        k = _rope(ck_ref[0, rows, :], cos, sin, HEAD_W // 2) * scale
        o = _gla_chunk(q * ret_q_dec, k * ret_k_dec, k * ret_k_end, cv_ref[0, rows, :], ret_dec, st_ref, tril_mask)
        o = jnp.concatenate([_groupnorm(o_h, cng_ref[...]) for o_h in o], axis=1)
        oc_ref[0, rows, :] = o * _silu(cg_ref[0, rows, :])
        g_c = gc_ref[0, rows, :] + bc_ref[...]
        g_r = gr_ref[0, c] + br_ref[...]
        lf_c = _log_sigmoid(g_c)
        b_c = _cumsum_rows(tril, jnp.concatenate(
            [jnp.broadcast_to(lf_c[:, N_HEADS + h:N_HEADS + h + 1], (L, HEAD_W)) for h in heads], axis=1))
        b_r = _cumsum_lanes(_log_sigmoid(g_r), triu)
        dq = (dq_ref[0, rows, :] * scale)
        dk = dk_ref[0, rows, :]
        dqb, dkb, dvb = dq.astype(BF16), dk.astype(BF16), dv_ref[0, rows, :].astype(BF16)
        cts = [ct_ref[h] for h in heads]
        q_c = [_bdot_nt(_head(dqb, h), cts[h]) for h in heads]
        q_k = [_bdot_nt(_head(dqb, h), _head(dkb, h)) for h in heads]
        m_prev = [m_ref[h:h + 1, 0:1] for h in heads]
        bc1, m_t, w_mat = [], [], []
        for h in heads:
            bc = _head(b_c, h)[:, :L]
            dm = jnp.where(tril_mask, bc - b_r[N_HEADS + h:N_HEADS + h + 1, :] + g_r[h:h + 1, :], NEG_INF)
            bc1.append(bc[:, 0:1])
            m_t.append(jnp.maximum(bc1[h] + m_prev[h], jnp.max(dm, axis=1, keepdims=True)))
            w_mat.append(jnp.exp(dm - m_t[h]) * q_k[h])
        w_v = [_bdot(w_mat[h], _head(dvb, h)) for h in heads]
        kws, c_scales = [], []
        for h in heads:
            m_new = m_t[h][L - 1:L, :]
            b_last = bc1[h][L - 1:L, :]
            c_scales.append(jnp.exp(b_last + m_prev[h] - m_new))
            kws.append(_head(dk, h) * jnp.exp(b_last - bc1[h] + g_c[:, h:h + 1] - m_new))
            m_ref[h:h + 1, :] = jnp.broadcast_to(m_new, (1, LANES))
        c_upd = [_bdot_tn(_head(dvb, h), kws[h]) for h in heads]
        hs = []
        for h in heads:
            n_row = n_ref[h:h + 1, :]
            inter = jnp.exp(bc1[h] + m_prev[h] - m_t[h])
            num = inter * q_c[h] + w_v[h]
            den = (inter * jnp.sum(_head(dq, h) * n_row, axis=1, keepdims=True)
                   + jnp.sum(w_mat[h], axis=1, keepdims=True))
            hs.append(_groupnorm(num / jnp.maximum(jnp.abs(den), jnp.exp(-m_t[h])), dng_ref[...]))
            ct_ref[h] = c_scales[h] * cts[h] + c_upd[h]
            n_ref[h:h + 1, :] = c_scales[h] * n_row + jnp.sum(kws[h], axis=0, keepdims=True)
        od_ref[0, rows, :] = jnp.concatenate(hs, axis=1) * _sigmoid(do_ref[0, rows, :])
        return carry

    lax.fori_loop(0, tt // L, body, 0)

    @pl.when(t == pl.num_programs(1) - 1)
    def _():
        sc_ref[0] = st_ref[...]
        dc_ref[0] = ct_ref[...]
        dn_ref[0] = n_ref[...]
        dm_ref[0] = m_ref[...]


def _od_prompt(h3, gates, b_if, cos_t, sin_t, c_norm_g, d_norm_g, tt):
    B, T, _ = h3.shape
    tt = min(tt, T)
    L = math.gcd(T, CHUNK)
    gates_r = gates.reshape(B, T // L, L, 2 * N_HEADS).transpose(0, 1, 3, 2)
    col = lambda c: pl.BlockSpec((1, tt, MIX_W), lambda b, t: (b, t, c))
    tab = pl.BlockSpec((tt, LANES), lambda b, t: (t, 0))
    vec = pl.BlockSpec((1, HEAD_W), lambda b, t: (0, 0))
    mat_state = pl.BlockSpec((1, N_HEADS, HEAD_W, HEAD_W), lambda b, t: (b, 0, 0, 0))
    row_state = pl.BlockSpec((1, SUBLANES, LANES), lambda b, t: (b, 0, 0))
    out_blk = pl.BlockSpec((1, tt, MIX_W), lambda b, t: (b, t, 0))
    return pl.pallas_call(
        functools.partial(_od_kernel, tt=tt, L=L),
        out_shape=(jax.ShapeDtypeStruct((B, T, MIX_W), F32), jax.ShapeDtypeStruct((B, T, MIX_W), F32),
                   jax.ShapeDtypeStruct((B, N_HEADS, HEAD_W, HEAD_W), F32),
                   jax.ShapeDtypeStruct((B, N_HEADS, HEAD_W, HEAD_W), F32),
                   jax.ShapeDtypeStruct((B, SUBLANES, LANES), F32),
                   jax.ShapeDtypeStruct((B, SUBLANES, LANES), F32)),
        grid=(B, T // tt),
        in_specs=[col(0), col(1), col(2), col(3), col(4), col(5), col(6), col(7),
                  pl.BlockSpec((1, tt, 2 * N_HEADS), lambda b, t: (b, t, 0)),
                  pl.BlockSpec((1, tt // L, 2 * N_HEADS, L), lambda b, t: (b, t, 0, 0)),
                  pl.BlockSpec((1, 2 * N_HEADS), lambda b, t: (0, 0)),
                  pl.BlockSpec((2 * N_HEADS, 1), lambda b, t: (0, 0)),
                  tab, tab, vec, vec],
        out_specs=(out_blk, out_blk, mat_state, mat_state, row_state, row_state),
        scratch_shapes=[pltpu.VMEM((N_HEADS, HEAD_W, HEAD_W), F32), pltpu.VMEM((N_HEADS, HEAD_W, HEAD_W), F32),
                        pltpu.VMEM((SUBLANES, LANES), F32), pltpu.VMEM((SUBLANES, LANES), F32)],
        compiler_params=_cparams(2),
        name="od_prompt",
    )(h3, h3, h3, h3, h3, h3, h3, h3, gates, gates_r, b_if.reshape(1, -1), b_if.reshape(-1, 1),
      cos_t, sin_t, c_norm_g.reshape(1, -1), d_norm_g.reshape(1, -1))


def _xattn_kernel(x_ref, wq_ref, mk_ref, mv_ref, wo_ref, g_ref, b_ref, o_ref):
    x = x_ref[...]
    q = _bdot(x, wq_ref[...])
    qb = (q * (DH_X ** -0.5)).astype(BF16)
    outs = []
    for h in range(N_HEADS):
        cols = slice(h * DH_X, (h + 1) * DH_X)
        s = lax.dot_general(qb[:, cols], mk_ref[0, :, cols], NT_DIMS, preferred_element_type=F32)
        p = jnp.exp(s - jnp.max(s, axis=1, keepdims=True))
        l = jnp.sum(p, axis=1, keepdims=True)
        outs.append((jnp.dot(p.astype(BF16), mv_ref[0, :, cols], preferred_element_type=F32) / l).astype(BF16))
    y = jnp.dot(jnp.concatenate(outs, axis=1), wo_ref[...], preferred_element_type=F32)
    o_ref[...] = _layernorm(DN_ALPHA * x + y, g_ref[...], b_ref[...])


def _xattn_prompt(x, wq, mk, mv, wo, g, b, T, tm):
    M = x.shape[0]
    tm = min(tm, T)
    n_mem = mk.shape[1]
    per_b = T // tm
    full = lambda shape: pl.BlockSpec(shape, lambda i: (0,) * len(shape))
    mem = pl.BlockSpec((1, n_mem, D_MODEL), lambda i: (i // per_b, 0, 0))
    return pl.pallas_call(
        _xattn_kernel,
        out_shape=jax.ShapeDtypeStruct((M, D_MODEL), F32),
        grid=(M // tm,),
        in_specs=[pl.BlockSpec((tm, D_MODEL), lambda i: (i, 0)), full((D_MODEL, D_MODEL)), mem, mem,
                  full((D_MODEL, D_MODEL)), full((1, D_MODEL)), full((1, D_MODEL))],
        out_specs=pl.BlockSpec((tm, D_MODEL), lambda i: (i, 0)),
        compiler_params=_cparams(1),
        name="xattn_prompt",
    )(x, wq, mk, mv, wo, g.reshape(1, -1), b.reshape(1, -1))


def _ffn_kernel(x_ref, wu_ref, wg_ref, cw_ref, cb_ref, wd_ref, g_ref, b_ref, o_ref, tail_ref,
                xb_ref, acc_ref, stage_ref, carry_ref, *, tm, per_b):
    i, j = pl.program_id(0), pl.program_id(1)

    @pl.when(j == 0)
    def _():
        xb_ref[...] = x_ref[...].astype(BF16)

    @pl.when(i % per_b == 0)
    def _():
        carry_ref[j] = jnp.zeros(carry_ref.shape[1:], F32)

    xb = xb_ref[...]
    stage_ref[0:SUBLANES, :] = carry_ref[j]
    stage_ref[SUBLANES:, :] = jnp.dot(xb, wu_ref[...], preferred_element_type=F32)
    last = stage_ref[tm:tm + SUBLANES, :]
    carry_ref[j] = last
    tail_ref[0] = last
    conv = (cb_ref[...] + cw_ref[0:1, :] * stage_ref[SUBLANES - 2:SUBLANES - 2 + tm, :]
            + cw_ref[1:2, :] * stage_ref[SUBLANES - 1:SUBLANES - 1 + tm, :]
            + cw_ref[2:3, :] * stage_ref[SUBLANES:, :])
    gate = jnp.dot(xb, wg_ref[...], preferred_element_type=F32)
    y = _bdot(jax.nn.gelu(conv) * gate, wd_ref[...])

    @pl.when(j == 0)
    def _():
        acc_ref[...] = y

    @pl.when(j > 0)
    def _():
        acc_ref[...] = acc_ref[...] + y

    @pl.when(j == pl.num_programs(1) - 1)
    def _():
        o_ref[...] = _layernorm(DN_ALPHA * x_ref[...] + acc_ref[...], g_ref[...], b_ref[...])


def _ffn_prompt(x, w_up, conv_w, conv_b, w_down, g, b, T, tm, tf):
    M = x.shape[0]
    d_ff = w_down.shape[0]
    tm = min(tm, T)
    nf = d_ff // tf
    per_b = T // tm
    return pl.pallas_call(
        functools.partial(_ffn_kernel, tm=tm, per_b=per_b),
        out_shape=(jax.ShapeDtypeStruct((M, D_MODEL), F32),
                   jax.ShapeDtypeStruct((M // tm, SUBLANES, d_ff), F32)),
        grid=(M // tm, nf),
        in_specs=[pl.BlockSpec((tm, D_MODEL), lambda i, j: (i, 0)),
                  pl.BlockSpec((D_MODEL, tf), lambda i, j: (0, j)),
                  pl.BlockSpec((D_MODEL, tf), lambda i, j: (0, nf + j)),
                  pl.BlockSpec((3, tf), lambda i, j: (0, j)),
                  pl.BlockSpec((1, tf), lambda i, j: (0, j)),
                  pl.BlockSpec((tf, D_MODEL), lambda i, j: (j, 0)),
                  pl.BlockSpec((1, D_MODEL), lambda i, j: (0, 0)),
                  pl.BlockSpec((1, D_MODEL), lambda i, j: (0, 0))],
        out_specs=(pl.BlockSpec((tm, D_MODEL), lambda i, j: (i, 0)),
                   pl.BlockSpec((1, SUBLANES, tf), lambda i, j: (i, 0, j))),
        scratch_shapes=[pltpu.VMEM((tm, D_MODEL), BF16), pltpu.VMEM((tm, D_MODEL), F32),
                        pltpu.VMEM((tm + SUBLANES, tf), F32), pltpu.VMEM((nf, SUBLANES, tf), F32)],
        compiler_params=_cparams(2),
        name="ffn_prompt",
    )(x, w_up, w_up, conv_w, conv_b.reshape(1, -1), w_down, g.reshape(1, -1), b.reshape(1, -1))


def _dattn_sample_kernel(*refs, n_pages, page, lam_init):
    pt_ref = refs[0]
    lam_ref, g_ref, q_ref, kn_ref, vn_ref = refs[1:6]
    k_refs = refs[6:6 + n_pages]
    v_refs = refs[6 + n_pages:6 + 2 * n_pages]
    o_ref = refs[6 + 2 * n_pages]
    del pt_ref
    n_rows = 2 * N_HEADS
    q = q_ref[0]
    lane = lax.broadcasted_iota(jnp.int32, (n_rows, MIX_W), 1)
    row = lax.broadcasted_iota(jnp.int32, (n_rows, MIX_W), 0)
    qbd = jnp.where(lax.shift_right_logical(lane, 6) == row, jnp.broadcast_to(q, (n_rows, MIX_W)), 0.0)
    s_new = jnp.sum(qbd * kn_ref[0], axis=1, keepdims=True)
    scores = [_bdot(qbd, k_refs[p][0]) for p in range(n_pages)]
    m = s_new
    for s in scores:
        m = jnp.maximum(m, jnp.max(s, axis=1, keepdims=True))
    p_new = jnp.exp2(s_new - m)
    l = p_new
    vn = vn_ref[0]
    accs = [p_new * vn[:, h * HEAD_W:(h + 1) * HEAD_W] for h in range(N_HEADS)]
    for p in range(n_pages):
        pr = jnp.exp2(scores[p] - m)
        l = l + jnp.sum(pr, axis=1, keepdims=True)
        for h in range(N_HEADS):
            accs[h] = accs[h] + _bdot(pr, v_refs[p][0, pl.ds(h, page, stride=N_HEADS), :])
    lam = _lambda(lam_ref, lam_init)
    outs = []
    for h in range(N_HEADS):
        o = accs[h] / l
        d = o[2 * h:2 * h + 1, :] - lam * o[2 * h + 1:2 * h + 2, :]
        outs.append(_rmsnorm(d, g_ref[...]) * (1.0 - lam_init))
    o_ref[0] = jnp.concatenate(outs, axis=1)


def _dattn_sample(q, k_new, v_new, kt_pool, v_pool, pool_base, page_table, lam_p, subln_g, lam_init):
    DB = q.shape[0]
    n_pages = page_table.shape[1]
    page = kt_pool.shape[2]
    tok = pl.BlockSpec((1, 1, MIX_W), lambda b, pt: (b, 0, 0))

    def k_spec(p):
        return pl.BlockSpec((1, MIX_W, page), lambda b, pt: (pool_base + pt[b, p], 0, 0))

    def v_spec(p):
        return pl.BlockSpec((1, page * N_HEADS, HEAD_W), lambda b, pt: (pool_base + pt[b, p], 0, 0))

    grid_spec = pltpu.PrefetchScalarGridSpec(
        num_scalar_prefetch=1,
        grid=(DB,),
        in_specs=[pl.BlockSpec((4, DH_A), lambda b, pt: (0, 0)),
                  pl.BlockSpec((1, HEAD_W), lambda b, pt: (0, 0)), tok, tok, tok]
                 + [k_spec(p) for p in range(n_pages)] + [v_spec(p) for p in range(n_pages)],
        out_specs=tok,
    )
    out = pl.pallas_call(
        functools.partial(_dattn_sample_kernel, n_pages=n_pages, page=page, lam_init=lam_init),
        out_shape=jax.ShapeDtypeStruct((DB, 1, MIX_W), F32),
        grid_spec=grid_spec,
        compiler_params=_cparams(1),
        name="dattn_sample",
    )(page_table, lam_p, subln_g.reshape(1, -1), q.reshape(DB, 1, MIX_W), k_new.reshape(DB, 1, MIX_W),
      v_new.reshape(DB, 1, MIX_W), *([kt_pool] * n_pages), *([v_pool] * n_pages))
    return out.reshape(DB, MIX_W)


def _hgrn_step_kernel(h_ref, s0_ref, lb_ref, ng_ref, o_ref, s_ref, *, bt):
    def body(i, carry):
        row = h_ref[i]
        heads = range(N_HEADS)
        blk = lambda n: row[:, n * MIX_W:(n + 1) * MIX_W]
        lb = lb_ref[...]
        f = lb + (1.0 - lb) * _sigmoid(blk(4))
        q, v = _silu(blk(3)), blk(5)
        f_col = [_col(_head(f, h)) for h in heads]
        s_new = [f_col[h] * s0_ref[i, h] + (1.0 - f_col[h]) * _head(v, h) for h in heads]
        for h in heads:
            s_ref[i, h] = s_new[h]
        o = [_bdot(jnp.broadcast_to(_head(q, h), (SUBLANES, HEAD_W)), s_new[h])[0:1, :] for h in heads]
        o = jnp.concatenate([_rmsnorm(o[h], ng_ref[...]) for h in heads], axis=1)
        o_ref[i] = o * _silu(blk(6))
        return carry

    lax.fori_loop(0, bt, body, 0, unroll=2)


def _hgrn_step(h, s0, lb, norm_g, bt):
    DB, W = h.shape
    bt = min(bt, DB)
    state = pl.BlockSpec((bt, N_HEADS, HEAD_W, HEAD_W), lambda i: (i, 0, 0, 0))
    o, s = pl.pallas_call(
        functools.partial(_hgrn_step_kernel, bt=bt),
        out_shape=(jax.ShapeDtypeStruct((DB, 1, MIX_W), F32), jax.ShapeDtypeStruct(s0.shape, F32)),
        grid=(DB // bt,),
        in_specs=[pl.BlockSpec((bt, 1, W), lambda i: (i, 0, 0)), state,
                  pl.BlockSpec((1, MIX_W), lambda i: (0, 0)), pl.BlockSpec((1, HEAD_W), lambda i: (0, 0))],
        out_specs=(pl.BlockSpec((bt, 1, MIX_W), lambda i: (i, 0, 0)), state),
        compiler_params=_cparams(1),
        name="hgrn_step",
    )(h.reshape(DB, 1, W), s0, lb.reshape(1, -1), norm_g.reshape(1, -1))
    return o.reshape(DB, MIX_W), s


def _od_step_kernel(h_ref, gt_ref, bif_ref, cos_ref, sin_ref, sc0_ref, dc0_ref, dn0_ref, dm0_ref, cng_ref, dng_ref,
                    oc_ref, od_ref, sc_ref, dc_ref, dn_ref, dm_ref, *, bt):
    scale = HEAD_W ** -0.5

    def body(i, carry):
        row = h_ref[i]
        cos, sin = cos_ref[...], sin_ref[...]
        gates = gt_ref[i] + bif_ref[...]
        m_row = dm0_ref[i]
        heads = range(N_HEADS)
        blk = lambda n: row[:, n * MIX_W:(n + 1) * MIX_W]
        q = _rope(blk(0), cos, sin, HEAD_W // 2)
        k = _rope(blk(1), cos, sin, HEAD_W // 2) * scale
        cv, dk, dv, dq = blk(2), blk(5), blk(6), blk(4) * scale
        lf = _log_sigmoid(gates)
        m_new = [jnp.maximum(lf[:, N_HEADS + h:N_HEADS + h + 1] + m_row[:, h:h + 1], gates[:, h:h + 1])
                 for h in heads]
        c_scale = [jnp.exp(lf[:, N_HEADS + h:N_HEADS + h + 1] + m_row[:, h:h + 1] - m_new[h]) for h in heads]
        kw = [_head(dk, h) * jnp.exp(gates[:, h:h + 1] - m_new[h]) for h in heads]
        k_col = [_col(_head(k, h)) for h in heads]
        kw_col = [_col(kw[h]) for h in heads]
        s_new = [math.exp(_log_gamma(h)) * sc0_ref[i, h] + k_col[h] * _head(cv, h) for h in heads]
        c_new = [c_scale[h] * dc0_ref[i, h] + kw_col[h] * _head(dv, h) for h in heads]
        n_new = [c_scale[h] * dn0_ref[i, h:h + 1, :] + kw[h] for h in heads]
        for h in heads:
            sc_ref[i, h] = s_new[h]
            dc_ref[i, h] = c_new[h]
            dn_ref[i, h:h + 1, :] = n_new[h]
        rows8 = lambda x: jnp.broadcast_to(x, (SUBLANES, HEAD_W))
        o = [_bdot(rows8(_head(q, h)), s_new[h])[0:1, :] for h in heads]
        num = [_bdot(rows8(_head(dq, h)), c_new[h])[0:1, :] for h in heads]
        den = [jnp.sum(_head(dq, h) * n_new[h], axis=1, keepdims=True) for h in heads]
        hh = [num[h] / jnp.maximum(jnp.abs(den[h]), jnp.exp(-m_new[h])) for h in heads]
        oc = jnp.concatenate([_groupnorm(o[h], cng_ref[...]) for h in heads], axis=1)
        od = jnp.concatenate([_groupnorm(hh[h], dng_ref[...]) for h in heads], axis=1)
        oc_ref[i] = oc * _silu(blk(3))
        od_ref[i] = od * _sigmoid(blk(7))
        dm_ref[i] = jnp.concatenate(m_new, axis=1)
        return carry

    lax.fori_loop(0, bt, body, 0)


def _od_step(h, gates, b_if, cos_t, sin_t, sc0, dc0, dn0, dm0, c_norm_g, d_norm_g, bt):
    DB, W = h.shape
    bt = min(bt, DB)
    mat = pl.BlockSpec((bt, N_HEADS, HEAD_W, HEAD_W), lambda i: (i, 0, 0, 0))
    nblk = pl.BlockSpec((bt, N_HEADS, HEAD_W), lambda i: (i, 0, 0))
    mblk = pl.BlockSpec((bt, 1, N_HEADS), lambda i: (i, 0, 0))
    vec = pl.BlockSpec((1, LANES), lambda i: (0, 0))
    out = pl.BlockSpec((bt, 1, MIX_W), lambda i: (i, 0, 0))
    b_pad = jnp.zeros((1, LANES), F32).at[0, :2 * N_HEADS].set(b_if)
    oc, od, sc, dc, dn, dm = pl.pallas_call(
        functools.partial(_od_step_kernel, bt=bt),
        out_shape=(jax.ShapeDtypeStruct((DB, 1, MIX_W), F32), jax.ShapeDtypeStruct((DB, 1, MIX_W), F32),
                   jax.ShapeDtypeStruct(sc0.shape, F32), jax.ShapeDtypeStruct(dc0.shape, F32),
                   jax.ShapeDtypeStruct(dn0.shape, F32), jax.ShapeDtypeStruct((DB, 1, N_HEADS), F32)),
        grid=(DB // bt,),
        in_specs=[pl.BlockSpec((bt, 1, W), lambda i: (i, 0, 0)),
                  pl.BlockSpec((bt, 1, LANES), lambda i: (i, 0, 0)), vec, vec, vec,
                  mat, mat, nblk, mblk, vec, vec],
        out_specs=(out, out, mat, mat, nblk, mblk),
        compiler_params=_cparams(1),
        name="od_step",
    )(h.reshape(DB, 1, W), gates.reshape(DB, 1, LANES), b_pad, cos_t, sin_t, sc0, dc0, dn0,
      dm0.reshape(DB, 1, N_HEADS), c_norm_g.reshape(1, -1), d_norm_g.reshape(1, -1))
    return oc.reshape(DB, MIX_W), od.reshape(DB, MIX_W), sc, dc, dn, dm.reshape(DB, N_HEADS)


X_HALVES = DH_X // LANES
X_ROWS = N_HEADS * X_HALVES


def _xattn_sample_kernel(q_ref, mk_ref, mv_ref, o_ref, *, xb):
    for t in range(xb):
        _xattn_sample_row(q_ref.at[t], mk_ref.at[t], mv_ref.at[t], o_ref.at[t])


def _xattn_sample_row(q_ref, mk_ref, mv_ref, o_ref):
    q = q_ref[...] * (DH_X ** -0.5)
    n_cols = mk_ref.shape[0]
    q2 = jnp.concatenate([q[:, h * DH_X + c * LANES:h * DH_X + (c + 1) * LANES]
                          for c in range(X_HALVES) for h in range(N_HEADS)], axis=0)
    s2 = _bdot_nt(q2, mk_ref[...])
    row = lax.broadcasted_iota(jnp.int32, (X_ROWS, n_cols), 0)
    col = lax.broadcasted_iota(jnp.int32, (X_ROWS, n_cols), 1)
    own = (col & (X_ROWS - 1)) == row
    s2 = jnp.where(own, s2, 0.0)
    s4 = s2[:N_HEADS] + pltpu.roll(s2, n_cols - N_HEADS, 1)[N_HEADS:]
    own4 = own[:N_HEADS]
    s4 = jnp.where(own4, s4, NEG_INF)
    p4 = jnp.exp(s4 - jnp.max(s4, axis=1, keepdims=True))
    l4 = jnp.sum(p4, axis=1, keepdims=True)
    p4 = p4 / l4
    p2 = jnp.concatenate([p4, pltpu.roll(p4, N_HEADS, 1)], axis=0)
    o2 = _bdot(p2, mv_ref[...])
    o_ref[...] = jnp.concatenate([o2[c * N_HEADS + h:c * N_HEADS + h + 1, :]
                                  for h in range(N_HEADS) for c in range(X_HALVES)], axis=1)


def _xattn_sample(q, mk, mv, base):
    DB = q.shape[0]
    n_rows = mk.shape[1]
    xb = math.gcd(DB, 4)
    tok = pl.BlockSpec((xb, 1, D_MODEL), lambda b: (b, 0, 0))
    mem = pl.BlockSpec((xb, n_rows, LANES), lambda b: (base // xb + b, 0, 0))
    assert base % xb == 0
    out = pl.pallas_call(
        functools.partial(_xattn_sample_kernel, xb=xb),
        out_shape=jax.ShapeDtypeStruct((DB, 1, D_MODEL), F32),
        grid=(DB // xb,),
        in_specs=[tok, mem, mem],
        out_specs=tok,
        compiler_params=_cparams(1),
        name="xattn_sample",
    )(q.reshape(DB, 1, D_MODEL), mk, mv)
    return out.reshape(DB, D_MODEL)


def _mem_rows(cache):
    n_l, DB, n_mem = cache.shape[:3]
    c = cache.reshape(n_l, DB, n_mem, N_HEADS, X_HALVES, LANES).transpose(0, 1, 2, 4, 3, 5)
    return c.reshape(n_l * DB, n_mem * X_ROWS, LANES)


def _ffn_sample_kernel(x_ref, up_ref, gate_ref, b0_ref, b1_ref, cw_ref, cb_ref, wd_ref, g_ref, b_ref, o_ref, acc_ref):
    j = pl.program_id(0)
    conv = (cb_ref[...] + cw_ref[0:1, :] * b0_ref[...] + cw_ref[1:2, :] * b1_ref[...]
            + cw_ref[2:3, :] * up_ref[...])
    y = _bdot(jax.nn.gelu(conv) * gate_ref[...], wd_ref[...])

    @pl.when(j == 0)
    def _():
        acc_ref[...] = y

    @pl.when(j > 0)
    def _():
        acc_ref[...] = acc_ref[...] + y

    @pl.when(j == pl.num_programs(0) - 1)
    def _():
        o_ref[...] = _layernorm(DN_ALPHA * x_ref[...] + acc_ref[...], g_ref[...], b_ref[...])


def _ffn_sample(x, ug, buf0, buf1, conv_w, conv_b, w_down, g, b, tf):
    DB = x.shape[0]
    d_ff = w_down.shape[0]
    nf = d_ff // tf
    ff = lambda off: pl.BlockSpec((DB, tf), lambda j: (0, off + j))
    full = lambda shape: pl.BlockSpec(shape, lambda j: (0,) * len(shape))
    return pl.pallas_call(
        _ffn_sample_kernel,
        out_shape=jax.ShapeDtypeStruct((DB, D_MODEL), F32),
        grid=(nf,),
        in_specs=[full((DB, D_MODEL)), ff(0), ff(nf), ff(0), ff(0),
                  pl.BlockSpec((3, tf), lambda j: (0, j)), pl.BlockSpec((1, tf), lambda j: (0, j)),
                  pl.BlockSpec((tf, D_MODEL), lambda j: (j, 0)), full((1, D_MODEL)), full((1, D_MODEL))],
        out_specs=full((DB, D_MODEL)),
        scratch_shapes=[pltpu.VMEM((DB, D_MODEL), F32)],
        compiler_params=_cparams(1),
        name="ffn_sample",
    )(x, ug, ug, buf0, buf1, conv_w, conv_b.reshape(1, -1), w_down, g.reshape(1, -1), b.reshape(1, -1))


def kernel(x_prompt, x_sample, cache_a_k, cache_a_v, state_b, state_c, state_d_c, state_d_n, state_d_m,
           cache_mem_k, cache_mem_v, state_conv, page_table, mem_prompt,
           ev_w_in, ev_w_out, ev_lam, ev_subln_g, ev_lb_logits, ev_b_norm_g,
           od_w_in, od_b_if, od_w_out, od_c_norm_g, od_d_norm_g,
           ln_g, ln_b, xa_wq, xa_wkv, xa_wo, ffn_w_up, ffn_conv_w, ffn_conv_b, ffn_w_down):
    B, T, _ = x_prompt.shape
    DB, t_s, _ = x_sample.shape
    assert t_s == 1, "the sample group is a single decoding step"
    assert T % CHUNK == 0
    n_pool, page = cache_a_k.shape[1], cache_a_k.shape[2]
    n_pages = page_table.shape[1]
    past = n_pages * page
    n_mem = mem_prompt.shape[1]
    d_ff = ffn_w_down.shape[1]
    tf = d_ff // 2
    N = B * T
    TM = 512

    pos_p = jnp.arange(T)
    pos_s = jnp.full((DB,), past, jnp.int32)
    lb_table = jnp.cumsum(jax.nn.softmax(ev_lb_logits.astype(F32), axis=0), axis=0)

    xp = x_prompt.reshape(N, D_MODEL)
    xs = x_sample.reshape(DB, D_MODEL)
    outs = {k: [] for k in ("ak_p", "av_p", "ak_s", "av_s", "sb_p", "sb_s", "sc_p", "sc_s", "dc_p", "dc_s",
                            "dn_p", "dn_s", "dm_p", "dm_s", "mk_p", "mv_p", "cv_p", "cv_s")}
    swap = lambda s: jnp.swapaxes(s, -1, -2)
    mem_k_rows, mem_v_rows = _mem_rows(cache_mem_k), _mem_rows(cache_mem_v)

    for l in range(DEPTH):
        j = l // 2
        if l % 2 == 0:
            lam_init = 0.8 - 0.6 * math.exp(-0.3 * l)
            w_in = ev_w_in[j].astype(BF16)
            w_out = ev_w_out[j].astype(BF16)
            h = _matmul(xp, w_in, TM, w_in.shape[1], "ev_in_prompt")
            cos_t, sin_t = _rope_tables(pos_p, DH_A)
            h3 = h.reshape(B, T, -1)
            qt, kb, kt32, vt, v_rows = _ev_prep_prompt(h3, cos_t, sin_t, TM)
            o_a = _dattn_prompt(qt, kb, vt, ev_lam[j], ev_subln_g[j], lam_init, 512)
            o_b, st = _hgrn_prompt(h3, lb_table[j], ev_b_norm_g[j], 512)
            outs["ak_p"].append(kt32.reshape(B, N_HEADS, 2, DH_A, T).transpose(0, 4, 1, 2, 3))
            outs["av_p"].append(v_rows.reshape(B, T, N_HEADS, HEAD_W))
            outs["sb_p"].append(swap(st))
            xp = _proj_ln([o_a.reshape(N, MIX_W), o_b.reshape(N, MIX_W)], [w_out[:MIX_W], w_out[MIX_W:]],
                          xp, ln_g[l, 0], ln_b[l, 0], TM, "ev_out_prompt")
            hs = _matmul(xs, w_in, DB, 512, "ev_in_sample")
            cos_s, sin_s = _rope_tables(pos_s, DH_A)
            qs, ks32, kts32 = _ev_prep_sample(hs, cos_s, sin_s)
            vs32 = hs[:, 2 * MIX_W:3 * MIX_W]
            kt_pool = cache_a_k.transpose(0, 1, 3, 4, 5, 2).reshape(-1, MIX_W, page)
            v_pool = cache_a_v.reshape(-1, page * N_HEADS, HEAD_W)
            oa_s = _dattn_sample(qs, ks32, vs32, kt_pool, v_pool, j * n_pool, page_table,
                                 ev_lam[j], ev_subln_g[j], lam_init)
            ob_s, sb_s = _hgrn_step(hs, state_b[j], lb_table[j], ev_b_norm_g[j], 8)
            outs["ak_s"].append(kts32.reshape(N_HEADS, 2, DH_A, DB).transpose(3, 0, 1, 2)[:, None])
            outs["av_s"].append(vs32.reshape(DB, 1, N_HEADS, HEAD_W))
            outs["sb_s"].append(sb_s)
            xs = _proj_ln([oa_s, ob_s], [w_out[:MIX_W], w_out[MIX_W:]], xs, ln_g[l, 0], ln_b[l, 0], DB,
                          "ev_out_sample")
        else:
            n_main = 8 * MIX_W
            w_in = od_w_in[j][:, :n_main].astype(BF16)
            w_gate = jnp.pad(od_w_in[j][:, n_main:], ((0, 0), (0, LANES - 2 * N_HEADS))).astype(BF16)
            w_out = od_w_out[j].astype(BF16)
            h = _matmul(xp, w_in, TM, w_in.shape[1], "od_in_prompt")
            gates = _matmul(xp, w_gate, 1024, LANES, "od_gates_prompt")[:, :2 * N_HEADS]
            cos_t, sin_t = _rope_tables(pos_p, HEAD_W)
            o_c, o_d, sct, dct, dn, dm = _od_prompt(h.reshape(B, T, -1), gates.reshape(B, T, -1), od_b_if[j],
                                                    cos_t, sin_t, od_c_norm_g[j], od_d_norm_g[j], 512)
            outs["sc_p"].append(swap(sct))
            outs["dc_p"].append(swap(dct))
            outs["dn_p"].append(dn[:, :N_HEADS, :])
            outs["dm_p"].append(dm[:, :N_HEADS, 0])
            xp = _proj_ln([o_c.reshape(N, MIX_W), o_d.reshape(N, MIX_W)], [w_out[:MIX_W], w_out[MIX_W:]],
                          xp, ln_g[l, 0], ln_b[l, 0], TM, "od_out_prompt")
            hs = _matmul(xs, w_in, DB, 512, "od_in_sample")
            gates_s = _matmul(xs, w_gate, DB, LANES, "od_gates_sample")
            cos_s, sin_s = _rope_tables(pos_s[:1], HEAD_W)
            oc_s, od_s, sc_s, dc_s, dn_s, dm_s = _od_step(hs, gates_s, od_b_if[j], cos_s, sin_s,
                                                          state_c[j], state_d_c[j], state_d_n[j], state_d_m[j],
                                                          od_c_norm_g[j], od_d_norm_g[j], 8)
            outs["sc_s"].append(sc_s)
            outs["dc_s"].append(dc_s)
            outs["dn_s"].append(dn_s)
            outs["dm_s"].append(dm_s)
            xs = _proj_ln([oc_s, od_s], [w_out[:MIX_W], w_out[MIX_W:]], xs, ln_g[l, 0], ln_b[l, 0], DB,
                          "od_out_sample")

        wq = xa_wq[l].astype(BF16)
        wo = xa_wo[l].astype(BF16)
        mkv = _matmul(mem_prompt.reshape(B * n_mem, D_MODEL), xa_wkv[l].astype(BF16), 512, 512, "mem_kv")
        mk, mv = mkv[:, :D_MODEL], mkv[:, D_MODEL:]
        outs["mk_p"].append(mk.reshape(B, n_mem, N_HEADS, DH_X))
        outs["mv_p"].append(mv.reshape(B, n_mem, N_HEADS, DH_X))
        xp = _xattn_prompt(xp, wq, mk.astype(BF16).reshape(B, n_mem, D_MODEL),
                           mv.astype(BF16).reshape(B, n_mem, D_MODEL), wo, ln_g[l, 1], ln_b[l, 1], T, TM)
        q_s = _matmul(xs, wq, DB, 512, "xattn_q_sample")
        xo_s = _xattn_sample(q_s, mem_k_rows, mem_v_rows, l * DB)
        xs = _proj_ln([xo_s], [wo], xs, ln_g[l, 1], ln_b[l, 1], DB, "xattn_out_sample")

        w_up = ffn_w_up[l].astype(BF16)
        w_down = ffn_w_down[l].astype(BF16)
        tm_f = min(TM, T)
        xp, tails = _ffn_prompt(xp, w_up, ffn_conv_w[l], ffn_conv_b[l], w_down, ln_g[l, 2], ln_b[l, 2], T, TM, tf)
        tails = tails.reshape(B, T // tm_f, SUBLANES, d_ff)
        outs["cv_p"].append(tails[:, -1, SUBLANES - 2:, :])
        ug_s = _matmul(xs, w_up, DB, tf, "ffn_up_sample")
        buf = state_conv[l]
        xs = _ffn_sample(xs, ug_s, buf[:, 0, :], buf[:, 1, :], ffn_conv_w[l], ffn_conv_b[l], w_down,
                         ln_g[l, 2], ln_b[l, 2], tf)
        outs["cv_s"].append(jnp.stack([buf[:, 1, :], ug_s[:, :d_ff]], axis=1))

    st = lambda k: jnp.stack(outs[k])
    return (xp.reshape(B, T, D_MODEL), xs.reshape(DB, 1, D_MODEL),
            st("ak_p"), st("av_p"), st("ak_s"), st("av_s"), st("sb_p"), st("sb_s"),
            st("sc_p"), st("sc_s"), st("dc_p"), st("dc_s"), st("dn_p"), st("dn_s"), st("dm_p"), st("dm_s"),
            st("mk_p"), st("mv_p"), st("cv_p"), st("cv_s"))
```

```python
import functools
import math

import numpy as np
import jax
import jax.numpy as jnp
from jax import lax
from jax.experimental import pallas as pl
from jax.experimental.pallas import tpu as pltpu

F32 = jnp.float32
BF16 = jnp.bfloat16

D_MODEL = 1024
MIX_W = D_MODEL // 2
N_HEADS = 4
HEAD_W = MIX_W // N_HEADS
DH_A = HEAD_W // 2
DH_X = D_MODEL // N_HEADS
CHUNK = 64
OD_CHUNK = 128
ROPE_THETA = 10000.0
NORM_EPS = 1e-5
DEPTH = 2
DN_ALPHA = (2.0 * DEPTH) ** 0.25
LANES = 128
SUBLANES = 8
VMEM_LIMIT = 56 * 1024 * 1024
NEG_INF = float("-inf")

NT_DIMS = (((1,), (1,)), ((), ()))
TN_DIMS = (((0,), (0,)), ((), ()))


def _cparams(n_axes, vmem=VMEM_LIMIT):
    return pltpu.CompilerParams(dimension_semantics=("arbitrary",) * n_axes, vmem_limit_bytes=vmem)


def _bdot(a, b):
    return jnp.dot(a.astype(BF16), b.astype(BF16), preferred_element_type=F32)


def _bdot_nt(a, b):
    return lax.dot_general(a.astype(BF16), b.astype(BF16), NT_DIMS, preferred_element_type=F32)


def _bdot_tn(a, b):
    return lax.dot_general(a.astype(BF16), b.astype(BF16), TN_DIMS, preferred_element_type=F32)


def _split3(x):
    p1 = x.astype(BF16)
    r1 = x - p1.astype(F32)
    p2 = r1.astype(BF16)
    p3 = (r1 - p2.astype(F32)).astype(BF16)
    return p1, p2, p3


def _cumsum_rows(tri, x):
    p1, p2, p3 = _split3(x)
    d = functools.partial(jnp.dot, preferred_element_type=F32)
    return d(tri, p1) + d(tri, p2) + d(tri, p3)


def _cumsum_lanes(x, triu):
    p1, p2, p3 = _split3(x)
    d = functools.partial(jnp.dot, preferred_element_type=F32)
    return d(p1, triu) + d(p2, triu) + d(p3, triu)


def _tri(L, lower):
    r = lax.broadcasted_iota(jnp.int32, (L, L), 0)
    c = lax.broadcasted_iota(jnp.int32, (L, L), 1)
    return (r >= c) if lower else (r <= c)


def _sigmoid(x):
    return 1.0 / (1.0 + jnp.exp(-x))


def _silu(x):
    return x * _sigmoid(x)


def _log_sigmoid(x):
    return jnp.minimum(x, 0.0) - jnp.log(1.0 + jnp.exp(-jnp.abs(x)))


def _layernorm(z, g, b):
    mu = jnp.mean(z, -1, keepdims=True)
    zc = z - mu
    var = jnp.mean(zc * zc, -1, keepdims=True)
    return zc * lax.rsqrt(var + NORM_EPS) * g + b


def _rmsnorm(x, g):
    return x * lax.rsqrt(jnp.mean(x * x, -1, keepdims=True) + NORM_EPS) * g


def _groupnorm(x, g):
    mu = jnp.mean(x, -1, keepdims=True)
    xc = x - mu
    var = jnp.mean(xc * xc, -1, keepdims=True)
    return xc * lax.rsqrt(var + NORM_EPS) * g


def _rope(x, cos, sin, half):
    outs = []
    for c in range(x.shape[1] // LANES):
        xc = x[:, c * LANES:(c + 1) * LANES]
        if 2 * half == LANES:
            sw = pltpu.roll(xc, half, 1)
        else:
            lane = lax.broadcasted_iota(jnp.int32, xc.shape, 1)
            first = (lane & (2 * half - 1)) < half
            sw = jnp.where(first, pltpu.roll(xc, LANES - half, 1), pltpu.roll(xc, half, 1))
        outs.append(xc * cos + sw * sin)
    return outs[0] if len(outs) == 1 else jnp.concatenate(outs, axis=1)


def _rope_tables(pos, d):
    inv = ROPE_THETA ** (-jnp.arange(0, d // 2, dtype=F32) * 2.0 / d)
    ang = pos.astype(F32)[:, None] * inv[None, :]
    cos, sin = jnp.cos(ang), jnp.sin(ang)
    reps = LANES // d
    cos_t = jnp.tile(jnp.concatenate([cos, cos], -1), (1, reps))
    sin_t = jnp.tile(jnp.concatenate([-sin, sin], -1), (1, reps))
    return cos_t, sin_t


def _col(row):
    return jnp.broadcast_to(row, (LANES, LANES)).T


def _mm_kernel(x_ref, w_ref, o_ref, xb_ref):
    @pl.when(pl.program_id(1) == 0)
    def _():
        xb_ref[...] = x_ref[...].astype(BF16)

    o_ref[...] = jnp.dot(xb_ref[...], w_ref[...], preferred_element_type=F32).astype(o_ref.dtype)


def _matmul(x, w, tm, tn, name, out_dtype=F32):
    M, K = x.shape
    N = w.shape[1]
    tm, tn = min(tm, M), min(tn, N)
    return pl.pallas_call(
        _mm_kernel,
        out_shape=jax.ShapeDtypeStruct((M, N), out_dtype),
        grid=(M // tm, N // tn),
        in_specs=[pl.BlockSpec((tm, K), lambda i, j: (i, 0)),
                  pl.BlockSpec((K, tn), lambda i, j: (0, j))],
        out_specs=pl.BlockSpec((tm, tn), lambda i, j: (i, j)),
        scratch_shapes=[pltpu.VMEM((tm, K), BF16)],
        compiler_params=_cparams(2),
        name=name,
    )(x, w)


_HEAD_NORMS = {"rms": _rmsnorm, "group": _groupnorm}
_GATE_ACTS = {"silu": _silu, "sigmoid": _sigmoid}


def _proj_ln_kernel(*refs, n_in, gated):
    a_refs, w_refs = refs[:n_in], refs[n_in:2 * n_in]
    x_ref, g_ref, b_ref = refs[2 * n_in:2 * n_in + 3]
    extra, o_ref = refs[2 * n_in + 3:-1], refs[-1]
    y = None
    for a_ref, w_ref, gate in zip(a_refs, w_refs, gated):
        a = a_ref[...]
        if gate is not None:
            norm, act = _HEAD_NORMS[gate[0]], _GATE_ACTS[gate[1]]
            gain_ref, gate_ref, extra = extra[0], extra[1], extra[2:]
            a = jnp.concatenate([norm(_head(a, h), gain_ref[...]) for h in range(N_HEADS)], axis=1)
            a = a * act(gate_ref[...])
        ya = _bdot(a, w_ref[...])
        y = ya if y is None else y + ya
    o_ref[...] = _layernorm(DN_ALPHA * x_ref[...] + y, g_ref[...], b_ref[...])


def _proj_ln(acts, weights, x, g, b, tm, name, gates=None):
    M = x.shape[0]
    tm = min(tm, M)
    n_in = len(acts)
    gates = gates or [None] * n_in
    in_specs = [pl.BlockSpec((tm, a.shape[1]), lambda i: (i, 0)) for a in acts]
    in_specs += [pl.BlockSpec(w.shape, lambda i: (0, 0)) for w in weights]
    in_specs += [pl.BlockSpec((tm, D_MODEL), lambda i: (i, 0)),
                 pl.BlockSpec((1, D_MODEL), lambda i: (0, 0)),
                 pl.BlockSpec((1, D_MODEL), lambda i: (0, 0))]
    extra = []
    for gate in gates:
        if gate is not None:
            _, gain, _, gate_arr, blk = gate
            in_specs += [pl.BlockSpec((1, HEAD_W), lambda i: (0, 0)),
                         pl.BlockSpec((tm, MIX_W), lambda i, blk=blk: (i, blk))]
            extra += [gain.reshape(1, -1), gate_arr]
    return pl.pallas_call(
        functools.partial(_proj_ln_kernel, n_in=n_in,
                          gated=tuple(None if gt is None else (gt[0], gt[2]) for gt in gates)),
        out_shape=jax.ShapeDtypeStruct((M, D_MODEL), F32),
        grid=(M // tm,),
        in_specs=in_specs,
        out_specs=pl.BlockSpec((tm, D_MODEL), lambda i: (i, 0)),
        compiler_params=_cparams(1),
        name=name,
    )(*acts, *weights, x, g.reshape(1, -1), b.reshape(1, -1), *extra)


Q_SCALE = DH_A ** -0.5 * math.log2(math.e)
VT_ROWS = HEAD_W + 16


def _ev_in_prompt_kernel(x_ref, w_ref, cos_ref, sin_ref, qt_ref, kb_ref, kt_ref, vt_ref, vr_ref, hb_ref):
    xb = x_ref[0].astype(BF16)
    cos, sin = cos_ref[...], sin_ref[...]
    proj = lambda lo, hi: jnp.dot(xb, w_ref[:, lo * MIX_W:hi * MIX_W], preferred_element_type=F32)
    hb_ref[0] = proj(3, 7)
    q = _rope(proj(0, 1), cos, sin, DH_A // 2)
    k = _rope(proj(1, 2), cos, sin, DH_A // 2)
    qt_ref[0] = (q * Q_SCALE).T.astype(BF16)
    kb_ref[0] = k.astype(BF16)
    kt_ref[0] = k.T
    v = proj(2, 3)
    tm = v.shape[0]
    vt = v.T.astype(BF16)
    ones = jnp.ones((VT_ROWS - HEAD_W, tm), BF16)
    for h in range(N_HEADS):
        vt_ref[0, h, :HEAD_W, :] = vt[h * HEAD_W:(h + 1) * HEAD_W]
        vt_ref[0, h, HEAD_W:, :] = ones
        vr_ref[0, pl.ds(h, tm, stride=N_HEADS), :] = _head(v, h)


def _ev_in_prompt(x3, w_in, cos_t, sin_t, tm):
    B, T, _ = x3.shape
    tm = min(tm, T)
    tab = pl.BlockSpec((tm, LANES), lambda b, i: (i, 0))
    tr = pl.BlockSpec((1, MIX_W, tm), lambda b, i: (b, 0, i))
    tshape = lambda dt: jax.ShapeDtypeStruct((B, MIX_W, T), dt)
    return pl.pallas_call(
        _ev_in_prompt_kernel,
        out_shape=(tshape(BF16), jax.ShapeDtypeStruct((B, T, MIX_W), BF16), tshape(F32),
                   jax.ShapeDtypeStruct((B, N_HEADS, VT_ROWS, T), BF16),
                   jax.ShapeDtypeStruct((B, T * N_HEADS, HEAD_W), F32),
                   jax.ShapeDtypeStruct((B, T, 4 * MIX_W), F32)),
        grid=(B, T // tm),
        in_specs=[pl.BlockSpec((1, tm, D_MODEL), lambda b, i: (b, i, 0)),
                  pl.BlockSpec(w_in.shape, lambda b, i: (0, 0)), tab, tab],
        out_specs=(tr, pl.BlockSpec((1, tm, MIX_W), lambda b, i: (b, i, 0)), tr,
                   pl.BlockSpec((1, N_HEADS, VT_ROWS, tm), lambda b, i: (b, 0, 0, i)),
                   pl.BlockSpec((1, tm * N_HEADS, HEAD_W), lambda b, i: (b, i, 0)),
                   pl.BlockSpec((1, tm, 4 * MIX_W), lambda b, i: (b, i, 0))),
        compiler_params=_cparams(2),
        name="ev_in_prompt",
    )(x3, w_in, cos_t, sin_t)


def _ev_prep_sample_kernel(qk_ref, cos_ref, sin_ref, q_ref, k_ref, kt_ref):
    cos, sin = cos_ref[...], sin_ref[...]
    k = _rope(qk_ref[:, MIX_W:], cos, sin, DH_A // 2)
    q_ref[...] = _rope(qk_ref[:, :MIX_W], cos, sin, DH_A // 2) * Q_SCALE
    k_ref[...] = k
    kt_ref[...] = k.T


def _ev_prep_sample(h, cos_t, sin_t):
    DB = h.shape[0]
    full = lambda shape: pl.BlockSpec(shape, lambda i: (0,) * len(shape))
    return pl.pallas_call(
        _ev_prep_sample_kernel,
        out_shape=(jax.ShapeDtypeStruct((DB, MIX_W), F32), jax.ShapeDtypeStruct((DB, MIX_W), F32),
                   jax.ShapeDtypeStruct((MIX_W, DB), F32)),
        grid=(1,),
        in_specs=[full((DB, 2 * MIX_W)), full((DB, LANES)), full((DB, LANES))],
        out_specs=(full((DB, MIX_W)), full((DB, MIX_W)), full((MIX_W, DB))),
        compiler_params=_cparams(1),
        name="ev_prep_sample",
    )(h, cos_t, sin_t)


def _lambda(lam_ref, lam_init):
    lp = lam_ref[...]
    s01 = jnp.sum(lp[0:1] * lp[1:2], axis=1, keepdims=True)
    s23 = jnp.sum(lp[2:3] * lp[3:4], axis=1, keepdims=True)
    return jnp.exp(s01) - jnp.exp(s23) + lam_init


def _dattn_kernel(lam_ref, g_ref, qt_ref, k_ref, vt_ref, o_ref, m_ref, acc_ref, *, tq, lam_init):
    i = pl.program_id(2)
    qt = qt_ref[0]
    sub = lax.broadcasted_iota(jnp.int32, qt.shape, 0)
    zero = jnp.zeros_like(qt)
    q_maps = [jnp.where(sub < DH_A, qt, zero), jnp.where(sub >= DH_A, qt, zero)]
    maps = range(2)
    m_ref[...] = jnp.full(m_ref.shape, NEG_INF, F32)
    acc_ref[...] = jnp.zeros(acc_ref.shape, F32)

    def steps(js, masked):
        ks, vts, ss = [], [], []
        for j in js:
            start = pl.multiple_of(j * tq, tq)
            ks.append(k_ref[0, pl.ds(start, tq), :])
            vts.append(vt_ref[0, 0, :, pl.ds(start, tq)])
        for kj in ks:
            ss.append([jnp.dot(kj, q_maps[c], preferred_element_type=F32) for c in maps])
        for n in range(len(js)):
            for c in maps:
                s = ss[n][c]
                if masked:
                    r = lax.broadcasted_iota(jnp.int32, s.shape, 0)
                    col = lax.broadcasted_iota(jnp.int32, s.shape, 1)
                    s = jnp.where(r <= col, s, NEG_INF)
                m_prev = m_ref[c]
                m_new = jnp.maximum(m_prev, jnp.max(s, axis=0, keepdims=True))
                alpha = jnp.exp2(m_prev - m_new)
                p = jnp.exp2(s - m_new)
                acc_ref[c] = alpha * acc_ref[c] + jnp.dot(vts[n], p.astype(BF16), preferred_element_type=F32)
                m_ref[c] = m_new

    def body(jj, carry):
        steps([2 * jj, 2 * jj + 1], False)
        return carry

    lax.fori_loop(0, lax.shift_right_logical(i, 1), body, 0)

    @pl.when((i & 1) == 1)
    def _():
        steps([i - 1], False)

    steps([i], True)
    lam = _lambda(lam_ref, lam_init)
    o = [acc_ref[c, :HEAD_W, :] / acc_ref[c, HEAD_W:HEAD_W + 1, :] for c in maps]
    d = (o[0] - lam * o[1]).T
    o_ref[0] = _rmsnorm(d, g_ref[...]) * (1.0 - lam_init)


def _dattn_prompt(qt, kb, vt, lam_p, subln_g, lam_init, tq):
    B, T, _ = kb.shape
    tq = min(tq, T)
    return pl.pallas_call(
        functools.partial(_dattn_kernel, tq=tq, lam_init=lam_init),
        out_shape=jax.ShapeDtypeStruct((B, T, MIX_W), F32),
        grid=(B, N_HEADS, T // tq),
        in_specs=[pl.BlockSpec((4, DH_A), lambda b, h, i: (0, 0)),
                  pl.BlockSpec((1, HEAD_W), lambda b, h, i: (0, 0)),
                  pl.BlockSpec((1, HEAD_W, tq), lambda b, h, i: (b, h, i)),
                  pl.BlockSpec((1, T, HEAD_W), lambda b, h, i: (b, 0, h)),
                  pl.BlockSpec((1, 1, VT_ROWS, T), lambda b, h, i: (b, h, 0, 0))],
        out_specs=pl.BlockSpec((1, tq, HEAD_W), lambda b, h, i: (b, i, h)),
        scratch_shapes=[pltpu.VMEM((2, 1, tq), F32), pltpu.VMEM((2, VT_ROWS, tq), F32)],
        compiler_params=_cparams(3),
        name="dattn_prompt",
    )(lam_p, subln_g.reshape(1, -1), qt, kb, vt)


def _head(x, h):
    return x[:, h * HEAD_W:(h + 1) * HEAD_W]


def _gla_chunk(q_in, k_in, k_end, v, dec, st_ref, tril_mask):
    q_in, k_in, k_end, v = (x.astype(BF16) for x in (q_in, k_in, k_end, v))
    heads = range(N_HEADS)
    sts = [st_ref[h] for h in heads]
    inter = [_bdot_nt(_head(q_in, h), sts[h]) for h in heads]
    upd = [_bdot_tn(_head(v, h), _head(k_end, h)) for h in heads]
    attn = [jnp.where(tril_mask, _bdot_nt(_head(q_in, h), _head(k_in, h)), 0.0) for h in heads]
    intra = [_bdot(attn[h], _head(v, h)) for h in heads]
    for h in heads:
        st_ref[h] = sts[h] * (dec[h] if isinstance(dec, (list, tuple)) else _head(dec, h)) + upd[h]
    return [intra[h] + inter[h] for h in heads]


def _hgrn_kernel(q_ref, f_ref, i_ref, lb_ref, o_ref, s_ref, st_ref, *, tt, L):
    t = pl.program_id(1)

    @pl.when(t == 0)
    def _():
        st_ref[...] = jnp.zeros(st_ref.shape, F32)

    tril_mask = _tri(L, True)
    tril = tril_mask.astype(BF16)

    def body(c, carry):
        rows = pl.ds(pl.multiple_of(c * L, L), L)
        lb = lb_ref[...]
        f = lb + (1.0 - lb) * _sigmoid(f_ref[0, rows, :])
        b = _cumsum_rows(tril, jnp.log(f))
        b_end = b[L - 1:L, :]
        k = 1.0 - f
        q_in = _silu(q_ref[0, rows, :]) * jnp.exp(b)
        o = _gla_chunk(q_in, k * jnp.exp(-b), k * jnp.exp(b_end - b), i_ref[0, rows, :], jnp.exp(b_end),
                       st_ref, tril_mask)
        o_ref[0, rows, :] = jnp.concatenate(o, axis=1)
        return carry

    lax.fori_loop(0, tt // L, body, 0, unroll=2)

    @pl.when(t == pl.num_programs(1) - 1)
    def _():
        s_ref[0] = st_ref[...]


def _hgrn_prompt(h3, lb, tt):
    B, T, _ = h3.shape
    tt = min(tt, T)
    L = math.gcd(T, CHUNK)
    col = lambda c: pl.BlockSpec((1, tt, MIX_W), lambda b, t: (b, t, c))
    return pl.pallas_call(
        functools.partial(_hgrn_kernel, tt=tt, L=L),
        out_shape=(jax.ShapeDtypeStruct((B, T, MIX_W), F32),
                   jax.ShapeDtypeStruct((B, N_HEADS, HEAD_W, HEAD_W), F32)),
        grid=(B, T // tt),
        in_specs=[col(0), col(1), col(2), pl.BlockSpec((1, MIX_W), lambda b, t: (0, 0))],
        out_specs=(pl.BlockSpec((1, tt, MIX_W), lambda b, t: (b, t, 0)),
                   pl.BlockSpec((1, N_HEADS, HEAD_W, HEAD_W), lambda b, t: (b, 0, 0, 0))),
        scratch_shapes=[pltpu.VMEM((N_HEADS, HEAD_W, HEAD_W), F32)],
        compiler_params=_cparams(2),
        name="hgrn_prompt",
    )(h3, h3, h3, lb.reshape(1, -1))


def _log_gamma(h):
    return float(np.log(1.0 - 2.0 ** (-5.0 - h)))


def _od_kernel(cq_ref, ck_ref, cv_ref, dq_ref, dk_ref, dv_ref, gc_ref, gr_ref,
               bc_ref, br_ref, cos_ref, sin_ref,
               oc_ref, od_ref, sc_ref, dc_ref, dn_ref, dm_ref,
               st_ref, ct_ref, n_ref, m_ref, *, tt, L):
    t = pl.program_id(1)

    @pl.when(t == 0)
    def _():
        st_ref[...] = jnp.zeros(st_ref.shape, F32)
        ct_ref[...] = jnp.zeros(ct_ref.shape, F32)
        n_ref[...] = jnp.zeros(n_ref.shape, F32)
        m_ref[...] = jnp.zeros(m_ref.shape, F32)

    tril_mask = _tri(L, True)
    tril = tril_mask.astype(BF16)
    triu = _tri(L, False).astype(BF16)
    scale = HEAD_W ** -0.5
    heads = range(N_HEADS)
    pos1 = (lax.broadcasted_iota(jnp.int32, (L, HEAD_W), 0) + 1).astype(F32)
    ret_b = jnp.concatenate([pos1 * _log_gamma(h) for h in heads], axis=1)
    ret_b_end = jnp.concatenate([jnp.full((L, HEAD_W), L * _log_gamma(h), F32) for h in heads], axis=1)
    ret_q_dec, ret_k_dec, ret_k_end = jnp.exp(ret_b), jnp.exp(-ret_b), jnp.exp(ret_b_end - ret_b)
    ret_dec = [math.exp(L * _log_gamma(h)) for h in heads]

    def body(c, carry):
        rows = pl.ds(pl.multiple_of(c * L, L), L)
        cos, sin = cos_ref[rows, :], sin_ref[rows, :]
        q = _rope(cq_ref[0, rows, :], cos, sin, HEAD_W // 2)
        k = _rope(ck_ref[0, rows, :], cos, sin, HEAD_W // 2) * scale
        o = _gla_chunk(q * ret_q_dec, k * ret_k_dec, k * ret_k_end, cv_ref[0, rows, :], ret_dec, st_ref, tril_mask)
        oc_ref[0, rows, :] = jnp.concatenate(o, axis=1)
        g_c = gc_ref[0, rows, :] + bc_ref[...]
        g_r = gr_ref[0, c] + br_ref[...]
        lf_c = _log_sigmoid(g_c)
        b_c = _cumsum_rows(tril, jnp.concatenate(
            [jnp.broadcast_to(lf_c[:, N_HEADS + h:N_HEADS + h + 1], (L, HEAD_W)) for h in heads], axis=1))
        b_r = _cumsum_lanes(_log_sigmoid(g_r), triu)
        dq = (dq_ref[0, rows, :] * scale)
        dk = dk_ref[0, rows, :]
        dqb, dkb, dvb = dq.astype(BF16), dk.astype(BF16), dv_ref[0, rows, :].astype(BF16)
        cts = [ct_ref[h] for h in heads]
        q_c = [_bdot_nt(_head(dqb, h), cts[h]) for h in heads]
        q_k = [_bdot_nt(_head(dqb, h), _head(dkb, h)) for h in heads]
        m_prev = [m_ref[h:h + 1, 0:1] for h in heads]
        bc1, m_t, w_mat = [], [], []
        for h in heads:
            bc = _head(b_c, h)[:, :L]
            dm = jnp.where(tril_mask, bc - b_r[N_HEADS + h:N_HEADS + h + 1, :] + g_r[h:h + 1, :], NEG_INF)
            bc1.append(bc[:, 0:1])
            m_t.append(jnp.maximum(bc1[h] + m_prev[h], jnp.max(dm, axis=1, keepdims=True)))
            w_mat.append(jnp.exp(dm - m_t[h]) * q_k[h])
        w_v = [_bdot(w_mat[h], _head(dvb, h)) for h in heads]
        kws, c_scales = [], []
        for h in heads:
            m_new = m_t[h][L - 1:L, :]
            b_last = bc1[h][L - 1:L, :]
            c_scales.append(jnp.exp(b_last + m_prev[h] - m_new))
            kws.append(_head(dk, h) * jnp.exp(b_last - bc1[h] + g_c[:, h:h + 1] - m_new))
            m_ref[h:h + 1, :] = jnp.broadcast_to(m_new, (1, LANES))
        c_upd = [_bdot_tn(_head(dvb, h), kws[h]) for h in heads]
        hs = []
        for h in heads:
            n_row = n_ref[h:h + 1, :]
            inter = jnp.exp(bc1[h] + m_prev[h] - m_t[h])
            num = inter * q_c[h] + w_v[h]
            den = (inter * jnp.sum(_head(dq, h) * n_row, axis=1, keepdims=True)
                   + jnp.sum(w_mat[h], axis=1, keepdims=True))
            hs.append(num / jnp.maximum(jnp.abs(den), jnp.exp(-m_t[h])))
            ct_ref[h] = c_scales[h] * cts[h] + c_upd[h]
            n_ref[h:h + 1, :] = c_scales[h] * n_row + jnp.sum(kws[h], axis=0, keepdims=True)
        od_ref[0, rows, :] = jnp.concatenate(hs, axis=1)
        return carry

    lax.fori_loop(0, tt // L, body, 0)

    @pl.when(t == pl.num_programs(1) - 1)
    def _():
        sc_ref[0] = st_ref[...]
        dc_ref[0] = ct_ref[...]
        dn_ref[0] = n_ref[...]
        dm_ref[0] = m_ref[...]


def _od_in_kernel(x_ref, w_ref, wg_ref, h_ref, g_ref, xb_ref):
    @pl.when(pl.program_id(1) == 0)
    def _():
        xb_ref[...] = x_ref[...].astype(BF16)

    xb = xb_ref[...]
    h_ref[...] = jnp.dot(xb, w_ref[...], preferred_element_type=F32)

    @pl.when(pl.program_id(1) == 0)
    def _():
        g_ref[...] = jnp.dot(xb, wg_ref[...], preferred_element_type=F32)


def _od_in_prompt(x, w_in, w_gate, tm, tn):
    M, K = x.shape
    N = w_in.shape[1]
    tm = min(tm, M)
    return pl.pallas_call(
        _od_in_kernel,
        out_shape=(jax.ShapeDtypeStruct((M, N), F32), jax.ShapeDtypeStruct((M, LANES), F32)),
        grid=(M // tm, N // tn),
        in_specs=[pl.BlockSpec((tm, K), lambda i, j: (i, 0)),
                  pl.BlockSpec((K, tn), lambda i, j: (0, j)),
                  pl.BlockSpec((K, LANES), lambda i, j: (0, 0))],
        out_specs=(pl.BlockSpec((tm, tn), lambda i, j: (i, j)), pl.BlockSpec((tm, LANES), lambda i, j: (i, 0))),
        scratch_shapes=[pltpu.VMEM((tm, K), BF16)],
        compiler_params=_cparams(2),
        name="od_in_prompt",
    )(x, w_in, w_gate)


def _od_prompt(h3, gates, b_if, cos_t, sin_t, tt):
    B, T, _ = h3.shape
    tt = min(tt, T)
    L = math.gcd(T, OD_CHUNK)
    gates_r = gates[:, :, :2 * N_HEADS].reshape(B, T // L, L, 2 * N_HEADS).transpose(0, 1, 3, 2)
    b_pad = jnp.zeros((1, LANES), F32).at[0, :2 * N_HEADS].set(b_if)
    col = lambda c: pl.BlockSpec((1, tt, MIX_W), lambda b, t: (b, t, c))
    tab = pl.BlockSpec((tt, LANES), lambda b, t: (t, 0))
    mat_state = pl.BlockSpec((1, N_HEADS, HEAD_W, HEAD_W), lambda b, t: (b, 0, 0, 0))
    row_state = pl.BlockSpec((1, SUBLANES, LANES), lambda b, t: (b, 0, 0))
    out_blk = pl.BlockSpec((1, tt, MIX_W), lambda b, t: (b, t, 0))
    return pl.pallas_call(
        functools.partial(_od_kernel, tt=tt, L=L),
        out_shape=(jax.ShapeDtypeStruct((B, T, MIX_W), F32), jax.ShapeDtypeStruct((B, T, MIX_W), F32),
                   jax.ShapeDtypeStruct((B, N_HEADS, HEAD_W, HEAD_W), F32),
                   jax.ShapeDtypeStruct((B, N_HEADS, HEAD_W, HEAD_W), F32),
                   jax.ShapeDtypeStruct((B, SUBLANES, LANES), F32),
                   jax.ShapeDtypeStruct((B, SUBLANES, LANES), F32)),
        grid=(B, T // tt),
        in_specs=[col(0), col(1), col(2), col(4), col(5), col(6),
                  pl.BlockSpec((1, tt, LANES), lambda b, t: (b, t, 0)),
                  pl.BlockSpec((1, tt // L, 2 * N_HEADS, L), lambda b, t: (b, t, 0, 0)),
                  pl.BlockSpec((1, LANES), lambda b, t: (0, 0)),
                  pl.BlockSpec((2 * N_HEADS, 1), lambda b, t: (0, 0)),
                  tab, tab],
        out_specs=(out_blk, out_blk, mat_state, mat_state, row_state, row_state),
        scratch_shapes=[pltpu.VMEM((N_HEADS, HEAD_W, HEAD_W), F32), pltpu.VMEM((N_HEADS, HEAD_W, HEAD_W), F32),
                        pltpu.VMEM((SUBLANES, LANES), F32), pltpu.VMEM((SUBLANES, LANES), F32)],
        compiler_params=_cparams(2),
        name="od_prompt",
    )(h3, h3, h3, h3, h3, h3, gates, gates_r, b_pad, b_if.reshape(-1, 1), cos_t, sin_t)


def _xattn_kernel(x_ref, wq_ref, mk_ref, mv_ref, wo_ref, g_ref, b_ref, o_ref):
    x = x_ref[...]
    q = _bdot(x, wq_ref[...])
    qb = (q * (DH_X ** -0.5)).astype(BF16)
    heads = range(N_HEADS)
    cols = [slice(h * DH_X, (h + 1) * DH_X) for h in heads]
    s = [lax.dot_general(qb[:, cols[h]], mk_ref[0, :, cols[h]], NT_DIMS, preferred_element_type=F32) for h in heads]
    p = [jnp.exp(s[h] - jnp.max(s[h], axis=1, keepdims=True)) for h in heads]
    l = [jnp.sum(p[h], axis=1, keepdims=True) for h in heads]
    pv = [jnp.dot(p[h].astype(BF16), mv_ref[0, :, cols[h]], preferred_element_type=F32) for h in heads]
    outs = [(pv[h] / l[h]).astype(BF16) for h in heads]
    y = jnp.dot(jnp.concatenate(outs, axis=1), wo_ref[...], preferred_element_type=F32)
    o_ref[...] = _layernorm(DN_ALPHA * x + y, g_ref[...], b_ref[...])


def _xattn_prompt(x, wq, mk, mv, wo, g, b, T, tm):
    M = x.shape[0]
    tm = min(tm, T)
    n_mem = mk.shape[1]
    per_b = T // tm
    full = lambda shape: pl.BlockSpec(shape, lambda i: (0,) * len(shape))
    mem = pl.BlockSpec((1, n_mem, D_MODEL), lambda i: (i // per_b, 0, 0))
    return pl.pallas_call(
        _xattn_kernel,
        out_shape=jax.ShapeDtypeStruct((M, D_MODEL), F32),
        grid=(M // tm,),
        in_specs=[pl.BlockSpec((tm, D_MODEL), lambda i: (i, 0)), full((D_MODEL, D_MODEL)), mem, mem,
                  full((D_MODEL, D_MODEL)), full((1, D_MODEL)), full((1, D_MODEL))],
        out_specs=pl.BlockSpec((tm, D_MODEL), lambda i: (i, 0)),
        compiler_params=_cparams(1),
        name="xattn_prompt",
    )(x, wq, mk, mv, wo, g.reshape(1, -1), b.reshape(1, -1))


def _ffn_kernel(x_ref, wu_ref, wg_ref, cw_ref, cb_ref, wd_ref, g_ref, b_ref, o_ref, tail_ref,
                xb_ref, acc_ref, stage_ref, carry_ref, *, tm, per_b):
    i, j = pl.program_id(0), pl.program_id(1)

    @pl.when(j == 0)
    def _():
        xb_ref[...] = x_ref[...].astype(BF16)

    @pl.when(i % per_b == 0)
    def _():
        carry_ref[j] = jnp.zeros(carry_ref.shape[1:], F32)

    xb = xb_ref[...]
    stage_ref[0:SUBLANES, :] = carry_ref[j]
    stage_ref[SUBLANES:, :] = jnp.dot(xb, wu_ref[...], preferred_element_type=F32)
    last = stage_ref[tm:tm + SUBLANES, :]
    carry_ref[j] = last
    tail_ref[0] = last
    conv = (cb_ref[...] + cw_ref[0:1, :] * stage_ref[SUBLANES - 2:SUBLANES - 2 + tm, :]
            + cw_ref[1:2, :] * stage_ref[SUBLANES - 1:SUBLANES - 1 + tm, :]
            + cw_ref[2:3, :] * stage_ref[SUBLANES:, :])
    gate = jnp.dot(xb, wg_ref[...], preferred_element_type=F32)
    y = _bdot(jax.nn.gelu(conv) * gate, wd_ref[...])

    @pl.when(j == 0)
    def _():
        acc_ref[...] = y

    @pl.when(j > 0)
    def _():
        acc_ref[...] = acc_ref[...] + y

    @pl.when(j == pl.num_programs(1) - 1)
    def _():
        o_ref[...] = _layernorm(DN_ALPHA * x_ref[...] + acc_ref[...], g_ref[...], b_ref[...])


def _ffn_prompt(x, w_up, conv_w, conv_b, w_down, g, b, T, tm, tf):
    M = x.shape[0]
    d_ff = w_down.shape[0]
    tm = min(tm, T)
    nf = d_ff // tf
    per_b = T // tm
    return pl.pallas_call(
        functools.partial(_ffn_kernel, tm=tm, per_b=per_b),
        out_shape=(jax.ShapeDtypeStruct((M, D_MODEL), F32),
                   jax.ShapeDtypeStruct((M // tm, SUBLANES, d_ff), F32)),
        grid=(M // tm, nf),
        in_specs=[pl.BlockSpec((tm, D_MODEL), lambda i, j: (i, 0)),
                  pl.BlockSpec((D_MODEL, tf), lambda i, j: (0, j)),
                  pl.BlockSpec((D_MODEL, tf), lambda i, j: (0, nf + j)),
                  pl.BlockSpec((3, tf), lambda i, j: (0, j)),
                  pl.BlockSpec((1, tf), lambda i, j: (0, j)),
                  pl.BlockSpec((tf, D_MODEL), lambda i, j: (j, 0)),
                  pl.BlockSpec((1, D_MODEL), lambda i, j: (0, 0)),
                  pl.BlockSpec((1, D_MODEL), lambda i, j: (0, 0))],
        out_specs=(pl.BlockSpec((tm, D_MODEL), lambda i, j: (i, 0)),
                   pl.BlockSpec((1, SUBLANES, tf), lambda i, j: (i, 0, j))),
        scratch_shapes=[pltpu.VMEM((tm, D_MODEL), BF16), pltpu.VMEM((tm, D_MODEL), F32),
                        pltpu.VMEM((tm + SUBLANES, tf), F32), pltpu.VMEM((nf, SUBLANES, tf), F32)],
        compiler_params=_cparams(2),
        name="ffn_prompt",
    )(x, w_up, w_up, conv_w, conv_b.reshape(1, -1), w_down, g.reshape(1, -1), b.reshape(1, -1))


def _dattn_sample_kernel(*refs, n_pages, page, lam_init):
    pt_ref = refs[0]
    lam_ref, g_ref, q_ref, kn_ref, vn_ref = refs[1:6]
    k_refs = refs[6:6 + n_pages]
    v_refs = refs[6 + n_pages:6 + 2 * n_pages]
    o_ref = refs[6 + 2 * n_pages]
    del pt_ref
    n_rows = 2 * N_HEADS
    q = q_ref[0]
    lane = lax.broadcasted_iota(jnp.int32, (n_rows, MIX_W), 1)
    row = lax.broadcasted_iota(jnp.int32, (n_rows, MIX_W), 0)
    qbd = jnp.where(lax.shift_right_logical(lane, 6) == row, jnp.broadcast_to(q, (n_rows, MIX_W)), 0.0)
    s_new = jnp.sum(qbd * kn_ref[0], axis=1, keepdims=True)
    scores = [_bdot(qbd, k_refs[p][0]) for p in range(n_pages)]
    m = s_new
    for s in scores:
        m = jnp.maximum(m, jnp.max(s, axis=1, keepdims=True))
    p_new = jnp.exp2(s_new - m)
    l = p_new
    vn = vn_ref[0]
    accs = [p_new * vn[:, h * HEAD_W:(h + 1) * HEAD_W] for h in range(N_HEADS)]
    for p in range(n_pages):
        pr = jnp.exp2(scores[p] - m)
        l = l + jnp.sum(pr, axis=1, keepdims=True)
        for h in range(N_HEADS):
            accs[h] = accs[h] + _bdot(pr, v_refs[p][0, pl.ds(h, page, stride=N_HEADS), :])
    lam = _lambda(lam_ref, lam_init)
    outs = []
    for h in range(N_HEADS):
        o = accs[h] / l
        d = o[2 * h:2 * h + 1, :] - lam * o[2 * h + 1:2 * h + 2, :]
        outs.append(_rmsnorm(d, g_ref[...]) * (1.0 - lam_init))
    o_ref[0] = jnp.concatenate(outs, axis=1)


def _dattn_sample(q, k_new, v_new, kt_pool, v_pool, pool_base, page_table, lam_p, subln_g, lam_init):
    DB = q.shape[0]
    n_pages = page_table.shape[1]
    page = kt_pool.shape[2]
    tok = pl.BlockSpec((1, 1, MIX_W), lambda b, pt: (b, 0, 0))

    def k_spec(p):
        return pl.BlockSpec((1, MIX_W, page), lambda b, pt: (pool_base + pt[b, p], 0, 0))

    def v_spec(p):
        return pl.BlockSpec((1, page * N_HEADS, HEAD_W), lambda b, pt: (pool_base + pt[b, p], 0, 0))

    grid_spec = pltpu.PrefetchScalarGridSpec(
        num_scalar_prefetch=1,
        grid=(DB,),
        in_specs=[pl.BlockSpec((4, DH_A), lambda b, pt: (0, 0)),
                  pl.BlockSpec((1, HEAD_W), lambda b, pt: (0, 0)), tok, tok, tok]
                 + [k_spec(p) for p in range(n_pages)] + [v_spec(p) for p in range(n_pages)],
        out_specs=tok,
    )
    out = pl.pallas_call(
        functools.partial(_dattn_sample_kernel, n_pages=n_pages, page=page, lam_init=lam_init),
        out_shape=jax.ShapeDtypeStruct((DB, 1, MIX_W), F32),
        grid_spec=grid_spec,
        compiler_params=_cparams(1),
        name="dattn_sample",
    )(page_table, lam_p, subln_g.reshape(1, -1), q.reshape(DB, 1, MIX_W), k_new.reshape(DB, 1, MIX_W),
      v_new.reshape(DB, 1, MIX_W), *([kt_pool] * n_pages), *([v_pool] * n_pages))
    return out.reshape(DB, MIX_W)


def _hgrn_step_kernel(h_ref, s0_ref, lb_ref, ng_ref, o_ref, s_ref, *, bt):
    def body(i, carry):
        row = h_ref[i]
        heads = range(N_HEADS)
        blk = lambda n: row[:, n * MIX_W:(n + 1) * MIX_W]
        lb = lb_ref[...]
        f = lb + (1.0 - lb) * _sigmoid(blk(4))
        q, v = _silu(blk(3)), blk(5)
        f_col = [_col(_head(f, h)) for h in heads]
        s_new = [f_col[h] * s0_ref[i, h] + (1.0 - f_col[h]) * _head(v, h) for h in heads]
        for h in heads:
            s_ref[i, h] = s_new[h]
        o = [_bdot(jnp.broadcast_to(_head(q, h), (SUBLANES, HEAD_W)), s_new[h])[0:1, :] for h in heads]
        o = jnp.concatenate([_rmsnorm(o[h], ng_ref[...]) for h in heads], axis=1)
        o_ref[i] = o * _silu(blk(6))
        return carry

    lax.fori_loop(0, bt, body, 0, unroll=2)


def _hgrn_step(h, s0, lb, norm_g, bt):
    DB, W = h.shape
    bt = min(bt, DB)
    state = pl.BlockSpec((bt, N_HEADS, HEAD_W, HEAD_W), lambda i: (i, 0, 0, 0))
    o, s = pl.pallas_call(
        functools.partial(_hgrn_step_kernel, bt=bt),
        out_shape=(jax.ShapeDtypeStruct((DB, 1, MIX_W), F32), jax.ShapeDtypeStruct(s0.shape, F32)),
        grid=(DB // bt,),
        in_specs=[pl.BlockSpec((bt, 1, W), lambda i: (i, 0, 0)), state,
                  pl.BlockSpec((1, MIX_W), lambda i: (0, 0)), pl.BlockSpec((1, HEAD_W), lambda i: (0, 0))],
        out_specs=(pl.BlockSpec((bt, 1, MIX_W), lambda i: (i, 0, 0)), state),
        compiler_params=_cparams(1),
        name="hgrn_step",
    )(h.reshape(DB, 1, W), s0, lb.reshape(1, -1), norm_g.reshape(1, -1))
    return o.reshape(DB, MIX_W), s


def _od_step_kernel(h_ref, gt_ref, bif_ref, cos_ref, sin_ref, sc0_ref, dc0_ref, dn0_ref, dm0_ref, cng_ref, dng_ref,
                    oc_ref, od_ref, sc_ref, dc_ref, dn_ref, dm_ref, *, bt):
    scale = HEAD_W ** -0.5

    def body(i, carry):
        row = h_ref[i]
        cos, sin = cos_ref[...], sin_ref[...]
        gates = gt_ref[i] + bif_ref[...]
        m_row = dm0_ref[i]
        heads = range(N_HEADS)
        blk = lambda n: row[:, n * MIX_W:(n + 1) * MIX_W]
        q = _rope(blk(0), cos, sin, HEAD_W // 2)
        k = _rope(blk(1), cos, sin, HEAD_W // 2) * scale
        cv, dk, dv, dq = blk(2), blk(5), blk(6), blk(4) * scale
        lf = _log_sigmoid(gates)
        m_new = [jnp.maximum(lf[:, N_HEADS + h:N_HEADS + h + 1] + m_row[:, h:h + 1], gates[:, h:h + 1])
                 for h in heads]
        c_scale = [jnp.exp(lf[:, N_HEADS + h:N_HEADS + h + 1] + m_row[:, h:h + 1] - m_new[h]) for h in heads]
        kw = [_head(dk, h) * jnp.exp(gates[:, h:h + 1] - m_new[h]) for h in heads]
        k_col = [_col(_head(k, h)) for h in heads]
        kw_col = [_col(kw[h]) for h in heads]
        s_new = [math.exp(_log_gamma(h)) * sc0_ref[i, h] + k_col[h] * _head(cv, h) for h in heads]
        c_new = [c_scale[h] * dc0_ref[i, h] + kw_col[h] * _head(dv, h) for h in heads]
        n_new = [c_scale[h] * dn0_ref[i, h:h + 1, :] + kw[h] for h in heads]
        for h in heads:
            sc_ref[i, h] = s_new[h]
            dc_ref[i, h] = c_new[h]
            dn_ref[i, h:h + 1, :] = n_new[h]
        rows8 = lambda x: jnp.broadcast_to(x, (SUBLANES, HEAD_W))
        o = [_bdot(rows8(_head(q, h)), s_new[h])[0:1, :] for h in heads]
        num = [_bdot(rows8(_head(dq, h)), c_new[h])[0:1, :] for h in heads]
        den = [jnp.sum(_head(dq, h) * n_new[h], axis=1, keepdims=True) for h in heads]
        hh = [num[h] / jnp.maximum(jnp.abs(den[h]), jnp.exp(-m_new[h])) for h in heads]
        oc = jnp.concatenate([_groupnorm(o[h], cng_ref[...]) for h in heads], axis=1)
        od = jnp.concatenate([_groupnorm(hh[h], dng_ref[...]) for h in heads], axis=1)
        oc_ref[i] = oc * _silu(blk(3))
        od_ref[i] = od * _sigmoid(blk(7))
        dm_ref[i] = jnp.concatenate(m_new, axis=1)
        return carry

    lax.fori_loop(0, bt, body, 0)


def _od_step(h, gates, b_if, cos_t, sin_t, sc0, dc0, dn0, dm0, c_norm_g, d_norm_g, bt):
    DB, W = h.shape
    bt = min(bt, DB)
    mat = pl.BlockSpec((bt, N_HEADS, HEAD_W, HEAD_W), lambda i: (i, 0, 0, 0))
    nblk = pl.BlockSpec((bt, N_HEADS, HEAD_W), lambda i: (i, 0, 0))
    mblk = pl.BlockSpec((bt, 1, N_HEADS), lambda i: (i, 0, 0))
    vec = pl.BlockSpec((1, LANES), lambda i: (0, 0))
    out = pl.BlockSpec((bt, 1, MIX_W), lambda i: (i, 0, 0))
    b_pad = jnp.zeros((1, LANES), F32).at[0, :2 * N_HEADS].set(b_if)
    oc, od, sc, dc, dn, dm = pl.pallas_call(
        functools.partial(_od_step_kernel, bt=bt),
        out_shape=(jax.ShapeDtypeStruct((DB, 1, MIX_W), F32), jax.ShapeDtypeStruct((DB, 1, MIX_W), F32),
                   jax.ShapeDtypeStruct(sc0.shape, F32), jax.ShapeDtypeStruct(dc0.shape, F32),
                   jax.ShapeDtypeStruct(dn0.shape, F32), jax.ShapeDtypeStruct((DB, 1, N_HEADS), F32)),
        grid=(DB // bt,),
        in_specs=[pl.BlockSpec((bt, 1, W), lambda i: (i, 0, 0)),
                  pl.BlockSpec((bt, 1, LANES), lambda i: (i, 0, 0)), vec, vec, vec,
                  mat, mat, nblk, mblk, vec, vec],
        out_specs=(out, out, mat, mat, nblk, mblk),
        compiler_params=_cparams(1),
        name="od_step",
    )(h.reshape(DB, 1, W), gates.reshape(DB, 1, LANES), b_pad, cos_t, sin_t, sc0, dc0, dn0,
      dm0.reshape(DB, 1, N_HEADS), c_norm_g.reshape(1, -1), d_norm_g.reshape(1, -1))
    return oc.reshape(DB, MIX_W), od.reshape(DB, MIX_W), sc, dc, dn, dm.reshape(DB, N_HEADS)


X_HALVES = DH_X // LANES
X_ROWS = N_HEADS * X_HALVES


def _xattn_sample_kernel(q_ref, mk_ref, mv_ref, o_ref, *, xb):
    for t in range(xb):
        _xattn_sample_row(q_ref.at[t], mk_ref.at[t], mv_ref.at[t], o_ref.at[t])


def _xattn_sample_row(q_ref, mk_ref, mv_ref, o_ref):
    q = q_ref[...] * (DH_X ** -0.5)
    n_cols = mk_ref.shape[0]
    q2 = jnp.concatenate([q[:, h * DH_X + c * LANES:h * DH_X + (c + 1) * LANES]
                          for c in range(X_HALVES) for h in range(N_HEADS)], axis=0)
    s2 = _bdot_nt(q2, mk_ref[...])
    row = lax.broadcasted_iota(jnp.int32, (X_ROWS, n_cols), 0)
    col = lax.broadcasted_iota(jnp.int32, (X_ROWS, n_cols), 1)
    own = (col & (X_ROWS - 1)) == row
    s2 = jnp.where(own, s2, 0.0)
    s4 = s2[:N_HEADS] + pltpu.roll(s2, n_cols - N_HEADS, 1)[N_HEADS:]
    own4 = own[:N_HEADS]
    s4 = jnp.where(own4, s4, NEG_INF)
    p4 = jnp.exp(s4 - jnp.max(s4, axis=1, keepdims=True))
    l4 = jnp.sum(p4, axis=1, keepdims=True)
    p4 = p4 / l4
    p2 = jnp.concatenate([p4, pltpu.roll(p4, N_HEADS, 1)], axis=0)
    o2 = _bdot(p2, mv_ref[...])
    o_ref[...] = jnp.concatenate([o2[c * N_HEADS + h:c * N_HEADS + h + 1, :]
                                  for h in range(N_HEADS) for c in range(X_HALVES)], axis=1)


def _xattn_sample(q, mk, mv, base):
    DB = q.shape[0]
    n_rows = mk.shape[1]
    xb = math.gcd(DB, 4)
    tok = pl.BlockSpec((xb, 1, D_MODEL), lambda b: (b, 0, 0))
    mem = pl.BlockSpec((xb, n_rows, LANES), lambda b: (base // xb + b, 0, 0))
    assert base % xb == 0
    out = pl.pallas_call(
        functools.partial(_xattn_sample_kernel, xb=xb),
        out_shape=jax.ShapeDtypeStruct((DB, 1, D_MODEL), F32),
        grid=(DB // xb,),
        in_specs=[tok, mem, mem],
        out_specs=tok,
        compiler_params=_cparams(1),
        name="xattn_sample",
    )(q.reshape(DB, 1, D_MODEL), mk, mv)
    return out.reshape(DB, D_MODEL)


def _mem_rows(cache):
    n_l, DB, n_mem = cache.shape[:3]
    c = cache.reshape(n_l, DB, n_mem, N_HEADS, X_HALVES, LANES).transpose(0, 1, 2, 4, 3, 5)
    return c.reshape(n_l * DB, n_mem * X_ROWS, LANES)


def _ffn_sample_kernel(x_ref, up_ref, gate_ref, b0_ref, b1_ref, cw_ref, cb_ref, wd_ref, g_ref, b_ref, o_ref, acc_ref):
    j = pl.program_id(0)
    conv = (cb_ref[...] + cw_ref[0:1, :] * b0_ref[...] + cw_ref[1:2, :] * b1_ref[...]
            + cw_ref[2:3, :] * up_ref[...])
    y = _bdot(jax.nn.gelu(conv) * gate_ref[...], wd_ref[...])

    @pl.when(j == 0)
    def _():
        acc_ref[...] = y

    @pl.when(j > 0)
    def _():
        acc_ref[...] = acc_ref[...] + y

    @pl.when(j == pl.num_programs(0) - 1)
    def _():
        o_ref[...] = _layernorm(DN_ALPHA * x_ref[...] + acc_ref[...], g_ref[...], b_ref[...])


def _ffn_sample(x, ug, buf0, buf1, conv_w, conv_b, w_down, g, b, tf):
    DB = x.shape[0]
    d_ff = w_down.shape[0]
    nf = d_ff // tf
    ff = lambda off: pl.BlockSpec((DB, tf), lambda j: (0, off + j))
    full = lambda shape: pl.BlockSpec(shape, lambda j: (0,) * len(shape))
    return pl.pallas_call(
        _ffn_sample_kernel,
        out_shape=jax.ShapeDtypeStruct((DB, D_MODEL), F32),
        grid=(nf,),
        in_specs=[full((DB, D_MODEL)), ff(0), ff(nf), ff(0), ff(0),
                  pl.BlockSpec((3, tf), lambda j: (0, j)), pl.BlockSpec((1, tf), lambda j: (0, j)),
                  pl.BlockSpec((tf, D_MODEL), lambda j: (j, 0)), full((1, D_MODEL)), full((1, D_MODEL))],
        out_specs=full((DB, D_MODEL)),
        scratch_shapes=[pltpu.VMEM((DB, D_MODEL), F32)],
        compiler_params=_cparams(1),
        name="ffn_sample",
    )(x, ug, ug, buf0, buf1, conv_w, conv_b.reshape(1, -1), w_down, g.reshape(1, -1), b.reshape(1, -1))


def kernel(x_prompt, x_sample, cache_a_k, cache_a_v, state_b, state_c, state_d_c, state_d_n, state_d_m,
           cache_mem_k, cache_mem_v, state_conv, page_table, mem_prompt,
           ev_w_in, ev_w_out, ev_lam, ev_subln_g, ev_lb_logits, ev_b_norm_g,
           od_w_in, od_b_if, od_w_out, od_c_norm_g, od_d_norm_g,
           ln_g, ln_b, xa_wq, xa_wkv, xa_wo, ffn_w_up, ffn_conv_w, ffn_conv_b, ffn_w_down):
    B, T, _ = x_prompt.shape
    DB, t_s, _ = x_sample.shape
    assert t_s == 1, "the sample group is a single decoding step"
    assert T % CHUNK == 0
    n_pool, page = cache_a_k.shape[1], cache_a_k.shape[2]
    n_pages = page_table.shape[1]
    past = n_pages * page
    n_mem = mem_prompt.shape[1]
    d_ff = ffn_w_down.shape[1]
    tf = d_ff // 2
    N = B * T
    TM = 512

    pos_p = jnp.arange(T)
    pos_s = jnp.full((DB,), past, jnp.int32)
    lb_table = jnp.cumsum(jax.nn.softmax(ev_lb_logits.astype(F32), axis=0), axis=0)

    xp = x_prompt.reshape(N, D_MODEL)
    xs = x_sample.reshape(DB, D_MODEL)
    outs = {k: [] for k in ("ak_p", "av_p", "ak_s", "av_s", "sb_p", "sb_s", "sc_p", "sc_s", "dc_p", "dc_s",
                            "dn_p", "dn_s", "dm_p", "dm_s", "mk_p", "mv_p", "cv_p", "cv_s")}
    swap = lambda s: jnp.swapaxes(s, -1, -2)
    mem_k_rows, mem_v_rows = _mem_rows(cache_mem_k), _mem_rows(cache_mem_v)

    for l in range(DEPTH):
        j = l // 2
        if l % 2 == 0:
            lam_init = 0.8 - 0.6 * math.exp(-0.3 * l)
            w_in = ev_w_in[j].astype(BF16)
            w_out = ev_w_out[j].astype(BF16)
            cos_t, sin_t = _rope_tables(pos_p, DH_A)
            qt, kb, kt32, vt, v_rows, h_b = _ev_in_prompt(xp.reshape(B, T, D_MODEL), w_in, cos_t, sin_t, TM)
            o_a = _dattn_prompt(qt, kb, vt, ev_lam[j], ev_subln_g[j], lam_init, 512)
            o_b, st = _hgrn_prompt(h_b, lb_table[j], 512)
            outs["ak_p"].append(kt32.reshape(B, N_HEADS, 2, DH_A, T).transpose(0, 4, 1, 2, 3))
            outs["av_p"].append(v_rows.reshape(B, T, N_HEADS, HEAD_W))
            outs["sb_p"].append(swap(st))
            xp = _proj_ln([o_a.reshape(N, MIX_W), o_b.reshape(N, MIX_W)], [w_out[:MIX_W], w_out[MIX_W:]],
                          xp, ln_g[l, 0], ln_b[l, 0], TM, "ev_out_prompt",
                          gates=[None, ("rms", ev_b_norm_g[j], "silu", h_b.reshape(N, -1), 3)])
            hs = _matmul(xs, w_in, DB, 512, "ev_in_sample")
            cos_s, sin_s = _rope_tables(pos_s, DH_A)
            qs, ks32, kts32 = _ev_prep_sample(hs, cos_s, sin_s)
            vs32 = hs[:, 2 * MIX_W:3 * MIX_W]
            kt_pool = cache_a_k.transpose(0, 1, 3, 4, 5, 2).reshape(-1, MIX_W, page)
            v_pool = cache_a_v.reshape(-1, page * N_HEADS, HEAD_W)
            oa_s = _dattn_sample(qs, ks32, vs32, kt_pool, v_pool, j * n_pool, page_table,
                                 ev_lam[j], ev_subln_g[j], lam_init)
            ob_s, sb_s = _hgrn_step(hs, state_b[j], lb_table[j], ev_b_norm_g[j], 8)
            outs["ak_s"].append(kts32.reshape(N_HEADS, 2, DH_A, DB).transpose(3, 0, 1, 2)[:, None])
            outs["av_s"].append(vs32.reshape(DB, 1, N_HEADS, HEAD_W))
            outs["sb_s"].append(sb_s)
            xs = _proj_ln([oa_s, ob_s], [w_out[:MIX_W], w_out[MIX_W:]], xs, ln_g[l, 0], ln_b[l, 0], DB,
                          "ev_out_sample")
        else:
            n_main = 8 * MIX_W
            w_in = od_w_in[j][:, :n_main].astype(BF16)
            w_gate = jnp.pad(od_w_in[j][:, n_main:], ((0, 0), (0, LANES - 2 * N_HEADS))).astype(BF16)
            w_out = od_w_out[j].astype(BF16)
            h, gates = _od_in_prompt(xp, w_in, w_gate, TM, w_in.shape[1])
            cos_t, sin_t = _rope_tables(pos_p, HEAD_W)
            o_c, o_d, sct, dct, dn, dm = _od_prompt(h.reshape(B, T, -1), gates.reshape(B, T, -1), od_b_if[j],
                                                    cos_t, sin_t, 512)
            outs["sc_p"].append(swap(sct))
            outs["dc_p"].append(swap(dct))
            outs["dn_p"].append(dn[:, :N_HEADS, :])
            outs["dm_p"].append(dm[:, :N_HEADS, 0])
            xp = _proj_ln([o_c.reshape(N, MIX_W), o_d.reshape(N, MIX_W)], [w_out[:MIX_W], w_out[MIX_W:]],
                          xp, ln_g[l, 0], ln_b[l, 0], TM, "od_out_prompt",
                          gates=[("group", od_c_norm_g[j], "silu", h, 3), ("group", od_d_norm_g[j], "sigmoid", h, 7)])
            hs = _matmul(xs, w_in, DB, 512, "od_in_sample")
            gates_s = _matmul(xs, w_gate, DB, LANES, "od_gates_sample")
            cos_s, sin_s = _rope_tables(pos_s[:1], HEAD_W)
            oc_s, od_s, sc_s, dc_s, dn_s, dm_s = _od_step(hs, gates_s, od_b_if[j], cos_s, sin_s,
                                                          state_c[j], state_d_c[j], state_d_n[j], state_d_m[j],
                                                          od_c_norm_g[j], od_d_norm_g[j], 8)
            outs["sc_s"].append(sc_s)
            outs["dc_s"].append(dc_s)
            outs["dn_s"].append(dn_s)
            outs["dm_s"].append(dm_s)
            xs = _proj_ln([oc_s, od_s], [w_out[:MIX_W], w_out[MIX_W:]], xs, ln_g[l, 0], ln_b[l, 0], DB,
                          "od_out_sample")

        wq = xa_wq[l].astype(BF16)
        wo = xa_wo[l].astype(BF16)
        mkv = _matmul(mem_prompt.reshape(B * n_mem, D_MODEL), xa_wkv[l].astype(BF16), 512, 512, "mem_kv")
        mk, mv = mkv[:, :D_MODEL], mkv[:, D_MODEL:]
        outs["mk_p"].append(mk.reshape(B, n_mem, N_HEADS, DH_X))
        outs["mv_p"].append(mv.reshape(B, n_mem, N_HEADS, DH_X))
        xp = _xattn_prompt(xp, wq, mk.astype(BF16).reshape(B, n_mem, D_MODEL),
                           mv.astype(BF16).reshape(B, n_mem, D_MODEL), wo, ln_g[l, 1], ln_b[l, 1], T, TM)
        q_s = _matmul(xs, wq, DB, 512, "xattn_q_sample")
        xo_s = _xattn_sample(q_s, mem_k_rows, mem_v_rows, l * DB)
        xs = _proj_ln([xo_s], [wo], xs, ln_g[l, 1], ln_b[l, 1], DB, "xattn_out_sample")

        w_up = ffn_w_up[l].astype(BF16)
        w_down = ffn_w_down[l].astype(BF16)
        tm_f = min(TM, T)
        xp, tails = _ffn_prompt(xp, w_up, ffn_conv_w[l], ffn_conv_b[l], w_down, ln_g[l, 2], ln_b[l, 2], T, TM, tf)
        tails = tails.reshape(B, T // tm_f, SUBLANES, d_ff)
        outs["cv_p"].append(tails[:, -1, SUBLANES - 2:, :])
        ug_s = _matmul(xs, w_up, DB, tf, "ffn_up_sample")
        buf = state_conv[l]
        xs = _ffn_sample(xs, ug_s, buf[:, 0, :], buf[:, 1, :], ffn_conv_w[l], ffn_conv_b[l], w_down,
                         ln_g[l, 2], ln_b[l, 2], tf)
        outs["cv_s"].append(jnp.stack([buf[:, 1, :], ug_s[:, :d_ff]], axis=1))

    st = lambda k: jnp.stack(outs[k])
    return (xp.reshape(B, T, D_MODEL), xs.reshape(DB, 1, D_MODEL),
            st("ak_p"), st("av_p"), st("ak_s"), st("av_s"), st("sb_p"), st("sb_s"),
            st("sc_p"), st("sc_s"), st("dc_p"), st("dc_s"), st("dn_p"), st("dn_s"), st("dm_p"), st("dm_s"),
            st("mk_p"), st("mv_p"), st("cv_p"), st("cv_s"))
```

```python
import functools
import math

import numpy as np
import jax
import jax.numpy as jnp
from jax import lax
from jax.experimental import pallas as pl
from jax.experimental.pallas import tpu as pltpu

F32 = jnp.float32
BF16 = jnp.bfloat16

D_MODEL = 1024
MIX_W = D_MODEL // 2
N_HEADS = 4
HEAD_W = MIX_W // N_HEADS
DH_A = HEAD_W // 2
DH_X = D_MODEL // N_HEADS
CHUNK = 64
OD_CHUNK = 128
ROPE_THETA = 10000.0
NORM_EPS = 1e-5
DEPTH = 2
DN_ALPHA = (2.0 * DEPTH) ** 0.25
LANES = 128
SUBLANES = 8
VMEM_LIMIT = 56 * 1024 * 1024
NEG_INF = float("-inf")

NT_DIMS = (((1,), (1,)), ((), ()))
TN_DIMS = (((0,), (0,)), ((), ()))


def _cparams(n_axes, vmem=VMEM_LIMIT):
    return pltpu.CompilerParams(dimension_semantics=("arbitrary",) * n_axes, vmem_limit_bytes=vmem)


def _bdot(a, b):
    return jnp.dot(a.astype(BF16), b.astype(BF16), preferred_element_type=F32)


def _bdot_nt(a, b):
    return lax.dot_general(a.astype(BF16), b.astype(BF16), NT_DIMS, preferred_element_type=F32)


def _bdot_tn(a, b):
    return lax.dot_general(a.astype(BF16), b.astype(BF16), TN_DIMS, preferred_element_type=F32)


def _split3(x):
    p1 = x.astype(BF16)
    r1 = x - p1.astype(F32)
    p2 = r1.astype(BF16)
    p3 = (r1 - p2.astype(F32)).astype(BF16)
    return p1, p2, p3


def _cumsum_rows(tri, x):
    p1, p2, p3 = _split3(x)
    d = functools.partial(jnp.dot, preferred_element_type=F32)
    return d(tri, p1) + d(tri, p2) + d(tri, p3)


def _cumsum_lanes(x, triu):
    p1, p2, p3 = _split3(x)
    d = functools.partial(jnp.dot, preferred_element_type=F32)
    return d(p1, triu) + d(p2, triu) + d(p3, triu)


def _tri(L, lower):
    r = lax.broadcasted_iota(jnp.int32, (L, L), 0)
    c = lax.broadcasted_iota(jnp.int32, (L, L), 1)
    return (r >= c) if lower else (r <= c)


def _sigmoid(x):
    return 1.0 / (1.0 + jnp.exp(-x))


def _silu(x):
    return x * _sigmoid(x)


def _log_sigmoid(x):
    return jnp.minimum(x, 0.0) - jnp.log(1.0 + jnp.exp(-jnp.abs(x)))


def _layernorm(z, g, b):
    mu = jnp.mean(z, -1, keepdims=True)
    zc = z - mu
    var = jnp.mean(zc * zc, -1, keepdims=True)
    return zc * lax.rsqrt(var + NORM_EPS) * g + b


def _rmsnorm(x, g):
    return x * lax.rsqrt(jnp.mean(x * x, -1, keepdims=True) + NORM_EPS) * g


def _groupnorm(x, g):
    mu = jnp.mean(x, -1, keepdims=True)
    xc = x - mu
    var = jnp.mean(xc * xc, -1, keepdims=True)
    return xc * lax.rsqrt(var + NORM_EPS) * g


def _rope(x, cos, sin, half):
    outs = []
    for c in range(x.shape[1] // LANES):
        xc = x[:, c * LANES:(c + 1) * LANES]
        if 2 * half == LANES:
            sw = pltpu.roll(xc, half, 1)
        else:
            lane = lax.broadcasted_iota(jnp.int32, xc.shape, 1)
            first = (lane & (2 * half - 1)) < half
            sw = jnp.where(first, pltpu.roll(xc, LANES - half, 1), pltpu.roll(xc, half, 1))
        outs.append(xc * cos + sw * sin)
    return outs[0] if len(outs) == 1 else jnp.concatenate(outs, axis=1)


def _rope_tables(pos, d):
    inv = ROPE_THETA ** (-jnp.arange(0, d // 2, dtype=F32) * 2.0 / d)
    reps = 2 * LANES // d
    inv_t = jnp.tile(inv, reps)
    sign = jnp.tile(jnp.concatenate([-jnp.ones(d // 2, F32), jnp.ones(d // 2, F32)]), reps // 2)
    ang = pos.astype(F32)[:, None] * inv_t[None, :]
    return jnp.cos(ang), jnp.sin(ang) * sign[None, :]


def _col(row):
    return jnp.broadcast_to(row, (LANES, LANES)).T


def _mm_kernel(x_ref, w_ref, o_ref, xb_ref):
    @pl.when(pl.program_id(1) == 0)
    def _():
        xb_ref[...] = x_ref[...].astype(BF16)

    o_ref[...] = jnp.dot(xb_ref[...], w_ref[...], preferred_element_type=F32).astype(o_ref.dtype)


def _matmul(x, w, tm, tn, name, out_dtype=F32):
    M, K = x.shape
    N = w.shape[1]
    tm, tn = min(tm, M), min(tn, N)
    return pl.pallas_call(
        _mm_kernel,
        out_shape=jax.ShapeDtypeStruct((M, N), out_dtype),
        grid=(M // tm, N // tn),
        in_specs=[pl.BlockSpec((tm, K), lambda i, j: (i, 0)),
                  pl.BlockSpec((K, tn), lambda i, j: (0, j))],
        out_specs=pl.BlockSpec((tm, tn), lambda i, j: (i, j)),
        scratch_shapes=[pltpu.VMEM((tm, K), BF16)],
        compiler_params=_cparams(2),
        name=name,
    )(x, w)


_HEAD_NORMS = {"rms": _rmsnorm, "group": _groupnorm}
_GATE_ACTS = {"silu": _silu, "sigmoid": _sigmoid}


def _proj_ln_kernel(*refs, n_in, gated):
    a_refs, w_refs = refs[:n_in], refs[n_in:2 * n_in]
    x_ref, g_ref, b_ref = refs[2 * n_in:2 * n_in + 3]
    extra, o_ref = refs[2 * n_in + 3:-1], refs[-1]
    y = None
    for a_ref, w_ref, gate in zip(a_refs, w_refs, gated):
        a = a_ref[...]
        if gate is not None:
            norm, act = _HEAD_NORMS[gate[0]], _GATE_ACTS[gate[1]]
            gain_ref, gate_ref, extra = extra[0], extra[1], extra[2:]
            a = jnp.concatenate([norm(_head(a, h), gain_ref[...]) for h in range(N_HEADS)], axis=1)
            a = a * act(gate_ref[...])
        ya = _bdot(a, w_ref[...])
        y = ya if y is None else y + ya
    o_ref[...] = _layernorm(DN_ALPHA * x_ref[...] + y, g_ref[...], b_ref[...])


def _proj_ln(acts, weights, x, g, b, tm, name, gates=None):
    M = x.shape[0]
    tm = min(tm, M)
    n_in = len(acts)
    gates = gates or [None] * n_in
    in_specs = [pl.BlockSpec((tm, a.shape[1]), lambda i: (i, 0)) for a in acts]
    in_specs += [pl.BlockSpec(w.shape, lambda i: (0, 0)) for w in weights]
    in_specs += [pl.BlockSpec((tm, D_MODEL), lambda i: (i, 0)),
                 pl.BlockSpec((1, D_MODEL), lambda i: (0, 0)),
                 pl.BlockSpec((1, D_MODEL), lambda i: (0, 0))]
    extra = []
    for gate in gates:
        if gate is not None:
            _, gain, _, gate_arr, blk = gate
            in_specs += [pl.BlockSpec((1, HEAD_W), lambda i: (0, 0)),
                         pl.BlockSpec((tm, MIX_W), lambda i, blk=blk: (i, blk))]
            extra += [gain.reshape(1, -1), gate_arr]
    return pl.pallas_call(
        functools.partial(_proj_ln_kernel, n_in=n_in,
                          gated=tuple(None if gt is None else (gt[0], gt[2]) for gt in gates)),
        out_shape=jax.ShapeDtypeStruct((M, D_MODEL), F32),
        grid=(M // tm,),
        in_specs=in_specs,
        out_specs=pl.BlockSpec((tm, D_MODEL), lambda i: (i, 0)),
        compiler_params=_cparams(1),
        name=name,
    )(*acts, *weights, x, g.reshape(1, -1), b.reshape(1, -1), *extra)


Q_SCALE = DH_A ** -0.5 * math.log2(math.e)
VT_ROWS = HEAD_W + 16


def _ev_in_prompt_kernel(x_ref, w_ref, cos_ref, sin_ref, qt_ref, kb_ref, kt_ref, vt_ref, vr_ref, hb_ref):
    xb = x_ref[0].astype(BF16)
    cos, sin = cos_ref[...], sin_ref[...]
    proj = lambda lo, hi: jnp.dot(xb, w_ref[:, lo * MIX_W:hi * MIX_W], preferred_element_type=F32)
    hb_ref[0] = proj(3, 7)
    q = _rope(proj(0, 1), cos, sin, DH_A // 2)
    k = _rope(proj(1, 2), cos, sin, DH_A // 2)
    qt_ref[0] = (q * Q_SCALE).T.astype(BF16)
    kb_ref[0] = k.astype(BF16)
    kt_ref[0] = k.T
    v = proj(2, 3)
    tm = v.shape[0]
    vt = v.T.astype(BF16)
    ones = jnp.ones((VT_ROWS - HEAD_W, tm), BF16)
    for h in range(N_HEADS):
        vt_ref[0, h, :HEAD_W, :] = vt[h * HEAD_W:(h + 1) * HEAD_W]
        vt_ref[0, h, HEAD_W:, :] = ones
        vr_ref[0, pl.ds(h, tm, stride=N_HEADS), :] = _head(v, h)


def _ev_in_prompt(x3, w_in, cos_t, sin_t, tm):
    B, T, _ = x3.shape
    tm = min(tm, T)
    tab = pl.BlockSpec((tm, LANES), lambda b, i: (i, 0))
    tr = pl.BlockSpec((1, MIX_W, tm), lambda b, i: (b, 0, i))
    tshape = lambda dt: jax.ShapeDtypeStruct((B, MIX_W, T), dt)
    return pl.pallas_call(
        _ev_in_prompt_kernel,
        out_shape=(tshape(BF16), jax.ShapeDtypeStruct((B, T, MIX_W), BF16), tshape(F32),
                   jax.ShapeDtypeStruct((B, N_HEADS, VT_ROWS, T), BF16),
                   jax.ShapeDtypeStruct((B, T * N_HEADS, HEAD_W), F32),
                   jax.ShapeDtypeStruct((B, T, 4 * MIX_W), F32)),
        grid=(B, T // tm),
        in_specs=[pl.BlockSpec((1, tm, D_MODEL), lambda b, i: (b, i, 0)),
                  pl.BlockSpec(w_in.shape, lambda b, i: (0, 0)), tab, tab],
        out_specs=(tr, pl.BlockSpec((1, tm, MIX_W), lambda b, i: (b, i, 0)), tr,
                   pl.BlockSpec((1, N_HEADS, VT_ROWS, tm), lambda b, i: (b, 0, 0, i)),
                   pl.BlockSpec((1, tm * N_HEADS, HEAD_W), lambda b, i: (b, i, 0)),
                   pl.BlockSpec((1, tm, 4 * MIX_W), lambda b, i: (b, i, 0))),
        compiler_params=_cparams(2),
        name="ev_in_prompt",
    )(x3, w_in, cos_t, sin_t)


def _ev_prep_sample_kernel(qk_ref, cos_ref, sin_ref, q_ref, k_ref, kt_ref):
    cos, sin = cos_ref[...], sin_ref[...]
    k = _rope(qk_ref[:, MIX_W:], cos, sin, DH_A // 2)
    q_ref[...] = _rope(qk_ref[:, :MIX_W], cos, sin, DH_A // 2) * Q_SCALE
    k_ref[...] = k
    kt_ref[...] = k.T


def _ev_prep_sample(h, cos_t, sin_t):
    DB = h.shape[0]
    full = lambda shape: pl.BlockSpec(shape, lambda i: (0,) * len(shape))
    return pl.pallas_call(
        _ev_prep_sample_kernel,
        out_shape=(jax.ShapeDtypeStruct((DB, MIX_W), F32), jax.ShapeDtypeStruct((DB, MIX_W), F32),
                   jax.ShapeDtypeStruct((MIX_W, DB), F32)),
        grid=(1,),
        in_specs=[full((DB, 2 * MIX_W)), full((DB, LANES)), full((DB, LANES))],
        out_specs=(full((DB, MIX_W)), full((DB, MIX_W)), full((MIX_W, DB))),
        compiler_params=_cparams(1),
        name="ev_prep_sample",
    )(h, cos_t, sin_t)


def _lambda(lam_ref, lam_init):
    lp = lam_ref[...]
    s01 = jnp.sum(lp[0:1] * lp[1:2], axis=1, keepdims=True)
    s23 = jnp.sum(lp[2:3] * lp[3:4], axis=1, keepdims=True)
    return jnp.exp(s01) - jnp.exp(s23) + lam_init


def _dattn_kernel(lam_ref, g_ref, qt_ref, k_ref, vt_ref, o_ref, m_ref, acc_ref, *, tq, lam_init):
    i = pl.program_id(2)
    qt = qt_ref[0]
    sub = lax.broadcasted_iota(jnp.int32, qt.shape, 0)
    zero = jnp.zeros_like(qt)
    q_maps = [jnp.where(sub < DH_A, qt, zero), jnp.where(sub >= DH_A, qt, zero)]
    maps = range(2)
    m_ref[...] = jnp.full(m_ref.shape, NEG_INF, F32)
    acc_ref[...] = jnp.zeros(acc_ref.shape, F32)

    def steps(js, masked):
        ks, vts, ss = [], [], []
        for j in js:
            start = pl.multiple_of(j * tq, tq)
            ks.append(k_ref[0, pl.ds(start, tq), :])
            vts.append(vt_ref[0, 0, :, pl.ds(start, tq)])
        for kj in ks:
            ss.append([jnp.dot(kj, q_maps[c], preferred_element_type=F32) for c in maps])
        for n in range(len(js)):
            for c in maps:
                s = ss[n][c]
                if masked:
                    r = lax.broadcasted_iota(jnp.int32, s.shape, 0)
                    col = lax.broadcasted_iota(jnp.int32, s.shape, 1)
                    s = jnp.where(r <= col, s, NEG_INF)
                m_prev = m_ref[c]
                m_new = jnp.maximum(m_prev, jnp.max(s, axis=0, keepdims=True))
                alpha = jnp.exp2(m_prev - m_new)
                p = jnp.exp2(s - m_new)
                acc_ref[c] = alpha * acc_ref[c] + jnp.dot(vts[n], p.astype(BF16), preferred_element_type=F32)
                m_ref[c] = m_new

    def body(jj, carry):
        steps([2 * jj, 2 * jj + 1], False)
        return carry

    lax.fori_loop(0, lax.shift_right_logical(i, 1), body, 0)

    @pl.when((i & 1) == 1)
    def _():
        steps([i - 1], False)

    steps([i], True)
    lam = _lambda(lam_ref, lam_init)
    o = [acc_ref[c, :HEAD_W, :] / acc_ref[c, HEAD_W:HEAD_W + 1, :] for c in maps]
    d = (o[0] - lam * o[1]).T
    o_ref[0] = _rmsnorm(d, g_ref[...]) * (1.0 - lam_init)


def _dattn_prompt(qt, kb, vt, lam_p, subln_g, lam_init, tq):
    B, T, _ = kb.shape
    tq = min(tq, T)
    return pl.pallas_call(
        functools.partial(_dattn_kernel, tq=tq, lam_init=lam_init),
        out_shape=jax.ShapeDtypeStruct((B, T, MIX_W), F32),
        grid=(B, N_HEADS, T // tq),
        in_specs=[pl.BlockSpec((4, DH_A), lambda b, h, i: (0, 0)),
                  pl.BlockSpec((1, HEAD_W), lambda b, h, i: (0, 0)),
                  pl.BlockSpec((1, HEAD_W, tq), lambda b, h, i: (b, h, i)),
                  pl.BlockSpec((1, T, HEAD_W), lambda b, h, i: (b, 0, h)),
                  pl.BlockSpec((1, 1, VT_ROWS, T), lambda b, h, i: (b, h, 0, 0))],
        out_specs=pl.BlockSpec((1, tq, HEAD_W), lambda b, h, i: (b, i, h)),
        scratch_shapes=[pltpu.VMEM((2, 1, tq), F32), pltpu.VMEM((2, VT_ROWS, tq), F32)],
        compiler_params=_cparams(3),
        name="dattn_prompt",
    )(lam_p, subln_g.reshape(1, -1), qt, kb, vt)


def _head(x, h):
    return x[:, h * HEAD_W:(h + 1) * HEAD_W]


def _gla_chunk(q_in, k_in, k_end, v, dec, st_ref, tril_mask):
    q_in, k_in, k_end, v = (x.astype(BF16) for x in (q_in, k_in, k_end, v))
    heads = range(N_HEADS)
    sts = [st_ref[h] for h in heads]
    inter = [_bdot_nt(_head(q_in, h), sts[h]) for h in heads]
    upd = [_bdot_tn(_head(v, h), _head(k_end, h)) for h in heads]
    attn = [jnp.where(tril_mask, _bdot_nt(_head(q_in, h), _head(k_in, h)), 0.0) for h in heads]
    intra = [_bdot(attn[h], _head(v, h)) for h in heads]
    for h in heads:
        st_ref[h] = sts[h] * (dec[h] if isinstance(dec, (list, tuple)) else _head(dec, h)) + upd[h]
    return [intra[h] + inter[h] for h in heads]


def _hgrn_kernel(q_ref, f_ref, i_ref, lb_ref, o_ref, s_ref, st_ref, *, tt, L):
    t = pl.program_id(1)

    @pl.when(t == 0)
    def _():
        st_ref[...] = jnp.zeros(st_ref.shape, F32)

    tril_mask = _tri(L, True)
    tril = tril_mask.astype(BF16)

    def body(c, carry):
        rows = pl.ds(pl.multiple_of(c * L, L), L)
        lb = lb_ref[...]
        f = lb + (1.0 - lb) * _sigmoid(f_ref[0, rows, :])
        b = _cumsum_rows(tril, jnp.log(f))
        b_end = b[L - 1:L, :]
        k = 1.0 - f
        q_in = _silu(q_ref[0, rows, :]) * jnp.exp(b)
        o = _gla_chunk(q_in, k * jnp.exp(-b), k * jnp.exp(b_end - b), i_ref[0, rows, :], jnp.exp(b_end),
                       st_ref, tril_mask)
        o_ref[0, rows, :] = jnp.concatenate(o, axis=1)
        return carry

    lax.fori_loop(0, tt // L, body, 0, unroll=2)

    @pl.when(t == pl.num_programs(1) - 1)
    def _():
        s_ref[0] = st_ref[...]


def _hgrn_prompt(h3, lb, tt):
    B, T, _ = h3.shape
    tt = min(tt, T)
    L = math.gcd(T, CHUNK)
    col = lambda c: pl.BlockSpec((1, tt, MIX_W), lambda b, t: (b, t, c))
    return pl.pallas_call(
        functools.partial(_hgrn_kernel, tt=tt, L=L),
        out_shape=(jax.ShapeDtypeStruct((B, T, MIX_W), F32),
                   jax.ShapeDtypeStruct((B, N_HEADS, HEAD_W, HEAD_W), F32)),
        grid=(B, T // tt),
        in_specs=[col(0), col(1), col(2), pl.BlockSpec((1, MIX_W), lambda b, t: (0, 0))],
        out_specs=(pl.BlockSpec((1, tt, MIX_W), lambda b, t: (b, t, 0)),
                   pl.BlockSpec((1, N_HEADS, HEAD_W, HEAD_W), lambda b, t: (b, 0, 0, 0))),
        scratch_shapes=[pltpu.VMEM((N_HEADS, HEAD_W, HEAD_W), F32)],
        compiler_params=_cparams(2),
        name="hgrn_prompt",
    )(h3, h3, h3, lb.reshape(1, -1))


def _log_gamma(h):
    return float(np.log(1.0 - 2.0 ** (-5.0 - h)))


def _od_kernel(cq_ref, ck_ref, cv_ref, dq_ref, dk_ref, dv_ref, gc_ref, gr_ref,
               bc_ref, br_ref, cos_ref, sin_ref,
               oc_ref, od_ref, sc_ref, dc_ref, dn_ref, dm_ref,
               st_ref, ct_ref, n_ref, m_ref, *, tt, L):
    t = pl.program_id(1)

    @pl.when(t == 0)
    def _():
        st_ref[...] = jnp.zeros(st_ref.shape, F32)
        ct_ref[...] = jnp.zeros(ct_ref.shape, F32)
        n_ref[...] = jnp.zeros(n_ref.shape, F32)
        m_ref[...] = jnp.zeros(m_ref.shape, F32)

    tril_mask = _tri(L, True)
    tril = tril_mask.astype(BF16)
    triu = _tri(L, False).astype(BF16)
    scale = HEAD_W ** -0.5
    heads = range(N_HEADS)
    pos1 = (lax.broadcasted_iota(jnp.int32, (L, HEAD_W), 0) + 1).astype(F32)
    ret_b = jnp.concatenate([pos1 * _log_gamma(h) for h in heads], axis=1)
    ret_b_end = jnp.concatenate([jnp.full((L, HEAD_W), L * _log_gamma(h), F32) for h in heads], axis=1)
    ret_q_dec, ret_k_dec, ret_k_end = jnp.exp(ret_b), jnp.exp(-ret_b), jnp.exp(ret_b_end - ret_b)
    ret_dec = [math.exp(L * _log_gamma(h)) for h in heads]

    def body(c, carry):
        rows = pl.ds(pl.multiple_of(c * L, L), L)
        cos, sin = cos_ref[rows, :], sin_ref[rows, :]
        q = _rope(cq_ref[0, rows, :], cos, sin, HEAD_W // 2)
        k = _rope(ck_ref[0, rows, :], cos, sin, HEAD_W // 2) * scale
        o = _gla_chunk(q * ret_q_dec, k * ret_k_dec, k * ret_k_end, cv_ref[0, rows, :], ret_dec, st_ref, tril_mask)
        oc_ref[0, rows, :] = jnp.concatenate(o, axis=1)
        g_c = gc_ref[0, rows, :] + bc_ref[...]
        g_r = gr_ref[0, c] + br_ref[...]
        lf_c = _log_sigmoid(g_c)
        b_c = _cumsum_rows(tril, jnp.concatenate(
            [jnp.broadcast_to(lf_c[:, N_HEADS + h:N_HEADS + h + 1], (L, HEAD_W)) for h in heads], axis=1))
        b_r = _cumsum_lanes(_log_sigmoid(g_r), triu)
        dq = (dq_ref[0, rows, :] * scale)
        dk = dk_ref[0, rows, :]
        dqb, dkb, dvb = dq.astype(BF16), dk.astype(BF16), dv_ref[0, rows, :].astype(BF16)
        cts = [ct_ref[h] for h in heads]
        q_c = [_bdot_nt(_head(dqb, h), cts[h]) for h in heads]
        q_k = [_bdot_nt(_head(dqb, h), _head(dkb, h)) for h in heads]
        m_prev = [m_ref[h:h + 1, 0:1] for h in heads]
        bc1, m_t, w_mat = [], [], []
        for h in heads:
            bc = _head(b_c, h)[:, :L]
            dm = jnp.where(tril_mask, bc - b_r[N_HEADS + h:N_HEADS + h + 1, :] + g_r[h:h + 1, :], NEG_INF)
            bc1.append(bc[:, 0:1])
            m_t.append(jnp.maximum(bc1[h] + m_prev[h], jnp.max(dm, axis=1, keepdims=True)))
            w_mat.append(jnp.exp(dm - m_t[h]) * q_k[h])
        w_v = [_bdot(w_mat[h], _head(dvb, h)) for h in heads]
        kws, c_scales = [], []
        for h in heads:
            m_new = m_t[h][L - 1:L, :]
            b_last = bc1[h][L - 1:L, :]
            c_scales.append(jnp.exp(b_last + m_prev[h] - m_new))
            kws.append(_head(dk, h) * jnp.exp(b_last - bc1[h] + g_c[:, h:h + 1] - m_new))
            m_ref[h:h + 1, :] = jnp.broadcast_to(m_new, (1, LANES))
        c_upd = [_bdot_tn(_head(dvb, h), kws[h]) for h in heads]
        hs = []
        for h in heads:
            n_row = n_ref[h:h + 1, :]
            inter = jnp.exp(bc1[h] + m_prev[h] - m_t[h])
            num = inter * q_c[h] + w_v[h]
            den = (inter * jnp.sum(_head(dq, h) * n_row, axis=1, keepdims=True)
                   + jnp.sum(w_mat[h], axis=1, keepdims=True))
            hs.append(num / jnp.maximum(jnp.abs(den), jnp.exp(-m_t[h])))
            ct_ref[h] = c_scales[h] * cts[h] + c_upd[h]
            n_ref[h:h + 1, :] = c_scales[h] * n_row + jnp.sum(kws[h], axis=0, keepdims=True)
        od_ref[0, rows, :] = jnp.concatenate(hs, axis=1)
        return carry

    lax.fori_loop(0, tt // L, body, 0)

    @pl.when(t == pl.num_programs(1) - 1)
    def _():
        sc_ref[0] = st_ref[...]
        dc_ref[0] = ct_ref[...]
        dn_ref[0] = n_ref[...]
        dm_ref[0] = m_ref[...]


def _od_in_kernel(x_ref, w_ref, wg_ref, h_ref, g_ref, xb_ref):
    @pl.when(pl.program_id(1) == 0)
    def _():
        xb_ref[...] = x_ref[...].astype(BF16)

    xb = xb_ref[...]
    h_ref[...] = jnp.dot(xb, w_ref[...], preferred_element_type=F32)

    @pl.when(pl.program_id(1) == 0)
    def _():
        g_ref[...] = jnp.dot(xb, wg_ref[...], preferred_element_type=F32)


def _od_in_prompt(x, w_in, w_gate, tm, tn):
    M, K = x.shape
    N = w_in.shape[1]
    tm = min(tm, M)
    return pl.pallas_call(
        _od_in_kernel,
        out_shape=(jax.ShapeDtypeStruct((M, N), F32), jax.ShapeDtypeStruct((M, LANES), F32)),
        grid=(M // tm, N // tn),
        in_specs=[pl.BlockSpec((tm, K), lambda i, j: (i, 0)),
                  pl.BlockSpec((K, tn), lambda i, j: (0, j)),
                  pl.BlockSpec((K, LANES), lambda i, j: (0, 0))],
        out_specs=(pl.BlockSpec((tm, tn), lambda i, j: (i, j)), pl.BlockSpec((tm, LANES), lambda i, j: (i, 0))),
        scratch_shapes=[pltpu.VMEM((tm, K), BF16)],
        compiler_params=_cparams(2),
        name="od_in_prompt",
    )(x, w_in, w_gate)


def _od_prompt(h3, gates, b_if, cos_t, sin_t, tt):
    B, T, _ = h3.shape
    tt = min(tt, T)
    L = math.gcd(T, OD_CHUNK)
    gates_r = gates[:, :, :2 * N_HEADS].reshape(B, T // L, L, 2 * N_HEADS).transpose(0, 1, 3, 2)
    b_pad = jnp.zeros((1, LANES), F32).at[0, :2 * N_HEADS].set(b_if)
    col = lambda c: pl.BlockSpec((1, tt, MIX_W), lambda b, t: (b, t, c))
    tab = pl.BlockSpec((tt, LANES), lambda b, t: (t, 0))
    mat_state = pl.BlockSpec((1, N_HEADS, HEAD_W, HEAD_W), lambda b, t: (b, 0, 0, 0))
    row_state = pl.BlockSpec((1, SUBLANES, LANES), lambda b, t: (b, 0, 0))
    out_blk = pl.BlockSpec((1, tt, MIX_W), lambda b, t: (b, t, 0))
    return pl.pallas_call(
        functools.partial(_od_kernel, tt=tt, L=L),
        out_shape=(jax.ShapeDtypeStruct((B, T, MIX_W), F32), jax.ShapeDtypeStruct((B, T, MIX_W), F32),
                   jax.ShapeDtypeStruct((B, N_HEADS, HEAD_W, HEAD_W), F32),
                   jax.ShapeDtypeStruct((B, N_HEADS, HEAD_W, HEAD_W), F32),
                   jax.ShapeDtypeStruct((B, SUBLANES, LANES), F32),
                   jax.ShapeDtypeStruct((B, SUBLANES, LANES), F32)),
        grid=(B, T // tt),
        in_specs=[col(0), col(1), col(2), col(4), col(5), col(6),
                  pl.BlockSpec((1, tt, LANES), lambda b, t: (b, t, 0)),
                  pl.BlockSpec((1, tt // L, 2 * N_HEADS, L), lambda b, t: (b, t, 0, 0)),
                  pl.BlockSpec((1, LANES), lambda b, t: (0, 0)),
                  pl.BlockSpec((2 * N_HEADS, 1), lambda b, t: (0, 0)),
                  tab, tab],
        out_specs=(out_blk, out_blk, mat_state, mat_state, row_state, row_state),
        scratch_shapes=[pltpu.VMEM((N_HEADS, HEAD_W, HEAD_W), F32), pltpu.VMEM((N_HEADS, HEAD_W, HEAD_W), F32),
                        pltpu.VMEM((SUBLANES, LANES), F32), pltpu.VMEM((SUBLANES, LANES), F32)],
        compiler_params=_cparams(2),
        name="od_prompt",
    )(h3, h3, h3, h3, h3, h3, gates, gates_r, b_pad, b_if.reshape(-1, 1), cos_t, sin_t)


def _xattn_kernel(x_ref, wq_ref, mk_ref, mv_ref, wo_ref, g_ref, b_ref, o_ref):
    x = x_ref[...]
    q = _bdot(x, wq_ref[...])
    qb = (q * (DH_X ** -0.5)).astype(BF16)
    heads = range(N_HEADS)
    cols = [slice(h * DH_X, (h + 1) * DH_X) for h in heads]
    s = [lax.dot_general(qb[:, cols[h]], mk_ref[0, :, cols[h]], NT_DIMS, preferred_element_type=F32) for h in heads]
    p = [jnp.exp(s[h] - jnp.max(s[h], axis=1, keepdims=True)) for h in heads]
    l = [jnp.sum(p[h], axis=1, keepdims=True) for h in heads]
    pv = [jnp.dot(p[h].astype(BF16), mv_ref[0, :, cols[h]], preferred_element_type=F32) for h in heads]
    outs = [(pv[h] / l[h]).astype(BF16) for h in heads]
    y = jnp.dot(jnp.concatenate(outs, axis=1), wo_ref[...], preferred_element_type=F32)
    o_ref[...] = _layernorm(DN_ALPHA * x + y, g_ref[...], b_ref[...])


def _xattn_prompt(x, wq, mk, mv, wo, g, b, T, tm):
    M = x.shape[0]
    tm = min(tm, T)
    n_mem = mk.shape[1]
    per_b = T // tm
    full = lambda shape: pl.BlockSpec(shape, lambda i: (0,) * len(shape))
    mem = pl.BlockSpec((1, n_mem, D_MODEL), lambda i: (i // per_b, 0, 0))
    return pl.pallas_call(
        _xattn_kernel,
        out_shape=jax.ShapeDtypeStruct((M, D_MODEL), F32),
        grid=(M // tm,),
        in_specs=[pl.BlockSpec((tm, D_MODEL), lambda i: (i, 0)), full((D_MODEL, D_MODEL)), mem, mem,
                  full((D_MODEL, D_MODEL)), full((1, D_MODEL)), full((1, D_MODEL))],
        out_specs=pl.BlockSpec((tm, D_MODEL), lambda i: (i, 0)),
        compiler_params=_cparams(1),
        name="xattn_prompt",
    )(x, wq, mk, mv, wo, g.reshape(1, -1), b.reshape(1, -1))


def _ffn_kernel(x_ref, wu_ref, wg_ref, cw_ref, cb_ref, wd_ref, g_ref, b_ref, o_ref, tail_ref,
                xb_ref, acc_ref, stage_ref, carry_ref, *, tm, per_b, single):
    i, j = pl.program_id(0), pl.program_id(1)

    @pl.when(j == 0)
    def _():
        xb_ref[...] = x_ref[...].astype(BF16)

    @pl.when(i % per_b == 0)
    def _():
        carry_ref[j] = jnp.zeros(carry_ref.shape[1:], F32)

    xb = xb_ref[...]
    stage_ref[0:SUBLANES, :] = carry_ref[j]
    stage_ref[SUBLANES:, :] = jnp.dot(xb, wu_ref[...], preferred_element_type=F32)
    last = stage_ref[tm:tm + SUBLANES, :]
    carry_ref[j] = last
    tail_ref[0] = last
    conv = (cb_ref[...] + cw_ref[0:1, :] * stage_ref[SUBLANES - 2:SUBLANES - 2 + tm, :]
            + cw_ref[1:2, :] * stage_ref[SUBLANES - 1:SUBLANES - 1 + tm, :]
            + cw_ref[2:3, :] * stage_ref[SUBLANES:, :])
    gate = jnp.dot(xb, wg_ref[...], preferred_element_type=F32)
    y = _bdot(jax.nn.gelu(conv) * gate, wd_ref[...])
    if single:
        o_ref[...] = _layernorm(DN_ALPHA * x_ref[...] + y, g_ref[...], b_ref[...])
        return

    @pl.when(j == 0)
    def _():
        acc_ref[...] = y

    @pl.when(j > 0)
    def _():
        acc_ref[...] = acc_ref[...] + y

    @pl.when(j == pl.num_programs(1) - 1)
    def _():
        o_ref[...] = _layernorm(DN_ALPHA * x_ref[...] + acc_ref[...], g_ref[...], b_ref[...])


def _ffn_prompt(x, w_up, conv_w, conv_b, w_down, g, b, T, tm, tf):
    M = x.shape[0]
    d_ff = w_down.shape[0]
    tm = min(tm, T)
    nf = d_ff // tf
    per_b = T // tm
    wmode = dict(pipeline_mode=pl.Buffered(1)) if nf == 1 else {}
    return pl.pallas_call(
        functools.partial(_ffn_kernel, tm=tm, per_b=per_b, single=nf == 1),
        out_shape=(jax.ShapeDtypeStruct((M, D_MODEL), F32),
                   jax.ShapeDtypeStruct((M // tm, SUBLANES, d_ff), F32)),
        grid=(M // tm, nf),
        in_specs=[pl.BlockSpec((tm, D_MODEL), lambda i, j: (i, 0)),
                  pl.BlockSpec((D_MODEL, tf), lambda i, j: (0, j), **wmode),
                  pl.BlockSpec((D_MODEL, tf), lambda i, j: (0, nf + j), **wmode),
                  pl.BlockSpec((3, tf), lambda i, j: (0, j)),
                  pl.BlockSpec((1, tf), lambda i, j: (0, j)),
                  pl.BlockSpec((tf, D_MODEL), lambda i, j: (j, 0), **wmode),
                  pl.BlockSpec((1, D_MODEL), lambda i, j: (0, 0)),
                  pl.BlockSpec((1, D_MODEL), lambda i, j: (0, 0))],
        out_specs=(pl.BlockSpec((tm, D_MODEL), lambda i, j: (i, 0)),
                   pl.BlockSpec((1, SUBLANES, tf), lambda i, j: (i, 0, j))),
        scratch_shapes=[pltpu.VMEM((tm, D_MODEL), BF16), pltpu.VMEM((tm, D_MODEL), F32),
                        pltpu.VMEM((tm + SUBLANES, tf), F32), pltpu.VMEM((nf, SUBLANES, tf), F32)],
        compiler_params=_cparams(2),
        name="ffn_prompt",
    )(x, w_up, w_up, conv_w, conv_b.reshape(1, -1), w_down, g.reshape(1, -1), b.reshape(1, -1))


def _dattn_sample_kernel(*refs, n_pages, page, lam_init):
    pt_ref = refs[0]
    lam_ref, g_ref, q_ref, kn_ref, vn_ref = refs[1:6]
    k_refs = refs[6:6 + n_pages]
    v_refs = refs[6 + n_pages:6 + 2 * n_pages]
    o_ref = refs[6 + 2 * n_pages]
    del pt_ref
    n_rows = 2 * N_HEADS
    q = q_ref[0]
    lane = lax.broadcasted_iota(jnp.int32, (n_rows, MIX_W), 1)
    row = lax.broadcasted_iota(jnp.int32, (n_rows, MIX_W), 0)
    qbd = jnp.where(lax.shift_right_logical(lane, 6) == row, jnp.broadcast_to(q, (n_rows, MIX_W)), 0.0)
    s_new = jnp.sum(qbd * kn_ref[0], axis=1, keepdims=True)
    scores = [_bdot(qbd, k_refs[p][0]) for p in range(n_pages)]
    m = s_new
    for s in scores:
        m = jnp.maximum(m, jnp.max(s, axis=1, keepdims=True))
    p_new = jnp.exp2(s_new - m)
    l = p_new
    vn = vn_ref[0]
    accs = [p_new * vn[:, h * HEAD_W:(h + 1) * HEAD_W] for h in range(N_HEADS)]
    for p in range(n_pages):
        pr = jnp.exp2(scores[p] - m)
        l = l + jnp.sum(pr, axis=1, keepdims=True)
        for h in range(N_HEADS):
            accs[h] = accs[h] + _bdot(pr, v_refs[p][0, pl.ds(h, page, stride=N_HEADS), :])
    lam = _lambda(lam_ref, lam_init)
    outs = []
    for h in range(N_HEADS):
        o = accs[h] / l
        d = o[2 * h:2 * h + 1, :] - lam * o[2 * h + 1:2 * h + 2, :]
        outs.append(_rmsnorm(d, g_ref[...]) * (1.0 - lam_init))
    o_ref[0] = jnp.concatenate(outs, axis=1)


def _dattn_sample(q, k_new, v_new, kt_pool, v_pool, pool_base, page_table, lam_p, subln_g, lam_init):
    DB = q.shape[0]
    n_pages = page_table.shape[1]
    page = kt_pool.shape[2]
    tok = pl.BlockSpec((1, 1, MIX_W), lambda b, pt: (b, 0, 0))

    def k_spec(p):
        return pl.BlockSpec((1, MIX_W, page), lambda b, pt: (pool_base + pt[b, p], 0, 0))

    def v_spec(p):
        return pl.BlockSpec((1, page * N_HEADS, HEAD_W), lambda b, pt: (pool_base + pt[b, p], 0, 0))

    grid_spec = pltpu.PrefetchScalarGridSpec(
        num_scalar_prefetch=1,
        grid=(DB,),
        in_specs=[pl.BlockSpec((4, DH_A), lambda b, pt: (0, 0)),
                  pl.BlockSpec((1, HEAD_W), lambda b, pt: (0, 0)), tok, tok, tok]
                 + [k_spec(p) for p in range(n_pages)] + [v_spec(p) for p in range(n_pages)],
        out_specs=tok,
    )
    out = pl.pallas_call(
        functools.partial(_dattn_sample_kernel, n_pages=n_pages, page=page, lam_init=lam_init),
        out_shape=jax.ShapeDtypeStruct((DB, 1, MIX_W), F32),
        grid_spec=grid_spec,
        compiler_params=_cparams(1),
        name="dattn_sample",
    )(page_table, lam_p, subln_g.reshape(1, -1), q.reshape(DB, 1, MIX_W), k_new.reshape(DB, 1, MIX_W),
      v_new.reshape(DB, 1, MIX_W), *([kt_pool] * n_pages), *([v_pool] * n_pages))
    return out.reshape(DB, MIX_W)


def _hgrn_step_kernel(h_ref, s0_ref, lb_ref, ng_ref, o_ref, s_ref, *, bt):
    def body(i, carry):
        row = h_ref[i]
        heads = range(N_HEADS)
        blk = lambda n: row[:, n * MIX_W:(n + 1) * MIX_W]
        lb = lb_ref[...]
        f = lb + (1.0 - lb) * _sigmoid(blk(4))
        q, v = _silu(blk(3)), blk(5)
        f_col = [_col(_head(f, h)) for h in heads]
        s_new = [f_col[h] * s0_ref[i, h] + (1.0 - f_col[h]) * _head(v, h) for h in heads]
        for h in heads:
            s_ref[i, h] = s_new[h]
        o = [_bdot(jnp.broadcast_to(_head(q, h), (SUBLANES, HEAD_W)), s_new[h])[0:1, :] for h in heads]
        o = jnp.concatenate([_rmsnorm(o[h], ng_ref[...]) for h in heads], axis=1)
        o_ref[i] = o * _silu(blk(6))
        return carry

    lax.fori_loop(0, bt, body, 0, unroll=2)


def _hgrn_step(h, s0, lb, norm_g, bt):
    DB, W = h.shape
    bt = min(bt, DB)
    state = pl.BlockSpec((bt, N_HEADS, HEAD_W, HEAD_W), lambda i: (i, 0, 0, 0))
    o, s = pl.pallas_call(
        functools.partial(_hgrn_step_kernel, bt=bt),
        out_shape=(jax.ShapeDtypeStruct((DB, 1, MIX_W), F32), jax.ShapeDtypeStruct(s0.shape, F32)),
        grid=(DB // bt,),
        in_specs=[pl.BlockSpec((bt, 1, W), lambda i: (i, 0, 0)), state,
                  pl.BlockSpec((1, MIX_W), lambda i: (0, 0)), pl.BlockSpec((1, HEAD_W), lambda i: (0, 0))],
        out_specs=(pl.BlockSpec((bt, 1, MIX_W), lambda i: (i, 0, 0)), state),
        compiler_params=_cparams(1),
        name="hgrn_step",
    )(h.reshape(DB, 1, W), s0, lb.reshape(1, -1), norm_g.reshape(1, -1))
    return o.reshape(DB, MIX_W), s


def _od_step_kernel(h_ref, gt_ref, bif_ref, cos_ref, sin_ref, sc0_ref, dc0_ref, dn0_ref, dm0_ref, cng_ref, dng_ref,
                    oc_ref, od_ref, sc_ref, dc_ref, dn_ref, dm_ref, *, bt):
    scale = HEAD_W ** -0.5

    def body(i, carry):
        row = h_ref[i]
        cos, sin = cos_ref[...], sin_ref[...]
        gates = gt_ref[i] + bif_ref[...]
        m_row = dm0_ref[i]
        heads = range(N_HEADS)
        blk = lambda n: row[:, n * MIX_W:(n + 1) * MIX_W]
        q = _rope(blk(0), cos, sin, HEAD_W // 2)
        k = _rope(blk(1), cos, sin, HEAD_W // 2) * scale
        cv, dk, dv, dq = blk(2), blk(5), blk(6), blk(4) * scale
        lf = _log_sigmoid(gates)
        m_new = [jnp.maximum(lf[:, N_HEADS + h:N_HEADS + h + 1] + m_row[:, h:h + 1], gates[:, h:h + 1])
                 for h in heads]
        c_scale = [jnp.exp(lf[:, N_HEADS + h:N_HEADS + h + 1] + m_row[:, h:h + 1] - m_new[h]) for h in heads]
        kw = [_head(dk, h) * jnp.exp(gates[:, h:h + 1] - m_new[h]) for h in heads]
        k_col = [_col(_head(k, h)) for h in heads]
        kw_col = [_col(kw[h]) for h in heads]
        s_new = [math.exp(_log_gamma(h)) * sc0_ref[i, h] + k_col[h] * _head(cv, h) for h in heads]
        c_new = [c_scale[h] * dc0_ref[i, h] + kw_col[h] * _head(dv, h) for h in heads]
        n_new = [c_scale[h] * dn0_ref[i, h:h + 1, :] + kw[h] for h in heads]
        for h in heads:
            sc_ref[i, h] = s_new[h]
            dc_ref[i, h] = c_new[h]
            dn_ref[i, h:h + 1, :] = n_new[h]
        rows8 = lambda x: jnp.broadcast_to(x, (SUBLANES, HEAD_W))
        o = [_bdot(rows8(_head(q, h)), s_new[h])[0:1, :] for h in heads]
        num = [_bdot(rows8(_head(dq, h)), c_new[h])[0:1, :] for h in heads]
        den = [jnp.sum(_head(dq, h) * n_new[h], axis=1, keepdims=True) for h in heads]
        hh = [num[h] / jnp.maximum(jnp.abs(den[h]), jnp.exp(-m_new[h])) for h in heads]
        oc = jnp.concatenate([_groupnorm(o[h], cng_ref[...]) for h in heads], axis=1)
        od = jnp.concatenate([_groupnorm(hh[h], dng_ref[...]) for h in heads], axis=1)
        oc_ref[i] = oc * _silu(blk(3))
        od_ref[i] = od * _sigmoid(blk(7))
        dm_ref[i] = jnp.concatenate(m_new, axis=1)
        return carry

    lax.fori_loop(0, bt, body, 0)


def _od_step(h, gates, b_if, cos_t, sin_t, sc0, dc0, dn0, dm0, c_norm_g, d_norm_g, bt):
    DB, W = h.shape
    bt = min(bt, DB)
    mat = pl.BlockSpec((bt, N_HEADS, HEAD_W, HEAD_W), lambda i: (i, 0, 0, 0))
    nblk = pl.BlockSpec((bt, N_HEADS, HEAD_W), lambda i: (i, 0, 0))
    mblk = pl.BlockSpec((bt, 1, N_HEADS), lambda i: (i, 0, 0))
    vec = pl.BlockSpec((1, LANES), lambda i: (0, 0))
    out = pl.BlockSpec((bt, 1, MIX_W), lambda i: (i, 0, 0))
    b_pad = jnp.zeros((1, LANES), F32).at[0, :2 * N_HEADS].set(b_if)
    oc, od, sc, dc, dn, dm = pl.pallas_call(
        functools.partial(_od_step_kernel, bt=bt),
        out_shape=(jax.ShapeDtypeStruct((DB, 1, MIX_W), F32), jax.ShapeDtypeStruct((DB, 1, MIX_W), F32),
                   jax.ShapeDtypeStruct(sc0.shape, F32), jax.ShapeDtypeStruct(dc0.shape, F32),
                   jax.ShapeDtypeStruct(dn0.shape, F32), jax.ShapeDtypeStruct((DB, 1, N_HEADS), F32)),
        grid=(DB // bt,),
        in_specs=[pl.BlockSpec((bt, 1, W), lambda i: (i, 0, 0)),
                  pl.BlockSpec((bt, 1, LANES), lambda i: (i, 0, 0)), vec, vec, vec,
                  mat, mat, nblk, mblk, vec, vec],
        out_specs=(out, out, mat, mat, nblk, mblk),
        compiler_params=_cparams(1),
        name="od_step",
    )(h.reshape(DB, 1, W), gates.reshape(DB, 1, LANES), b_pad, cos_t, sin_t, sc0, dc0, dn0,
      dm0.reshape(DB, 1, N_HEADS), c_norm_g.reshape(1, -1), d_norm_g.reshape(1, -1))
    return oc.reshape(DB, MIX_W), od.reshape(DB, MIX_W), sc, dc, dn, dm.reshape(DB, N_HEADS)


X_HALVES = DH_X // LANES
X_ROWS = N_HEADS * X_HALVES


def _xattn_sample_kernel(q_ref, mk_ref, mv_ref, o_ref, *, xb):
    for t in range(xb):
        _xattn_sample_row(q_ref.at[t], mk_ref.at[t], mv_ref.at[t], o_ref.at[t])


def _xattn_sample_row(q_ref, mk_ref, mv_ref, o_ref):
    q = q_ref[...] * (DH_X ** -0.5)
    n_cols = mk_ref.shape[0]
    q2 = jnp.concatenate([q[:, h * DH_X + c * LANES:h * DH_X + (c + 1) * LANES]
                          for c in range(X_HALVES) for h in range(N_HEADS)], axis=0)
    s2 = _bdot_nt(q2, mk_ref[...])
    row = lax.broadcasted_iota(jnp.int32, (X_ROWS, n_cols), 0)
    col = lax.broadcasted_iota(jnp.int32, (X_ROWS, n_cols), 1)
    own = (col & (X_ROWS - 1)) == row
    s2 = jnp.where(own, s2, 0.0)
    s4 = s2[:N_HEADS] + pltpu.roll(s2, n_cols - N_HEADS, 1)[N_HEADS:]
    own4 = own[:N_HEADS]
    s4 = jnp.where(own4, s4, NEG_INF)
    p4 = jnp.exp(s4 - jnp.max(s4, axis=1, keepdims=True))
    l4 = jnp.sum(p4, axis=1, keepdims=True)
    p4 = p4 / l4
    p2 = jnp.concatenate([p4, pltpu.roll(p4, N_HEADS, 1)], axis=0)
    o2 = _bdot(p2, mv_ref[...])
    o_ref[...] = jnp.concatenate([o2[c * N_HEADS + h:c * N_HEADS + h + 1, :]
                                  for h in range(N_HEADS) for c in range(X_HALVES)], axis=1)


def _xattn_sample(q, mk, mv, base):
    DB = q.shape[0]
    n_rows = mk.shape[1]
    xb = math.gcd(DB, 4)
    tok = pl.BlockSpec((xb, 1, D_MODEL), lambda b: (b, 0, 0))
    mem = pl.BlockSpec((xb, n_rows, LANES), lambda b: (base // xb + b, 0, 0))
    assert base % xb == 0
    out = pl.pallas_call(
        functools.partial(_xattn_sample_kernel, xb=xb),
        out_shape=jax.ShapeDtypeStruct((DB, 1, D_MODEL), F32),
        grid=(DB // xb,),
        in_specs=[tok, mem, mem],
        out_specs=tok,
        compiler_params=_cparams(1),
        name="xattn_sample",
    )(q.reshape(DB, 1, D_MODEL), mk, mv)
    return out.reshape(DB, D_MODEL)


def _mem_rows(cache):
    n_l, DB, n_mem = cache.shape[:3]
    c = cache.reshape(n_l, DB, n_mem, N_HEADS, X_HALVES, LANES).transpose(0, 1, 2, 4, 3, 5)
    return c.reshape(n_l * DB, n_mem * X_ROWS, LANES)


def _ffn_sample_kernel(x_ref, up_ref, gate_ref, b0_ref, b1_ref, cw_ref, cb_ref, wd_ref, g_ref, b_ref, o_ref, acc_ref):
    j = pl.program_id(0)
    conv = (cb_ref[...] + cw_ref[0:1, :] * b0_ref[...] + cw_ref[1:2, :] * b1_ref[...]
            + cw_ref[2:3, :] * up_ref[...])
    y = _bdot(jax.nn.gelu(conv) * gate_ref[...], wd_ref[...])

    @pl.when(j == 0)
    def _():
        acc_ref[...] = y

    @pl.when(j > 0)
    def _():
        acc_ref[...] = acc_ref[...] + y

    @pl.when(j == pl.num_programs(0) - 1)
    def _():
        o_ref[...] = _layernorm(DN_ALPHA * x_ref[...] + acc_ref[...], g_ref[...], b_ref[...])


def _ffn_sample(x, ug, buf0, buf1, conv_w, conv_b, w_down, g, b, tf):
    DB = x.shape[0]
    d_ff = w_down.shape[0]
    nf = d_ff // tf
    ff = lambda off: pl.BlockSpec((DB, tf), lambda j: (0, off + j))
    full = lambda shape: pl.BlockSpec(shape, lambda j: (0,) * len(shape))
    return pl.pallas_call(
        _ffn_sample_kernel,
        out_shape=jax.ShapeDtypeStruct((DB, D_MODEL), F32),
        grid=(nf,),
        in_specs=[full((DB, D_MODEL)), ff(0), ff(nf), ff(0), ff(0),
                  pl.BlockSpec((3, tf), lambda j: (0, j)), pl.BlockSpec((1, tf), lambda j: (0, j)),
                  pl.BlockSpec((tf, D_MODEL), lambda j: (j, 0)), full((1, D_MODEL)), full((1, D_MODEL))],
        out_specs=full((DB, D_MODEL)),
        scratch_shapes=[pltpu.VMEM((DB, D_MODEL), F32)],
        compiler_params=_cparams(1),
        name="ffn_sample",
    )(x, ug, ug, buf0, buf1, conv_w, conv_b.reshape(1, -1), w_down, g.reshape(1, -1), b.reshape(1, -1))


def kernel(x_prompt, x_sample, cache_a_k, cache_a_v, state_b, state_c, state_d_c, state_d_n, state_d_m,
           cache_mem_k, cache_mem_v, state_conv, page_table, mem_prompt,
           ev_w_in, ev_w_out, ev_lam, ev_subln_g, ev_lb_logits, ev_b_norm_g,
           od_w_in, od_b_if, od_w_out, od_c_norm_g, od_d_norm_g,
           ln_g, ln_b, xa_wq, xa_wkv, xa_wo, ffn_w_up, ffn_conv_w, ffn_conv_b, ffn_w_down):
    B, T, _ = x_prompt.shape
    DB, t_s, _ = x_sample.shape
    assert t_s == 1, "the sample group is a single decoding step"
    assert T % CHUNK == 0
    n_pool, page = cache_a_k.shape[1], cache_a_k.shape[2]
    n_pages = page_table.shape[1]
    past = n_pages * page
    n_mem = mem_prompt.shape[1]
    d_ff = ffn_w_down.shape[1]
    tf = d_ff // 2
    N = B * T
    TM = 512

    pos_p = jnp.arange(T)
    pos_s = jnp.full((DB,), past, jnp.int32)
    lb_table = jnp.cumsum(jax.nn.softmax(ev_lb_logits.astype(F32), axis=0), axis=0)

    xp = x_prompt.reshape(N, D_MODEL)
    xs = x_sample.reshape(DB, D_MODEL)
    outs = {k: [] for k in ("ak_p", "av_p", "ak_s", "av_s", "sb_p", "sb_s", "sc_p", "sc_s", "dc_p", "dc_s",
                            "dn_p", "dn_s", "dm_p", "dm_s", "mk_p", "mv_p", "cv_p", "cv_s")}
    swap = lambda s: jnp.swapaxes(s, -1, -2)
    mem_k_rows, mem_v_rows = _mem_rows(cache_mem_k), _mem_rows(cache_mem_v)

    for l in range(DEPTH):
        j = l // 2
        if l % 2 == 0:
            lam_init = 0.8 - 0.6 * math.exp(-0.3 * l)
            w_in = ev_w_in[j].astype(BF16)
            w_out = ev_w_out[j].astype(BF16)
            cos_t, sin_t = _rope_tables(pos_p, DH_A)
            qt, kb, kt32, vt, v_rows, h_b = _ev_in_prompt(xp.reshape(B, T, D_MODEL), w_in, cos_t, sin_t, TM)
            o_a = _dattn_prompt(qt, kb, vt, ev_lam[j], ev_subln_g[j], lam_init, 512)
            o_b, st = _hgrn_prompt(h_b, lb_table[j], 512)
            outs["ak_p"].append(kt32.reshape(B, N_HEADS, 2, DH_A, T).transpose(0, 4, 1, 2, 3))
            outs["av_p"].append(v_rows.reshape(B, T, N_HEADS, HEAD_W))
            outs["sb_p"].append(swap(st))
            xp = _proj_ln([o_a.reshape(N, MIX_W), o_b.reshape(N, MIX_W)], [w_out[:MIX_W], w_out[MIX_W:]],
                          xp, ln_g[l, 0], ln_b[l, 0], TM, "ev_out_prompt",
                          gates=[None, ("rms", ev_b_norm_g[j], "silu", h_b.reshape(N, -1), 3)])
            hs = _matmul(xs, w_in, DB, 512, "ev_in_sample")
            cos_s, sin_s = _rope_tables(pos_s, DH_A)
            qs, ks32, kts32 = _ev_prep_sample(hs, cos_s, sin_s)
            vs32 = hs[:, 2 * MIX_W:3 * MIX_W]
            kt_pool = cache_a_k.transpose(0, 1, 3, 4, 5, 2).reshape(-1, MIX_W, page)
            v_pool = cache_a_v.reshape(-1, page * N_HEADS, HEAD_W)
            oa_s = _dattn_sample(qs, ks32, vs32, kt_pool, v_pool, j * n_pool, page_table,
                                 ev_lam[j], ev_subln_g[j], lam_init)
            ob_s, sb_s = _hgrn_step(hs, state_b[j], lb_table[j], ev_b_norm_g[j], 8)
            outs["ak_s"].append(kts32.reshape(N_HEADS, 2, DH_A, DB).transpose(3, 0, 1, 2)[:, None])
            outs["av_s"].append(vs32.reshape(DB, 1, N_HEADS, HEAD_W))
            outs["sb_s"].append(sb_s)
            xs = _proj_ln([oa_s, ob_s], [w_out[:MIX_W], w_out[MIX_W:]], xs, ln_g[l, 0], ln_b[l, 0], DB,
                          "ev_out_sample")
        else:
            n_main = 8 * MIX_W
            w_in = od_w_in[j][:, :n_main].astype(BF16)
            w_gate = jnp.pad(od_w_in[j][:, n_main:], ((0, 0), (0, LANES - 2 * N_HEADS))).astype(BF16)
            w_out = od_w_out[j].astype(BF16)
            h, gates = _od_in_prompt(xp, w_in, w_gate, TM, w_in.shape[1])
            cos_t, sin_t = _rope_tables(pos_p, HEAD_W)
            o_c, o_d, sct, dct, dn, dm = _od_prompt(h.reshape(B, T, -1), gates.reshape(B, T, -1), od_b_if[j],
                                                    cos_t, sin_t, 512)
            outs["sc_p"].append(swap(sct))
            outs["dc_p"].append(swap(dct))
            outs["dn_p"].append(dn[:, :N_HEADS, :])
            outs["dm_p"].append(dm[:, :N_HEADS, 0])
            xp = _proj_ln([o_c.reshape(N, MIX_W), o_d.reshape(N, MIX_W)], [w_out[:MIX_W], w_out[MIX_W:]],
                          xp, ln_g[l, 0], ln_b[l, 0], TM, "od_out_prompt",
                          gates=[("group", od_c_norm_g[j], "silu", h, 3), ("group", od_d_norm_g[j], "sigmoid", h, 7)])
            hs = _matmul(xs, w_in, DB, 512, "od_in_sample")
            gates_s = _matmul(xs, w_gate, DB, LANES, "od_gates_sample")
            cos_s, sin_s = _rope_tables(pos_s[:1], HEAD_W)
            oc_s, od_s, sc_s, dc_s, dn_s, dm_s = _od_step(hs, gates_s, od_b_if[j], cos_s, sin_s,
                                                          state_c[j], state_d_c[j], state_d_n[j], state_d_m[j],
                                                          od_c_norm_g[j], od_d_norm_g[j], 8)
            outs["sc_s"].append(sc_s)
            outs["dc_s"].append(dc_s)
            outs["dn_s"].append(dn_s)
            outs["dm_s"].append(dm_s)
            xs = _proj_ln([oc_s, od_s], [w_out[:MIX_W], w_out[MIX_W:]], xs, ln_g[l, 0], ln_b[l, 0], DB,
                          "od_out_sample")

        wq = xa_wq[l].astype(BF16)
        wo = xa_wo[l].astype(BF16)
        mkv = _matmul(mem_prompt.reshape(B * n_mem, D_MODEL), xa_wkv[l].astype(BF16), 512, 512, "mem_kv")
        mk, mv = mkv[:, :D_MODEL], mkv[:, D_MODEL:]
        outs["mk_p"].append(mk.reshape(B, n_mem, N_HEADS, DH_X))
        outs["mv_p"].append(mv.reshape(B, n_mem, N_HEADS, DH_X))
        xp = _xattn_prompt(xp, wq, mk.astype(BF16).reshape(B, n_mem, D_MODEL),
                           mv.astype(BF16).reshape(B, n_mem, D_MODEL), wo, ln_g[l, 1], ln_b[l, 1], T, 1024)
        q_s = _matmul(xs, wq, DB, 512, "xattn_q_sample")
        xo_s = _xattn_sample(q_s, mem_k_rows, mem_v_rows, l * DB)
        xs = _proj_ln([xo_s], [wo], xs, ln_g[l, 1], ln_b[l, 1], DB, "xattn_out_sample")

        w_up = ffn_w_up[l].astype(BF16)
        w_down = ffn_w_down[l].astype(BF16)
        tm_f = min(TM, T)
        xp, tails = _ffn_prompt(xp, w_up, ffn_conv_w[l], ffn_conv_b[l], w_down, ln_g[l, 2], ln_b[l, 2], T, TM, d_ff)
        tails = tails.reshape(B, T // tm_f, SUBLANES, d_ff)
        outs["cv_p"].append(tails[:, -1, SUBLANES - 2:, :])
        ug_s = _matmul(xs, w_up, DB, tf, "ffn_up_sample")
        buf = state_conv[l]
        xs = _ffn_sample(xs, ug_s, buf[:, 0, :], buf[:, 1, :], ffn_conv_w[l], ffn_conv_b[l], w_down,
                         ln_g[l, 2], ln_b[l, 2], tf)
        outs["cv_s"].append(jnp.stack([buf[:, 1, :], ug_s[:, :d_ff]], axis=1))

    st = lambda k: jnp.stack(outs[k])
    return (xp.reshape(B, T, D_MODEL), xs.reshape(DB, 1, D_MODEL),
            st("ak_p"), st("av_p"), st("ak_s"), st("av_s"), st("sb_p"), st("sb_s"),
            st("sc_p"), st("sc_s"), st("dc_p"), st("dc_s"), st("dn_p"), st("dn_s"), st("dm_p"), st("dm_s"),
            st("mk_p"), st("mv_p"), st("cv_p"), st("cv_s"))
```

```python
import functools
import math

import numpy as np
import jax
import jax.numpy as jnp
from jax import lax
from jax.experimental import pallas as pl
from jax.experimental.pallas import tpu as pltpu

F32 = jnp.float32
BF16 = jnp.bfloat16

D_MODEL = 1024
MIX_W = D_MODEL // 2
N_HEADS = 4
HEAD_W = MIX_W // N_HEADS
DH_A = HEAD_W // 2
DH_X = D_MODEL // N_HEADS
CHUNK = 64
OD_CHUNK = 128
ROPE_THETA = 10000.0
NORM_EPS = 1e-5
DEPTH = 2
DN_ALPHA = (2.0 * DEPTH) ** 0.25
LANES = 128
SUBLANES = 8
VMEM_LIMIT = 56 * 1024 * 1024
NEG_INF = float("-inf")

NT_DIMS = (((1,), (1,)), ((), ()))
TN_DIMS = (((0,), (0,)), ((), ()))


def _cparams(n_axes, vmem=VMEM_LIMIT):
    return pltpu.CompilerParams(dimension_semantics=("arbitrary",) * n_axes, vmem_limit_bytes=vmem)


def _bdot(a, b):
    return jnp.dot(a.astype(BF16), b.astype(BF16), preferred_element_type=F32)


def _bdot_nt(a, b):
    return lax.dot_general(a.astype(BF16), b.astype(BF16), NT_DIMS, preferred_element_type=F32)


def _bdot_tn(a, b):
    return lax.dot_general(a.astype(BF16), b.astype(BF16), TN_DIMS, preferred_element_type=F32)


def _split3(x):
    p1 = x.astype(BF16)
    r1 = x - p1.astype(F32)
    p2 = r1.astype(BF16)
    p3 = (r1 - p2.astype(F32)).astype(BF16)
    return p1, p2, p3


def _cumsum_rows(tri, x):
    p1, p2, p3 = _split3(x)
    d = functools.partial(jnp.dot, preferred_element_type=F32)
    return d(tri, p1) + d(tri, p2) + d(tri, p3)


def _cumsum_lanes(x, triu):
    p1, p2, p3 = _split3(x)
    d = functools.partial(jnp.dot, preferred_element_type=F32)
    return d(p1, triu) + d(p2, triu) + d(p3, triu)


def _tri(L, lower):
    r = lax.broadcasted_iota(jnp.int32, (L, L), 0)
    c = lax.broadcasted_iota(jnp.int32, (L, L), 1)
    return (r >= c) if lower else (r <= c)


def _sigmoid(x):
    return 1.0 / (1.0 + jnp.exp(-x))


def _silu(x):
    return x * _sigmoid(x)


def _log_sigmoid(x):
    return jnp.minimum(x, 0.0) - jnp.log(1.0 + jnp.exp(-jnp.abs(x)))


def _layernorm(z, g, b):
    mu = jnp.mean(z, -1, keepdims=True)
    zc = z - mu
    var = jnp.mean(zc * zc, -1, keepdims=True)
    return zc * lax.rsqrt(var + NORM_EPS) * g + b


def _rmsnorm(x, g):
    return x * lax.rsqrt(jnp.mean(x * x, -1, keepdims=True) + NORM_EPS) * g


def _groupnorm(x, g):
    mu = jnp.mean(x, -1, keepdims=True)
    xc = x - mu
    var = jnp.mean(xc * xc, -1, keepdims=True)
    return xc * lax.rsqrt(var + NORM_EPS) * g


def _rope(x, cos, sin, half):
    outs = []
    for c in range(x.shape[1] // LANES):
        xc = x[:, c * LANES:(c + 1) * LANES]
        if 2 * half == LANES:
            sw = pltpu.roll(xc, half, 1)
        else:
            lane = lax.broadcasted_iota(jnp.int32, xc.shape, 1)
            first = (lane & (2 * half - 1)) < half
            sw = jnp.where(first, pltpu.roll(xc, LANES - half, 1), pltpu.roll(xc, half, 1))
        outs.append(xc * cos + sw * sin)
    return outs[0] if len(outs) == 1 else jnp.concatenate(outs, axis=1)


def _rope_tables(pos, d):
    inv = ROPE_THETA ** (-jnp.arange(0, d // 2, dtype=F32) * 2.0 / d)
    ang = pos.astype(F32)[:, None] * inv[None, :]
    cos, sin = jnp.cos(ang), jnp.sin(ang)
    reps = LANES // d
    cos_t = jnp.tile(jnp.concatenate([cos, cos], -1), (1, reps))
    sin_t = jnp.tile(jnp.concatenate([-sin, sin], -1), (1, reps))
    return cos_t, sin_t


def _col(row):
    return jnp.broadcast_to(row, (LANES, LANES)).T


def _mm_kernel(x_ref, w_ref, o_ref, xb_ref):
    @pl.when(pl.program_id(1) == 0)
    def _():
        xb_ref[...] = x_ref[...].astype(BF16)

    o_ref[...] = jnp.dot(xb_ref[...], w_ref[...], preferred_element_type=F32).astype(o_ref.dtype)


def _matmul(x, w, tm, tn, name, out_dtype=F32):
    M, K = x.shape
    N = w.shape[1]
    tm, tn = min(tm, M), min(tn, N)
    return pl.pallas_call(
        _mm_kernel,
        out_shape=jax.ShapeDtypeStruct((M, N), out_dtype),
        grid=(M // tm, N // tn),
        in_specs=[pl.BlockSpec((tm, K), lambda i, j: (i, 0)),
                  pl.BlockSpec((K, tn), lambda i, j: (0, j))],
        out_specs=pl.BlockSpec((tm, tn), lambda i, j: (i, j)),
        scratch_shapes=[pltpu.VMEM((tm, K), BF16)],
        compiler_params=_cparams(2),
        name=name,
    )(x, w)


_HEAD_NORMS = {"rms": _rmsnorm, "group": _groupnorm}
_GATE_ACTS = {"silu": _silu, "sigmoid": _sigmoid}


def _proj_ln_kernel(*refs, n_in, gated):
    a_refs, w_refs = refs[:n_in], refs[n_in:2 * n_in]
    x_ref, g_ref, b_ref = refs[2 * n_in:2 * n_in + 3]
    extra, o_ref = refs[2 * n_in + 3:-1], refs[-1]
    y = None
    for a_ref, w_ref, gate in zip(a_refs, w_refs, gated):
        a = a_ref[...]
        if gate is not None:
            norm, act = _HEAD_NORMS[gate[0]], _GATE_ACTS[gate[1]]
            gain_ref, gate_ref, extra = extra[0], extra[1], extra[2:]
            a = jnp.concatenate([norm(_head(a, h), gain_ref[...]) for h in range(N_HEADS)], axis=1)
            a = a * act(gate_ref[...])
        ya = _bdot(a, w_ref[...])
        y = ya if y is None else y + ya
    o_ref[...] = _layernorm(DN_ALPHA * x_ref[...] + y, g_ref[...], b_ref[...])


def _proj_ln(acts, weights, x, g, b, tm, name, gates=None):
    M = x.shape[0]
    tm = min(tm, M)
    n_in = len(acts)
    gates = gates or [None] * n_in
    in_specs = [pl.BlockSpec((tm, a.shape[1]), lambda i: (i, 0)) for a in acts]
    in_specs += [pl.BlockSpec(w.shape, lambda i: (0, 0)) for w in weights]
    in_specs += [pl.BlockSpec((tm, D_MODEL), lambda i: (i, 0)),
                 pl.BlockSpec((1, D_MODEL), lambda i: (0, 0)),
                 pl.BlockSpec((1, D_MODEL), lambda i: (0, 0))]
    extra = []
    for gate in gates:
        if gate is not None:
            _, gain, _, gate_arr, blk = gate
            in_specs += [pl.BlockSpec((1, HEAD_W), lambda i: (0, 0)),
                         pl.BlockSpec((tm, MIX_W), lambda i, blk=blk: (i, blk))]
            extra += [gain.reshape(1, -1), gate_arr]
    return pl.pallas_call(
        functools.partial(_proj_ln_kernel, n_in=n_in,
                          gated=tuple(None if gt is None else (gt[0], gt[2]) for gt in gates)),
        out_shape=jax.ShapeDtypeStruct((M, D_MODEL), F32),
        grid=(M // tm,),
        in_specs=in_specs,
        out_specs=pl.BlockSpec((tm, D_MODEL), lambda i: (i, 0)),
        compiler_params=_cparams(1),
        name=name,
    )(*acts, *weights, x, g.reshape(1, -1), b.reshape(1, -1), *extra)


Q_SCALE = DH_A ** -0.5 * math.log2(math.e)
VT_ROWS = HEAD_W + 16


def _ev_in_prompt_kernel(x_ref, w_ref, cos_ref, sin_ref, qt_ref, kb_ref, kt_ref, vt_ref, vr_ref, hb_ref):
    xb = x_ref[0].astype(BF16)
    cos, sin = cos_ref[...], sin_ref[...]
    proj = lambda lo, hi: jnp.dot(xb, w_ref[:, lo * MIX_W:hi * MIX_W], preferred_element_type=F32)
    hb_ref[0] = proj(3, 7)
    q = _rope(proj(0, 1), cos, sin, DH_A // 2)
    k = _rope(proj(1, 2), cos, sin, DH_A // 2)
    qt_ref[0] = (q * Q_SCALE).T.astype(BF16)
    kb_ref[0] = k.astype(BF16)
    kt_ref[0] = k.T
    v = proj(2, 3)
    tm = v.shape[0]
    vt = v.T.astype(BF16)
    ones = jnp.ones((VT_ROWS - HEAD_W, tm), BF16)
    for h in range(N_HEADS):
        vt_ref[0, h, :HEAD_W, :] = vt[h * HEAD_W:(h + 1) * HEAD_W]
        vt_ref[0, h, HEAD_W:, :] = ones
        vr_ref[0, pl.ds(h, tm, stride=N_HEADS), :] = _head(v, h)


def _ev_in_prompt(x3, w_in, cos_t, sin_t, tm):
    B, T, _ = x3.shape
    tm = min(tm, T)
    tab = pl.BlockSpec((tm, LANES), lambda b, i: (i, 0))
    tr = pl.BlockSpec((1, MIX_W, tm), lambda b, i: (b, 0, i))
    tshape = lambda dt: jax.ShapeDtypeStruct((B, MIX_W, T), dt)
    return pl.pallas_call(
        _ev_in_prompt_kernel,
        out_shape=(tshape(BF16), jax.ShapeDtypeStruct((B, T, MIX_W), BF16), tshape(F32),
                   jax.ShapeDtypeStruct((B, N_HEADS, VT_ROWS, T), BF16),
                   jax.ShapeDtypeStruct((B, T * N_HEADS, HEAD_W), F32),
                   jax.ShapeDtypeStruct((B, T, 4 * MIX_W), F32)),
        grid=(B, T // tm),
        in_specs=[pl.BlockSpec((1, tm, D_MODEL), lambda b, i: (b, i, 0)),
                  pl.BlockSpec(w_in.shape, lambda b, i: (0, 0)), tab, tab],
        out_specs=(tr, pl.BlockSpec((1, tm, MIX_W), lambda b, i: (b, i, 0)), tr,
                   pl.BlockSpec((1, N_HEADS, VT_ROWS, tm), lambda b, i: (b, 0, 0, i)),
                   pl.BlockSpec((1, tm * N_HEADS, HEAD_W), lambda b, i: (b, i, 0)),
                   pl.BlockSpec((1, tm, 4 * MIX_W), lambda b, i: (b, i, 0))),
        compiler_params=_cparams(2),
        name="ev_in_prompt",
    )(x3, w_in, cos_t, sin_t)


def _ev_prep_sample_kernel(qk_ref, cos_ref, sin_ref, q_ref, k_ref, kt_ref):
    cos, sin = cos_ref[...], sin_ref[...]
    k = _rope(qk_ref[:, MIX_W:], cos, sin, DH_A // 2)
    q_ref[...] = _rope(qk_ref[:, :MIX_W], cos, sin, DH_A // 2) * Q_SCALE
    k_ref[...] = k
    kt_ref[...] = k.T


def _ev_prep_sample(h, cos_t, sin_t):
    DB = h.shape[0]
    full = lambda shape: pl.BlockSpec(shape, lambda i: (0,) * len(shape))
    return pl.pallas_call(
        _ev_prep_sample_kernel,
        out_shape=(jax.ShapeDtypeStruct((DB, MIX_W), F32), jax.ShapeDtypeStruct((DB, MIX_W), F32),
                   jax.ShapeDtypeStruct((MIX_W, DB), F32)),
        grid=(1,),
        in_specs=[full((DB, 2 * MIX_W)), full((DB, LANES)), full((DB, LANES))],
        out_specs=(full((DB, MIX_W)), full((DB, MIX_W)), full((MIX_W, DB))),
        compiler_params=_cparams(1),
        name="ev_prep_sample",
    )(h, cos_t, sin_t)


def _lambda(lam_ref, lam_init):
    lp = lam_ref[...]
    s01 = jnp.sum(lp[0:1] * lp[1:2], axis=1, keepdims=True)
    s23 = jnp.sum(lp[2:3] * lp[3:4], axis=1, keepdims=True)
    return jnp.exp(s01) - jnp.exp(s23) + lam_init


def _dattn_kernel(lam_ref, g_ref, qt_ref, k_ref, vt_ref, o_ref, m_ref, acc_ref, s_ref, *, tq, lam_init):
    i = pl.program_id(2)
    qt = qt_ref[0]
    sub = lax.broadcasted_iota(jnp.int32, qt.shape, 0)
    zero = jnp.zeros_like(qt)
    q_maps = [jnp.where(sub < DH_A, qt, zero), jnp.where(sub >= DH_A, qt, zero)]
    maps = range(2)
    m_ref[...] = jnp.full(m_ref.shape, NEG_INF, F32)
    acc_ref[...] = jnp.zeros(acc_ref.shape, F32)

    def scores(j, slot):
        kj = k_ref[0, pl.ds(pl.multiple_of(j * tq, tq), tq), :]
        for c in maps:
            s_ref[slot, c] = jnp.dot(kj, q_maps[c], preferred_element_type=F32)

    def absorb(j, slot, masked):
        vt = vt_ref[0, 0, :, pl.ds(pl.multiple_of(j * tq, tq), tq)]
        for c in maps:
            s = s_ref[slot, c]
            if masked:
                r = lax.broadcasted_iota(jnp.int32, s.shape, 0)
                col = lax.broadcasted_iota(jnp.int32, s.shape, 1)
                s = jnp.where(r <= col, s, NEG_INF)
            m_prev = m_ref[c]
            m_new = jnp.maximum(m_prev, jnp.max(s, axis=0, keepdims=True))
            alpha = jnp.exp2(m_prev - m_new)
            p = jnp.exp2(s - m_new)
            acc_ref[c] = alpha * acc_ref[c] + jnp.dot(vt, p.astype(BF16), preferred_element_type=F32)
            m_ref[c] = m_new

    scores(0, 0)

    def body(jj, carry):
        scores(2 * jj + 1, 1)
        absorb(2 * jj, 0, False)
        scores(2 * jj + 2, 0)
        absorb(2 * jj + 1, 1, False)
        return carry

    lax.fori_loop(0, lax.shift_right_logical(i, 1), body, 0)

    @pl.when((i & 1) == 1)
    def _():
        scores(i, 1)
        absorb(i - 1, 0, False)
        absorb(i, 1, True)

    @pl.when((i & 1) == 0)
    def _():
        absorb(i, 0, True)

    lam = _lambda(lam_ref, lam_init)
    o = [acc_ref[c, :HEAD_W, :] / acc_ref[c, HEAD_W:HEAD_W + 1, :] for c in maps]
    d = (o[0] - lam * o[1]).T
    o_ref[0] = _rmsnorm(d, g_ref[...]) * (1.0 - lam_init)


def _dattn_prompt(qt, kb, vt, lam_p, subln_g, lam_init, tq):
    B, T, _ = kb.shape
    tq = min(tq, T)
    return pl.pallas_call(
        functools.partial(_dattn_kernel, tq=tq, lam_init=lam_init),
        out_shape=jax.ShapeDtypeStruct((B, T, MIX_W), F32),
        grid=(B, N_HEADS, T // tq),
        in_specs=[pl.BlockSpec((4, DH_A), lambda b, h, i: (0, 0)),
                  pl.BlockSpec((1, HEAD_W), lambda b, h, i: (0, 0)),
                  pl.BlockSpec((1, HEAD_W, tq), lambda b, h, i: (b, h, i)),
                  pl.BlockSpec((1, T, HEAD_W), lambda b, h, i: (b, 0, h)),
                  pl.BlockSpec((1, 1, VT_ROWS, T), lambda b, h, i: (b, h, 0, 0))],
        out_specs=pl.BlockSpec((1, tq, HEAD_W), lambda b, h, i: (b, i, h)),
        scratch_shapes=[pltpu.VMEM((2, 1, tq), F32), pltpu.VMEM((2, VT_ROWS, tq), F32),
                        pltpu.VMEM((2, 2, tq, tq), F32)],
        compiler_params=_cparams(3),
        name="dattn_prompt",
    )(lam_p, subln_g.reshape(1, -1), qt, kb, vt)


def _head(x, h):
    return x[:, h * HEAD_W:(h + 1) * HEAD_W]


def _gla_chunk(q_in, k_in, k_end, v, dec, st_ref, tril_mask):
    q_in, k_in, k_end, v = (x.astype(BF16) for x in (q_in, k_in, k_end, v))
    heads = range(N_HEADS)
    sts = [st_ref[h] for h in heads]
    inter = [_bdot_nt(_head(q_in, h), sts[h]) for h in heads]
    upd = [_bdot_tn(_head(v, h), _head(k_end, h)) for h in heads]
    attn = [jnp.where(tril_mask, _bdot_nt(_head(q_in, h), _head(k_in, h)), 0.0) for h in heads]
    intra = [_bdot(attn[h], _head(v, h)) for h in heads]
    for h in heads:
        st_ref[h] = sts[h] * (dec[h] if isinstance(dec, (list, tuple)) else _head(dec, h)) + upd[h]
    return [intra[h] + inter[h] for h in heads]


def _hgrn_kernel(q_ref, f_ref, i_ref, lb_ref, o_ref, s_ref, st_ref, *, tt, L):
    t = pl.program_id(1)

    @pl.when(t == 0)
    def _():
        st_ref[...] = jnp.zeros(st_ref.shape, F32)

    tril_mask = _tri(L, True)
    tril = tril_mask.astype(BF16)

    def body(c, carry):
        rows = pl.ds(pl.multiple_of(c * L, L), L)
        lb = lb_ref[...]
        f = lb + (1.0 - lb) * _sigmoid(f_ref[0, rows, :])
        b = _cumsum_rows(tril, jnp.log(f))
        b_end = b[L - 1:L, :]
        k = 1.0 - f
        q_in = _silu(q_ref[0, rows, :]) * jnp.exp(b)
        o = _gla_chunk(q_in, k * jnp.exp(-b), k * jnp.exp(b_end - b), i_ref[0, rows, :], jnp.exp(b_end),
                       st_ref, tril_mask)
        o_ref[0, rows, :] = jnp.concatenate(o, axis=1)
        return carry

    lax.fori_loop(0, tt // L, body, 0, unroll=2)

    @pl.when(t == pl.num_programs(1) - 1)
    def _():
        s_ref[0] = st_ref[...]


def _hgrn_prompt(h3, lb, tt):
    B, T, _ = h3.shape
    tt = min(tt, T)
    L = math.gcd(T, CHUNK)
    col = lambda c: pl.BlockSpec((1, tt, MIX_W), lambda b, t: (b, t, c))
    return pl.pallas_call(
        functools.partial(_hgrn_kernel, tt=tt, L=L),
        out_shape=(jax.ShapeDtypeStruct((B, T, MIX_W), F32),
                   jax.ShapeDtypeStruct((B, N_HEADS, HEAD_W, HEAD_W), F32)),
        grid=(B, T // tt),
        in_specs=[col(0), col(1), col(2), pl.BlockSpec((1, MIX_W), lambda b, t: (0, 0))],
        out_specs=(pl.BlockSpec((1, tt, MIX_W), lambda b, t: (b, t, 0)),
                   pl.BlockSpec((1, N_HEADS, HEAD_W, HEAD_W), lambda b, t: (b, 0, 0, 0))),
        scratch_shapes=[pltpu.VMEM((N_HEADS, HEAD_W, HEAD_W), F32)],
        compiler_params=_cparams(2),
        name="hgrn_prompt",
    )(h3, h3, h3, lb.reshape(1, -1))


def _log_gamma(h):
    return float(np.log(1.0 - 2.0 ** (-5.0 - h)))


def _od_kernel(cq_ref, ck_ref, cv_ref, dq_ref, dk_ref, dv_ref, gc_ref, gr_ref,
               bc_ref, br_ref, cos_ref, sin_ref,
               oc_ref, od_ref, sc_ref, dc_ref, dn_ref, dm_ref,
               st_ref, ct_ref, n_ref, m_ref, *, tt, L):
    t = pl.program_id(1)

    @pl.when(t == 0)
    def _():
        st_ref[...] = jnp.zeros(st_ref.shape, F32)
        ct_ref[...] = jnp.zeros(ct_ref.shape, F32)
        n_ref[...] = jnp.zeros(n_ref.shape, F32)
        m_ref[...] = jnp.zeros(m_ref.shape, F32)

    tril_mask = _tri(L, True)
    tril = tril_mask.astype(BF16)
    triu = _tri(L, False).astype(BF16)
    scale = HEAD_W ** -0.5
    heads = range(N_HEADS)
    pos1 = (lax.broadcasted_iota(jnp.int32, (L, HEAD_W), 0) + 1).astype(F32)
    ret_b = jnp.concatenate([pos1 * _log_gamma(h) for h in heads], axis=1)
    ret_b_end = jnp.concatenate([jnp.full((L, HEAD_W), L * _log_gamma(h), F32) for h in heads], axis=1)
    ret_q_dec, ret_k_dec, ret_k_end = jnp.exp(ret_b), jnp.exp(-ret_b), jnp.exp(ret_b_end - ret_b)
    ret_dec = [math.exp(L * _log_gamma(h)) for h in heads]

    def body(c, carry):
        rows = pl.ds(pl.multiple_of(c * L, L), L)
        cos, sin = cos_ref[rows, :], sin_ref[rows, :]
        q = _rope(cq_ref[0, rows, :], cos, sin, HEAD_W // 2)
        k = _rope(ck_ref[0, rows, :], cos, sin, HEAD_W // 2) * scale
        o = _gla_chunk(q * ret_q_dec, k * ret_k_dec, k * ret_k_end, cv_ref[0, rows, :], ret_dec, st_ref, tril_mask)
        oc_ref[0, rows, :] = jnp.concatenate(o, axis=1)
        g_c = gc_ref[0, rows, :] + bc_ref[...]
        g_r = gr_ref[0, c] + br_ref[...]
        lf_c = _log_sigmoid(g_c)
        b_c = _cumsum_rows(tril, jnp.concatenate(
            [jnp.broadcast_to(lf_c[:, N_HEADS + h:N_HEADS + h + 1], (L, HEAD_W)) for h in heads], axis=1))
        b_r = _cumsum_lanes(_log_sigmoid(g_r), triu)
        dq = (dq_ref[0, rows, :] * scale)
        dk = dk_ref[0, rows, :]
        dqb, dkb, dvb = dq.astype(BF16), dk.astype(BF16), dv_ref[0, rows, :].astype(BF16)
        cts = [ct_ref[h] for h in heads]
        q_c = [_bdot_nt(_head(dqb, h), cts[h]) for h in heads]
        q_k = [_bdot_nt(_head(dqb, h), _head(dkb, h)) for h in heads]
        m_prev = [m_ref[h:h + 1, 0:1] for h in heads]
        bc1, m_t, w_mat = [], [], []
        for h in heads:
            bc = _head(b_c, h)[:, :L]
            dm = jnp.where(tril_mask, bc - b_r[N_HEADS + h:N_HEADS + h + 1, :] + g_r[h:h + 1, :], NEG_INF)
            bc1.append(bc[:, 0:1])
            m_t.append(jnp.maximum(bc1[h] + m_prev[h], jnp.max(dm, axis=1, keepdims=True)))
            w_mat.append(jnp.exp(dm - m_t[h]) * q_k[h])
        w_v = [_bdot(w_mat[h], _head(dvb, h)) for h in heads]
        kws, c_scales = [], []
        for h in heads:
            m_new = m_t[h][L - 1:L, :]
            b_last = bc1[h][L - 1:L, :]
            c_scales.append(jnp.exp(b_last + m_prev[h] - m_new))
            kws.append(_head(dk, h) * jnp.exp(b_last - bc1[h] + g_c[:, h:h + 1] - m_new))
            m_ref[h:h + 1, :] = jnp.broadcast_to(m_new, (1, LANES))
        c_upd = [_bdot_tn(_head(dvb, h), kws[h]) for h in heads]
        hs = []
        for h in heads:
            n_row = n_ref[h:h + 1, :]
            inter = jnp.exp(bc1[h] + m_prev[h] - m_t[h])
            num = inter * q_c[h] + w_v[h]
            den = (inter * jnp.sum(_head(dq, h) * n_row, axis=1, keepdims=True)
                   + jnp.sum(w_mat[h], axis=1, keepdims=True))
            hs.append(num / jnp.maximum(jnp.abs(den), jnp.exp(-m_t[h])))
            ct_ref[h] = c_scales[h] * cts[h] + c_upd[h]
            n_ref[h:h + 1, :] = c_scales[h] * n_row + jnp.sum(kws[h], axis=0, keepdims=True)
        od_ref[0, rows, :] = jnp.concatenate(hs, axis=1)
        return carry

    lax.fori_loop(0, tt // L, body, 0)

    @pl.when(t == pl.num_programs(1) - 1)
    def _():
        sc_ref[0] = st_ref[...]
        dc_ref[0] = ct_ref[...]
        dn_ref[0] = n_ref[...]
        dm_ref[0] = m_ref[...]


def _od_in_kernel(x_ref, w_ref, wg_ref, h_ref, g_ref, xb_ref):
    @pl.when(pl.program_id(1) == 0)
    def _():
        xb_ref[...] = x_ref[...].astype(BF16)

    xb = xb_ref[...]
    h_ref[...] = jnp.dot(xb, w_ref[...], preferred_element_type=F32)

    @pl.when(pl.program_id(1) == 0)
    def _():
        g_ref[...] = jnp.dot(xb, wg_ref[...], preferred_element_type=F32)


def _od_in_prompt(x, w_in, w_gate, tm, tn):
    M, K = x.shape
    N = w_in.shape[1]
    tm = min(tm, M)
    return pl.pallas_call(
        _od_in_kernel,
        out_shape=(jax.ShapeDtypeStruct((M, N), F32), jax.ShapeDtypeStruct((M, LANES), F32)),
        grid=(M // tm, N // tn),
        in_specs=[pl.BlockSpec((tm, K), lambda i, j: (i, 0)),
                  pl.BlockSpec((K, tn), lambda i, j: (0, j)),
                  pl.BlockSpec((K, LANES), lambda i, j: (0, 0))],
        out_specs=(pl.BlockSpec((tm, tn), lambda i, j: (i, j)), pl.BlockSpec((tm, LANES), lambda i, j: (i, 0))),
        scratch_shapes=[pltpu.VMEM((tm, K), BF16)],
        compiler_params=_cparams(2),
        name="od_in_prompt",
    )(x, w_in, w_gate)


def _od_prompt(h3, gates, b_if, cos_t, sin_t, tt):
    B, T, _ = h3.shape
    tt = min(tt, T)
    L = math.gcd(T, OD_CHUNK)
    gates_r = gates[:, :, :2 * N_HEADS].reshape(B, T // L, L, 2 * N_HEADS).transpose(0, 1, 3, 2)
    b_pad = jnp.zeros((1, LANES), F32).at[0, :2 * N_HEADS].set(b_if)
    col = lambda c: pl.BlockSpec((1, tt, MIX_W), lambda b, t: (b, t, c))
    tab = pl.BlockSpec((tt, LANES), lambda b, t: (t, 0))
    mat_state = pl.BlockSpec((1, N_HEADS, HEAD_W, HEAD_W), lambda b, t: (b, 0, 0, 0))
    row_state = pl.BlockSpec((1, SUBLANES, LANES), lambda b, t: (b, 0, 0))
    out_blk = pl.BlockSpec((1, tt, MIX_W), lambda b, t: (b, t, 0))
    return pl.pallas_call(
        functools.partial(_od_kernel, tt=tt, L=L),
        out_shape=(jax.ShapeDtypeStruct((B, T, MIX_W), F32), jax.ShapeDtypeStruct((B, T, MIX_W), F32),
                   jax.ShapeDtypeStruct((B, N_HEADS, HEAD_W, HEAD_W), F32),
                   jax.ShapeDtypeStruct((B, N_HEADS, HEAD_W, HEAD_W), F32),
                   jax.ShapeDtypeStruct((B, SUBLANES, LANES), F32),
                   jax.ShapeDtypeStruct((B, SUBLANES, LANES), F32)),
        grid=(B, T // tt),
        in_specs=[col(0), col(1), col(2), col(4), col(5), col(6),
                  pl.BlockSpec((1, tt, LANES), lambda b, t: (b, t, 0)),
                  pl.BlockSpec((1, tt // L, 2 * N_HEADS, L), lambda b, t: (b, t, 0, 0)),
                  pl.BlockSpec((1, LANES), lambda b, t: (0, 0)),
                  pl.BlockSpec((2 * N_HEADS, 1), lambda b, t: (0, 0)),
                  tab, tab],
        out_specs=(out_blk, out_blk, mat_state, mat_state, row_state, row_state),
        scratch_shapes=[pltpu.VMEM((N_HEADS, HEAD_W, HEAD_W), F32), pltpu.VMEM((N_HEADS, HEAD_W, HEAD_W), F32),
                        pltpu.VMEM((SUBLANES, LANES), F32), pltpu.VMEM((SUBLANES, LANES), F32)],
        compiler_params=_cparams(2),
        name="od_prompt",
    )(h3, h3, h3, h3, h3, h3, gates, gates_r, b_pad, b_if.reshape(-1, 1), cos_t, sin_t)


def _xattn_kernel(x_ref, wq_ref, mk_ref, mv_ref, wo_ref, g_ref, b_ref, o_ref):
    x = x_ref[...]
    q = _bdot(x, wq_ref[...])
    qb = (q * (DH_X ** -0.5)).astype(BF16)
    heads = range(N_HEADS)
    cols = [slice(h * DH_X, (h + 1) * DH_X) for h in heads]
    s = [lax.dot_general(qb[:, cols[h]], mk_ref[0, :, cols[h]], NT_DIMS, preferred_element_type=F32) for h in heads]
    p = [jnp.exp(s[h] - jnp.max(s[h], axis=1, keepdims=True)) for h in heads]
    l = [jnp.sum(p[h], axis=1, keepdims=True) for h in heads]
    pv = [jnp.dot(p[h].astype(BF16), mv_ref[0, :, cols[h]], preferred_element_type=F32) for h in heads]
    outs = [(pv[h] / l[h]).astype(BF16) for h in heads]
    y = jnp.dot(jnp.concatenate(outs, axis=1), wo_ref[...], preferred_element_type=F32)
    o_ref[...] = _layernorm(DN_ALPHA * x + y, g_ref[...], b_ref[...])


def _xattn_prompt(x, wq, mk, mv, wo, g, b, T, tm):
    M = x.shape[0]
    tm = min(tm, T)
    n_mem = mk.shape[1]
    per_b = T // tm
    full = lambda shape: pl.BlockSpec(shape, lambda i: (0,) * len(shape))
    mem = pl.BlockSpec((1, n_mem, D_MODEL), lambda i: (i // per_b, 0, 0))
    return pl.pallas_call(
        _xattn_kernel,
        out_shape=jax.ShapeDtypeStruct((M, D_MODEL), F32),
        grid=(M // tm,),
        in_specs=[pl.BlockSpec((tm, D_MODEL), lambda i: (i, 0)), full((D_MODEL, D_MODEL)), mem, mem,
                  full((D_MODEL, D_MODEL)), full((1, D_MODEL)), full((1, D_MODEL))],
        out_specs=pl.BlockSpec((tm, D_MODEL), lambda i: (i, 0)),
        compiler_params=_cparams(1),
        name="xattn_prompt",
    )(x, wq, mk, mv, wo, g.reshape(1, -1), b.reshape(1, -1))


def _ffn_kernel(x_ref, wu_ref, wg_ref, cw_ref, cb_ref, wd_ref, g_ref, b_ref, o_ref, tail_ref,
                xb_ref, acc_ref, stage_ref, carry_ref, *, tm, per_b, single):
    i, j = pl.program_id(0), pl.program_id(1)

    @pl.when(j == 0)
    def _():
        xb_ref[...] = x_ref[...].astype(BF16)

    @pl.when(i % per_b == 0)
    def _():
        carry_ref[j] = jnp.zeros(carry_ref.shape[1:], F32)

    xb = xb_ref[...]
    stage_ref[0:SUBLANES, :] = carry_ref[j]
    stage_ref[SUBLANES:, :] = jnp.dot(xb, wu_ref[...], preferred_element_type=F32)
    last = stage_ref[tm:tm + SUBLANES, :]
    carry_ref[j] = last
    tail_ref[0] = last
    conv = (cb_ref[...] + cw_ref[0:1, :] * stage_ref[SUBLANES - 2:SUBLANES - 2 + tm, :]
            + cw_ref[1:2, :] * stage_ref[SUBLANES - 1:SUBLANES - 1 + tm, :]
            + cw_ref[2:3, :] * stage_ref[SUBLANES:, :])
    gate = jnp.dot(xb, wg_ref[...], preferred_element_type=F32)
    y = _bdot(jax.nn.gelu(conv) * gate, wd_ref[...])
    if single:
        o_ref[...] = _layernorm(DN_ALPHA * x_ref[...] + y, g_ref[...], b_ref[...])
        return

    @pl.when(j == 0)
    def _():
        acc_ref[...] = y

    @pl.when(j > 0)
    def _():
        acc_ref[...] = acc_ref[...] + y

    @pl.when(j == pl.num_programs(1) - 1)
    def _():
        o_ref[...] = _layernorm(DN_ALPHA * x_ref[...] + acc_ref[...], g_ref[...], b_ref[...])


def _ffn_prompt(x, w_up, conv_w, conv_b, w_down, g, b, T, tm, tf):
    M = x.shape[0]
    d_ff = w_down.shape[0]
    tm = min(tm, T)
    nf = d_ff // tf
    per_b = T // tm
    wmode = dict(pipeline_mode=pl.Buffered(1)) if nf == 1 else {}
    return pl.pallas_call(
        functools.partial(_ffn_kernel, tm=tm, per_b=per_b, single=nf == 1),
        out_shape=(jax.ShapeDtypeStruct((M, D_MODEL), F32),
                   jax.ShapeDtypeStruct((M // tm, SUBLANES, d_ff), F32)),
        grid=(M // tm, nf),
        in_specs=[pl.BlockSpec((tm, D_MODEL), lambda i, j: (i, 0)),
                  pl.BlockSpec((D_MODEL, tf), lambda i, j: (0, j), **wmode),
                  pl.BlockSpec((D_MODEL, tf), lambda i, j: (0, nf + j), **wmode),
                  pl.BlockSpec((3, tf), lambda i, j: (0, j)),
                  pl.BlockSpec((1, tf), lambda i, j: (0, j)),
                  pl.BlockSpec((tf, D_MODEL), lambda i, j: (j, 0), **wmode),
                  pl.BlockSpec((1, D_MODEL), lambda i, j: (0, 0)),
                  pl.BlockSpec((1, D_MODEL), lambda i, j: (0, 0))],
        out_specs=(pl.BlockSpec((tm, D_MODEL), lambda i, j: (i, 0)),
                   pl.BlockSpec((1, SUBLANES, tf), lambda i, j: (i, 0, j))),
        scratch_shapes=[pltpu.VMEM((tm, D_MODEL), BF16), pltpu.VMEM((tm, D_MODEL), F32),
                        pltpu.VMEM((tm + SUBLANES, tf), F32), pltpu.VMEM((nf, SUBLANES, tf), F32)],
        compiler_params=_cparams(2),
        name="ffn_prompt",
    )(x, w_up, w_up, conv_w, conv_b.reshape(1, -1), w_down, g.reshape(1, -1), b.reshape(1, -1))


def _dattn_sample_kernel(*refs, n_pages, page, lam_init):
    pt_ref = refs[0]
    lam_ref, g_ref, q_ref, kn_ref, vn_ref = refs[1:6]
    k_refs = refs[6:6 + n_pages]
    v_refs = refs[6 + n_pages:6 + 2 * n_pages]
    o_ref = refs[6 + 2 * n_pages]
    del pt_ref
    n_rows = 2 * N_HEADS
    q = q_ref[0]
    lane = lax.broadcasted_iota(jnp.int32, (n_rows, MIX_W), 1)
    row = lax.broadcasted_iota(jnp.int32, (n_rows, MIX_W), 0)
    qbd = jnp.where(lax.shift_right_logical(lane, 6) == row, jnp.broadcast_to(q, (n_rows, MIX_W)), 0.0)
    s_new = jnp.sum(qbd * kn_ref[0], axis=1, keepdims=True)
    scores = [_bdot(qbd, k_refs[p][0]) for p in range(n_pages)]
    m = s_new
    for s in scores:
        m = jnp.maximum(m, jnp.max(s, axis=1, keepdims=True))
    p_new = jnp.exp2(s_new - m)
    l = p_new
    vn = vn_ref[0]
    accs = [p_new * vn[:, h * HEAD_W:(h + 1) * HEAD_W] for h in range(N_HEADS)]
    for p in range(n_pages):
        pr = jnp.exp2(scores[p] - m)
        l = l + jnp.sum(pr, axis=1, keepdims=True)
        for h in range(N_HEADS):
            accs[h] = accs[h] + _bdot(pr, v_refs[p][0, pl.ds(h, page, stride=N_HEADS), :])
    lam = _lambda(lam_ref, lam_init)
    outs = []
    for h in range(N_HEADS):
        o = accs[h] / l
        d = o[2 * h:2 * h + 1, :] - lam * o[2 * h + 1:2 * h + 2, :]
        outs.append(_rmsnorm(d, g_ref[...]) * (1.0 - lam_init))
    o_ref[0] = jnp.concatenate(outs, axis=1)


def _dattn_sample(q, k_new, v_new, kt_pool, v_pool, pool_base, page_table, lam_p, subln_g, lam_init):
    DB = q.shape[0]
    n_pages = page_table.shape[1]
    page = kt_pool.shape[2]
    tok = pl.BlockSpec((1, 1, MIX_W), lambda b, pt: (b, 0, 0))

    def k_spec(p):
        return pl.BlockSpec((1, MIX_W, page), lambda b, pt: (pool_base + pt[b, p], 0, 0))

    def v_spec(p):
        return pl.BlockSpec((1, page * N_HEADS, HEAD_W), lambda b, pt: (pool_base + pt[b, p], 0, 0))

    grid_spec = pltpu.PrefetchScalarGridSpec(
        num_scalar_prefetch=1,
        grid=(DB,),
        in_specs=[pl.BlockSpec((4, DH_A), lambda b, pt: (0, 0)),
                  pl.BlockSpec((1, HEAD_W), lambda b, pt: (0, 0)), tok, tok, tok]
                 + [k_spec(p) for p in range(n_pages)] + [v_spec(p) for p in range(n_pages)],
        out_specs=tok,
    )
    out = pl.pallas_call(
        functools.partial(_dattn_sample_kernel, n_pages=n_pages, page=page, lam_init=lam_init),
        out_shape=jax.ShapeDtypeStruct((DB, 1, MIX_W), F32),
        grid_spec=grid_spec,
        compiler_params=_cparams(1),
        name="dattn_sample",
    )(page_table, lam_p, subln_g.reshape(1, -1), q.reshape(DB, 1, MIX_W), k_new.reshape(DB, 1, MIX_W),
      v_new.reshape(DB, 1, MIX_W), *([kt_pool] * n_pages), *([v_pool] * n_pages))
    return out.reshape(DB, MIX_W)


def _hgrn_step_kernel(h_ref, s0_ref, lb_ref, ng_ref, o_ref, s_ref, *, bt):
    def body(i, carry):
        row = h_ref[i]
        heads = range(N_HEADS)
        blk = lambda n: row[:, n * MIX_W:(n + 1) * MIX_W]
        lb = lb_ref[...]
        f = lb + (1.0 - lb) * _sigmoid(blk(4))
        q, v = _silu(blk(3)), blk(5)
        f_col = [_col(_head(f, h)) for h in heads]
        s_new = [f_col[h] * s0_ref[i, h] + (1.0 - f_col[h]) * _head(v, h) for h in heads]
        for h in heads:
            s_ref[i, h] = s_new[h]
        o = [_bdot(jnp.broadcast_to(_head(q, h), (SUBLANES, HEAD_W)), s_new[h])[0:1, :] for h in heads]
        o = jnp.concatenate([_rmsnorm(o[h], ng_ref[...]) for h in heads], axis=1)
        o_ref[i] = o * _silu(blk(6))
        return carry

    lax.fori_loop(0, bt, body, 0, unroll=2)


def _hgrn_step(h, s0, lb, norm_g, bt):
    DB, W = h.shape
    bt = min(bt, DB)
    state = pl.BlockSpec((bt, N_HEADS, HEAD_W, HEAD_W), lambda i: (i, 0, 0, 0))
    o, s = pl.pallas_call(
        functools.partial(_hgrn_step_kernel, bt=bt),
        out_shape=(jax.ShapeDtypeStruct((DB, 1, MIX_W), F32), jax.ShapeDtypeStruct(s0.shape, F32)),
        grid=(DB // bt,),
        in_specs=[pl.BlockSpec((bt, 1, W), lambda i: (i, 0, 0)), state,
                  pl.BlockSpec((1, MIX_W), lambda i: (0, 0)), pl.BlockSpec((1, HEAD_W), lambda i: (0, 0))],
        out_specs=(pl.BlockSpec((bt, 1, MIX_W), lambda i: (i, 0, 0)), state),
        compiler_params=_cparams(1),
        name="hgrn_step",
    )(h.reshape(DB, 1, W), s0, lb.reshape(1, -1), norm_g.reshape(1, -1))
    return o.reshape(DB, MIX_W), s


def _od_step_kernel(h_ref, gt_ref, bif_ref, cos_ref, sin_ref, sc0_ref, dc0_ref, dn0_ref, dm0_ref, cng_ref, dng_ref,
                    oc_ref, od_ref, sc_ref, dc_ref, dn_ref, dm_ref, *, bt):
    scale = HEAD_W ** -0.5

    def body(i, carry):
        row = h_ref[i]
        cos, sin = cos_ref[...], sin_ref[...]
        gates = gt_ref[i] + bif_ref[...]
        m_row = dm0_ref[i]
        heads = range(N_HEADS)
        blk = lambda n: row[:, n * MIX_W:(n + 1) * MIX_W]
        q = _rope(blk(0), cos, sin, HEAD_W // 2)
        k = _rope(blk(1), cos, sin, HEAD_W // 2) * scale
        cv, dk, dv, dq = blk(2), blk(5), blk(6), blk(4) * scale
        lf = _log_sigmoid(gates)
        m_new = [jnp.maximum(lf[:, N_HEADS + h:N_HEADS + h + 1] + m_row[:, h:h + 1], gates[:, h:h + 1])
                 for h in heads]
        c_scale = [jnp.exp(lf[:, N_HEADS + h:N_HEADS + h + 1] + m_row[:, h:h + 1] - m_new[h]) for h in heads]
        kw = [_head(dk, h) * jnp.exp(gates[:, h:h + 1] - m_new[h]) for h in heads]
        k_col = [_col(_head(k, h)) for h in heads]
        kw_col = [_col(kw[h]) for h in heads]
        s_new = [math.exp(_log_gamma(h)) * sc0_ref[i, h] + k_col[h] * _head(cv, h) for h in heads]
        c_new = [c_scale[h] * dc0_ref[i, h] + kw_col[h] * _head(dv, h) for h in heads]
        n_new = [c_scale[h] * dn0_ref[i, h:h + 1, :] + kw[h] for h in heads]
        for h in heads:
            sc_ref[i, h] = s_new[h]
            dc_ref[i, h] = c_new[h]
            dn_ref[i, h:h + 1, :] = n_new[h]
        rows8 = lambda x: jnp.broadcast_to(x, (SUBLANES, HEAD_W))
        o = [_bdot(rows8(_head(q, h)), s_new[h])[0:1, :] for h in heads]
        num = [_bdot(rows8(_head(dq, h)), c_new[h])[0:1, :] for h in heads]
        den = [jnp.sum(_head(dq, h) * n_new[h], axis=1, keepdims=True) for h in heads]
        hh = [num[h] / jnp.maximum(jnp.abs(den[h]), jnp.exp(-m_new[h])) for h in heads]
        oc = jnp.concatenate([_groupnorm(o[h], cng_ref[...]) for h in heads], axis=1)
        od = jnp.concatenate([_groupnorm(hh[h], dng_ref[...]) for h in heads], axis=1)
        oc_ref[i] = oc * _silu(blk(3))
        od_ref[i] = od * _sigmoid(blk(7))
        dm_ref[i] = jnp.concatenate(m_new, axis=1)
        return carry

    lax.fori_loop(0, bt, body, 0)


def _od_step(h, gates, b_if, cos_t, sin_t, sc0, dc0, dn0, dm0, c_norm_g, d_norm_g, bt):
    DB, W = h.shape
    bt = min(bt, DB)
    mat = pl.BlockSpec((bt, N_HEADS, HEAD_W, HEAD_W), lambda i: (i, 0, 0, 0))
    nblk = pl.BlockSpec((bt, N_HEADS, HEAD_W), lambda i: (i, 0, 0))
    mblk = pl.BlockSpec((bt, 1, N_HEADS), lambda i: (i, 0, 0))
    vec = pl.BlockSpec((1, LANES), lambda i: (0, 0))
    out = pl.BlockSpec((bt, 1, MIX_W), lambda i: (i, 0, 0))
    b_pad = jnp.zeros((1, LANES), F32).at[0, :2 * N_HEADS].set(b_if)
    oc, od, sc, dc, dn, dm = pl.pallas_call(
        functools.partial(_od_step_kernel, bt=bt),
        out_shape=(jax.ShapeDtypeStruct((DB, 1, MIX_W), F32), jax.ShapeDtypeStruct((DB, 1, MIX_W), F32),
                   jax.ShapeDtypeStruct(sc0.shape, F32), jax.ShapeDtypeStruct(dc0.shape, F32),
                   jax.ShapeDtypeStruct(dn0.shape, F32), jax.ShapeDtypeStruct((DB, 1, N_HEADS), F32)),
        grid=(DB // bt,),
        in_specs=[pl.BlockSpec((bt, 1, W), lambda i: (i, 0, 0)),
                  pl.BlockSpec((bt, 1, LANES), lambda i: (i, 0, 0)), vec, vec, vec,
                  mat, mat, nblk, mblk, vec, vec],
        out_specs=(out, out, mat, mat, nblk, mblk),
        compiler_params=_cparams(1),
        name="od_step",
    )(h.reshape(DB, 1, W), gates.reshape(DB, 1, LANES), b_pad, cos_t, sin_t, sc0, dc0, dn0,
      dm0.reshape(DB, 1, N_HEADS), c_norm_g.reshape(1, -1), d_norm_g.reshape(1, -1))
    return oc.reshape(DB, MIX_W), od.reshape(DB, MIX_W), sc, dc, dn, dm.reshape(DB, N_HEADS)


X_HALVES = DH_X // LANES
X_ROWS = N_HEADS * X_HALVES


def _xattn_sample_kernel(q_ref, mk_ref, mv_ref, o_ref, *, xb):
    for t in range(xb):
        _xattn_sample_row(q_ref.at[t], mk_ref.at[t], mv_ref.at[t], o_ref.at[t])


def _xattn_sample_row(q_ref, mk_ref, mv_ref, o_ref):
    q = q_ref[...] * (DH_X ** -0.5)
    n_cols = mk_ref.shape[0]
    q2 = jnp.concatenate([q[:, h * DH_X + c * LANES:h * DH_X + (c + 1) * LANES]
                          for c in range(X_HALVES) for h in range(N_HEADS)], axis=0)
    s2 = _bdot_nt(q2, mk_ref[...])
    row = lax.broadcasted_iota(jnp.int32, (X_ROWS, n_cols), 0)
    col = lax.broadcasted_iota(jnp.int32, (X_ROWS, n_cols), 1)
    own = (col & (X_ROWS - 1)) == row
    s2 = jnp.where(own, s2, 0.0)
    s4 = s2[:N_HEADS] + pltpu.roll(s2, n_cols - N_HEADS, 1)[N_HEADS:]
    own4 = own[:N_HEADS]
    s4 = jnp.where(own4, s4, NEG_INF)
    p4 = jnp.exp(s4 - jnp.max(s4, axis=1, keepdims=True))
    l4 = jnp.sum(p4, axis=1, keepdims=True)
    p4 = p4 / l4
    p2 = jnp.concatenate([p4, pltpu.roll(p4, N_HEADS, 1)], axis=0)
    o2 = _bdot(p2, mv_ref[...])
    o_ref[...] = jnp.concatenate([o2[c * N_HEADS + h:c * N_HEADS + h + 1, :]
                                  for h in range(N_HEADS) for c in range(X_HALVES)], axis=1)


def _xattn_sample(q, mk, mv, base):
    DB = q.shape[0]
    n_rows = mk.shape[1]
    xb = math.gcd(DB, 4)
    tok = pl.BlockSpec((xb, 1, D_MODEL), lambda b: (b, 0, 0))
    mem = pl.BlockSpec((xb, n_rows, LANES), lambda b: (base // xb + b, 0, 0))
    assert base % xb == 0
    out = pl.pallas_call(
        functools.partial(_xattn_sample_kernel, xb=xb),
        out_shape=jax.ShapeDtypeStruct((DB, 1, D_MODEL), F32),
        grid=(DB // xb,),
        in_specs=[tok, mem, mem],
        out_specs=tok,
        compiler_params=_cparams(1),
        name="xattn_sample",
    )(q.reshape(DB, 1, D_MODEL), mk, mv)
    return out.reshape(DB, D_MODEL)


def _mem_rows(cache):
    n_l, DB, n_mem = cache.shape[:3]
    c = cache.reshape(n_l, DB, n_mem, N_HEADS, X_HALVES, LANES).transpose(0, 1, 2, 4, 3, 5)
    return c.reshape(n_l * DB, n_mem * X_ROWS, LANES)


def _ffn_sample_kernel(x_ref, up_ref, gate_ref, b0_ref, b1_ref, cw_ref, cb_ref, wd_ref, g_ref, b_ref, o_ref, acc_ref):
    j = pl.program_id(0)
    conv = (cb_ref[...] + cw_ref[0:1, :] * b0_ref[...] + cw_ref[1:2, :] * b1_ref[...]
            + cw_ref[2:3, :] * up_ref[...])
    y = _bdot(jax.nn.gelu(conv) * gate_ref[...], wd_ref[...])

    @pl.when(j == 0)
    def _():
        acc_ref[...] = y

    @pl.when(j > 0)
    def _():
        acc_ref[...] = acc_ref[...] + y

    @pl.when(j == pl.num_programs(0) - 1)
    def _():
        o_ref[...] = _layernorm(DN_ALPHA * x_ref[...] + acc_ref[...], g_ref[...], b_ref[...])


def _ffn_sample(x, ug, buf0, buf1, conv_w, conv_b, w_down, g, b, tf):
    DB = x.shape[0]
    d_ff = w_down.shape[0]
    nf = d_ff // tf
    ff = lambda off: pl.BlockSpec((DB, tf), lambda j: (0, off + j))
    full = lambda shape: pl.BlockSpec(shape, lambda j: (0,) * len(shape))
    return pl.pallas_call(
        _ffn_sample_kernel,
        out_shape=jax.ShapeDtypeStruct((DB, D_MODEL), F32),
        grid=(nf,),
        in_specs=[full((DB, D_MODEL)), ff(0), ff(nf), ff(0), ff(0),
                  pl.BlockSpec((3, tf), lambda j: (0, j)), pl.BlockSpec((1, tf), lambda j: (0, j)),
                  pl.BlockSpec((tf, D_MODEL), lambda j: (j, 0)), full((1, D_MODEL)), full((1, D_MODEL))],
        out_specs=full((DB, D_MODEL)),
        scratch_shapes=[pltpu.VMEM((DB, D_MODEL), F32)],
        compiler_params=_cparams(1),
        name="ffn_sample",
    )(x, ug, ug, buf0, buf1, conv_w, conv_b.reshape(1, -1), w_down, g.reshape(1, -1), b.reshape(1, -1))


def kernel(x_prompt, x_sample, cache_a_k, cache_a_v, state_b, state_c, state_d_c, state_d_n, state_d_m,
           cache_mem_k, cache_mem_v, state_conv, page_table, mem_prompt,
           ev_w_in, ev_w_out, ev_lam, ev_subln_g, ev_lb_logits, ev_b_norm_g,
           od_w_in, od_b_if, od_w_out, od_c_norm_g, od_d_norm_g,
           ln_g, ln_b, xa_wq, xa_wkv, xa_wo, ffn_w_up, ffn_conv_w, ffn_conv_b, ffn_w_down):
    B, T, _ = x_prompt.shape
    DB, t_s, _ = x_sample.shape
    assert t_s == 1, "the sample group is a single decoding step"
    assert T % CHUNK == 0
    n_pool, page = cache_a_k.shape[1], cache_a_k.shape[2]
    n_pages = page_table.shape[1]
    past = n_pages * page
    n_mem = mem_prompt.shape[1]
    d_ff = ffn_w_down.shape[1]
    tf = d_ff // 2
    N = B * T
    TM = 512

    pos_p = jnp.arange(T)
    pos_s = jnp.full((DB,), past, jnp.int32)
    lb_table = jnp.cumsum(jax.nn.softmax(ev_lb_logits.astype(F32), axis=0), axis=0)

    xp = x_prompt.reshape(N, D_MODEL)
    xs = x_sample.reshape(DB, D_MODEL)
    outs = {k: [] for k in ("ak_p", "av_p", "ak_s", "av_s", "sb_p", "sb_s", "sc_p", "sc_s", "dc_p", "dc_s",
                            "dn_p", "dn_s", "dm_p", "dm_s", "mk_p", "mv_p", "cv_p", "cv_s")}
    swap = lambda s: jnp.swapaxes(s, -1, -2)
    mem_k_rows, mem_v_rows = _mem_rows(cache_mem_k), _mem_rows(cache_mem_v)

    for l in range(DEPTH):
        j = l // 2
        if l % 2 == 0:
            lam_init = 0.8 - 0.6 * math.exp(-0.3 * l)
            w_in = ev_w_in[j].astype(BF16)
            w_out = ev_w_out[j].astype(BF16)
            cos_t, sin_t = _rope_tables(pos_p, DH_A)
            qt, kb, kt32, vt, v_rows, h_b = _ev_in_prompt(xp.reshape(B, T, D_MODEL), w_in, cos_t, sin_t, TM)
            o_a = _dattn_prompt(qt, kb, vt, ev_lam[j], ev_subln_g[j], lam_init, 512)
            o_b, st = _hgrn_prompt(h_b, lb_table[j], 512)
            outs["ak_p"].append(kt32.reshape(B, N_HEADS, 2, DH_A, T).transpose(0, 4, 1, 2, 3))
            outs["av_p"].append(v_rows.reshape(B, T, N_HEADS, HEAD_W))
            outs["sb_p"].append(swap(st))
            xp = _proj_ln([o_a.reshape(N, MIX_W), o_b.reshape(N, MIX_W)], [w_out[:MIX_W], w_out[MIX_W:]],
                          xp, ln_g[l, 0], ln_b[l, 0], TM, "ev_out_prompt",
                          gates=[None, ("rms", ev_b_norm_g[j], "silu", h_b.reshape(N, -1), 3)])
            hs = _matmul(xs, w_in, DB, 512, "ev_in_sample")
            cos_s, sin_s = _rope_tables(pos_s, DH_A)
            qs, ks32, kts32 = _ev_prep_sample(hs, cos_s, sin_s)
            vs32 = hs[:, 2 * MIX_W:3 * MIX_W]
            kt_pool = cache_a_k.transpose(0, 1, 3, 4, 5, 2).reshape(-1, MIX_W, page)
            v_pool = cache_a_v.reshape(-1, page * N_HEADS, HEAD_W)
            oa_s = _dattn_sample(qs, ks32, vs32, kt_pool, v_pool, j * n_pool, page_table,
                                 ev_lam[j], ev_subln_g[j], lam_init)
            ob_s, sb_s = _hgrn_step(hs, state_b[j], lb_table[j], ev_b_norm_g[j], 8)
            outs["ak_s"].append(kts32.reshape(N_HEADS, 2, DH_A, DB).transpose(3, 0, 1, 2)[:, None])
            outs["av_s"].append(vs32.reshape(DB, 1, N_HEADS, HEAD_W))
            outs["sb_s"].append(sb_s)
            xs = _proj_ln([oa_s, ob_s], [w_out[:MIX_W], w_out[MIX_W:]], xs, ln_g[l, 0], ln_b[l, 0], DB,
                          "ev_out_sample")
        else:
            n_main = 8 * MIX_W
            w_in = od_w_in[j][:, :n_main].astype(BF16)
            w_gate = jnp.pad(od_w_in[j][:, n_main:], ((0, 0), (0, LANES - 2 * N_HEADS))).astype(BF16)
            w_out = od_w_out[j].astype(BF16)
            h, gates = _od_in_prompt(xp, w_in, w_gate, TM, w_in.shape[1])
            cos_t, sin_t = _rope_tables(pos_p, HEAD_W)
            o_c, o_d, sct, dct, dn, dm = _od_prompt(h.reshape(B, T, -1), gates.reshape(B, T, -1), od_b_if[j],
                                                    cos_t, sin_t, 512)
            outs["sc_p"].append(swap(sct))
            outs["dc_p"].append(swap(dct))
            outs["dn_p"].append(dn[:, :N_HEADS, :])
            outs["dm_p"].append(dm[:, :N_HEADS, 0])
            xp = _proj_ln([o_c.reshape(N, MIX_W), o_d.reshape(N, MIX_W)], [w_out[:MIX_W], w_out[MIX_W:]],
                          xp, ln_g[l, 0], ln_b[l, 0], TM, "od_out_prompt",
                          gates=[("group", od_c_norm_g[j], "silu", h, 3), ("group", od_d_norm_g[j], "sigmoid", h, 7)])
            hs = _matmul(xs, w_in, DB, 512, "od_in_sample")
            gates_s = _matmul(xs, w_gate, DB, LANES, "od_gates_sample")
            cos_s, sin_s = _rope_tables(pos_s[:1], HEAD_W)
            oc_s, od_s, sc_s, dc_s, dn_s, dm_s = _od_step(hs, gates_s, od_b_if[j], cos_s, sin_s,
                                                          state_c[j], state_d_c[j], state_d_n[j], state_d_m[j],
                                                          od_c_norm_g[j], od_d_norm_g[j], 8)
            outs["sc_s"].append(sc_s)
            outs["dc_s"].append(dc_s)
            outs["dn_s"].append(dn_s)
            outs["dm_s"].append(dm_s)
            xs = _proj_ln([oc_s, od_s], [w_out[:MIX_W], w_out[MIX_W:]], xs, ln_g[l, 0], ln_b[l, 0], DB,
                          "od_out_sample")

        wq = xa_wq[l].astype(BF16)
        wo = xa_wo[l].astype(BF16)
        mkv = _matmul(mem_prompt.reshape(B * n_mem, D_MODEL), xa_wkv[l].astype(BF16), 512, 512, "mem_kv")
        mk, mv = mkv[:, :D_MODEL], mkv[:, D_MODEL:]
        outs["mk_p"].append(mk.reshape(B, n_mem, N_HEADS, DH_X))
        outs["mv_p"].append(mv.reshape(B, n_mem, N_HEADS, DH_X))
        xp = _xattn_prompt(xp, wq, mk.astype(BF16).reshape(B, n_mem, D_MODEL),
                           mv.astype(BF16).reshape(B, n_mem, D_MODEL), wo, ln_g[l, 1], ln_b[l, 1], T, 1024)
        q_s = _matmul(xs, wq, DB, 512, "xattn_q_sample")
        xo_s = _xattn_sample(q_s, mem_k_rows, mem_v_rows, l * DB)
        xs = _proj_ln([xo_s], [wo], xs, ln_g[l, 1], ln_b[l, 1], DB, "xattn_out_sample")

        w_up = ffn_w_up[l].astype(BF16)
        w_down = ffn_w_down[l].astype(BF16)
        tm_f = min(TM, T)
        xp, tails = _ffn_prompt(xp, w_up, ffn_conv_w[l], ffn_conv_b[l], w_down, ln_g[l, 2], ln_b[l, 2], T, TM, d_ff)
        tails = tails.reshape(B, T // tm_f, SUBLANES, d_ff)
        outs["cv_p"].append(tails[:, -1, SUBLANES - 2:, :])
        ug_s = _matmul(xs, w_up, DB, tf, "ffn_up_sample")
        buf = state_conv[l]
        xs = _ffn_sample(xs, ug_s, buf[:, 0, :], buf[:, 1, :], ffn_conv_w[l], ffn_conv_b[l], w_down,
                         ln_g[l, 2], ln_b[l, 2], tf)
        outs["cv_s"].append(jnp.stack([buf[:, 1, :], ug_s[:, :d_ff]], axis=1))

    st = lambda k: jnp.stack(outs[k])
    return (xp.reshape(B, T, D_MODEL), xs.reshape(DB, 1, D_MODEL),
            st("ak_p"), st("av_p"), st("ak_s"), st("av_s"), st("sb_p"), st("sb_s"),
            st("sc_p"), st("sc_s"), st("dc_p"), st("dc_s"), st("dn_p"), st("dn_s"), st("dm_p"), st("dm_s"),
            st("mk_p"), st("mv_p"), st("cv_p"), st("cv_s"))
```

```python
import functools
import math

import numpy as np
import jax
import jax.numpy as jnp
from jax import lax
from jax.experimental import pallas as pl
from jax.experimental.pallas import tpu as pltpu

F32 = jnp.float32
BF16 = jnp.bfloat16

D_MODEL = 1024
MIX_W = D_MODEL // 2
N_HEADS = 4
HEAD_W = MIX_W // N_HEADS
DH_A = HEAD_W // 2
DH_X = D_MODEL // N_HEADS
CHUNK = 64
OD_CHUNK = 128
ROPE_THETA = 10000.0
NORM_EPS = 1e-5
DEPTH = 2
DN_ALPHA = (2.0 * DEPTH) ** 0.25
LANES = 128
SUBLANES = 8
VMEM_LIMIT = 56 * 1024 * 1024
NEG_INF = float("-inf")

NT_DIMS = (((1,), (1,)), ((), ()))
TN_DIMS = (((0,), (0,)), ((), ()))


def _wspec(w, block_shape, index_map, **kw):
    if isinstance(w, tuple):
        stack, layer = w
        return stack, pl.BlockSpec((None,) + tuple(block_shape), lambda *a: (layer,) + tuple(index_map(*a)), **kw)
    return w, pl.BlockSpec(tuple(block_shape), index_map, **kw)


def _wshape(w):
    return w[0].shape[1:] if isinstance(w, tuple) else w.shape


def _cparams(n_axes, vmem=VMEM_LIMIT):
    return pltpu.CompilerParams(dimension_semantics=("arbitrary",) * n_axes, vmem_limit_bytes=vmem)


def _bdot(a, b):
    return jnp.dot(a.astype(BF16), b.astype(BF16), preferred_element_type=F32)


def _bdot_nt(a, b):
    return lax.dot_general(a.astype(BF16), b.astype(BF16), NT_DIMS, preferred_element_type=F32)


def _bdot_tn(a, b):
    return lax.dot_general(a.astype(BF16), b.astype(BF16), TN_DIMS, preferred_element_type=F32)


def _split3(x):
    p1 = x.astype(BF16)
    r1 = x - p1.astype(F32)
    p2 = r1.astype(BF16)
    p3 = (r1 - p2.astype(F32)).astype(BF16)
    return p1, p2, p3


def _cumsum_rows(tri, x):
    p1, p2, p3 = _split3(x)
    d = functools.partial(jnp.dot, preferred_element_type=F32)
    return d(tri, p1) + d(tri, p2) + d(tri, p3)


def _cumsum_lanes(x, triu):
    p1, p2, p3 = _split3(x)
    d = functools.partial(jnp.dot, preferred_element_type=F32)
    return d(p1, triu) + d(p2, triu) + d(p3, triu)


def _tri(L, lower):
    r = lax.broadcasted_iota(jnp.int32, (L, L), 0)
    c = lax.broadcasted_iota(jnp.int32, (L, L), 1)
    return (r >= c) if lower else (r <= c)


def _sigmoid(x):
    return 1.0 / (1.0 + jnp.exp(-x))


def _silu(x):
    return x * _sigmoid(x)


def _log_sigmoid(x):
    return jnp.minimum(x, 0.0) - jnp.log(1.0 + jnp.exp(-jnp.abs(x)))


def _layernorm(z, g, b):
    mu = jnp.mean(z, -1, keepdims=True)
    zc = z - mu
    var = jnp.mean(zc * zc, -1, keepdims=True)
    return zc * lax.rsqrt(var + NORM_EPS) * g + b


def _rmsnorm(x, g):
    return x * lax.rsqrt(jnp.mean(x * x, -1, keepdims=True) + NORM_EPS) * g


def _groupnorm(x, g):
    mu = jnp.mean(x, -1, keepdims=True)
    xc = x - mu
    var = jnp.mean(xc * xc, -1, keepdims=True)
    return xc * lax.rsqrt(var + NORM_EPS) * g


def _rope(x, cos, sin, half):
    outs = []
    for c in range(x.shape[1] // LANES):
        xc = x[:, c * LANES:(c + 1) * LANES]
        if 2 * half == LANES:
            sw = pltpu.roll(xc, half, 1)
        else:
            lane = lax.broadcasted_iota(jnp.int32, xc.shape, 1)
            first = (lane & (2 * half - 1)) < half
            sw = jnp.where(first, pltpu.roll(xc, LANES - half, 1), pltpu.roll(xc, half, 1))
        outs.append(xc * cos + sw * sin)
    return outs[0] if len(outs) == 1 else jnp.concatenate(outs, axis=1)


def _rope_tables(pos, d):
    inv = ROPE_THETA ** (-jnp.arange(0, d // 2, dtype=F32) * 2.0 / d)
    ang = pos.astype(F32)[:, None] * inv[None, :]
    cos, sin = jnp.cos(ang), jnp.sin(ang)
    reps = LANES // d
    cos_t = jnp.tile(jnp.concatenate([cos, cos], -1), (1, reps))
    sin_t = jnp.tile(jnp.concatenate([-sin, sin], -1), (1, reps))
    return cos_t, sin_t


def _col(row):
    return jnp.broadcast_to(row, (LANES, LANES)).T


def _mm_kernel(x_ref, w_ref, o_ref, xb_ref):
    @pl.when(pl.program_id(1) == 0)
    def _():
        xb_ref[...] = x_ref[...].astype(BF16)

    o_ref[...] = jnp.dot(xb_ref[...], w_ref[...], preferred_element_type=F32).astype(o_ref.dtype)


def _matmul(x, w, tm, tn, name, out_dtype=F32):
    M, K = x.shape
    N = _wshape(w)[1]
    tm, tn = min(tm, M), min(tn, N)
    w, w_spec = _wspec(w, (K, tn), lambda i, j: (0, j))
    return pl.pallas_call(
        _mm_kernel,
        out_shape=jax.ShapeDtypeStruct((M, N), out_dtype),
        grid=(M // tm, N // tn),
        in_specs=[pl.BlockSpec((tm, K), lambda i, j: (i, 0)), w_spec],
        out_specs=pl.BlockSpec((tm, tn), lambda i, j: (i, j)),
        scratch_shapes=[pltpu.VMEM((tm, K), BF16)],
        compiler_params=_cparams(2),
        name=name,
    )(x, w)


_HEAD_NORMS = {"rms": _rmsnorm, "group": _groupnorm}
_GATE_ACTS = {"silu": _silu, "sigmoid": _sigmoid}


def _proj_ln_kernel(*refs, n_in, gated):
    a_refs, w_refs = refs[:n_in], refs[n_in:2 * n_in]
    x_ref, g_ref, b_ref = refs[2 * n_in:2 * n_in + 3]
    extra, o_ref = refs[2 * n_in + 3:-1], refs[-1]
    y = None
    for a_ref, w_ref, gate in zip(a_refs, w_refs, gated):
        a = a_ref[...]
        if gate is not None:
            norm, act = _HEAD_NORMS[gate[0]], _GATE_ACTS[gate[1]]
            gain_ref, gate_ref, extra = extra[0], extra[1], extra[2:]
            a = jnp.concatenate([norm(_head(a, h), gain_ref[...]) for h in range(N_HEADS)], axis=1)
            a = a * act(gate_ref[...])
        ya = _bdot(a, w_ref[...])
        y = ya if y is None else y + ya
    o_ref[...] = _layernorm(DN_ALPHA * x_ref[...] + y, g_ref[...], b_ref[...])


def _proj_ln(acts, weight, x, g, b, tm, name, gates=None):
    M = x.shape[0]
    tm = min(tm, M)
    n_in = len(acts)
    gates = gates or [None] * n_in
    in_specs = [pl.BlockSpec((tm, a.shape[1]), lambda i: (i, 0)) for a in acts]
    w_args = []
    for r, a in enumerate(acts):
        w_arr, w_spec = _wspec(weight, (a.shape[1], D_MODEL), lambda i, r=r: (r, 0))
        w_args.append(w_arr)
        in_specs.append(w_spec)
    in_specs += [pl.BlockSpec((tm, D_MODEL), lambda i: (i, 0)),
                 pl.BlockSpec((1, D_MODEL), lambda i: (0, 0)),
                 pl.BlockSpec((1, D_MODEL), lambda i: (0, 0))]
    extra = []
    for gate in gates:
        if gate is not None:
            _, gain, _, gate_arr, blk = gate
            in_specs += [pl.BlockSpec((1, HEAD_W), lambda i: (0, 0)),
                         pl.BlockSpec((tm, MIX_W), lambda i, blk=blk: (i, blk))]
            extra += [gain.reshape(1, -1), gate_arr]
    return pl.pallas_call(
        functools.partial(_proj_ln_kernel, n_in=n_in,
                          gated=tuple(None if gt is None else (gt[0], gt[2]) for gt in gates)),
        out_shape=jax.ShapeDtypeStruct((M, D_MODEL), F32),
        grid=(M // tm,),
        in_specs=in_specs,
        out_specs=pl.BlockSpec((tm, D_MODEL), lambda i: (i, 0)),
        compiler_params=_cparams(1),
        name=name,
    )(*acts, *w_args, x, g.reshape(1, -1), b.reshape(1, -1), *extra)


Q_SCALE = DH_A ** -0.5 * math.log2(math.e)
VT_ROWS = HEAD_W + 16


def _ev_in_prompt_kernel(x_ref, w_ref, cos_ref, sin_ref, qt_ref, kb_ref, kt_ref, vt_ref, vr_ref, hb_ref):
    xb = x_ref[0].astype(BF16)
    cos, sin = cos_ref[...], sin_ref[...]
    proj = lambda lo, hi: jnp.dot(xb, w_ref[:, lo * MIX_W:hi * MIX_W], preferred_element_type=F32)
    hb_ref[0] = proj(3, 7)
    q = _rope(proj(0, 1), cos, sin, DH_A // 2)
    k = _rope(proj(1, 2), cos, sin, DH_A // 2)
    qt_ref[0] = (q * Q_SCALE).T.astype(BF16)
    kb_ref[0] = k.astype(BF16)
    kt_ref[0] = k.T
    v = proj(2, 3)
    tm = v.shape[0]
    vt = v.T.astype(BF16)
    ones = jnp.ones((VT_ROWS - HEAD_W, tm), BF16)
    for h in range(N_HEADS):
        vt_ref[0, h, :HEAD_W, :] = vt[h * HEAD_W:(h + 1) * HEAD_W]
        vt_ref[0, h, HEAD_W:, :] = ones
        vr_ref[0, pl.ds(h, tm, stride=N_HEADS), :] = _head(v, h)


def _ev_in_prompt(x3, w_in, cos_t, sin_t, tm):
    B, T, _ = x3.shape
    tm = min(tm, T)
    tab = pl.BlockSpec((tm, LANES), lambda b, i: (i, 0))
    tr = pl.BlockSpec((1, MIX_W, tm), lambda b, i: (b, 0, i))
    tshape = lambda dt: jax.ShapeDtypeStruct((B, MIX_W, T), dt)
    return pl.pallas_call(
        _ev_in_prompt_kernel,
        out_shape=(tshape(BF16), jax.ShapeDtypeStruct((B, T, MIX_W), BF16), tshape(F32),
                   jax.ShapeDtypeStruct((B, N_HEADS, VT_ROWS, T), BF16),
                   jax.ShapeDtypeStruct((B, T * N_HEADS, HEAD_W), F32),
                   jax.ShapeDtypeStruct((B, T, 4 * MIX_W), F32)),
        grid=(B, T // tm),
        in_specs=[pl.BlockSpec((1, tm, D_MODEL), lambda b, i: (b, i, 0)),
                  pl.BlockSpec(w_in.shape, lambda b, i: (0, 0)), tab, tab],
        out_specs=(tr, pl.BlockSpec((1, tm, MIX_W), lambda b, i: (b, i, 0)), tr,
                   pl.BlockSpec((1, N_HEADS, VT_ROWS, tm), lambda b, i: (b, 0, 0, i)),
                   pl.BlockSpec((1, tm * N_HEADS, HEAD_W), lambda b, i: (b, i, 0)),
                   pl.BlockSpec((1, tm, 4 * MIX_W), lambda b, i: (b, i, 0))),
        compiler_params=_cparams(2),
        name="ev_in_prompt",
    )(x3, w_in, cos_t, sin_t)


def _ev_prep_sample_kernel(qk_ref, cos_ref, sin_ref, q_ref, k_ref, kt_ref):
    cos, sin = cos_ref[...], sin_ref[...]
    k = _rope(qk_ref[:, MIX_W:], cos, sin, DH_A // 2)
    q_ref[...] = _rope(qk_ref[:, :MIX_W], cos, sin, DH_A // 2) * Q_SCALE
    k_ref[...] = k
    kt_ref[...] = k.T


def _ev_prep_sample(h, cos_t, sin_t):
    DB = h.shape[0]
    full = lambda shape: pl.BlockSpec(shape, lambda i: (0,) * len(shape))
    return pl.pallas_call(
        _ev_prep_sample_kernel,
        out_shape=(jax.ShapeDtypeStruct((DB, MIX_W), F32), jax.ShapeDtypeStruct((DB, MIX_W), F32),
                   jax.ShapeDtypeStruct((MIX_W, DB), F32)),
        grid=(1,),
        in_specs=[full((DB, 2 * MIX_W)), full((DB, LANES)), full((DB, LANES))],
        out_specs=(full((DB, MIX_W)), full((DB, MIX_W)), full((MIX_W, DB))),
        compiler_params=_cparams(1),
        name="ev_prep_sample",
    )(h, cos_t, sin_t)


def _lambda(lam_ref, lam_init):
    lp = lam_ref[...]
    s01 = jnp.sum(lp[0:1] * lp[1:2], axis=1, keepdims=True)
    s23 = jnp.sum(lp[2:3] * lp[3:4], axis=1, keepdims=True)
    return jnp.exp(s01) - jnp.exp(s23) + lam_init


def _dattn_kernel(lam_ref, g_ref, qt_ref, k_ref, vt_ref, o_ref, m_ref, acc_ref, *, tq, lam_init):
    i = pl.program_id(2)
    qt = qt_ref[0]
    sub = lax.broadcasted_iota(jnp.int32, qt.shape, 0)
    zero = jnp.zeros_like(qt)
    q_maps = [jnp.where(sub < DH_A, qt, zero), jnp.where(sub >= DH_A, qt, zero)]
    maps = range(2)
    m_ref[...] = jnp.full(m_ref.shape, NEG_INF, F32)
    acc_ref[...] = jnp.zeros(acc_ref.shape, F32)

    def steps(js, masked):
        ks, vts, ss = [], [], []
        for j in js:
            start = pl.multiple_of(j * tq, tq)
            ks.append(k_ref[0, pl.ds(start, tq), :])
            vts.append(vt_ref[0, 0, :, pl.ds(start, tq)])
        for kj in ks:
            ss.append([jnp.dot(kj, q_maps[c], preferred_element_type=F32) for c in maps])
        for n in range(len(js)):
            for c in maps:
                s = ss[n][c]
                if masked:
                    r = lax.broadcasted_iota(jnp.int32, s.shape, 0)
                    col = lax.broadcasted_iota(jnp.int32, s.shape, 1)
                    s = jnp.where(r <= col, s, NEG_INF)
                m_prev = m_ref[c]
                m_new = jnp.maximum(m_prev, jnp.max(s, axis=0, keepdims=True))
                alpha = jnp.exp2(m_prev - m_new)
                p = jnp.exp2(s - m_new)
                acc_ref[c] = alpha * acc_ref[c] + jnp.dot(vts[n], p.astype(BF16), preferred_element_type=F32)
                m_ref[c] = m_new

    def body(jj, carry):
        steps([2 * jj, 2 * jj + 1], False)
        return carry

    lax.fori_loop(0, lax.shift_right_logical(i, 1), body, 0)

    @pl.when((i & 1) == 1)
    def _():
        steps([i - 1], False)

    steps([i], True)
    lam = _lambda(lam_ref, lam_init)
    o = [acc_ref[c, :HEAD_W, :] / acc_ref[c, HEAD_W:HEAD_W + 1, :] for c in maps]
    d = (o[0] - lam * o[1]).T
    o_ref[0] = (_rmsnorm(d, g_ref[...]) * (1.0 - lam_init)).astype(BF16)


def _dattn_prompt(qt, kb, vt, lam_p, subln_g, lam_init, tq):
    B, T, _ = kb.shape
    tq = min(tq, T)
    return pl.pallas_call(
        functools.partial(_dattn_kernel, tq=tq, lam_init=lam_init),
        out_shape=jax.ShapeDtypeStruct((B, T, MIX_W), BF16),
        grid=(B, N_HEADS, T // tq),
        in_specs=[pl.BlockSpec((4, DH_A), lambda b, h, i: (0, 0)),
                  pl.BlockSpec((1, HEAD_W), lambda b, h, i: (0, 0)),
                  pl.BlockSpec((1, HEAD_W, tq), lambda b, h, i: (b, h, i)),
                  pl.BlockSpec((1, T, HEAD_W), lambda b, h, i: (b, 0, h)),
                  pl.BlockSpec((1, 1, VT_ROWS, T), lambda b, h, i: (b, h, 0, 0))],
        out_specs=pl.BlockSpec((1, tq, HEAD_W), lambda b, h, i: (b, i, h)),
        scratch_shapes=[pltpu.VMEM((2, 1, tq), F32), pltpu.VMEM((2, VT_ROWS, tq), F32)],
        compiler_params=_cparams(3),
        name="dattn_prompt",
    )(lam_p, subln_g.reshape(1, -1), qt, kb, vt)


def _head(x, h):
    return x[:, h * HEAD_W:(h + 1) * HEAD_W]


def _gla_chunk(q_in, k_in, k_end, v, dec, st_ref, tril_mask):
    q_in, k_in, k_end, v = (x.astype(BF16) for x in (q_in, k_in, k_end, v))
    heads = range(N_HEADS)
    sts = [st_ref[h] for h in heads]
    inter = [_bdot_nt(_head(q_in, h), sts[h]) for h in heads]
    upd = [_bdot_tn(_head(v, h), _head(k_end, h)) for h in heads]
    attn = [jnp.where(tril_mask, _bdot_nt(_head(q_in, h), _head(k_in, h)), 0.0) for h in heads]
    intra = [_bdot(attn[h], _head(v, h)) for h in heads]
    for h in heads:
        st_ref[h] = sts[h] * (dec[h] if isinstance(dec, (list, tuple)) else _head(dec, h)) + upd[h]
    return [intra[h] + inter[h] for h in heads]


def _hgrn_kernel(q_ref, f_ref, i_ref, lb_ref, o_ref, s_ref, st_ref, *, tt, L):
    t = pl.program_id(1)

    @pl.when(t == 0)
    def _():
        st_ref[...] = jnp.zeros(st_ref.shape, F32)

    tril_mask = _tri(L, True)
    tril = tril_mask.astype(BF16)

    def body(c, carry):
        rows = pl.ds(pl.multiple_of(c * L, L), L)
        lb = lb_ref[...]
        f = lb + (1.0 - lb) * _sigmoid(f_ref[0, rows, :])
        b = _cumsum_rows(tril, jnp.log(f))
        b_end = b[L - 1:L, :]
        k = 1.0 - f
        q_in = _silu(q_ref[0, rows, :]) * jnp.exp(b)
        o = _gla_chunk(q_in, k * jnp.exp(-b), k * jnp.exp(b_end - b), i_ref[0, rows, :], jnp.exp(b_end),
                       st_ref, tril_mask)
        o_ref[0, rows, :] = jnp.concatenate(o, axis=1)
        return carry

    lax.fori_loop(0, tt // L, body, 0, unroll=2)

    @pl.when(t == pl.num_programs(1) - 1)
    def _():
        s_ref[0] = st_ref[...]


def _hgrn_prompt(h3, lb, tt):
    B, T, _ = h3.shape
    tt = min(tt, T)
    L = math.gcd(T, CHUNK)
    col = lambda c: pl.BlockSpec((1, tt, MIX_W), lambda b, t: (b, t, c))
    return pl.pallas_call(
        functools.partial(_hgrn_kernel, tt=tt, L=L),
        out_shape=(jax.ShapeDtypeStruct((B, T, MIX_W), F32),
                   jax.ShapeDtypeStruct((B, N_HEADS, HEAD_W, HEAD_W), F32)),
        grid=(B, T // tt),
        in_specs=[col(0), col(1), col(2), pl.BlockSpec((1, MIX_W), lambda b, t: (0, 0))],
        out_specs=(pl.BlockSpec((1, tt, MIX_W), lambda b, t: (b, t, 0)),
                   pl.BlockSpec((1, N_HEADS, HEAD_W, HEAD_W), lambda b, t: (b, 0, 0, 0))),
        scratch_shapes=[pltpu.VMEM((N_HEADS, HEAD_W, HEAD_W), F32)],
        compiler_params=_cparams(2),
        name="hgrn_prompt",
    )(h3, h3, h3, lb.reshape(1, -1))


def _log_gamma(h):
    return float(np.log(1.0 - 2.0 ** (-5.0 - h)))


def _od_kernel(cq_ref, ck_ref, cv_ref, dq_ref, dk_ref, dv_ref, gc_ref, gr_ref,
               bc_ref, br_ref, cos_ref, sin_ref,
               oc_ref, od_ref, sc_ref, dc_ref, dn_ref, dm_ref,
               st_ref, ct_ref, n_ref, m_ref, *, tt, L):
    t = pl.program_id(1)

    @pl.when(t == 0)
    def _():
        st_ref[...] = jnp.zeros(st_ref.shape, F32)
        ct_ref[...] = jnp.zeros(ct_ref.shape, F32)
        n_ref[...] = jnp.zeros(n_ref.shape, F32)
        m_ref[...] = jnp.zeros(m_ref.shape, F32)

    tril_mask = _tri(L, True)
    tril = tril_mask.astype(BF16)
    triu = _tri(L, False).astype(BF16)
    scale = HEAD_W ** -0.5
    heads = range(N_HEADS)
    pos1 = (lax.broadcasted_iota(jnp.int32, (L, HEAD_W), 0) + 1).astype(F32)
    ret_b = jnp.concatenate([pos1 * _log_gamma(h) for h in heads], axis=1)
    ret_b_end = jnp.concatenate([jnp.full((L, HEAD_W), L * _log_gamma(h), F32) for h in heads], axis=1)
    ret_q_dec, ret_k_dec, ret_k_end = jnp.exp(ret_b), jnp.exp(-ret_b), jnp.exp(ret_b_end - ret_b)
    ret_dec = [math.exp(L * _log_gamma(h)) for h in heads]

    def body(c, carry):
        rows = pl.ds(pl.multiple_of(c * L, L), L)
        cos, sin = cos_ref[rows, :], sin_ref[rows, :]
        q = _rope(cq_ref[0, rows, :], cos, sin, HEAD_W // 2)
        k = _rope(ck_ref[0, rows, :], cos, sin, HEAD_W // 2) * scale
        o = _gla_chunk(q * ret_q_dec, k * ret_k_dec, k * ret_k_end, cv_ref[0, rows, :], ret_dec, st_ref, tril_mask)
        oc_ref[0, rows, :] = jnp.concatenate(o, axis=1)
        g_c = gc_ref[0, rows, :] + bc_ref[...]
        g_r = gr_ref[0, c] + br_ref[...]
        lf_c = _log_sigmoid(g_c)
        b_c = _cumsum_rows(tril, jnp.concatenate(
            [jnp.broadcast_to(lf_c[:, N_HEADS + h:N_HEADS + h + 1], (L, HEAD_W)) for h in heads], axis=1))
        b_r = _cumsum_lanes(_log_sigmoid(g_r), triu)
        dq = (dq_ref[0, rows, :] * scale)
        dk = dk_ref[0, rows, :]
        dqb, dkb, dvb = dq.astype(BF16), dk.astype(BF16), dv_ref[0, rows, :].astype(BF16)
        cts = [ct_ref[h] for h in heads]
        q_c = [_bdot_nt(_head(dqb, h), cts[h]) for h in heads]
        q_k = [_bdot_nt(_head(dqb, h), _head(dkb, h)) for h in heads]
        m_prev = [m_ref[h:h + 1, 0:1] for h in heads]
        bc1, m_t, w_mat = [], [], []
        for h in heads:
            bc = _head(b_c, h)[:, :L]
            dm = jnp.where(tril_mask, bc - b_r[N_HEADS + h:N_HEADS + h + 1, :] + g_r[h:h + 1, :], NEG_INF)
            bc1.append(bc[:, 0:1])
            m_t.append(jnp.maximum(bc1[h] + m_prev[h], jnp.max(dm, axis=1, keepdims=True)))
            w_mat.append(jnp.exp(dm - m_t[h]) * q_k[h])
        w_v = [_bdot(w_mat[h], _head(dvb, h)) for h in heads]
        kws, c_scales = [], []
        for h in heads:
            m_new = m_t[h][L - 1:L, :]
            b_last = bc1[h][L - 1:L, :]
            c_scales.append(jnp.exp(b_last + m_prev[h] - m_new))
            kws.append(_head(dk, h) * jnp.exp(b_last - bc1[h] + g_c[:, h:h + 1] - m_new))
            m_ref[h:h + 1, :] = jnp.broadcast_to(m_new, (1, LANES))
        c_upd = [_bdot_tn(_head(dvb, h), kws[h]) for h in heads]
        hs = []
        for h in heads:
            n_row = n_ref[h:h + 1, :]
            inter = jnp.exp(bc1[h] + m_prev[h] - m_t[h])
            num = inter * q_c[h] + w_v[h]
            den = (inter * jnp.sum(_head(dq, h) * n_row, axis=1, keepdims=True)
                   + jnp.sum(w_mat[h], axis=1, keepdims=True))
            hs.append(num / jnp.maximum(jnp.abs(den), jnp.exp(-m_t[h])))
            ct_ref[h] = c_scales[h] * cts[h] + c_upd[h]
            n_ref[h:h + 1, :] = c_scales[h] * n_row + jnp.sum(kws[h], axis=0, keepdims=True)
        od_ref[0, rows, :] = jnp.concatenate(hs, axis=1)
        return carry

    lax.fori_loop(0, tt // L, body, 0)

    @pl.when(t == pl.num_programs(1) - 1)
    def _():
        sc_ref[0] = st_ref[...]
        dc_ref[0] = ct_ref[...]
        dn_ref[0] = n_ref[...]
        dm_ref[0] = m_ref[...]


def _od_in_kernel(x_ref, w_ref, wg_ref, h_ref, g_ref, xb_ref):
    @pl.when(pl.program_id(1) == 0)
    def _():
        xb_ref[...] = x_ref[...].astype(BF16)

    xb = xb_ref[...]
    h_ref[...] = jnp.dot(xb, w_ref[...], preferred_element_type=F32)

    @pl.when(pl.program_id(1) == 0)
    def _():
        g_ref[...] = jnp.dot(xb, wg_ref[...], preferred_element_type=F32)


def _od_in_prompt(x, w_in, w_gate, tm, tn):
    M, K = x.shape
    N = w_in.shape[1]
    tm = min(tm, M)
    return pl.pallas_call(
        _od_in_kernel,
        out_shape=(jax.ShapeDtypeStruct((M, N), F32), jax.ShapeDtypeStruct((M, LANES), F32)),
        grid=(M // tm, N // tn),
        in_specs=[pl.BlockSpec((tm, K), lambda i, j: (i, 0)),
                  pl.BlockSpec((K, tn), lambda i, j: (0, j)),
                  pl.BlockSpec((K, LANES), lambda i, j: (0, 0))],
        out_specs=(pl.BlockSpec((tm, tn), lambda i, j: (i, j)), pl.BlockSpec((tm, LANES), lambda i, j: (i, 0))),
        scratch_shapes=[pltpu.VMEM((tm, K), BF16)],
        compiler_params=_cparams(2),
        name="od_in_prompt",
    )(x, w_in, w_gate)


def _od_prompt(h3, gates, b_if, cos_t, sin_t, tt):
    B, T, _ = h3.shape
    tt = min(tt, T)
    L = math.gcd(T, OD_CHUNK)
    gates_r = gates[:, :, :2 * N_HEADS].reshape(B, T // L, L, 2 * N_HEADS).transpose(0, 1, 3, 2)
    b_pad = jnp.zeros((1, LANES), F32).at[0, :2 * N_HEADS].set(b_if)
    col = lambda c: pl.BlockSpec((1, tt, MIX_W), lambda b, t: (b, t, c))
    tab = pl.BlockSpec((tt, LANES), lambda b, t: (t, 0))
    mat_state = pl.BlockSpec((1, N_HEADS, HEAD_W, HEAD_W), lambda b, t: (b, 0, 0, 0))
    row_state = pl.BlockSpec((1, SUBLANES, LANES), lambda b, t: (b, 0, 0))
    out_blk = pl.BlockSpec((1, tt, MIX_W), lambda b, t: (b, t, 0))
    return pl.pallas_call(
        functools.partial(_od_kernel, tt=tt, L=L),
        out_shape=(jax.ShapeDtypeStruct((B, T, MIX_W), F32), jax.ShapeDtypeStruct((B, T, MIX_W), F32),
                   jax.ShapeDtypeStruct((B, N_HEADS, HEAD_W, HEAD_W), F32),
                   jax.ShapeDtypeStruct((B, N_HEADS, HEAD_W, HEAD_W), F32),
                   jax.ShapeDtypeStruct((B, SUBLANES, LANES), F32),
                   jax.ShapeDtypeStruct((B, SUBLANES, LANES), F32)),
        grid=(B, T // tt),
        in_specs=[col(0), col(1), col(2), col(4), col(5), col(6),
                  pl.BlockSpec((1, tt, LANES), lambda b, t: (b, t, 0)),
                  pl.BlockSpec((1, tt // L, 2 * N_HEADS, L), lambda b, t: (b, t, 0, 0)),
                  pl.BlockSpec((1, LANES), lambda b, t: (0, 0)),
                  pl.BlockSpec((2 * N_HEADS, 1), lambda b, t: (0, 0)),
                  tab, tab],
        out_specs=(out_blk, out_blk, mat_state, mat_state, row_state, row_state),
        scratch_shapes=[pltpu.VMEM((N_HEADS, HEAD_W, HEAD_W), F32), pltpu.VMEM((N_HEADS, HEAD_W, HEAD_W), F32),
                        pltpu.VMEM((SUBLANES, LANES), F32), pltpu.VMEM((SUBLANES, LANES), F32)],
        compiler_params=_cparams(2),
        name="od_prompt",
    )(h3, h3, h3, h3, h3, h3, gates, gates_r, b_pad, b_if.reshape(-1, 1), cos_t, sin_t)


def _xattn_kernel(x_ref, wq_ref, mk_ref, mv_ref, wo_ref, g_ref, b_ref, o_ref):
    x = x_ref[...]
    q = _bdot(x, wq_ref[...])
    qb = (q * (DH_X ** -0.5)).astype(BF16)
    heads = range(N_HEADS)
    cols = [slice(h * DH_X, (h + 1) * DH_X) for h in heads]
    s = [lax.dot_general(qb[:, cols[h]], mk_ref[0, :, cols[h]], NT_DIMS, preferred_element_type=F32) for h in heads]
    p = [jnp.exp(s[h] - jnp.max(s[h], axis=1, keepdims=True)) for h in heads]
    l = [jnp.sum(p[h], axis=1, keepdims=True) for h in heads]
    pv = [jnp.dot(p[h].astype(BF16), mv_ref[0, :, cols[h]], preferred_element_type=F32) for h in heads]
    outs = [(pv[h] / l[h]).astype(BF16) for h in heads]
    y = jnp.dot(jnp.concatenate(outs, axis=1), wo_ref[...], preferred_element_type=F32)
    o_ref[...] = _layernorm(DN_ALPHA * x + y, g_ref[...], b_ref[...])


def _xattn_prompt(x, wq, mk, mv, wo, g, b, T, tm):
    M = x.shape[0]
    tm = min(tm, T)
    n_mem = mk.shape[1]
    per_b = T // tm
    full = lambda shape: pl.BlockSpec(shape, lambda i: (0,) * len(shape))
    mem = pl.BlockSpec((1, n_mem, D_MODEL), lambda i: (i // per_b, 0, 0))
    wq, wq_spec = _wspec(wq, (D_MODEL, D_MODEL), lambda i: (0, 0))
    wo, wo_spec = _wspec(wo, (D_MODEL, D_MODEL), lambda i: (0, 0))
    return pl.pallas_call(
        _xattn_kernel,
        out_shape=jax.ShapeDtypeStruct((M, D_MODEL), F32),
        grid=(M // tm,),
        in_specs=[pl.BlockSpec((tm, D_MODEL), lambda i: (i, 0)), wq_spec, mem, mem,
                  wo_spec, full((1, D_MODEL)), full((1, D_MODEL))],
        out_specs=pl.BlockSpec((tm, D_MODEL), lambda i: (i, 0)),
        compiler_params=_cparams(1),
        name="xattn_prompt",
    )(x, wq, mk, mv, wo, g.reshape(1, -1), b.reshape(1, -1))


def _ffn_kernel(x_ref, wu_ref, wg_ref, cw_ref, cb_ref, wd_ref, g_ref, b_ref, o_ref, tail_ref,
                xb_ref, acc_ref, stage_ref, carry_ref, *, tm, per_b, single):
    i, j = pl.program_id(0), pl.program_id(1)

    @pl.when(j == 0)
    def _():
        xb_ref[...] = x_ref[...].astype(BF16)

    @pl.when(i % per_b == 0)
    def _():
        carry_ref[j] = jnp.zeros(carry_ref.shape[1:], F32)

    xb = xb_ref[...]
    stage_ref[0:SUBLANES, :] = carry_ref[j]
    stage_ref[SUBLANES:, :] = jnp.dot(xb, wu_ref[...], preferred_element_type=F32)
    last = stage_ref[tm:tm + SUBLANES, :]
    carry_ref[j] = last
    tail_ref[0] = last
    conv = (cb_ref[...] + cw_ref[0:1, :] * stage_ref[SUBLANES - 2:SUBLANES - 2 + tm, :]
            + cw_ref[1:2, :] * stage_ref[SUBLANES - 1:SUBLANES - 1 + tm, :]
            + cw_ref[2:3, :] * stage_ref[SUBLANES:, :])
    gate = jnp.dot(xb, wg_ref[...], preferred_element_type=F32)
    y = _bdot(jax.nn.gelu(conv) * gate, wd_ref[...])
    if single:
        o_ref[...] = _layernorm(DN_ALPHA * x_ref[...] + y, g_ref[...], b_ref[...])
        return

    @pl.when(j == 0)
    def _():
        acc_ref[...] = y

    @pl.when(j > 0)
    def _():
        acc_ref[...] = acc_ref[...] + y

    @pl.when(j == pl.num_programs(1) - 1)
    def _():
        o_ref[...] = _layernorm(DN_ALPHA * x_ref[...] + acc_ref[...], g_ref[...], b_ref[...])


def _ffn_prompt(x, w_up, conv_w, conv_b, w_down, g, b, T, tm, tf):
    M = x.shape[0]
    d_ff = _wshape(w_down)[0]
    tm = min(tm, T)
    nf = d_ff // tf
    per_b = T // tm
    wmode = dict(pipeline_mode=pl.Buffered(1)) if nf == 1 else {}
    w_up_arr, up_spec = _wspec(w_up, (D_MODEL, tf), lambda i, j: (0, j), **wmode)
    _, gate_spec = _wspec(w_up, (D_MODEL, tf), lambda i, j: (0, nf + j), **wmode)
    w_down_arr, down_spec = _wspec(w_down, (tf, D_MODEL), lambda i, j: (j, 0), **wmode)
    return pl.pallas_call(
        functools.partial(_ffn_kernel, tm=tm, per_b=per_b, single=nf == 1),
        out_shape=(jax.ShapeDtypeStruct((M, D_MODEL), F32),
                   jax.ShapeDtypeStruct((M // tm, SUBLANES, d_ff), F32)),
        grid=(M // tm, nf),
        in_specs=[pl.BlockSpec((tm, D_MODEL), lambda i, j: (i, 0)),
                  up_spec, gate_spec,
                  pl.BlockSpec((3, tf), lambda i, j: (0, j)),
                  pl.BlockSpec((1, tf), lambda i, j: (0, j)),
                  down_spec,
                  pl.BlockSpec((1, D_MODEL), lambda i, j: (0, 0)),
                  pl.BlockSpec((1, D_MODEL), lambda i, j: (0, 0))],
        out_specs=(pl.BlockSpec((tm, D_MODEL), lambda i, j: (i, 0)),
                   pl.BlockSpec((1, SUBLANES, tf), lambda i, j: (i, 0, j))),
        scratch_shapes=[pltpu.VMEM((tm, D_MODEL), BF16), pltpu.VMEM((tm, D_MODEL), F32),
                        pltpu.VMEM((tm + SUBLANES, tf), F32), pltpu.VMEM((nf, SUBLANES, tf), F32)],
        compiler_params=_cparams(2),
        name="ffn_prompt",
    )(x, w_up_arr, w_up_arr, conv_w, conv_b.reshape(1, -1), w_down_arr, g.reshape(1, -1), b.reshape(1, -1))


def _dattn_sample_kernel(*refs, n_pages, page, lam_init):
    pt_ref = refs[0]
    lam_ref, g_ref, q_ref, kn_ref, vn_ref = refs[1:6]
    k_refs = refs[6:6 + n_pages]
    v_refs = refs[6 + n_pages:6 + 2 * n_pages]
    o_ref = refs[6 + 2 * n_pages]
    del pt_ref
    n_rows = 2 * N_HEADS
    q = q_ref[0]
    lane = lax.broadcasted_iota(jnp.int32, (n_rows, MIX_W), 1)
    row = lax.broadcasted_iota(jnp.int32, (n_rows, MIX_W), 0)
    qbd = jnp.where(lax.shift_right_logical(lane, 6) == row, jnp.broadcast_to(q, (n_rows, MIX_W)), 0.0)
    s_new = jnp.sum(qbd * kn_ref[0], axis=1, keepdims=True)
    scores = [_bdot(qbd, k_refs[p][0]) for p in range(n_pages)]
    m = s_new
    for s in scores:
        m = jnp.maximum(m, jnp.max(s, axis=1, keepdims=True))
    p_new = jnp.exp2(s_new - m)
    l = p_new
    vn = vn_ref[0]
    accs = [p_new * vn[:, h * HEAD_W:(h + 1) * HEAD_W] for h in range(N_HEADS)]
    for p in range(n_pages):
        pr = jnp.exp2(scores[p] - m)
        l = l + jnp.sum(pr, axis=1, keepdims=True)
        for h in range(N_HEADS):
            accs[h] = accs[h] + _bdot(pr, v_refs[p][0, pl.ds(h, page, stride=N_HEADS), :])
    lam = _lambda(lam_ref, lam_init)
    outs = []
    for h in range(N_HEADS):
        o = accs[h] / l
        d = o[2 * h:2 * h + 1, :] - lam * o[2 * h + 1:2 * h + 2, :]
        outs.append(_rmsnorm(d, g_ref[...]) * (1.0 - lam_init))
    o_ref[0] = jnp.concatenate(outs, axis=1)


def _dattn_sample(q, k_new, v_new, kt_pool, v_pool, pool_base, page_table, lam_p, subln_g, lam_init):
    DB = q.shape[0]
    n_pages = page_table.shape[1]
    page = kt_pool.shape[2]
    tok = pl.BlockSpec((1, 1, MIX_W), lambda b, pt: (b, 0, 0))

    def k_spec(p):
        return pl.BlockSpec((1, MIX_W, page), lambda b, pt: (pool_base + pt[b, p], 0, 0))

    def v_spec(p):
        return pl.BlockSpec((1, page * N_HEADS, HEAD_W), lambda b, pt: (pool_base + pt[b, p], 0, 0))

    grid_spec = pltpu.PrefetchScalarGridSpec(
        num_scalar_prefetch=1,
        grid=(DB,),
        in_specs=[pl.BlockSpec((4, DH_A), lambda b, pt: (0, 0)),
                  pl.BlockSpec((1, HEAD_W), lambda b, pt: (0, 0)), tok, tok, tok]
                 + [k_spec(p) for p in range(n_pages)] + [v_spec(p) for p in range(n_pages)],
        out_specs=tok,
    )
    out = pl.pallas_call(
        functools.partial(_dattn_sample_kernel, n_pages=n_pages, page=page, lam_init=lam_init),
        out_shape=jax.ShapeDtypeStruct((DB, 1, MIX_W), F32),
        grid_spec=grid_spec,
        compiler_params=_cparams(1),
        name="dattn_sample",
    )(page_table, lam_p, subln_g.reshape(1, -1), q.reshape(DB, 1, MIX_W), k_new.reshape(DB, 1, MIX_W),
      v_new.reshape(DB, 1, MIX_W), *([kt_pool] * n_pages), *([v_pool] * n_pages))
    return out.reshape(DB, MIX_W)


def _hgrn_step_kernel(h_ref, s0_ref, lb_ref, ng_ref, o_ref, s_ref, *, bt):
    def body(i, carry):
        row = h_ref[i]
        heads = range(N_HEADS)
        blk = lambda n: row[:, n * MIX_W:(n + 1) * MIX_W]
        lb = lb_ref[...]
        f = lb + (1.0 - lb) * _sigmoid(blk(4))
        q, v = _silu(blk(3)), blk(5)
        f_col = [_col(_head(f, h)) for h in heads]
        s_new = [f_col[h] * s0_ref[i, h] + (1.0 - f_col[h]) * _head(v, h) for h in heads]
        for h in heads:
            s_ref[i, h] = s_new[h]
        o = [_bdot(jnp.broadcast_to(_head(q, h), (SUBLANES, HEAD_W)), s_new[h])[0:1, :] for h in heads]
        o = jnp.concatenate([_rmsnorm(o[h], ng_ref[...]) for h in heads], axis=1)
        o_ref[i] = o * _silu(blk(6))
        return carry

    lax.fori_loop(0, bt, body, 0, unroll=2)


def _hgrn_step(h, s0, lb, norm_g, bt):
    DB, W = h.shape
    bt = min(bt, DB)
    state = pl.BlockSpec((bt, N_HEADS, HEAD_W, HEAD_W), lambda i: (i, 0, 0, 0))
    o, s = pl.pallas_call(
        functools.partial(_hgrn_step_kernel, bt=bt),
        out_shape=(jax.ShapeDtypeStruct((DB, 1, MIX_W), F32), jax.ShapeDtypeStruct(s0.shape, F32)),
        grid=(DB // bt,),
        in_specs=[pl.BlockSpec((bt, 1, W), lambda i: (i, 0, 0)), state,
                  pl.BlockSpec((1, MIX_W), lambda i: (0, 0)), pl.BlockSpec((1, HEAD_W), lambda i: (0, 0))],
        out_specs=(pl.BlockSpec((bt, 1, MIX_W), lambda i: (i, 0, 0)), state),
        compiler_params=_cparams(1),
        name="hgrn_step",
    )(h.reshape(DB, 1, W), s0, lb.reshape(1, -1), norm_g.reshape(1, -1))
    return o.reshape(DB, MIX_W), s


def _od_step_kernel(h_ref, gt_ref, bif_ref, cos_ref, sin_ref, sc0_ref, dc0_ref, dn0_ref, dm0_ref, cng_ref, dng_ref,
                    oc_ref, od_ref, sc_ref, dc_ref, dn_ref, dm_ref, *, bt):
    scale = HEAD_W ** -0.5

    def body(i, carry):
        row = h_ref[i]
        cos, sin = cos_ref[...], sin_ref[...]
        gates = gt_ref[i] + bif_ref[...]
        m_row = dm0_ref[i]
        heads = range(N_HEADS)
        blk = lambda n: row[:, n * MIX_W:(n + 1) * MIX_W]
        q = _rope(blk(0), cos, sin, HEAD_W // 2)
        k = _rope(blk(1), cos, sin, HEAD_W // 2) * scale
        cv, dk, dv, dq = blk(2), blk(5), blk(6), blk(4) * scale
        lf = _log_sigmoid(gates)
        m_new = [jnp.maximum(lf[:, N_HEADS + h:N_HEADS + h + 1] + m_row[:, h:h + 1], gates[:, h:h + 1])
                 for h in heads]
        c_scale = [jnp.exp(lf[:, N_HEADS + h:N_HEADS + h + 1] + m_row[:, h:h + 1] - m_new[h]) for h in heads]
        kw = [_head(dk, h) * jnp.exp(gates[:, h:h + 1] - m_new[h]) for h in heads]
        k_col = [_col(_head(k, h)) for h in heads]
        kw_col = [_col(kw[h]) for h in heads]
        s_new = [math.exp(_log_gamma(h)) * sc0_ref[i, h] + k_col[h] * _head(cv, h) for h in heads]
        c_new = [c_scale[h] * dc0_ref[i, h] + kw_col[h] * _head(dv, h) for h in heads]
        n_new = [c_scale[h] * dn0_ref[i, h:h + 1, :] + kw[h] for h in heads]
        for h in heads:
            sc_ref[i, h] = s_new[h]
            dc_ref[i, h] = c_new[h]
            dn_ref[i, h:h + 1, :] = n_new[h]
        rows8 = lambda x: jnp.broadcast_to(x, (SUBLANES, HEAD_W))
        o = [_bdot(rows8(_head(q, h)), s_new[h])[0:1, :] for h in heads]
        num = [_bdot(rows8(_head(dq, h)), c_new[h])[0:1, :] for h in heads]
        den = [jnp.sum(_head(dq, h) * n_new[h], axis=1, keepdims=True) for h in heads]
        hh = [num[h] / jnp.maximum(jnp.abs(den[h]), jnp.exp(-m_new[h])) for h in heads]
        oc = jnp.concatenate([_groupnorm(o[h], cng_ref[...]) for h in heads], axis=1)
        od = jnp.concatenate([_groupnorm(hh[h], dng_ref[...]) for h in heads], axis=1)
        oc_ref[i] = oc * _silu(blk(3))
        od_ref[i] = od * _sigmoid(blk(7))
        dm_ref[i] = jnp.concatenate(m_new, axis=1)
        return carry

    lax.fori_loop(0, bt, body, 0)


def _od_step(h, gates, b_if, cos_t, sin_t, sc0, dc0, dn0, dm0, c_norm_g, d_norm_g, bt):
    DB, W = h.shape
    bt = min(bt, DB)
    mat = pl.BlockSpec((bt, N_HEADS, HEAD_W, HEAD_W), lambda i: (i, 0, 0, 0))
    nblk = pl.BlockSpec((bt, N_HEADS, HEAD_W), lambda i: (i, 0, 0))
    mblk = pl.BlockSpec((bt, 1, N_HEADS), lambda i: (i, 0, 0))
    vec = pl.BlockSpec((1, LANES), lambda i: (0, 0))
    out = pl.BlockSpec((bt, 1, MIX_W), lambda i: (i, 0, 0))
    b_pad = jnp.zeros((1, LANES), F32).at[0, :2 * N_HEADS].set(b_if)
    oc, od, sc, dc, dn, dm = pl.pallas_call(
        functools.partial(_od_step_kernel, bt=bt),
        out_shape=(jax.ShapeDtypeStruct((DB, 1, MIX_W), F32), jax.ShapeDtypeStruct((DB, 1, MIX_W), F32),
                   jax.ShapeDtypeStruct(sc0.shape, F32), jax.ShapeDtypeStruct(dc0.shape, F32),
                   jax.ShapeDtypeStruct(dn0.shape, F32), jax.ShapeDtypeStruct((DB, 1, N_HEADS), F32)),
        grid=(DB // bt,),
        in_specs=[pl.BlockSpec((bt, 1, W), lambda i: (i, 0, 0)),
                  pl.BlockSpec((bt, 1, LANES), lambda i: (i, 0, 0)), vec, vec, vec,
                  mat, mat, nblk, mblk, vec, vec],
        out_specs=(out, out, mat, mat, nblk, mblk),
        compiler_params=_cparams(1),
        name="od_step",
    )(h.reshape(DB, 1, W), gates.reshape(DB, 1, LANES), b_pad, cos_t, sin_t, sc0, dc0, dn0,
      dm0.reshape(DB, 1, N_HEADS), c_norm_g.reshape(1, -1), d_norm_g.reshape(1, -1))
    return oc.reshape(DB, MIX_W), od.reshape(DB, MIX_W), sc, dc, dn, dm.reshape(DB, N_HEADS)


X_HALVES = DH_X // LANES
X_ROWS = N_HEADS * X_HALVES


def _xattn_sample_kernel(q_ref, mk_ref, mv_ref, o_ref, *, xb):
    for t in range(xb):
        _xattn_sample_row(q_ref.at[t], mk_ref.at[t], mv_ref.at[t], o_ref.at[t])


def _xattn_sample_row(q_ref, mk_ref, mv_ref, o_ref):
    q = q_ref[...] * (DH_X ** -0.5)
    n_cols = mk_ref.shape[0]
    q2 = jnp.concatenate([q[:, h * DH_X + c * LANES:h * DH_X + (c + 1) * LANES]
                          for c in range(X_HALVES) for h in range(N_HEADS)], axis=0)
    s2 = _bdot_nt(q2, mk_ref[...])
    row = lax.broadcasted_iota(jnp.int32, (X_ROWS, n_cols), 0)
    col = lax.broadcasted_iota(jnp.int32, (X_ROWS, n_cols), 1)
    own = (col & (X_ROWS - 1)) == row
    s2 = jnp.where(own, s2, 0.0)
    s4 = s2[:N_HEADS] + pltpu.roll(s2, n_cols - N_HEADS, 1)[N_HEADS:]
    own4 = own[:N_HEADS]
    s4 = jnp.where(own4, s4, NEG_INF)
    p4 = jnp.exp(s4 - jnp.max(s4, axis=1, keepdims=True))
    l4 = jnp.sum(p4, axis=1, keepdims=True)
    p4 = p4 / l4
    p2 = jnp.concatenate([p4, pltpu.roll(p4, N_HEADS, 1)], axis=0)
    o2 = _bdot(p2, mv_ref[...])
    o_ref[...] = jnp.concatenate([o2[c * N_HEADS + h:c * N_HEADS + h + 1, :]
                                  for h in range(N_HEADS) for c in range(X_HALVES)], axis=1)


def _xattn_sample(q, mk, mv, base):
    DB = q.shape[0]
    n_rows = mk.shape[1]
    xb = math.gcd(DB, 4)
    tok = pl.BlockSpec((xb, 1, D_MODEL), lambda b: (b, 0, 0))
    mem = pl.BlockSpec((xb, n_rows, LANES), lambda b: (base // xb + b, 0, 0))
    assert base % xb == 0
    out = pl.pallas_call(
        functools.partial(_xattn_sample_kernel, xb=xb),
        out_shape=jax.ShapeDtypeStruct((DB, 1, D_MODEL), F32),
        grid=(DB // xb,),
        in_specs=[tok, mem, mem],
        out_specs=tok,
        compiler_params=_cparams(1),
        name="xattn_sample",
    )(q.reshape(DB, 1, D_MODEL), mk, mv)
    return out.reshape(DB, D_MODEL)


def _mem_rows(cache):
    n_l, DB, n_mem = cache.shape[:3]
    c = cache.reshape(n_l, DB, n_mem, N_HEADS, X_HALVES, LANES).transpose(0, 1, 2, 4, 3, 5)
    return c.reshape(n_l * DB, n_mem * X_ROWS, LANES)


def _ffn_sample_kernel(x_ref, up_ref, gate_ref, b0_ref, b1_ref, cw_ref, cb_ref, wd_ref, g_ref, b_ref, o_ref, acc_ref):
    j = pl.program_id(0)
    conv = (cb_ref[...] + cw_ref[0:1, :] * b0_ref[...] + cw_ref[1:2, :] * b1_ref[...]
            + cw_ref[2:3, :] * up_ref[...])
    y = _bdot(jax.nn.gelu(conv) * gate_ref[...], wd_ref[...])

    @pl.when(j == 0)
    def _():
        acc_ref[...] = y

    @pl.when(j > 0)
    def _():
        acc_ref[...] = acc_ref[...] + y

    @pl.when(j == pl.num_programs(0) - 1)
    def _():
        o_ref[...] = _layernorm(DN_ALPHA * x_ref[...] + acc_ref[...], g_ref[...], b_ref[...])


def _ffn_sample(x, ug, buf0, buf1, conv_w, conv_b, w_down, g, b, tf):
    DB = x.shape[0]
    d_ff = _wshape(w_down)[0]
    nf = d_ff // tf
    ff = lambda off: pl.BlockSpec((DB, tf), lambda j: (0, off + j))
    full = lambda shape: pl.BlockSpec(shape, lambda j: (0,) * len(shape))
    w_down, down_spec = _wspec(w_down, (tf, D_MODEL), lambda j: (j, 0))
    return pl.pallas_call(
        _ffn_sample_kernel,
        out_shape=jax.ShapeDtypeStruct((DB, D_MODEL), F32),
        grid=(nf,),
        in_specs=[full((DB, D_MODEL)), ff(0), ff(nf), ff(0), ff(0),
                  pl.BlockSpec((3, tf), lambda j: (0, j)), pl.BlockSpec((1, tf), lambda j: (0, j)),
                  down_spec, full((1, D_MODEL)), full((1, D_MODEL))],
        out_specs=full((DB, D_MODEL)),
        scratch_shapes=[pltpu.VMEM((DB, D_MODEL), F32)],
        compiler_params=_cparams(1),
        name="ffn_sample",
    )(x, ug, ug, buf0, buf1, conv_w, conv_b.reshape(1, -1), w_down, g.reshape(1, -1), b.reshape(1, -1))


def kernel(x_prompt, x_sample, cache_a_k, cache_a_v, state_b, state_c, state_d_c, state_d_n, state_d_m,
           cache_mem_k, cache_mem_v, state_conv, page_table, mem_prompt,
           ev_w_in, ev_w_out, ev_lam, ev_subln_g, ev_lb_logits, ev_b_norm_g,
           od_w_in, od_b_if, od_w_out, od_c_norm_g, od_d_norm_g,
           ln_g, ln_b, xa_wq, xa_wkv, xa_wo, ffn_w_up, ffn_conv_w, ffn_conv_b, ffn_w_down):
    B, T, _ = x_prompt.shape
    DB, t_s, _ = x_sample.shape
    assert t_s == 1, "the sample group is a single decoding step"
    assert T % CHUNK == 0
    n_pool, page = cache_a_k.shape[1], cache_a_k.shape[2]
    n_pages = page_table.shape[1]
    past = n_pages * page
    n_mem = mem_prompt.shape[1]
    d_ff = ffn_w_down.shape[1]
    tf = d_ff // 2
    N = B * T
    TM = 512

    pos_p = jnp.arange(T)
    pos_s = jnp.full((DB,), past, jnp.int32)
    lb_table = jnp.cumsum(jax.nn.softmax(ev_lb_logits.astype(F32), axis=0), axis=0)

    xp = x_prompt.reshape(N, D_MODEL)
    xs = x_sample.reshape(DB, D_MODEL)
    outs = {k: [] for k in ("ak_p", "av_p", "ak_s", "av_s", "sb_p", "sb_s", "sc_p", "sc_s", "dc_p", "dc_s",
                            "dn_p", "dn_s", "dm_p", "dm_s", "mk_p", "mv_p", "cv_p", "cv_s")}
    swap = lambda s: jnp.swapaxes(s, -1, -2)
    mem_k_rows, mem_v_rows = _mem_rows(cache_mem_k), _mem_rows(cache_mem_v)
    wq_all, wkv_all, wo_all = xa_wq.astype(BF16), xa_wkv.astype(BF16), xa_wo.astype(BF16)
    w_up_all, w_down_all = ffn_w_up.astype(BF16), ffn_w_down.astype(BF16)

    for l in range(DEPTH):
        j = l // 2
        if l % 2 == 0:
            lam_init = 0.8 - 0.6 * math.exp(-0.3 * l)
            w_in = ev_w_in[j].astype(BF16)
            w_out = ev_w_out[j].astype(BF16)
            cos_t, sin_t = _rope_tables(pos_p, DH_A)
            qt, kb, kt32, vt, v_rows, h_b = _ev_in_prompt(xp.reshape(B, T, D_MODEL), w_in, cos_t, sin_t, TM)
            o_a = _dattn_prompt(qt, kb, vt, ev_lam[j], ev_subln_g[j], lam_init, 512)
            o_b, st = _hgrn_prompt(h_b, lb_table[j], 512)
            outs["ak_p"].append(kt32.reshape(B, N_HEADS, 2, DH_A, T).transpose(0, 4, 1, 2, 3))
            outs["av_p"].append(v_rows.reshape(B, T, N_HEADS, HEAD_W))
            outs["sb_p"].append(swap(st))
            xp = _proj_ln([o_a.reshape(N, MIX_W), o_b.reshape(N, MIX_W)], w_out,
                          xp, ln_g[l, 0], ln_b[l, 0], TM, "ev_out_prompt",
                          gates=[None, ("rms", ev_b_norm_g[j], "silu", h_b.reshape(N, -1), 3)])
            hs = _matmul(xs, w_in, DB, 512, "ev_in_sample")
            cos_s, sin_s = _rope_tables(pos_s, DH_A)
            qs, ks32, kts32 = _ev_prep_sample(hs, cos_s, sin_s)
            vs32 = hs[:, 2 * MIX_W:3 * MIX_W]
            kt_pool = cache_a_k.transpose(0, 1, 3, 4, 5, 2).reshape(-1, MIX_W, page)
            v_pool = cache_a_v.reshape(-1, page * N_HEADS, HEAD_W)
            oa_s = _dattn_sample(qs, ks32, vs32, kt_pool, v_pool, j * n_pool, page_table,
                                 ev_lam[j], ev_subln_g[j], lam_init)
            ob_s, sb_s = _hgrn_step(hs, state_b[j], lb_table[j], ev_b_norm_g[j], 8)
            outs["ak_s"].append(kts32.reshape(N_HEADS, 2, DH_A, DB).transpose(3, 0, 1, 2)[:, None])
            outs["av_s"].append(vs32.reshape(DB, 1, N_HEADS, HEAD_W))
            outs["sb_s"].append(sb_s)
            xs = _proj_ln([oa_s, ob_s], w_out, xs, ln_g[l, 0], ln_b[l, 0], DB, "ev_out_sample")
        else:
            n_main = 8 * MIX_W
            w_in = od_w_in[j][:, :n_main].astype(BF16)
            w_gate = jnp.pad(od_w_in[j][:, n_main:], ((0, 0), (0, LANES - 2 * N_HEADS))).astype(BF16)
            w_out = od_w_out[j].astype(BF16)
            h, gates = _od_in_prompt(xp, w_in, w_gate, TM, w_in.shape[1])
            cos_t, sin_t = _rope_tables(pos_p, HEAD_W)
            o_c, o_d, sct, dct, dn, dm = _od_prompt(h.reshape(B, T, -1), gates.reshape(B, T, -1), od_b_if[j],
                                                    cos_t, sin_t, 512)
            outs["sc_p"].append(swap(sct))
            outs["dc_p"].append(swap(dct))
            outs["dn_p"].append(dn[:, :N_HEADS, :])
            outs["dm_p"].append(dm[:, :N_HEADS, 0])
            xp = _proj_ln([o_c.reshape(N, MIX_W), o_d.reshape(N, MIX_W)], w_out,
                          xp, ln_g[l, 0], ln_b[l, 0], TM, "od_out_prompt",
                          gates=[("group", od_c_norm_g[j], "silu", h, 3), ("group", od_d_norm_g[j], "sigmoid", h, 7)])
            hs = _matmul(xs, w_in, DB, 512, "od_in_sample")
            gates_s = _matmul(xs, w_gate, DB, LANES, "od_gates_sample")
            cos_s, sin_s = _rope_tables(pos_s[:1], HEAD_W)
            oc_s, od_s, sc_s, dc_s, dn_s, dm_s = _od_step(hs, gates_s, od_b_if[j], cos_s, sin_s,
                                                          state_c[j], state_d_c[j], state_d_n[j], state_d_m[j],
                                                          od_c_norm_g[j], od_d_norm_g[j], 8)
            outs["sc_s"].append(sc_s)
            outs["dc_s"].append(dc_s)
            outs["dn_s"].append(dn_s)
            outs["dm_s"].append(dm_s)
            xs = _proj_ln([oc_s, od_s], w_out, xs, ln_g[l, 0], ln_b[l, 0], DB, "od_out_sample")

        wq, wo = (wq_all, l), (wo_all, l)
        mkv = _matmul(mem_prompt.reshape(B * n_mem, D_MODEL), (wkv_all, l), 512, 512, "mem_kv")
        mk, mv = mkv[:, :D_MODEL], mkv[:, D_MODEL:]
        outs["mk_p"].append(mk.reshape(B, n_mem, N_HEADS, DH_X))
        outs["mv_p"].append(mv.reshape(B, n_mem, N_HEADS, DH_X))
        xp = _xattn_prompt(xp, wq, mk.astype(BF16).reshape(B, n_mem, D_MODEL),
                           mv.astype(BF16).reshape(B, n_mem, D_MODEL), wo, ln_g[l, 1], ln_b[l, 1], T, 1024)
        q_s = _matmul(xs, wq, DB, 512, "xattn_q_sample")
        xo_s = _xattn_sample(q_s, mem_k_rows, mem_v_rows, l * DB)
        xs = _proj_ln([xo_s], wo, xs, ln_g[l, 1], ln_b[l, 1], DB, "xattn_out_sample")

        w_up, w_down = (w_up_all, l), (w_down_all, l)
        tm_f = min(TM, T)
        xp, tails = _ffn_prompt(xp, w_up, ffn_conv_w[l], ffn_conv_b[l], w_down, ln_g[l, 2], ln_b[l, 2], T, TM, d_ff)
        tails = tails.reshape(B, T // tm_f, SUBLANES, d_ff)
        outs["cv_p"].append(tails[:, -1, SUBLANES - 2:, :])
        ug_s = _matmul(xs, w_up, DB, tf, "ffn_up_sample")
        buf = state_conv[l]
        xs = _ffn_sample(xs, ug_s, buf[:, 0, :], buf[:, 1, :], ffn_conv_w[l], ffn_conv_b[l], w_down,
                         ln_g[l, 2], ln_b[l, 2], tf)
        outs["cv_s"].append(jnp.stack([buf[:, 1, :], ug_s[:, :d_ff]], axis=1))

    st = lambda k: jnp.stack(outs[k])
    return (xp.reshape(B, T, D_MODEL), xs.reshape(DB, 1, D_MODEL),
            st("ak_p"), st("av_p"), st("ak_s"), st("av_s"), st("sb_p"), st("sb_s"),
            st("sc_p"), st("sc_s"), st("dc_p"), st("dc_s"), st("dn_p"), st("dn_s"), st("dm_p"), st("dm_s"),
            st("mk_p"), st("mv_p"), st("cv_p"), st("cv_s"))
```

```python
import functools
import math

import numpy as np
import jax
import jax.numpy as jnp
from jax import lax
from jax.experimental import pallas as pl
from jax.experimental.pallas import tpu as pltpu

F32 = jnp.float32
BF16 = jnp.bfloat16

D_MODEL = 1024
MIX_W = D_MODEL // 2
N_HEADS = 4
HEAD_W = MIX_W // N_HEADS
DH_A = HEAD_W // 2
DH_X = D_MODEL // N_HEADS
CHUNK = 64
OD_CHUNK = 128
ROPE_THETA = 10000.0
NORM_EPS = 1e-5
DEPTH = 2
DN_ALPHA = (2.0 * DEPTH) ** 0.25
LANES = 128
SUBLANES = 8
VMEM_LIMIT = 56 * 1024 * 1024
NEG_INF = float("-inf")

NT_DIMS = (((1,), (1,)), ((), ()))
TN_DIMS = (((0,), (0,)), ((), ()))


def _wspec(w, block_shape, index_map, **kw):
    if isinstance(w, tuple):
        stack, layer = w
        return stack, pl.BlockSpec((None,) + tuple(block_shape), lambda *a: (layer,) + tuple(index_map(*a)), **kw)
    return w, pl.BlockSpec(tuple(block_shape), index_map, **kw)


def _wshape(w):
    return w[0].shape[1:] if isinstance(w, tuple) else w.shape


def _cparams(n_axes, vmem=VMEM_LIMIT):
    return pltpu.CompilerParams(dimension_semantics=("arbitrary",) * n_axes, vmem_limit_bytes=vmem)


def _bdot(a, b):
    return jnp.dot(a.astype(BF16), b.astype(BF16), preferred_element_type=F32)


def _bdot_nt(a, b):
    return lax.dot_general(a.astype(BF16), b.astype(BF16), NT_DIMS, preferred_element_type=F32)


def _bdot_tn(a, b):
    return lax.dot_general(a.astype(BF16), b.astype(BF16), TN_DIMS, preferred_element_type=F32)


def _split3(x):
    p1 = x.astype(BF16)
    r1 = x - p1.astype(F32)
    p2 = r1.astype(BF16)
    p3 = (r1 - p2.astype(F32)).astype(BF16)
    return p1, p2, p3


def _cumsum_rows(tri, x):
    p1, p2, p3 = _split3(x)
    d = functools.partial(jnp.dot, preferred_element_type=F32)
    return d(tri, p1) + d(tri, p2) + d(tri, p3)


def _cumsum_lanes(x, triu):
    p1, p2, p3 = _split3(x)
    d = functools.partial(jnp.dot, preferred_element_type=F32)
    return d(p1, triu) + d(p2, triu) + d(p3, triu)


def _tri(L, lower):
    r = lax.broadcasted_iota(jnp.int32, (L, L), 0)
    c = lax.broadcasted_iota(jnp.int32, (L, L), 1)
    return (r >= c) if lower else (r <= c)


def _sigmoid(x):
    return 1.0 / (1.0 + jnp.exp(-x))


def _silu(x):
    return x * _sigmoid(x)


def _log_sigmoid(x):
    return jnp.minimum(x, 0.0) - jnp.log(1.0 + jnp.exp(-jnp.abs(x)))


def _layernorm(z, g, b):
    mu = jnp.mean(z, -1, keepdims=True)
    zc = z - mu
    var = jnp.mean(zc * zc, -1, keepdims=True)
    return zc * lax.rsqrt(var + NORM_EPS) * g + b


def _rmsnorm(x, g):
    return x * lax.rsqrt(jnp.mean(x * x, -1, keepdims=True) + NORM_EPS) * g


def _groupnorm(x, g):
    mu = jnp.mean(x, -1, keepdims=True)
    xc = x - mu
    var = jnp.mean(xc * xc, -1, keepdims=True)
    return xc * lax.rsqrt(var + NORM_EPS) * g


def _rope(x, cos, sin, half):
    outs = []
    for c in range(x.shape[1] // LANES):
        xc = x[:, c * LANES:(c + 1) * LANES]
        if 2 * half == LANES:
            sw = pltpu.roll(xc, half, 1)
        else:
            lane = lax.broadcasted_iota(jnp.int32, xc.shape, 1)
            first = (lane & (2 * half - 1)) < half
            sw = jnp.where(first, pltpu.roll(xc, LANES - half, 1), pltpu.roll(xc, half, 1))
        outs.append(xc * cos + sw * sin)
    return outs[0] if len(outs) == 1 else jnp.concatenate(outs, axis=1)


def _rope_tables(pos, d):
    inv = ROPE_THETA ** (-jnp.arange(0, d // 2, dtype=F32) * 2.0 / d)
    ang = pos.astype(F32)[:, None] * inv[None, :]
    cos, sin = jnp.cos(ang), jnp.sin(ang)
    reps = LANES // d
    cos_t = jnp.tile(jnp.concatenate([cos, cos], -1), (1, reps))
    sin_t = jnp.tile(jnp.concatenate([-sin, sin], -1), (1, reps))
    return cos_t, sin_t


def _col(row):
    return jnp.broadcast_to(row, (LANES, LANES)).T


def _mm_kernel(x_ref, w_ref, o_ref, xb_ref):
    @pl.when(pl.program_id(1) == 0)
    def _():
        xb_ref[...] = x_ref[...].astype(BF16)

    o_ref[...] = jnp.dot(xb_ref[...], w_ref[...], preferred_element_type=F32).astype(o_ref.dtype)


def _matmul(x, w, tm, tn, name, out_dtype=F32):
    M, K = x.shape
    N = _wshape(w)[1]
    tm, tn = min(tm, M), min(tn, N)
    w, w_spec = _wspec(w, (K, tn), lambda i, j: (0, j))
    return pl.pallas_call(
        _mm_kernel,
        out_shape=jax.ShapeDtypeStruct((M, N), out_dtype),
        grid=(M // tm, N // tn),
        in_specs=[pl.BlockSpec((tm, K), lambda i, j: (i, 0)), w_spec],
        out_specs=pl.BlockSpec((tm, tn), lambda i, j: (i, j)),
        scratch_shapes=[pltpu.VMEM((tm, K), BF16)],
        compiler_params=_cparams(2),
        name=name,
    )(x, w)


_HEAD_NORMS = {"rms": _rmsnorm, "group": _groupnorm}
_GATE_ACTS = {"silu": _silu, "sigmoid": _sigmoid}


def _proj_ln_kernel(*refs, n_in, gated):
    a_refs, w_refs = refs[:n_in], refs[n_in:2 * n_in]
    x_ref, g_ref, b_ref = refs[2 * n_in:2 * n_in + 3]
    extra, o_ref = refs[2 * n_in + 3:-1], refs[-1]
    y = None
    for a_ref, w_ref, gate in zip(a_refs, w_refs, gated):
        a = a_ref[...]
        if gate is not None:
            norm, act = _HEAD_NORMS[gate[0]], _GATE_ACTS[gate[1]]
            gain_ref, gate_ref, extra = extra[0], extra[1], extra[2:]
            a = jnp.concatenate([norm(_head(a, h), gain_ref[...]) for h in range(N_HEADS)], axis=1)
            a = a * act(gate_ref[...])
        ya = _bdot(a, w_ref[...])
        y = ya if y is None else y + ya
    o_ref[...] = _layernorm(DN_ALPHA * x_ref[...] + y, g_ref[...], b_ref[...])


def _proj_ln(acts, weight, x, g, b, tm, name, gates=None):
    M = x.shape[0]
    tm = min(tm, M)
    n_in = len(acts)
    gates = gates or [None] * n_in
    in_specs = [pl.BlockSpec((tm, a.shape[1]), lambda i: (i, 0)) for a in acts]
    w_args = []
    for r, a in enumerate(acts):
        w_arr, w_spec = _wspec(weight, (a.shape[1], D_MODEL), lambda i, r=r: (r, 0))
        w_args.append(w_arr)
        in_specs.append(w_spec)
    in_specs += [pl.BlockSpec((tm, D_MODEL), lambda i: (i, 0)),
                 pl.BlockSpec((1, D_MODEL), lambda i: (0, 0)),
                 pl.BlockSpec((1, D_MODEL), lambda i: (0, 0))]
    extra = []
    for gate in gates:
        if gate is not None:
            _, gain, _, gate_arr, blk = gate
            in_specs += [pl.BlockSpec((1, HEAD_W), lambda i: (0, 0)),
                         pl.BlockSpec((tm, MIX_W), lambda i, blk=blk: (i, blk))]
            extra += [gain.reshape(1, -1), gate_arr]
    return pl.pallas_call(
        functools.partial(_proj_ln_kernel, n_in=n_in,
                          gated=tuple(None if gt is None else (gt[0], gt[2]) for gt in gates)),
        out_shape=jax.ShapeDtypeStruct((M, D_MODEL), F32),
        grid=(M // tm,),
        in_specs=in_specs,
        out_specs=pl.BlockSpec((tm, D_MODEL), lambda i: (i, 0)),
        compiler_params=_cparams(1),
        name=name,
    )(*acts, *w_args, x, g.reshape(1, -1), b.reshape(1, -1), *extra)


Q_SCALE = DH_A ** -0.5 * math.log2(math.e)
VT_ROWS = HEAD_W + 16


def _ev_in_prompt_kernel(x_ref, w_ref, cos_ref, sin_ref, qt_ref, kb_ref, kt_ref, vt_ref, vr_ref, hb_ref):
    xb = x_ref[0].astype(BF16)
    cos, sin = cos_ref[...], sin_ref[...]
    proj = lambda lo, hi: jnp.dot(xb, w_ref[:, lo * MIX_W:hi * MIX_W], preferred_element_type=F32)
    hb_ref[0] = proj(3, 7)
    q = _rope(proj(0, 1), cos, sin, DH_A // 2)
    k = _rope(proj(1, 2), cos, sin, DH_A // 2)
    qt_ref[0] = (q * Q_SCALE).T.astype(BF16)
    kb_ref[0] = k.astype(BF16)
    kt_ref[0] = k.T
    v = proj(2, 3)
    tm = v.shape[0]
    vt = v.T.astype(BF16)
    ones = jnp.ones((VT_ROWS - HEAD_W, tm), BF16)
    for h in range(N_HEADS):
        vt_ref[0, h, :HEAD_W, :] = vt[h * HEAD_W:(h + 1) * HEAD_W]
        vt_ref[0, h, HEAD_W:, :] = ones
        vr_ref[0, pl.ds(h, tm, stride=N_HEADS), :] = _head(v, h)


def _ev_in_prompt(x3, w_in, cos_t, sin_t, tm):
    B, T, _ = x3.shape
    tm = min(tm, T)
    tab = pl.BlockSpec((tm, LANES), lambda b, i: (i, 0))
    tr = pl.BlockSpec((1, MIX_W, tm), lambda b, i: (b, 0, i))
    tshape = lambda dt: jax.ShapeDtypeStruct((B, MIX_W, T), dt)
    return pl.pallas_call(
        _ev_in_prompt_kernel,
        out_shape=(tshape(BF16), jax.ShapeDtypeStruct((B, T, MIX_W), BF16), tshape(F32),
                   jax.ShapeDtypeStruct((B, N_HEADS, VT_ROWS, T), BF16),
                   jax.ShapeDtypeStruct((B, T * N_HEADS, HEAD_W), F32),
                   jax.ShapeDtypeStruct((B, T, 4 * MIX_W), F32)),
        grid=(B, T // tm),
        in_specs=[pl.BlockSpec((1, tm, D_MODEL), lambda b, i: (b, i, 0)),
                  pl.BlockSpec(w_in.shape, lambda b, i: (0, 0)), tab, tab],
        out_specs=(tr, pl.BlockSpec((1, tm, MIX_W), lambda b, i: (b, i, 0)), tr,
                   pl.BlockSpec((1, N_HEADS, VT_ROWS, tm), lambda b, i: (b, 0, 0, i)),
                   pl.BlockSpec((1, tm * N_HEADS, HEAD_W), lambda b, i: (b, i, 0)),
                   pl.BlockSpec((1, tm, 4 * MIX_W), lambda b, i: (b, i, 0))),
        compiler_params=_cparams(2),
        name="ev_in_prompt",
    )(x3, w_in, cos_t, sin_t)


def _ev_prep_sample_kernel(qk_ref, cos_ref, sin_ref, q_ref, k_ref, kt_ref):
    cos, sin = cos_ref[...], sin_ref[...]
    k = _rope(qk_ref[:, MIX_W:], cos, sin, DH_A // 2)
    q_ref[...] = _rope(qk_ref[:, :MIX_W], cos, sin, DH_A // 2) * Q_SCALE
    k_ref[...] = k
    kt_ref[...] = k.T


def _ev_prep_sample(h, cos_t, sin_t):
    DB = h.shape[0]
    full = lambda shape: pl.BlockSpec(shape, lambda i: (0,) * len(shape))
    return pl.pallas_call(
        _ev_prep_sample_kernel,
        out_shape=(jax.ShapeDtypeStruct((DB, MIX_W), F32), jax.ShapeDtypeStruct((DB, MIX_W), F32),
                   jax.ShapeDtypeStruct((MIX_W, DB), F32)),
        grid=(1,),
        in_specs=[full((DB, 2 * MIX_W)), full((DB, LANES)), full((DB, LANES))],
        out_specs=(full((DB, MIX_W)), full((DB, MIX_W)), full((MIX_W, DB))),
        compiler_params=_cparams(1),
        name="ev_prep_sample",
    )(h, cos_t, sin_t)


def _lambda(lam_ref, lam_init):
    lp = lam_ref[...]
    s01 = jnp.sum(lp[0:1] * lp[1:2], axis=1, keepdims=True)
    s23 = jnp.sum(lp[2:3] * lp[3:4], axis=1, keepdims=True)
    return jnp.exp(s01) - jnp.exp(s23) + lam_init


def _dattn_kernel(lam_ref, g_ref, qt_ref, k_ref, vt_ref, o_ref, m_ref, acc_ref, *, tq, lam_init):
    i = pl.program_id(2)
    qt = qt_ref[0]
    sub = lax.broadcasted_iota(jnp.int32, qt.shape, 0)
    zero = jnp.zeros_like(qt)
    q_maps = [jnp.where(sub < DH_A, qt, zero), jnp.where(sub >= DH_A, qt, zero)]
    maps = range(2)
    m_ref[...] = jnp.full(m_ref.shape, NEG_INF, F32)
    acc_ref[...] = jnp.zeros(acc_ref.shape, F32)

    def steps(js, masked):
        ks, vts, ss = [], [], []
        for j in js:
            start = pl.multiple_of(j * tq, tq)
            ks.append(k_ref[0, pl.ds(start, tq), :])
            vts.append(vt_ref[0, 0, :, pl.ds(start, tq)])
        for kj in ks:
            ss.append([jnp.dot(kj, q_maps[c], preferred_element_type=F32) for c in maps])
        for n in range(len(js)):
            for c in maps:
                s = ss[n][c]
                if masked:
                    r = lax.broadcasted_iota(jnp.int32, s.shape, 0)
                    col = lax.broadcasted_iota(jnp.int32, s.shape, 1)
                    s = jnp.where(r <= col, s, NEG_INF)
                m_prev = m_ref[c]
                m_new = jnp.maximum(m_prev, jnp.max(s, axis=0, keepdims=True))
                alpha = jnp.exp2(m_prev - m_new)
                p = jnp.exp2(s - m_new)
                acc_ref[c] = alpha * acc_ref[c] + jnp.dot(vts[n], p.astype(BF16), preferred_element_type=F32)
                m_ref[c] = m_new

    def body(jj, carry):
        steps([2 * jj, 2 * jj + 1], False)
        return carry

    lax.fori_loop(0, lax.shift_right_logical(i, 1), body, 0)

    @pl.when((i & 1) == 1)
    def _():
        steps([i - 1], False)

    steps([i], True)
    lam = _lambda(lam_ref, lam_init)
    o = [acc_ref[c, :HEAD_W, :] / acc_ref[c, HEAD_W:HEAD_W + 1, :] for c in maps]
    d = (o[0] - lam * o[1]).T
    o_ref[0] = (_rmsnorm(d, g_ref[...]) * (1.0 - lam_init)).astype(BF16)


def _dattn_prompt(qt, kb, vt, lam_p, subln_g, lam_init, tq):
    B, T, _ = kb.shape
    tq = min(tq, T)
    return pl.pallas_call(
        functools.partial(_dattn_kernel, tq=tq, lam_init=lam_init),
        out_shape=jax.ShapeDtypeStruct((B, T, MIX_W), BF16),
        grid=(B, N_HEADS, T // tq),
        in_specs=[pl.BlockSpec((4, DH_A), lambda b, h, i: (0, 0)),
                  pl.BlockSpec((1, HEAD_W), lambda b, h, i: (0, 0)),
                  pl.BlockSpec((1, HEAD_W, tq), lambda b, h, i: (b, h, i)),
                  pl.BlockSpec((1, T, HEAD_W), lambda b, h, i: (b, 0, h)),
                  pl.BlockSpec((1, 1, VT_ROWS, T), lambda b, h, i: (b, h, 0, 0))],
        out_specs=pl.BlockSpec((1, tq, HEAD_W), lambda b, h, i: (b, i, h)),
        scratch_shapes=[pltpu.VMEM((2, 1, tq), F32), pltpu.VMEM((2, VT_ROWS, tq), F32)],
        compiler_params=_cparams(3),
        name="dattn_prompt",
    )(lam_p, subln_g.reshape(1, -1), qt, kb, vt)


def _head(x, h):
    return x[:, h * HEAD_W:(h + 1) * HEAD_W]


def _gla_chunk(seqs, st_ref, tril_mask):
    seqs = [tuple(x.astype(BF16) for x in s[:4]) + (s[4],) for s in seqs]
    chains = [(n, h) for n in range(len(seqs)) for h in range(N_HEADS)]
    part = lambda n, i, h: _head(seqs[n][i], h)
    sts = [st_ref[N_HEADS * n + h] for n, h in chains]
    inter = [_bdot_nt(part(n, 0, h), sts[c]) for c, (n, h) in enumerate(chains)]
    upd = [_bdot_tn(part(n, 3, h), part(n, 2, h)) for n, h in chains]
    attn = [jnp.where(tril_mask, _bdot_nt(part(n, 0, h), part(n, 1, h)), 0.0) for n, h in chains]
    intra = [_bdot(attn[c], part(n, 3, h)) for c, (n, h) in enumerate(chains)]
    for c, (n, h) in enumerate(chains):
        dec = seqs[n][4]
        st_ref[N_HEADS * n + h] = sts[c] * (dec[h] if isinstance(dec, (list, tuple)) else _head(dec, h)) + upd[c]
    outs = [intra[c] + inter[c] for c in range(len(chains))]
    return [outs[N_HEADS * n:N_HEADS * (n + 1)] for n in range(len(seqs))]


def _hgrn_kernel(q_ref, f_ref, i_ref, lb_ref, o_ref, s_ref, st_ref, *, tt, L, n_seq):
    t = pl.program_id(0)

    @pl.when(t == 0)
    def _():
        st_ref[...] = jnp.zeros(st_ref.shape, F32)

    tril_mask = _tri(L, True)
    tril = tril_mask.astype(BF16)

    def body(c, carry):
        rows = pl.ds(pl.multiple_of(c * L, L), L)
        lb = lb_ref[...]
        seqs = []
        for n in range(n_seq):
            f = lb + (1.0 - lb) * _sigmoid(f_ref[n, rows, :])
            b = _cumsum_rows(tril, jnp.log(f))
            b_end = b[L - 1:L, :]
            k = 1.0 - f
            q_in = _silu(q_ref[n, rows, :]) * jnp.exp(b)
            seqs.append((q_in, k * jnp.exp(-b), k * jnp.exp(b_end - b), i_ref[n, rows, :], jnp.exp(b_end)))
        for n, o in enumerate(_gla_chunk(seqs, st_ref, tril_mask)):
            o_ref[n, rows, :] = jnp.concatenate(o, axis=1)
        return carry

    lax.fori_loop(0, tt // L, body, 0, unroll=2)

    @pl.when(t == pl.num_programs(0) - 1)
    def _():
        s_ref[...] = st_ref[...]


def _hgrn_prompt(h3, lb, tt):
    B, T, _ = h3.shape
    tt = min(tt, T)
    L = math.gcd(T, CHUNK)
    col = lambda c: pl.BlockSpec((B, tt, MIX_W), lambda t: (0, t, c))
    o, st = pl.pallas_call(
        functools.partial(_hgrn_kernel, tt=tt, L=L, n_seq=B),
        out_shape=(jax.ShapeDtypeStruct((B, T, MIX_W), F32),
                   jax.ShapeDtypeStruct((B * N_HEADS, HEAD_W, HEAD_W), F32)),
        grid=(T // tt,),
        in_specs=[col(0), col(1), col(2), pl.BlockSpec((1, MIX_W), lambda t: (0, 0))],
        out_specs=(pl.BlockSpec((B, tt, MIX_W), lambda t: (0, t, 0)),
                   pl.BlockSpec((B * N_HEADS, HEAD_W, HEAD_W), lambda t: (0, 0, 0))),
        scratch_shapes=[pltpu.VMEM((B * N_HEADS, HEAD_W, HEAD_W), F32)],
        compiler_params=_cparams(1),
        name="hgrn_prompt",
    )(h3, h3, h3, lb.reshape(1, -1))
    return o, st.reshape(B, N_HEADS, HEAD_W, HEAD_W)


def _log_gamma(h):
    return float(np.log(1.0 - 2.0 ** (-5.0 - h)))


def _od_kernel(cq_ref, ck_ref, cv_ref, dq_ref, dk_ref, dv_ref, gc_ref, gr_ref,
               bc_ref, br_ref, cos_ref, sin_ref,
               oc_ref, od_ref, sc_ref, dc_ref, dn_ref, dm_ref,
               st_ref, ct_ref, n_ref, m_ref, *, tt, L):
    t = pl.program_id(1)

    @pl.when(t == 0)
    def _():
        st_ref[...] = jnp.zeros(st_ref.shape, F32)
        ct_ref[...] = jnp.zeros(ct_ref.shape, F32)
        n_ref[...] = jnp.zeros(n_ref.shape, F32)
        m_ref[...] = jnp.zeros(m_ref.shape, F32)

    tril_mask = _tri(L, True)
    tril = tril_mask.astype(BF16)
    triu = _tri(L, False).astype(BF16)
    scale = HEAD_W ** -0.5
    heads = range(N_HEADS)
    pos1 = (lax.broadcasted_iota(jnp.int32, (L, HEAD_W), 0) + 1).astype(F32)
    ret_b = jnp.concatenate([pos1 * _log_gamma(h) for h in heads], axis=1)
    ret_b_end = jnp.concatenate([jnp.full((L, HEAD_W), L * _log_gamma(h), F32) for h in heads], axis=1)
    ret_q_dec, ret_k_dec, ret_k_end = jnp.exp(ret_b), jnp.exp(-ret_b), jnp.exp(ret_b_end - ret_b)
    ret_dec = [math.exp(L * _log_gamma(h)) for h in heads]

    def body(c, carry):
        rows = pl.ds(pl.multiple_of(c * L, L), L)
        cos, sin = cos_ref[rows, :], sin_ref[rows, :]
        q = _rope(cq_ref[0, rows, :], cos, sin, HEAD_W // 2)
        k = _rope(ck_ref[0, rows, :], cos, sin, HEAD_W // 2) * scale
        o, = _gla_chunk([(q * ret_q_dec, k * ret_k_dec, k * ret_k_end, cv_ref[0, rows, :], ret_dec)],
                        st_ref, tril_mask)
        oc_ref[0, rows, :] = jnp.concatenate(o, axis=1)
        g_c = gc_ref[0, rows, :] + bc_ref[...]
        g_r = gr_ref[0, c] + br_ref[...]
        lf_c = _log_sigmoid(g_c)
        b_c = _cumsum_rows(tril, jnp.concatenate(
            [jnp.broadcast_to(lf_c[:, N_HEADS + h:N_HEADS + h + 1], (L, HEAD_W)) for h in heads], axis=1))
        b_r = _cumsum_lanes(_log_sigmoid(g_r), triu)
        dq = (dq_ref[0, rows, :] * scale)
        dk = dk_ref[0, rows, :]
        dqb, dkb, dvb = dq.astype(BF16), dk.astype(BF16), dv_ref[0, rows, :].astype(BF16)
        cts = [ct_ref[h] for h in heads]
        q_c = [_bdot_nt(_head(dqb, h), cts[h]) for h in heads]
        q_k = [_bdot_nt(_head(dqb, h), _head(dkb, h)) for h in heads]
        m_prev = [m_ref[h:h + 1, 0:1] for h in heads]
        bc1, m_t, w_mat = [], [], []
        for h in heads:
            bc = _head(b_c, h)[:, :L]
            dm = jnp.where(tril_mask, bc - b_r[N_HEADS + h:N_HEADS + h + 1, :] + g_r[h:h + 1, :], NEG_INF)
            bc1.append(bc[:, 0:1])
            m_t.append(jnp.maximum(bc1[h] + m_prev[h], jnp.max(dm, axis=1, keepdims=True)))
            w_mat.append(jnp.exp(dm - m_t[h]) * q_k[h])
        w_v = [_bdot(w_mat[h], _head(dvb, h)) for h in heads]
        kws, c_scales = [], []
        for h in heads:
            m_new = m_t[h][L - 1:L, :]
            b_last = bc1[h][L - 1:L, :]
            c_scales.append(jnp.exp(b_last + m_prev[h] - m_new))
            kws.append(_head(dk, h) * jnp.exp(b_last - bc1[h] + g_c[:, h:h + 1] - m_new))
            m_ref[h:h + 1, :] = jnp.broadcast_to(m_new, (1, LANES))
        c_upd = [_bdot_tn(_head(dvb, h), kws[h]) for h in heads]
        hs = []
        for h in heads:
            n_row = n_ref[h:h + 1, :]
            inter = jnp.exp(bc1[h] + m_prev[h] - m_t[h])
            num = inter * q_c[h] + w_v[h]
            den = (inter * jnp.sum(_head(dq, h) * n_row, axis=1, keepdims=True)
                   + jnp.sum(w_mat[h], axis=1, keepdims=True))
            hs.append(num / jnp.maximum(jnp.abs(den), jnp.exp(-m_t[h])))
            ct_ref[h] = c_scales[h] * cts[h] + c_upd[h]
            n_ref[h:h + 1, :] = c_scales[h] * n_row + jnp.sum(kws[h], axis=0, keepdims=True)
        od_ref[0, rows, :] = jnp.concatenate(hs, axis=1)
        return carry

    lax.fori_loop(0, tt // L, body, 0)

    @pl.when(t == pl.num_programs(1) - 1)
    def _():
        sc_ref[0] = st_ref[...]
        dc_ref[0] = ct_ref[...]
        dn_ref[0] = n_ref[...]
        dm_ref[0] = m_ref[...]


def _od_in_kernel(x_ref, w_ref, wg_ref, h_ref, g_ref, xb_ref):
    @pl.when(pl.program_id(1) == 0)
    def _():
        xb_ref[...] = x_ref[...].astype(BF16)

    xb = xb_ref[...]
    h_ref[...] = jnp.dot(xb, w_ref[...], preferred_element_type=F32)

    @pl.when(pl.program_id(1) == 0)
    def _():
        g_ref[...] = jnp.dot(xb, wg_ref[...], preferred_element_type=F32)


def _od_in_prompt(x, w_in, w_gate, tm, tn):
    M, K = x.shape
    N = w_in.shape[1]
    tm = min(tm, M)
    return pl.pallas_call(
        _od_in_kernel,
        out_shape=(jax.ShapeDtypeStruct((M, N), F32), jax.ShapeDtypeStruct((M, LANES), F32)),
        grid=(M // tm, N // tn),
        in_specs=[pl.BlockSpec((tm, K), lambda i, j: (i, 0)),
                  pl.BlockSpec((K, tn), lambda i, j: (0, j)),
                  pl.BlockSpec((K, LANES), lambda i, j: (0, 0))],
        out_specs=(pl.BlockSpec((tm, tn), lambda i, j: (i, j)), pl.BlockSpec((tm, LANES), lambda i, j: (i, 0))),
        scratch_shapes=[pltpu.VMEM((tm, K), BF16)],
        compiler_params=_cparams(2),
        name="od_in_prompt",
    )(x, w_in, w_gate)


def _od_prompt(h3, gates, b_if, cos_t, sin_t, tt):
    B, T, _ = h3.shape
    tt = min(tt, T)
    L = math.gcd(T, OD_CHUNK)
    gates_r = gates[:, :, :2 * N_HEADS].reshape(B, T // L, L, 2 * N_HEADS).transpose(0, 1, 3, 2)
    b_pad = jnp.zeros((1, LANES), F32).at[0, :2 * N_HEADS].set(b_if)
    col = lambda c: pl.BlockSpec((1, tt, MIX_W), lambda b, t: (b, t, c))
    tab = pl.BlockSpec((tt, LANES), lambda b, t: (t, 0))
    mat_state = pl.BlockSpec((1, N_HEADS, HEAD_W, HEAD_W), lambda b, t: (b, 0, 0, 0))
    row_state = pl.BlockSpec((1, SUBLANES, LANES), lambda b, t: (b, 0, 0))
    out_blk = pl.BlockSpec((1, tt, MIX_W), lambda b, t: (b, t, 0))
    return pl.pallas_call(
        functools.partial(_od_kernel, tt=tt, L=L),
        out_shape=(jax.ShapeDtypeStruct((B, T, MIX_W), F32), jax.ShapeDtypeStruct((B, T, MIX_W), F32),
                   jax.ShapeDtypeStruct((B, N_HEADS, HEAD_W, HEAD_W), F32),
                   jax.ShapeDtypeStruct((B, N_HEADS, HEAD_W, HEAD_W), F32),
                   jax.ShapeDtypeStruct((B, SUBLANES, LANES), F32),
                   jax.ShapeDtypeStruct((B, SUBLANES, LANES), F32)),
        grid=(B, T // tt),
        in_specs=[col(0), col(1), col(2), col(4), col(5), col(6),
                  pl.BlockSpec((1, tt, LANES), lambda b, t: (b, t, 0)),
                  pl.BlockSpec((1, tt // L, 2 * N_HEADS, L), lambda b, t: (b, t, 0, 0)),
                  pl.BlockSpec((1, LANES), lambda b, t: (0, 0)),
                  pl.BlockSpec((2 * N_HEADS, 1), lambda b, t: (0, 0)),
                  tab, tab],
        out_specs=(out_blk, out_blk, mat_state, mat_state, row_state, row_state),
        scratch_shapes=[pltpu.VMEM((N_HEADS, HEAD_W, HEAD_W), F32), pltpu.VMEM((N_HEADS, HEAD_W, HEAD_W), F32),
                        pltpu.VMEM((SUBLANES, LANES), F32), pltpu.VMEM((SUBLANES, LANES), F32)],
        compiler_params=_cparams(2),
        name="od_prompt",
    )(h3, h3, h3, h3, h3, h3, gates, gates_r, b_pad, b_if.reshape(-1, 1), cos_t, sin_t)


def _xattn_kernel(x_ref, wq_ref, mk_ref, mv_ref, wo_ref, g_ref, b_ref, o_ref):
    x = x_ref[...]
    q = _bdot(x, wq_ref[...])
    qb = (q * (DH_X ** -0.5)).astype(BF16)
    heads = range(N_HEADS)
    cols = [slice(h * DH_X, (h + 1) * DH_X) for h in heads]
    s = [lax.dot_general(qb[:, cols[h]], mk_ref[0, :, cols[h]], NT_DIMS, preferred_element_type=F32) for h in heads]
    p = [jnp.exp(s[h] - jnp.max(s[h], axis=1, keepdims=True)) for h in heads]
    l = [jnp.sum(p[h], axis=1, keepdims=True) for h in heads]
    pv = [jnp.dot(p[h].astype(BF16), mv_ref[0, :, cols[h]], preferred_element_type=F32) for h in heads]
    outs = [(pv[h] / l[h]).astype(BF16) for h in heads]
    y = jnp.dot(jnp.concatenate(outs, axis=1), wo_ref[...], preferred_element_type=F32)
    o_ref[...] = _layernorm(DN_ALPHA * x + y, g_ref[...], b_ref[...])


def _xattn_prompt(x, wq, mk, mv, wo, g, b, T, tm):
    M = x.shape[0]
    tm = min(tm, T)
    n_mem = mk.shape[1]
    per_b = T // tm
    full = lambda shape: pl.BlockSpec(shape, lambda i: (0,) * len(shape))
    mem = pl.BlockSpec((1, n_mem, D_MODEL), lambda i: (i // per_b, 0, 0))
    wq, wq_spec = _wspec(wq, (D_MODEL, D_MODEL), lambda i: (0, 0))
    wo, wo_spec = _wspec(wo, (D_MODEL, D_MODEL), lambda i: (0, 0))
    return pl.pallas_call(
        _xattn_kernel,
        out_shape=jax.ShapeDtypeStruct((M, D_MODEL), F32),
        grid=(M // tm,),
        in_specs=[pl.BlockSpec((tm, D_MODEL), lambda i: (i, 0)), wq_spec, mem, mem,
                  wo_spec, full((1, D_MODEL)), full((1, D_MODEL))],
        out_specs=pl.BlockSpec((tm, D_MODEL), lambda i: (i, 0)),
        compiler_params=_cparams(1),
        name="xattn_prompt",
    )(x, wq, mk, mv, wo, g.reshape(1, -1), b.reshape(1, -1))


def _ffn_kernel(x_ref, wu_ref, wg_ref, cw_ref, cb_ref, wd_ref, g_ref, b_ref, o_ref, tail_ref,
                xb_ref, acc_ref, stage_ref, carry_ref, *, tm, per_b, single):
    i, j = pl.program_id(0), pl.program_id(1)

    @pl.when(j == 0)
    def _():
        xb_ref[...] = x_ref[...].astype(BF16)

    @pl.when(i % per_b == 0)
    def _():
        carry_ref[j] = jnp.zeros(carry_ref.shape[1:], F32)

    xb = xb_ref[...]
    stage_ref[0:SUBLANES, :] = carry_ref[j]
    stage_ref[SUBLANES:, :] = jnp.dot(xb, wu_ref[...], preferred_element_type=F32)
    last = stage_ref[tm:tm + SUBLANES, :]
    carry_ref[j] = last
    tail_ref[0] = last
    conv = (cb_ref[...] + cw_ref[0:1, :] * stage_ref[SUBLANES - 2:SUBLANES - 2 + tm, :]
            + cw_ref[1:2, :] * stage_ref[SUBLANES - 1:SUBLANES - 1 + tm, :]
            + cw_ref[2:3, :] * stage_ref[SUBLANES:, :])
    gate = jnp.dot(xb, wg_ref[...], preferred_element_type=F32)
    y = _bdot(jax.nn.gelu(conv) * gate, wd_ref[...])
    if single:
        o_ref[...] = _layernorm(DN_ALPHA * x_ref[...] + y, g_ref[...], b_ref[...])
        return

    @pl.when(j == 0)
    def _():
        acc_ref[...] = y

    @pl.when(j > 0)
    def _():
        acc_ref[...] = acc_ref[...] + y

    @pl.when(j == pl.num_programs(1) - 1)
    def _():
        o_ref[...] = _layernorm(DN_ALPHA * x_ref[...] + acc_ref[...], g_ref[...], b_ref[...])


def _ffn_prompt(x, w_up, conv_w, conv_b, w_down, g, b, T, tm, tf):
    M = x.shape[0]
    d_ff = _wshape(w_down)[0]
    tm = min(tm, T)
    nf = d_ff // tf
    per_b = T // tm
    wmode = dict(pipeline_mode=pl.Buffered(1)) if nf == 1 else {}
    w_up_arr, up_spec = _wspec(w_up, (D_MODEL, tf), lambda i, j: (0, j), **wmode)
    _, gate_spec = _wspec(w_up, (D_MODEL, tf), lambda i, j: (0, nf + j), **wmode)
    w_down_arr, down_spec = _wspec(w_down, (tf, D_MODEL), lambda i, j: (j, 0), **wmode)
    return pl.pallas_call(
        functools.partial(_ffn_kernel, tm=tm, per_b=per_b, single=nf == 1),
        out_shape=(jax.ShapeDtypeStruct((M, D_MODEL), F32),
                   jax.ShapeDtypeStruct((M // tm, SUBLANES, d_ff), F32)),
        grid=(M // tm, nf),
        in_specs=[pl.BlockSpec((tm, D_MODEL), lambda i, j: (i, 0)),
                  up_spec, gate_spec,
                  pl.BlockSpec((3, tf), lambda i, j: (0, j)),
                  pl.BlockSpec((1, tf), lambda i, j: (0, j)),
                  down_spec,
                  pl.BlockSpec((1, D_MODEL), lambda i, j: (0, 0)),
                  pl.BlockSpec((1, D_MODEL), lambda i, j: (0, 0))],
        out_specs=(pl.BlockSpec((tm, D_MODEL), lambda i, j: (i, 0)),
                   pl.BlockSpec((1, SUBLANES, tf), lambda i, j: (i, 0, j))),
        scratch_shapes=[pltpu.VMEM((tm, D_MODEL), BF16), pltpu.VMEM((tm, D_MODEL), F32),
                        pltpu.VMEM((tm + SUBLANES, tf), F32), pltpu.VMEM((nf, SUBLANES, tf), F32)],
        compiler_params=_cparams(2),
        name="ffn_prompt",
    )(x, w_up_arr, w_up_arr, conv_w, conv_b.reshape(1, -1), w_down_arr, g.reshape(1, -1), b.reshape(1, -1))


def _dattn_sample_kernel(*refs, n_pages, page, lam_init, rows):
    lam_ref, g_ref, q_ref, kn_ref, vn_ref = refs[1:6]
    o_ref = refs[6 + 2 * rows * n_pages]
    for r in range(rows):
        k_refs = refs[6 + r * n_pages:6 + (r + 1) * n_pages]
        v_refs = refs[6 + (rows + r) * n_pages:6 + (rows + r + 1) * n_pages]
        _dattn_sample_row(lam_ref, g_ref, q_ref.at[r], kn_ref.at[r], vn_ref.at[r], k_refs, v_refs, o_ref.at[r],
                          n_pages=n_pages, page=page, lam_init=lam_init)


def _dattn_sample_row(lam_ref, g_ref, q_ref, kn_ref, vn_ref, k_refs, v_refs, o_ref, *, n_pages, page, lam_init):
    n_rows = 2 * N_HEADS
    q = q_ref[...]
    lane = lax.broadcasted_iota(jnp.int32, (n_rows, MIX_W), 1)
    row = lax.broadcasted_iota(jnp.int32, (n_rows, MIX_W), 0)
    qbd = jnp.where(lax.shift_right_logical(lane, 6) == row, jnp.broadcast_to(q, (n_rows, MIX_W)), 0.0)
    s_new = jnp.sum(qbd * kn_ref[...], axis=1, keepdims=True)
    scores = [_bdot(qbd, k_refs[p][0]) for p in range(n_pages)]
    m = s_new
    for s in scores:
        m = jnp.maximum(m, jnp.max(s, axis=1, keepdims=True))
    p_new = jnp.exp2(s_new - m)
    l = p_new
    vn = vn_ref[...]
    accs = [p_new * vn[:, h * HEAD_W:(h + 1) * HEAD_W] for h in range(N_HEADS)]
    for p in range(n_pages):
        pr = jnp.exp2(scores[p] - m)
        l = l + jnp.sum(pr, axis=1, keepdims=True)
        for h in range(N_HEADS):
            accs[h] = accs[h] + _bdot(pr, v_refs[p][0, pl.ds(h, page, stride=N_HEADS), :])
    lam = _lambda(lam_ref, lam_init)
    outs = []
    for h in range(N_HEADS):
        o = accs[h] / l
        d = o[2 * h:2 * h + 1, :] - lam * o[2 * h + 1:2 * h + 2, :]
        outs.append(_rmsnorm(d, g_ref[...]) * (1.0 - lam_init))
    o_ref[...] = jnp.concatenate(outs, axis=1)


def _dattn_sample(q, k_new, v_new, kt_pool, v_pool, pool_base, page_table, lam_p, subln_g, lam_init):
    DB = q.shape[0]
    n_pages = page_table.shape[1]
    page = kt_pool.shape[2]
    rows = math.gcd(DB, 2)
    tok = pl.BlockSpec((rows, 1, MIX_W), lambda b, pt: (b, 0, 0))

    def k_spec(r, p):
        return pl.BlockSpec((1, MIX_W, page), lambda b, pt: (pool_base + pt[rows * b + r, p], 0, 0))

    def v_spec(r, p):
        return pl.BlockSpec((1, page * N_HEADS, HEAD_W), lambda b, pt: (pool_base + pt[rows * b + r, p], 0, 0))

    pages = [(r, p) for r in range(rows) for p in range(n_pages)]
    grid_spec = pltpu.PrefetchScalarGridSpec(
        num_scalar_prefetch=1,
        grid=(DB // rows,),
        in_specs=[pl.BlockSpec((4, DH_A), lambda b, pt: (0, 0)),
                  pl.BlockSpec((1, HEAD_W), lambda b, pt: (0, 0)), tok, tok, tok]
                 + [k_spec(r, p) for r, p in pages] + [v_spec(r, p) for r, p in pages],
        out_specs=tok,
    )
    out = pl.pallas_call(
        functools.partial(_dattn_sample_kernel, n_pages=n_pages, page=page, lam_init=lam_init, rows=rows),
        out_shape=jax.ShapeDtypeStruct((DB, 1, MIX_W), F32),
        grid_spec=grid_spec,
        compiler_params=_cparams(1),
        name="dattn_sample",
    )(page_table, lam_p, subln_g.reshape(1, -1), q.reshape(DB, 1, MIX_W), k_new.reshape(DB, 1, MIX_W),
      v_new.reshape(DB, 1, MIX_W), *([kt_pool] * len(pages)), *([v_pool] * len(pages)))
    return out.reshape(DB, MIX_W)


def _hgrn_step_kernel(h_ref, s0_ref, lb_ref, ng_ref, o_ref, s_ref, *, bt):
    def body(i, carry):
        row = h_ref[i]
        heads = range(N_HEADS)
        blk = lambda n: row[:, n * MIX_W:(n + 1) * MIX_W]
        lb = lb_ref[...]
        f = lb + (1.0 - lb) * _sigmoid(blk(4))
        q, v = _silu(blk(3)), blk(5)
        f_col = [_col(_head(f, h)) for h in heads]
        s_new = [f_col[h] * s0_ref[i, h] + (1.0 - f_col[h]) * _head(v, h) for h in heads]
        for h in heads:
            s_ref[i, h] = s_new[h]
        o = [_bdot(jnp.broadcast_to(_head(q, h), (SUBLANES, HEAD_W)), s_new[h])[0:1, :] for h in heads]
        o = jnp.concatenate([_rmsnorm(o[h], ng_ref[...]) for h in heads], axis=1)
        o_ref[i] = o * _silu(blk(6))
        return carry

    lax.fori_loop(0, bt, body, 0, unroll=2)


def _hgrn_step(h, s0, lb, norm_g, bt):
    DB, W = h.shape
    bt = min(bt, DB)
    state = pl.BlockSpec((bt, N_HEADS, HEAD_W, HEAD_W), lambda i: (i, 0, 0, 0))
    o, s = pl.pallas_call(
        functools.partial(_hgrn_step_kernel, bt=bt),
        out_shape=(jax.ShapeDtypeStruct((DB, 1, MIX_W), F32), jax.ShapeDtypeStruct(s0.shape, F32)),
        grid=(DB // bt,),
        in_specs=[pl.BlockSpec((bt, 1, W), lambda i: (i, 0, 0)), state,
                  pl.BlockSpec((1, MIX_W), lambda i: (0, 0)), pl.BlockSpec((1, HEAD_W), lambda i: (0, 0))],
        out_specs=(pl.BlockSpec((bt, 1, MIX_W), lambda i: (i, 0, 0)), state),
        compiler_params=_cparams(1),
        name="hgrn_step",
    )(h.reshape(DB, 1, W), s0, lb.reshape(1, -1), norm_g.reshape(1, -1))
    return o.reshape(DB, MIX_W), s


def _od_step_kernel(h_ref, gt_ref, bif_ref, cos_ref, sin_ref, sc0_ref, dc0_ref, dn0_ref, dm0_ref, cng_ref, dng_ref,
                    oc_ref, od_ref, sc_ref, dc_ref, dn_ref, dm_ref, *, bt):
    scale = HEAD_W ** -0.5

    def body(i, carry):
        row = h_ref[i]
        cos, sin = cos_ref[...], sin_ref[...]
        gates = gt_ref[i] + bif_ref[...]
        m_row = dm0_ref[i]
        heads = range(N_HEADS)
        blk = lambda n: row[:, n * MIX_W:(n + 1) * MIX_W]
        q = _rope(blk(0), cos, sin, HEAD_W // 2)
        k = _rope(blk(1), cos, sin, HEAD_W // 2) * scale
        cv, dk, dv, dq = blk(2), blk(5), blk(6), blk(4) * scale
        lf = _log_sigmoid(gates)
        m_new = [jnp.maximum(lf[:, N_HEADS + h:N_HEADS + h + 1] + m_row[:, h:h + 1], gates[:, h:h + 1])
                 for h in heads]
        c_scale = [jnp.exp(lf[:, N_HEADS + h:N_HEADS + h + 1] + m_row[:, h:h + 1] - m_new[h]) for h in heads]
        kw = [_head(dk, h) * jnp.exp(gates[:, h:h + 1] - m_new[h]) for h in heads]
        k_col = [_col(_head(k, h)) for h in heads]
        kw_col = [_col(kw[h]) for h in heads]
        s_new = [math.exp(_log_gamma(h)) * sc0_ref[i, h] + k_col[h] * _head(cv, h) for h in heads]
        c_new = [c_scale[h] * dc0_ref[i, h] + kw_col[h] * _head(dv, h) for h in heads]
        n_new = [c_scale[h] * dn0_ref[i, h:h + 1, :] + kw[h] for h in heads]
        for h in heads:
            sc_ref[i, h] = s_new[h]
            dc_ref[i, h] = c_new[h]
            dn_ref[i, h:h + 1, :] = n_new[h]
        rows8 = lambda x: jnp.broadcast_to(x, (SUBLANES, HEAD_W))
        o = [_bdot(rows8(_head(q, h)), s_new[h])[0:1, :] for h in heads]
        num = [_bdot(rows8(_head(dq, h)), c_new[h])[0:1, :] for h in heads]
        den = [jnp.sum(_head(dq, h) * n_new[h], axis=1, keepdims=True) for h in heads]
        hh = [num[h] / jnp.maximum(jnp.abs(den[h]), jnp.exp(-m_new[h])) for h in heads]
        oc = jnp.concatenate([_groupnorm(o[h], cng_ref[...]) for h in heads], axis=1)
        od = jnp.concatenate([_groupnorm(hh[h], dng_ref[...]) for h in heads], axis=1)
        oc_ref[i] = oc * _silu(blk(3))
        od_ref[i] = od * _sigmoid(blk(7))
        dm_ref[i] = jnp.concatenate(m_new, axis=1)
        return carry

    lax.fori_loop(0, bt, body, 0)


def _od_step(h, gates, b_if, cos_t, sin_t, sc0, dc0, dn0, dm0, c_norm_g, d_norm_g, bt):
    DB, W = h.shape
    bt = min(bt, DB)
    mat = pl.BlockSpec((bt, N_HEADS, HEAD_W, HEAD_W), lambda i: (i, 0, 0, 0))
    nblk = pl.BlockSpec((bt, N_HEADS, HEAD_W), lambda i: (i, 0, 0))
    mblk = pl.BlockSpec((bt, 1, N_HEADS), lambda i: (i, 0, 0))
    vec = pl.BlockSpec((1, LANES), lambda i: (0, 0))
    out = pl.BlockSpec((bt, 1, MIX_W), lambda i: (i, 0, 0))
    b_pad = jnp.zeros((1, LANES), F32).at[0, :2 * N_HEADS].set(b_if)
    oc, od, sc, dc, dn, dm = pl.pallas_call(
        functools.partial(_od_step_kernel, bt=bt),
        out_shape=(jax.ShapeDtypeStruct((DB, 1, MIX_W), F32), jax.ShapeDtypeStruct((DB, 1, MIX_W), F32),
                   jax.ShapeDtypeStruct(sc0.shape, F32), jax.ShapeDtypeStruct(dc0.shape, F32),
                   jax.ShapeDtypeStruct(dn0.shape, F32), jax.ShapeDtypeStruct((DB, 1, N_HEADS), F32)),
        grid=(DB // bt,),
        in_specs=[pl.BlockSpec((bt, 1, W), lambda i: (i, 0, 0)),
                  pl.BlockSpec((bt, 1, LANES), lambda i: (i, 0, 0)), vec, vec, vec,
                  mat, mat, nblk, mblk, vec, vec],
        out_specs=(out, out, mat, mat, nblk, mblk),
        compiler_params=_cparams(1),
        name="od_step",
    )(h.reshape(DB, 1, W), gates.reshape(DB, 1, LANES), b_pad, cos_t, sin_t, sc0, dc0, dn0,
      dm0.reshape(DB, 1, N_HEADS), c_norm_g.reshape(1, -1), d_norm_g.reshape(1, -1))
    return oc.reshape(DB, MIX_W), od.reshape(DB, MIX_W), sc, dc, dn, dm.reshape(DB, N_HEADS)


X_HALVES = DH_X // LANES
X_ROWS = N_HEADS * X_HALVES


def _xattn_sample_kernel(q_ref, mk_ref, mv_ref, o_ref, *, xb):
    for t in range(xb):
        _xattn_sample_row(q_ref.at[t], mk_ref.at[t], mv_ref.at[t], o_ref.at[t])


def _xattn_sample_row(q_ref, mk_ref, mv_ref, o_ref):
    q = q_ref[...] * (DH_X ** -0.5)
    n_cols = mk_ref.shape[0]
    q2 = jnp.concatenate([q[:, h * DH_X + c * LANES:h * DH_X + (c + 1) * LANES]
                          for c in range(X_HALVES) for h in range(N_HEADS)], axis=0)
    s2 = _bdot_nt(q2, mk_ref[...])
    row = lax.broadcasted_iota(jnp.int32, (X_ROWS, n_cols), 0)
    col = lax.broadcasted_iota(jnp.int32, (X_ROWS, n_cols), 1)
    own = (col & (X_ROWS - 1)) == row
    s2 = jnp.where(own, s2, 0.0)
    s4 = s2[:N_HEADS] + pltpu.roll(s2, n_cols - N_HEADS, 1)[N_HEADS:]
    own4 = own[:N_HEADS]
    s4 = jnp.where(own4, s4, NEG_INF)
    p4 = jnp.exp(s4 - jnp.max(s4, axis=1, keepdims=True))
    l4 = jnp.sum(p4, axis=1, keepdims=True)
    p4 = p4 / l4
    p2 = jnp.concatenate([p4, pltpu.roll(p4, N_HEADS, 1)], axis=0)
    o2 = _bdot(p2, mv_ref[...])
    o_ref[...] = jnp.concatenate([o2[c * N_HEADS + h:c * N_HEADS + h + 1, :]
                                  for h in range(N_HEADS) for c in range(X_HALVES)], axis=1)


def _xattn_sample(q, mk, mv, base):
    DB = q.shape[0]
    n_rows = mk.shape[1]
    xb = math.gcd(DB, 4)
    tok = pl.BlockSpec((xb, 1, D_MODEL), lambda b: (b, 0, 0))
    mem = pl.BlockSpec((xb, n_rows, LANES), lambda b: (base // xb + b, 0, 0))
    assert base % xb == 0
    out = pl.pallas_call(
        functools.partial(_xattn_sample_kernel, xb=xb),
        out_shape=jax.ShapeDtypeStruct((DB, 1, D_MODEL), F32),
        grid=(DB // xb,),
        in_specs=[tok, mem, mem],
        out_specs=tok,
        compiler_params=_cparams(1),
        name="xattn_sample",
    )(q.reshape(DB, 1, D_MODEL), mk, mv)
    return out.reshape(DB, D_MODEL)


def _mem_rows(cache):
    n_l, DB, n_mem = cache.shape[:3]
    c = cache.reshape(n_l, DB, n_mem, N_HEADS, X_HALVES, LANES).transpose(0, 1, 2, 4, 3, 5)
    return c.reshape(n_l * DB, n_mem * X_ROWS, LANES)


def _ffn_sample_kernel(x_ref, up_ref, gate_ref, b0_ref, b1_ref, cw_ref, cb_ref, wd_ref, g_ref, b_ref, o_ref, acc_ref):
    j = pl.program_id(0)
    conv = (cb_ref[...] + cw_ref[0:1, :] * b0_ref[...] + cw_ref[1:2, :] * b1_ref[...]
            + cw_ref[2:3, :] * up_ref[...])
    y = _bdot(jax.nn.gelu(conv) * gate_ref[...], wd_ref[...])

    @pl.when(j == 0)
    def _():
        acc_ref[...] = y

    @pl.when(j > 0)
    def _():
        acc_ref[...] = acc_ref[...] + y

    @pl.when(j == pl.num_programs(0) - 1)
    def _():
        o_ref[...] = _layernorm(DN_ALPHA * x_ref[...] + acc_ref[...], g_ref[...], b_ref[...])


def _ffn_sample(x, ug, buf0, buf1, conv_w, conv_b, w_down, g, b, tf):
    DB = x.shape[0]
    d_ff = _wshape(w_down)[0]
    nf = d_ff // tf
    ff = lambda off: pl.BlockSpec((DB, tf), lambda j: (0, off + j))
    full = lambda shape: pl.BlockSpec(shape, lambda j: (0,) * len(shape))
    w_down, down_spec = _wspec(w_down, (tf, D_MODEL), lambda j: (j, 0))
    return pl.pallas_call(
        _ffn_sample_kernel,
        out_shape=jax.ShapeDtypeStruct((DB, D_MODEL), F32),
        grid=(nf,),
        in_specs=[full((DB, D_MODEL)), ff(0), ff(nf), ff(0), ff(0),
                  pl.BlockSpec((3, tf), lambda j: (0, j)), pl.BlockSpec((1, tf), lambda j: (0, j)),
                  down_spec, full((1, D_MODEL)), full((1, D_MODEL))],
        out_specs=full((DB, D_MODEL)),
        scratch_shapes=[pltpu.VMEM((DB, D_MODEL), F32)],
        compiler_params=_cparams(1),
        name="ffn_sample",
    )(x, ug, ug, buf0, buf1, conv_w, conv_b.reshape(1, -1), w_down, g.reshape(1, -1), b.reshape(1, -1))


def kernel(x_prompt, x_sample, cache_a_k, cache_a_v, state_b, state_c, state_d_c, state_d_n, state_d_m,
           cache_mem_k, cache_mem_v, state_conv, page_table, mem_prompt,
           ev_w_in, ev_w_out, ev_lam, ev_subln_g, ev_lb_logits, ev_b_norm_g,
           od_w_in, od_b_if, od_w_out, od_c_norm_g, od_d_norm_g,
           ln_g, ln_b, xa_wq, xa_wkv, xa_wo, ffn_w_up, ffn_conv_w, ffn_conv_b, ffn_w_down):
    B, T, _ = x_prompt.shape
    DB, t_s, _ = x_sample.shape
    assert t_s == 1, "the sample group is a single decoding step"
    assert T % CHUNK == 0
    n_pool, page = cache_a_k.shape[1], cache_a_k.shape[2]
    n_pages = page_table.shape[1]
    past = n_pages * page
    n_mem = mem_prompt.shape[1]
    d_ff = ffn_w_down.shape[1]
    tf = d_ff // 2
    N = B * T
    TM = 512

    pos_p = jnp.arange(T)
    pos_s = jnp.full((DB,), past, jnp.int32)
    lb_table = jnp.cumsum(jax.nn.softmax(ev_lb_logits.astype(F32), axis=0), axis=0)

    xp = x_prompt.reshape(N, D_MODEL)
    xs = x_sample.reshape(DB, D_MODEL)
    outs = {k: [] for k in ("ak_p", "av_p", "ak_s", "av_s", "sb_p", "sb_s", "sc_p", "sc_s", "dc_p", "dc_s",
                            "dn_p", "dn_s", "dm_p", "dm_s", "mk_p", "mv_p", "cv_p", "cv_s")}
    swap = lambda s: jnp.swapaxes(s, -1, -2)
    mem_k_rows, mem_v_rows = _mem_rows(cache_mem_k), _mem_rows(cache_mem_v)
    wq_all, wkv_all, wo_all = xa_wq.astype(BF16), xa_wkv.astype(BF16), xa_wo.astype(BF16)
    w_up_all, w_down_all = ffn_w_up.astype(BF16), ffn_w_down.astype(BF16)

    for l in range(DEPTH):
        j = l // 2
        if l % 2 == 0:
            lam_init = 0.8 - 0.6 * math.exp(-0.3 * l)
            w_in = ev_w_in[j].astype(BF16)
            w_out = ev_w_out[j].astype(BF16)
            cos_t, sin_t = _rope_tables(pos_p, DH_A)
            qt, kb, kt32, vt, v_rows, h_b = _ev_in_prompt(xp.reshape(B, T, D_MODEL), w_in, cos_t, sin_t, TM)
            o_a = _dattn_prompt(qt, kb, vt, ev_lam[j], ev_subln_g[j], lam_init, 512)
            o_b, st = _hgrn_prompt(h_b, lb_table[j], 512)
            outs["ak_p"].append(kt32.reshape(B, N_HEADS, 2, DH_A, T).transpose(0, 4, 1, 2, 3))
            outs["av_p"].append(v_rows.reshape(B, T, N_HEADS, HEAD_W))
            outs["sb_p"].append(swap(st))
            xp = _proj_ln([o_a.reshape(N, MIX_W), o_b.reshape(N, MIX_W)], w_out,
                          xp, ln_g[l, 0], ln_b[l, 0], TM, "ev_out_prompt",
                          gates=[None, ("rms", ev_b_norm_g[j], "silu", h_b.reshape(N, -1), 3)])
            hs = _matmul(xs, w_in, DB, 512, "ev_in_sample")
            cos_s, sin_s = _rope_tables(pos_s, DH_A)
            qs, ks32, kts32 = _ev_prep_sample(hs, cos_s, sin_s)
            vs32 = hs[:, 2 * MIX_W:3 * MIX_W]
            kt_pool = cache_a_k.transpose(0, 1, 3, 4, 5, 2).reshape(-1, MIX_W, page)
            v_pool = cache_a_v.reshape(-1, page * N_HEADS, HEAD_W)
            oa_s = _dattn_sample(qs, ks32, vs32, kt_pool, v_pool, j * n_pool, page_table,
                                 ev_lam[j], ev_subln_g[j], lam_init)
            ob_s, sb_s = _hgrn_step(hs, state_b[j], lb_table[j], ev_b_norm_g[j], 8)
            outs["ak_s"].append(kts32.reshape(N_HEADS, 2, DH_A, DB).transpose(3, 0, 1, 2)[:, None])
            outs["av_s"].append(vs32.reshape(DB, 1, N_HEADS, HEAD_W))
            outs["sb_s"].append(sb_s)
            xs = _proj_ln([oa_s, ob_s], w_out, xs, ln_g[l, 0], ln_b[l, 0], DB, "ev_out_sample")
        else:
            n_main = 8 * MIX_W
            w_in = od_w_in[j][:, :n_main].astype(BF16)
            w_gate = jnp.pad(od_w_in[j][:, n_main:], ((0, 0), (0, LANES - 2 * N_HEADS))).astype(BF16)
            w_out = od_w_out[j].astype(BF16)
            h, gates = _od_in_prompt(xp, w_in, w_gate, TM, w_in.shape[1])
            cos_t, sin_t = _rope_tables(pos_p, HEAD_W)
            o_c, o_d, sct, dct, dn, dm = _od_prompt(h.reshape(B, T, -1), gates.reshape(B, T, -1), od_b_if[j],
                                                    cos_t, sin_t, 512)
            outs["sc_p"].append(swap(sct))
            outs["dc_p"].append(swap(dct))
            outs["dn_p"].append(dn[:, :N_HEADS, :])
            outs["dm_p"].append(dm[:, :N_HEADS, 0])
            xp = _proj_ln([o_c.reshape(N, MIX_W), o_d.reshape(N, MIX_W)], w_out,
                          xp, ln_g[l, 0], ln_b[l, 0], TM, "od_out_prompt",
                          gates=[("group", od_c_norm_g[j], "silu", h, 3), ("group", od_d_norm_g[j], "sigmoid", h, 7)])
            hs = _matmul(xs, w_in, DB, 512, "od_in_sample")
            gates_s = _matmul(xs, w_gate, DB, LANES, "od_gates_sample")
            cos_s, sin_s = _rope_tables(pos_s[:1], HEAD_W)
            oc_s, od_s, sc_s, dc_s, dn_s, dm_s = _od_step(hs, gates_s, od_b_if[j], cos_s, sin_s,
                                                          state_c[j], state_d_c[j], state_d_n[j], state_d_m[j],
                                                          od_c_norm_g[j], od_d_norm_g[j], 8)
            outs["sc_s"].append(sc_s)
            outs["dc_s"].append(dc_s)
            outs["dn_s"].append(dn_s)
            outs["dm_s"].append(dm_s)
            xs = _proj_ln([oc_s, od_s], w_out, xs, ln_g[l, 0], ln_b[l, 0], DB, "od_out_sample")

        wq, wo = (wq_all, l), (wo_all, l)
        mkv = _matmul(mem_prompt.reshape(B * n_mem, D_MODEL), (wkv_all, l), 512, 512, "mem_kv")
        mk, mv = mkv[:, :D_MODEL], mkv[:, D_MODEL:]
        outs["mk_p"].append(mk.reshape(B, n_mem, N_HEADS, DH_X))
        outs["mv_p"].append(mv.reshape(B, n_mem, N_HEADS, DH_X))
        xp = _xattn_prompt(xp, wq, mk.astype(BF16).reshape(B, n_mem, D_MODEL),
                           mv.astype(BF16).reshape(B, n_mem, D_MODEL), wo, ln_g[l, 1], ln_b[l, 1], T, 1024)
        q_s = _matmul(xs, wq, DB, 512, "xattn_q_sample")
        xo_s = _xattn_sample(q_s, mem_k_rows, mem_v_rows, l * DB)
        xs = _proj_ln([xo_s], wo, xs, ln_g[l, 1], ln_b[l, 1], DB, "xattn_out_sample")

        w_up, w_down = (w_up_all, l), (w_down_all, l)
        tm_f = min(TM, T)
        xp, tails = _ffn_prompt(xp, w_up, ffn_conv_w[l], ffn_conv_b[l], w_down, ln_g[l, 2], ln_b[l, 2], T, TM, d_ff)
        tails = tails.reshape(B, T // tm_f, SUBLANES, d_ff)
        outs["cv_p"].append(tails[:, -1, SUBLANES - 2:, :])
        ug_s = _matmul(xs, w_up, DB, tf, "ffn_up_sample")
        buf = state_conv[l]
        xs = _ffn_sample(xs, ug_s, buf[:, 0, :], buf[:, 1, :], ffn_conv_w[l], ffn_conv_b[l], w_down,
                         ln_g[l, 2], ln_b[l, 2], tf)
        outs["cv_s"].append(jnp.stack([buf[:, 1, :], ug_s[:, :d_ff]], axis=1))

    st = lambda k: jnp.stack(outs[k])
    return (xp.reshape(B, T, D_MODEL), xs.reshape(DB, 1, D_MODEL),
            st("ak_p"), st("av_p"), st("ak_s"), st("av_s"), st("sb_p"), st("sb_s"),
            st("sc_p"), st("sc_s"), st("dc_p"), st("dc_s"), st("dn_p"), st("dn_s"), st("dm_p"), st("dm_s"),
            st("mk_p"), st("mv_p"), st("cv_p"), st("cv_s"))
```

```python
import functools
import math

import numpy as np
import jax
import jax.numpy as jnp
from jax import lax
from jax.experimental import pallas as pl
from jax.experimental.pallas import tpu as pltpu

F32 = jnp.float32
BF16 = jnp.bfloat16

D_MODEL = 1024
MIX_W = D_MODEL // 2
N_HEADS = 4
HEAD_W = MIX_W // N_HEADS
DH_A = HEAD_W // 2
DH_X = D_MODEL // N_HEADS
CHUNK = 64
OD_CHUNK = 128
ROPE_THETA = 10000.0
NORM_EPS = 1e-5
DEPTH = 2
DN_ALPHA = (2.0 * DEPTH) ** 0.25
LANES = 128
SUBLANES = 8
VMEM_LIMIT = 56 * 1024 * 1024
NEG_INF = float("-inf")

NT_DIMS = (((1,), (1,)), ((), ()))
TN_DIMS = (((0,), (0,)), ((), ()))


def _wspec(w, block_shape, index_map, **kw):
    if isinstance(w, tuple):
        stack, layer = w
        return stack, pl.BlockSpec((None,) + tuple(block_shape), lambda *a: (layer,) + tuple(index_map(*a)), **kw)
    return w, pl.BlockSpec(tuple(block_shape), index_map, **kw)


def _wshape(w):
    return w[0].shape[1:] if isinstance(w, tuple) else w.shape


def _cparams(n_axes, vmem=VMEM_LIMIT):
    return pltpu.CompilerParams(dimension_semantics=("arbitrary",) * n_axes, vmem_limit_bytes=vmem)


def _bdot(a, b):
    return jnp.dot(a.astype(BF16), b.astype(BF16), preferred_element_type=F32)


def _bdot_nt(a, b):
    return lax.dot_general(a.astype(BF16), b.astype(BF16), NT_DIMS, preferred_element_type=F32)


def _bdot_tn(a, b):
    return lax.dot_general(a.astype(BF16), b.astype(BF16), TN_DIMS, preferred_element_type=F32)


def _split3(x):
    p1 = x.astype(BF16)
    r1 = x - p1.astype(F32)
    p2 = r1.astype(BF16)
    p3 = (r1 - p2.astype(F32)).astype(BF16)
    return p1, p2, p3


def _cumsum_rows(tri, x):
    p1, p2, p3 = _split3(x)
    d = functools.partial(jnp.dot, preferred_element_type=F32)
    return d(tri, p1) + d(tri, p2) + d(tri, p3)


def _cumsum_lanes(x, triu):
    p1, p2, p3 = _split3(x)
    d = functools.partial(jnp.dot, preferred_element_type=F32)
    return d(p1, triu) + d(p2, triu) + d(p3, triu)


def _tri(L, lower):
    r = lax.broadcasted_iota(jnp.int32, (L, L), 0)
    c = lax.broadcasted_iota(jnp.int32, (L, L), 1)
    return (r >= c) if lower else (r <= c)


def _sigmoid(x):
    return 1.0 / (1.0 + jnp.exp(-x))


def _silu(x):
    return x * _sigmoid(x)


def _log_sigmoid(x):
    return jnp.minimum(x, 0.0) - jnp.log(1.0 + jnp.exp(-jnp.abs(x)))


def _layernorm(z, g, b):
    mu = jnp.mean(z, -1, keepdims=True)
    zc = z - mu
    var = jnp.mean(zc * zc, -1, keepdims=True)
    return zc * lax.rsqrt(var + NORM_EPS) * g + b


def _rmsnorm(x, g):
    return x * lax.rsqrt(jnp.mean(x * x, -1, keepdims=True) + NORM_EPS) * g


def _groupnorm(x, g):
    mu = jnp.mean(x, -1, keepdims=True)
    xc = x - mu
    var = jnp.mean(xc * xc, -1, keepdims=True)
    return xc * lax.rsqrt(var + NORM_EPS) * g


def _rope(x, cos, sin, half):
    outs = []
    for c in range(x.shape[1] // LANES):
        xc = x[:, c * LANES:(c + 1) * LANES]
        if 2 * half == LANES:
            sw = pltpu.roll(xc, half, 1)
        else:
            lane = lax.broadcasted_iota(jnp.int32, xc.shape, 1)
            first = (lane & (2 * half - 1)) < half
            sw = jnp.where(first, pltpu.roll(xc, LANES - half, 1), pltpu.roll(xc, half, 1))
        outs.append(xc * cos + sw * sin)
    return outs[0] if len(outs) == 1 else jnp.concatenate(outs, axis=1)


def _rope_tables(pos, d):
    inv = ROPE_THETA ** (-jnp.arange(0, d // 2, dtype=F32) * 2.0 / d)
    ang = pos.astype(F32)[:, None] * inv[None, :]
    cos, sin = jnp.cos(ang), jnp.sin(ang)
    reps = LANES // d
    cos_t = jnp.tile(jnp.concatenate([cos, cos], -1), (1, reps))
    sin_t = jnp.tile(jnp.concatenate([-sin, sin], -1), (1, reps))
    return cos_t, sin_t


def _col(row):
    return jnp.broadcast_to(row, (LANES, LANES)).T


def _mm_kernel(x_ref, w_ref, o_ref, xb_ref):
    @pl.when(pl.program_id(1) == 0)
    def _():
        xb_ref[...] = x_ref[...].astype(BF16)

    o_ref[...] = jnp.dot(xb_ref[...], w_ref[...], preferred_element_type=F32).astype(o_ref.dtype)


def _matmul(x, w, tm, tn, name, out_dtype=F32):
    M, K = x.shape
    N = _wshape(w)[1]
    tm, tn = min(tm, M), min(tn, N)
    w, w_spec = _wspec(w, (K, tn), lambda i, j: (0, j))
    return pl.pallas_call(
        _mm_kernel,
        out_shape=jax.ShapeDtypeStruct((M, N), out_dtype),
        grid=(M // tm, N // tn),
        in_specs=[pl.BlockSpec((tm, K), lambda i, j: (i, 0)), w_spec],
        out_specs=pl.BlockSpec((tm, tn), lambda i, j: (i, j)),
        scratch_shapes=[pltpu.VMEM((tm, K), BF16)],
        compiler_params=_cparams(2),
        name=name,
    )(x, w)


_HEAD_NORMS = {"rms": _rmsnorm, "group": _groupnorm}
_GATE_ACTS = {"silu": _silu, "sigmoid": _sigmoid}


def _proj_ln_kernel(*refs, n_in, gated):
    a_refs, w_refs = refs[:n_in], refs[n_in:2 * n_in]
    x_ref, g_ref, b_ref = refs[2 * n_in:2 * n_in + 3]
    extra, o_ref = refs[2 * n_in + 3:-1], refs[-1]
    y = None
    for a_ref, w_ref, gate in zip(a_refs, w_refs, gated):
        a = a_ref[...]
        if gate is not None:
            norm, act = _HEAD_NORMS[gate[0]], _GATE_ACTS[gate[1]]
            gain_ref, gate_ref, extra = extra[0], extra[1], extra[2:]
            a = jnp.concatenate([norm(_head(a, h), gain_ref[...]) for h in range(N_HEADS)], axis=1)
            a = a * act(gate_ref[...])
        ya = _bdot(a, w_ref[...])
        y = ya if y is None else y + ya
    o_ref[...] = _layernorm(DN_ALPHA * x_ref[...] + y, g_ref[...], b_ref[...])


def _proj_ln(acts, weight, x, g, b, tm, name, gates=None):
    M = x.shape[0]
    tm = min(tm, M)
    n_in = len(acts)
    gates = gates or [None] * n_in
    in_specs = [pl.BlockSpec((tm, a.shape[1]), lambda i: (i, 0)) for a in acts]
    w_args = []
    for r, a in enumerate(acts):
        w_arr, w_spec = _wspec(weight, (a.shape[1], D_MODEL), lambda i, r=r: (r, 0))
        w_args.append(w_arr)
        in_specs.append(w_spec)
    in_specs += [pl.BlockSpec((tm, D_MODEL), lambda i: (i, 0)),
                 pl.BlockSpec((1, D_MODEL), lambda i: (0, 0)),
                 pl.BlockSpec((1, D_MODEL), lambda i: (0, 0))]
    extra = []
    for gate in gates:
        if gate is not None:
            _, gain, _, gate_arr, blk = gate
            in_specs += [pl.BlockSpec((1, HEAD_W), lambda i: (0, 0)),
                         pl.BlockSpec((tm, MIX_W), lambda i, blk=blk: (i, blk))]
            extra += [gain.reshape(1, -1), gate_arr]
    return pl.pallas_call(
        functools.partial(_proj_ln_kernel, n_in=n_in,
                          gated=tuple(None if gt is None else (gt[0], gt[2]) for gt in gates)),
        out_shape=jax.ShapeDtypeStruct((M, D_MODEL), F32),
        grid=(M // tm,),
        in_specs=in_specs,
        out_specs=pl.BlockSpec((tm, D_MODEL), lambda i: (i, 0)),
        compiler_params=_cparams(1),
        name=name,
    )(*acts, *w_args, x, g.reshape(1, -1), b.reshape(1, -1), *extra)


Q_SCALE = DH_A ** -0.5 * math.log2(math.e)
VT_ROWS = HEAD_W + 16


def _ev_in_prompt_kernel(x_ref, w_ref, cos_ref, sin_ref, qt_ref, kb_ref, kt_ref, vt_ref, vr_ref, hb_ref):
    xb = x_ref[0].astype(BF16)
    cos, sin = cos_ref[...], sin_ref[...]
    proj = lambda lo, hi: jnp.dot(xb, w_ref[:, lo * MIX_W:hi * MIX_W], preferred_element_type=F32)
    hb_ref[0] = proj(3, 7)
    q = _rope(proj(0, 1), cos, sin, DH_A // 2)
    k = _rope(proj(1, 2), cos, sin, DH_A // 2)
    qt_ref[0] = (q * Q_SCALE).T.astype(BF16)
    kb_ref[0] = k.astype(BF16)
    kt_ref[0] = k.T
    v = proj(2, 3)
    tm = v.shape[0]
    vt = v.T.astype(BF16)
    ones = jnp.ones((VT_ROWS - HEAD_W, tm), BF16)
    for h in range(N_HEADS):
        vt_ref[0, h, :HEAD_W, :] = vt[h * HEAD_W:(h + 1) * HEAD_W]
        vt_ref[0, h, HEAD_W:, :] = ones
        vr_ref[0, pl.ds(h, tm, stride=N_HEADS), :] = _head(v, h)


def _ev_in_prompt(x3, w_in, cos_t, sin_t, tm):
    B, T, _ = x3.shape
    tm = min(tm, T)
    tab = pl.BlockSpec((tm, LANES), lambda b, i: (i, 0))
    tr = pl.BlockSpec((1, MIX_W, tm), lambda b, i: (b, 0, i))
    tshape = lambda dt: jax.ShapeDtypeStruct((B, MIX_W, T), dt)
    return pl.pallas_call(
        _ev_in_prompt_kernel,
        out_shape=(tshape(BF16), jax.ShapeDtypeStruct((B, T, MIX_W), BF16), tshape(F32),
                   jax.ShapeDtypeStruct((B, N_HEADS, VT_ROWS, T), BF16),
                   jax.ShapeDtypeStruct((B, T * N_HEADS, HEAD_W), F32),
                   jax.ShapeDtypeStruct((B, T, 4 * MIX_W), F32)),
        grid=(B, T // tm),
        in_specs=[pl.BlockSpec((1, tm, D_MODEL), lambda b, i: (b, i, 0)),
                  pl.BlockSpec(w_in.shape, lambda b, i: (0, 0)), tab, tab],
        out_specs=(tr, pl.BlockSpec((1, tm, MIX_W), lambda b, i: (b, i, 0)), tr,
                   pl.BlockSpec((1, N_HEADS, VT_ROWS, tm), lambda b, i: (b, 0, 0, i)),
                   pl.BlockSpec((1, tm * N_HEADS, HEAD_W), lambda b, i: (b, i, 0)),
                   pl.BlockSpec((1, tm, 4 * MIX_W), lambda b, i: (b, i, 0))),
        compiler_params=_cparams(2),
        name="ev_in_prompt",
    )(x3, w_in, cos_t, sin_t)


def _ev_prep_sample_kernel(qk_ref, cos_ref, sin_ref, q_ref, k_ref, kt_ref):
    cos, sin = cos_ref[...], sin_ref[...]
    k = _rope(qk_ref[:, MIX_W:], cos, sin, DH_A // 2)
    q_ref[...] = _rope(qk_ref[:, :MIX_W], cos, sin, DH_A // 2) * Q_SCALE
    k_ref[...] = k
    kt_ref[...] = k.T


def _ev_prep_sample(h, cos_t, sin_t):
    DB = h.shape[0]
    full = lambda shape: pl.BlockSpec(shape, lambda i: (0,) * len(shape))
    return pl.pallas_call(
        _ev_prep_sample_kernel,
        out_shape=(jax.ShapeDtypeStruct((DB, MIX_W), F32), jax.ShapeDtypeStruct((DB, MIX_W), F32),
                   jax.ShapeDtypeStruct((MIX_W, DB), F32)),
        grid=(1,),
        in_specs=[full((DB, 2 * MIX_W)), full((DB, LANES)), full((DB, LANES))],
        out_specs=(full((DB, MIX_W)), full((DB, MIX_W)), full((MIX_W, DB))),
        compiler_params=_cparams(1),
        name="ev_prep_sample",
    )(h, cos_t, sin_t)


def _lambda(lam_ref, lam_init):
    lp = lam_ref[...]
    s01 = jnp.sum(lp[0:1] * lp[1:2], axis=1, keepdims=True)
    s23 = jnp.sum(lp[2:3] * lp[3:4], axis=1, keepdims=True)
    return jnp.exp(s01) - jnp.exp(s23) + lam_init


def _dattn_kernel(lam_ref, g_ref, qt_ref, k_ref, vt_ref, o_ref, m_ref, acc_ref, *, tq, lam_init):
    i = pl.program_id(2)
    qt = qt_ref[0]
    sub = lax.broadcasted_iota(jnp.int32, qt.shape, 0)
    zero = jnp.zeros_like(qt)
    q_maps = [jnp.where(sub < DH_A, qt, zero), jnp.where(sub >= DH_A, qt, zero)]
    maps = range(2)
    m_ref[...] = jnp.full(m_ref.shape, NEG_INF, F32)
    acc_ref[...] = jnp.zeros(acc_ref.shape, F32)

    def steps(js, masked):
        ks, vts, ss = [], [], []
        for j in js:
            start = pl.multiple_of(j * tq, tq)
            ks.append(k_ref[0, pl.ds(start, tq), :])
            vts.append(vt_ref[0, 0, :, pl.ds(start, tq)])
        for kj in ks:
            ss.append([jnp.dot(kj, q_maps[c], preferred_element_type=F32) for c in maps])
        for n in range(len(js)):
            for c in maps:
                s = ss[n][c]
                if masked:
                    r = lax.broadcasted_iota(jnp.int32, s.shape, 0)
                    col = lax.broadcasted_iota(jnp.int32, s.shape, 1)
                    s = jnp.where(r <= col, s, NEG_INF)
                m_prev = m_ref[c]
                m_new = jnp.maximum(m_prev, jnp.max(s, axis=0, keepdims=True))
                alpha = jnp.exp2(m_prev - m_new)
                p = jnp.exp2(s - m_new)
                acc_ref[c] = alpha * acc_ref[c] + jnp.dot(vts[n], p.astype(BF16), preferred_element_type=F32)
                m_ref[c] = m_new

    def body(jj, carry):
        steps([2 * jj, 2 * jj + 1], False)
        return carry

    lax.fori_loop(0, lax.shift_right_logical(i, 1), body, 0)

    @pl.when((i & 1) == 1)
    def _():
        steps([i - 1], False)

    steps([i], True)
    lam = _lambda(lam_ref, lam_init)
    o = [acc_ref[c, :HEAD_W, :] / acc_ref[c, HEAD_W:HEAD_W + 1, :] for c in maps]
    d = (o[0] - lam * o[1]).T
    o_ref[0] = (_rmsnorm(d, g_ref[...]) * (1.0 - lam_init)).astype(BF16)


def _dattn_prompt(qt, kb, vt, lam_p, subln_g, lam_init, tq):
    B, T, _ = kb.shape
    tq = min(tq, T)
    return pl.pallas_call(
        functools.partial(_dattn_kernel, tq=tq, lam_init=lam_init),
        out_shape=jax.ShapeDtypeStruct((B, T, MIX_W), BF16),
        grid=(B, N_HEADS, T // tq),
        in_specs=[pl.BlockSpec((4, DH_A), lambda b, h, i: (0, 0)),
                  pl.BlockSpec((1, HEAD_W), lambda b, h, i: (0, 0)),
                  pl.BlockSpec((1, HEAD_W, tq), lambda b, h, i: (b, h, i)),
                  pl.BlockSpec((1, T, HEAD_W), lambda b, h, i: (b, 0, h)),
                  pl.BlockSpec((1, 1, VT_ROWS, T), lambda b, h, i: (b, h, 0, 0))],
        out_specs=pl.BlockSpec((1, tq, HEAD_W), lambda b, h, i: (b, i, h)),
        scratch_shapes=[pltpu.VMEM((2, 1, tq), F32), pltpu.VMEM((2, VT_ROWS, tq), F32)],
        compiler_params=_cparams(3),
        name="dattn_prompt",
    )(lam_p, subln_g.reshape(1, -1), qt, kb, vt)


def _head(x, h):
    return x[:, h * HEAD_W:(h + 1) * HEAD_W]


def _gla_chunk(seqs, st_ref, tril_mask):
    seqs = [tuple(x.astype(BF16) for x in s[:4]) + (s[4],) for s in seqs]
    chains = [(n, h) for n in range(len(seqs)) for h in range(N_HEADS)]
    part = lambda n, i, h: _head(seqs[n][i], h)
    sts = [st_ref[N_HEADS * n + h] for n, h in chains]
    inter = [_bdot_nt(part(n, 0, h), sts[c]) for c, (n, h) in enumerate(chains)]
    upd = [_bdot_tn(part(n, 3, h), part(n, 2, h)) for n, h in chains]
    attn = [jnp.where(tril_mask, _bdot_nt(part(n, 0, h), part(n, 1, h)), 0.0) for n, h in chains]
    intra = [_bdot(attn[c], part(n, 3, h)) for c, (n, h) in enumerate(chains)]
    for c, (n, h) in enumerate(chains):
        dec = seqs[n][4]
        st_ref[N_HEADS * n + h] = sts[c] * (dec[h] if isinstance(dec, (list, tuple)) else _head(dec, h)) + upd[c]
    outs = [intra[c] + inter[c] for c in range(len(chains))]
    return [outs[N_HEADS * n:N_HEADS * (n + 1)] for n in range(len(seqs))]


def _hgrn_kernel(q_ref, f_ref, i_ref, lb_ref, o_ref, s_ref, st_ref, *, tt, L, n_seq):
    t = pl.program_id(0)

    @pl.when(t == 0)
    def _():
        st_ref[...] = jnp.zeros(st_ref.shape, F32)

    tril_mask = _tri(L, True)
    tril = tril_mask.astype(BF16)

    def body(c, carry):
        rows = pl.ds(pl.multiple_of(c * L, L), L)
        lb = lb_ref[...]
        seqs = []
        for n in range(n_seq):
            f = lb + (1.0 - lb) * _sigmoid(f_ref[n, rows, :])
            b = _cumsum_rows(tril, jnp.log(f))
            b_end = b[L - 1:L, :]
            k = 1.0 - f
            q_in = _silu(q_ref[n, rows, :]) * jnp.exp(b)
            seqs.append((q_in, k * jnp.exp(-b), k * jnp.exp(b_end - b), i_ref[n, rows, :], jnp.exp(b_end)))
        for n, o in enumerate(_gla_chunk(seqs, st_ref, tril_mask)):
            o_ref[n, rows, :] = jnp.concatenate(o, axis=1)
        return carry

    lax.fori_loop(0, tt // L, body, 0, unroll=2)

    @pl.when(t == pl.num_programs(0) - 1)
    def _():
        s_ref[...] = st_ref[...]


def _hgrn_prompt(h3, lb, tt):
    B, T, _ = h3.shape
    tt = min(tt, T)
    L = math.gcd(T, CHUNK)
    col = lambda c: pl.BlockSpec((B, tt, MIX_W), lambda t: (0, t, c))
    o, st = pl.pallas_call(
        functools.partial(_hgrn_kernel, tt=tt, L=L, n_seq=B),
        out_shape=(jax.ShapeDtypeStruct((B, T, MIX_W), F32),
                   jax.ShapeDtypeStruct((B * N_HEADS, HEAD_W, HEAD_W), F32)),
        grid=(T // tt,),
        in_specs=[col(0), col(1), col(2), pl.BlockSpec((1, MIX_W), lambda t: (0, 0))],
        out_specs=(pl.BlockSpec((B, tt, MIX_W), lambda t: (0, t, 0)),
                   pl.BlockSpec((B * N_HEADS, HEAD_W, HEAD_W), lambda t: (0, 0, 0))),
        scratch_shapes=[pltpu.VMEM((B * N_HEADS, HEAD_W, HEAD_W), F32)],
        compiler_params=_cparams(1),
        name="hgrn_prompt",
    )(h3, h3, h3, lb.reshape(1, -1))
    return o, st.reshape(B, N_HEADS, HEAD_W, HEAD_W)


def _log_gamma(h):
    return float(np.log(1.0 - 2.0 ** (-5.0 - h)))


def _od_kernel(cq_ref, ck_ref, cv_ref, dq_ref, dk_ref, dv_ref, gc_ref, gr_ref,
               bc_ref, br_ref,
               oc_ref, od_ref, sc_ref, dc_ref, dn_ref, dm_ref,
               st_ref, ct_ref, n_ref, m_ref, *, tt, L, n_seq):
    t = pl.program_id(0)

    @pl.when(t == 0)
    def _():
        st_ref[...] = jnp.zeros(st_ref.shape, F32)
        ct_ref[...] = jnp.zeros(ct_ref.shape, F32)
        n_ref[...] = jnp.zeros(n_ref.shape, F32)
        m_ref[...] = jnp.zeros(m_ref.shape, F32)

    tril_mask = _tri(L, True)
    tril = tril_mask.astype(BF16)
    triu = _tri(L, False).astype(BF16)
    scale = HEAD_W ** -0.5
    heads = range(N_HEADS)
    pos1 = (lax.broadcasted_iota(jnp.int32, (L, HEAD_W), 0) + 1).astype(F32)
    ret_b = jnp.concatenate([pos1 * _log_gamma(h) for h in heads], axis=1)
    ret_b_end = jnp.concatenate([jnp.full((L, HEAD_W), L * _log_gamma(h), F32) for h in heads], axis=1)
    ret_q_dec, ret_k_dec, ret_k_end = jnp.exp(ret_b), jnp.exp(-ret_b), jnp.exp(ret_b_end - ret_b)
    ret_dec = [math.exp(L * _log_gamma(h)) for h in heads]

    seqs = range(n_seq)
    chains = [(n, h) for n in seqs for h in heads]

    def body(c, carry):
        rows = pl.ds(pl.multiple_of(c * L, L), L)
        ret = []
        for n in seqs:
            q, k = cq_ref[n, rows, :], ck_ref[n, rows, :]
            ret.append((q * ret_q_dec, k * ret_k_dec, k * ret_k_end, cv_ref[n, rows, :], ret_dec))
        for n, o in enumerate(_gla_chunk(ret, st_ref, tril_mask)):
            oc_ref[n, rows, :] = jnp.concatenate(o, axis=1)
        g_c = [gc_ref[n, rows, :] + bc_ref[...] for n in seqs]
        g_r = [gr_ref[n, c] + br_ref[...] for n in seqs]
        lf_c = [_log_sigmoid(g) for g in g_c]
        b_c = [_cumsum_rows(tril, jnp.concatenate(
            [jnp.broadcast_to(lf_c[n][:, N_HEADS + h:N_HEADS + h + 1], (L, HEAD_W)) for h in heads], axis=1))
            for n in seqs]
        b_r = [_cumsum_lanes(_log_sigmoid(g), triu) for g in g_r]
        dq = [dq_ref[n, rows, :] * scale for n in seqs]
        dk = [dk_ref[n, rows, :] for n in seqs]
        dqb = [x.astype(BF16) for x in dq]
        dkb = [x.astype(BF16) for x in dk]
        dvb = [dv_ref[n, rows, :].astype(BF16) for n in seqs]
        cts = [ct_ref[i] for i in range(len(chains))]
        q_c = [_bdot_nt(_head(dqb[n], h), cts[i]) for i, (n, h) in enumerate(chains)]
        q_k = [_bdot_nt(_head(dqb[n], h), _head(dkb[n], h)) for n, h in chains]
        m_prev = [m_ref[i:i + 1, :] for i in range(len(chains))]
        bcw, m_t, w_mat = [], [], []
        for i, (n, h) in enumerate(chains):
            bcw.append(_head(b_c[n], h))
            bc = bcw[i][:, :L]
            dm = jnp.where(tril_mask, bc - b_r[n][N_HEADS + h:N_HEADS + h + 1, :] + g_r[n][h:h + 1, :], NEG_INF)
            m_t.append(jnp.maximum(bcw[i] + m_prev[i], jnp.max(dm, axis=1, keepdims=True)))
            w_mat.append(jnp.exp(dm - m_t[i][:, :L]) * q_k[i])
        w_v = [_bdot(w_mat[i], _head(dvb[n], h)) for i, (n, h) in enumerate(chains)]
        kws, c_scales = [], []
        for i, (n, h) in enumerate(chains):
            m_new = m_t[i][L - 1:L, :]
            b_last = bcw[i][L - 1:L, :]
            c_scales.append(jnp.exp(b_last + m_prev[i] - m_new))
            kws.append(_head(dk[n], h) * jnp.exp(b_last - bcw[i] + g_c[n][:, h:h + 1] - m_new))
            m_ref[i:i + 1, :] = m_new
        c_upd = [_bdot_tn(_head(dvb[n], h), kws[i]) for i, (n, h) in enumerate(chains)]
        hs = []
        for i, (n, h) in enumerate(chains):
            n_row = n_ref[i:i + 1, :]
            inter = jnp.exp(bcw[i] + m_prev[i] - m_t[i])
            num = inter * q_c[i] + w_v[i]
            den = (inter[:, 0:1] * jnp.sum(_head(dq[n], h) * n_row, axis=1, keepdims=True)
                   + jnp.sum(w_mat[i], axis=1, keepdims=True))
            hs.append(num / jnp.maximum(jnp.abs(den), jnp.exp(-m_t[i])))
            ct_ref[i] = c_scales[i] * cts[i] + c_upd[i]
            n_ref[i:i + 1, :] = c_scales[i] * n_row + jnp.sum(kws[i], axis=0, keepdims=True)
        for n in seqs:
            od_ref[n, rows, :] = jnp.concatenate(hs[N_HEADS * n:N_HEADS * (n + 1)], axis=1)
        return carry

    lax.fori_loop(0, tt // L, body, 0)

    @pl.when(t == pl.num_programs(0) - 1)
    def _():
        sc_ref[...] = st_ref[...]
        dc_ref[...] = ct_ref[...]
        dn_ref[...] = n_ref[...]
        dm_ref[...] = m_ref[...]


def _od_in_kernel(x_ref, w_ref, wg_ref, cos_ref, sin_ref, h_ref, g_ref):
    xb = x_ref[...].astype(BF16)
    cos, sin = cos_ref[...], sin_ref[...]
    proj = lambda lo, hi: jnp.dot(xb, w_ref[:, lo * MIX_W:hi * MIX_W], preferred_element_type=F32)
    h_ref[:, 2 * MIX_W:] = proj(2, 8)
    h_ref[:, :MIX_W] = _rope(proj(0, 1), cos, sin, HEAD_W // 2)
    h_ref[:, MIX_W:2 * MIX_W] = _rope(proj(1, 2), cos, sin, HEAD_W // 2) * (HEAD_W ** -0.5)
    g_ref[...] = jnp.dot(xb, wg_ref[...], preferred_element_type=F32)


def _od_in_prompt(x, w_in, w_gate, cos_t, sin_t, tm):
    M, K = x.shape
    N = w_in.shape[1]
    tm = min(tm, cos_t.shape[0])
    n_tab = cos_t.shape[0] // tm
    tab = pl.BlockSpec((tm, LANES), lambda i: (i % n_tab, 0))
    return pl.pallas_call(
        _od_in_kernel,
        out_shape=(jax.ShapeDtypeStruct((M, N), F32), jax.ShapeDtypeStruct((M, LANES), F32)),
        grid=(M // tm,),
        in_specs=[pl.BlockSpec((tm, K), lambda i: (i, 0)),
                  pl.BlockSpec((K, N), lambda i: (0, 0)),
                  pl.BlockSpec((K, LANES), lambda i: (0, 0)), tab, tab],
        out_specs=(pl.BlockSpec((tm, N), lambda i: (i, 0)), pl.BlockSpec((tm, LANES), lambda i: (i, 0))),
        compiler_params=_cparams(1),
        name="od_in_prompt",
    )(x, w_in, w_gate, cos_t, sin_t)


def _od_prompt(h3, gates, b_if, tt):
    B, T, _ = h3.shape
    tt = min(tt, T)
    L = math.gcd(T, OD_CHUNK)
    gates_r = gates[:, :, :2 * N_HEADS].reshape(B, T // L, L, 2 * N_HEADS).transpose(0, 1, 3, 2)
    b_pad = jnp.zeros((1, LANES), F32).at[0, :2 * N_HEADS].set(b_if)
    n_chain = B * N_HEADS
    n_row = -(-n_chain // SUBLANES) * SUBLANES
    col = lambda c: pl.BlockSpec((B, tt, MIX_W), lambda t: (0, t, c))
    mat_state = pl.BlockSpec((n_chain, HEAD_W, HEAD_W), lambda t: (0, 0, 0))
    row_state = pl.BlockSpec((n_row, LANES), lambda t: (0, 0))
    out_blk = pl.BlockSpec((B, tt, MIX_W), lambda t: (0, t, 0))
    mat_shape = jax.ShapeDtypeStruct((n_chain, HEAD_W, HEAD_W), F32)
    row_shape = jax.ShapeDtypeStruct((n_row, LANES), F32)
    o_c, o_d, sct, dct, dn, dm = pl.pallas_call(
        functools.partial(_od_kernel, tt=tt, L=L, n_seq=B),
        out_shape=(jax.ShapeDtypeStruct((B, T, MIX_W), F32), jax.ShapeDtypeStruct((B, T, MIX_W), F32),
                   mat_shape, mat_shape, row_shape, row_shape),
        grid=(T // tt,),
        in_specs=[col(0), col(1), col(2), col(4), col(5), col(6),
                  pl.BlockSpec((B, tt, LANES), lambda t: (0, t, 0)),
                  pl.BlockSpec((B, tt // L, 2 * N_HEADS, L), lambda t: (0, t, 0, 0)),
                  pl.BlockSpec((1, LANES), lambda t: (0, 0)),
                  pl.BlockSpec((2 * N_HEADS, 1), lambda t: (0, 0))],
        out_specs=(out_blk, out_blk, mat_state, mat_state, row_state, row_state),
        scratch_shapes=[pltpu.VMEM(mat_shape.shape, F32), pltpu.VMEM(mat_shape.shape, F32),
                        pltpu.VMEM(row_shape.shape, F32), pltpu.VMEM(row_shape.shape, F32)],
        compiler_params=_cparams(1),
        name="od_prompt",
    )(h3, h3, h3, h3, h3, h3, gates, gates_r, b_pad, b_if.reshape(-1, 1))
    per_seq = lambda s: s.reshape((B, N_HEADS) + s.shape[1:])
    return (o_c, o_d, per_seq(sct), per_seq(dct), per_seq(dn[:n_chain]), per_seq(dm[:n_chain, 0]))


def _xattn_kernel(x_ref, wq_ref, mk_ref, mv_ref, wo_ref, g_ref, b_ref, o_ref):
    x = x_ref[...]
    q = _bdot(x, wq_ref[...])
    qb = (q * (DH_X ** -0.5)).astype(BF16)
    heads = range(N_HEADS)
    cols = [slice(h * DH_X, (h + 1) * DH_X) for h in heads]
    s = [lax.dot_general(qb[:, cols[h]], mk_ref[0, :, cols[h]], NT_DIMS, preferred_element_type=F32) for h in heads]
    p = [jnp.exp(s[h] - jnp.max(s[h], axis=1, keepdims=True)) for h in heads]
    l = [jnp.sum(p[h], axis=1, keepdims=True) for h in heads]
    pv = [jnp.dot(p[h].astype(BF16), mv_ref[0, :, cols[h]], preferred_element_type=F32) for h in heads]
    outs = [(pv[h] / l[h]).astype(BF16) for h in heads]
    y = jnp.dot(jnp.concatenate(outs, axis=1), wo_ref[...], preferred_element_type=F32)
    o_ref[...] = _layernorm(DN_ALPHA * x + y, g_ref[...], b_ref[...])


def _xattn_prompt(x, wq, mk, mv, wo, g, b, T, tm):
    M = x.shape[0]
    tm = min(tm, T)
    n_mem = mk.shape[1]
    per_b = T // tm
    full = lambda shape: pl.BlockSpec(shape, lambda i: (0,) * len(shape))
    mem = pl.BlockSpec((1, n_mem, D_MODEL), lambda i: (i // per_b, 0, 0))
    wq, wq_spec = _wspec(wq, (D_MODEL, D_MODEL), lambda i: (0, 0))
    wo, wo_spec = _wspec(wo, (D_MODEL, D_MODEL), lambda i: (0, 0))
    return pl.pallas_call(
        _xattn_kernel,
        out_shape=jax.ShapeDtypeStruct((M, D_MODEL), F32),
        grid=(M // tm,),
        in_specs=[pl.BlockSpec((tm, D_MODEL), lambda i: (i, 0)), wq_spec, mem, mem,
                  wo_spec, full((1, D_MODEL)), full((1, D_MODEL))],
        out_specs=pl.BlockSpec((tm, D_MODEL), lambda i: (i, 0)),
        compiler_params=_cparams(1),
        name="xattn_prompt",
    )(x, wq, mk, mv, wo, g.reshape(1, -1), b.reshape(1, -1))


def _ffn_kernel(x_ref, wu_ref, wg_ref, cw_ref, cb_ref, wd_ref, g_ref, b_ref, o_ref, tail_ref,
                xb_ref, acc_ref, stage_ref, carry_ref, *, tm, per_b, single):
    i, j = pl.program_id(0), pl.program_id(1)

    @pl.when(j == 0)
    def _():
        xb_ref[...] = x_ref[...].astype(BF16)

    @pl.when(i % per_b == 0)
    def _():
        carry_ref[j] = jnp.zeros(carry_ref.shape[1:], F32)

    xb = xb_ref[...]
    stage_ref[0:SUBLANES, :] = carry_ref[j]
    stage_ref[SUBLANES:, :] = jnp.dot(xb, wu_ref[...], preferred_element_type=F32)
    last = stage_ref[tm:tm + SUBLANES, :]
    carry_ref[j] = last
    tail_ref[0] = last
    conv = (cb_ref[...] + cw_ref[0:1, :] * stage_ref[SUBLANES - 2:SUBLANES - 2 + tm, :]
            + cw_ref[1:2, :] * stage_ref[SUBLANES - 1:SUBLANES - 1 + tm, :]
            + cw_ref[2:3, :] * stage_ref[SUBLANES:, :])
    gate = jnp.dot(xb, wg_ref[...], preferred_element_type=F32)
    y = _bdot(jax.nn.gelu(conv) * gate, wd_ref[...])
    if single:
        o_ref[...] = _layernorm(DN_ALPHA * x_ref[...] + y, g_ref[...], b_ref[...])
        return

    @pl.when(j == 0)
    def _():
        acc_ref[...] = y

    @pl.when(j > 0)
    def _():
        acc_ref[...] = acc_ref[...] + y

    @pl.when(j == pl.num_programs(1) - 1)
    def _():
        o_ref[...] = _layernorm(DN_ALPHA * x_ref[...] + acc_ref[...], g_ref[...], b_ref[...])


def _ffn_prompt(x, w_up, conv_w, conv_b, w_down, g, b, T, tm, tf):
    M = x.shape[0]
    d_ff = _wshape(w_down)[0]
    tm = min(tm, T)
    nf = d_ff // tf
    per_b = T // tm
    wmode = dict(pipeline_mode=pl.Buffered(1)) if nf == 1 else {}
    w_up_arr, up_spec = _wspec(w_up, (D_MODEL, tf), lambda i, j: (0, j), **wmode)
    _, gate_spec = _wspec(w_up, (D_MODEL, tf), lambda i, j: (0, nf + j), **wmode)
    w_down_arr, down_spec = _wspec(w_down, (tf, D_MODEL), lambda i, j: (j, 0), **wmode)
    return pl.pallas_call(
        functools.partial(_ffn_kernel, tm=tm, per_b=per_b, single=nf == 1),
        out_shape=(jax.ShapeDtypeStruct((M, D_MODEL), F32),
                   jax.ShapeDtypeStruct((M // tm, SUBLANES, d_ff), F32)),
        grid=(M // tm, nf),
        in_specs=[pl.BlockSpec((tm, D_MODEL), lambda i, j: (i, 0)),
                  up_spec, gate_spec,
                  pl.BlockSpec((3, tf), lambda i, j: (0, j)),
                  pl.BlockSpec((1, tf), lambda i, j: (0, j)),
                  down_spec,
                  pl.BlockSpec((1, D_MODEL), lambda i, j: (0, 0)),
                  pl.BlockSpec((1, D_MODEL), lambda i, j: (0, 0))],
        out_specs=(pl.BlockSpec((tm, D_MODEL), lambda i, j: (i, 0)),
                   pl.BlockSpec((1, SUBLANES, tf), lambda i, j: (i, 0, j))),
        scratch_shapes=[pltpu.VMEM((tm, D_MODEL), BF16), pltpu.VMEM((tm, D_MODEL), F32),
                        pltpu.VMEM((tm + SUBLANES, tf), F32), pltpu.VMEM((nf, SUBLANES, tf), F32)],
        compiler_params=_cparams(2),
        name="ffn_prompt",
    )(x, w_up_arr, w_up_arr, conv_w, conv_b.reshape(1, -1), w_down_arr, g.reshape(1, -1), b.reshape(1, -1))


def _dattn_sample_kernel(*refs, n_pages, page, lam_init, rows):
    lam_ref, g_ref, q_ref, kn_ref, vn_ref = refs[1:6]
    o_ref = refs[6 + 2 * rows * n_pages]
    for r in range(rows):
        k_refs = refs[6 + r * n_pages:6 + (r + 1) * n_pages]
        v_refs = refs[6 + (rows + r) * n_pages:6 + (rows + r + 1) * n_pages]
        _dattn_sample_row(lam_ref, g_ref, q_ref.at[r], kn_ref.at[r], vn_ref.at[r], k_refs, v_refs, o_ref.at[r],
                          n_pages=n_pages, page=page, lam_init=lam_init)


def _dattn_sample_row(lam_ref, g_ref, q_ref, kn_ref, vn_ref, k_refs, v_refs, o_ref, *, n_pages, page, lam_init):
    n_rows = 2 * N_HEADS
    q = q_ref[...]
    lane = lax.broadcasted_iota(jnp.int32, (n_rows, MIX_W), 1)
    row = lax.broadcasted_iota(jnp.int32, (n_rows, MIX_W), 0)
    qbd = jnp.where(lax.shift_right_logical(lane, 6) == row, jnp.broadcast_to(q, (n_rows, MIX_W)), 0.0)
    s_new = jnp.sum(qbd * kn_ref[...], axis=1, keepdims=True)
    scores = [_bdot(qbd, k_refs[p][0]) for p in range(n_pages)]
    m = s_new
    for s in scores:
        m = jnp.maximum(m, jnp.max(s, axis=1, keepdims=True))
    p_new = jnp.exp2(s_new - m)
    l = p_new
    vn = vn_ref[...]
    accs = [p_new * vn[:, h * HEAD_W:(h + 1) * HEAD_W] for h in range(N_HEADS)]
    for p in range(n_pages):
        pr = jnp.exp2(scores[p] - m)
        l = l + jnp.sum(pr, axis=1, keepdims=True)
        for h in range(N_HEADS):
            accs[h] = accs[h] + _bdot(pr, v_refs[p][0, pl.ds(h, page, stride=N_HEADS), :])
    lam = _lambda(lam_ref, lam_init)
    outs = []
    for h in range(N_HEADS):
        o = accs[h] / l
        d = o[2 * h:2 * h + 1, :] - lam * o[2 * h + 1:2 * h + 2, :]
        outs.append(_rmsnorm(d, g_ref[...]) * (1.0 - lam_init))
    o_ref[...] = jnp.concatenate(outs, axis=1)


def _dattn_sample(q, k_new, v_new, kt_pool, v_pool, pool_base, page_table, lam_p, subln_g, lam_init):
    DB = q.shape[0]
    n_pages = page_table.shape[1]
    page = kt_pool.shape[2]
    rows = math.gcd(DB, 2)
    tok = pl.BlockSpec((rows, 1, MIX_W), lambda b, pt: (b, 0, 0))

    def k_spec(r, p):
        return pl.BlockSpec((1, MIX_W, page), lambda b, pt: (pool_base + pt[rows * b + r, p], 0, 0))

    def v_spec(r, p):
        return pl.BlockSpec((1, page * N_HEADS, HEAD_W), lambda b, pt: (pool_base + pt[rows * b + r, p], 0, 0))

    pages = [(r, p) for r in range(rows) for p in range(n_pages)]
    grid_spec = pltpu.PrefetchScalarGridSpec(
        num_scalar_prefetch=1,
        grid=(DB // rows,),
        in_specs=[pl.BlockSpec((4, DH_A), lambda b, pt: (0, 0)),
                  pl.BlockSpec((1, HEAD_W), lambda b, pt: (0, 0)), tok, tok, tok]
                 + [k_spec(r, p) for r, p in pages] + [v_spec(r, p) for r, p in pages],
        out_specs=tok,
    )
    out = pl.pallas_call(
        functools.partial(_dattn_sample_kernel, n_pages=n_pages, page=page, lam_init=lam_init, rows=rows),
        out_shape=jax.ShapeDtypeStruct((DB, 1, MIX_W), F32),
        grid_spec=grid_spec,
        compiler_params=_cparams(1),
        name="dattn_sample",
    )(page_table, lam_p, subln_g.reshape(1, -1), q.reshape(DB, 1, MIX_W), k_new.reshape(DB, 1, MIX_W),
      v_new.reshape(DB, 1, MIX_W), *([kt_pool] * len(pages)), *([v_pool] * len(pages)))
    return out.reshape(DB, MIX_W)


def _hgrn_step_kernel(h_ref, s0_ref, lb_ref, ng_ref, o_ref, s_ref, *, bt):
    def body(i, carry):
        row = h_ref[i]
        heads = range(N_HEADS)
        blk = lambda n: row[:, n * MIX_W:(n + 1) * MIX_W]
        lb = lb_ref[...]
        f = lb + (1.0 - lb) * _sigmoid(blk(4))
        q, v = _silu(blk(3)), blk(5)
        f_col = [_col(_head(f, h)) for h in heads]
        s_new = [f_col[h] * s0_ref[i, h] + (1.0 - f_col[h]) * _head(v, h) for h in heads]
        for h in heads:
            s_ref[i, h] = s_new[h]
        o = [_bdot(jnp.broadcast_to(_head(q, h), (SUBLANES, HEAD_W)), s_new[h])[0:1, :] for h in heads]
        o = jnp.concatenate([_rmsnorm(o[h], ng_ref[...]) for h in heads], axis=1)
        o_ref[i] = o * _silu(blk(6))
        return carry

    lax.fori_loop(0, bt, body, 0, unroll=2)


def _hgrn_step(h, s0, lb, norm_g, bt):
    DB, W = h.shape
    bt = min(bt, DB)
    state = pl.BlockSpec((bt, N_HEADS, HEAD_W, HEAD_W), lambda i: (i, 0, 0, 0))
    o, s = pl.pallas_call(
        functools.partial(_hgrn_step_kernel, bt=bt),
        out_shape=(jax.ShapeDtypeStruct((DB, 1, MIX_W), F32), jax.ShapeDtypeStruct(s0.shape, F32)),
        grid=(DB // bt,),
        in_specs=[pl.BlockSpec((bt, 1, W), lambda i: (i, 0, 0)), state,
                  pl.BlockSpec((1, MIX_W), lambda i: (0, 0)), pl.BlockSpec((1, HEAD_W), lambda i: (0, 0))],
        out_specs=(pl.BlockSpec((bt, 1, MIX_W), lambda i: (i, 0, 0)), state),
        compiler_params=_cparams(1),
        name="hgrn_step",
    )(h.reshape(DB, 1, W), s0, lb.reshape(1, -1), norm_g.reshape(1, -1))
    return o.reshape(DB, MIX_W), s


def _od_step_kernel(h_ref, gt_ref, bif_ref, cos_ref, sin_ref, sc0_ref, dc0_ref, dn0_ref, dm0_ref, cng_ref, dng_ref,
                    oc_ref, od_ref, sc_ref, dc_ref, dn_ref, dm_ref, *, bt):
    scale = HEAD_W ** -0.5

    def body(i, carry):
        row = h_ref[i]
        cos, sin = cos_ref[...], sin_ref[...]
        gates = gt_ref[i] + bif_ref[...]
        m_row = dm0_ref[i]
        heads = range(N_HEADS)
        blk = lambda n: row[:, n * MIX_W:(n + 1) * MIX_W]
        q = _rope(blk(0), cos, sin, HEAD_W // 2)
        k = _rope(blk(1), cos, sin, HEAD_W // 2) * scale
        cv, dk, dv, dq = blk(2), blk(5), blk(6), blk(4) * scale
        lf = _log_sigmoid(gates)
        m_new = [jnp.maximum(lf[:, N_HEADS + h:N_HEADS + h + 1] + m_row[:, h:h + 1], gates[:, h:h + 1])
                 for h in heads]
        c_scale = [jnp.exp(lf[:, N_HEADS + h:N_HEADS + h + 1] + m_row[:, h:h + 1] - m_new[h]) for h in heads]
        kw = [_head(dk, h) * jnp.exp(gates[:, h:h + 1] - m_new[h]) for h in heads]
        k_col = [_col(_head(k, h)) for h in heads]
        kw_col = [_col(kw[h]) for h in heads]
        s_new = [math.exp(_log_gamma(h)) * sc0_ref[i, h] + k_col[h] * _head(cv, h) for h in heads]
        c_new = [c_scale[h] * dc0_ref[i, h] + kw_col[h] * _head(dv, h) for h in heads]
        n_new = [c_scale[h] * dn0_ref[i, h:h + 1, :] + kw[h] for h in heads]
        for h in heads:
            sc_ref[i, h] = s_new[h]
            dc_ref[i, h] = c_new[h]
            dn_ref[i, h:h + 1, :] = n_new[h]
        rows8 = lambda x: jnp.broadcast_to(x, (SUBLANES, HEAD_W))
        o = [_bdot(rows8(_head(q, h)), s_new[h])[0:1, :] for h in heads]
        num = [_bdot(rows8(_head(dq, h)), c_new[h])[0:1, :] for h in heads]
        den = [jnp.sum(_head(dq, h) * n_new[h], axis=1, keepdims=True) for h in heads]
        hh = [num[h] / jnp.maximum(jnp.abs(den[h]), jnp.exp(-m_new[h])) for h in heads]
        oc = jnp.concatenate([_groupnorm(o[h], cng_ref[...]) for h in heads], axis=1)
        od = jnp.concatenate([_groupnorm(hh[h], dng_ref[...]) for h in heads], axis=1)
        oc_ref[i] = oc * _silu(blk(3))
        od_ref[i] = od * _sigmoid(blk(7))
        dm_ref[i] = jnp.concatenate(m_new, axis=1)
        return carry

    lax.fori_loop(0, bt, body, 0)


def _od_step(h, gates, b_if, cos_t, sin_t, sc0, dc0, dn0, dm0, c_norm_g, d_norm_g, bt):
    DB, W = h.shape
    bt = min(bt, DB)
    mat = pl.BlockSpec((bt, N_HEADS, HEAD_W, HEAD_W), lambda i: (i, 0, 0, 0))
    nblk = pl.BlockSpec((bt, N_HEADS, HEAD_W), lambda i: (i, 0, 0))
    mblk = pl.BlockSpec((bt, 1, N_HEADS), lambda i: (i, 0, 0))
    vec = pl.BlockSpec((1, LANES), lambda i: (0, 0))
    out = pl.BlockSpec((bt, 1, MIX_W), lambda i: (i, 0, 0))
    b_pad = jnp.zeros((1, LANES), F32).at[0, :2 * N_HEADS].set(b_if)
    oc, od, sc, dc, dn, dm = pl.pallas_call(
        functools.partial(_od_step_kernel, bt=bt),
        out_shape=(jax.ShapeDtypeStruct((DB, 1, MIX_W), F32), jax.ShapeDtypeStruct((DB, 1, MIX_W), F32),
                   jax.ShapeDtypeStruct(sc0.shape, F32), jax.ShapeDtypeStruct(dc0.shape, F32),
                   jax.ShapeDtypeStruct(dn0.shape, F32), jax.ShapeDtypeStruct((DB, 1, N_HEADS), F32)),
        grid=(DB // bt,),
        in_specs=[pl.BlockSpec((bt, 1, W), lambda i: (i, 0, 0)),
                  pl.BlockSpec((bt, 1, LANES), lambda i: (i, 0, 0)), vec, vec, vec,
                  mat, mat, nblk, mblk, vec, vec],
        out_specs=(out, out, mat, mat, nblk, mblk),
        compiler_params=_cparams(1),
        name="od_step",
    )(h.reshape(DB, 1, W), gates.reshape(DB, 1, LANES), b_pad, cos_t, sin_t, sc0, dc0, dn0,
      dm0.reshape(DB, 1, N_HEADS), c_norm_g.reshape(1, -1), d_norm_g.reshape(1, -1))
    return oc.reshape(DB, MIX_W), od.reshape(DB, MIX_W), sc, dc, dn, dm.reshape(DB, N_HEADS)


X_HALVES = DH_X // LANES
X_ROWS = N_HEADS * X_HALVES


def _xattn_sample_kernel(q_ref, mk_ref, mv_ref, o_ref, *, xb):
    for t in range(xb):
        _xattn_sample_row(q_ref.at[t], mk_ref.at[t], mv_ref.at[t], o_ref.at[t])


def _xattn_sample_row(q_ref, mk_ref, mv_ref, o_ref):
    q = q_ref[...] * (DH_X ** -0.5)
    n_cols = mk_ref.shape[0]
    q2 = jnp.concatenate([q[:, h * DH_X + c * LANES:h * DH_X + (c + 1) * LANES]
                          for c in range(X_HALVES) for h in range(N_HEADS)], axis=0)
    s2 = _bdot_nt(q2, mk_ref[...])
    row = lax.broadcasted_iota(jnp.int32, (X_ROWS, n_cols), 0)
    col = lax.broadcasted_iota(jnp.int32, (X_ROWS, n_cols), 1)
    own = (col & (X_ROWS - 1)) == row
    s2 = jnp.where(own, s2, 0.0)
    s4 = s2[:N_HEADS] + pltpu.roll(s2, n_cols - N_HEADS, 1)[N_HEADS:]
    own4 = own[:N_HEADS]
    s4 = jnp.where(own4, s4, NEG_INF)
    p4 = jnp.exp(s4 - jnp.max(s4, axis=1, keepdims=True))
    l4 = jnp.sum(p4, axis=1, keepdims=True)
    p4 = p4 / l4
    p2 = jnp.concatenate([p4, pltpu.roll(p4, N_HEADS, 1)], axis=0)
    o2 = _bdot(p2, mv_ref[...])
    o_ref[...] = jnp.concatenate([o2[c * N_HEADS + h:c * N_HEADS + h + 1, :]
                                  for h in range(N_HEADS) for c in range(X_HALVES)], axis=1)


def _xattn_sample(q, mk, mv, base):
    DB = q.shape[0]
    n_rows = mk.shape[1]
    xb = math.gcd(DB, 4)
    tok = pl.BlockSpec((xb, 1, D_MODEL), lambda b: (b, 0, 0))
    mem = pl.BlockSpec((xb, n_rows, LANES), lambda b: (base // xb + b, 0, 0))
    assert base % xb == 0
    out = pl.pallas_call(
        functools.partial(_xattn_sample_kernel, xb=xb),
        out_shape=jax.ShapeDtypeStruct((DB, 1, D_MODEL), F32),
        grid=(DB // xb,),
        in_specs=[tok, mem, mem],
        out_specs=tok,
        compiler_params=_cparams(1),
        name="xattn_sample",
    )(q.reshape(DB, 1, D_MODEL), mk, mv)
    return out.reshape(DB, D_MODEL)


def _mem_rows(cache):
    n_l, DB, n_mem = cache.shape[:3]
    c = cache.reshape(n_l, DB, n_mem, N_HEADS, X_HALVES, LANES).transpose(0, 1, 2, 4, 3, 5)
    return c.reshape(n_l * DB, n_mem * X_ROWS, LANES)


def _ffn_sample_kernel(x_ref, up_ref, gate_ref, b0_ref, b1_ref, cw_ref, cb_ref, wd_ref, g_ref, b_ref, o_ref, acc_ref):
    j = pl.program_id(0)
    conv = (cb_ref[...] + cw_ref[0:1, :] * b0_ref[...] + cw_ref[1:2, :] * b1_ref[...]
            + cw_ref[2:3, :] * up_ref[...])
    y = _bdot(jax.nn.gelu(conv) * gate_ref[...], wd_ref[...])

    @pl.when(j == 0)
    def _():
        acc_ref[...] = y

    @pl.when(j > 0)
    def _():
        acc_ref[...] = acc_ref[...] + y

    @pl.when(j == pl.num_programs(0) - 1)
    def _():
        o_ref[...] = _layernorm(DN_ALPHA * x_ref[...] + acc_ref[...], g_ref[...], b_ref[...])


def _ffn_sample(x, ug, buf0, buf1, conv_w, conv_b, w_down, g, b, tf):
    DB = x.shape[0]
    d_ff = _wshape(w_down)[0]
    nf = d_ff // tf
    ff = lambda off: pl.BlockSpec((DB, tf), lambda j: (0, off + j))
    full = lambda shape: pl.BlockSpec(shape, lambda j: (0,) * len(shape))
    w_down, down_spec = _wspec(w_down, (tf, D_MODEL), lambda j: (j, 0))
    return pl.pallas_call(
        _ffn_sample_kernel,
        out_shape=jax.ShapeDtypeStruct((DB, D_MODEL), F32),
        grid=(nf,),
        in_specs=[full((DB, D_MODEL)), ff(0), ff(nf), ff(0), ff(0),
                  pl.BlockSpec((3, tf), lambda j: (0, j)), pl.BlockSpec((1, tf), lambda j: (0, j)),
                  down_spec, full((1, D_MODEL)), full((1, D_MODEL))],
        out_specs=full((DB, D_MODEL)),
        scratch_shapes=[pltpu.VMEM((DB, D_MODEL), F32)],
        compiler_params=_cparams(1),
        name="ffn_sample",
    )(x, ug, ug, buf0, buf1, conv_w, conv_b.reshape(1, -1), w_down, g.reshape(1, -1), b.reshape(1, -1))


def kernel(x_prompt, x_sample, cache_a_k, cache_a_v, state_b, state_c, state_d_c, state_d_n, state_d_m,
           cache_mem_k, cache_mem_v, state_conv, page_table, mem_prompt,
           ev_w_in, ev_w_out, ev_lam, ev_subln_g, ev_lb_logits, ev_b_norm_g,
           od_w_in, od_b_if, od_w_out, od_c_norm_g, od_d_norm_g,
           ln_g, ln_b, xa_wq, xa_wkv, xa_wo, ffn_w_up, ffn_conv_w, ffn_conv_b, ffn_w_down):
    B, T, _ = x_prompt.shape
    DB, t_s, _ = x_sample.shape
    assert t_s == 1, "the sample group is a single decoding step"
    assert T % CHUNK == 0
    n_pool, page = cache_a_k.shape[1], cache_a_k.shape[2]
    n_pages = page_table.shape[1]
    past = n_pages * page
    n_mem = mem_prompt.shape[1]
    d_ff = ffn_w_down.shape[1]
    tf = d_ff // 2
    N = B * T
    TM = 512

    pos_p = jnp.arange(T)
    pos_s = jnp.full((DB,), past, jnp.int32)
    lb_table = jnp.cumsum(jax.nn.softmax(ev_lb_logits.astype(F32), axis=0), axis=0)

    xp = x_prompt.reshape(N, D_MODEL)
    xs = x_sample.reshape(DB, D_MODEL)
    outs = {k: [] for k in ("ak_p", "av_p", "ak_s", "av_s", "sb_p", "sb_s", "sc_p", "sc_s", "dc_p", "dc_s",
                            "dn_p", "dn_s", "dm_p", "dm_s", "mk_p", "mv_p", "cv_p", "cv_s")}
    swap = lambda s: jnp.swapaxes(s, -1, -2)
    mem_k_rows, mem_v_rows = _mem_rows(cache_mem_k), _mem_rows(cache_mem_v)
    wq_all, wkv_all, wo_all = xa_wq.astype(BF16), xa_wkv.astype(BF16), xa_wo.astype(BF16)
    w_up_all, w_down_all = ffn_w_up.astype(BF16), ffn_w_down.astype(BF16)

    for l in range(DEPTH):
        j = l // 2
        if l % 2 == 0:
            lam_init = 0.8 - 0.6 * math.exp(-0.3 * l)
            w_in = ev_w_in[j].astype(BF16)
            w_out = ev_w_out[j].astype(BF16)
            cos_t, sin_t = _rope_tables(pos_p, DH_A)
            qt, kb, kt32, vt, v_rows, h_b = _ev_in_prompt(xp.reshape(B, T, D_MODEL), w_in, cos_t, sin_t, TM)
            o_a = _dattn_prompt(qt, kb, vt, ev_lam[j], ev_subln_g[j], lam_init, 512)
            o_b, st = _hgrn_prompt(h_b, lb_table[j], 512)
            outs["ak_p"].append(kt32.reshape(B, N_HEADS, 2, DH_A, T).transpose(0, 4, 1, 2, 3))
            outs["av_p"].append(v_rows.reshape(B, T, N_HEADS, HEAD_W))
            outs["sb_p"].append(swap(st))
            xp = _proj_ln([o_a.reshape(N, MIX_W), o_b.reshape(N, MIX_W)], w_out,
                          xp, ln_g[l, 0], ln_b[l, 0], TM, "ev_out_prompt",
                          gates=[None, ("rms", ev_b_norm_g[j], "silu", h_b.reshape(N, -1), 3)])
            hs = _matmul(xs, w_in, DB, 512, "ev_in_sample")
            cos_s, sin_s = _rope_tables(pos_s, DH_A)
            qs, ks32, kts32 = _ev_prep_sample(hs, cos_s, sin_s)
            vs32 = hs[:, 2 * MIX_W:3 * MIX_W]
            kt_pool = cache_a_k.transpose(0, 1, 3, 4, 5, 2).reshape(-1, MIX_W, page)
            v_pool = cache_a_v.reshape(-1, page * N_HEADS, HEAD_W)
            oa_s = _dattn_sample(qs, ks32, vs32, kt_pool, v_pool, j * n_pool, page_table,
                                 ev_lam[j], ev_subln_g[j], lam_init)
            ob_s, sb_s = _hgrn_step(hs, state_b[j], lb_table[j], ev_b_norm_g[j], 8)
            outs["ak_s"].append(kts32.reshape(N_HEADS, 2, DH_A, DB).transpose(3, 0, 1, 2)[:, None])
            outs["av_s"].append(vs32.reshape(DB, 1, N_HEADS, HEAD_W))
            outs["sb_s"].append(sb_s)
            xs = _proj_ln([oa_s, ob_s], w_out, xs, ln_g[l, 0], ln_b[l, 0], DB, "ev_out_sample")
        else:
            n_main = 8 * MIX_W
            w_in = od_w_in[j][:, :n_main].astype(BF16)
            w_gate = jnp.pad(od_w_in[j][:, n_main:], ((0, 0), (0, LANES - 2 * N_HEADS))).astype(BF16)
            w_out = od_w_out[j].astype(BF16)
            cos_t, sin_t = _rope_tables(pos_p, HEAD_W)
            h, gates = _od_in_prompt(xp, w_in, w_gate, cos_t, sin_t, TM)
            o_c, o_d, sct, dct, dn, dm = _od_prompt(h.reshape(B, T, -1), gates.reshape(B, T, -1), od_b_if[j], 512)
            outs["sc_p"].append(swap(sct))
            outs["dc_p"].append(swap(dct))
            outs["dn_p"].append(dn)
            outs["dm_p"].append(dm)
            xp = _proj_ln([o_c.reshape(N, MIX_W), o_d.reshape(N, MIX_W)], w_out,
                          xp, ln_g[l, 0], ln_b[l, 0], TM, "od_out_prompt",
                          gates=[("group", od_c_norm_g[j], "silu", h, 3), ("group", od_d_norm_g[j], "sigmoid", h, 7)])
            hs = _matmul(xs, w_in, DB, 512, "od_in_sample")
            gates_s = _matmul(xs, w_gate, DB, LANES, "od_gates_sample")
            cos_s, sin_s = _rope_tables(pos_s[:1], HEAD_W)
            oc_s, od_s, sc_s, dc_s, dn_s, dm_s = _od_step(hs, gates_s, od_b_if[j], cos_s, sin_s,
                                                          state_c[j], state_d_c[j], state_d_n[j], state_d_m[j],
                                                          od_c_norm_g[j], od_d_norm_g[j], 8)
            outs["sc_s"].append(sc_s)
            outs["dc_s"].append(dc_s)
            outs["dn_s"].append(dn_s)
            outs["dm_s"].append(dm_s)
            xs = _proj_ln([oc_s, od_s], w_out, xs, ln_g[l, 0], ln_b[l, 0], DB, "od_out_sample")

        wq, wo = (wq_all, l), (wo_all, l)
        mkv = _matmul(mem_prompt.reshape(B * n_mem, D_MODEL), (wkv_all, l), 512, 512, "mem_kv")
        mk, mv = mkv[:, :D_MODEL], mkv[:, D_MODEL:]
        outs["mk_p"].append(mk.reshape(B, n_mem, N_HEADS, DH_X))
        outs["mv_p"].append(mv.reshape(B, n_mem, N_HEADS, DH_X))
        xp = _xattn_prompt(xp, wq, mk.astype(BF16).reshape(B, n_mem, D_MODEL),
                           mv.astype(BF16).reshape(B, n_mem, D_MODEL), wo, ln_g[l, 1], ln_b[l, 1], T, 1024)
        q_s = _matmul(xs, wq, DB, 512, "xattn_q_sample")
        xo_s = _xattn_sample(q_s, mem_k_rows, mem_v_rows, l * DB)
        xs = _proj_ln([xo_s], wo, xs, ln_g[l, 1], ln_b[l, 1], DB, "xattn_out_sample")

        w_up, w_down = (w_up_all, l), (w_down_all, l)
        tm_f = min(TM, T)
        xp, tails = _ffn_prompt(xp, w_up, ffn_conv_w[l], ffn_conv_b[l], w_down, ln_g[l, 2], ln_b[l, 2], T, TM, d_ff)
        tails = tails.reshape(B, T // tm_f, SUBLANES, d_ff)
        outs["cv_p"].append(tails[:, -1, SUBLANES - 2:, :])
        ug_s = _matmul(xs, w_up, DB, tf, "ffn_up_sample")
        buf = state_conv[l]
        xs = _ffn_sample(xs, ug_s, buf[:, 0, :], buf[:, 1, :], ffn_conv_w[l], ffn_conv_b[l], w_down,
                         ln_g[l, 2], ln_b[l, 2], tf)
        outs["cv_s"].append(jnp.stack([buf[:, 1, :], ug_s[:, :d_ff]], axis=1))

    st = lambda k: jnp.stack(outs[k])
    return (xp.reshape(B, T, D_MODEL), xs.reshape(DB, 1, D_MODEL),
            st("ak_p"), st("av_p"), st("ak_s"), st("av_s"), st("sb_p"), st("sb_s"),
            st("sc_p"), st("sc_s"), st("dc_p"), st("dc_s"), st("dn_p"), st("dn_s"), st("dm_p"), st("dm_s"),
            st("mk_p"), st("mv_p"), st("cv_p"), st("cv_s"))
```

```python
import functools
import math

import numpy as np
import jax
import jax.numpy as jnp
from jax import lax
from jax.experimental import pallas as pl
from jax.experimental.pallas import tpu as pltpu

F32 = jnp.float32
BF16 = jnp.bfloat16

D_MODEL = 1024
MIX_W = D_MODEL // 2
N_HEADS = 4
HEAD_W = MIX_W // N_HEADS
DH_A = HEAD_W // 2
DH_X = D_MODEL // N_HEADS
CHUNK = 64
OD_CHUNK = 128
ROPE_THETA = 10000.0
NORM_EPS = 1e-5
DEPTH = 2
DN_ALPHA = (2.0 * DEPTH) ** 0.25
LANES = 128
SUBLANES = 8
VMEM_LIMIT = 56 * 1024 * 1024
NEG_INF = float("-inf")

NT_DIMS = (((1,), (1,)), ((), ()))
TN_DIMS = (((0,), (0,)), ((), ()))


def _wspec(w, block_shape, index_map, **kw):
    if isinstance(w, tuple):
        stack, layer = w
        return stack, pl.BlockSpec((None,) + tuple(block_shape), lambda *a: (layer,) + tuple(index_map(*a)), **kw)
    return w, pl.BlockSpec(tuple(block_shape), index_map, **kw)


def _wshape(w):
    return w[0].shape[1:] if isinstance(w, tuple) else w.shape


def _cparams(n_axes, vmem=VMEM_LIMIT):
    return pltpu.CompilerParams(dimension_semantics=("arbitrary",) * n_axes, vmem_limit_bytes=vmem)


def _bdot(a, b):
    return jnp.dot(a.astype(BF16), b.astype(BF16), preferred_element_type=F32)


def _bdot_nt(a, b):
    return lax.dot_general(a.astype(BF16), b.astype(BF16), NT_DIMS, preferred_element_type=F32)


def _bdot_tn(a, b):
    return lax.dot_general(a.astype(BF16), b.astype(BF16), TN_DIMS, preferred_element_type=F32)


def _split3(x):
    p1 = x.astype(BF16)
    r1 = x - p1.astype(F32)
    p2 = r1.astype(BF16)
    p3 = (r1 - p2.astype(F32)).astype(BF16)
    return p1, p2, p3


def _cumsum_rows(tri, x):
    p1, p2, p3 = _split3(x)
    d = functools.partial(jnp.dot, preferred_element_type=F32)
    return d(tri, p1) + d(tri, p2) + d(tri, p3)


def _cumsum_lanes(x, triu):
    p1, p2, p3 = _split3(x)
    d = functools.partial(jnp.dot, preferred_element_type=F32)
    return d(p1, triu) + d(p2, triu) + d(p3, triu)


def _tri(L, lower):
    r = lax.broadcasted_iota(jnp.int32, (L, L), 0)
    c = lax.broadcasted_iota(jnp.int32, (L, L), 1)
    return (r >= c) if lower else (r <= c)


def _sigmoid(x):
    return 1.0 / (1.0 + jnp.exp(-x))


def _silu(x):
    return x * _sigmoid(x)


def _log_sigmoid(x):
    return jnp.minimum(x, 0.0) - jnp.log(1.0 + jnp.exp(-jnp.abs(x)))


def _layernorm(z, g, b):
    mu = jnp.mean(z, -1, keepdims=True)
    zc = z - mu
    var = jnp.mean(zc * zc, -1, keepdims=True)
    return zc * lax.rsqrt(var + NORM_EPS) * g + b


def _rmsnorm(x, g):
    return x * lax.rsqrt(jnp.mean(x * x, -1, keepdims=True) + NORM_EPS) * g


def _groupnorm(x, g):
    mu = jnp.mean(x, -1, keepdims=True)
    xc = x - mu
    var = jnp.mean(xc * xc, -1, keepdims=True)
    return xc * lax.rsqrt(var + NORM_EPS) * g


def _rope(x, cos, sin, half):
    outs = []
    for c in range(x.shape[1] // LANES):
        xc = x[:, c * LANES:(c + 1) * LANES]
        if 2 * half == LANES:
            sw = pltpu.roll(xc, half, 1)
        else:
            lane = lax.broadcasted_iota(jnp.int32, xc.shape, 1)
            first = (lane & (2 * half - 1)) < half
            sw = jnp.where(first, pltpu.roll(xc, LANES - half, 1), pltpu.roll(xc, half, 1))
        outs.append(xc * cos + sw * sin)
    return outs[0] if len(outs) == 1 else jnp.concatenate(outs, axis=1)


def _rope_tables(pos, d):
    inv = ROPE_THETA ** (-jnp.arange(0, d // 2, dtype=F32) * 2.0 / d)
    ang = pos.astype(F32)[:, None] * inv[None, :]
    cos, sin = jnp.cos(ang), jnp.sin(ang)
    reps = LANES // d
    cos_t = jnp.tile(jnp.concatenate([cos, cos], -1), (1, reps))
    sin_t = jnp.tile(jnp.concatenate([-sin, sin], -1), (1, reps))
    return cos_t, sin_t


def _col(row):
    return jnp.broadcast_to(row, (LANES, LANES)).T


def _mm_kernel(x_ref, w_ref, o_ref, xb_ref):
    @pl.when(pl.program_id(1) == 0)
    def _():
        xb_ref[...] = x_ref[...].astype(BF16)

    o_ref[...] = jnp.dot(xb_ref[...], w_ref[...], preferred_element_type=F32).astype(o_ref.dtype)


def _matmul(x, w, tm, tn, name, out_dtype=F32):
    M, K = x.shape
    N = _wshape(w)[1]
    tm, tn = min(tm, M), min(tn, N)
    w, w_spec = _wspec(w, (K, tn), lambda i, j: (0, j))
    return pl.pallas_call(
        _mm_kernel,
        out_shape=jax.ShapeDtypeStruct((M, N), out_dtype),
        grid=(M // tm, N // tn),
        in_specs=[pl.BlockSpec((tm, K), lambda i, j: (i, 0)), w_spec],
        out_specs=pl.BlockSpec((tm, tn), lambda i, j: (i, j)),
        scratch_shapes=[pltpu.VMEM((tm, K), BF16)],
        compiler_params=_cparams(2),
        name=name,
    )(x, w)


_HEAD_NORMS = {"rms": _rmsnorm, "group": _groupnorm}
_GATE_ACTS = {"silu": _silu, "sigmoid": _sigmoid}


def _proj_ln_kernel(*refs, n_in, gated):
    a_refs, w_refs = refs[:n_in], refs[n_in:2 * n_in]
    x_ref, g_ref, b_ref = refs[2 * n_in:2 * n_in + 3]
    extra, o_ref = refs[2 * n_in + 3:-1], refs[-1]
    y = None
    for a_ref, w_ref, gate in zip(a_refs, w_refs, gated):
        a = a_ref[...]
        if gate is not None:
            norm, act = _HEAD_NORMS[gate[0]], _GATE_ACTS[gate[1]]
            gain_ref, gate_ref, extra = extra[0], extra[1], extra[2:]
            a = jnp.concatenate([norm(_head(a, h), gain_ref[...]) for h in range(N_HEADS)], axis=1)
            a = a * act(gate_ref[...])
        ya = _bdot(a, w_ref[...])
        y = ya if y is None else y + ya
    o_ref[...] = _layernorm(DN_ALPHA * x_ref[...] + y, g_ref[...], b_ref[...])


def _proj_ln(acts, weight, x, g, b, tm, name, gates=None):
    M = x.shape[0]
    tm = min(tm, M)
    n_in = len(acts)
    gates = gates or [None] * n_in
    in_specs = [pl.BlockSpec((tm, a.shape[1]), lambda i: (i, 0)) for a in acts]
    w_args = []
    for r, a in enumerate(acts):
        w_arr, w_spec = _wspec(weight, (a.shape[1], D_MODEL), lambda i, r=r: (r, 0))
        w_args.append(w_arr)
        in_specs.append(w_spec)
    in_specs += [pl.BlockSpec((tm, D_MODEL), lambda i: (i, 0)),
                 pl.BlockSpec((1, D_MODEL), lambda i: (0, 0)),
                 pl.BlockSpec((1, D_MODEL), lambda i: (0, 0))]
    extra = []
    for gate in gates:
        if gate is not None:
            _, gain, _, gate_arr, blk = gate
            in_specs += [pl.BlockSpec((1, HEAD_W), lambda i: (0, 0)),
                         pl.BlockSpec((tm, MIX_W), lambda i, blk=blk: (i, blk))]
            extra += [gain.reshape(1, -1), gate_arr]
    return pl.pallas_call(
        functools.partial(_proj_ln_kernel, n_in=n_in,
                          gated=tuple(None if gt is None else (gt[0], gt[2]) for gt in gates)),
        out_shape=jax.ShapeDtypeStruct((M, D_MODEL), F32),
        grid=(M // tm,),
        in_specs=in_specs,
        out_specs=pl.BlockSpec((tm, D_MODEL), lambda i: (i, 0)),
        compiler_params=_cparams(1),
        name=name,
    )(*acts, *w_args, x, g.reshape(1, -1), b.reshape(1, -1), *extra)


Q_SCALE = DH_A ** -0.5 * math.log2(math.e)
VT_ROWS = HEAD_W + 16


def _ev_in_prompt_kernel(x_ref, w_ref, cos_ref, sin_ref, qt_ref, kb_ref, kt_ref, vt_ref, vr_ref, hb_ref):
    xb = x_ref[0].astype(BF16)
    cos, sin = cos_ref[...], sin_ref[...]
    proj = lambda lo, hi: jnp.dot(xb, w_ref[:, lo * MIX_W:hi * MIX_W], preferred_element_type=F32)
    hb_ref[0] = proj(3, 7)
    q = _rope(proj(0, 1), cos, sin, DH_A // 2)
    k = _rope(proj(1, 2), cos, sin, DH_A // 2)
    qt_ref[0] = (q * Q_SCALE).T.astype(BF16)
    kb_ref[0] = k.astype(BF16)
    kt_ref[0] = k.T
    v = proj(2, 3)
    tm = v.shape[0]
    vt = v.T.astype(BF16)
    ones = jnp.ones((VT_ROWS - HEAD_W, tm), BF16)
    for h in range(N_HEADS):
        vt_ref[0, h, :HEAD_W, :] = vt[h * HEAD_W:(h + 1) * HEAD_W]
        vt_ref[0, h, HEAD_W:, :] = ones
        vr_ref[0, pl.ds(h, tm, stride=N_HEADS), :] = _head(v, h)


def _ev_in_prompt(x3, w_in, cos_t, sin_t, tm):
    B, T, _ = x3.shape
    tm = min(tm, T)
    tab = pl.BlockSpec((tm, LANES), lambda b, i: (i, 0))
    tr = pl.BlockSpec((1, MIX_W, tm), lambda b, i: (b, 0, i))
    tshape = lambda dt: jax.ShapeDtypeStruct((B, MIX_W, T), dt)
    return pl.pallas_call(
        _ev_in_prompt_kernel,
        out_shape=(tshape(BF16), jax.ShapeDtypeStruct((B, T, MIX_W), BF16), tshape(F32),
                   jax.ShapeDtypeStruct((B, N_HEADS, VT_ROWS, T), BF16),
                   jax.ShapeDtypeStruct((B, T * N_HEADS, HEAD_W), F32),
                   jax.ShapeDtypeStruct((B, T, 4 * MIX_W), F32)),
        grid=(B, T // tm),
        in_specs=[pl.BlockSpec((1, tm, D_MODEL), lambda b, i: (b, i, 0)),
                  pl.BlockSpec(w_in.shape, lambda b, i: (0, 0)), tab, tab],
        out_specs=(tr, pl.BlockSpec((1, tm, MIX_W), lambda b, i: (b, i, 0)), tr,
                   pl.BlockSpec((1, N_HEADS, VT_ROWS, tm), lambda b, i: (b, 0, 0, i)),
                   pl.BlockSpec((1, tm * N_HEADS, HEAD_W), lambda b, i: (b, i, 0)),
                   pl.BlockSpec((1, tm, 4 * MIX_W), lambda b, i: (b, i, 0))),
        compiler_params=_cparams(2),
        name="ev_in_prompt",
    )(x3, w_in, cos_t, sin_t)


def _ev_prep_sample_kernel(qk_ref, cos_ref, sin_ref, q_ref, k_ref, kt_ref):
    cos, sin = cos_ref[...], sin_ref[...]
    k = _rope(qk_ref[:, MIX_W:], cos, sin, DH_A // 2)
    q_ref[...] = _rope(qk_ref[:, :MIX_W], cos, sin, DH_A // 2) * Q_SCALE
    k_ref[...] = k
    kt_ref[...] = k.T


def _ev_prep_sample(h, cos_t, sin_t):
    DB = h.shape[0]
    full = lambda shape: pl.BlockSpec(shape, lambda i: (0,) * len(shape))
    return pl.pallas_call(
        _ev_prep_sample_kernel,
        out_shape=(jax.ShapeDtypeStruct((DB, MIX_W), F32), jax.ShapeDtypeStruct((DB, MIX_W), F32),
                   jax.ShapeDtypeStruct((MIX_W, DB), F32)),
        grid=(1,),
        in_specs=[full((DB, 2 * MIX_W)), full((DB, LANES)), full((DB, LANES))],
        out_specs=(full((DB, MIX_W)), full((DB, MIX_W)), full((MIX_W, DB))),
        compiler_params=_cparams(1),
        name="ev_prep_sample",
    )(h, cos_t, sin_t)


def _lambda(lam_ref, lam_init):
    lp = lam_ref[...]
    s01 = jnp.sum(lp[0:1] * lp[1:2], axis=1, keepdims=True)
    s23 = jnp.sum(lp[2:3] * lp[3:4], axis=1, keepdims=True)
    return jnp.exp(s01) - jnp.exp(s23) + lam_init


def _dattn_kernel(lam_ref, g_ref, qt_ref, k_ref, vt_ref, o_ref, m_ref, acc_ref, *, tq, lam_init):
    i = pl.program_id(2)
    qt = qt_ref[0]
    sub = lax.broadcasted_iota(jnp.int32, qt.shape, 0)
    zero = jnp.zeros_like(qt)
    q_maps = [jnp.where(sub < DH_A, qt, zero), jnp.where(sub >= DH_A, qt, zero)]
    maps = range(2)
    m_ref[...] = jnp.full(m_ref.shape, NEG_INF, F32)
    acc_ref[...] = jnp.zeros(acc_ref.shape, F32)

    def steps(js, masked):
        ks, vts, ss = [], [], []
        for j in js:
            start = pl.multiple_of(j * tq, tq)
            ks.append(k_ref[0, pl.ds(start, tq), :])
            vts.append(vt_ref[0, 0, :, pl.ds(start, tq)])
        for kj in ks:
            ss.append([jnp.dot(kj, q_maps[c], preferred_element_type=F32) for c in maps])
        for n in range(len(js)):
            for c in maps:
                s = ss[n][c]
                if masked:
                    r = lax.broadcasted_iota(jnp.int32, s.shape, 0)
                    col = lax.broadcasted_iota(jnp.int32, s.shape, 1)
                    s = jnp.where(r <= col, s, NEG_INF)
                m_prev = m_ref[c]
                m_new = jnp.maximum(m_prev, jnp.max(s, axis=0, keepdims=True))
                alpha = jnp.exp2(m_prev - m_new)
                p = jnp.exp2(s - m_new)
                acc_ref[c] = alpha * acc_ref[c] + jnp.dot(vts[n], p.astype(BF16), preferred_element_type=F32)
                m_ref[c] = m_new

    def body(jj, carry):
        steps([2 * jj, 2 * jj + 1], False)
        return carry

    lax.fori_loop(0, lax.shift_right_logical(i, 1), body, 0)

    @pl.when((i & 1) == 1)
    def _():
        steps([i - 1], False)

    steps([i], True)
    lam = _lambda(lam_ref, lam_init)
    o = [acc_ref[c, :HEAD_W, :] / acc_ref[c, HEAD_W:HEAD_W + 1, :] for c in maps]
    d = (o[0] - lam * o[1]).T
    o_ref[0] = (_rmsnorm(d, g_ref[...]) * (1.0 - lam_init)).astype(BF16)


def _dattn_prompt(qt, kb, vt, lam_p, subln_g, lam_init, tq):
    B, T, _ = kb.shape
    tq = min(tq, T)
    return pl.pallas_call(
        functools.partial(_dattn_kernel, tq=tq, lam_init=lam_init),
        out_shape=jax.ShapeDtypeStruct((B, T, MIX_W), BF16),
        grid=(B, N_HEADS, T // tq),
        in_specs=[pl.BlockSpec((4, DH_A), lambda b, h, i: (0, 0)),
                  pl.BlockSpec((1, HEAD_W), lambda b, h, i: (0, 0)),
                  pl.BlockSpec((1, HEAD_W, tq), lambda b, h, i: (b, h, i)),
                  pl.BlockSpec((1, T, HEAD_W), lambda b, h, i: (b, 0, h)),
                  pl.BlockSpec((1, 1, VT_ROWS, T), lambda b, h, i: (b, h, 0, 0))],
        out_specs=pl.BlockSpec((1, tq, HEAD_W), lambda b, h, i: (b, i, h)),
        scratch_shapes=[pltpu.VMEM((2, 1, tq), F32), pltpu.VMEM((2, VT_ROWS, tq), F32)],
        compiler_params=_cparams(3),
        name="dattn_prompt",
    )(lam_p, subln_g.reshape(1, -1), qt, kb, vt)


def _head(x, h):
    return x[:, h * HEAD_W:(h + 1) * HEAD_W]


def _gla_chunk(seqs, st_ref, tril_mask):
    seqs = [tuple(x.astype(BF16) for x in s[:4]) + (s[4],) for s in seqs]
    chains = [(n, h) for n in range(len(seqs)) for h in range(N_HEADS)]
    part = lambda n, i, h: _head(seqs[n][i], h)
    sts = [st_ref[N_HEADS * n + h] for n, h in chains]
    inter = [_bdot_nt(part(n, 0, h), sts[c]) for c, (n, h) in enumerate(chains)]
    upd = [_bdot_tn(part(n, 3, h), part(n, 2, h)) for n, h in chains]
    attn = [jnp.where(tril_mask, _bdot_nt(part(n, 0, h), part(n, 1, h)), 0.0) for n, h in chains]
    intra = [_bdot(attn[c], part(n, 3, h)) for c, (n, h) in enumerate(chains)]
    for c, (n, h) in enumerate(chains):
        dec = seqs[n][4]
        st_ref[N_HEADS * n + h] = sts[c] * (dec[h] if isinstance(dec, (list, tuple)) else _head(dec, h)) + upd[c]
    outs = [intra[c] + inter[c] for c in range(len(chains))]
    return [outs[N_HEADS * n:N_HEADS * (n + 1)] for n in range(len(seqs))]


def _hgrn_kernel(q_ref, f_ref, i_ref, lb_ref, o_ref, s_ref, st_ref, *, tt, L, n_seq):
    t = pl.program_id(0)

    @pl.when(t == 0)
    def _():
        st_ref[...] = jnp.zeros(st_ref.shape, F32)

    tril_mask = _tri(L, True)
    tril = tril_mask.astype(BF16)

    def body(c, carry):
        rows = pl.ds(pl.multiple_of(c * L, L), L)
        lb = lb_ref[...]
        seqs = []
        for n in range(n_seq):
            f = lb + (1.0 - lb) * _sigmoid(f_ref[n, rows, :])
            b = _cumsum_rows(tril, jnp.log(f))
            b_end = b[L - 1:L, :]
            k = 1.0 - f
            q_in = _silu(q_ref[n, rows, :]) * jnp.exp(b)
            seqs.append((q_in, k * jnp.exp(-b), k * jnp.exp(b_end - b), i_ref[n, rows, :], jnp.exp(b_end)))
        for n, o in enumerate(_gla_chunk(seqs, st_ref, tril_mask)):
            o_ref[n, rows, :] = jnp.concatenate(o, axis=1)
        return carry

    lax.fori_loop(0, tt // L, body, 0, unroll=2)

    @pl.when(t == pl.num_programs(0) - 1)
    def _():
        s_ref[...] = st_ref[...]


def _hgrn_prompt(h3, lb, tt):
    B, T, _ = h3.shape
    tt = min(tt, T)
    L = math.gcd(T, CHUNK)
    col = lambda c: pl.BlockSpec((B, tt, MIX_W), lambda t: (0, t, c))
    o, st = pl.pallas_call(
        functools.partial(_hgrn_kernel, tt=tt, L=L, n_seq=B),
        out_shape=(jax.ShapeDtypeStruct((B, T, MIX_W), F32),
                   jax.ShapeDtypeStruct((B * N_HEADS, HEAD_W, HEAD_W), F32)),
        grid=(T // tt,),
        in_specs=[col(0), col(1), col(2), pl.BlockSpec((1, MIX_W), lambda t: (0, 0))],
        out_specs=(pl.BlockSpec((B, tt, MIX_W), lambda t: (0, t, 0)),
                   pl.BlockSpec((B * N_HEADS, HEAD_W, HEAD_W), lambda t: (0, 0, 0))),
        scratch_shapes=[pltpu.VMEM((B * N_HEADS, HEAD_W, HEAD_W), F32)],
        compiler_params=_cparams(1),
        name="hgrn_prompt",
    )(h3, h3, h3, lb.reshape(1, -1))
    return o, st.reshape(B, N_HEADS, HEAD_W, HEAD_W)


def _log_gamma(h):
    return float(np.log(1.0 - 2.0 ** (-5.0 - h)))


def _od_kernel(cq_ref, ck_ref, cv_ref, dq_ref, dk_ref, dv_ref, gc_ref, gr_ref,
               bc_ref, br_ref,
               oc_ref, od_ref, sc_ref, dc_ref, dn_ref, dm_ref,
               st_ref, ct_ref, n_ref, m_ref, *, tt, L, n_seq):
    t = pl.program_id(0)

    @pl.when(t == 0)
    def _():
        st_ref[...] = jnp.zeros(st_ref.shape, F32)
        ct_ref[...] = jnp.zeros(ct_ref.shape, F32)
        n_ref[...] = jnp.zeros(n_ref.shape, F32)
        m_ref[...] = jnp.zeros(m_ref.shape, F32)

    tril_mask = _tri(L, True)
    tril = tril_mask.astype(BF16)
    triu = _tri(L, False).astype(BF16)
    scale = HEAD_W ** -0.5
    heads = range(N_HEADS)
    pos1 = (lax.broadcasted_iota(jnp.int32, (L, HEAD_W), 0) + 1).astype(F32)
    ret_b = jnp.concatenate([pos1 * _log_gamma(h) for h in heads], axis=1)
    ret_b_end = jnp.concatenate([jnp.full((L, HEAD_W), L * _log_gamma(h), F32) for h in heads], axis=1)
    ret_q_dec, ret_k_dec, ret_k_end = jnp.exp(ret_b), jnp.exp(-ret_b), jnp.exp(ret_b_end - ret_b)
    ret_dec = [math.exp(L * _log_gamma(h)) for h in heads]

    seqs = range(n_seq)
    chains = [(n, h) for n in seqs for h in heads]

    def body(c, carry):
        rows = pl.ds(pl.multiple_of(c * L, L), L)
        ret = []
        for n in seqs:
            q, k = cq_ref[n, rows, :], ck_ref[n, rows, :]
            ret.append((q * ret_q_dec, k * ret_k_dec, k * ret_k_end, cv_ref[n, rows, :], ret_dec))
        for n, o in enumerate(_gla_chunk(ret, st_ref, tril_mask)):
            oc_ref[n, rows, :] = jnp.concatenate(o, axis=1)
        g_c = [gc_ref[n, rows, :] + bc_ref[...] for n in seqs]
        g_r = [gr_ref[n, c] + br_ref[...] for n in seqs]
        lf_c = [_log_sigmoid(g) for g in g_c]
        b_c = [_cumsum_rows(tril, jnp.concatenate(
            [jnp.broadcast_to(lf_c[n][:, N_HEADS + h:N_HEADS + h + 1], (L, HEAD_W)) for h in heads], axis=1))
            for n in seqs]
        b_r = [_cumsum_lanes(_log_sigmoid(g), triu) for g in g_r]
        dq = [dq_ref[n, rows, :] * scale for n in seqs]
        dk = [dk_ref[n, rows, :] for n in seqs]
        dqb = [x.astype(BF16) for x in dq]
        dkb = [x.astype(BF16) for x in dk]
        dvb = [dv_ref[n, rows, :].astype(BF16) for n in seqs]
        cts = [ct_ref[i] for i in range(len(chains))]
        q_c = [_bdot_nt(_head(dqb[n], h), cts[i]) for i, (n, h) in enumerate(chains)]
        q_k = [_bdot_nt(_head(dqb[n], h), _head(dkb[n], h)) for n, h in chains]
        m_prev = [m_ref[i:i + 1, :] for i in range(len(chains))]
        bcw, m_t, w_mat = [], [], []
        for i, (n, h) in enumerate(chains):
            bcw.append(_head(b_c[n], h))
            bc = bcw[i][:, :L]
            dm = jnp.where(tril_mask, bc - b_r[n][N_HEADS + h:N_HEADS + h + 1, :] + g_r[n][h:h + 1, :], NEG_INF)
            m_t.append(jnp.maximum(bcw[i] + m_prev[i], jnp.max(dm, axis=1, keepdims=True)))
            w_mat.append(jnp.exp(dm - m_t[i][:, :L]) * q_k[i])
        w_v = [_bdot(w_mat[i], _head(dvb[n], h)) for i, (n, h) in enumerate(chains)]
        kws, c_scales = [], []
        for i, (n, h) in enumerate(chains):
            m_new = m_t[i][L - 1:L, :]
            b_last = bcw[i][L - 1:L, :]
            c_scales.append(jnp.exp(b_last + m_prev[i] - m_new))
            kws.append(_head(dk[n], h) * jnp.exp(b_last - bcw[i] + g_c[n][:, h:h + 1] - m_new))
            m_ref[i:i + 1, :] = m_new
        c_upd = [_bdot_tn(_head(dvb[n], h), kws[i]) for i, (n, h) in enumerate(chains)]
        hs = []
        for i, (n, h) in enumerate(chains):
            n_row = n_ref[i:i + 1, :]
            inter = jnp.exp(bcw[i] + m_prev[i] - m_t[i])
            num = inter * q_c[i] + w_v[i]
            den = (inter[:, 0:1] * jnp.sum(_head(dq[n], h) * n_row, axis=1, keepdims=True)
                   + jnp.sum(w_mat[i], axis=1, keepdims=True))
            hs.append(num / jnp.maximum(jnp.abs(den), jnp.exp(-m_t[i])))
            ct_ref[i] = c_scales[i] * cts[i] + c_upd[i]
            n_ref[i:i + 1, :] = c_scales[i] * n_row + jnp.sum(kws[i], axis=0, keepdims=True)
        for n in seqs:
            od_ref[n, rows, :] = jnp.concatenate(hs[N_HEADS * n:N_HEADS * (n + 1)], axis=1)
        return carry

    lax.fori_loop(0, tt // L, body, 0)

    @pl.when(t == pl.num_programs(0) - 1)
    def _():
        sc_ref[...] = st_ref[...]
        dc_ref[...] = ct_ref[...]
        dn_ref[...] = n_ref[...]
        dm_ref[...] = m_ref[...]


def _od_in_kernel(x_ref, w_ref, wg_ref, cos_ref, sin_ref, h_ref, g_ref):
    xb = x_ref[...].astype(BF16)
    cos, sin = cos_ref[...], sin_ref[...]
    proj = lambda lo, hi: jnp.dot(xb, w_ref[:, lo * MIX_W:hi * MIX_W], preferred_element_type=F32)
    h_ref[:, 2 * MIX_W:] = proj(2, 8)
    h_ref[:, :MIX_W] = _rope(proj(0, 1), cos, sin, HEAD_W // 2)
    h_ref[:, MIX_W:2 * MIX_W] = _rope(proj(1, 2), cos, sin, HEAD_W // 2) * (HEAD_W ** -0.5)
    g_ref[...] = jnp.dot(xb, wg_ref[...], preferred_element_type=F32)


def _od_in_prompt(x, w_in, w_gate, cos_t, sin_t, tm):
    M, K = x.shape
    N = w_in.shape[1]
    tm = min(tm, cos_t.shape[0])
    n_tab = cos_t.shape[0] // tm
    tab = pl.BlockSpec((tm, LANES), lambda i: (i % n_tab, 0))
    return pl.pallas_call(
        _od_in_kernel,
        out_shape=(jax.ShapeDtypeStruct((M, N), F32), jax.ShapeDtypeStruct((M, LANES), F32)),
        grid=(M // tm,),
        in_specs=[pl.BlockSpec((tm, K), lambda i: (i, 0)),
                  pl.BlockSpec((K, N), lambda i: (0, 0)),
                  pl.BlockSpec((K, LANES), lambda i: (0, 0)), tab, tab],
        out_specs=(pl.BlockSpec((tm, N), lambda i: (i, 0)), pl.BlockSpec((tm, LANES), lambda i: (i, 0))),
        compiler_params=_cparams(1),
        name="od_in_prompt",
    )(x, w_in, w_gate, cos_t, sin_t)


def _od_prompt(h3, gates, b_if, tt):
    B, T, _ = h3.shape
    tt = min(tt, T)
    L = math.gcd(T, OD_CHUNK)
    gates_r = gates[:, :, :2 * N_HEADS].reshape(B, T // L, L, 2 * N_HEADS).transpose(0, 1, 3, 2)
    b_pad = jnp.zeros((1, LANES), F32).at[0, :2 * N_HEADS].set(b_if)
    n_chain = B * N_HEADS
    n_row = -(-n_chain // SUBLANES) * SUBLANES
    col = lambda c: pl.BlockSpec((B, tt, MIX_W), lambda t: (0, t, c))
    mat_state = pl.BlockSpec((n_chain, HEAD_W, HEAD_W), lambda t: (0, 0, 0))
    row_state = pl.BlockSpec((n_row, LANES), lambda t: (0, 0))
    out_blk = pl.BlockSpec((B, tt, MIX_W), lambda t: (0, t, 0))
    mat_shape = jax.ShapeDtypeStruct((n_chain, HEAD_W, HEAD_W), F32)
    row_shape = jax.ShapeDtypeStruct((n_row, LANES), F32)
    o_c, o_d, sct, dct, dn, dm = pl.pallas_call(
        functools.partial(_od_kernel, tt=tt, L=L, n_seq=B),
        out_shape=(jax.ShapeDtypeStruct((B, T, MIX_W), F32), jax.ShapeDtypeStruct((B, T, MIX_W), F32),
                   mat_shape, mat_shape, row_shape, row_shape),
        grid=(T // tt,),
        in_specs=[col(0), col(1), col(2), col(4), col(5), col(6),
                  pl.BlockSpec((B, tt, LANES), lambda t: (0, t, 0)),
                  pl.BlockSpec((B, tt // L, 2 * N_HEADS, L), lambda t: (0, t, 0, 0)),
                  pl.BlockSpec((1, LANES), lambda t: (0, 0)),
                  pl.BlockSpec((2 * N_HEADS, 1), lambda t: (0, 0))],
        out_specs=(out_blk, out_blk, mat_state, mat_state, row_state, row_state),
        scratch_shapes=[pltpu.VMEM(mat_shape.shape, F32), pltpu.VMEM(mat_shape.shape, F32),
                        pltpu.VMEM(row_shape.shape, F32), pltpu.VMEM(row_shape.shape, F32)],
        compiler_params=_cparams(1),
        name="od_prompt",
    )(h3, h3, h3, h3, h3, h3, gates, gates_r, b_pad, b_if.reshape(-1, 1))
    per_seq = lambda s: s.reshape((B, N_HEADS) + s.shape[1:])
    return (o_c, o_d, per_seq(sct), per_seq(dct), per_seq(dn[:n_chain]), per_seq(dm[:n_chain, 0]))


def _xattn_kernel(x_ref, wq_ref, mk_ref, mv_ref, wo_ref, g_ref, b_ref, o_ref):
    x = x_ref[...]
    q = _bdot(x, wq_ref[...])
    qb = (q * (DH_X ** -0.5)).astype(BF16)
    heads = range(N_HEADS)
    cols = [slice(h * DH_X, (h + 1) * DH_X) for h in heads]
    s = [lax.dot_general(qb[:, cols[h]], mk_ref[0, :, cols[h]], NT_DIMS, preferred_element_type=F32) for h in heads]
    p = [jnp.exp(s[h] - jnp.max(s[h], axis=1, keepdims=True)) for h in heads]
    l = [jnp.sum(p[h], axis=1, keepdims=True) for h in heads]
    pv = [jnp.dot(p[h].astype(BF16), mv_ref[0, :, cols[h]], preferred_element_type=F32) for h in heads]
    outs = [(pv[h] / l[h]).astype(BF16) for h in heads]
    y = jnp.dot(jnp.concatenate(outs, axis=1), wo_ref[...], preferred_element_type=F32)
    o_ref[...] = _layernorm(DN_ALPHA * x + y, g_ref[...], b_ref[...])


def _xattn_prompt(x, wq, mk, mv, wo, g, b, T, tm):
    M = x.shape[0]
    tm = min(tm, T)
    n_mem = mk.shape[1]
    per_b = T // tm
    full = lambda shape: pl.BlockSpec(shape, lambda i: (0,) * len(shape))
    mem = pl.BlockSpec((1, n_mem, D_MODEL), lambda i: (i // per_b, 0, 0))
    wq, wq_spec = _wspec(wq, (D_MODEL, D_MODEL), lambda i: (0, 0))
    wo, wo_spec = _wspec(wo, (D_MODEL, D_MODEL), lambda i: (0, 0))
    return pl.pallas_call(
        _xattn_kernel,
        out_shape=jax.ShapeDtypeStruct((M, D_MODEL), F32),
        grid=(M // tm,),
        in_specs=[pl.BlockSpec((tm, D_MODEL), lambda i: (i, 0)), wq_spec, mem, mem,
                  wo_spec, full((1, D_MODEL)), full((1, D_MODEL))],
        out_specs=pl.BlockSpec((tm, D_MODEL), lambda i: (i, 0)),
        compiler_params=_cparams(1),
        name="xattn_prompt",
    )(x, wq, mk, mv, wo, g.reshape(1, -1), b.reshape(1, -1))


def _ffn_kernel(x_ref, wu_ref, wg_ref, cw_ref, cb_ref, wd_ref, g_ref, b_ref, o_ref, tail_ref,
                xb_ref, acc_ref, stage_ref, carry_ref, *, tm, per_b, single):
    i, j = pl.program_id(0), pl.program_id(1)

    @pl.when(j == 0)
    def _():
        xb_ref[...] = x_ref[...].astype(BF16)

    @pl.when(i % per_b == 0)
    def _():
        carry_ref[j] = jnp.zeros(carry_ref.shape[1:], F32)

    xb = xb_ref[...]
    stage_ref[0:SUBLANES, :] = carry_ref[j]
    stage_ref[SUBLANES:, :] = jnp.dot(xb, wu_ref[...], preferred_element_type=F32)
    last = stage_ref[tm:tm + SUBLANES, :]
    carry_ref[j] = last
    tail_ref[0] = last
    conv = (cb_ref[...] + cw_ref[0:1, :] * stage_ref[SUBLANES - 2:SUBLANES - 2 + tm, :]
            + cw_ref[1:2, :] * stage_ref[SUBLANES - 1:SUBLANES - 1 + tm, :]
            + cw_ref[2:3, :] * stage_ref[SUBLANES:, :])
    gate = jnp.dot(xb, wg_ref[...], preferred_element_type=F32)
    y = _bdot(jax.nn.gelu(conv) * gate, wd_ref[...])
    if single:
        o_ref[...] = _layernorm(DN_ALPHA * x_ref[...] + y, g_ref[...], b_ref[...])
        return

    @pl.when(j == 0)
    def _():
        acc_ref[...] = y

    @pl.when(j > 0)
    def _():
        acc_ref[...] = acc_ref[...] + y

    @pl.when(j == pl.num_programs(1) - 1)
    def _():
        o_ref[...] = _layernorm(DN_ALPHA * x_ref[...] + acc_ref[...], g_ref[...], b_ref[...])


def _ffn_prompt(x, w_up, conv_w, conv_b, w_down, g, b, T, tm, tf):
    M = x.shape[0]
    d_ff = _wshape(w_down)[0]
    tm = min(tm, T)
    nf = d_ff // tf
    per_b = T // tm
    wmode = dict(pipeline_mode=pl.Buffered(1)) if nf == 1 else {}
    w_up_arr, up_spec = _wspec(w_up, (D_MODEL, tf), lambda i, j: (0, j), **wmode)
    _, gate_spec = _wspec(w_up, (D_MODEL, tf), lambda i, j: (0, nf + j), **wmode)
    w_down_arr, down_spec = _wspec(w_down, (tf, D_MODEL), lambda i, j: (j, 0), **wmode)
    return pl.pallas_call(
        functools.partial(_ffn_kernel, tm=tm, per_b=per_b, single=nf == 1),
        out_shape=(jax.ShapeDtypeStruct((M, D_MODEL), F32),
                   jax.ShapeDtypeStruct((M // tm, SUBLANES, d_ff), F32)),
        grid=(M // tm, nf),
        in_specs=[pl.BlockSpec((tm, D_MODEL), lambda i, j: (i, 0)),
                  up_spec, gate_spec,
                  pl.BlockSpec((3, tf), lambda i, j: (0, j)),
                  pl.BlockSpec((1, tf), lambda i, j: (0, j)),
                  down_spec,
                  pl.BlockSpec((1, D_MODEL), lambda i, j: (0, 0)),
                  pl.BlockSpec((1, D_MODEL), lambda i, j: (0, 0))],
        out_specs=(pl.BlockSpec((tm, D_MODEL), lambda i, j: (i, 0)),
                   pl.BlockSpec((1, SUBLANES, tf), lambda i, j: (i, 0, j))),
        scratch_shapes=[pltpu.VMEM((tm, D_MODEL), BF16), pltpu.VMEM((tm, D_MODEL), F32),
                        pltpu.VMEM((tm + SUBLANES, tf), F32), pltpu.VMEM((nf, SUBLANES, tf), F32)],
        compiler_params=_cparams(2),
        name="ffn_prompt",
    )(x, w_up_arr, w_up_arr, conv_w, conv_b.reshape(1, -1), w_down_arr, g.reshape(1, -1), b.reshape(1, -1))


def _dattn_sample_kernel(*refs, n_pages, page, lam_init, rows):
    lam_ref, g_ref, q_ref, kn_ref, vn_ref = refs[1:6]
    o_ref = refs[6 + 2 * rows * n_pages]
    for r in range(rows):
        k_refs = refs[6 + r * n_pages:6 + (r + 1) * n_pages]
        v_refs = refs[6 + (rows + r) * n_pages:6 + (rows + r + 1) * n_pages]
        _dattn_sample_row(lam_ref, g_ref, q_ref.at[r], kn_ref.at[r], vn_ref.at[r], k_refs, v_refs, o_ref.at[r],
                          n_pages=n_pages, page=page, lam_init=lam_init)


def _dattn_sample_row(lam_ref, g_ref, q_ref, kn_ref, vn_ref, k_refs, v_refs, o_ref, *, n_pages, page, lam_init):
    n_rows = 2 * N_HEADS
    q = q_ref[...]
    lane = lax.broadcasted_iota(jnp.int32, (n_rows, MIX_W), 1)
    row = lax.broadcasted_iota(jnp.int32, (n_rows, MIX_W), 0)
    qbd = jnp.where(lax.shift_right_logical(lane, 6) == row, jnp.broadcast_to(q, (n_rows, MIX_W)), 0.0)
    s_new = jnp.sum(qbd * kn_ref[...], axis=1, keepdims=True)
    scores = [_bdot(qbd, k_refs[p][0]) for p in range(n_pages)]
    m = s_new
    for s in scores:
        m = jnp.maximum(m, jnp.max(s, axis=1, keepdims=True))
    p_new = jnp.exp2(s_new - m)
    l = p_new
    vn = vn_ref[...]
    accs = [p_new * vn[:, h * HEAD_W:(h + 1) * HEAD_W] for h in range(N_HEADS)]
    for p in range(n_pages):
        pr = jnp.exp2(scores[p] - m)
        l = l + jnp.sum(pr, axis=1, keepdims=True)
        for h in range(N_HEADS):
            accs[h] = accs[h] + _bdot(pr, v_refs[p][0, pl.ds(h, page, stride=N_HEADS), :])
    lam = _lambda(lam_ref, lam_init)
    outs = []
    for h in range(N_HEADS):
        o = accs[h] / l
        d = o[2 * h:2 * h + 1, :] - lam * o[2 * h + 1:2 * h + 2, :]
        outs.append(_rmsnorm(d, g_ref[...]) * (1.0 - lam_init))
    o_ref[...] = jnp.concatenate(outs, axis=1)


def _dattn_sample(q, k_new, v_new, kt_pool, v_pool, pool_base, page_table, lam_p, subln_g, lam_init):
    DB = q.shape[0]
    n_pages = page_table.shape[1]
    page = kt_pool.shape[2]
    rows = math.gcd(DB, 2)
    tok = pl.BlockSpec((rows, 1, MIX_W), lambda b, pt: (b, 0, 0))

    def k_spec(r, p):
        return pl.BlockSpec((1, MIX_W, page), lambda b, pt: (pool_base + pt[rows * b + r, p], 0, 0))

    def v_spec(r, p):
        return pl.BlockSpec((1, page * N_HEADS, HEAD_W), lambda b, pt: (pool_base + pt[rows * b + r, p], 0, 0))

    pages = [(r, p) for r in range(rows) for p in range(n_pages)]
    grid_spec = pltpu.PrefetchScalarGridSpec(
        num_scalar_prefetch=1,
        grid=(DB // rows,),
        in_specs=[pl.BlockSpec((4, DH_A), lambda b, pt: (0, 0)),
                  pl.BlockSpec((1, HEAD_W), lambda b, pt: (0, 0)), tok, tok, tok]
                 + [k_spec(r, p) for r, p in pages] + [v_spec(r, p) for r, p in pages],
        out_specs=tok,
    )
    out = pl.pallas_call(
        functools.partial(_dattn_sample_kernel, n_pages=n_pages, page=page, lam_init=lam_init, rows=rows),
        out_shape=jax.ShapeDtypeStruct((DB, 1, MIX_W), F32),
        grid_spec=grid_spec,
        compiler_params=_cparams(1),
        name="dattn_sample",
    )(page_table, lam_p, subln_g.reshape(1, -1), q.reshape(DB, 1, MIX_W), k_new.reshape(DB, 1, MIX_W),
      v_new.reshape(DB, 1, MIX_W), *([kt_pool] * len(pages)), *([v_pool] * len(pages)))
    return out.reshape(DB, MIX_W)


def _hgrn_step_kernel(h_ref, s0_ref, lb_ref, ng_ref, o_ref, s_ref, *, bt):
    def body(i, carry):
        row = h_ref[i]
        heads = range(N_HEADS)
        blk = lambda n: row[:, n * MIX_W:(n + 1) * MIX_W]
        lb = lb_ref[...]
        f = lb + (1.0 - lb) * _sigmoid(blk(4))
        q, v = _silu(blk(3)), blk(5)
        f_col = [_col(_head(f, h)) for h in heads]
        s_new = [f_col[h] * s0_ref[i, h] + (1.0 - f_col[h]) * _head(v, h) for h in heads]
        for h in heads:
            s_ref[i, h] = s_new[h]
        o = [_bdot(jnp.broadcast_to(_head(q, h), (SUBLANES, HEAD_W)), s_new[h])[0:1, :] for h in heads]
        o = jnp.concatenate([_rmsnorm(o[h], ng_ref[...]) for h in heads], axis=1)
        o_ref[i] = o * _silu(blk(6))
        return carry

    lax.fori_loop(0, bt, body, 0, unroll=2)


def _hgrn_step(h, s0, lb, norm_g, bt):
    DB, W = h.shape
    bt = min(bt, DB)
    state = pl.BlockSpec((bt, N_HEADS, HEAD_W, HEAD_W), lambda i: (i, 0, 0, 0))
    o, s = pl.pallas_call(
        functools.partial(_hgrn_step_kernel, bt=bt),
        out_shape=(jax.ShapeDtypeStruct((DB, 1, MIX_W), F32), jax.ShapeDtypeStruct(s0.shape, F32)),
        grid=(DB // bt,),
        in_specs=[pl.BlockSpec((bt, 1, W), lambda i: (i, 0, 0)), state,
                  pl.BlockSpec((1, MIX_W), lambda i: (0, 0)), pl.BlockSpec((1, HEAD_W), lambda i: (0, 0))],
        out_specs=(pl.BlockSpec((bt, 1, MIX_W), lambda i: (i, 0, 0)), state),
        compiler_params=_cparams(1),
        name="hgrn_step",
    )(h.reshape(DB, 1, W), s0, lb.reshape(1, -1), norm_g.reshape(1, -1))
    return o.reshape(DB, MIX_W), s


def _od_step_kernel(h_ref, gt_ref, bif_ref, cos_ref, sin_ref, sc0_ref, dc0_ref, dn0_ref, dm0_ref, cng_ref, dng_ref,
                    oc_ref, od_ref, sc_ref, dc_ref, dn_ref, dm_ref, *, bt):
    scale = HEAD_W ** -0.5

    def body(i, carry):
        row = h_ref[i]
        cos, sin = cos_ref[...], sin_ref[...]
        gates = gt_ref[i] + bif_ref[...]
        m_row = dm0_ref[i]
        heads = range(N_HEADS)
        blk = lambda n: row[:, n * MIX_W:(n + 1) * MIX_W]
        q = _rope(blk(0), cos, sin, HEAD_W // 2)
        k = _rope(blk(1), cos, sin, HEAD_W // 2) * scale
        cv, dk, dv, dq = blk(2), blk(5), blk(6), blk(4) * scale
        lf = _log_sigmoid(gates)
        m_new = [jnp.maximum(lf[:, N_HEADS + h:N_HEADS + h + 1] + m_row[:, h:h + 1], gates[:, h:h + 1])
                 for h in heads]
        c_scale = [jnp.exp(lf[:, N_HEADS + h:N_HEADS + h + 1] + m_row[:, h:h + 1] - m_new[h]) for h in heads]
        kw = [_head(dk, h) * jnp.exp(gates[:, h:h + 1] - m_new[h]) for h in heads]
        k_col = [_col(_head(k, h)) for h in heads]
        kw_col = [_col(kw[h]) for h in heads]
        s_new = [math.exp(_log_gamma(h)) * sc0_ref[i, h] + k_col[h] * _head(cv, h) for h in heads]
        c_new = [c_scale[h] * dc0_ref[i, h] + kw_col[h] * _head(dv, h) for h in heads]
        n_new = [c_scale[h] * dn0_ref[i, h:h + 1, :] + kw[h] for h in heads]
        for h in heads:
            sc_ref[i, h] = s_new[h]
            dc_ref[i, h] = c_new[h]
            dn_ref[i, h:h + 1, :] = n_new[h]
        rows8 = lambda x: jnp.broadcast_to(x, (SUBLANES, HEAD_W))
        o = [_bdot(rows8(_head(q, h)), s_new[h])[0:1, :] for h in heads]
        num = [_bdot(rows8(_head(dq, h)), c_new[h])[0:1, :] for h in heads]
        den = [jnp.sum(_head(dq, h) * n_new[h], axis=1, keepdims=True) for h in heads]
        hh = [num[h] / jnp.maximum(jnp.abs(den[h]), jnp.exp(-m_new[h])) for h in heads]
        oc = jnp.concatenate([_groupnorm(o[h], cng_ref[...]) for h in heads], axis=1)
        od = jnp.concatenate([_groupnorm(hh[h], dng_ref[...]) for h in heads], axis=1)
        oc_ref[i] = oc * _silu(blk(3))
        od_ref[i] = od * _sigmoid(blk(7))
        dm_ref[i] = jnp.concatenate(m_new, axis=1)
        return carry

    lax.fori_loop(0, bt, body, 0, unroll=2)


def _od_step(h, gates, b_if, cos_t, sin_t, sc0, dc0, dn0, dm0, c_norm_g, d_norm_g, bt):
    DB, W = h.shape
    bt = min(bt, DB)
    mat = pl.BlockSpec((bt, N_HEADS, HEAD_W, HEAD_W), lambda i: (i, 0, 0, 0))
    nblk = pl.BlockSpec((bt, N_HEADS, HEAD_W), lambda i: (i, 0, 0))
    mblk = pl.BlockSpec((bt, 1, N_HEADS), lambda i: (i, 0, 0))
    vec = pl.BlockSpec((1, LANES), lambda i: (0, 0))
    out = pl.BlockSpec((bt, 1, MIX_W), lambda i: (i, 0, 0))
    b_pad = jnp.zeros((1, LANES), F32).at[0, :2 * N_HEADS].set(b_if)
    oc, od, sc, dc, dn, dm = pl.pallas_call(
        functools.partial(_od_step_kernel, bt=bt),
        out_shape=(jax.ShapeDtypeStruct((DB, 1, MIX_W), F32), jax.ShapeDtypeStruct((DB, 1, MIX_W), F32),
                   jax.ShapeDtypeStruct(sc0.shape, F32), jax.ShapeDtypeStruct(dc0.shape, F32),
                   jax.ShapeDtypeStruct(dn0.shape, F32), jax.ShapeDtypeStruct((DB, 1, N_HEADS), F32)),
        grid=(DB // bt,),
        in_specs=[pl.BlockSpec((bt, 1, W), lambda i: (i, 0, 0)),
                  pl.BlockSpec((bt, 1, LANES), lambda i: (i, 0, 0)), vec, vec, vec,
                  mat, mat, nblk, mblk, vec, vec],
        out_specs=(out, out, mat, mat, nblk, mblk),
        compiler_params=_cparams(1),
        name="od_step",
    )(h.reshape(DB, 1, W), gates.reshape(DB, 1, LANES), b_pad, cos_t, sin_t, sc0, dc0, dn0,
      dm0.reshape(DB, 1, N_HEADS), c_norm_g.reshape(1, -1), d_norm_g.reshape(1, -1))
    return oc.reshape(DB, MIX_W), od.reshape(DB, MIX_W), sc, dc, dn, dm.reshape(DB, N_HEADS)


X_HALVES = DH_X // LANES
X_ROWS = N_HEADS * X_HALVES


def _xattn_sample_kernel(q_ref, mk_ref, mv_ref, o_ref, *, xb):
    rows = range(xb)
    n_cols = mk_ref.shape[1]
    row = lax.broadcasted_iota(jnp.int32, (X_ROWS, n_cols), 0)
    col = lax.broadcasted_iota(jnp.int32, (X_ROWS, n_cols), 1)
    own = (col & (X_ROWS - 1)) == row
    own4 = own[:N_HEADS]
    q = [q_ref[t] * (DH_X ** -0.5) for t in rows]
    q2 = [jnp.concatenate([q[t][:, h * DH_X + c * LANES:h * DH_X + (c + 1) * LANES]
                           for c in range(X_HALVES) for h in range(N_HEADS)], axis=0) for t in rows]
    s2 = [jnp.where(own, _bdot_nt(q2[t], mk_ref[t]), 0.0) for t in rows]
    s4 = [jnp.where(own4, s2[t][:N_HEADS] + pltpu.roll(s2[t], n_cols - N_HEADS, 1)[N_HEADS:], NEG_INF) for t in rows]
    p4 = [jnp.exp(s4[t] - jnp.max(s4[t], axis=1, keepdims=True)) for t in rows]
    p4 = [p4[t] / jnp.sum(p4[t], axis=1, keepdims=True) for t in rows]
    p2 = [jnp.concatenate([p4[t], pltpu.roll(p4[t], N_HEADS, 1)], axis=0) for t in rows]
    o2 = [_bdot(p2[t], mv_ref[t]) for t in rows]
    for t in rows:
        o_ref[t] = jnp.concatenate([o2[t][c * N_HEADS + h:c * N_HEADS + h + 1, :]
                                    for h in range(N_HEADS) for c in range(X_HALVES)], axis=1)


def _xattn_sample(q, mk, mv, base):
    DB = q.shape[0]
    n_rows = mk.shape[1]
    xb = math.gcd(DB, 4)
    tok = pl.BlockSpec((xb, 1, D_MODEL), lambda b: (b, 0, 0))
    mem = pl.BlockSpec((xb, n_rows, LANES), lambda b: (base // xb + b, 0, 0))
    assert base % xb == 0
    out = pl.pallas_call(
        functools.partial(_xattn_sample_kernel, xb=xb),
        out_shape=jax.ShapeDtypeStruct((DB, 1, D_MODEL), F32),
        grid=(DB // xb,),
        in_specs=[tok, mem, mem],
        out_specs=tok,
        compiler_params=_cparams(1),
        name="xattn_sample",
    )(q.reshape(DB, 1, D_MODEL), mk, mv)
    return out.reshape(DB, D_MODEL)


def _mem_rows(cache):
    n_l, DB, n_mem = cache.shape[:3]
    c = cache.reshape(n_l, DB, n_mem, N_HEADS, X_HALVES, LANES).transpose(0, 1, 2, 4, 3, 5)
    return c.reshape(n_l * DB, n_mem * X_ROWS, LANES)


def _ffn_sample_kernel(x_ref, up_ref, gate_ref, b0_ref, b1_ref, cw_ref, cb_ref, wd_ref, g_ref, b_ref, o_ref, acc_ref):
    j = pl.program_id(0)
    conv = (cb_ref[...] + cw_ref[0:1, :] * b0_ref[...] + cw_ref[1:2, :] * b1_ref[...]
            + cw_ref[2:3, :] * up_ref[...])
    y = _bdot(jax.nn.gelu(conv) * gate_ref[...], wd_ref[...])

    @pl.when(j == 0)
    def _():
        acc_ref[...] = y

    @pl.when(j > 0)
    def _():
        acc_ref[...] = acc_ref[...] + y

    @pl.when(j == pl.num_programs(0) - 1)
    def _():
        o_ref[...] = _layernorm(DN_ALPHA * x_ref[...] + acc_ref[...], g_ref[...], b_ref[...])


def _ffn_sample(x, ug, buf0, buf1, conv_w, conv_b, w_down, g, b, tf):
    DB = x.shape[0]
    d_ff = _wshape(w_down)[0]
    nf = d_ff // tf
    ff = lambda off: pl.BlockSpec((DB, tf), lambda j: (0, off + j))
    full = lambda shape: pl.BlockSpec(shape, lambda j: (0,) * len(shape))
    w_down, down_spec = _wspec(w_down, (tf, D_MODEL), lambda j: (j, 0))
    return pl.pallas_call(
        _ffn_sample_kernel,
        out_shape=jax.ShapeDtypeStruct((DB, D_MODEL), F32),
        grid=(nf,),
        in_specs=[full((DB, D_MODEL)), ff(0), ff(nf), ff(0), ff(0),
                  pl.BlockSpec((3, tf), lambda j: (0, j)), pl.BlockSpec((1, tf), lambda j: (0, j)),
                  down_spec, full((1, D_MODEL)), full((1, D_MODEL))],
        out_specs=full((DB, D_MODEL)),
        scratch_shapes=[pltpu.VMEM((DB, D_MODEL), F32)],
        compiler_params=_cparams(1),
        name="ffn_sample",
    )(x, ug, ug, buf0, buf1, conv_w, conv_b.reshape(1, -1), w_down, g.reshape(1, -1), b.reshape(1, -1))


def kernel(x_prompt, x_sample, cache_a_k, cache_a_v, state_b, state_c, state_d_c, state_d_n, state_d_m,
           cache_mem_k, cache_mem_v, state_conv, page_table, mem_prompt,
           ev_w_in, ev_w_out, ev_lam, ev_subln_g, ev_lb_logits, ev_b_norm_g,
           od_w_in, od_b_if, od_w_out, od_c_norm_g, od_d_norm_g,
           ln_g, ln_b, xa_wq, xa_wkv, xa_wo, ffn_w_up, ffn_conv_w, ffn_conv_b, ffn_w_down):
    B, T, _ = x_prompt.shape
    DB, t_s, _ = x_sample.shape
    assert t_s == 1, "the sample group is a single decoding step"
    assert T % CHUNK == 0
    n_pool, page = cache_a_k.shape[1], cache_a_k.shape[2]
    n_pages = page_table.shape[1]
    past = n_pages * page
    n_mem = mem_prompt.shape[1]
    d_ff = ffn_w_down.shape[1]
    tf = d_ff // 2
    N = B * T
    TM = 512

    pos_p = jnp.arange(T)
    pos_s = jnp.full((DB,), past, jnp.int32)
    lb_table = jnp.cumsum(jax.nn.softmax(ev_lb_logits.astype(F32), axis=0), axis=0)

    xp = x_prompt.reshape(N, D_MODEL)
    xs = x_sample.reshape(DB, D_MODEL)
    outs = {k: [] for k in ("ak_p", "av_p", "ak_s", "av_s", "sb_p", "sb_s", "sc_p", "sc_s", "dc_p", "dc_s",
                            "dn_p", "dn_s", "dm_p", "dm_s", "mk_p", "mv_p", "cv_p", "cv_s")}
    swap = lambda s: jnp.swapaxes(s, -1, -2)
    mem_k_rows, mem_v_rows = _mem_rows(cache_mem_k), _mem_rows(cache_mem_v)
    wq_all, wkv_all, wo_all = xa_wq.astype(BF16), xa_wkv.astype(BF16), xa_wo.astype(BF16)
    w_up_all, w_down_all = ffn_w_up.astype(BF16), ffn_w_down.astype(BF16)

    for l in range(DEPTH):
        j = l // 2
        if l % 2 == 0:
            lam_init = 0.8 - 0.6 * math.exp(-0.3 * l)
            w_in = ev_w_in[j].astype(BF16)
            w_out = ev_w_out[j].astype(BF16)
            cos_t, sin_t = _rope_tables(pos_p, DH_A)
            qt, kb, kt32, vt, v_rows, h_b = _ev_in_prompt(xp.reshape(B, T, D_MODEL), w_in, cos_t, sin_t, TM)
            o_a = _dattn_prompt(qt, kb, vt, ev_lam[j], ev_subln_g[j], lam_init, 512)
            o_b, st = _hgrn_prompt(h_b, lb_table[j], 1024)
            outs["ak_p"].append(kt32.reshape(B, N_HEADS, 2, DH_A, T).transpose(0, 4, 1, 2, 3))
            outs["av_p"].append(v_rows.reshape(B, T, N_HEADS, HEAD_W))
            outs["sb_p"].append(swap(st))
            xp = _proj_ln([o_a.reshape(N, MIX_W), o_b.reshape(N, MIX_W)], w_out,
                          xp, ln_g[l, 0], ln_b[l, 0], TM, "ev_out_prompt",
                          gates=[None, ("rms", ev_b_norm_g[j], "silu", h_b.reshape(N, -1), 3)])
            hs = _matmul(xs, w_in, DB, 512, "ev_in_sample")
            cos_s, sin_s = _rope_tables(pos_s, DH_A)
            qs, ks32, kts32 = _ev_prep_sample(hs, cos_s, sin_s)
            vs32 = hs[:, 2 * MIX_W:3 * MIX_W]
            kt_pool = cache_a_k.transpose(0, 1, 3, 4, 5, 2).reshape(-1, MIX_W, page)
            v_pool = cache_a_v.reshape(-1, page * N_HEADS, HEAD_W)
            oa_s = _dattn_sample(qs, ks32, vs32, kt_pool, v_pool, j * n_pool, page_table,
                                 ev_lam[j], ev_subln_g[j], lam_init)
            ob_s, sb_s = _hgrn_step(hs, state_b[j], lb_table[j], ev_b_norm_g[j], 8)
            outs["ak_s"].append(kts32.reshape(N_HEADS, 2, DH_A, DB).transpose(3, 0, 1, 2)[:, None])
            outs["av_s"].append(vs32.reshape(DB, 1, N_HEADS, HEAD_W))
            outs["sb_s"].append(sb_s)
            xs = _proj_ln([oa_s, ob_s], w_out, xs, ln_g[l, 0], ln_b[l, 0], DB, "ev_out_sample")
        else:
            n_main = 8 * MIX_W
            w_in = od_w_in[j][:, :n_main].astype(BF16)
            w_gate = jnp.pad(od_w_in[j][:, n_main:], ((0, 0), (0, LANES - 2 * N_HEADS))).astype(BF16)
            w_out = od_w_out[j].astype(BF16)
            cos_t, sin_t = _rope_tables(pos_p, HEAD_W)
            h, gates = _od_in_prompt(xp, w_in, w_gate, cos_t, sin_t, TM)
            o_c, o_d, sct, dct, dn, dm = _od_prompt(h.reshape(B, T, -1), gates.reshape(B, T, -1), od_b_if[j], 512)
            outs["sc_p"].append(swap(sct))
            outs["dc_p"].append(swap(dct))
            outs["dn_p"].append(dn)
            outs["dm_p"].append(dm)
            xp = _proj_ln([o_c.reshape(N, MIX_W), o_d.reshape(N, MIX_W)], w_out,
                          xp, ln_g[l, 0], ln_b[l, 0], TM, "od_out_prompt",
                          gates=[("group", od_c_norm_g[j], "silu", h, 3), ("group", od_d_norm_g[j], "sigmoid", h, 7)])
            hs = _matmul(xs, w_in, DB, 512, "od_in_sample")
            gates_s = _matmul(xs, w_gate, DB, LANES, "od_gates_sample")
            cos_s, sin_s = _rope_tables(pos_s[:1], HEAD_W)
            oc_s, od_s, sc_s, dc_s, dn_s, dm_s = _od_step(hs, gates_s, od_b_if[j], cos_s, sin_s,
                                                          state_c[j], state_d_c[j], state_d_n[j], state_d_m[j],
                                                          od_c_norm_g[j], od_d_norm_g[j], 8)
            outs["sc_s"].append(sc_s)
            outs["dc_s"].append(dc_s)
            outs["dn_s"].append(dn_s)
            outs["dm_s"].append(dm_s)
            xs = _proj_ln([oc_s, od_s], w_out, xs, ln_g[l, 0], ln_b[l, 0], DB, "od_out_sample")

        wq, wo = (wq_all, l), (wo_all, l)
        mkv = _matmul(mem_prompt.reshape(B * n_mem, D_MODEL), (wkv_all, l), 512, 512, "mem_kv")
        mk, mv = mkv[:, :D_MODEL], mkv[:, D_MODEL:]
        outs["mk_p"].append(mk.reshape(B, n_mem, N_HEADS, DH_X))
        outs["mv_p"].append(mv.reshape(B, n_mem, N_HEADS, DH_X))
        xp = _xattn_prompt(xp, wq, mk.astype(BF16).reshape(B, n_mem, D_MODEL),
                           mv.astype(BF16).reshape(B, n_mem, D_MODEL), wo, ln_g[l, 1], ln_b[l, 1], T, 1024)
        q_s = _matmul(xs, wq, DB, 512, "xattn_q_sample")
        xo_s = _xattn_sample(q_s, mem_k_rows, mem_v_rows, l * DB)
        xs = _proj_ln([xo_s], wo, xs, ln_g[l, 1], ln_b[l, 1], DB, "xattn_out_sample")

        w_up, w_down = (w_up_all, l), (w_down_all, l)
        tm_f = min(TM, T)
        xp, tails = _ffn_prompt(xp, w_up, ffn_conv_w[l], ffn_conv_b[l], w_down, ln_g[l, 2], ln_b[l, 2], T, TM, d_ff)
        tails = tails.reshape(B, T // tm_f, SUBLANES, d_ff)
        outs["cv_p"].append(tails[:, -1, SUBLANES - 2:, :])
        ug_s = _matmul(xs, w_up, DB, tf, "ffn_up_sample")
        buf = state_conv[l]
        xs = _ffn_sample(xs, ug_s, buf[:, 0, :], buf[:, 1, :], ffn_conv_w[l], ffn_conv_b[l], w_down,
                         ln_g[l, 2], ln_b[l, 2], tf)
        outs["cv_s"].append(jnp.stack([buf[:, 1, :], ug_s[:, :d_ff]], axis=1))

    st = lambda k: jnp.stack(outs[k])
    return (xp.reshape(B, T, D_MODEL), xs.reshape(DB, 1, D_MODEL),
            st("ak_p"), st("av_p"), st("ak_s"), st("av_s"), st("sb_p"), st("sb_s"),
            st("sc_p"), st("sc_s"), st("dc_p"), st("dc_s"), st("dn_p"), st("dn_s"), st("dm_p"), st("dm_s"),
            st("mk_p"), st("mv_p"), st("cv_p"), st("cv_s"))
```

```python
import functools
import math

import numpy as np
import jax
import jax.numpy as jnp
from jax import lax
from jax.experimental import pallas as pl
from jax.experimental.pallas import tpu as pltpu

F32 = jnp.float32
BF16 = jnp.bfloat16

D_MODEL = 1024
MIX_W = D_MODEL // 2
N_HEADS = 4
HEAD_W = MIX_W // N_HEADS
DH_A = HEAD_W // 2
DH_X = D_MODEL // N_HEADS
CHUNK = 64
OD_CHUNK = 128
ROPE_THETA = 10000.0
NORM_EPS = 1e-5
DEPTH = 2
DN_ALPHA = (2.0 * DEPTH) ** 0.25
LANES = 128
SUBLANES = 8
VMEM_LIMIT = 56 * 1024 * 1024
NEG_INF = float("-inf")

NT_DIMS = (((1,), (1,)), ((), ()))
TN_DIMS = (((0,), (0,)), ((), ()))


def _wspec(w, block_shape, index_map, **kw):
    if isinstance(w, tuple):
        stack, layer = w
        return stack, pl.BlockSpec((None,) + tuple(block_shape), lambda *a: (layer,) + tuple(index_map(*a)), **kw)
    return w, pl.BlockSpec(tuple(block_shape), index_map, **kw)


def _wshape(w):
    return w[0].shape[1:] if isinstance(w, tuple) else w.shape


def _cparams(n_axes, vmem=VMEM_LIMIT):
    return pltpu.CompilerParams(dimension_semantics=("arbitrary",) * n_axes, vmem_limit_bytes=vmem)


def _bdot(a, b):
    return jnp.dot(a.astype(BF16), b.astype(BF16), preferred_element_type=F32)


def _bdot_nt(a, b):
    return lax.dot_general(a.astype(BF16), b.astype(BF16), NT_DIMS, preferred_element_type=F32)


def _bdot_tn(a, b):
    return lax.dot_general(a.astype(BF16), b.astype(BF16), TN_DIMS, preferred_element_type=F32)


def _split3(x):
    p1 = x.astype(BF16)
    r1 = x - p1.astype(F32)
    p2 = r1.astype(BF16)
    p3 = (r1 - p2.astype(F32)).astype(BF16)
    return p1, p2, p3


def _cumsum_rows(tri, x):
    p1, p2, p3 = _split3(x)
    d = functools.partial(jnp.dot, preferred_element_type=F32)
    return d(tri, p1) + d(tri, p2) + d(tri, p3)


def _cumsum_lanes(x, triu):
    p1, p2, p3 = _split3(x)
    d = functools.partial(jnp.dot, preferred_element_type=F32)
    return d(p1, triu) + d(p2, triu) + d(p3, triu)


def _tri(L, lower):
    r = lax.broadcasted_iota(jnp.int32, (L, L), 0)
    c = lax.broadcasted_iota(jnp.int32, (L, L), 1)
    return (r >= c) if lower else (r <= c)


def _sigmoid(x):
    return 1.0 / (1.0 + jnp.exp(-x))


def _silu(x):
    return x * _sigmoid(x)


def _log_sigmoid(x):
    return jnp.minimum(x, 0.0) - jnp.log(1.0 + jnp.exp(-jnp.abs(x)))


def _layernorm(z, g, b):
    mu = jnp.mean(z, -1, keepdims=True)
    zc = z - mu
    var = jnp.mean(zc * zc, -1, keepdims=True)
    return zc * lax.rsqrt(var + NORM_EPS) * g + b


def _rmsnorm(x, g):
    return x * lax.rsqrt(jnp.mean(x * x, -1, keepdims=True) + NORM_EPS) * g


def _groupnorm(x, g):
    mu = jnp.mean(x, -1, keepdims=True)
    xc = x - mu
    var = jnp.mean(xc * xc, -1, keepdims=True)
    return xc * lax.rsqrt(var + NORM_EPS) * g


def _rope(x, cos, sin, half):
    outs = []
    for c in range(x.shape[1] // LANES):
        xc = x[:, c * LANES:(c + 1) * LANES]
        if 2 * half == LANES:
            sw = pltpu.roll(xc, half, 1)
        else:
            lane = lax.broadcasted_iota(jnp.int32, xc.shape, 1)
            first = (lane & (2 * half - 1)) < half
            sw = jnp.where(first, pltpu.roll(xc, LANES - half, 1), pltpu.roll(xc, half, 1))
        outs.append(xc * cos + sw * sin)
    return outs[0] if len(outs) == 1 else jnp.concatenate(outs, axis=1)


def _rope_tables(pos, d):
    inv = ROPE_THETA ** (-jnp.arange(0, d // 2, dtype=F32) * 2.0 / d)
    ang = pos.astype(F32)[:, None] * inv[None, :]
    cos, sin = jnp.cos(ang), jnp.sin(ang)
    reps = LANES // d
    cos_t = jnp.tile(jnp.concatenate([cos, cos], -1), (1, reps))
    sin_t = jnp.tile(jnp.concatenate([-sin, sin], -1), (1, reps))
    return cos_t, sin_t


def _col(row):
    return jnp.broadcast_to(row, (LANES, LANES)).T


def _mm_kernel(x_ref, w_ref, o_ref, xb_ref):
    @pl.when(pl.program_id(1) == 0)
    def _():
        xb_ref[...] = x_ref[...].astype(BF16)

    o_ref[...] = jnp.dot(xb_ref[...], w_ref[...], preferred_element_type=F32).astype(o_ref.dtype)


def _matmul(x, w, tm, tn, name, out_dtype=F32):
    M, K = x.shape
    N = _wshape(w)[1]
    tm, tn = min(tm, M), min(tn, N)
    w, w_spec = _wspec(w, (K, tn), lambda i, j: (0, j))
    return pl.pallas_call(
        _mm_kernel,
        out_shape=jax.ShapeDtypeStruct((M, N), out_dtype),
        grid=(M // tm, N // tn),
        in_specs=[pl.BlockSpec((tm, K), lambda i, j: (i, 0)), w_spec],
        out_specs=pl.BlockSpec((tm, tn), lambda i, j: (i, j)),
        scratch_shapes=[pltpu.VMEM((tm, K), BF16)],
        compiler_params=_cparams(2),
        name=name,
    )(x, w)


_HEAD_NORMS = {"rms": _rmsnorm, "group": _groupnorm}
_GATE_ACTS = {"silu": _silu, "sigmoid": _sigmoid}


def _proj_ln_kernel(*refs, n_in, gated):
    a_refs, w_refs = refs[:n_in], refs[n_in:2 * n_in]
    x_ref, g_ref, b_ref = refs[2 * n_in:2 * n_in + 3]
    extra, o_ref = refs[2 * n_in + 3:-1], refs[-1]
    y = None
    for a_ref, w_ref, gate in zip(a_refs, w_refs, gated):
        a = a_ref[...]
        if gate is not None:
            norm, act = _HEAD_NORMS[gate[0]], _GATE_ACTS[gate[1]]
            gain_ref, gate_ref, extra = extra[0], extra[1], extra[2:]
            a = jnp.concatenate([norm(_head(a, h), gain_ref[...]) for h in range(N_HEADS)], axis=1)
            a = a * act(gate_ref[...])
        ya = _bdot(a, w_ref[...])
        y = ya if y is None else y + ya
    o_ref[...] = _layernorm(DN_ALPHA * x_ref[...] + y, g_ref[...], b_ref[...])


def _proj_ln(acts, weight, x, g, b, tm, name, gates=None):
    M = x.shape[0]
    tm = min(tm, M)
    n_in = len(acts)
    gates = gates or [None] * n_in
    in_specs = [pl.BlockSpec((tm, a.shape[1]), lambda i: (i, 0)) for a in acts]
    w_args = []
    for r, a in enumerate(acts):
        w_arr, w_spec = _wspec(weight, (a.shape[1], D_MODEL), lambda i, r=r: (r, 0))
        w_args.append(w_arr)
        in_specs.append(w_spec)
    in_specs += [pl.BlockSpec((tm, D_MODEL), lambda i: (i, 0)),
                 pl.BlockSpec((1, D_MODEL), lambda i: (0, 0)),
                 pl.BlockSpec((1, D_MODEL), lambda i: (0, 0))]
    extra = []
    for gate in gates:
        if gate is not None:
            _, gain, _, gate_arr, blk = gate
            in_specs += [pl.BlockSpec((1, HEAD_W), lambda i: (0, 0)),
                         pl.BlockSpec((tm, MIX_W), lambda i, blk=blk: (i, blk))]
            extra += [gain.reshape(1, -1), gate_arr]
    return pl.pallas_call(
        functools.partial(_proj_ln_kernel, n_in=n_in,
                          gated=tuple(None if gt is None else (gt[0], gt[2]) for gt in gates)),
        out_shape=jax.ShapeDtypeStruct((M, D_MODEL), F32),
        grid=(M // tm,),
        in_specs=in_specs,
        out_specs=pl.BlockSpec((tm, D_MODEL), lambda i: (i, 0)),
        compiler_params=_cparams(1),
        name=name,
    )(*acts, *w_args, x, g.reshape(1, -1), b.reshape(1, -1), *extra)


Q_SCALE = DH_A ** -0.5 * math.log2(math.e)
VT_ROWS = HEAD_W + 16


def _ev_in_prompt_kernel(x_ref, w_ref, cos_ref, sin_ref, qt_ref, kb_ref, kt_ref, vt_ref, vr_ref, hb_ref):
    xb = x_ref[0].astype(BF16)
    cos, sin = cos_ref[...], sin_ref[...]
    proj = lambda lo, hi: jnp.dot(xb, w_ref[:, lo * MIX_W:hi * MIX_W], preferred_element_type=F32)
    hb_ref[0] = proj(3, 7)
    q = _rope(proj(0, 1), cos, sin, DH_A // 2)
    k = _rope(proj(1, 2), cos, sin, DH_A // 2)
    qt_ref[0] = (q * Q_SCALE).T.astype(BF16)
    kb_ref[0] = k.astype(BF16)
    kt_ref[0] = k.T
    v = proj(2, 3)
    tm = v.shape[0]
    vt = v.T.astype(BF16)
    ones = jnp.ones((VT_ROWS - HEAD_W, tm), BF16)
    for h in range(N_HEADS):
        vt_ref[0, h, :HEAD_W, :] = vt[h * HEAD_W:(h + 1) * HEAD_W]
        vt_ref[0, h, HEAD_W:, :] = ones
        vr_ref[0, pl.ds(h, tm, stride=N_HEADS), :] = _head(v, h)


def _ev_in_prompt(x3, w_in, cos_t, sin_t, tm):
    B, T, _ = x3.shape
    tm = min(tm, T)
    tab = pl.BlockSpec((tm, LANES), lambda b, i: (i, 0))
    tr = pl.BlockSpec((1, MIX_W, tm), lambda b, i: (b, 0, i))
    tshape = lambda dt: jax.ShapeDtypeStruct((B, MIX_W, T), dt)
    return pl.pallas_call(
        _ev_in_prompt_kernel,
        out_shape=(tshape(BF16), jax.ShapeDtypeStruct((B, T, MIX_W), BF16), tshape(F32),
                   jax.ShapeDtypeStruct((B, N_HEADS, VT_ROWS, T), BF16),
                   jax.ShapeDtypeStruct((B, T * N_HEADS, HEAD_W), F32),
                   jax.ShapeDtypeStruct((B, T, 4 * MIX_W), F32)),
        grid=(B, T // tm),
        in_specs=[pl.BlockSpec((1, tm, D_MODEL), lambda b, i: (b, i, 0)),
                  pl.BlockSpec(w_in.shape, lambda b, i: (0, 0)), tab, tab],
        out_specs=(tr, pl.BlockSpec((1, tm, MIX_W), lambda b, i: (b, i, 0)), tr,
                   pl.BlockSpec((1, N_HEADS, VT_ROWS, tm), lambda b, i: (b, 0, 0, i)),
                   pl.BlockSpec((1, tm * N_HEADS, HEAD_W), lambda b, i: (b, i, 0)),
                   pl.BlockSpec((1, tm, 4 * MIX_W), lambda b, i: (b, i, 0))),
        compiler_params=_cparams(2),
        name="ev_in_prompt",
    )(x3, w_in, cos_t, sin_t)


def _ev_prep_sample_kernel(qk_ref, cos_ref, sin_ref, q_ref, k_ref, kt_ref):
    cos, sin = cos_ref[...], sin_ref[...]
    k = _rope(qk_ref[:, MIX_W:], cos, sin, DH_A // 2)
    q_ref[...] = _rope(qk_ref[:, :MIX_W], cos, sin, DH_A // 2) * Q_SCALE
    k_ref[...] = k
    kt_ref[...] = k.T


def _ev_prep_sample(h, cos_t, sin_t):
    DB = h.shape[0]
    full = lambda shape: pl.BlockSpec(shape, lambda i: (0,) * len(shape))
    return pl.pallas_call(
        _ev_prep_sample_kernel,
        out_shape=(jax.ShapeDtypeStruct((DB, MIX_W), F32), jax.ShapeDtypeStruct((DB, MIX_W), F32),
                   jax.ShapeDtypeStruct((MIX_W, DB), F32)),
        grid=(1,),
        in_specs=[full((DB, 2 * MIX_W)), full((DB, LANES)), full((DB, LANES))],
        out_specs=(full((DB, MIX_W)), full((DB, MIX_W)), full((MIX_W, DB))),
        compiler_params=_cparams(1),
        name="ev_prep_sample",
    )(h, cos_t, sin_t)


def _lambda(lam_ref, lam_init):
    lp = lam_ref[...]
    s01 = jnp.sum(lp[0:1] * lp[1:2], axis=1, keepdims=True)
    s23 = jnp.sum(lp[2:3] * lp[3:4], axis=1, keepdims=True)
    return jnp.exp(s01) - jnp.exp(s23) + lam_init


def _dattn_kernel(lam_ref, g_ref, qt_ref, k_ref, vt_ref, o_ref, m_ref, acc_ref, *, tq, lam_init):
    i = pl.program_id(2)
    qt = qt_ref[0]
    sub = lax.broadcasted_iota(jnp.int32, qt.shape, 0)
    zero = jnp.zeros_like(qt)
    q_maps = [jnp.where(sub < DH_A, qt, zero), jnp.where(sub >= DH_A, qt, zero)]
    maps = range(2)
    m_ref[...] = jnp.full(m_ref.shape, NEG_INF, F32)
    acc_ref[...] = jnp.zeros(acc_ref.shape, F32)

    def steps(js):
        ks, vts, ss = [], [], []
        for j in js:
            start = pl.multiple_of(j * tq, tq)
            ks.append(k_ref[0, pl.ds(start, tq), :])
            vts.append(vt_ref[0, 0, :, pl.ds(start, tq)])
        for kj in ks:
            ss.append([jnp.dot(kj, q_maps[c], preferred_element_type=F32) for c in maps])
        for n in range(len(js)):
            for c in maps:
                s = ss[n][c]
                m_prev = m_ref[c]
                m_new = jnp.maximum(m_prev, jnp.max(s, axis=0, keepdims=True))
                alpha = jnp.exp2(m_prev - m_new)
                p = jnp.exp2(s - m_new)
                acc_ref[c] = alpha * acc_ref[c] + jnp.dot(vts[n], p.astype(BF16), preferred_element_type=F32)
                m_ref[c] = m_new

    def body(jj, carry):
        steps([2 * jj, 2 * jj + 1])
        return carry

    lax.fori_loop(0, lax.shift_right_logical(i, 1), body, 0)

    @pl.when((i & 1) == 1)
    def _():
        steps([i - 1])

    half = tq // 2
    start = pl.multiple_of(i * tq, tq)
    k_top, k_bot = k_ref[0, pl.ds(start, half), :], k_ref[0, pl.ds(start + half, half), :]
    vt_top, vt_bot = vt_ref[0, 0, :, pl.ds(start, half)], vt_ref[0, 0, :, pl.ds(start + half, half)]
    causal = lambda s: jnp.where(lax.broadcasted_iota(jnp.int32, s.shape, 0)
                                 <= lax.broadcasted_iota(jnp.int32, s.shape, 1), s, NEG_INF)
    s_top = [causal(jnp.dot(k_top, q_maps[c], preferred_element_type=F32)) for c in maps]
    s_bot = [causal(jnp.dot(k_bot, q_maps[c][:, half:], preferred_element_type=F32)) for c in maps]
    for c in maps:
        m_prev = m_ref[c]
        m_new = jnp.maximum(m_prev, jnp.max(s_top[c], axis=0, keepdims=True))
        m_new = jnp.concatenate([m_new[:, :half],
                                 jnp.maximum(m_new[:, half:], jnp.max(s_bot[c], axis=0, keepdims=True))], axis=1)
        p_top = jnp.exp2(s_top[c] - m_new).astype(BF16)
        p_bot = jnp.exp2(s_bot[c] - m_new[:, half:]).astype(BF16)
        acc_ref[c] = jnp.exp2(m_prev - m_new) * acc_ref[c] + jnp.dot(vt_top, p_top, preferred_element_type=F32)
        acc_ref[c, :, half:] = acc_ref[c, :, half:] + jnp.dot(vt_bot, p_bot, preferred_element_type=F32)
    lam = _lambda(lam_ref, lam_init)
    o = [acc_ref[c, :HEAD_W, :] / acc_ref[c, HEAD_W:HEAD_W + 1, :] for c in maps]
    d = (o[0] - lam * o[1]).T
    o_ref[0] = (_rmsnorm(d, g_ref[...]) * (1.0 - lam_init)).astype(BF16)


def _dattn_prompt(qt, kb, vt, lam_p, subln_g, lam_init, tq):
    B, T, _ = kb.shape
    tq = min(tq, T)
    return pl.pallas_call(
        functools.partial(_dattn_kernel, tq=tq, lam_init=lam_init),
        out_shape=jax.ShapeDtypeStruct((B, T, MIX_W), BF16),
        grid=(B, N_HEADS, T // tq),
        in_specs=[pl.BlockSpec((4, DH_A), lambda b, h, i: (0, 0)),
                  pl.BlockSpec((1, HEAD_W), lambda b, h, i: (0, 0)),
                  pl.BlockSpec((1, HEAD_W, tq), lambda b, h, i: (b, h, i)),
                  pl.BlockSpec((1, T, HEAD_W), lambda b, h, i: (b, 0, h)),
                  pl.BlockSpec((1, 1, VT_ROWS, T), lambda b, h, i: (b, h, 0, 0))],
        out_specs=pl.BlockSpec((1, tq, HEAD_W), lambda b, h, i: (b, i, h)),
        scratch_shapes=[pltpu.VMEM((2, 1, tq), F32), pltpu.VMEM((2, VT_ROWS, tq), F32)],
        compiler_params=_cparams(3),
        name="dattn_prompt",
    )(lam_p, subln_g.reshape(1, -1), qt, kb, vt)


def _head(x, h):
    return x[:, h * HEAD_W:(h + 1) * HEAD_W]


def _gla_chunk(seqs, st_ref, tril_mask):
    seqs = [tuple(x.astype(BF16) for x in s[:4]) + (s[4],) for s in seqs]
    chains = [(n, h) for n in range(len(seqs)) for h in range(N_HEADS)]
    part = lambda n, i, h: _head(seqs[n][i], h)
    sts = [st_ref[N_HEADS * n + h] for n, h in chains]
    inter = [_bdot_nt(part(n, 0, h), sts[c]) for c, (n, h) in enumerate(chains)]
    upd = [_bdot_tn(part(n, 3, h), part(n, 2, h)) for n, h in chains]
    attn = [jnp.where(tril_mask, _bdot_nt(part(n, 0, h), part(n, 1, h)), 0.0) for n, h in chains]
    intra = [_bdot(attn[c], part(n, 3, h)) for c, (n, h) in enumerate(chains)]
    for c, (n, h) in enumerate(chains):
        dec = seqs[n][4]
        st_ref[N_HEADS * n + h] = sts[c] * (dec[h] if isinstance(dec, (list, tuple)) else _head(dec, h)) + upd[c]
    outs = [intra[c] + inter[c] for c in range(len(chains))]
    return [outs[N_HEADS * n:N_HEADS * (n + 1)] for n in range(len(seqs))]


def _hgrn_kernel(q_ref, f_ref, i_ref, lb_ref, o_ref, s_ref, st_ref, *, tt, L, n_seq):
    t = pl.program_id(0)

    @pl.when(t == 0)
    def _():
        st_ref[...] = jnp.zeros(st_ref.shape, F32)

    tril_mask = _tri(L, True)
    tril = tril_mask.astype(BF16)

    def body(c, carry):
        rows = pl.ds(pl.multiple_of(c * L, L), L)
        lb = lb_ref[...]
        seqs = []
        for n in range(n_seq):
            f = lb + (1.0 - lb) * _sigmoid(f_ref[n, rows, :])
            b = _cumsum_rows(tril, jnp.log(f))
            b_end = b[L - 1:L, :]
            k = 1.0 - f
            q_in = _silu(q_ref[n, rows, :]) * jnp.exp(b)
            seqs.append((q_in, k * jnp.exp(-b), k * jnp.exp(b_end - b), i_ref[n, rows, :], jnp.exp(b_end)))
        for n, o in enumerate(_gla_chunk(seqs, st_ref, tril_mask)):
            o_ref[n, rows, :] = jnp.concatenate(o, axis=1)
        return carry

    lax.fori_loop(0, tt // L, body, 0, unroll=2)

    @pl.when(t == pl.num_programs(0) - 1)
    def _():
        s_ref[...] = st_ref[...]


def _hgrn_prompt(h3, lb, tt):
    B, T, _ = h3.shape
    tt = min(tt, T)
    L = math.gcd(T, CHUNK)
    col = lambda c: pl.BlockSpec((B, tt, MIX_W), lambda t: (0, t, c))
    o, st = pl.pallas_call(
        functools.partial(_hgrn_kernel, tt=tt, L=L, n_seq=B),
        out_shape=(jax.ShapeDtypeStruct((B, T, MIX_W), F32),
                   jax.ShapeDtypeStruct((B * N_HEADS, HEAD_W, HEAD_W), F32)),
        grid=(T // tt,),
        in_specs=[col(0), col(1), col(2), pl.BlockSpec((1, MIX_W), lambda t: (0, 0))],
        out_specs=(pl.BlockSpec((B, tt, MIX_W), lambda t: (0, t, 0)),
                   pl.BlockSpec((B * N_HEADS, HEAD_W, HEAD_W), lambda t: (0, 0, 0))),
        scratch_shapes=[pltpu.VMEM((B * N_HEADS, HEAD_W, HEAD_W), F32)],
        compiler_params=_cparams(1),
        name="hgrn_prompt",
    )(h3, h3, h3, lb.reshape(1, -1))
    return o, st.reshape(B, N_HEADS, HEAD_W, HEAD_W)


def _log_gamma(h):
    return float(np.log(1.0 - 2.0 ** (-5.0 - h)))


def _od_kernel(cq_ref, ck_ref, cv_ref, dq_ref, dk_ref, dv_ref, gc_ref, gr_ref,
               bc_ref, br_ref,
               oc_ref, od_ref, sc_ref, dc_ref, dn_ref, dm_ref,
               st_ref, ct_ref, n_ref, m_ref, *, tt, L, n_seq):
    t = pl.program_id(0)

    @pl.when(t == 0)
    def _():
        st_ref[...] = jnp.zeros(st_ref.shape, F32)
        ct_ref[...] = jnp.zeros(ct_ref.shape, F32)
        n_ref[...] = jnp.zeros(n_ref.shape, F32)
        m_ref[...] = jnp.zeros(m_ref.shape, F32)

    tril_mask = _tri(L, True)
    tril = tril_mask.astype(BF16)
    triu = _tri(L, False).astype(BF16)
    scale = HEAD_W ** -0.5
    heads = range(N_HEADS)
    pos1 = (lax.broadcasted_iota(jnp.int32, (L, HEAD_W), 0) + 1).astype(F32)
    ret_b = jnp.concatenate([pos1 * _log_gamma(h) for h in heads], axis=1)
    ret_b_end = jnp.concatenate([jnp.full((L, HEAD_W), L * _log_gamma(h), F32) for h in heads], axis=1)
    ret_q_dec, ret_k_dec, ret_k_end = jnp.exp(ret_b), jnp.exp(-ret_b), jnp.exp(ret_b_end - ret_b)
    ret_dec = [math.exp(L * _log_gamma(h)) for h in heads]

    seqs = range(n_seq)
    chains = [(n, h) for n in seqs for h in heads]

    def body(c, carry):
        rows = pl.ds(pl.multiple_of(c * L, L), L)
        ret = []
        for n in seqs:
            q, k = cq_ref[n, rows, :], ck_ref[n, rows, :]
            ret.append((q * ret_q_dec, k * ret_k_dec, k * ret_k_end, cv_ref[n, rows, :], ret_dec))
        for n, o in enumerate(_gla_chunk(ret, st_ref, tril_mask)):
            oc_ref[n, rows, :] = jnp.concatenate(o, axis=1)
        g_c = [gc_ref[n, rows, :] + bc_ref[...] for n in seqs]
        g_r = [gr_ref[n, c] + br_ref[...] for n in seqs]
        lf_c = [_log_sigmoid(g) for g in g_c]
        b_c = [_cumsum_rows(tril, jnp.concatenate(
            [jnp.broadcast_to(lf_c[n][:, N_HEADS + h:N_HEADS + h + 1], (L, HEAD_W)) for h in heads], axis=1))
            for n in seqs]
        b_r = [_cumsum_lanes(_log_sigmoid(g), triu) for g in g_r]
        dq = [dq_ref[n, rows, :] * scale for n in seqs]
        dk = [dk_ref[n, rows, :] for n in seqs]
        dqb = [x.astype(BF16) for x in dq]
        dkb = [x.astype(BF16) for x in dk]
        dvb = [dv_ref[n, rows, :].astype(BF16) for n in seqs]
        cts = [ct_ref[i] for i in range(len(chains))]
        q_c = [_bdot_nt(_head(dqb[n], h), cts[i]) for i, (n, h) in enumerate(chains)]
        q_k = [_bdot_nt(_head(dqb[n], h), _head(dkb[n], h)) for n, h in chains]
        m_prev = [m_ref[i:i + 1, :] for i in range(len(chains))]
        bcw, m_t, w_mat = [], [], []
        for i, (n, h) in enumerate(chains):
            bcw.append(_head(b_c[n], h))
            bc = bcw[i][:, :L]
            dm = jnp.where(tril_mask, bc - b_r[n][N_HEADS + h:N_HEADS + h + 1, :] + g_r[n][h:h + 1, :], NEG_INF)
            m_t.append(jnp.maximum(bcw[i] + m_prev[i], jnp.max(dm, axis=1, keepdims=True)))
            w_mat.append(jnp.exp(dm - m_t[i][:, :L]) * q_k[i])
        w_v = [_bdot(w_mat[i], _head(dvb[n], h)) for i, (n, h) in enumerate(chains)]
        kws, c_scales = [], []
        for i, (n, h) in enumerate(chains):
            m_new = m_t[i][L - 1:L, :]
            b_last = bcw[i][L - 1:L, :]
            c_scales.append(jnp.exp(b_last + m_prev[i] - m_new))
            kws.append(_head(dk[n], h) * jnp.exp(b_last - bcw[i] + g_c[n][:, h:h + 1] - m_new))
            m_ref[i:i + 1, :] = m_new
        c_upd = [_bdot_tn(_head(dvb[n], h), kws[i]) for i, (n, h) in enumerate(chains)]
        hs = []
        for i, (n, h) in enumerate(chains):
            n_row = n_ref[i:i + 1, :]
            inter = jnp.exp(bcw[i] + m_prev[i] - m_t[i])
            num = inter * q_c[i] + w_v[i]
            den = (inter[:, 0:1] * jnp.sum(_head(dq[n], h) * n_row, axis=1, keepdims=True)
                   + jnp.sum(w_mat[i], axis=1, keepdims=True))
            hs.append(num / jnp.maximum(jnp.abs(den), jnp.exp(-m_t[i])))
            ct_ref[i] = c_scales[i] * cts[i] + c_upd[i]
            n_ref[i:i + 1, :] = c_scales[i] * n_row + jnp.sum(kws[i], axis=0, keepdims=True)
        for n in seqs:
            od_ref[n, rows, :] = jnp.concatenate(hs[N_HEADS * n:N_HEADS * (n + 1)], axis=1)
        return carry

    lax.fori_loop(0, tt // L, body, 0)

    @pl.when(t == pl.num_programs(0) - 1)
    def _():
        sc_ref[...] = st_ref[...]
        dc_ref[...] = ct_ref[...]
        dn_ref[...] = n_ref[...]
        dm_ref[...] = m_ref[...]


def _od_in_kernel(x_ref, w_ref, wg_ref, cos_ref, sin_ref, h_ref, g_ref):
    xb = x_ref[...].astype(BF16)
    cos, sin = cos_ref[...], sin_ref[...]
    proj = lambda lo, hi: jnp.dot(xb, w_ref[:, lo * MIX_W:hi * MIX_W], preferred_element_type=F32)
    h_ref[:, 2 * MIX_W:] = proj(2, 8)
    h_ref[:, :MIX_W] = _rope(proj(0, 1), cos, sin, HEAD_W // 2)
    h_ref[:, MIX_W:2 * MIX_W] = _rope(proj(1, 2), cos, sin, HEAD_W // 2) * (HEAD_W ** -0.5)
    g_ref[...] = jnp.dot(xb, wg_ref[...], preferred_element_type=F32)


def _od_in_prompt(x, w_in, w_gate, cos_t, sin_t, tm):
    M, K = x.shape
    N = w_in.shape[1]
    tm = min(tm, cos_t.shape[0])
    n_tab = cos_t.shape[0] // tm
    tab = pl.BlockSpec((tm, LANES), lambda i: (i % n_tab, 0))
    return pl.pallas_call(
        _od_in_kernel,
        out_shape=(jax.ShapeDtypeStruct((M, N), F32), jax.ShapeDtypeStruct((M, LANES), F32)),
        grid=(M // tm,),
        in_specs=[pl.BlockSpec((tm, K), lambda i: (i, 0)),
                  pl.BlockSpec((K, N), lambda i: (0, 0)),
                  pl.BlockSpec((K, LANES), lambda i: (0, 0)), tab, tab],
        out_specs=(pl.BlockSpec((tm, N), lambda i: (i, 0)), pl.BlockSpec((tm, LANES), lambda i: (i, 0))),
        compiler_params=_cparams(1),
        name="od_in_prompt",
    )(x, w_in, w_gate, cos_t, sin_t)


def _od_prompt(h3, gates, b_if, tt):
    B, T, _ = h3.shape
    tt = min(tt, T)
    L = math.gcd(T, OD_CHUNK)
    gates_r = gates[:, :, :2 * N_HEADS].reshape(B, T // L, L, 2 * N_HEADS).transpose(0, 1, 3, 2)
    b_pad = jnp.zeros((1, LANES), F32).at[0, :2 * N_HEADS].set(b_if)
    n_chain = B * N_HEADS
    n_row = -(-n_chain // SUBLANES) * SUBLANES
    col = lambda c: pl.BlockSpec((B, tt, MIX_W), lambda t: (0, t, c))
    mat_state = pl.BlockSpec((n_chain, HEAD_W, HEAD_W), lambda t: (0, 0, 0))
    row_state = pl.BlockSpec((n_row, LANES), lambda t: (0, 0))
    out_blk = pl.BlockSpec((B, tt, MIX_W), lambda t: (0, t, 0))
    mat_shape = jax.ShapeDtypeStruct((n_chain, HEAD_W, HEAD_W), F32)
    row_shape = jax.ShapeDtypeStruct((n_row, LANES), F32)
    o_c, o_d, sct, dct, dn, dm = pl.pallas_call(
        functools.partial(_od_kernel, tt=tt, L=L, n_seq=B),
        out_shape=(jax.ShapeDtypeStruct((B, T, MIX_W), F32), jax.ShapeDtypeStruct((B, T, MIX_W), F32),
                   mat_shape, mat_shape, row_shape, row_shape),
        grid=(T // tt,),
        in_specs=[col(0), col(1), col(2), col(4), col(5), col(6),
                  pl.BlockSpec((B, tt, LANES), lambda t: (0, t, 0)),
                  pl.BlockSpec((B, tt // L, 2 * N_HEADS, L), lambda t: (0, t, 0, 0)),
                  pl.BlockSpec((1, LANES), lambda t: (0, 0)),
                  pl.BlockSpec((2 * N_HEADS, 1), lambda t: (0, 0))],
        out_specs=(out_blk, out_blk, mat_state, mat_state, row_state, row_state),
        scratch_shapes=[pltpu.VMEM(mat_shape.shape, F32), pltpu.VMEM(mat_shape.shape, F32),
                        pltpu.VMEM(row_shape.shape, F32), pltpu.VMEM(row_shape.shape, F32)],
        compiler_params=_cparams(1),
        name="od_prompt",
    )(h3, h3, h3, h3, h3, h3, gates, gates_r, b_pad, b_if.reshape(-1, 1))
    per_seq = lambda s: s.reshape((B, N_HEADS) + s.shape[1:])
    return (o_c, o_d, per_seq(sct), per_seq(dct), per_seq(dn[:n_chain]), per_seq(dm[:n_chain, 0]))


def _xattn_kernel(x_ref, wq_ref, mk_ref, mv_ref, wo_ref, g_ref, b_ref, o_ref):
    x = x_ref[...]
    q = _bdot(x, wq_ref[...])
    qb = (q * (DH_X ** -0.5)).astype(BF16)
    heads = range(N_HEADS)
    cols = [slice(h * DH_X, (h + 1) * DH_X) for h in heads]
    s = [lax.dot_general(qb[:, cols[h]], mk_ref[0, :, cols[h]], NT_DIMS, preferred_element_type=F32) for h in heads]
    p = [jnp.exp(s[h] - jnp.max(s[h], axis=1, keepdims=True)) for h in heads]
    l = [jnp.sum(p[h], axis=1, keepdims=True) for h in heads]
    pv = [jnp.dot(p[h].astype(BF16), mv_ref[0, :, cols[h]], preferred_element_type=F32) for h in heads]
    outs = [(pv[h] / l[h]).astype(BF16) for h in heads]
    y = jnp.dot(jnp.concatenate(outs, axis=1), wo_ref[...], preferred_element_type=F32)
    o_ref[...] = _layernorm(DN_ALPHA * x + y, g_ref[...], b_ref[...])


def _xattn_prompt(x, wq, mk, mv, wo, g, b, T, tm):
    M = x.shape[0]
    tm = min(tm, T)
    n_mem = mk.shape[1]
    per_b = T // tm
    full = lambda shape: pl.BlockSpec(shape, lambda i: (0,) * len(shape))
    mem = pl.BlockSpec((1, n_mem, D_MODEL), lambda i: (i // per_b, 0, 0))
    wq, wq_spec = _wspec(wq, (D_MODEL, D_MODEL), lambda i: (0, 0))
    wo, wo_spec = _wspec(wo, (D_MODEL, D_MODEL), lambda i: (0, 0))
    return pl.pallas_call(
        _xattn_kernel,
        out_shape=jax.ShapeDtypeStruct((M, D_MODEL), F32),
        grid=(M // tm,),
        in_specs=[pl.BlockSpec((tm, D_MODEL), lambda i: (i, 0)), wq_spec, mem, mem,
                  wo_spec, full((1, D_MODEL)), full((1, D_MODEL))],
        out_specs=pl.BlockSpec((tm, D_MODEL), lambda i: (i, 0)),
        compiler_params=_cparams(1),
        name="xattn_prompt",
    )(x, wq, mk, mv, wo, g.reshape(1, -1), b.reshape(1, -1))


def _ffn_kernel(x_ref, wu_ref, wg_ref, cw_ref, cb_ref, wd_ref, g_ref, b_ref, o_ref, tail_ref,
                xb_ref, acc_ref, stage_ref, carry_ref, *, tm, per_b, single):
    i, j = pl.program_id(0), pl.program_id(1)

    @pl.when(j == 0)
    def _():
        xb_ref[...] = x_ref[...].astype(BF16)

    @pl.when(i % per_b == 0)
    def _():
        carry_ref[j] = jnp.zeros(carry_ref.shape[1:], F32)

    xb = xb_ref[...]
    stage_ref[0:SUBLANES, :] = carry_ref[j]
    stage_ref[SUBLANES:, :] = jnp.dot(xb, wu_ref[...], preferred_element_type=F32)
    last = stage_ref[tm:tm + SUBLANES, :]
    carry_ref[j] = last
    tail_ref[0] = last
    conv = (cb_ref[...] + cw_ref[0:1, :] * stage_ref[SUBLANES - 2:SUBLANES - 2 + tm, :]
            + cw_ref[1:2, :] * stage_ref[SUBLANES - 1:SUBLANES - 1 + tm, :]
            + cw_ref[2:3, :] * stage_ref[SUBLANES:, :])
    gate = jnp.dot(xb, wg_ref[...], preferred_element_type=F32)
    y = _bdot(jax.nn.gelu(conv) * gate, wd_ref[...])
    if single:
        o_ref[...] = _layernorm(DN_ALPHA * x_ref[...] + y, g_ref[...], b_ref[...])
        return

    @pl.when(j == 0)
    def _():
        acc_ref[...] = y

    @pl.when(j > 0)
    def _():
        acc_ref[...] = acc_ref[...] + y

    @pl.when(j == pl.num_programs(1) - 1)
    def _():
        o_ref[...] = _layernorm(DN_ALPHA * x_ref[...] + acc_ref[...], g_ref[...], b_ref[...])


def _ffn_prompt(x, w_up, conv_w, conv_b, w_down, g, b, T, tm, tf):
    M = x.shape[0]
    d_ff = _wshape(w_down)[0]
    tm = min(tm, T)
    nf = d_ff // tf
    per_b = T // tm
    wmode = dict(pipeline_mode=pl.Buffered(1)) if nf == 1 else {}
    w_up_arr, up_spec = _wspec(w_up, (D_MODEL, tf), lambda i, j: (0, j), **wmode)
    _, gate_spec = _wspec(w_up, (D_MODEL, tf), lambda i, j: (0, nf + j), **wmode)
    w_down_arr, down_spec = _wspec(w_down, (tf, D_MODEL), lambda i, j: (j, 0), **wmode)
    return pl.pallas_call(
        functools.partial(_ffn_kernel, tm=tm, per_b=per_b, single=nf == 1),
        out_shape=(jax.ShapeDtypeStruct((M, D_MODEL), F32),
                   jax.ShapeDtypeStruct((M // tm, SUBLANES, d_ff), F32)),
        grid=(M // tm, nf),
        in_specs=[pl.BlockSpec((tm, D_MODEL), lambda i, j: (i, 0)),
                  up_spec, gate_spec,
                  pl.BlockSpec((3, tf), lambda i, j: (0, j)),
                  pl.BlockSpec((1, tf), lambda i, j: (0, j)),
                  down_spec,
                  pl.BlockSpec((1, D_MODEL), lambda i, j: (0, 0)),
                  pl.BlockSpec((1, D_MODEL), lambda i, j: (0, 0))],
        out_specs=(pl.BlockSpec((tm, D_MODEL), lambda i, j: (i, 0)),
                   pl.BlockSpec((1, SUBLANES, tf), lambda i, j: (i, 0, j))),
        scratch_shapes=[pltpu.VMEM((tm, D_MODEL), BF16), pltpu.VMEM((tm, D_MODEL), F32),
                        pltpu.VMEM((tm + SUBLANES, tf), F32), pltpu.VMEM((nf, SUBLANES, tf), F32)],
        compiler_params=_cparams(2),
        name="ffn_prompt",
    )(x, w_up_arr, w_up_arr, conv_w, conv_b.reshape(1, -1), w_down_arr, g.reshape(1, -1), b.reshape(1, -1))


def _dattn_sample_kernel(*refs, n_pages, page, lam_init, rows):
    lam_ref, g_ref, q_ref, kn_ref, vn_ref = refs[1:6]
    o_ref = refs[6 + 2 * rows * n_pages]
    for r in range(rows):
        k_refs = refs[6 + r * n_pages:6 + (r + 1) * n_pages]
        v_refs = refs[6 + (rows + r) * n_pages:6 + (rows + r + 1) * n_pages]
        _dattn_sample_row(lam_ref, g_ref, q_ref.at[r], kn_ref.at[r], vn_ref.at[r], k_refs, v_refs, o_ref.at[r],
                          n_pages=n_pages, page=page, lam_init=lam_init)


def _dattn_sample_row(lam_ref, g_ref, q_ref, kn_ref, vn_ref, k_refs, v_refs, o_ref, *, n_pages, page, lam_init):
    n_rows = 2 * N_HEADS
    q = q_ref[...]
    lane = lax.broadcasted_iota(jnp.int32, (n_rows, MIX_W), 1)
    row = lax.broadcasted_iota(jnp.int32, (n_rows, MIX_W), 0)
    qbd = jnp.where(lax.shift_right_logical(lane, 6) == row, jnp.broadcast_to(q, (n_rows, MIX_W)), 0.0)
    s_new = jnp.sum(qbd * kn_ref[...], axis=1, keepdims=True)
    scores = [_bdot(qbd, k_refs[p][0]) for p in range(n_pages)]
    m = s_new
    for s in scores:
        m = jnp.maximum(m, jnp.max(s, axis=1, keepdims=True))
    p_new = jnp.exp2(s_new - m)
    l = p_new
    vn = vn_ref[...]
    accs = [p_new * vn[:, h * HEAD_W:(h + 1) * HEAD_W] for h in range(N_HEADS)]
    for p in range(n_pages):
        pr = jnp.exp2(scores[p] - m)
        l = l + jnp.sum(pr, axis=1, keepdims=True)
        for h in range(N_HEADS):
            accs[h] = accs[h] + _bdot(pr, v_refs[p][0, pl.ds(h, page, stride=N_HEADS), :])
    lam = _lambda(lam_ref, lam_init)
    outs = []
    for h in range(N_HEADS):
        o = accs[h] / l
        d = o[2 * h:2 * h + 1, :] - lam * o[2 * h + 1:2 * h + 2, :]
        outs.append(_rmsnorm(d, g_ref[...]) * (1.0 - lam_init))
    o_ref[...] = jnp.concatenate(outs, axis=1)


def _dattn_sample(q, k_new, v_new, kt_pool, v_pool, pool_base, page_table, lam_p, subln_g, lam_init):
    DB = q.shape[0]
    n_pages = page_table.shape[1]
    page = kt_pool.shape[2]
    rows = math.gcd(DB, 2)
    tok = pl.BlockSpec((rows, 1, MIX_W), lambda b, pt: (b, 0, 0))

    def k_spec(r, p):
        return pl.BlockSpec((1, MIX_W, page), lambda b, pt: (pool_base + pt[rows * b + r, p], 0, 0))

    def v_spec(r, p):
        return pl.BlockSpec((1, page * N_HEADS, HEAD_W), lambda b, pt: (pool_base + pt[rows * b + r, p], 0, 0))

    pages = [(r, p) for r in range(rows) for p in range(n_pages)]
    grid_spec = pltpu.PrefetchScalarGridSpec(
        num_scalar_prefetch=1,
        grid=(DB // rows,),
        in_specs=[pl.BlockSpec((4, DH_A), lambda b, pt: (0, 0)),
                  pl.BlockSpec((1, HEAD_W), lambda b, pt: (0, 0)), tok, tok, tok]
                 + [k_spec(r, p) for r, p in pages] + [v_spec(r, p) for r, p in pages],
        out_specs=tok,
    )
    out = pl.pallas_call(
        functools.partial(_dattn_sample_kernel, n_pages=n_pages, page=page, lam_init=lam_init, rows=rows),
        out_shape=jax.ShapeDtypeStruct((DB, 1, MIX_W), F32),
        grid_spec=grid_spec,
        compiler_params=_cparams(1),
        name="dattn_sample",
    )(page_table, lam_p, subln_g.reshape(1, -1), q.reshape(DB, 1, MIX_W), k_new.reshape(DB, 1, MIX_W),
      v_new.reshape(DB, 1, MIX_W), *([kt_pool] * len(pages)), *([v_pool] * len(pages)))
    return out.reshape(DB, MIX_W)


def _hgrn_step_kernel(h_ref, s0_ref, lb_ref, ng_ref, o_ref, s_ref, *, bt):
    def body(i, carry):
        row = h_ref[i]
        heads = range(N_HEADS)
        blk = lambda n: row[:, n * MIX_W:(n + 1) * MIX_W]
        lb = lb_ref[...]
        f = lb + (1.0 - lb) * _sigmoid(blk(4))
        q, v = _silu(blk(3)), blk(5)
        f_col = [_col(_head(f, h)) for h in heads]
        s_new = [f_col[h] * s0_ref[i, h] + (1.0 - f_col[h]) * _head(v, h) for h in heads]
        for h in heads:
            s_ref[i, h] = s_new[h]
        o = [_bdot(jnp.broadcast_to(_head(q, h), (SUBLANES, HEAD_W)), s_new[h])[0:1, :] for h in heads]
        o = jnp.concatenate([_rmsnorm(o[h], ng_ref[...]) for h in heads], axis=1)
        o_ref[i] = o * _silu(blk(6))
        return carry

    lax.fori_loop(0, bt, body, 0, unroll=2)


def _hgrn_step(h, s0, lb, norm_g, bt):
    DB, W = h.shape
    bt = min(bt, DB)
    state = pl.BlockSpec((bt, N_HEADS, HEAD_W, HEAD_W), lambda i: (i, 0, 0, 0))
    o, s = pl.pallas_call(
        functools.partial(_hgrn_step_kernel, bt=bt),
        out_shape=(jax.ShapeDtypeStruct((DB, 1, MIX_W), F32), jax.ShapeDtypeStruct(s0.shape, F32)),
        grid=(DB // bt,),
        in_specs=[pl.BlockSpec((bt, 1, W), lambda i: (i, 0, 0)), state,
                  pl.BlockSpec((1, MIX_W), lambda i: (0, 0)), pl.BlockSpec((1, HEAD_W), lambda i: (0, 0))],
        out_specs=(pl.BlockSpec((bt, 1, MIX_W), lambda i: (i, 0, 0)), state),
        compiler_params=_cparams(1),
        name="hgrn_step",
    )(h.reshape(DB, 1, W), s0, lb.reshape(1, -1), norm_g.reshape(1, -1))
    return o.reshape(DB, MIX_W), s


def _od_step_kernel(h_ref, gt_ref, bif_ref, cos_ref, sin_ref, sc0_ref, dc0_ref, dn0_ref, dm0_ref, cng_ref, dng_ref,
                    oc_ref, od_ref, sc_ref, dc_ref, dn_ref, dm_ref, *, bt):
    scale = HEAD_W ** -0.5

    def body(i, carry):
        row = h_ref[i]
        cos, sin = cos_ref[...], sin_ref[...]
        gates = gt_ref[i] + bif_ref[...]
        m_row = dm0_ref[i]
        heads = range(N_HEADS)
        blk = lambda n: row[:, n * MIX_W:(n + 1) * MIX_W]
        q = _rope(blk(0), cos, sin, HEAD_W // 2)
        k = _rope(blk(1), cos, sin, HEAD_W // 2) * scale
        cv, dk, dv, dq = blk(2), blk(5), blk(6), blk(4) * scale
        lf = _log_sigmoid(gates)
        m_new = [jnp.maximum(lf[:, N_HEADS + h:N_HEADS + h + 1] + m_row[:, h:h + 1], gates[:, h:h + 1])
                 for h in heads]
        c_scale = [jnp.exp(lf[:, N_HEADS + h:N_HEADS + h + 1] + m_row[:, h:h + 1] - m_new[h]) for h in heads]
        kw = [_head(dk, h) * jnp.exp(gates[:, h:h + 1] - m_new[h]) for h in heads]
        k_col = [_col(_head(k, h)) for h in heads]
        kw_col = [_col(kw[h]) for h in heads]
        s_new = [math.exp(_log_gamma(h)) * sc0_ref[i, h] + k_col[h] * _head(cv, h) for h in heads]
        c_new = [c_scale[h] * dc0_ref[i, h] + kw_col[h] * _head(dv, h) for h in heads]
        n_new = [c_scale[h] * dn0_ref[i, h:h + 1, :] + kw[h] for h in heads]
        for h in heads:
            sc_ref[i, h] = s_new[h]
            dc_ref[i, h] = c_new[h]
            dn_ref[i, h:h + 1, :] = n_new[h]
        rows8 = lambda x: jnp.broadcast_to(x, (SUBLANES, HEAD_W))
        o = [_bdot(rows8(_head(q, h)), s_new[h])[0:1, :] for h in heads]
        num = [_bdot(rows8(_head(dq, h)), c_new[h])[0:1, :] for h in heads]
        den = [jnp.sum(_head(dq, h) * n_new[h], axis=1, keepdims=True) for h in heads]
        hh = [num[h] / jnp.maximum(jnp.abs(den[h]), jnp.exp(-m_new[h])) for h in heads]
        oc = jnp.concatenate([_groupnorm(o[h], cng_ref[...]) for h in heads], axis=1)
        od = jnp.concatenate([_groupnorm(hh[h], dng_ref[...]) for h in heads], axis=1)
        oc_ref[i] = oc * _silu(blk(3))
        od_ref[i] = od * _sigmoid(blk(7))
        dm_ref[i] = jnp.concatenate(m_new, axis=1)
        return carry

    lax.fori_loop(0, bt, body, 0, unroll=2)


def _od_step(h, gates, b_if, cos_t, sin_t, sc0, dc0, dn0, dm0, c_norm_g, d_norm_g, bt):
    DB, W = h.shape
    bt = min(bt, DB)
    mat = pl.BlockSpec((bt, N_HEADS, HEAD_W, HEAD_W), lambda i: (i, 0, 0, 0))
    nblk = pl.BlockSpec((bt, N_HEADS, HEAD_W), lambda i: (i, 0, 0))
    mblk = pl.BlockSpec((bt, 1, N_HEADS), lambda i: (i, 0, 0))
    vec = pl.BlockSpec((1, LANES), lambda i: (0, 0))
    out = pl.BlockSpec((bt, 1, MIX_W), lambda i: (i, 0, 0))
    b_pad = jnp.zeros((1, LANES), F32).at[0, :2 * N_HEADS].set(b_if)
    oc, od, sc, dc, dn, dm = pl.pallas_call(
        functools.partial(_od_step_kernel, bt=bt),
        out_shape=(jax.ShapeDtypeStruct((DB, 1, MIX_W), F32), jax.ShapeDtypeStruct((DB, 1, MIX_W), F32),
                   jax.ShapeDtypeStruct(sc0.shape, F32), jax.ShapeDtypeStruct(dc0.shape, F32),
                   jax.ShapeDtypeStruct(dn0.shape, F32), jax.ShapeDtypeStruct((DB, 1, N_HEADS), F32)),
        grid=(DB // bt,),
        in_specs=[pl.BlockSpec((bt, 1, W), lambda i: (i, 0, 0)),
                  pl.BlockSpec((bt, 1, LANES), lambda i: (i, 0, 0)), vec, vec, vec,
                  mat, mat, nblk, mblk, vec, vec],
        out_specs=(out, out, mat, mat, nblk, mblk),
        compiler_params=_cparams(1),
        name="od_step",
    )(h.reshape(DB, 1, W), gates.reshape(DB, 1, LANES), b_pad, cos_t, sin_t, sc0, dc0, dn0,
      dm0.reshape(DB, 1, N_HEADS), c_norm_g.reshape(1, -1), d_norm_g.reshape(1, -1))
    return oc.reshape(DB, MIX_W), od.reshape(DB, MIX_W), sc, dc, dn, dm.reshape(DB, N_HEADS)


X_HALVES = DH_X // LANES
X_ROWS = N_HEADS * X_HALVES


def _xattn_sample_kernel(q_ref, mk_ref, mv_ref, o_ref, *, xb):
    rows = range(xb)
    n_cols = mk_ref.shape[1]
    row = lax.broadcasted_iota(jnp.int32, (X_ROWS, n_cols), 0)
    col = lax.broadcasted_iota(jnp.int32, (X_ROWS, n_cols), 1)
    own = (col & (X_ROWS - 1)) == row
    own4 = own[:N_HEADS]
    q = [q_ref[t] * (DH_X ** -0.5) for t in rows]
    q2 = [jnp.concatenate([q[t][:, h * DH_X + c * LANES:h * DH_X + (c + 1) * LANES]
                           for c in range(X_HALVES) for h in range(N_HEADS)], axis=0) for t in rows]
    s2 = [jnp.where(own, _bdot_nt(q2[t], mk_ref[t]), 0.0) for t in rows]
    s4 = [jnp.where(own4, s2[t][:N_HEADS] + pltpu.roll(s2[t], n_cols - N_HEADS, 1)[N_HEADS:], NEG_INF) for t in rows]
    p4 = [jnp.exp(s4[t] - jnp.max(s4[t], axis=1, keepdims=True)) for t in rows]
    p4 = [p4[t] / jnp.sum(p4[t], axis=1, keepdims=True) for t in rows]
    p2 = [jnp.concatenate([p4[t], pltpu.roll(p4[t], N_HEADS, 1)], axis=0) for t in rows]
    o2 = [_bdot(p2[t], mv_ref[t]) for t in rows]
    for t in rows:
        o_ref[t] = jnp.concatenate([o2[t][c * N_HEADS + h:c * N_HEADS + h + 1, :]
                                    for h in range(N_HEADS) for c in range(X_HALVES)], axis=1)


def _xattn_sample(q, mk, mv, base):
    DB = q.shape[0]
    n_rows = mk.shape[1]
    xb = math.gcd(DB, 4)
    tok = pl.BlockSpec((xb, 1, D_MODEL), lambda b: (b, 0, 0))
    mem = pl.BlockSpec((xb, n_rows, LANES), lambda b: (base // xb + b, 0, 0))
    assert base % xb == 0
    out = pl.pallas_call(
        functools.partial(_xattn_sample_kernel, xb=xb),
        out_shape=jax.ShapeDtypeStruct((DB, 1, D_MODEL), F32),
        grid=(DB // xb,),
        in_specs=[tok, mem, mem],
        out_specs=tok,
        compiler_params=_cparams(1),
        name="xattn_sample",
    )(q.reshape(DB, 1, D_MODEL), mk, mv)
    return out.reshape(DB, D_MODEL)


def _mem_rows(cache):
    n_l, DB, n_mem = cache.shape[:3]
    c = cache.reshape(n_l, DB, n_mem, N_HEADS, X_HALVES, LANES).transpose(0, 1, 2, 4, 3, 5)
    return c.reshape(n_l * DB, n_mem * X_ROWS, LANES)


def _ffn_sample_kernel(x_ref, up_ref, gate_ref, b0_ref, b1_ref, cw_ref, cb_ref, wd_ref, g_ref, b_ref, o_ref, acc_ref):
    j = pl.program_id(0)
    conv = (cb_ref[...] + cw_ref[0:1, :] * b0_ref[...] + cw_ref[1:2, :] * b1_ref[...]
            + cw_ref[2:3, :] * up_ref[...])
    y = _bdot(jax.nn.gelu(conv) * gate_ref[...], wd_ref[...])

    @pl.when(j == 0)
    def _():
        acc_ref[...] = y

    @pl.when(j > 0)
    def _():
        acc_ref[...] = acc_ref[...] + y

    @pl.when(j == pl.num_programs(0) - 1)
    def _():
        o_ref[...] = _layernorm(DN_ALPHA * x_ref[...] + acc_ref[...], g_ref[...], b_ref[...])


def _ffn_sample(x, ug, buf0, buf1, conv_w, conv_b, w_down, g, b, tf):
    DB = x.shape[0]
    d_ff = _wshape(w_down)[0]
    nf = d_ff // tf
    ff = lambda off: pl.BlockSpec((DB, tf), lambda j: (0, off + j))
    full = lambda shape: pl.BlockSpec(shape, lambda j: (0,) * len(shape))
    w_down, down_spec = _wspec(w_down, (tf, D_MODEL), lambda j: (j, 0))
    return pl.pallas_call(
        _ffn_sample_kernel,
        out_shape=jax.ShapeDtypeStruct((DB, D_MODEL), F32),
        grid=(nf,),
        in_specs=[full((DB, D_MODEL)), ff(0), ff(nf), ff(0), ff(0),
                  pl.BlockSpec((3, tf), lambda j: (0, j)), pl.BlockSpec((1, tf), lambda j: (0, j)),
                  down_spec, full((1, D_MODEL)), full((1, D_MODEL))],
        out_specs=full((DB, D_MODEL)),
        scratch_shapes=[pltpu.VMEM((DB, D_MODEL), F32)],
        compiler_params=_cparams(1),
        name="ffn_sample",
    )(x, ug, ug, buf0, buf1, conv_w, conv_b.reshape(1, -1), w_down, g.reshape(1, -1), b.reshape(1, -1))


def kernel(x_prompt, x_sample, cache_a_k, cache_a_v, state_b, state_c, state_d_c, state_d_n, state_d_m,
           cache_mem_k, cache_mem_v, state_conv, page_table, mem_prompt,
           ev_w_in, ev_w_out, ev_lam, ev_subln_g, ev_lb_logits, ev_b_norm_g,
           od_w_in, od_b_if, od_w_out, od_c_norm_g, od_d_norm_g,
           ln_g, ln_b, xa_wq, xa_wkv, xa_wo, ffn_w_up, ffn_conv_w, ffn_conv_b, ffn_w_down):
    B, T, _ = x_prompt.shape
    DB, t_s, _ = x_sample.shape
    assert t_s == 1, "the sample group is a single decoding step"
    assert T % CHUNK == 0
    n_pool, page = cache_a_k.shape[1], cache_a_k.shape[2]
    n_pages = page_table.shape[1]
    past = n_pages * page
    n_mem = mem_prompt.shape[1]
    d_ff = ffn_w_down.shape[1]
    tf = d_ff // 2
    N = B * T
    TM = 512

    pos_p = jnp.arange(T)
    pos_s = jnp.full((DB,), past, jnp.int32)
    lb_table = jnp.cumsum(jax.nn.softmax(ev_lb_logits.astype(F32), axis=0), axis=0)

    xp = x_prompt.reshape(N, D_MODEL)
    xs = x_sample.reshape(DB, D_MODEL)
    outs = {k: [] for k in ("ak_p", "av_p", "ak_s", "av_s", "sb_p", "sb_s", "sc_p", "sc_s", "dc_p", "dc_s",
                            "dn_p", "dn_s", "dm_p", "dm_s", "mk_p", "mv_p", "cv_p", "cv_s")}
    swap = lambda s: jnp.swapaxes(s, -1, -2)
    mem_k_rows, mem_v_rows = _mem_rows(cache_mem_k), _mem_rows(cache_mem_v)
    wq_all, wkv_all, wo_all = xa_wq.astype(BF16), xa_wkv.astype(BF16), xa_wo.astype(BF16)
    w_up_all, w_down_all = ffn_w_up.astype(BF16), ffn_w_down.astype(BF16)

    for l in range(DEPTH):
        j = l // 2
        if l % 2 == 0:
            lam_init = 0.8 - 0.6 * math.exp(-0.3 * l)
            w_in = ev_w_in[j].astype(BF16)
            w_out = ev_w_out[j].astype(BF16)
            cos_t, sin_t = _rope_tables(pos_p, DH_A)
            qt, kb, kt32, vt, v_rows, h_b = _ev_in_prompt(xp.reshape(B, T, D_MODEL), w_in, cos_t, sin_t, TM)
            o_a = _dattn_prompt(qt, kb, vt, ev_lam[j], ev_subln_g[j], lam_init, 512)
            o_b, st = _hgrn_prompt(h_b, lb_table[j], 1024)
            outs["ak_p"].append(kt32.reshape(B, N_HEADS, 2, DH_A, T).transpose(0, 4, 1, 2, 3))
            outs["av_p"].append(v_rows.reshape(B, T, N_HEADS, HEAD_W))
            outs["sb_p"].append(swap(st))
            xp = _proj_ln([o_a.reshape(N, MIX_W), o_b.reshape(N, MIX_W)], w_out,
                          xp, ln_g[l, 0], ln_b[l, 0], 2 * TM, "ev_out_prompt",
                          gates=[None, ("rms", ev_b_norm_g[j], "silu", h_b.reshape(N, -1), 3)])
            hs = _matmul(xs, w_in, DB, 512, "ev_in_sample")
            cos_s, sin_s = _rope_tables(pos_s, DH_A)
            qs, ks32, kts32 = _ev_prep_sample(hs, cos_s, sin_s)
            vs32 = hs[:, 2 * MIX_W:3 * MIX_W]
            kt_pool = cache_a_k.transpose(0, 1, 3, 4, 5, 2).reshape(-1, MIX_W, page)
            v_pool = cache_a_v.reshape(-1, page * N_HEADS, HEAD_W)
            oa_s = _dattn_sample(qs, ks32, vs32, kt_pool, v_pool, j * n_pool, page_table,
                                 ev_lam[j], ev_subln_g[j], lam_init)
            ob_s, sb_s = _hgrn_step(hs, state_b[j], lb_table[j], ev_b_norm_g[j], 8)
            outs["ak_s"].append(kts32.reshape(N_HEADS, 2, DH_A, DB).transpose(3, 0, 1, 2)[:, None])
            outs["av_s"].append(vs32.reshape(DB, 1, N_HEADS, HEAD_W))
            outs["sb_s"].append(sb_s)
            xs = _proj_ln([oa_s, ob_s], w_out, xs, ln_g[l, 0], ln_b[l, 0], DB, "ev_out_sample")
        else:
            n_main = 8 * MIX_W
            w_in = od_w_in[j][:, :n_main].astype(BF16)
            w_gate = jnp.pad(od_w_in[j][:, n_main:], ((0, 0), (0, LANES - 2 * N_HEADS))).astype(BF16)
            w_out = od_w_out[j].astype(BF16)
            cos_t, sin_t = _rope_tables(pos_p, HEAD_W)
            h, gates = _od_in_prompt(xp, w_in, w_gate, cos_t, sin_t, TM)
            o_c, o_d, sct, dct, dn, dm = _od_prompt(h.reshape(B, T, -1), gates.reshape(B, T, -1), od_b_if[j], 512)
            outs["sc_p"].append(swap(sct))
            outs["dc_p"].append(swap(dct))
            outs["dn_p"].append(dn)
            outs["dm_p"].append(dm)
            xp = _proj_ln([o_c.reshape(N, MIX_W), o_d.reshape(N, MIX_W)], w_out,
                          xp, ln_g[l, 0], ln_b[l, 0], 2 * TM, "od_out_prompt",
                          gates=[("group", od_c_norm_g[j], "silu", h, 3), ("group", od_d_norm_g[j], "sigmoid", h, 7)])
            hs = _matmul(xs, w_in, DB, 512, "od_in_sample")
            gates_s = _matmul(xs, w_gate, DB, LANES, "od_gates_sample")
            cos_s, sin_s = _rope_tables(pos_s[:1], HEAD_W)
            oc_s, od_s, sc_s, dc_s, dn_s, dm_s = _od_step(hs, gates_s, od_b_if[j], cos_s, sin_s,
                                                          state_c[j], state_d_c[j], state_d_n[j], state_d_m[j],
                                                          od_c_norm_g[j], od_d_norm_g[j], 8)
            outs["sc_s"].append(sc_s)
            outs["dc_s"].append(dc_s)
            outs["dn_s"].append(dn_s)
            outs["dm_s"].append(dm_s)
            xs = _proj_ln([oc_s, od_s], w_out, xs, ln_g[l, 0], ln_b[l, 0], DB, "od_out_sample")

        wq, wo = (wq_all, l), (wo_all, l)
        mkv = _matmul(mem_prompt.reshape(B * n_mem, D_MODEL), (wkv_all, l), 512, 512, "mem_kv")
        mk, mv = mkv[:, :D_MODEL], mkv[:, D_MODEL:]
        outs["mk_p"].append(mk.reshape(B, n_mem, N_HEADS, DH_X))
        outs["mv_p"].append(mv.reshape(B, n_mem, N_HEADS, DH_X))
        xp = _xattn_prompt(xp, wq, mk.astype(BF16).reshape(B, n_mem, D_MODEL),
                           mv.astype(BF16).reshape(B, n_mem, D_MODEL), wo, ln_g[l, 1], ln_b[l, 1], T, 1024)
        q_s = _matmul(xs, wq, DB, 512, "xattn_q_sample")
        xo_s = _xattn_sample(q_s, mem_k_rows, mem_v_rows, l * DB)
        xs = _proj_ln([xo_s], wo, xs, ln_g[l, 1], ln_b[l, 1], DB, "xattn_out_sample")

        w_up, w_down = (w_up_all, l), (w_down_all, l)
        tm_f = min(TM, T)
        xp, tails = _ffn_prompt(xp, w_up, ffn_conv_w[l], ffn_conv_b[l], w_down, ln_g[l, 2], ln_b[l, 2], T, TM, d_ff)
        tails = tails.reshape(B, T // tm_f, SUBLANES, d_ff)
        outs["cv_p"].append(tails[:, -1, SUBLANES - 2:, :])
        ug_s = _matmul(xs, w_up, DB, tf, "ffn_up_sample")
        buf = state_conv[l]
        xs = _ffn_sample(xs, ug_s, buf[:, 0, :], buf[:, 1, :], ffn_conv_w[l], ffn_conv_b[l], w_down,
                         ln_g[l, 2], ln_b[l, 2], tf)
        outs["cv_s"].append(jnp.stack([buf[:, 1, :], ug_s[:, :d_ff]], axis=1))

    st = lambda k: jnp.stack(outs[k])
    return (xp.reshape(B, T, D_MODEL), xs.reshape(DB, 1, D_MODEL),
            st("ak_p"), st("av_p"), st("ak_s"), st("av_s"), st("sb_p"), st("sb_s"),
            st("sc_p"), st("sc_s"), st("dc_p"), st("dc_s"), st("dn_p"), st("dn_s"), st("dm_p"), st("dm_s"),
            st("mk_p"), st("mv_p"), st("cv_p"), st("cv_s"))
```

```python
import functools
import math

import numpy as np
import jax
import jax.numpy as jnp
from jax import lax
from jax.experimental import pallas as pl
from jax.experimental.pallas import tpu as pltpu

F32 = jnp.float32
BF16 = jnp.bfloat16

D_MODEL = 1024
MIX_W = D_MODEL // 2
N_HEADS = 4
HEAD_W = MIX_W // N_HEADS
DH_A = HEAD_W // 2
DH_X = D_MODEL // N_HEADS
CHUNK = 64
OD_CHUNK = 128
ROPE_THETA = 10000.0
NORM_EPS = 1e-5
DEPTH = 2
DN_ALPHA = (2.0 * DEPTH) ** 0.25
LANES = 128
SUBLANES = 8
VMEM_LIMIT = 56 * 1024 * 1024
NEG_INF = float("-inf")

ROWS_MATMUL = 512
ROWS_LIGHT = 1024
ATTN_BLOCK = 512
TIME_BLOCK_EV = 1024
TIME_BLOCK_OD = 512
SAMPLE_COLS = 512
STEP_ROWS = 8

NT_DIMS = (((1,), (1,)), ((), ()))
TN_DIMS = (((0,), (0,)), ((), ()))


def _wspec(w, block_shape, index_map, **kw):
    if isinstance(w, tuple):
        stack, layer = w
        return stack, pl.BlockSpec((None,) + tuple(block_shape), lambda *a: (layer,) + tuple(index_map(*a)), **kw)
    return w, pl.BlockSpec(tuple(block_shape), index_map, **kw)


def _wshape(w):
    return w[0].shape[1:] if isinstance(w, tuple) else w.shape


def _cparams(n_axes, vmem=VMEM_LIMIT):
    return pltpu.CompilerParams(dimension_semantics=("arbitrary",) * n_axes, vmem_limit_bytes=vmem)


def _bdot(a, b):
    return jnp.dot(a.astype(BF16), b.astype(BF16), preferred_element_type=F32)


def _bdot_nt(a, b):
    return lax.dot_general(a.astype(BF16), b.astype(BF16), NT_DIMS, preferred_element_type=F32)


def _bdot_tn(a, b):
    return lax.dot_general(a.astype(BF16), b.astype(BF16), TN_DIMS, preferred_element_type=F32)


def _split3(x):
    p1 = x.astype(BF16)
    r1 = x - p1.astype(F32)
    p2 = r1.astype(BF16)
    p3 = (r1 - p2.astype(F32)).astype(BF16)
    return p1, p2, p3


def _cumsum_rows(tri, x):
    p1, p2, p3 = _split3(x)
    d = functools.partial(jnp.dot, preferred_element_type=F32)
    return d(tri, p1) + d(tri, p2) + d(tri, p3)


def _cumsum_lanes(x, triu):
    p1, p2, p3 = _split3(x)
    d = functools.partial(jnp.dot, preferred_element_type=F32)
    return d(p1, triu) + d(p2, triu) + d(p3, triu)


def _tri(L, lower):
    r = lax.broadcasted_iota(jnp.int32, (L, L), 0)
    c = lax.broadcasted_iota(jnp.int32, (L, L), 1)
    return (r >= c) if lower else (r <= c)


def _sigmoid(x):
    return 1.0 / (1.0 + jnp.exp(-x))


def _silu(x):
    return x * _sigmoid(x)


def _log_sigmoid(x):
    return jnp.minimum(x, 0.0) - jnp.log(1.0 + jnp.exp(-jnp.abs(x)))


def _layernorm(z, g, b):
    mu = jnp.mean(z, -1, keepdims=True)
    zc = z - mu
    var = jnp.mean(zc * zc, -1, keepdims=True)
    return zc * lax.rsqrt(var + NORM_EPS) * g + b


def _rmsnorm(x, g):
    return x * lax.rsqrt(jnp.mean(x * x, -1, keepdims=True) + NORM_EPS) * g


def _groupnorm(x, g):
    mu = jnp.mean(x, -1, keepdims=True)
    xc = x - mu
    var = jnp.mean(xc * xc, -1, keepdims=True)
    return xc * lax.rsqrt(var + NORM_EPS) * g


def _rope(x, cos, sin, half):
    outs = []
    for c in range(x.shape[1] // LANES):
        xc = x[:, c * LANES:(c + 1) * LANES]
        if 2 * half == LANES:
            sw = pltpu.roll(xc, half, 1)
        else:
            lane = lax.broadcasted_iota(jnp.int32, xc.shape, 1)
            first = (lane & (2 * half - 1)) < half
            sw = jnp.where(first, pltpu.roll(xc, LANES - half, 1), pltpu.roll(xc, half, 1))
        outs.append(xc * cos + sw * sin)
    return outs[0] if len(outs) == 1 else jnp.concatenate(outs, axis=1)


def _rope_tables(pos, d):
    inv = ROPE_THETA ** (-jnp.arange(0, d // 2, dtype=F32) * 2.0 / d)
    ang = pos.astype(F32)[:, None] * inv[None, :]
    cos, sin = jnp.cos(ang), jnp.sin(ang)
    reps = LANES // d
    cos_t = jnp.tile(jnp.concatenate([cos, cos], -1), (1, reps))
    sin_t = jnp.tile(jnp.concatenate([-sin, sin], -1), (1, reps))
    return cos_t, sin_t


def _col(row):
    return jnp.broadcast_to(row, (LANES, LANES)).T


def _mm_kernel(x_ref, w_ref, o_ref, xb_ref):
    @pl.when(pl.program_id(1) == 0)
    def _():
        xb_ref[...] = x_ref[...].astype(BF16)

    o_ref[...] = jnp.dot(xb_ref[...], w_ref[...], preferred_element_type=F32).astype(o_ref.dtype)


def _matmul(x, w, tm, tn, name, out_dtype=F32):
    M, K = x.shape
    N = _wshape(w)[1]
    tm, tn = min(tm, M), min(tn, N)
    w, w_spec = _wspec(w, (K, tn), lambda i, j: (0, j))
    return pl.pallas_call(
        _mm_kernel,
        out_shape=jax.ShapeDtypeStruct((M, N), out_dtype),
        grid=(M // tm, N // tn),
        in_specs=[pl.BlockSpec((tm, K), lambda i, j: (i, 0)), w_spec],
        out_specs=pl.BlockSpec((tm, tn), lambda i, j: (i, j)),
        scratch_shapes=[pltpu.VMEM((tm, K), BF16)],
        compiler_params=_cparams(2),
        name=name,
    )(x, w)


_HEAD_NORMS = {"rms": _rmsnorm, "group": _groupnorm}
_GATE_ACTS = {"silu": _silu, "sigmoid": _sigmoid}


def _proj_ln_kernel(*refs, n_in, gated):
    a_refs, w_refs = refs[:n_in], refs[n_in:2 * n_in]
    x_ref, g_ref, b_ref = refs[2 * n_in:2 * n_in + 3]
    extra, o_ref = refs[2 * n_in + 3:-1], refs[-1]
    y = None
    for a_ref, w_ref, gate in zip(a_refs, w_refs, gated):
        a = a_ref[...]
        if gate is not None:
            norm, act = _HEAD_NORMS[gate[0]], _GATE_ACTS[gate[1]]
            gain_ref, gate_ref, extra = extra[0], extra[1], extra[2:]
            a = jnp.concatenate([norm(_head(a, h), gain_ref[...]) for h in range(N_HEADS)], axis=1)
            a = a * act(gate_ref[...])
        ya = _bdot(a, w_ref[...])
        y = ya if y is None else y + ya
    o_ref[...] = _layernorm(DN_ALPHA * x_ref[...] + y, g_ref[...], b_ref[...])


def _proj_ln(acts, weight, x, g, b, tm, name, gates=None):
    M = x.shape[0]
    tm = min(tm, M)
    n_in = len(acts)
    gates = gates or [None] * n_in
    in_specs = [pl.BlockSpec((tm, a.shape[1]), lambda i: (i, 0)) for a in acts]
    w_args = []
    for r, a in enumerate(acts):
        w_arr, w_spec = _wspec(weight, (a.shape[1], D_MODEL), lambda i, r=r: (r, 0))
        w_args.append(w_arr)
        in_specs.append(w_spec)
    in_specs += [pl.BlockSpec((tm, D_MODEL), lambda i: (i, 0)),
                 pl.BlockSpec((1, D_MODEL), lambda i: (0, 0)),
                 pl.BlockSpec((1, D_MODEL), lambda i: (0, 0))]
    extra = []
    for gate in gates:
        if gate is not None:
            _, gain, _, gate_arr, blk = gate
            in_specs += [pl.BlockSpec((1, HEAD_W), lambda i: (0, 0)),
                         pl.BlockSpec((tm, MIX_W), lambda i, blk=blk: (i, blk))]
            extra += [gain.reshape(1, -1), gate_arr]
    return pl.pallas_call(
        functools.partial(_proj_ln_kernel, n_in=n_in,
                          gated=tuple(None if gt is None else (gt[0], gt[2]) for gt in gates)),
        out_shape=jax.ShapeDtypeStruct((M, D_MODEL), F32),
        grid=(M // tm,),
        in_specs=in_specs,
        out_specs=pl.BlockSpec((tm, D_MODEL), lambda i: (i, 0)),
        compiler_params=_cparams(1),
        name=name,
    )(*acts, *w_args, x, g.reshape(1, -1), b.reshape(1, -1), *extra)


Q_SCALE = DH_A ** -0.5 * math.log2(math.e)
VT_ROWS = HEAD_W + 16


def _ev_in_prompt_kernel(x_ref, w_ref, cos_ref, sin_ref, qt_ref, kb_ref, kt_ref, vt_ref, vr_ref, hb_ref):
    xb = x_ref[0].astype(BF16)
    cos, sin = cos_ref[...], sin_ref[...]
    proj = lambda lo, hi: jnp.dot(xb, w_ref[:, lo * MIX_W:hi * MIX_W], preferred_element_type=F32)
    hb_ref[0] = proj(3, 7)
    q = _rope(proj(0, 1), cos, sin, DH_A // 2)
    k = _rope(proj(1, 2), cos, sin, DH_A // 2)
    qt_ref[0] = (q * Q_SCALE).T.astype(BF16)
    kb_ref[0] = k.astype(BF16)
    kt_ref[0] = k.T
    v = proj(2, 3)
    tm = v.shape[0]
    vt = v.T.astype(BF16)
    ones = jnp.ones((VT_ROWS - HEAD_W, tm), BF16)
    for h in range(N_HEADS):
        vt_ref[0, h, :HEAD_W, :] = vt[h * HEAD_W:(h + 1) * HEAD_W]
        vt_ref[0, h, HEAD_W:, :] = ones
        vr_ref[0, pl.ds(h, tm, stride=N_HEADS), :] = _head(v, h)


def _ev_in_prompt(x3, w_in, cos_t, sin_t, tm):
    B, T, _ = x3.shape
    tm = min(tm, T)
    tab = pl.BlockSpec((tm, LANES), lambda b, i: (i, 0))
    tr = pl.BlockSpec((1, MIX_W, tm), lambda b, i: (b, 0, i))
    tshape = lambda dt: jax.ShapeDtypeStruct((B, MIX_W, T), dt)
    return pl.pallas_call(
        _ev_in_prompt_kernel,
        out_shape=(tshape(BF16), jax.ShapeDtypeStruct((B, T, MIX_W), BF16), tshape(F32),
                   jax.ShapeDtypeStruct((B, N_HEADS, VT_ROWS, T), BF16),
                   jax.ShapeDtypeStruct((B, T * N_HEADS, HEAD_W), F32),
                   jax.ShapeDtypeStruct((B, T, 4 * MIX_W), F32)),
        grid=(B, T // tm),
        in_specs=[pl.BlockSpec((1, tm, D_MODEL), lambda b, i: (b, i, 0)),
                  pl.BlockSpec(w_in.shape, lambda b, i: (0, 0)), tab, tab],
        out_specs=(tr, pl.BlockSpec((1, tm, MIX_W), lambda b, i: (b, i, 0)), tr,
                   pl.BlockSpec((1, N_HEADS, VT_ROWS, tm), lambda b, i: (b, 0, 0, i)),
                   pl.BlockSpec((1, tm * N_HEADS, HEAD_W), lambda b, i: (b, i, 0)),
                   pl.BlockSpec((1, tm, 4 * MIX_W), lambda b, i: (b, i, 0))),
        compiler_params=_cparams(2),
        name="ev_in_prompt",
    )(x3, w_in, cos_t, sin_t)


def _ev_prep_sample_kernel(qk_ref, cos_ref, sin_ref, q_ref, k_ref, kt_ref):
    cos, sin = cos_ref[...], sin_ref[...]
    k = _rope(qk_ref[:, MIX_W:], cos, sin, DH_A // 2)
    q_ref[...] = _rope(qk_ref[:, :MIX_W], cos, sin, DH_A // 2) * Q_SCALE
    k_ref[...] = k
    kt_ref[...] = k.T


def _ev_prep_sample(h, cos_t, sin_t):
    DB = h.shape[0]
    full = lambda shape: pl.BlockSpec(shape, lambda i: (0,) * len(shape))
    return pl.pallas_call(
        _ev_prep_sample_kernel,
        out_shape=(jax.ShapeDtypeStruct((DB, MIX_W), F32), jax.ShapeDtypeStruct((DB, MIX_W), F32),
                   jax.ShapeDtypeStruct((MIX_W, DB), F32)),
        grid=(1,),
        in_specs=[full((DB, 2 * MIX_W)), full((DB, LANES)), full((DB, LANES))],
        out_specs=(full((DB, MIX_W)), full((DB, MIX_W)), full((MIX_W, DB))),
        compiler_params=_cparams(1),
        name="ev_prep_sample",
    )(h, cos_t, sin_t)


def _lambda(lam_ref, lam_init):
    lp = lam_ref[...]
    s01 = jnp.sum(lp[0:1] * lp[1:2], axis=1, keepdims=True)
    s23 = jnp.sum(lp[2:3] * lp[3:4], axis=1, keepdims=True)
    return jnp.exp(s01) - jnp.exp(s23) + lam_init


def _dattn_kernel(lam_ref, g_ref, qt_ref, k_ref, vt_ref, o_ref, m_ref, acc_ref, *, tq, lam_init):
    i = pl.program_id(2)
    qt = qt_ref[0]
    sub = lax.broadcasted_iota(jnp.int32, qt.shape, 0)
    zero = jnp.zeros_like(qt)
    q_maps = [jnp.where(sub < DH_A, qt, zero), jnp.where(sub >= DH_A, qt, zero)]
    maps = range(2)
    m_ref[...] = jnp.full(m_ref.shape, NEG_INF, F32)
    acc_ref[...] = jnp.zeros(acc_ref.shape, F32)

    def steps(js):
        ks, vts, ss = [], [], []
        for j in js:
            start = pl.multiple_of(j * tq, tq)
            ks.append(k_ref[0, pl.ds(start, tq), :])
            vts.append(vt_ref[0, 0, :, pl.ds(start, tq)])
        for kj in ks:
            ss.append([jnp.dot(kj, q_maps[c], preferred_element_type=F32) for c in maps])
        for n in range(len(js)):
            for c in maps:
                s = ss[n][c]
                m_prev = m_ref[c]
                m_new = jnp.maximum(m_prev, jnp.max(s, axis=0, keepdims=True))
                alpha = jnp.exp2(m_prev - m_new)
                p = jnp.exp2(s - m_new)
                acc_ref[c] = alpha * acc_ref[c] + jnp.dot(vts[n], p.astype(BF16), preferred_element_type=F32)
                m_ref[c] = m_new

    def body(jj, carry):
        steps([2 * jj, 2 * jj + 1])
        return carry

    lax.fori_loop(0, lax.shift_right_logical(i, 1), body, 0)

    @pl.when((i & 1) == 1)
    def _():
        steps([i - 1])

    half = tq // 2
    start = pl.multiple_of(i * tq, tq)
    k_top, k_bot = k_ref[0, pl.ds(start, half), :], k_ref[0, pl.ds(start + half, half), :]
    vt_top, vt_bot = vt_ref[0, 0, :, pl.ds(start, half)], vt_ref[0, 0, :, pl.ds(start + half, half)]
    causal = lambda s: jnp.where(lax.broadcasted_iota(jnp.int32, s.shape, 0)
                                 <= lax.broadcasted_iota(jnp.int32, s.shape, 1), s, NEG_INF)
    s_top = [causal(jnp.dot(k_top, q_maps[c], preferred_element_type=F32)) for c in maps]
    s_bot = [causal(jnp.dot(k_bot, q_maps[c][:, half:], preferred_element_type=F32)) for c in maps]
    for c in maps:
        m_prev = m_ref[c]
        m_new = jnp.maximum(m_prev, jnp.max(s_top[c], axis=0, keepdims=True))
        m_new = jnp.concatenate([m_new[:, :half],
                                 jnp.maximum(m_new[:, half:], jnp.max(s_bot[c], axis=0, keepdims=True))], axis=1)
        p_top = jnp.exp2(s_top[c] - m_new).astype(BF16)
        p_bot = jnp.exp2(s_bot[c] - m_new[:, half:]).astype(BF16)
        acc_ref[c] = jnp.exp2(m_prev - m_new) * acc_ref[c] + jnp.dot(vt_top, p_top, preferred_element_type=F32)
        acc_ref[c, :, half:] = acc_ref[c, :, half:] + jnp.dot(vt_bot, p_bot, preferred_element_type=F32)
    lam = _lambda(lam_ref, lam_init)
    o = [acc_ref[c, :HEAD_W, :] / acc_ref[c, HEAD_W:HEAD_W + 1, :] for c in maps]
    d = (o[0] - lam * o[1]).T
    o_ref[0] = (_rmsnorm(d, g_ref[...]) * (1.0 - lam_init)).astype(BF16)


def _dattn_prompt(qt, kb, vt, lam_p, subln_g, lam_init, tq):
    B, T, _ = kb.shape
    tq = min(tq, T)
    return pl.pallas_call(
        functools.partial(_dattn_kernel, tq=tq, lam_init=lam_init),
        out_shape=jax.ShapeDtypeStruct((B, T, MIX_W), BF16),
        grid=(B, N_HEADS, T // tq),
        in_specs=[pl.BlockSpec((4, DH_A), lambda b, h, i: (0, 0)),
                  pl.BlockSpec((1, HEAD_W), lambda b, h, i: (0, 0)),
                  pl.BlockSpec((1, HEAD_W, tq), lambda b, h, i: (b, h, i)),
                  pl.BlockSpec((1, T, HEAD_W), lambda b, h, i: (b, 0, h)),
                  pl.BlockSpec((1, 1, VT_ROWS, T), lambda b, h, i: (b, h, 0, 0))],
        out_specs=pl.BlockSpec((1, tq, HEAD_W), lambda b, h, i: (b, i, h)),
        scratch_shapes=[pltpu.VMEM((2, 1, tq), F32), pltpu.VMEM((2, VT_ROWS, tq), F32)],
        compiler_params=_cparams(3),
        name="dattn_prompt",
    )(lam_p, subln_g.reshape(1, -1), qt, kb, vt)


def _head(x, h):
    return x[:, h * HEAD_W:(h + 1) * HEAD_W]


def _gla_chunk(seqs, st_ref, tril_mask):
    seqs = [tuple(x.astype(BF16) for x in s[:4]) + (s[4],) for s in seqs]
    chains = [(n, h) for n in range(len(seqs)) for h in range(N_HEADS)]
    part = lambda n, i, h: _head(seqs[n][i], h)
    sts = [st_ref[N_HEADS * n + h] for n, h in chains]
    inter = [_bdot_nt(part(n, 0, h), sts[c]) for c, (n, h) in enumerate(chains)]
    upd = [_bdot_tn(part(n, 3, h), part(n, 2, h)) for n, h in chains]
    attn = [jnp.where(tril_mask, _bdot_nt(part(n, 0, h), part(n, 1, h)), 0.0) for n, h in chains]
    intra = [_bdot(attn[c], part(n, 3, h)) for c, (n, h) in enumerate(chains)]
    for c, (n, h) in enumerate(chains):
        dec = seqs[n][4]
        st_ref[N_HEADS * n + h] = sts[c] * (dec[h] if isinstance(dec, (list, tuple)) else _head(dec, h)) + upd[c]
    outs = [intra[c] + inter[c] for c in range(len(chains))]
    return [outs[N_HEADS * n:N_HEADS * (n + 1)] for n in range(len(seqs))]


def _hgrn_kernel(q_ref, f_ref, i_ref, lb_ref, o_ref, s_ref, st_ref, *, tt, L, n_seq):
    t = pl.program_id(0)

    @pl.when(t == 0)
    def _():
        st_ref[...] = jnp.zeros(st_ref.shape, F32)

    tril_mask = _tri(L, True)
    tril = tril_mask.astype(BF16)

    def body(c, carry):
        rows = pl.ds(pl.multiple_of(c * L, L), L)
        lb = lb_ref[...]
        seqs = []
        for n in range(n_seq):
            f = lb + (1.0 - lb) * _sigmoid(f_ref[n, rows, :])
            b = _cumsum_rows(tril, jnp.log(f))
            b_end = b[L - 1:L, :]
            k = 1.0 - f
            q_in = _silu(q_ref[n, rows, :]) * jnp.exp(b)
            seqs.append((q_in, k * jnp.exp(-b), k * jnp.exp(b_end - b), i_ref[n, rows, :], jnp.exp(b_end)))
        for n, o in enumerate(_gla_chunk(seqs, st_ref, tril_mask)):
            o_ref[n, rows, :] = jnp.concatenate(o, axis=1)
        return carry

    lax.fori_loop(0, tt // L, body, 0, unroll=2)

    @pl.when(t == pl.num_programs(0) - 1)
    def _():
        s_ref[...] = st_ref[...]


def _hgrn_prompt(h3, lb, tt):
    B, T, _ = h3.shape
    tt = min(tt, T)
    L = math.gcd(T, CHUNK)
    col = lambda c: pl.BlockSpec((B, tt, MIX_W), lambda t: (0, t, c))
    o, st = pl.pallas_call(
        functools.partial(_hgrn_kernel, tt=tt, L=L, n_seq=B),
        out_shape=(jax.ShapeDtypeStruct((B, T, MIX_W), F32),
                   jax.ShapeDtypeStruct((B * N_HEADS, HEAD_W, HEAD_W), F32)),
        grid=(T // tt,),
        in_specs=[col(0), col(1), col(2), pl.BlockSpec((1, MIX_W), lambda t: (0, 0))],
        out_specs=(pl.BlockSpec((B, tt, MIX_W), lambda t: (0, t, 0)),
                   pl.BlockSpec((B * N_HEADS, HEAD_W, HEAD_W), lambda t: (0, 0, 0))),
        scratch_shapes=[pltpu.VMEM((B * N_HEADS, HEAD_W, HEAD_W), F32)],
        compiler_params=_cparams(1),
        name="hgrn_prompt",
    )(h3, h3, h3, lb.reshape(1, -1))
    return o, st.reshape(B, N_HEADS, HEAD_W, HEAD_W)


def _log_gamma(h):
    return float(np.log(1.0 - 2.0 ** (-5.0 - h)))


def _od_kernel(cq_ref, ck_ref, cv_ref, dq_ref, dk_ref, dv_ref, gc_ref, gr_ref,
               bc_ref, br_ref,
               oc_ref, od_ref, sc_ref, dc_ref, dn_ref, dm_ref,
               st_ref, ct_ref, n_ref, m_ref, *, tt, L, n_seq):
    t = pl.program_id(0)

    @pl.when(t == 0)
    def _():
        st_ref[...] = jnp.zeros(st_ref.shape, F32)
        ct_ref[...] = jnp.zeros(ct_ref.shape, F32)
        n_ref[...] = jnp.zeros(n_ref.shape, F32)
        m_ref[...] = jnp.zeros(m_ref.shape, F32)

    tril_mask = _tri(L, True)
    tril = tril_mask.astype(BF16)
    triu = _tri(L, False).astype(BF16)
    scale = HEAD_W ** -0.5
    heads = range(N_HEADS)
    pos1 = (lax.broadcasted_iota(jnp.int32, (L, HEAD_W), 0) + 1).astype(F32)
    ret_b = jnp.concatenate([pos1 * _log_gamma(h) for h in heads], axis=1)
    ret_b_end = jnp.concatenate([jnp.full((L, HEAD_W), L * _log_gamma(h), F32) for h in heads], axis=1)
    ret_q_dec, ret_k_dec, ret_k_end = jnp.exp(ret_b), jnp.exp(-ret_b), jnp.exp(ret_b_end - ret_b)
    ret_dec = [math.exp(L * _log_gamma(h)) for h in heads]

    seqs = range(n_seq)
    chains = [(n, h) for n in seqs for h in heads]

    def body(c, carry):
        rows = pl.ds(pl.multiple_of(c * L, L), L)
        ret = []
        for n in seqs:
            q, k = cq_ref[n, rows, :], ck_ref[n, rows, :]
            ret.append((q * ret_q_dec, k * ret_k_dec, k * ret_k_end, cv_ref[n, rows, :], ret_dec))
        for n, o in enumerate(_gla_chunk(ret, st_ref, tril_mask)):
            oc_ref[n, rows, :] = jnp.concatenate(o, axis=1)
        g_c = [gc_ref[n, rows, :] + bc_ref[...] for n in seqs]
        g_r = [gr_ref[n, c] + br_ref[...] for n in seqs]
        lf_c = [_log_sigmoid(g) for g in g_c]
        b_c = [_cumsum_rows(tril, jnp.concatenate(
            [jnp.broadcast_to(lf_c[n][:, N_HEADS + h:N_HEADS + h + 1], (L, HEAD_W)) for h in heads], axis=1))
            for n in seqs]
        b_r = [_cumsum_lanes(_log_sigmoid(g), triu) for g in g_r]
        dq = [dq_ref[n, rows, :] * scale for n in seqs]
        dk = [dk_ref[n, rows, :] for n in seqs]
        dqb = [x.astype(BF16) for x in dq]
        dkb = [x.astype(BF16) for x in dk]
        dvb = [dv_ref[n, rows, :].astype(BF16) for n in seqs]
        cts = [ct_ref[i] for i in range(len(chains))]
        q_c = [_bdot_nt(_head(dqb[n], h), cts[i]) for i, (n, h) in enumerate(chains)]
        q_k = [_bdot_nt(_head(dqb[n], h), _head(dkb[n], h)) for n, h in chains]
        m_prev = [m_ref[i:i + 1, :] for i in range(len(chains))]
        bcw, m_t, w_mat = [], [], []
        for i, (n, h) in enumerate(chains):
            bcw.append(_head(b_c[n], h))
            bc = bcw[i][:, :L]
            dm = jnp.where(tril_mask, bc - b_r[n][N_HEADS + h:N_HEADS + h + 1, :] + g_r[n][h:h + 1, :], NEG_INF)
            m_t.append(jnp.maximum(bcw[i] + m_prev[i], jnp.max(dm, axis=1, keepdims=True)))
            w_mat.append(jnp.exp(dm - m_t[i][:, :L]) * q_k[i])
        w_v = [_bdot(w_mat[i], _head(dvb[n], h)) for i, (n, h) in enumerate(chains)]
        kws, c_scales = [], []
        for i, (n, h) in enumerate(chains):
            m_new = m_t[i][L - 1:L, :]
            b_last = bcw[i][L - 1:L, :]
            c_scales.append(jnp.exp(b_last + m_prev[i] - m_new))
            kws.append(_head(dk[n], h) * jnp.exp(b_last - bcw[i] + g_c[n][:, h:h + 1] - m_new))
            m_ref[i:i + 1, :] = m_new
        c_upd = [_bdot_tn(_head(dvb[n], h), kws[i]) for i, (n, h) in enumerate(chains)]
        hs = []
        for i, (n, h) in enumerate(chains):
            n_row = n_ref[i:i + 1, :]
            inter = jnp.exp(bcw[i] + m_prev[i] - m_t[i])
            num = inter * q_c[i] + w_v[i]
            den = (inter[:, 0:1] * jnp.sum(_head(dq[n], h) * n_row, axis=1, keepdims=True)
                   + jnp.sum(w_mat[i], axis=1, keepdims=True))
            hs.append(num / jnp.maximum(jnp.abs(den), jnp.exp(-m_t[i])))
            ct_ref[i] = c_scales[i] * cts[i] + c_upd[i]
            n_ref[i:i + 1, :] = c_scales[i] * n_row + jnp.sum(kws[i], axis=0, keepdims=True)
        for n in seqs:
            od_ref[n, rows, :] = jnp.concatenate(hs[N_HEADS * n:N_HEADS * (n + 1)], axis=1)
        return carry

    lax.fori_loop(0, tt // L, body, 0)

    @pl.when(t == pl.num_programs(0) - 1)
    def _():
        sc_ref[...] = st_ref[...]
        dc_ref[...] = ct_ref[...]
        dn_ref[...] = n_ref[...]
        dm_ref[...] = m_ref[...]


def _od_in_kernel(x_ref, w_ref, wg_ref, cos_ref, sin_ref, h_ref, g_ref):
    xb = x_ref[...].astype(BF16)
    cos, sin = cos_ref[...], sin_ref[...]
    proj = lambda lo, hi: jnp.dot(xb, w_ref[:, lo * MIX_W:hi * MIX_W], preferred_element_type=F32)
    h_ref[:, 2 * MIX_W:] = proj(2, 8)
    h_ref[:, :MIX_W] = _rope(proj(0, 1), cos, sin, HEAD_W // 2)
    h_ref[:, MIX_W:2 * MIX_W] = _rope(proj(1, 2), cos, sin, HEAD_W // 2) * (HEAD_W ** -0.5)
    g_ref[...] = jnp.dot(xb, wg_ref[...], preferred_element_type=F32)


def _od_in_prompt(x, w_in, w_gate, cos_t, sin_t, tm):
    M, K = x.shape
    N = w_in.shape[1]
    tm = min(tm, cos_t.shape[0])
    n_tab = cos_t.shape[0] // tm
    tab = pl.BlockSpec((tm, LANES), lambda i: (i % n_tab, 0))
    return pl.pallas_call(
        _od_in_kernel,
        out_shape=(jax.ShapeDtypeStruct((M, N), F32), jax.ShapeDtypeStruct((M, LANES), F32)),
        grid=(M // tm,),
        in_specs=[pl.BlockSpec((tm, K), lambda i: (i, 0)),
                  pl.BlockSpec((K, N), lambda i: (0, 0)),
                  pl.BlockSpec((K, LANES), lambda i: (0, 0)), tab, tab],
        out_specs=(pl.BlockSpec((tm, N), lambda i: (i, 0)), pl.BlockSpec((tm, LANES), lambda i: (i, 0))),
        compiler_params=_cparams(1),
        name="od_in_prompt",
    )(x, w_in, w_gate, cos_t, sin_t)


def _od_prompt(h3, gates, b_if, tt):
    B, T, _ = h3.shape
    tt = min(tt, T)
    L = math.gcd(T, OD_CHUNK)
    gates_r = gates[:, :, :2 * N_HEADS].reshape(B, T // L, L, 2 * N_HEADS).transpose(0, 1, 3, 2)
    b_pad = jnp.zeros((1, LANES), F32).at[0, :2 * N_HEADS].set(b_if)
    n_chain = B * N_HEADS
    n_row = -(-n_chain // SUBLANES) * SUBLANES
    col = lambda c: pl.BlockSpec((B, tt, MIX_W), lambda t: (0, t, c))
    mat_state = pl.BlockSpec((n_chain, HEAD_W, HEAD_W), lambda t: (0, 0, 0))
    row_state = pl.BlockSpec((n_row, LANES), lambda t: (0, 0))
    out_blk = pl.BlockSpec((B, tt, MIX_W), lambda t: (0, t, 0))
    mat_shape = jax.ShapeDtypeStruct((n_chain, HEAD_W, HEAD_W), F32)
    row_shape = jax.ShapeDtypeStruct((n_row, LANES), F32)
    o_c, o_d, sct, dct, dn, dm = pl.pallas_call(
        functools.partial(_od_kernel, tt=tt, L=L, n_seq=B),
        out_shape=(jax.ShapeDtypeStruct((B, T, MIX_W), F32), jax.ShapeDtypeStruct((B, T, MIX_W), F32),
                   mat_shape, mat_shape, row_shape, row_shape),
        grid=(T // tt,),
        in_specs=[col(0), col(1), col(2), col(4), col(5), col(6),
                  pl.BlockSpec((B, tt, LANES), lambda t: (0, t, 0)),
                  pl.BlockSpec((B, tt // L, 2 * N_HEADS, L), lambda t: (0, t, 0, 0)),
                  pl.BlockSpec((1, LANES), lambda t: (0, 0)),
                  pl.BlockSpec((2 * N_HEADS, 1), lambda t: (0, 0))],
        out_specs=(out_blk, out_blk, mat_state, mat_state, row_state, row_state),
        scratch_shapes=[pltpu.VMEM(mat_shape.shape, F32), pltpu.VMEM(mat_shape.shape, F32),
                        pltpu.VMEM(row_shape.shape, F32), pltpu.VMEM(row_shape.shape, F32)],
        compiler_params=_cparams(1),
        name="od_prompt",
    )(h3, h3, h3, h3, h3, h3, gates, gates_r, b_pad, b_if.reshape(-1, 1))
    per_seq = lambda s: s.reshape((B, N_HEADS) + s.shape[1:])
    return (o_c, o_d, per_seq(sct), per_seq(dct), per_seq(dn[:n_chain]), per_seq(dm[:n_chain, 0]))


def _xattn_kernel(x_ref, wq_ref, mk_ref, mv_ref, wo_ref, g_ref, b_ref, o_ref):
    x = x_ref[...]
    q = _bdot(x, wq_ref[...])
    qb = (q * (DH_X ** -0.5)).astype(BF16)
    heads = range(N_HEADS)
    cols = [slice(h * DH_X, (h + 1) * DH_X) for h in heads]
    s = [lax.dot_general(qb[:, cols[h]], mk_ref[0, :, cols[h]], NT_DIMS, preferred_element_type=F32) for h in heads]
    p = [jnp.exp(s[h] - jnp.max(s[h], axis=1, keepdims=True)) for h in heads]
    l = [jnp.sum(p[h], axis=1, keepdims=True) for h in heads]
    pv = [jnp.dot(p[h].astype(BF16), mv_ref[0, :, cols[h]], preferred_element_type=F32) for h in heads]
    outs = [(pv[h] / l[h]).astype(BF16) for h in heads]
    y = jnp.dot(jnp.concatenate(outs, axis=1), wo_ref[...], preferred_element_type=F32)
    o_ref[...] = _layernorm(DN_ALPHA * x + y, g_ref[...], b_ref[...])


def _xattn_prompt(x, wq, mk, mv, wo, g, b, T, tm):
    M = x.shape[0]
    tm = min(tm, T)
    n_mem = mk.shape[1]
    per_b = T // tm
    full = lambda shape: pl.BlockSpec(shape, lambda i: (0,) * len(shape))
    mem = pl.BlockSpec((1, n_mem, D_MODEL), lambda i: (i // per_b, 0, 0))
    wq, wq_spec = _wspec(wq, (D_MODEL, D_MODEL), lambda i: (0, 0))
    wo, wo_spec = _wspec(wo, (D_MODEL, D_MODEL), lambda i: (0, 0))
    return pl.pallas_call(
        _xattn_kernel,
        out_shape=jax.ShapeDtypeStruct((M, D_MODEL), F32),
        grid=(M // tm,),
        in_specs=[pl.BlockSpec((tm, D_MODEL), lambda i: (i, 0)), wq_spec, mem, mem,
                  wo_spec, full((1, D_MODEL)), full((1, D_MODEL))],
        out_specs=pl.BlockSpec((tm, D_MODEL), lambda i: (i, 0)),
        compiler_params=_cparams(1),
        name="xattn_prompt",
    )(x, wq, mk, mv, wo, g.reshape(1, -1), b.reshape(1, -1))


def _ffn_kernel(x_ref, wu_ref, wg_ref, cw_ref, cb_ref, wd_ref, g_ref, b_ref, o_ref, tail_ref,
                xb_ref, acc_ref, stage_ref, carry_ref, *, tm, per_b, single):
    i, j = pl.program_id(0), pl.program_id(1)

    @pl.when(j == 0)
    def _():
        xb_ref[...] = x_ref[...].astype(BF16)

    @pl.when(i % per_b == 0)
    def _():
        carry_ref[j] = jnp.zeros(carry_ref.shape[1:], F32)

    xb = xb_ref[...]
    stage_ref[0:SUBLANES, :] = carry_ref[j]
    stage_ref[SUBLANES:, :] = jnp.dot(xb, wu_ref[...], preferred_element_type=F32)
    last = stage_ref[tm:tm + SUBLANES, :]
    carry_ref[j] = last
    tail_ref[0] = last
    conv = (cb_ref[...] + cw_ref[0:1, :] * stage_ref[SUBLANES - 2:SUBLANES - 2 + tm, :]
            + cw_ref[1:2, :] * stage_ref[SUBLANES - 1:SUBLANES - 1 + tm, :]
            + cw_ref[2:3, :] * stage_ref[SUBLANES:, :])
    gate = jnp.dot(xb, wg_ref[...], preferred_element_type=F32)
    y = _bdot(jax.nn.gelu(conv) * gate, wd_ref[...])
    if single:
        o_ref[...] = _layernorm(DN_ALPHA * x_ref[...] + y, g_ref[...], b_ref[...])
        return

    @pl.when(j == 0)
    def _():
        acc_ref[...] = y

    @pl.when(j > 0)
    def _():
        acc_ref[...] = acc_ref[...] + y

    @pl.when(j == pl.num_programs(1) - 1)
    def _():
        o_ref[...] = _layernorm(DN_ALPHA * x_ref[...] + acc_ref[...], g_ref[...], b_ref[...])


def _ffn_prompt(x, w_up, conv_w, conv_b, w_down, g, b, T, tm, tf):
    M = x.shape[0]
    d_ff = _wshape(w_down)[0]
    tm = min(tm, T)
    nf = d_ff // tf
    per_b = T // tm
    wmode = dict(pipeline_mode=pl.Buffered(1)) if nf == 1 else {}
    w_up_arr, up_spec = _wspec(w_up, (D_MODEL, tf), lambda i, j: (0, j), **wmode)
    _, gate_spec = _wspec(w_up, (D_MODEL, tf), lambda i, j: (0, nf + j), **wmode)
    w_down_arr, down_spec = _wspec(w_down, (tf, D_MODEL), lambda i, j: (j, 0), **wmode)
    return pl.pallas_call(
        functools.partial(_ffn_kernel, tm=tm, per_b=per_b, single=nf == 1),
        out_shape=(jax.ShapeDtypeStruct((M, D_MODEL), F32),
                   jax.ShapeDtypeStruct((M // tm, SUBLANES, d_ff), F32)),
        grid=(M // tm, nf),
        in_specs=[pl.BlockSpec((tm, D_MODEL), lambda i, j: (i, 0)),
                  up_spec, gate_spec,
                  pl.BlockSpec((3, tf), lambda i, j: (0, j)),
                  pl.BlockSpec((1, tf), lambda i, j: (0, j)),
                  down_spec,
                  pl.BlockSpec((1, D_MODEL), lambda i, j: (0, 0)),
                  pl.BlockSpec((1, D_MODEL), lambda i, j: (0, 0))],
        out_specs=(pl.BlockSpec((tm, D_MODEL), lambda i, j: (i, 0)),
                   pl.BlockSpec((1, SUBLANES, tf), lambda i, j: (i, 0, j))),
        scratch_shapes=[pltpu.VMEM((tm, D_MODEL), BF16), pltpu.VMEM((tm, D_MODEL), F32),
                        pltpu.VMEM((tm + SUBLANES, tf), F32), pltpu.VMEM((nf, SUBLANES, tf), F32)],
        compiler_params=_cparams(2),
        name="ffn_prompt",
    )(x, w_up_arr, w_up_arr, conv_w, conv_b.reshape(1, -1), w_down_arr, g.reshape(1, -1), b.reshape(1, -1))


def _dattn_sample_kernel(*refs, n_pages, page, lam_init, rows):
    lam_ref, g_ref, q_ref, kn_ref, vn_ref = refs[1:6]
    o_ref = refs[6 + 2 * rows * n_pages]
    for r in range(rows):
        k_refs = refs[6 + r * n_pages:6 + (r + 1) * n_pages]
        v_refs = refs[6 + (rows + r) * n_pages:6 + (rows + r + 1) * n_pages]
        _dattn_sample_row(lam_ref, g_ref, q_ref.at[r], kn_ref.at[r], vn_ref.at[r], k_refs, v_refs, o_ref.at[r],
                          n_pages=n_pages, page=page, lam_init=lam_init)


def _dattn_sample_row(lam_ref, g_ref, q_ref, kn_ref, vn_ref, k_refs, v_refs, o_ref, *, n_pages, page, lam_init):
    n_rows = 2 * N_HEADS
    q = q_ref[...]
    lane = lax.broadcasted_iota(jnp.int32, (n_rows, MIX_W), 1)
    row = lax.broadcasted_iota(jnp.int32, (n_rows, MIX_W), 0)
    qbd = jnp.where(lax.shift_right_logical(lane, 6) == row, jnp.broadcast_to(q, (n_rows, MIX_W)), 0.0)
    s_new = jnp.sum(qbd * kn_ref[...], axis=1, keepdims=True)
    scores = [_bdot(qbd, k_refs[p][0]) for p in range(n_pages)]
    m = s_new
    for s in scores:
        m = jnp.maximum(m, jnp.max(s, axis=1, keepdims=True))
    p_new = jnp.exp2(s_new - m)
    l = p_new
    vn = vn_ref[...]
    accs = [p_new * vn[:, h * HEAD_W:(h + 1) * HEAD_W] for h in range(N_HEADS)]
    for p in range(n_pages):
        pr = jnp.exp2(scores[p] - m)
        l = l + jnp.sum(pr, axis=1, keepdims=True)
        for h in range(N_HEADS):
            accs[h] = accs[h] + _bdot(pr, v_refs[p][0, pl.ds(h, page, stride=N_HEADS), :])
    lam = _lambda(lam_ref, lam_init)
    outs = []
    for h in range(N_HEADS):
        o = accs[h] / l
        d = o[2 * h:2 * h + 1, :] - lam * o[2 * h + 1:2 * h + 2, :]
        outs.append(_rmsnorm(d, g_ref[...]) * (1.0 - lam_init))
    o_ref[...] = jnp.concatenate(outs, axis=1)


def _dattn_sample(q, k_new, v_new, kt_pool, v_pool, pool_base, page_table, lam_p, subln_g, lam_init):
    DB = q.shape[0]
    n_pages = page_table.shape[1]
    page = kt_pool.shape[2]
    rows = math.gcd(DB, 2)
    tok = pl.BlockSpec((rows, 1, MIX_W), lambda b, pt: (b, 0, 0))

    def k_spec(r, p):
        return pl.BlockSpec((1, MIX_W, page), lambda b, pt: (pool_base + pt[rows * b + r, p], 0, 0))

    def v_spec(r, p):
        return pl.BlockSpec((1, page * N_HEADS, HEAD_W), lambda b, pt: (pool_base + pt[rows * b + r, p], 0, 0))

    pages = [(r, p) for r in range(rows) for p in range(n_pages)]
    grid_spec = pltpu.PrefetchScalarGridSpec(
        num_scalar_prefetch=1,
        grid=(DB // rows,),
        in_specs=[pl.BlockSpec((4, DH_A), lambda b, pt: (0, 0)),
                  pl.BlockSpec((1, HEAD_W), lambda b, pt: (0, 0)), tok, tok, tok]
                 + [k_spec(r, p) for r, p in pages] + [v_spec(r, p) for r, p in pages],
        out_specs=tok,
    )
    out = pl.pallas_call(
        functools.partial(_dattn_sample_kernel, n_pages=n_pages, page=page, lam_init=lam_init, rows=rows),
        out_shape=jax.ShapeDtypeStruct((DB, 1, MIX_W), F32),
        grid_spec=grid_spec,
        compiler_params=_cparams(1),
        name="dattn_sample",
    )(page_table, lam_p, subln_g.reshape(1, -1), q.reshape(DB, 1, MIX_W), k_new.reshape(DB, 1, MIX_W),
      v_new.reshape(DB, 1, MIX_W), *([kt_pool] * len(pages)), *([v_pool] * len(pages)))
    return out.reshape(DB, MIX_W)


def _hgrn_step_kernel(h_ref, s0_ref, lb_ref, ng_ref, o_ref, s_ref, *, bt):
    def body(i, carry):
        row = h_ref[i]
        heads = range(N_HEADS)
        blk = lambda n: row[:, n * MIX_W:(n + 1) * MIX_W]
        lb = lb_ref[...]
        f = lb + (1.0 - lb) * _sigmoid(blk(4))
        q, v = _silu(blk(3)), blk(5)
        f_col = [_col(_head(f, h)) for h in heads]
        s_new = [f_col[h] * s0_ref[i, h] + (1.0 - f_col[h]) * _head(v, h) for h in heads]
        for h in heads:
            s_ref[i, h] = s_new[h]
        o = [_bdot(jnp.broadcast_to(_head(q, h), (SUBLANES, HEAD_W)), s_new[h])[0:1, :] for h in heads]
        o = jnp.concatenate([_rmsnorm(o[h], ng_ref[...]) for h in heads], axis=1)
        o_ref[i] = o * _silu(blk(6))
        return carry

    lax.fori_loop(0, bt, body, 0, unroll=2)


def _hgrn_step(h, s0, lb, norm_g, bt):
    DB, W = h.shape
    bt = min(bt, DB)
    state = pl.BlockSpec((bt, N_HEADS, HEAD_W, HEAD_W), lambda i: (i, 0, 0, 0))
    o, s = pl.pallas_call(
        functools.partial(_hgrn_step_kernel, bt=bt),
        out_shape=(jax.ShapeDtypeStruct((DB, 1, MIX_W), F32), jax.ShapeDtypeStruct(s0.shape, F32)),
        grid=(DB // bt,),
        in_specs=[pl.BlockSpec((bt, 1, W), lambda i: (i, 0, 0)), state,
                  pl.BlockSpec((1, MIX_W), lambda i: (0, 0)), pl.BlockSpec((1, HEAD_W), lambda i: (0, 0))],
        out_specs=(pl.BlockSpec((bt, 1, MIX_W), lambda i: (i, 0, 0)), state),
        compiler_params=_cparams(1),
        name="hgrn_step",
    )(h.reshape(DB, 1, W), s0, lb.reshape(1, -1), norm_g.reshape(1, -1))
    return o.reshape(DB, MIX_W), s


def _od_step_kernel(h_ref, gt_ref, bif_ref, cos_ref, sin_ref, sc0_ref, dc0_ref, dn0_ref, dm0_ref, cng_ref, dng_ref,
                    oc_ref, od_ref, sc_ref, dc_ref, dn_ref, dm_ref, *, bt):
    scale = HEAD_W ** -0.5

    def body(i, carry):
        row = h_ref[i]
        cos, sin = cos_ref[...], sin_ref[...]
        gates = gt_ref[i] + bif_ref[...]
        m_row = dm0_ref[i]
        heads = range(N_HEADS)
        blk = lambda n: row[:, n * MIX_W:(n + 1) * MIX_W]
        q = _rope(blk(0), cos, sin, HEAD_W // 2)
        k = _rope(blk(1), cos, sin, HEAD_W // 2) * scale
        cv, dk, dv, dq = blk(2), blk(5), blk(6), blk(4) * scale
        lf = _log_sigmoid(gates)
        m_new = [jnp.maximum(lf[:, N_HEADS + h:N_HEADS + h + 1] + m_row[:, h:h + 1], gates[:, h:h + 1])
                 for h in heads]
        c_scale = [jnp.exp(lf[:, N_HEADS + h:N_HEADS + h + 1] + m_row[:, h:h + 1] - m_new[h]) for h in heads]
        kw = [_head(dk, h) * jnp.exp(gates[:, h:h + 1] - m_new[h]) for h in heads]
        k_col = [_col(_head(k, h)) for h in heads]
        kw_col = [_col(kw[h]) for h in heads]
        s_new = [math.exp(_log_gamma(h)) * sc0_ref[i, h] + k_col[h] * _head(cv, h) for h in heads]
        c_new = [c_scale[h] * dc0_ref[i, h] + kw_col[h] * _head(dv, h) for h in heads]
        n_new = [c_scale[h] * dn0_ref[i, h:h + 1, :] + kw[h] for h in heads]
        for h in heads:
            sc_ref[i, h] = s_new[h]
            dc_ref[i, h] = c_new[h]
            dn_ref[i, h:h + 1, :] = n_new[h]
        rows8 = lambda x: jnp.broadcast_to(x, (SUBLANES, HEAD_W))
        o = [_bdot(rows8(_head(q, h)), s_new[h])[0:1, :] for h in heads]
        num = [_bdot(rows8(_head(dq, h)), c_new[h])[0:1, :] for h in heads]
        den = [jnp.sum(_head(dq, h) * n_new[h], axis=1, keepdims=True) for h in heads]
        hh = [num[h] / jnp.maximum(jnp.abs(den[h]), jnp.exp(-m_new[h])) for h in heads]
        oc = jnp.concatenate([_groupnorm(o[h], cng_ref[...]) for h in heads], axis=1)
        od = jnp.concatenate([_groupnorm(hh[h], dng_ref[...]) for h in heads], axis=1)
        oc_ref[i] = oc * _silu(blk(3))
        od_ref[i] = od * _sigmoid(blk(7))
        dm_ref[i] = jnp.concatenate(m_new, axis=1)
        return carry

    lax.fori_loop(0, bt, body, 0, unroll=2)


def _od_step(h, gates, b_if, cos_t, sin_t, sc0, dc0, dn0, dm0, c_norm_g, d_norm_g, bt):
    DB, W = h.shape
    bt = min(bt, DB)
    mat = pl.BlockSpec((bt, N_HEADS, HEAD_W, HEAD_W), lambda i: (i, 0, 0, 0))
    nblk = pl.BlockSpec((bt, N_HEADS, HEAD_W), lambda i: (i, 0, 0))
    mblk = pl.BlockSpec((bt, 1, N_HEADS), lambda i: (i, 0, 0))
    vec = pl.BlockSpec((1, LANES), lambda i: (0, 0))
    out = pl.BlockSpec((bt, 1, MIX_W), lambda i: (i, 0, 0))
    b_pad = jnp.zeros((1, LANES), F32).at[0, :2 * N_HEADS].set(b_if)
    oc, od, sc, dc, dn, dm = pl.pallas_call(
        functools.partial(_od_step_kernel, bt=bt),
        out_shape=(jax.ShapeDtypeStruct((DB, 1, MIX_W), F32), jax.ShapeDtypeStruct((DB, 1, MIX_W), F32),
                   jax.ShapeDtypeStruct(sc0.shape, F32), jax.ShapeDtypeStruct(dc0.shape, F32),
                   jax.ShapeDtypeStruct(dn0.shape, F32), jax.ShapeDtypeStruct((DB, 1, N_HEADS), F32)),
        grid=(DB // bt,),
        in_specs=[pl.BlockSpec((bt, 1, W), lambda i: (i, 0, 0)),
                  pl.BlockSpec((bt, 1, LANES), lambda i: (i, 0, 0)), vec, vec, vec,
                  mat, mat, nblk, mblk, vec, vec],
        out_specs=(out, out, mat, mat, nblk, mblk),
        compiler_params=_cparams(1),
        name="od_step",
    )(h.reshape(DB, 1, W), gates.reshape(DB, 1, LANES), b_pad, cos_t, sin_t, sc0, dc0, dn0,
      dm0.reshape(DB, 1, N_HEADS), c_norm_g.reshape(1, -1), d_norm_g.reshape(1, -1))
    return oc.reshape(DB, MIX_W), od.reshape(DB, MIX_W), sc, dc, dn, dm.reshape(DB, N_HEADS)


X_HALVES = DH_X // LANES
X_ROWS = N_HEADS * X_HALVES


def _xattn_sample_kernel(q_ref, mk_ref, mv_ref, o_ref, *, xb):
    rows = range(xb)
    n_cols = mk_ref.shape[1]
    row = lax.broadcasted_iota(jnp.int32, (X_ROWS, n_cols), 0)
    col = lax.broadcasted_iota(jnp.int32, (X_ROWS, n_cols), 1)
    own = (col & (X_ROWS - 1)) == row
    own4 = own[:N_HEADS]
    q = [q_ref[t] * (DH_X ** -0.5) for t in rows]
    q2 = [jnp.concatenate([q[t][:, h * DH_X + c * LANES:h * DH_X + (c + 1) * LANES]
                           for c in range(X_HALVES) for h in range(N_HEADS)], axis=0) for t in rows]
    s2 = [jnp.where(own, _bdot_nt(q2[t], mk_ref[t]), 0.0) for t in rows]
    s4 = [jnp.where(own4, s2[t][:N_HEADS] + pltpu.roll(s2[t], n_cols - N_HEADS, 1)[N_HEADS:], NEG_INF) for t in rows]
    p4 = [jnp.exp(s4[t] - jnp.max(s4[t], axis=1, keepdims=True)) for t in rows]
    p4 = [p4[t] / jnp.sum(p4[t], axis=1, keepdims=True) for t in rows]
    p2 = [jnp.concatenate([p4[t], pltpu.roll(p4[t], N_HEADS, 1)], axis=0) for t in rows]
    o2 = [_bdot(p2[t], mv_ref[t]) for t in rows]
    for t in rows:
        o_ref[t] = jnp.concatenate([o2[t][c * N_HEADS + h:c * N_HEADS + h + 1, :]
                                    for h in range(N_HEADS) for c in range(X_HALVES)], axis=1)


def _xattn_sample(q, mk, mv, base):
    DB = q.shape[0]
    n_rows = mk.shape[1]
    xb = math.gcd(DB, 4)
    tok = pl.BlockSpec((xb, 1, D_MODEL), lambda b: (b, 0, 0))
    mem = pl.BlockSpec((xb, n_rows, LANES), lambda b: (base // xb + b, 0, 0))
    assert base % xb == 0
    out = pl.pallas_call(
        functools.partial(_xattn_sample_kernel, xb=xb),
        out_shape=jax.ShapeDtypeStruct((DB, 1, D_MODEL), F32),
        grid=(DB // xb,),
        in_specs=[tok, mem, mem],
        out_specs=tok,
        compiler_params=_cparams(1),
        name="xattn_sample",
    )(q.reshape(DB, 1, D_MODEL), mk, mv)
    return out.reshape(DB, D_MODEL)


def _mem_rows(cache):
    n_l, DB, n_mem = cache.shape[:3]
    c = cache.reshape(n_l, DB, n_mem, N_HEADS, X_HALVES, LANES).transpose(0, 1, 2, 4, 3, 5)
    return c.reshape(n_l * DB, n_mem * X_ROWS, LANES)


def _ffn_sample_kernel(x_ref, up_ref, gate_ref, b0_ref, b1_ref, cw_ref, cb_ref, wd_ref, g_ref, b_ref, o_ref, acc_ref):
    j = pl.program_id(0)
    conv = (cb_ref[...] + cw_ref[0:1, :] * b0_ref[...] + cw_ref[1:2, :] * b1_ref[...]
            + cw_ref[2:3, :] * up_ref[...])
    y = _bdot(jax.nn.gelu(conv) * gate_ref[...], wd_ref[...])

    @pl.when(j == 0)
    def _():
        acc_ref[...] = y

    @pl.when(j > 0)
    def _():
        acc_ref[...] = acc_ref[...] + y

    @pl.when(j == pl.num_programs(0) - 1)
    def _():
        o_ref[...] = _layernorm(DN_ALPHA * x_ref[...] + acc_ref[...], g_ref[...], b_ref[...])


def _ffn_sample(x, ug, buf0, buf1, conv_w, conv_b, w_down, g, b, tf):
    DB = x.shape[0]
    d_ff = _wshape(w_down)[0]
    nf = d_ff // tf
    ff = lambda off: pl.BlockSpec((DB, tf), lambda j: (0, off + j))
    full = lambda shape: pl.BlockSpec(shape, lambda j: (0,) * len(shape))
    w_down, down_spec = _wspec(w_down, (tf, D_MODEL), lambda j: (j, 0))
    return pl.pallas_call(
        _ffn_sample_kernel,
        out_shape=jax.ShapeDtypeStruct((DB, D_MODEL), F32),
        grid=(nf,),
        in_specs=[full((DB, D_MODEL)), ff(0), ff(nf), ff(0), ff(0),
                  pl.BlockSpec((3, tf), lambda j: (0, j)), pl.BlockSpec((1, tf), lambda j: (0, j)),
                  down_spec, full((1, D_MODEL)), full((1, D_MODEL))],
        out_specs=full((DB, D_MODEL)),
        scratch_shapes=[pltpu.VMEM((DB, D_MODEL), F32)],
        compiler_params=_cparams(1),
        name="ffn_sample",
    )(x, ug, ug, buf0, buf1, conv_w, conv_b.reshape(1, -1), w_down, g.reshape(1, -1), b.reshape(1, -1))


def kernel(x_prompt, x_sample, cache_a_k, cache_a_v, state_b, state_c, state_d_c, state_d_n, state_d_m,
           cache_mem_k, cache_mem_v, state_conv, page_table, mem_prompt,
           ev_w_in, ev_w_out, ev_lam, ev_subln_g, ev_lb_logits, ev_b_norm_g,
           od_w_in, od_b_if, od_w_out, od_c_norm_g, od_d_norm_g,
           ln_g, ln_b, xa_wq, xa_wkv, xa_wo, ffn_w_up, ffn_conv_w, ffn_conv_b, ffn_w_down):
    B, T, _ = x_prompt.shape
    DB, t_s, _ = x_sample.shape
    assert t_s == 1, "the sample group is a single decoding step"
    assert T % CHUNK == 0
    n_pool, page = cache_a_k.shape[1], cache_a_k.shape[2]
    n_pages = page_table.shape[1]
    past = n_pages * page
    n_mem = mem_prompt.shape[1]
    d_ff = ffn_w_down.shape[1]
    tf = d_ff // 2
    N = B * T
    TM = ROWS_MATMUL

    pos_p = jnp.arange(T)
    pos_s = jnp.full((DB,), past, jnp.int32)
    lb_table = jnp.cumsum(jax.nn.softmax(ev_lb_logits.astype(F32), axis=0), axis=0)

    xp = x_prompt.reshape(N, D_MODEL)
    xs = x_sample.reshape(DB, D_MODEL)
    outs = {k: [] for k in ("ak_p", "av_p", "ak_s", "av_s", "sb_p", "sb_s", "sc_p", "sc_s", "dc_p", "dc_s",
                            "dn_p", "dn_s", "dm_p", "dm_s", "mk_p", "mv_p", "cv_p", "cv_s")}
    swap = lambda s: jnp.swapaxes(s, -1, -2)
    mem_k_rows, mem_v_rows = _mem_rows(cache_mem_k), _mem_rows(cache_mem_v)
    wq_all, wkv_all, wo_all = xa_wq.astype(BF16), xa_wkv.astype(BF16), xa_wo.astype(BF16)
    w_up_all, w_down_all = ffn_w_up.astype(BF16), ffn_w_down.astype(BF16)

    for l in range(DEPTH):
        j = l // 2
        if l % 2 == 0:
            lam_init = 0.8 - 0.6 * math.exp(-0.3 * l)
            w_in = ev_w_in[j].astype(BF16)
            w_out = ev_w_out[j].astype(BF16)
            cos_t, sin_t = _rope_tables(pos_p, DH_A)
            qt, kb, kt32, vt, v_rows, h_b = _ev_in_prompt(xp.reshape(B, T, D_MODEL), w_in, cos_t, sin_t, TM)
            o_a = _dattn_prompt(qt, kb, vt, ev_lam[j], ev_subln_g[j], lam_init, ATTN_BLOCK)
            o_b, st = _hgrn_prompt(h_b, lb_table[j], TIME_BLOCK_EV)
            outs["ak_p"].append(kt32.reshape(B, N_HEADS, 2, DH_A, T).transpose(0, 4, 1, 2, 3))
            outs["av_p"].append(v_rows.reshape(B, T, N_HEADS, HEAD_W))
            outs["sb_p"].append(swap(st))
            xp = _proj_ln([o_a.reshape(N, MIX_W), o_b.reshape(N, MIX_W)], w_out,
                          xp, ln_g[l, 0], ln_b[l, 0], ROWS_LIGHT, "ev_out_prompt",
                          gates=[None, ("rms", ev_b_norm_g[j], "silu", h_b.reshape(N, -1), 3)])
            hs = _matmul(xs, w_in, DB, SAMPLE_COLS, "ev_in_sample")
            cos_s, sin_s = _rope_tables(pos_s, DH_A)
            qs, ks32, kts32 = _ev_prep_sample(hs, cos_s, sin_s)
            vs32 = hs[:, 2 * MIX_W:3 * MIX_W]
            kt_pool = cache_a_k.transpose(0, 1, 3, 4, 5, 2).reshape(-1, MIX_W, page)
            v_pool = cache_a_v.reshape(-1, page * N_HEADS, HEAD_W)
            oa_s = _dattn_sample(qs, ks32, vs32, kt_pool, v_pool, j * n_pool, page_table,
                                 ev_lam[j], ev_subln_g[j], lam_init)
            ob_s, sb_s = _hgrn_step(hs, state_b[j], lb_table[j], ev_b_norm_g[j], STEP_ROWS)
            outs["ak_s"].append(kts32.reshape(N_HEADS, 2, DH_A, DB).transpose(3, 0, 1, 2)[:, None])
            outs["av_s"].append(vs32.reshape(DB, 1, N_HEADS, HEAD_W))
            outs["sb_s"].append(sb_s)
            xs = _proj_ln([oa_s, ob_s], w_out, xs, ln_g[l, 0], ln_b[l, 0], DB, "ev_out_sample")
        else:
            n_main = 8 * MIX_W
            w_in = od_w_in[j][:, :n_main].astype(BF16)
            w_gate = jnp.pad(od_w_in[j][:, n_main:], ((0, 0), (0, LANES - 2 * N_HEADS))).astype(BF16)
            w_out = od_w_out[j].astype(BF16)
            cos_t, sin_t = _rope_tables(pos_p, HEAD_W)
            h, gates = _od_in_prompt(xp, w_in, w_gate, cos_t, sin_t, TM)
            o_c, o_d, sct, dct, dn, dm = _od_prompt(h.reshape(B, T, -1), gates.reshape(B, T, -1), od_b_if[j],
                                                    TIME_BLOCK_OD)
            outs["sc_p"].append(swap(sct))
            outs["dc_p"].append(swap(dct))
            outs["dn_p"].append(dn)
            outs["dm_p"].append(dm)
            xp = _proj_ln([o_c.reshape(N, MIX_W), o_d.reshape(N, MIX_W)], w_out,
                          xp, ln_g[l, 0], ln_b[l, 0], ROWS_LIGHT, "od_out_prompt",
                          gates=[("group", od_c_norm_g[j], "silu", h, 3), ("group", od_d_norm_g[j], "sigmoid", h, 7)])
            hs = _matmul(xs, w_in, DB, SAMPLE_COLS, "od_in_sample")
            gates_s = _matmul(xs, w_gate, DB, LANES, "od_gates_sample")
            cos_s, sin_s = _rope_tables(pos_s[:1], HEAD_W)
            oc_s, od_s, sc_s, dc_s, dn_s, dm_s = _od_step(hs, gates_s, od_b_if[j], cos_s, sin_s,
                                                          state_c[j], state_d_c[j], state_d_n[j], state_d_m[j],
                                                          od_c_norm_g[j], od_d_norm_g[j], STEP_ROWS)
            outs["sc_s"].append(sc_s)
            outs["dc_s"].append(dc_s)
            outs["dn_s"].append(dn_s)
            outs["dm_s"].append(dm_s)
            xs = _proj_ln([oc_s, od_s], w_out, xs, ln_g[l, 0], ln_b[l, 0], DB, "od_out_sample")

        wq, wo = (wq_all, l), (wo_all, l)
        mkv = _matmul(mem_prompt.reshape(B * n_mem, D_MODEL), (wkv_all, l), ROWS_MATMUL, SAMPLE_COLS, "mem_kv")
        mk, mv = mkv[:, :D_MODEL], mkv[:, D_MODEL:]
        outs["mk_p"].append(mk.reshape(B, n_mem, N_HEADS, DH_X))
        outs["mv_p"].append(mv.reshape(B, n_mem, N_HEADS, DH_X))
        xp = _xattn_prompt(xp, wq, mk.astype(BF16).reshape(B, n_mem, D_MODEL),
                           mv.astype(BF16).reshape(B, n_mem, D_MODEL), wo, ln_g[l, 1], ln_b[l, 1], T, ROWS_LIGHT)
        q_s = _matmul(xs, wq, DB, SAMPLE_COLS, "xattn_q_sample")
        xo_s = _xattn_sample(q_s, mem_k_rows, mem_v_rows, l * DB)
        xs = _proj_ln([xo_s], wo, xs, ln_g[l, 1], ln_b[l, 1], DB, "xattn_out_sample")

        w_up, w_down = (w_up_all, l), (w_down_all, l)
        tm_f = min(TM, T)
        xp, tails = _ffn_prompt(xp, w_up, ffn_conv_w[l], ffn_conv_b[l], w_down, ln_g[l, 2], ln_b[l, 2], T, TM, d_ff)
        tails = tails.reshape(B, T // tm_f, SUBLANES, d_ff)
        outs["cv_p"].append(tails[:, -1, SUBLANES - 2:, :])
        ug_s = _matmul(xs, w_up, DB, tf, "ffn_up_sample")
        buf = state_conv[l]
        xs = _ffn_sample(xs, ug_s, buf[:, 0, :], buf[:, 1, :], ffn_conv_w[l], ffn_conv_b[l], w_down,
                         ln_g[l, 2], ln_b[l, 2], tf)
        outs["cv_s"].append(jnp.stack([buf[:, 1, :], ug_s[:, :d_ff]], axis=1))

    st = lambda k: jnp.stack(outs[k])
    return (xp.reshape(B, T, D_MODEL), xs.reshape(DB, 1, D_MODEL),
            st("ak_p"), st("av_p"), st("ak_s"), st("av_s"), st("sb_p"), st("sb_s"),
            st("sc_p"), st("sc_s"), st("dc_p"), st("dc_s"), st("dn_p"), st("dn_s"), st("dm_p"), st("dm_s"),
            st("mk_p"), st("mv_p"), st("cv_p"), st("cv_s"))
```

```python
import functools
import math

import numpy as np
import jax
import jax.numpy as jnp
from jax import lax
from jax.experimental import pallas as pl
from jax.experimental.pallas import tpu as pltpu

F32 = jnp.float32
BF16 = jnp.bfloat16

D_MODEL = 1024
MIX_W = D_MODEL // 2
N_HEADS = 4
HEAD_W = MIX_W // N_HEADS
DH_A = HEAD_W // 2
DH_X = D_MODEL // N_HEADS
CHUNK = 64
OD_CHUNK = 128
ROPE_THETA = 10000.0
NORM_EPS = 1e-5
DEPTH = 2
DN_ALPHA = (2.0 * DEPTH) ** 0.25
LANES = 128
SUBLANES = 8
VMEM_LIMIT = 56 * 1024 * 1024
NEG_INF = float("-inf")

ROWS_MATMUL = 512
ROWS_LIGHT = 1024
ATTN_BLOCK = 512
TIME_BLOCK_EV = 1024
TIME_BLOCK_OD = 512
SAMPLE_COL_TILES = 2
STEP_ROWS = 8

NT_DIMS = (((1,), (1,)), ((), ()))
TN_DIMS = (((0,), (0,)), ((), ()))


def _wspec(w, block_shape, index_map, **kw):
    if isinstance(w, tuple):
        stack, layer = w
        return stack, pl.BlockSpec((None,) + tuple(block_shape), lambda *a: (layer,) + tuple(index_map(*a)), **kw)
    return w, pl.BlockSpec(tuple(block_shape), index_map, **kw)


def _wshape(w):
    return w[0].shape[1:] if isinstance(w, tuple) else w.shape


def _cparams(n_axes, vmem=VMEM_LIMIT):
    return pltpu.CompilerParams(dimension_semantics=("arbitrary",) * n_axes, vmem_limit_bytes=vmem)


def _bdot(a, b):
    return jnp.dot(a.astype(BF16), b.astype(BF16), preferred_element_type=F32)


def _bdot_nt(a, b):
    return lax.dot_general(a.astype(BF16), b.astype(BF16), NT_DIMS, preferred_element_type=F32)


def _bdot_tn(a, b):
    return lax.dot_general(a.astype(BF16), b.astype(BF16), TN_DIMS, preferred_element_type=F32)


def _split3(x):
    p1 = x.astype(BF16)
    r1 = x - p1.astype(F32)
    p2 = r1.astype(BF16)
    p3 = (r1 - p2.astype(F32)).astype(BF16)
    return p1, p2, p3


def _cumsum_rows(tri, x):
    p1, p2, p3 = _split3(x)
    d = functools.partial(jnp.dot, preferred_element_type=F32)
    return d(tri, p1) + d(tri, p2) + d(tri, p3)


def _cumsum_lanes(x, triu):
    p1, p2, p3 = _split3(x)
    d = functools.partial(jnp.dot, preferred_element_type=F32)
    return d(p1, triu) + d(p2, triu) + d(p3, triu)


def _tri(L, lower):
    r = lax.broadcasted_iota(jnp.int32, (L, L), 0)
    c = lax.broadcasted_iota(jnp.int32, (L, L), 1)
    return (r >= c) if lower else (r <= c)


def _sigmoid(x):
    return 1.0 / (1.0 + jnp.exp(-x))


def _silu(x):
    return x * _sigmoid(x)


def _log_sigmoid(x):
    return jnp.minimum(x, 0.0) - jnp.log(1.0 + jnp.exp(-jnp.abs(x)))


def _layernorm(z, g, b):
    mu = jnp.mean(z, -1, keepdims=True)
    zc = z - mu
    var = jnp.mean(zc * zc, -1, keepdims=True)
    return zc * lax.rsqrt(var + NORM_EPS) * g + b


def _rmsnorm(x, g):
    return x * lax.rsqrt(jnp.mean(x * x, -1, keepdims=True) + NORM_EPS) * g


def _groupnorm(x, g):
    mu = jnp.mean(x, -1, keepdims=True)
    xc = x - mu
    var = jnp.mean(xc * xc, -1, keepdims=True)
    return xc * lax.rsqrt(var + NORM_EPS) * g


def _rope(x, cos, sin, half):
    outs = []
    for c in range(x.shape[1] // LANES):
        xc = x[:, c * LANES:(c + 1) * LANES]
        if 2 * half == LANES:
            sw = pltpu.roll(xc, half, 1)
        else:
            lane = lax.broadcasted_iota(jnp.int32, xc.shape, 1)
            first = (lane & (2 * half - 1)) < half
            sw = jnp.where(first, pltpu.roll(xc, LANES - half, 1), pltpu.roll(xc, half, 1))
        outs.append(xc * cos + sw * sin)
    return outs[0] if len(outs) == 1 else jnp.concatenate(outs, axis=1)


def _rope_tables(pos, d):
    inv = ROPE_THETA ** (-jnp.arange(0, d // 2, dtype=F32) * 2.0 / d)
    ang = pos.astype(F32)[:, None] * inv[None, :]
    cos, sin = jnp.cos(ang), jnp.sin(ang)
    reps = LANES // d
    cos_t = jnp.tile(jnp.concatenate([cos, cos], -1), (1, reps))
    sin_t = jnp.tile(jnp.concatenate([-sin, sin], -1), (1, reps))
    return cos_t, sin_t


def _col(row):
    return jnp.broadcast_to(row, (LANES, LANES)).T


def _mm_kernel(x_ref, w_ref, o_ref, xb_ref):
    @pl.when(pl.program_id(1) == 0)
    def _():
        xb_ref[...] = x_ref[...].astype(BF16)

    o_ref[...] = jnp.dot(xb_ref[...], w_ref[...], preferred_element_type=F32).astype(o_ref.dtype)


def _matmul(x, w, tm, tn, name, out_dtype=F32):
    M, K = x.shape
    N = _wshape(w)[1]
    tm, tn = min(tm, M), min(tn, N)
    w, w_spec = _wspec(w, (K, tn), lambda i, j: (0, j))
    return pl.pallas_call(
        _mm_kernel,
        out_shape=jax.ShapeDtypeStruct((M, N), out_dtype),
        grid=(M // tm, N // tn),
        in_specs=[pl.BlockSpec((tm, K), lambda i, j: (i, 0)), w_spec],
        out_specs=pl.BlockSpec((tm, tn), lambda i, j: (i, j)),
        scratch_shapes=[pltpu.VMEM((tm, K), BF16)],
        compiler_params=_cparams(2),
        name=name,
    )(x, w)


_HEAD_NORMS = {"rms": _rmsnorm, "group": _groupnorm}
_GATE_ACTS = {"silu": _silu, "sigmoid": _sigmoid}


def _proj_ln_kernel(*refs, n_in, gated):
    a_refs, w_refs = refs[:n_in], refs[n_in:2 * n_in]
    x_ref, g_ref, b_ref = refs[2 * n_in:2 * n_in + 3]
    extra, o_ref = refs[2 * n_in + 3:-1], refs[-1]
    y = None
    for a_ref, w_ref, gate in zip(a_refs, w_refs, gated):
        a = a_ref[...]
        if gate is not None:
            norm, act = _HEAD_NORMS[gate[0]], _GATE_ACTS[gate[1]]
            gain_ref, gate_ref, extra = extra[0], extra[1], extra[2:]
            a = jnp.concatenate([norm(_head(a, h), gain_ref[...]) for h in range(N_HEADS)], axis=1)
            a = a * act(gate_ref[...])
        ya = _bdot(a, w_ref[...])
        y = ya if y is None else y + ya
    o_ref[...] = _layernorm(DN_ALPHA * x_ref[...] + y, g_ref[...], b_ref[...])


def _proj_ln(acts, weight, x, g, b, tm, name, gates=None):
    M = x.shape[0]
    tm = min(tm, M)
    n_in = len(acts)
    gates = gates or [None] * n_in
    in_specs = [pl.BlockSpec((tm, a.shape[1]), lambda i: (i, 0)) for a in acts]
    w_args = []
    for r, a in enumerate(acts):
        w_arr, w_spec = _wspec(weight, (a.shape[1], D_MODEL), lambda i, r=r: (r, 0))
        w_args.append(w_arr)
        in_specs.append(w_spec)
    in_specs += [pl.BlockSpec((tm, D_MODEL), lambda i: (i, 0)),
                 pl.BlockSpec((1, D_MODEL), lambda i: (0, 0)),
                 pl.BlockSpec((1, D_MODEL), lambda i: (0, 0))]
    extra = []
    for gate in gates:
        if gate is not None:
            _, gain, _, gate_arr, blk = gate
            in_specs += [pl.BlockSpec((1, HEAD_W), lambda i: (0, 0)),
                         pl.BlockSpec((tm, MIX_W), lambda i, blk=blk: (i, blk))]
            extra += [gain.reshape(1, -1), gate_arr]
    return pl.pallas_call(
        functools.partial(_proj_ln_kernel, n_in=n_in,
                          gated=tuple(None if gt is None else (gt[0], gt[2]) for gt in gates)),
        out_shape=jax.ShapeDtypeStruct((M, D_MODEL), F32),
        grid=(M // tm,),
        in_specs=in_specs,
        out_specs=pl.BlockSpec((tm, D_MODEL), lambda i: (i, 0)),
        compiler_params=_cparams(1),
        name=name,
    )(*acts, *w_args, x, g.reshape(1, -1), b.reshape(1, -1), *extra)


Q_SCALE = DH_A ** -0.5 * math.log2(math.e)
VT_ROWS = HEAD_W + 16


def _ev_in_prompt_kernel(x_ref, w_ref, cos_ref, sin_ref, qt_ref, kb_ref, kt_ref, vt_ref, vr_ref, hb_ref):
    xb = x_ref[0].astype(BF16)
    cos, sin = cos_ref[...], sin_ref[...]
    proj = lambda lo, hi: jnp.dot(xb, w_ref[:, lo * MIX_W:hi * MIX_W], preferred_element_type=F32)
    hb_ref[0] = proj(3, 7)
    q = _rope(proj(0, 1), cos, sin, DH_A // 2)
    k = _rope(proj(1, 2), cos, sin, DH_A // 2)
    qt_ref[0] = (q * Q_SCALE).T.astype(BF16)
    kb_ref[0] = k.astype(BF16)
    kt_ref[0] = k.T
    v = proj(2, 3)
    tm = v.shape[0]
    vt = v.T.astype(BF16)
    ones = jnp.ones((VT_ROWS - HEAD_W, tm), BF16)
    for h in range(N_HEADS):
        vt_ref[0, h, :HEAD_W, :] = vt[h * HEAD_W:(h + 1) * HEAD_W]
        vt_ref[0, h, HEAD_W:, :] = ones
        vr_ref[0, pl.ds(h, tm, stride=N_HEADS), :] = _head(v, h)


def _ev_in_prompt(x3, w_in, cos_t, sin_t, tm):
    B, T, _ = x3.shape
    tm = min(tm, T)
    tab = pl.BlockSpec((tm, LANES), lambda b, i: (i, 0))
    tr = pl.BlockSpec((1, MIX_W, tm), lambda b, i: (b, 0, i))
    tshape = lambda dt: jax.ShapeDtypeStruct((B, MIX_W, T), dt)
    return pl.pallas_call(
        _ev_in_prompt_kernel,
        out_shape=(tshape(BF16), jax.ShapeDtypeStruct((B, T, MIX_W), BF16), tshape(F32),
                   jax.ShapeDtypeStruct((B, N_HEADS, VT_ROWS, T), BF16),
                   jax.ShapeDtypeStruct((B, T * N_HEADS, HEAD_W), F32),
                   jax.ShapeDtypeStruct((B, T, 4 * MIX_W), F32)),
        grid=(B, T // tm),
        in_specs=[pl.BlockSpec((1, tm, D_MODEL), lambda b, i: (b, i, 0)),
                  pl.BlockSpec(w_in.shape, lambda b, i: (0, 0)), tab, tab],
        out_specs=(tr, pl.BlockSpec((1, tm, MIX_W), lambda b, i: (b, i, 0)), tr,
                   pl.BlockSpec((1, N_HEADS, VT_ROWS, tm), lambda b, i: (b, 0, 0, i)),
                   pl.BlockSpec((1, tm * N_HEADS, HEAD_W), lambda b, i: (b, i, 0)),
                   pl.BlockSpec((1, tm, 4 * MIX_W), lambda b, i: (b, i, 0))),
        compiler_params=_cparams(2),
        name="ev_in_prompt",
    )(x3, w_in, cos_t, sin_t)


def _ev_prep_sample_kernel(qk_ref, cos_ref, sin_ref, q_ref, k_ref, kt_ref):
    cos, sin = cos_ref[...], sin_ref[...]
    k = _rope(qk_ref[:, MIX_W:], cos, sin, DH_A // 2)
    q_ref[...] = _rope(qk_ref[:, :MIX_W], cos, sin, DH_A // 2) * Q_SCALE
    k_ref[...] = k
    kt_ref[...] = k.T


def _ev_prep_sample(h, cos_t, sin_t):
    DB = h.shape[0]
    full = lambda shape: pl.BlockSpec(shape, lambda i: (0,) * len(shape))
    return pl.pallas_call(
        _ev_prep_sample_kernel,
        out_shape=(jax.ShapeDtypeStruct((DB, MIX_W), F32), jax.ShapeDtypeStruct((DB, MIX_W), F32),
                   jax.ShapeDtypeStruct((MIX_W, DB), F32)),
        grid=(1,),
        in_specs=[full((DB, 2 * MIX_W)), full((DB, LANES)), full((DB, LANES))],
        out_specs=(full((DB, MIX_W)), full((DB, MIX_W)), full((MIX_W, DB))),
        compiler_params=_cparams(1),
        name="ev_prep_sample",
    )(h, cos_t, sin_t)


def _lambda(lam_ref, lam_init):
    lp = lam_ref[...]
    s01 = jnp.sum(lp[0:1] * lp[1:2], axis=1, keepdims=True)
    s23 = jnp.sum(lp[2:3] * lp[3:4], axis=1, keepdims=True)
    return jnp.exp(s01) - jnp.exp(s23) + lam_init


def _dattn_kernel(lam_ref, g_ref, qt_ref, k_ref, vt_ref, o_ref, m_ref, acc_ref, *, tq, lam_init):
    i = pl.program_id(2)
    qt = qt_ref[0]
    sub = lax.broadcasted_iota(jnp.int32, qt.shape, 0)
    zero = jnp.zeros_like(qt)
    q_maps = [jnp.where(sub < DH_A, qt, zero), jnp.where(sub >= DH_A, qt, zero)]
    maps = range(2)
    m_ref[...] = jnp.full(m_ref.shape, NEG_INF, F32)
    acc_ref[...] = jnp.zeros(acc_ref.shape, F32)

    def steps(js):
        ks, vts, ss = [], [], []
        for j in js:
            start = pl.multiple_of(j * tq, tq)
            ks.append(k_ref[0, pl.ds(start, tq), :])
            vts.append(vt_ref[0, 0, :, pl.ds(start, tq)])
        for kj in ks:
            ss.append([jnp.dot(kj, q_maps[c], preferred_element_type=F32) for c in maps])
        for n in range(len(js)):
            for c in maps:
                s = ss[n][c]
                m_prev = m_ref[c]
                m_new = jnp.maximum(m_prev, jnp.max(s, axis=0, keepdims=True))
                alpha = jnp.exp2(m_prev - m_new)
                p = jnp.exp2(s - m_new)
                acc_ref[c] = alpha * acc_ref[c] + jnp.dot(vts[n], p.astype(BF16), preferred_element_type=F32)
                m_ref[c] = m_new

    def body(jj, carry):
        steps([2 * jj, 2 * jj + 1])
        return carry

    lax.fori_loop(0, lax.shift_right_logical(i, 1), body, 0)

    @pl.when((i & 1) == 1)
    def _():
        steps([i - 1])

    half = tq // 2
    start = pl.multiple_of(i * tq, tq)
    k_top, k_bot = k_ref[0, pl.ds(start, half), :], k_ref[0, pl.ds(start + half, half), :]
    vt_top, vt_bot = vt_ref[0, 0, :, pl.ds(start, half)], vt_ref[0, 0, :, pl.ds(start + half, half)]
    causal = lambda s: jnp.where(lax.broadcasted_iota(jnp.int32, s.shape, 0)
                                 <= lax.broadcasted_iota(jnp.int32, s.shape, 1), s, NEG_INF)
    s_top = [causal(jnp.dot(k_top, q_maps[c], preferred_element_type=F32)) for c in maps]
    s_bot = [causal(jnp.dot(k_bot, q_maps[c][:, half:], preferred_element_type=F32)) for c in maps]
    for c in maps:
        m_prev = m_ref[c]
        m_new = jnp.maximum(m_prev, jnp.max(s_top[c], axis=0, keepdims=True))
        m_new = jnp.concatenate([m_new[:, :half],
                                 jnp.maximum(m_new[:, half:], jnp.max(s_bot[c], axis=0, keepdims=True))], axis=1)
        p_top = jnp.exp2(s_top[c] - m_new).astype(BF16)
        p_bot = jnp.exp2(s_bot[c] - m_new[:, half:]).astype(BF16)
        acc_ref[c] = jnp.exp2(m_prev - m_new) * acc_ref[c] + jnp.dot(vt_top, p_top, preferred_element_type=F32)
        acc_ref[c, :, half:] = acc_ref[c, :, half:] + jnp.dot(vt_bot, p_bot, preferred_element_type=F32)
    lam = _lambda(lam_ref, lam_init)
    o = [acc_ref[c, :HEAD_W, :] / acc_ref[c, HEAD_W:HEAD_W + 1, :] for c in maps]
    d = (o[0] - lam * o[1]).T
    o_ref[0] = (_rmsnorm(d, g_ref[...]) * (1.0 - lam_init)).astype(BF16)


def _dattn_prompt(qt, kb, vt, lam_p, subln_g, lam_init, tq):
    B, T, _ = kb.shape
    tq = min(tq, T)
    return pl.pallas_call(
        functools.partial(_dattn_kernel, tq=tq, lam_init=lam_init),
        out_shape=jax.ShapeDtypeStruct((B, T, MIX_W), BF16),
        grid=(B, N_HEADS, T // tq),
        in_specs=[pl.BlockSpec((4, DH_A), lambda b, h, i: (0, 0)),
                  pl.BlockSpec((1, HEAD_W), lambda b, h, i: (0, 0)),
                  pl.BlockSpec((1, HEAD_W, tq), lambda b, h, i: (b, h, i)),
                  pl.BlockSpec((1, T, HEAD_W), lambda b, h, i: (b, 0, h)),
                  pl.BlockSpec((1, 1, VT_ROWS, T), lambda b, h, i: (b, h, 0, 0))],
        out_specs=pl.BlockSpec((1, tq, HEAD_W), lambda b, h, i: (b, i, h)),
        scratch_shapes=[pltpu.VMEM((2, 1, tq), F32), pltpu.VMEM((2, VT_ROWS, tq), F32)],
        compiler_params=_cparams(3),
        name="dattn_prompt",
    )(lam_p, subln_g.reshape(1, -1), qt, kb, vt)


def _head(x, h):
    return x[:, h * HEAD_W:(h + 1) * HEAD_W]


def _gla_chunk(seqs, st_ref, tril_mask):
    seqs = [tuple(x.astype(BF16) for x in s[:4]) + (s[4],) for s in seqs]
    chains = [(n, h) for n in range(len(seqs)) for h in range(N_HEADS)]
    part = lambda n, i, h: _head(seqs[n][i], h)
    sts = [st_ref[N_HEADS * n + h] for n, h in chains]
    inter = [_bdot_nt(part(n, 0, h), sts[c]) for c, (n, h) in enumerate(chains)]
    upd = [_bdot_tn(part(n, 3, h), part(n, 2, h)) for n, h in chains]
    attn = [jnp.where(tril_mask, _bdot_nt(part(n, 0, h), part(n, 1, h)), 0.0) for n, h in chains]
    intra = [_bdot(attn[c], part(n, 3, h)) for c, (n, h) in enumerate(chains)]
    for c, (n, h) in enumerate(chains):
        dec = seqs[n][4]
        st_ref[N_HEADS * n + h] = sts[c] * (dec[h] if isinstance(dec, (list, tuple)) else _head(dec, h)) + upd[c]
    outs = [intra[c] + inter[c] for c in range(len(chains))]
    return [outs[N_HEADS * n:N_HEADS * (n + 1)] for n in range(len(seqs))]


def _hgrn_kernel(q_ref, f_ref, i_ref, lb_ref, o_ref, s_ref, st_ref, *, tt, L, n_seq):
    t = pl.program_id(0)

    @pl.when(t == 0)
    def _():
        st_ref[...] = jnp.zeros(st_ref.shape, F32)

    tril_mask = _tri(L, True)
    tril = tril_mask.astype(BF16)

    def body(c, carry):
        rows = pl.ds(pl.multiple_of(c * L, L), L)
        lb = lb_ref[...]
        seqs = []
        for n in range(n_seq):
            f = lb + (1.0 - lb) * _sigmoid(f_ref[n, rows, :])
            b = _cumsum_rows(tril, jnp.log(f))
            b_end = b[L - 1:L, :]
            k = 1.0 - f
            q_in = _silu(q_ref[n, rows, :]) * jnp.exp(b)
            seqs.append((q_in, k * jnp.exp(-b), k * jnp.exp(b_end - b), i_ref[n, rows, :], jnp.exp(b_end)))
        for n, o in enumerate(_gla_chunk(seqs, st_ref, tril_mask)):
            o_ref[n, rows, :] = jnp.concatenate(o, axis=1)
        return carry

    lax.fori_loop(0, tt // L, body, 0, unroll=2)

    @pl.when(t == pl.num_programs(0) - 1)
    def _():
        s_ref[...] = st_ref[...]


def _hgrn_prompt(h3, lb, tt):
    B, T, _ = h3.shape
    tt = min(tt, T)
    L = math.gcd(T, CHUNK)
    col = lambda c: pl.BlockSpec((B, tt, MIX_W), lambda t: (0, t, c))
    o, st = pl.pallas_call(
        functools.partial(_hgrn_kernel, tt=tt, L=L, n_seq=B),
        out_shape=(jax.ShapeDtypeStruct((B, T, MIX_W), F32),
                   jax.ShapeDtypeStruct((B * N_HEADS, HEAD_W, HEAD_W), F32)),
        grid=(T // tt,),
        in_specs=[col(0), col(1), col(2), pl.BlockSpec((1, MIX_W), lambda t: (0, 0))],
        out_specs=(pl.BlockSpec((B, tt, MIX_W), lambda t: (0, t, 0)),
                   pl.BlockSpec((B * N_HEADS, HEAD_W, HEAD_W), lambda t: (0, 0, 0))),
        scratch_shapes=[pltpu.VMEM((B * N_HEADS, HEAD_W, HEAD_W), F32)],
        compiler_params=_cparams(1),
        name="hgrn_prompt",
    )(h3, h3, h3, lb.reshape(1, -1))
    return o, st.reshape(B, N_HEADS, HEAD_W, HEAD_W)


def _log_gamma(h):
    return float(np.log(1.0 - 2.0 ** (-5.0 - h)))


def _od_kernel(cq_ref, ck_ref, cv_ref, dq_ref, dk_ref, dv_ref, gc_ref, gr_ref,
               bc_ref, br_ref,
               oc_ref, od_ref, sc_ref, dc_ref, dn_ref, dm_ref,
               st_ref, ct_ref, n_ref, m_ref, *, tt, L, n_seq):
    t = pl.program_id(0)

    @pl.when(t == 0)
    def _():
        st_ref[...] = jnp.zeros(st_ref.shape, F32)
        ct_ref[...] = jnp.zeros(ct_ref.shape, F32)
        n_ref[...] = jnp.zeros(n_ref.shape, F32)
        m_ref[...] = jnp.zeros(m_ref.shape, F32)

    tril_mask = _tri(L, True)
    tril = tril_mask.astype(BF16)
    triu = _tri(L, False).astype(BF16)
    scale = HEAD_W ** -0.5
    heads = range(N_HEADS)
    pos1 = (lax.broadcasted_iota(jnp.int32, (L, HEAD_W), 0) + 1).astype(F32)
    ret_b = jnp.concatenate([pos1 * _log_gamma(h) for h in heads], axis=1)
    ret_b_end = jnp.concatenate([jnp.full((L, HEAD_W), L * _log_gamma(h), F32) for h in heads], axis=1)
    ret_q_dec, ret_k_dec, ret_k_end = jnp.exp(ret_b), jnp.exp(-ret_b), jnp.exp(ret_b_end - ret_b)
    ret_dec = [math.exp(L * _log_gamma(h)) for h in heads]

    seqs = range(n_seq)
    chains = [(n, h) for n in seqs for h in heads]

    def body(c, carry):
        rows = pl.ds(pl.multiple_of(c * L, L), L)
        ret = []
        for n in seqs:
            q, k = cq_ref[n, rows, :], ck_ref[n, rows, :]
            ret.append((q * ret_q_dec, k * ret_k_dec, k * ret_k_end, cv_ref[n, rows, :], ret_dec))
        for n, o in enumerate(_gla_chunk(ret, st_ref, tril_mask)):
            oc_ref[n, rows, :] = jnp.concatenate(o, axis=1)
        g_c = [gc_ref[n, rows, :] + bc_ref[...] for n in seqs]
        g_r = [gr_ref[n, c] + br_ref[...] for n in seqs]
        lf_c = [_log_sigmoid(g) for g in g_c]
        b_c = [_cumsum_rows(tril, jnp.concatenate(
            [jnp.broadcast_to(lf_c[n][:, N_HEADS + h:N_HEADS + h + 1], (L, HEAD_W)) for h in heads], axis=1))
            for n in seqs]
        b_r = [_cumsum_lanes(_log_sigmoid(g), triu) for g in g_r]
        dq = [dq_ref[n, rows, :] * scale for n in seqs]
        dk = [dk_ref[n, rows, :] for n in seqs]
        dqb = [x.astype(BF16) for x in dq]
        dkb = [x.astype(BF16) for x in dk]
        dvb = [dv_ref[n, rows, :].astype(BF16) for n in seqs]
        cts = [ct_ref[i] for i in range(len(chains))]
        q_c = [_bdot_nt(_head(dqb[n], h), cts[i]) for i, (n, h) in enumerate(chains)]
        q_k = [_bdot_nt(_head(dqb[n], h), _head(dkb[n], h)) for n, h in chains]
        m_prev = [m_ref[i:i + 1, :] for i in range(len(chains))]
        bcw, m_t, w_mat = [], [], []
        for i, (n, h) in enumerate(chains):
            bcw.append(_head(b_c[n], h))
            bc = bcw[i][:, :L]
            dm = jnp.where(tril_mask, bc - b_r[n][N_HEADS + h:N_HEADS + h + 1, :] + g_r[n][h:h + 1, :], NEG_INF)
            m_t.append(jnp.maximum(bcw[i] + m_prev[i], jnp.max(dm, axis=1, keepdims=True)))
            w_mat.append(jnp.exp(dm - m_t[i][:, :L]) * q_k[i])
        w_v = [_bdot(w_mat[i], _head(dvb[n], h)) for i, (n, h) in enumerate(chains)]
        kws, c_scales = [], []
        for i, (n, h) in enumerate(chains):
            m_new = m_t[i][L - 1:L, :]
            b_last = bcw[i][L - 1:L, :]
            c_scales.append(jnp.exp(b_last + m_prev[i] - m_new))
            kws.append(_head(dk[n], h) * jnp.exp(b_last - bcw[i] + g_c[n][:, h:h + 1] - m_new))
            m_ref[i:i + 1, :] = m_new
        c_upd = [_bdot_tn(_head(dvb[n], h), kws[i]) for i, (n, h) in enumerate(chains)]
        hs = []
        for i, (n, h) in enumerate(chains):
            n_row = n_ref[i:i + 1, :]
            inter = jnp.exp(bcw[i] + m_prev[i] - m_t[i])
            num = inter * q_c[i] + w_v[i]
            den = (inter[:, 0:1] * jnp.sum(_head(dq[n], h) * n_row, axis=1, keepdims=True)
                   + jnp.sum(w_mat[i], axis=1, keepdims=True))
            hs.append(num / jnp.maximum(jnp.abs(den), jnp.exp(-m_t[i])))
            ct_ref[i] = c_scales[i] * cts[i] + c_upd[i]
            n_ref[i:i + 1, :] = c_scales[i] * n_row + jnp.sum(kws[i], axis=0, keepdims=True)
        for n in seqs:
            od_ref[n, rows, :] = jnp.concatenate(hs[N_HEADS * n:N_HEADS * (n + 1)], axis=1)
        return carry

    lax.fori_loop(0, tt // L, body, 0)

    @pl.when(t == pl.num_programs(0) - 1)
    def _():
        sc_ref[...] = st_ref[...]
        dc_ref[...] = ct_ref[...]
        dn_ref[...] = n_ref[...]
        dm_ref[...] = m_ref[...]


def _od_in_kernel(x_ref, w_ref, wg_ref, cos_ref, sin_ref, h_ref, g_ref):
    xb = x_ref[...].astype(BF16)
    cos, sin = cos_ref[...], sin_ref[...]
    proj = lambda lo, hi: jnp.dot(xb, w_ref[:, lo * MIX_W:hi * MIX_W], preferred_element_type=F32)
    h_ref[:, 2 * MIX_W:] = proj(2, 8)
    h_ref[:, :MIX_W] = _rope(proj(0, 1), cos, sin, HEAD_W // 2)
    h_ref[:, MIX_W:2 * MIX_W] = _rope(proj(1, 2), cos, sin, HEAD_W // 2) * (HEAD_W ** -0.5)
    g_ref[...] = jnp.dot(xb, wg_ref[...], preferred_element_type=F32)


def _od_in_prompt(x, w_in, w_gate, cos_t, sin_t, tm):
    M, K = x.shape
    N = w_in.shape[1]
    tm = min(tm, cos_t.shape[0])
    n_tab = cos_t.shape[0] // tm
    tab = pl.BlockSpec((tm, LANES), lambda i: (i % n_tab, 0))
    return pl.pallas_call(
        _od_in_kernel,
        out_shape=(jax.ShapeDtypeStruct((M, N), F32), jax.ShapeDtypeStruct((M, LANES), F32)),
        grid=(M // tm,),
        in_specs=[pl.BlockSpec((tm, K), lambda i: (i, 0)),
                  pl.BlockSpec((K, N), lambda i: (0, 0)),
                  pl.BlockSpec((K, LANES), lambda i: (0, 0)), tab, tab],
        out_specs=(pl.BlockSpec((tm, N), lambda i: (i, 0)), pl.BlockSpec((tm, LANES), lambda i: (i, 0))),
        compiler_params=_cparams(1),
        name="od_in_prompt",
    )(x, w_in, w_gate, cos_t, sin_t)


def _od_prompt(h3, gates, b_if, tt):
    B, T, _ = h3.shape
    tt = min(tt, T)
    L = math.gcd(T, OD_CHUNK)
    gates_r = gates[:, :, :2 * N_HEADS].reshape(B, T // L, L, 2 * N_HEADS).transpose(0, 1, 3, 2)
    b_pad = jnp.zeros((1, LANES), F32).at[0, :2 * N_HEADS].set(b_if)
    n_chain = B * N_HEADS
    n_row = -(-n_chain // SUBLANES) * SUBLANES
    col = lambda c: pl.BlockSpec((B, tt, MIX_W), lambda t: (0, t, c))
    mat_state = pl.BlockSpec((n_chain, HEAD_W, HEAD_W), lambda t: (0, 0, 0))
    row_state = pl.BlockSpec((n_row, LANES), lambda t: (0, 0))
    out_blk = pl.BlockSpec((B, tt, MIX_W), lambda t: (0, t, 0))
    mat_shape = jax.ShapeDtypeStruct((n_chain, HEAD_W, HEAD_W), F32)
    row_shape = jax.ShapeDtypeStruct((n_row, LANES), F32)
    o_c, o_d, sct, dct, dn, dm = pl.pallas_call(
        functools.partial(_od_kernel, tt=tt, L=L, n_seq=B),
        out_shape=(jax.ShapeDtypeStruct((B, T, MIX_W), F32), jax.ShapeDtypeStruct((B, T, MIX_W), F32),
                   mat_shape, mat_shape, row_shape, row_shape),
        grid=(T // tt,),
        in_specs=[col(0), col(1), col(2), col(4), col(5), col(6),
                  pl.BlockSpec((B, tt, LANES), lambda t: (0, t, 0)),
                  pl.BlockSpec((B, tt // L, 2 * N_HEADS, L), lambda t: (0, t, 0, 0)),
                  pl.BlockSpec((1, LANES), lambda t: (0, 0)),
                  pl.BlockSpec((2 * N_HEADS, 1), lambda t: (0, 0))],
        out_specs=(out_blk, out_blk, mat_state, mat_state, row_state, row_state),
        scratch_shapes=[pltpu.VMEM(mat_shape.shape, F32), pltpu.VMEM(mat_shape.shape, F32),
                        pltpu.VMEM(row_shape.shape, F32), pltpu.VMEM(row_shape.shape, F32)],
        compiler_params=_cparams(1),
        name="od_prompt",
    )(h3, h3, h3, h3, h3, h3, gates, gates_r, b_pad, b_if.reshape(-1, 1))
    per_seq = lambda s: s.reshape((B, N_HEADS) + s.shape[1:])
    return (o_c, o_d, per_seq(sct), per_seq(dct), per_seq(dn[:n_chain]), per_seq(dm[:n_chain, 0]))


def _xattn_kernel(x_ref, wq_ref, mk_ref, mv_ref, wo_ref, g_ref, b_ref, o_ref):
    x = x_ref[...]
    q = _bdot(x, wq_ref[...])
    qb = (q * (DH_X ** -0.5)).astype(BF16)
    heads = range(N_HEADS)
    cols = [slice(h * DH_X, (h + 1) * DH_X) for h in heads]
    s = [lax.dot_general(qb[:, cols[h]], mk_ref[0, :, cols[h]], NT_DIMS, preferred_element_type=F32) for h in heads]
    p = [jnp.exp(s[h] - jnp.max(s[h], axis=1, keepdims=True)) for h in heads]
    l = [jnp.sum(p[h], axis=1, keepdims=True) for h in heads]
    pv = [jnp.dot(p[h].astype(BF16), mv_ref[0, :, cols[h]], preferred_element_type=F32) for h in heads]
    outs = [(pv[h] / l[h]).astype(BF16) for h in heads]
    y = jnp.dot(jnp.concatenate(outs, axis=1), wo_ref[...], preferred_element_type=F32)
    o_ref[...] = _layernorm(DN_ALPHA * x + y, g_ref[...], b_ref[...])


def _xattn_prompt(x, wq, mk, mv, wo, g, b, T, tm):
    M = x.shape[0]
    tm = min(tm, T)
    n_mem = mk.shape[1]
    per_b = T // tm
    full = lambda shape: pl.BlockSpec(shape, lambda i: (0,) * len(shape))
    mem = pl.BlockSpec((1, n_mem, D_MODEL), lambda i: (i // per_b, 0, 0))
    wq, wq_spec = _wspec(wq, (D_MODEL, D_MODEL), lambda i: (0, 0))
    wo, wo_spec = _wspec(wo, (D_MODEL, D_MODEL), lambda i: (0, 0))
    return pl.pallas_call(
        _xattn_kernel,
        out_shape=jax.ShapeDtypeStruct((M, D_MODEL), F32),
        grid=(M // tm,),
        in_specs=[pl.BlockSpec((tm, D_MODEL), lambda i: (i, 0)), wq_spec, mem, mem,
                  wo_spec, full((1, D_MODEL)), full((1, D_MODEL))],
        out_specs=pl.BlockSpec((tm, D_MODEL), lambda i: (i, 0)),
        compiler_params=_cparams(1),
        name="xattn_prompt",
    )(x, wq, mk, mv, wo, g.reshape(1, -1), b.reshape(1, -1))


def _ffn_kernel(x_ref, wu_ref, wg_ref, cw_ref, cb_ref, wd_ref, g_ref, b_ref, o_ref, tail_ref,
                xb_ref, acc_ref, stage_ref, carry_ref, *, tm, per_b, single):
    i, j = pl.program_id(0), pl.program_id(1)

    @pl.when(j == 0)
    def _():
        xb_ref[...] = x_ref[...].astype(BF16)

    @pl.when(i % per_b == 0)
    def _():
        carry_ref[j] = jnp.zeros(carry_ref.shape[1:], F32)

    xb = xb_ref[...]
    stage_ref[0:SUBLANES, :] = carry_ref[j]
    stage_ref[SUBLANES:, :] = jnp.dot(xb, wu_ref[...], preferred_element_type=F32)
    last = stage_ref[tm:tm + SUBLANES, :]
    carry_ref[j] = last
    tail_ref[0] = last
    conv = (cb_ref[...] + cw_ref[0:1, :] * stage_ref[SUBLANES - 2:SUBLANES - 2 + tm, :]
            + cw_ref[1:2, :] * stage_ref[SUBLANES - 1:SUBLANES - 1 + tm, :]
            + cw_ref[2:3, :] * stage_ref[SUBLANES:, :])
    gate = jnp.dot(xb, wg_ref[...], preferred_element_type=F32)
    y = _bdot(jax.nn.gelu(conv) * gate, wd_ref[...])
    if single:
        o_ref[...] = _layernorm(DN_ALPHA * x_ref[...] + y, g_ref[...], b_ref[...])
        return

    @pl.when(j == 0)
    def _():
        acc_ref[...] = y

    @pl.when(j > 0)
    def _():
        acc_ref[...] = acc_ref[...] + y

    @pl.when(j == pl.num_programs(1) - 1)
    def _():
        o_ref[...] = _layernorm(DN_ALPHA * x_ref[...] + acc_ref[...], g_ref[...], b_ref[...])


def _ffn_prompt(x, w_up, conv_w, conv_b, w_down, g, b, T, tm, tf):
    M = x.shape[0]
    d_ff = _wshape(w_down)[0]
    tm = min(tm, T)
    nf = d_ff // tf
    per_b = T // tm
    wmode = dict(pipeline_mode=pl.Buffered(1)) if nf == 1 else {}
    w_up_arr, up_spec = _wspec(w_up, (D_MODEL, tf), lambda i, j: (0, j), **wmode)
    _, gate_spec = _wspec(w_up, (D_MODEL, tf), lambda i, j: (0, nf + j), **wmode)
    w_down_arr, down_spec = _wspec(w_down, (tf, D_MODEL), lambda i, j: (j, 0), **wmode)
    return pl.pallas_call(
        functools.partial(_ffn_kernel, tm=tm, per_b=per_b, single=nf == 1),
        out_shape=(jax.ShapeDtypeStruct((M, D_MODEL), F32),
                   jax.ShapeDtypeStruct((M // tm, SUBLANES, d_ff), F32)),
        grid=(M // tm, nf),
        in_specs=[pl.BlockSpec((tm, D_MODEL), lambda i, j: (i, 0)),
                  up_spec, gate_spec,
                  pl.BlockSpec((3, tf), lambda i, j: (0, j)),
                  pl.BlockSpec((1, tf), lambda i, j: (0, j)),
                  down_spec,
                  pl.BlockSpec((1, D_MODEL), lambda i, j: (0, 0)),
                  pl.BlockSpec((1, D_MODEL), lambda i, j: (0, 0))],
        out_specs=(pl.BlockSpec((tm, D_MODEL), lambda i, j: (i, 0)),
                   pl.BlockSpec((1, SUBLANES, tf), lambda i, j: (i, 0, j))),
        scratch_shapes=[pltpu.VMEM((tm, D_MODEL), BF16), pltpu.VMEM((tm, D_MODEL), F32),
                        pltpu.VMEM((tm + SUBLANES, tf), F32), pltpu.VMEM((nf, SUBLANES, tf), F32)],
        compiler_params=_cparams(2),
        name="ffn_prompt",
    )(x, w_up_arr, w_up_arr, conv_w, conv_b.reshape(1, -1), w_down_arr, g.reshape(1, -1), b.reshape(1, -1))


def _dattn_sample_kernel(*refs, n_pages, page, lam_init, rows):
    lam_ref, g_ref, q_ref, kn_ref, vn_ref = refs[1:6]
    o_ref = refs[6 + 2 * rows * n_pages]
    for r in range(rows):
        k_refs = refs[6 + r * n_pages:6 + (r + 1) * n_pages]
        v_refs = refs[6 + (rows + r) * n_pages:6 + (rows + r + 1) * n_pages]
        _dattn_sample_row(lam_ref, g_ref, q_ref.at[r], kn_ref.at[r], vn_ref.at[r], k_refs, v_refs, o_ref.at[r],
                          n_pages=n_pages, page=page, lam_init=lam_init)


def _dattn_sample_row(lam_ref, g_ref, q_ref, kn_ref, vn_ref, k_refs, v_refs, o_ref, *, n_pages, page, lam_init):
    n_rows = 2 * N_HEADS
    q = q_ref[...]
    lane = lax.broadcasted_iota(jnp.int32, (n_rows, MIX_W), 1)
    row = lax.broadcasted_iota(jnp.int32, (n_rows, MIX_W), 0)
    qbd = jnp.where(lax.shift_right_logical(lane, 6) == row, jnp.broadcast_to(q, (n_rows, MIX_W)), 0.0)
    s_new = jnp.sum(qbd * kn_ref[...], axis=1, keepdims=True)
    scores = [_bdot(qbd, k_refs[p][0]) for p in range(n_pages)]
    m = s_new
    for s in scores:
        m = jnp.maximum(m, jnp.max(s, axis=1, keepdims=True))
    p_new = jnp.exp2(s_new - m)
    l = p_new
    vn = vn_ref[...]
    accs = [p_new * vn[:, h * HEAD_W:(h + 1) * HEAD_W] for h in range(N_HEADS)]
    for p in range(n_pages):
        pr = jnp.exp2(scores[p] - m)
        l = l + jnp.sum(pr, axis=1, keepdims=True)
        for h in range(N_HEADS):
            accs[h] = accs[h] + _bdot(pr, v_refs[p][0, pl.ds(h, page, stride=N_HEADS), :])
    lam = _lambda(lam_ref, lam_init)
    outs = []
    for h in range(N_HEADS):
        o = accs[h] / l
        d = o[2 * h:2 * h + 1, :] - lam * o[2 * h + 1:2 * h + 2, :]
        outs.append(_rmsnorm(d, g_ref[...]) * (1.0 - lam_init))
    o_ref[...] = jnp.concatenate(outs, axis=1)


def _dattn_sample(q, k_new, v_new, kt_pool, v_pool, pool_base, page_table, lam_p, subln_g, lam_init):
    DB = q.shape[0]
    n_pages = page_table.shape[1]
    page = kt_pool.shape[2]
    rows = math.gcd(DB, 2)
    tok = pl.BlockSpec((rows, 1, MIX_W), lambda b, pt: (b, 0, 0))

    def k_spec(r, p):
        return pl.BlockSpec((1, MIX_W, page), lambda b, pt: (pool_base + pt[rows * b + r, p], 0, 0))

    def v_spec(r, p):
        return pl.BlockSpec((1, page * N_HEADS, HEAD_W), lambda b, pt: (pool_base + pt[rows * b + r, p], 0, 0))

    pages = [(r, p) for r in range(rows) for p in range(n_pages)]
    grid_spec = pltpu.PrefetchScalarGridSpec(
        num_scalar_prefetch=1,
        grid=(DB // rows,),
        in_specs=[pl.BlockSpec((4, DH_A), lambda b, pt: (0, 0)),
                  pl.BlockSpec((1, HEAD_W), lambda b, pt: (0, 0)), tok, tok, tok]
                 + [k_spec(r, p) for r, p in pages] + [v_spec(r, p) for r, p in pages],
        out_specs=tok,
    )
    out = pl.pallas_call(
        functools.partial(_dattn_sample_kernel, n_pages=n_pages, page=page, lam_init=lam_init, rows=rows),
        out_shape=jax.ShapeDtypeStruct((DB, 1, MIX_W), F32),
        grid_spec=grid_spec,
        compiler_params=_cparams(1),
        name="dattn_sample",
    )(page_table, lam_p, subln_g.reshape(1, -1), q.reshape(DB, 1, MIX_W), k_new.reshape(DB, 1, MIX_W),
      v_new.reshape(DB, 1, MIX_W), *([kt_pool] * len(pages)), *([v_pool] * len(pages)))
    return out.reshape(DB, MIX_W)


def _hgrn_step_kernel(h_ref, s0_ref, lb_ref, ng_ref, o_ref, s_ref, *, bt):
    def body(i, carry):
        row = h_ref[i]
        heads = range(N_HEADS)
        blk = lambda n: row[:, n * MIX_W:(n + 1) * MIX_W]
        lb = lb_ref[...]
        f = lb + (1.0 - lb) * _sigmoid(blk(4))
        q, v = _silu(blk(3)), blk(5)
        f_col = [_col(_head(f, h)) for h in heads]
        s_new = [f_col[h] * s0_ref[i, h] + (1.0 - f_col[h]) * _head(v, h) for h in heads]
        for h in heads:
            s_ref[i, h] = s_new[h]
        o = [_bdot(jnp.broadcast_to(_head(q, h), (SUBLANES, HEAD_W)), s_new[h])[0:1, :] for h in heads]
        o = jnp.concatenate([_rmsnorm(o[h], ng_ref[...]) for h in heads], axis=1)
        o_ref[i] = o * _silu(blk(6))
        return carry

    lax.fori_loop(0, bt, body, 0, unroll=2)


def _hgrn_step(h, s0, lb, norm_g, bt):
    DB, W = h.shape
    bt = min(bt, DB)
    state = pl.BlockSpec((bt, N_HEADS, HEAD_W, HEAD_W), lambda i: (i, 0, 0, 0))
    o, s = pl.pallas_call(
        functools.partial(_hgrn_step_kernel, bt=bt),
        out_shape=(jax.ShapeDtypeStruct((DB, 1, MIX_W), F32), jax.ShapeDtypeStruct(s0.shape, F32)),
        grid=(DB // bt,),
        in_specs=[pl.BlockSpec((bt, 1, W), lambda i: (i, 0, 0)), state,
                  pl.BlockSpec((1, MIX_W), lambda i: (0, 0)), pl.BlockSpec((1, HEAD_W), lambda i: (0, 0))],
        out_specs=(pl.BlockSpec((bt, 1, MIX_W), lambda i: (i, 0, 0)), state),
        compiler_params=_cparams(1),
        name="hgrn_step",
    )(h.reshape(DB, 1, W), s0, lb.reshape(1, -1), norm_g.reshape(1, -1))
    return o.reshape(DB, MIX_W), s


def _od_step_kernel(h_ref, gt_ref, bif_ref, cos_ref, sin_ref, sc0_ref, dc0_ref, dn0_ref, dm0_ref, cng_ref, dng_ref,
                    oc_ref, od_ref, sc_ref, dc_ref, dn_ref, dm_ref, *, bt):
    scale = HEAD_W ** -0.5

    def body(i, carry):
        row = h_ref[i]
        cos, sin = cos_ref[...], sin_ref[...]
        gates = gt_ref[i] + bif_ref[...]
        m_row = dm0_ref[i]
        heads = range(N_HEADS)
        blk = lambda n: row[:, n * MIX_W:(n + 1) * MIX_W]
        q = _rope(blk(0), cos, sin, HEAD_W // 2)
        k = _rope(blk(1), cos, sin, HEAD_W // 2) * scale
        cv, dk, dv, dq = blk(2), blk(5), blk(6), blk(4) * scale
        lf = _log_sigmoid(gates)
        m_new = [jnp.maximum(lf[:, N_HEADS + h:N_HEADS + h + 1] + m_row[:, h:h + 1], gates[:, h:h + 1])
                 for h in heads]
        c_scale = [jnp.exp(lf[:, N_HEADS + h:N_HEADS + h + 1] + m_row[:, h:h + 1] - m_new[h]) for h in heads]
        kw = [_head(dk, h) * jnp.exp(gates[:, h:h + 1] - m_new[h]) for h in heads]
        k_col = [_col(_head(k, h)) for h in heads]
        kw_col = [_col(kw[h]) for h in heads]
        s_new = [math.exp(_log_gamma(h)) * sc0_ref[i, h] + k_col[h] * _head(cv, h) for h in heads]
        c_new = [c_scale[h] * dc0_ref[i, h] + kw_col[h] * _head(dv, h) for h in heads]
        n_new = [c_scale[h] * dn0_ref[i, h:h + 1, :] + kw[h] for h in heads]
        for h in heads:
            sc_ref[i, h] = s_new[h]
            dc_ref[i, h] = c_new[h]
            dn_ref[i, h:h + 1, :] = n_new[h]
        rows8 = lambda x: jnp.broadcast_to(x, (SUBLANES, HEAD_W))
        o = [_bdot(rows8(_head(q, h)), s_new[h])[0:1, :] for h in heads]
        num = [_bdot(rows8(_head(dq, h)), c_new[h])[0:1, :] for h in heads]
        den = [jnp.sum(_head(dq, h) * n_new[h], axis=1, keepdims=True) for h in heads]
        hh = [num[h] / jnp.maximum(jnp.abs(den[h]), jnp.exp(-m_new[h])) for h in heads]
        oc = jnp.concatenate([_groupnorm(o[h], cng_ref[...]) for h in heads], axis=1)
        od = jnp.concatenate([_groupnorm(hh[h], dng_ref[...]) for h in heads], axis=1)
        oc_ref[i] = oc * _silu(blk(3))
        od_ref[i] = od * _sigmoid(blk(7))
        dm_ref[i] = jnp.concatenate(m_new, axis=1)
        return carry

    lax.fori_loop(0, bt, body, 0, unroll=2)


def _od_step(h, gates, b_if, cos_t, sin_t, sc0, dc0, dn0, dm0, c_norm_g, d_norm_g, bt):
    DB, W = h.shape
    bt = min(bt, DB)
    mat = pl.BlockSpec((bt, N_HEADS, HEAD_W, HEAD_W), lambda i: (i, 0, 0, 0))
    nblk = pl.BlockSpec((bt, N_HEADS, HEAD_W), lambda i: (i, 0, 0))
    mblk = pl.BlockSpec((bt, 1, N_HEADS), lambda i: (i, 0, 0))
    vec = pl.BlockSpec((1, LANES), lambda i: (0, 0))
    out = pl.BlockSpec((bt, 1, MIX_W), lambda i: (i, 0, 0))
    b_pad = jnp.zeros((1, LANES), F32).at[0, :2 * N_HEADS].set(b_if)
    oc, od, sc, dc, dn, dm = pl.pallas_call(
        functools.partial(_od_step_kernel, bt=bt),
        out_shape=(jax.ShapeDtypeStruct((DB, 1, MIX_W), F32), jax.ShapeDtypeStruct((DB, 1, MIX_W), F32),
                   jax.ShapeDtypeStruct(sc0.shape, F32), jax.ShapeDtypeStruct(dc0.shape, F32),
                   jax.ShapeDtypeStruct(dn0.shape, F32), jax.ShapeDtypeStruct((DB, 1, N_HEADS), F32)),
        grid=(DB // bt,),
        in_specs=[pl.BlockSpec((bt, 1, W), lambda i: (i, 0, 0)),
                  pl.BlockSpec((bt, 1, LANES), lambda i: (i, 0, 0)), vec, vec, vec,
                  mat, mat, nblk, mblk, vec, vec],
        out_specs=(out, out, mat, mat, nblk, mblk),
        compiler_params=_cparams(1),
        name="od_step",
    )(h.reshape(DB, 1, W), gates.reshape(DB, 1, LANES), b_pad, cos_t, sin_t, sc0, dc0, dn0,
      dm0.reshape(DB, 1, N_HEADS), c_norm_g.reshape(1, -1), d_norm_g.reshape(1, -1))
    return oc.reshape(DB, MIX_W), od.reshape(DB, MIX_W), sc, dc, dn, dm.reshape(DB, N_HEADS)


X_HALVES = DH_X // LANES
X_ROWS = N_HEADS * X_HALVES


def _xattn_sample_kernel(q_ref, mk_ref, mv_ref, o_ref, *, xb):
    rows = range(xb)
    n_cols = mk_ref.shape[1]
    row = lax.broadcasted_iota(jnp.int32, (X_ROWS, n_cols), 0)
    col = lax.broadcasted_iota(jnp.int32, (X_ROWS, n_cols), 1)
    own = (col & (X_ROWS - 1)) == row
    own4 = own[:N_HEADS]
    q = [q_ref[t] * (DH_X ** -0.5) for t in rows]
    q2 = [jnp.concatenate([q[t][:, h * DH_X + c * LANES:h * DH_X + (c + 1) * LANES]
                           for c in range(X_HALVES) for h in range(N_HEADS)], axis=0) for t in rows]
    s2 = [jnp.where(own, _bdot_nt(q2[t], mk_ref[t]), 0.0) for t in rows]
    s4 = [jnp.where(own4, s2[t][:N_HEADS] + pltpu.roll(s2[t], n_cols - N_HEADS, 1)[N_HEADS:], NEG_INF) for t in rows]
    p4 = [jnp.exp(s4[t] - jnp.max(s4[t], axis=1, keepdims=True)) for t in rows]
    p4 = [p4[t] / jnp.sum(p4[t], axis=1, keepdims=True) for t in rows]
    p2 = [jnp.concatenate([p4[t], pltpu.roll(p4[t], N_HEADS, 1)], axis=0) for t in rows]
    o2 = [_bdot(p2[t], mv_ref[t]) for t in rows]
    for t in rows:
        o_ref[t] = jnp.concatenate([o2[t][c * N_HEADS + h:c * N_HEADS + h + 1, :]
                                    for h in range(N_HEADS) for c in range(X_HALVES)], axis=1)


def _xattn_sample(q, mk, mv, base):
    DB = q.shape[0]
    n_rows = mk.shape[1]
    xb = math.gcd(DB, 4)
    tok = pl.BlockSpec((xb, 1, D_MODEL), lambda b: (b, 0, 0))
    mem = pl.BlockSpec((xb, n_rows, LANES), lambda b: (base // xb + b, 0, 0))
    assert base % xb == 0
    out = pl.pallas_call(
        functools.partial(_xattn_sample_kernel, xb=xb),
        out_shape=jax.ShapeDtypeStruct((DB, 1, D_MODEL), F32),
        grid=(DB // xb,),
        in_specs=[tok, mem, mem],
        out_specs=tok,
        compiler_params=_cparams(1),
        name="xattn_sample",
    )(q.reshape(DB, 1, D_MODEL), mk, mv)
    return out.reshape(DB, D_MODEL)


def _mem_rows(cache):
    n_l, DB, n_mem = cache.shape[:3]
    c = cache.reshape(n_l, DB, n_mem, N_HEADS, X_HALVES, LANES).transpose(0, 1, 2, 4, 3, 5)
    return c.reshape(n_l * DB, n_mem * X_ROWS, LANES)


def _ffn_sample_kernel(x_ref, up_ref, gate_ref, b0_ref, b1_ref, cw_ref, cb_ref, wd_ref, g_ref, b_ref, o_ref, acc_ref):
    j = pl.program_id(0)
    conv = (cb_ref[...] + cw_ref[0:1, :] * b0_ref[...] + cw_ref[1:2, :] * b1_ref[...]
            + cw_ref[2:3, :] * up_ref[...])
    y = _bdot(jax.nn.gelu(conv) * gate_ref[...], wd_ref[...])

    @pl.when(j == 0)
    def _():
        acc_ref[...] = y

    @pl.when(j > 0)
    def _():
        acc_ref[...] = acc_ref[...] + y

    @pl.when(j == pl.num_programs(0) - 1)
    def _():
        o_ref[...] = _layernorm(DN_ALPHA * x_ref[...] + acc_ref[...], g_ref[...], b_ref[...])


def _ffn_sample(x, ug, buf0, buf1, conv_w, conv_b, w_down, g, b, tf):
    DB = x.shape[0]
    d_ff = _wshape(w_down)[0]
    nf = d_ff // tf
    ff = lambda off: pl.BlockSpec((DB, tf), lambda j: (0, off + j))
    full = lambda shape: pl.BlockSpec(shape, lambda j: (0,) * len(shape))
    w_down, down_spec = _wspec(w_down, (tf, D_MODEL), lambda j: (j, 0))
    return pl.pallas_call(
        _ffn_sample_kernel,
        out_shape=jax.ShapeDtypeStruct((DB, D_MODEL), F32),
        grid=(nf,),
        in_specs=[full((DB, D_MODEL)), ff(0), ff(nf), ff(0), ff(0),
                  pl.BlockSpec((3, tf), lambda j: (0, j)), pl.BlockSpec((1, tf), lambda j: (0, j)),
                  down_spec, full((1, D_MODEL)), full((1, D_MODEL))],
        out_specs=full((DB, D_MODEL)),
        scratch_shapes=[pltpu.VMEM((DB, D_MODEL), F32)],
        compiler_params=_cparams(1),
        name="ffn_sample",
    )(x, ug, ug, buf0, buf1, conv_w, conv_b.reshape(1, -1), w_down, g.reshape(1, -1), b.reshape(1, -1))


def kernel(x_prompt, x_sample, cache_a_k, cache_a_v, state_b, state_c, state_d_c, state_d_n, state_d_m,
           cache_mem_k, cache_mem_v, state_conv, page_table, mem_prompt,
           ev_w_in, ev_w_out, ev_lam, ev_subln_g, ev_lb_logits, ev_b_norm_g,
           od_w_in, od_b_if, od_w_out, od_c_norm_g, od_d_norm_g,
           ln_g, ln_b, xa_wq, xa_wkv, xa_wo, ffn_w_up, ffn_conv_w, ffn_conv_b, ffn_w_down):
    B, T, _ = x_prompt.shape
    DB, t_s, _ = x_sample.shape
    assert t_s == 1, "the sample group is a single decoding step"
    assert T % CHUNK == 0
    n_pool, page = cache_a_k.shape[1], cache_a_k.shape[2]
    n_pages = page_table.shape[1]
    past = n_pages * page
    n_mem = mem_prompt.shape[1]
    d_ff = ffn_w_down.shape[1]
    tf = d_ff // 2
    N = B * T
    TM = ROWS_MATMUL
    cols = lambda w: _wshape(w)[1] // SAMPLE_COL_TILES

    pos_p = jnp.arange(T)
    pos_s = jnp.full((DB,), past, jnp.int32)
    lb_table = jnp.cumsum(jax.nn.softmax(ev_lb_logits.astype(F32), axis=0), axis=0)

    xp = x_prompt.reshape(N, D_MODEL)
    xs = x_sample.reshape(DB, D_MODEL)
    outs = {k: [] for k in ("ak_p", "av_p", "ak_s", "av_s", "sb_p", "sb_s", "sc_p", "sc_s", "dc_p", "dc_s",
                            "dn_p", "dn_s", "dm_p", "dm_s", "mk_p", "mv_p", "cv_p", "cv_s")}
    swap = lambda s: jnp.swapaxes(s, -1, -2)
    mem_k_rows, mem_v_rows = _mem_rows(cache_mem_k), _mem_rows(cache_mem_v)
    wq_all, wkv_all, wo_all = xa_wq.astype(BF16), xa_wkv.astype(BF16), xa_wo.astype(BF16)
    w_up_all, w_down_all = ffn_w_up.astype(BF16), ffn_w_down.astype(BF16)

    for l in range(DEPTH):
        j = l // 2
        if l % 2 == 0:
            lam_init = 0.8 - 0.6 * math.exp(-0.3 * l)
            w_in = ev_w_in[j].astype(BF16)
            w_out = ev_w_out[j].astype(BF16)
            cos_t, sin_t = _rope_tables(pos_p, DH_A)
            qt, kb, kt32, vt, v_rows, h_b = _ev_in_prompt(xp.reshape(B, T, D_MODEL), w_in, cos_t, sin_t, TM)
            o_a = _dattn_prompt(qt, kb, vt, ev_lam[j], ev_subln_g[j], lam_init, ATTN_BLOCK)
            o_b, st = _hgrn_prompt(h_b, lb_table[j], TIME_BLOCK_EV)
            outs["ak_p"].append(kt32.reshape(B, N_HEADS, 2, DH_A, T).transpose(0, 4, 1, 2, 3))
            outs["av_p"].append(v_rows.reshape(B, T, N_HEADS, HEAD_W))
            outs["sb_p"].append(swap(st))
            xp = _proj_ln([o_a.reshape(N, MIX_W), o_b.reshape(N, MIX_W)], w_out,
                          xp, ln_g[l, 0], ln_b[l, 0], ROWS_LIGHT, "ev_out_prompt",
                          gates=[None, ("rms", ev_b_norm_g[j], "silu", h_b.reshape(N, -1), 3)])
            hs = _matmul(xs, w_in, DB, cols(w_in), "ev_in_sample")
            cos_s, sin_s = _rope_tables(pos_s, DH_A)
            qs, ks32, kts32 = _ev_prep_sample(hs, cos_s, sin_s)
            vs32 = hs[:, 2 * MIX_W:3 * MIX_W]
            kt_pool = cache_a_k.transpose(0, 1, 3, 4, 5, 2).reshape(-1, MIX_W, page)
            v_pool = cache_a_v.reshape(-1, page * N_HEADS, HEAD_W)
            oa_s = _dattn_sample(qs, ks32, vs32, kt_pool, v_pool, j * n_pool, page_table,
                                 ev_lam[j], ev_subln_g[j], lam_init)
            ob_s, sb_s = _hgrn_step(hs, state_b[j], lb_table[j], ev_b_norm_g[j], STEP_ROWS)
            outs["ak_s"].append(kts32.reshape(N_HEADS, 2, DH_A, DB).transpose(3, 0, 1, 2)[:, None])
            outs["av_s"].append(vs32.reshape(DB, 1, N_HEADS, HEAD_W))
            outs["sb_s"].append(sb_s)
            xs = _proj_ln([oa_s, ob_s], w_out, xs, ln_g[l, 0], ln_b[l, 0], DB, "ev_out_sample")
        else:
            n_main = 8 * MIX_W
            w_in = od_w_in[j][:, :n_main].astype(BF16)
            w_gate = jnp.pad(od_w_in[j][:, n_main:], ((0, 0), (0, LANES - 2 * N_HEADS))).astype(BF16)
            w_out = od_w_out[j].astype(BF16)
            cos_t, sin_t = _rope_tables(pos_p, HEAD_W)
            h, gates = _od_in_prompt(xp, w_in, w_gate, cos_t, sin_t, TM)
            o_c, o_d, sct, dct, dn, dm = _od_prompt(h.reshape(B, T, -1), gates.reshape(B, T, -1), od_b_if[j],
                                                    TIME_BLOCK_OD)
            outs["sc_p"].append(swap(sct))
            outs["dc_p"].append(swap(dct))
            outs["dn_p"].append(dn)
            outs["dm_p"].append(dm)
            xp = _proj_ln([o_c.reshape(N, MIX_W), o_d.reshape(N, MIX_W)], w_out,
                          xp, ln_g[l, 0], ln_b[l, 0], ROWS_LIGHT, "od_out_prompt",
                          gates=[("group", od_c_norm_g[j], "silu", h, 3), ("group", od_d_norm_g[j], "sigmoid", h, 7)])
            hs = _matmul(xs, w_in, DB, cols(w_in), "od_in_sample")
            gates_s = _matmul(xs, w_gate, DB, LANES, "od_gates_sample")
            cos_s, sin_s = _rope_tables(pos_s[:1], HEAD_W)
            oc_s, od_s, sc_s, dc_s, dn_s, dm_s = _od_step(hs, gates_s, od_b_if[j], cos_s, sin_s,
                                                          state_c[j], state_d_c[j], state_d_n[j], state_d_m[j],
                                                          od_c_norm_g[j], od_d_norm_g[j], STEP_ROWS)
            outs["sc_s"].append(sc_s)
            outs["dc_s"].append(dc_s)
            outs["dn_s"].append(dn_s)
            outs["dm_s"].append(dm_s)
            xs = _proj_ln([oc_s, od_s], w_out, xs, ln_g[l, 0], ln_b[l, 0], DB, "od_out_sample")

        wq, wo = (wq_all, l), (wo_all, l)
        mkv = _matmul(mem_prompt.reshape(B * n_mem, D_MODEL), (wkv_all, l), ROWS_MATMUL, cols((wkv_all, l)), "mem_kv")
        mk, mv = mkv[:, :D_MODEL], mkv[:, D_MODEL:]
        outs["mk_p"].append(mk.reshape(B, n_mem, N_HEADS, DH_X))
        outs["mv_p"].append(mv.reshape(B, n_mem, N_HEADS, DH_X))
        xp = _xattn_prompt(xp, wq, mk.astype(BF16).reshape(B, n_mem, D_MODEL),
                           mv.astype(BF16).reshape(B, n_mem, D_MODEL), wo, ln_g[l, 1], ln_b[l, 1], T, ROWS_LIGHT)
        q_s = _matmul(xs, wq, DB, cols(wq), "xattn_q_sample")
        xo_s = _xattn_sample(q_s, mem_k_rows, mem_v_rows, l * DB)
        xs = _proj_ln([xo_s], wo, xs, ln_g[l, 1], ln_b[l, 1], DB, "xattn_out_sample")

        w_up, w_down = (w_up_all, l), (w_down_all, l)
        tm_f = min(TM, T)
        xp, tails = _ffn_prompt(xp, w_up, ffn_conv_w[l], ffn_conv_b[l], w_down, ln_g[l, 2], ln_b[l, 2], T, TM, d_ff)
        tails = tails.reshape(B, T // tm_f, SUBLANES, d_ff)
        outs["cv_p"].append(tails[:, -1, SUBLANES - 2:, :])
        ug_s = _matmul(xs, w_up, DB, cols(w_up), "ffn_up_sample")
        buf = state_conv[l]
        xs = _ffn_sample(xs, ug_s, buf[:, 0, :], buf[:, 1, :], ffn_conv_w[l], ffn_conv_b[l], w_down,
                         ln_g[l, 2], ln_b[l, 2], tf)
        outs["cv_s"].append(jnp.stack([buf[:, 1, :], ug_s[:, :d_ff]], axis=1))

    st = lambda k: jnp.stack(outs[k])
    return (xp.reshape(B, T, D_MODEL), xs.reshape(DB, 1, D_MODEL),
            st("ak_p"), st("av_p"), st("ak_s"), st("av_s"), st("sb_p"), st("sb_s"),
            st("sc_p"), st("sc_s"), st("dc_p"), st("dc_s"), st("dn_p"), st("dn_s"), st("dm_p"), st("dm_s"),
            st("mk_p"), st("mv_p"), st("cv_p"), st("cv_s"))
```

```python
import functools
import math

import numpy as np
import jax
import jax.numpy as jnp
from jax import lax
from jax.experimental import pallas as pl
from jax.experimental.pallas import tpu as pltpu

F32 = jnp.float32
BF16 = jnp.bfloat16

D_MODEL = 1024
MIX_W = D_MODEL // 2
N_HEADS = 4
HEAD_W = MIX_W // N_HEADS
DH_A = HEAD_W // 2
DH_X = D_MODEL // N_HEADS
CHUNK = 64
OD_CHUNK = 128
ROPE_THETA = 10000.0
NORM_EPS = 1e-5
DEPTH = 2
DN_ALPHA = (2.0 * DEPTH) ** 0.25
LANES = 128
SUBLANES = 8
VMEM_LIMIT = 56 * 1024 * 1024
NEG_INF = float("-inf")

ROWS_MATMUL = 512
ROWS_LIGHT = 1024
ATTN_BLOCK = 1024
TIME_BLOCK_EV = 1024
TIME_BLOCK_OD = 512
SAMPLE_COL_TILES = 2
STEP_ROWS = 8

NT_DIMS = (((1,), (1,)), ((), ()))
TN_DIMS = (((0,), (0,)), ((), ()))


def _wspec(w, block_shape, index_map, **kw):
    if isinstance(w, tuple):
        stack, layer = w
        return stack, pl.BlockSpec((None,) + tuple(block_shape), lambda *a: (layer,) + tuple(index_map(*a)), **kw)
    return w, pl.BlockSpec(tuple(block_shape), index_map, **kw)


def _wshape(w):
    return w[0].shape[1:] if isinstance(w, tuple) else w.shape


def _cparams(n_axes, vmem=VMEM_LIMIT):
    return pltpu.CompilerParams(dimension_semantics=("arbitrary",) * n_axes, vmem_limit_bytes=vmem)


def _bdot(a, b):
    return jnp.dot(a.astype(BF16), b.astype(BF16), preferred_element_type=F32)


def _bdot_nt(a, b):
    return lax.dot_general(a.astype(BF16), b.astype(BF16), NT_DIMS, preferred_element_type=F32)


def _bdot_tn(a, b):
    return lax.dot_general(a.astype(BF16), b.astype(BF16), TN_DIMS, preferred_element_type=F32)


def _split3(x):
    p1 = x.astype(BF16)
    r1 = x - p1.astype(F32)
    p2 = r1.astype(BF16)
    p3 = (r1 - p2.astype(F32)).astype(BF16)
    return p1, p2, p3


def _cumsum_rows(tri, x):
    p1, p2, p3 = _split3(x)
    d = functools.partial(jnp.dot, preferred_element_type=F32)
    return d(tri, p1) + d(tri, p2) + d(tri, p3)


def _cumsum_lanes(x, triu):
    p1, p2, p3 = _split3(x)
    d = functools.partial(jnp.dot, preferred_element_type=F32)
    return d(p1, triu) + d(p2, triu) + d(p3, triu)


def _tri(L, lower):
    r = lax.broadcasted_iota(jnp.int32, (L, L), 0)
    c = lax.broadcasted_iota(jnp.int32, (L, L), 1)
    return (r >= c) if lower else (r <= c)


def _sigmoid(x):
    return 1.0 / (1.0 + jnp.exp(-x))


def _silu(x):
    return x * _sigmoid(x)


def _log_sigmoid(x):
    return jnp.minimum(x, 0.0) - jnp.log(1.0 + jnp.exp(-jnp.abs(x)))


def _layernorm(z, g, b):
    mu = jnp.mean(z, -1, keepdims=True)
    zc = z - mu
    var = jnp.mean(zc * zc, -1, keepdims=True)
    return zc * lax.rsqrt(var + NORM_EPS) * g + b


def _rmsnorm(x, g):
    return x * lax.rsqrt(jnp.mean(x * x, -1, keepdims=True) + NORM_EPS) * g


def _groupnorm(x, g):
    mu = jnp.mean(x, -1, keepdims=True)
    xc = x - mu
    var = jnp.mean(xc * xc, -1, keepdims=True)
    return xc * lax.rsqrt(var + NORM_EPS) * g


def _rope(x, cos, sin, half):
    outs = []
    for c in range(x.shape[1] // LANES):
        xc = x[:, c * LANES:(c + 1) * LANES]
        if 2 * half == LANES:
            sw = pltpu.roll(xc, half, 1)
        else:
            lane = lax.broadcasted_iota(jnp.int32, xc.shape, 1)
            first = (lane & (2 * half - 1)) < half
            sw = jnp.where(first, pltpu.roll(xc, LANES - half, 1), pltpu.roll(xc, half, 1))
        outs.append(xc * cos + sw * sin)
    return outs[0] if len(outs) == 1 else jnp.concatenate(outs, axis=1)


def _rope_tables(pos, d):
    inv = ROPE_THETA ** (-jnp.arange(0, d // 2, dtype=F32) * 2.0 / d)
    ang = pos.astype(F32)[:, None] * inv[None, :]
    cos, sin = jnp.cos(ang), jnp.sin(ang)
    reps = LANES // d
    cos_t = jnp.tile(jnp.concatenate([cos, cos], -1), (1, reps))
    sin_t = jnp.tile(jnp.concatenate([-sin, sin], -1), (1, reps))
    return cos_t, sin_t


def _col(row):
    return jnp.broadcast_to(row, (LANES, LANES)).T


def _mm_kernel(x_ref, w_ref, o_ref, xb_ref):
    @pl.when(pl.program_id(1) == 0)
    def _():
        xb_ref[...] = x_ref[...].astype(BF16)

    o_ref[...] = jnp.dot(xb_ref[...], w_ref[...], preferred_element_type=F32).astype(o_ref.dtype)


def _matmul(x, w, tm, tn, name, out_dtype=F32):
    M, K = x.shape
    N = _wshape(w)[1]
    tm, tn = min(tm, M), min(tn, N)
    w, w_spec = _wspec(w, (K, tn), lambda i, j: (0, j))
    return pl.pallas_call(
        _mm_kernel,
        out_shape=jax.ShapeDtypeStruct((M, N), out_dtype),
        grid=(M // tm, N // tn),
        in_specs=[pl.BlockSpec((tm, K), lambda i, j: (i, 0)), w_spec],
        out_specs=pl.BlockSpec((tm, tn), lambda i, j: (i, j)),
        scratch_shapes=[pltpu.VMEM((tm, K), BF16)],
        compiler_params=_cparams(2),
        name=name,
    )(x, w)


_HEAD_NORMS = {"rms": _rmsnorm, "group": _groupnorm}
_GATE_ACTS = {"silu": _silu, "sigmoid": _sigmoid}


def _proj_ln_kernel(*refs, n_in, gated):
    a_refs, w_refs = refs[:n_in], refs[n_in:2 * n_in]
    x_ref, g_ref, b_ref = refs[2 * n_in:2 * n_in + 3]
    extra, o_ref = refs[2 * n_in + 3:-1], refs[-1]
    y = None
    for a_ref, w_ref, gate in zip(a_refs, w_refs, gated):
        a = a_ref[...]
        if gate is not None:
            norm, act = _HEAD_NORMS[gate[0]], _GATE_ACTS[gate[1]]
            gain_ref, gate_ref, extra = extra[0], extra[1], extra[2:]
            a = jnp.concatenate([norm(_head(a, h), gain_ref[...]) for h in range(N_HEADS)], axis=1)
            a = a * act(gate_ref[...])
        ya = _bdot(a, w_ref[...])
        y = ya if y is None else y + ya
    o_ref[...] = _layernorm(DN_ALPHA * x_ref[...] + y, g_ref[...], b_ref[...])


def _proj_ln(acts, weight, x, g, b, tm, name, gates=None):
    M = x.shape[0]
    tm = min(tm, M)
    n_in = len(acts)
    gates = gates or [None] * n_in
    in_specs = [pl.BlockSpec((tm, a.shape[1]), lambda i: (i, 0)) for a in acts]
    w_args = []
    for r, a in enumerate(acts):
        w_arr, w_spec = _wspec(weight, (a.shape[1], D_MODEL), lambda i, r=r: (r, 0))
        w_args.append(w_arr)
        in_specs.append(w_spec)
    in_specs += [pl.BlockSpec((tm, D_MODEL), lambda i: (i, 0)),
                 pl.BlockSpec((1, D_MODEL), lambda i: (0, 0)),
                 pl.BlockSpec((1, D_MODEL), lambda i: (0, 0))]
    extra = []
    for gate in gates:
        if gate is not None:
            _, gain, _, gate_arr, blk = gate
            in_specs += [pl.BlockSpec((1, HEAD_W), lambda i: (0, 0)),
                         pl.BlockSpec((tm, MIX_W), lambda i, blk=blk: (i, blk))]
            extra += [gain.reshape(1, -1), gate_arr]
    return pl.pallas_call(
        functools.partial(_proj_ln_kernel, n_in=n_in,
                          gated=tuple(None if gt is None else (gt[0], gt[2]) for gt in gates)),
        out_shape=jax.ShapeDtypeStruct((M, D_MODEL), F32),
        grid=(M // tm,),
        in_specs=in_specs,
        out_specs=pl.BlockSpec((tm, D_MODEL), lambda i: (i, 0)),
        compiler_params=_cparams(1),
        name=name,
    )(*acts, *w_args, x, g.reshape(1, -1), b.reshape(1, -1), *extra)


Q_SCALE = DH_A ** -0.5 * math.log2(math.e)
VT_ROWS = HEAD_W + 16


def _ev_in_prompt_kernel(x_ref, w_ref, cos_ref, sin_ref, qt_ref, kb_ref, kt_ref, vt_ref, vr_ref, hb_ref):
    xb = x_ref[0].astype(BF16)
    cos, sin = cos_ref[...], sin_ref[...]
    proj = lambda lo, hi: jnp.dot(xb, w_ref[:, lo * MIX_W:hi * MIX_W], preferred_element_type=F32)
    hb_ref[0] = proj(3, 7)
    q = _rope(proj(0, 1), cos, sin, DH_A // 2)
    k = _rope(proj(1, 2), cos, sin, DH_A // 2)
    qt_ref[0] = (q * Q_SCALE).T.astype(BF16)
    kb_ref[0] = k.astype(BF16)
    kt_ref[0] = k.T
    v = proj(2, 3)
    tm = v.shape[0]
    vt = v.T.astype(BF16)
    ones = jnp.ones((VT_ROWS - HEAD_W, tm), BF16)
    for h in range(N_HEADS):
        vt_ref[0, h, :HEAD_W, :] = vt[h * HEAD_W:(h + 1) * HEAD_W]
        vt_ref[0, h, HEAD_W:, :] = ones
        vr_ref[0, pl.ds(h, tm, stride=N_HEADS), :] = _head(v, h)


def _ev_in_prompt(x3, w_in, cos_t, sin_t, tm):
    B, T, _ = x3.shape
    tm = min(tm, T)
    tab = pl.BlockSpec((tm, LANES), lambda b, i: (i, 0))
    tr = pl.BlockSpec((1, MIX_W, tm), lambda b, i: (b, 0, i))
    tshape = lambda dt: jax.ShapeDtypeStruct((B, MIX_W, T), dt)
    return pl.pallas_call(
        _ev_in_prompt_kernel,
        out_shape=(tshape(BF16), jax.ShapeDtypeStruct((B, T, MIX_W), BF16), tshape(F32),
                   jax.ShapeDtypeStruct((B, N_HEADS, VT_ROWS, T), BF16),
                   jax.ShapeDtypeStruct((B, T * N_HEADS, HEAD_W), F32),
                   jax.ShapeDtypeStruct((B, T, 4 * MIX_W), F32)),
        grid=(B, T // tm),
        in_specs=[pl.BlockSpec((1, tm, D_MODEL), lambda b, i: (b, i, 0)),
                  pl.BlockSpec(w_in.shape, lambda b, i: (0, 0)), tab, tab],
        out_specs=(tr, pl.BlockSpec((1, tm, MIX_W), lambda b, i: (b, i, 0)), tr,
                   pl.BlockSpec((1, N_HEADS, VT_ROWS, tm), lambda b, i: (b, 0, 0, i)),
                   pl.BlockSpec((1, tm * N_HEADS, HEAD_W), lambda b, i: (b, i, 0)),
                   pl.BlockSpec((1, tm, 4 * MIX_W), lambda b, i: (b, i, 0))),
        compiler_params=_cparams(2),
        name="ev_in_prompt",
    )(x3, w_in, cos_t, sin_t)


def _ev_prep_sample_kernel(qk_ref, cos_ref, sin_ref, q_ref, k_ref, kt_ref):
    cos, sin = cos_ref[...], sin_ref[...]
    k = _rope(qk_ref[:, MIX_W:], cos, sin, DH_A // 2)
    q_ref[...] = _rope(qk_ref[:, :MIX_W], cos, sin, DH_A // 2) * Q_SCALE
    k_ref[...] = k
    kt_ref[...] = k.T


def _ev_prep_sample(h, cos_t, sin_t):
    DB = h.shape[0]
    full = lambda shape: pl.BlockSpec(shape, lambda i: (0,) * len(shape))
    return pl.pallas_call(
        _ev_prep_sample_kernel,
        out_shape=(jax.ShapeDtypeStruct((DB, MIX_W), F32), jax.ShapeDtypeStruct((DB, MIX_W), F32),
                   jax.ShapeDtypeStruct((MIX_W, DB), F32)),
        grid=(1,),
        in_specs=[full((DB, 2 * MIX_W)), full((DB, LANES)), full((DB, LANES))],
        out_specs=(full((DB, MIX_W)), full((DB, MIX_W)), full((MIX_W, DB))),
        compiler_params=_cparams(1),
        name="ev_prep_sample",
    )(h, cos_t, sin_t)


def _lambda(lam_ref, lam_init):
    lp = lam_ref[...]
    s01 = jnp.sum(lp[0:1] * lp[1:2], axis=1, keepdims=True)
    s23 = jnp.sum(lp[2:3] * lp[3:4], axis=1, keepdims=True)
    return jnp.exp(s01) - jnp.exp(s23) + lam_init


def _dattn_kernel(lam_ref, g_ref, qt_ref, k_ref, vt_ref, o_ref, m_ref, acc_ref, *, tq, lam_init):
    i = pl.program_id(2)
    qt = qt_ref[0]
    sub = lax.broadcasted_iota(jnp.int32, qt.shape, 0)
    zero = jnp.zeros_like(qt)
    q_maps = [jnp.where(sub < DH_A, qt, zero), jnp.where(sub >= DH_A, qt, zero)]
    maps = range(2)
    m_ref[...] = jnp.full(m_ref.shape, NEG_INF, F32)
    acc_ref[...] = jnp.zeros(acc_ref.shape, F32)

    def steps(js):
        ks, vts, ss = [], [], []
        for j in js:
            start = pl.multiple_of(j * tq, tq)
            ks.append(k_ref[0, pl.ds(start, tq), :])
            vts.append(vt_ref[0, 0, :, pl.ds(start, tq)])
        for kj in ks:
            ss.append([jnp.dot(kj, q_maps[c], preferred_element_type=F32) for c in maps])
        for n in range(len(js)):
            for c in maps:
                s = ss[n][c]
                m_prev = m_ref[c]
                m_new = jnp.maximum(m_prev, jnp.max(s, axis=0, keepdims=True))
                alpha = jnp.exp2(m_prev - m_new)
                p = jnp.exp2(s - m_new)
                acc_ref[c] = alpha * acc_ref[c] + jnp.dot(vts[n], p.astype(BF16), preferred_element_type=F32)
                m_ref[c] = m_new

    def body(jj, carry):
        steps([2 * jj, 2 * jj + 1])
        return carry

    lax.fori_loop(0, lax.shift_right_logical(i, 1), body, 0)

    @pl.when((i & 1) == 1)
    def _():
        steps([i - 1])

    half = tq // 2
    start = pl.multiple_of(i * tq, tq)
    k_top, k_bot = k_ref[0, pl.ds(start, half), :], k_ref[0, pl.ds(start + half, half), :]
    vt_top, vt_bot = vt_ref[0, 0, :, pl.ds(start, half)], vt_ref[0, 0, :, pl.ds(start + half, half)]
    causal = lambda s: jnp.where(lax.broadcasted_iota(jnp.int32, s.shape, 0)
                                 <= lax.broadcasted_iota(jnp.int32, s.shape, 1), s, NEG_INF)
    s_top = [causal(jnp.dot(k_top, q_maps[c], preferred_element_type=F32)) for c in maps]
    s_bot = [causal(jnp.dot(k_bot, q_maps[c][:, half:], preferred_element_type=F32)) for c in maps]
    for c in maps:
        m_prev = m_ref[c]
        m_new = jnp.maximum(m_prev, jnp.max(s_top[c], axis=0, keepdims=True))
        m_new = jnp.concatenate([m_new[:, :half],
                                 jnp.maximum(m_new[:, half:], jnp.max(s_bot[c], axis=0, keepdims=True))], axis=1)
        p_top = jnp.exp2(s_top[c] - m_new).astype(BF16)
        p_bot = jnp.exp2(s_bot[c] - m_new[:, half:]).astype(BF16)
        acc_ref[c] = jnp.exp2(m_prev - m_new) * acc_ref[c] + jnp.dot(vt_top, p_top, preferred_element_type=F32)
        acc_ref[c, :, half:] = acc_ref[c, :, half:] + jnp.dot(vt_bot, p_bot, preferred_element_type=F32)
    lam = _lambda(lam_ref, lam_init)
    o = [acc_ref[c, :HEAD_W, :] / acc_ref[c, HEAD_W:HEAD_W + 1, :] for c in maps]
    d = (o[0] - lam * o[1]).T
    o_ref[0] = (_rmsnorm(d, g_ref[...]) * (1.0 - lam_init)).astype(BF16)


def _dattn_prompt(qt, kb, vt, lam_p, subln_g, lam_init, tq):
    B, T, _ = kb.shape
    tq = min(tq, T)
    return pl.pallas_call(
        functools.partial(_dattn_kernel, tq=tq, lam_init=lam_init),
        out_shape=jax.ShapeDtypeStruct((B, T, MIX_W), BF16),
        grid=(B, N_HEADS, T // tq),
        in_specs=[pl.BlockSpec((4, DH_A), lambda b, h, i: (0, 0)),
                  pl.BlockSpec((1, HEAD_W), lambda b, h, i: (0, 0)),
                  pl.BlockSpec((1, HEAD_W, tq), lambda b, h, i: (b, h, i)),
                  pl.BlockSpec((1, T, HEAD_W), lambda b, h, i: (b, 0, h)),
                  pl.BlockSpec((1, 1, VT_ROWS, T), lambda b, h, i: (b, h, 0, 0))],
        out_specs=pl.BlockSpec((1, tq, HEAD_W), lambda b, h, i: (b, i, h)),
        scratch_shapes=[pltpu.VMEM((2, 1, tq), F32), pltpu.VMEM((2, VT_ROWS, tq), F32)],
        compiler_params=_cparams(3),
        name="dattn_prompt",
    )(lam_p, subln_g.reshape(1, -1), qt, kb, vt)


def _head(x, h):
    return x[:, h * HEAD_W:(h + 1) * HEAD_W]


def _gla_chunk(seqs, st_ref, tril_mask):
    seqs = [tuple(x.astype(BF16) for x in s[:4]) + (s[4],) for s in seqs]
    chains = [(n, h) for n in range(len(seqs)) for h in range(N_HEADS)]
    part = lambda n, i, h: _head(seqs[n][i], h)
    sts = [st_ref[N_HEADS * n + h] for n, h in chains]
    inter = [_bdot_nt(part(n, 0, h), sts[c]) for c, (n, h) in enumerate(chains)]
    upd = [_bdot_tn(part(n, 3, h), part(n, 2, h)) for n, h in chains]
    attn = [jnp.where(tril_mask, _bdot_nt(part(n, 0, h), part(n, 1, h)), 0.0) for n, h in chains]
    intra = [_bdot(attn[c], part(n, 3, h)) for c, (n, h) in enumerate(chains)]
    for c, (n, h) in enumerate(chains):
        dec = seqs[n][4]
        st_ref[N_HEADS * n + h] = sts[c] * (dec[h] if isinstance(dec, (list, tuple)) else _head(dec, h)) + upd[c]
    outs = [intra[c] + inter[c] for c in range(len(chains))]
    return [outs[N_HEADS * n:N_HEADS * (n + 1)] for n in range(len(seqs))]


def _hgrn_kernel(q_ref, f_ref, i_ref, lb_ref, o_ref, s_ref, st_ref, *, tt, L, n_seq):
    t = pl.program_id(0)

    @pl.when(t == 0)
    def _():
        st_ref[...] = jnp.zeros(st_ref.shape, F32)

    tril_mask = _tri(L, True)
    tril = tril_mask.astype(BF16)

    def body(c, carry):
        rows = pl.ds(pl.multiple_of(c * L, L), L)
        lb = lb_ref[...]
        seqs = []
        for n in range(n_seq):
            f = lb + (1.0 - lb) * _sigmoid(f_ref[n, rows, :])
            b = _cumsum_rows(tril, jnp.log(f))
            b_end = b[L - 1:L, :]
            k = 1.0 - f
            q_in = _silu(q_ref[n, rows, :]) * jnp.exp(b)
            seqs.append((q_in, k * jnp.exp(-b), k * jnp.exp(b_end - b), i_ref[n, rows, :], jnp.exp(b_end)))
        for n, o in enumerate(_gla_chunk(seqs, st_ref, tril_mask)):
            o_ref[n, rows, :] = jnp.concatenate(o, axis=1)
        return carry

    lax.fori_loop(0, tt // L, body, 0, unroll=2)

    @pl.when(t == pl.num_programs(0) - 1)
    def _():
        s_ref[...] = st_ref[...]


def _hgrn_prompt(h3, lb, tt):
    B, T, _ = h3.shape
    tt = min(tt, T)
    L = math.gcd(T, CHUNK)
    col = lambda c: pl.BlockSpec((B, tt, MIX_W), lambda t: (0, t, c))
    o, st = pl.pallas_call(
        functools.partial(_hgrn_kernel, tt=tt, L=L, n_seq=B),
        out_shape=(jax.ShapeDtypeStruct((B, T, MIX_W), F32),
                   jax.ShapeDtypeStruct((B * N_HEADS, HEAD_W, HEAD_W), F32)),
        grid=(T // tt,),
        in_specs=[col(0), col(1), col(2), pl.BlockSpec((1, MIX_W), lambda t: (0, 0))],
        out_specs=(pl.BlockSpec((B, tt, MIX_W), lambda t: (0, t, 0)),
                   pl.BlockSpec((B * N_HEADS, HEAD_W, HEAD_W), lambda t: (0, 0, 0))),
        scratch_shapes=[pltpu.VMEM((B * N_HEADS, HEAD_W, HEAD_W), F32)],
        compiler_params=_cparams(1),
        name="hgrn_prompt",
    )(h3, h3, h3, lb.reshape(1, -1))
    return o, st.reshape(B, N_HEADS, HEAD_W, HEAD_W)


def _log_gamma(h):
    return float(np.log(1.0 - 2.0 ** (-5.0 - h)))


def _od_kernel(cq_ref, ck_ref, cv_ref, dq_ref, dk_ref, dv_ref, gc_ref, gr_ref,
               bc_ref, br_ref,
               oc_ref, od_ref, sc_ref, dc_ref, dn_ref, dm_ref,
               st_ref, ct_ref, n_ref, m_ref, *, tt, L, n_seq):
    t = pl.program_id(0)

    @pl.when(t == 0)
    def _():
        st_ref[...] = jnp.zeros(st_ref.shape, F32)
        ct_ref[...] = jnp.zeros(ct_ref.shape, F32)
        n_ref[...] = jnp.zeros(n_ref.shape, F32)
        m_ref[...] = jnp.zeros(m_ref.shape, F32)

    tril_mask = _tri(L, True)
    tril = tril_mask.astype(BF16)
    triu = _tri(L, False).astype(BF16)
    scale = HEAD_W ** -0.5
    heads = range(N_HEADS)
    pos1 = (lax.broadcasted_iota(jnp.int32, (L, HEAD_W), 0) + 1).astype(F32)
    ret_b = jnp.concatenate([pos1 * _log_gamma(h) for h in heads], axis=1)
    ret_b_end = jnp.concatenate([jnp.full((L, HEAD_W), L * _log_gamma(h), F32) for h in heads], axis=1)
    ret_q_dec, ret_k_dec, ret_k_end = jnp.exp(ret_b), jnp.exp(-ret_b), jnp.exp(ret_b_end - ret_b)
    ret_dec = [math.exp(L * _log_gamma(h)) for h in heads]

    seqs = range(n_seq)
    chains = [(n, h) for n in seqs for h in heads]

    def body(c, carry):
        rows = pl.ds(pl.multiple_of(c * L, L), L)
        ret = []
        for n in seqs:
            q, k = cq_ref[n, rows, :], ck_ref[n, rows, :]
            ret.append((q * ret_q_dec, k * ret_k_dec, k * ret_k_end, cv_ref[n, rows, :], ret_dec))
        for n, o in enumerate(_gla_chunk(ret, st_ref, tril_mask)):
            oc_ref[n, rows, :] = jnp.concatenate(o, axis=1)
        g_c = [gc_ref[n, rows, :] + bc_ref[...] for n in seqs]
        g_r = [gr_ref[n, c] + br_ref[...] for n in seqs]
        lf_c = [_log_sigmoid(g) for g in g_c]
        b_c = [_cumsum_rows(tril, jnp.concatenate(
            [jnp.broadcast_to(lf_c[n][:, N_HEADS + h:N_HEADS + h + 1], (L, HEAD_W)) for h in heads], axis=1))
            for n in seqs]
        b_r = [_cumsum_lanes(_log_sigmoid(g), triu) for g in g_r]
        dq = [dq_ref[n, rows, :] * scale for n in seqs]
        dk = [dk_ref[n, rows, :] for n in seqs]
        dqb = [x.astype(BF16) for x in dq]
        dkb = [x.astype(BF16) for x in dk]
        dvb = [dv_ref[n, rows, :].astype(BF16) for n in seqs]
        cts = [ct_ref[i] for i in range(len(chains))]
        q_c = [_bdot_nt(_head(dqb[n], h), cts[i]) for i, (n, h) in enumerate(chains)]
        q_k = [_bdot_nt(_head(dqb[n], h), _head(dkb[n], h)) for n, h in chains]
        m_prev = [m_ref[i:i + 1, :] for i in range(len(chains))]
        bcw, m_t, w_mat = [], [], []
        for i, (n, h) in enumerate(chains):
            bcw.append(_head(b_c[n], h))
            bc = bcw[i][:, :L]
            dm = jnp.where(tril_mask, bc - b_r[n][N_HEADS + h:N_HEADS + h + 1, :] + g_r[n][h:h + 1, :], NEG_INF)
            m_t.append(jnp.maximum(bcw[i] + m_prev[i], jnp.max(dm, axis=1, keepdims=True)))
            w_mat.append(jnp.exp(dm - m_t[i][:, :L]) * q_k[i])
        w_v = [_bdot(w_mat[i], _head(dvb[n], h)) for i, (n, h) in enumerate(chains)]
        kws, c_scales = [], []
        for i, (n, h) in enumerate(chains):
            m_new = m_t[i][L - 1:L, :]
            b_last = bcw[i][L - 1:L, :]
            c_scales.append(jnp.exp(b_last + m_prev[i] - m_new))
            kws.append(_head(dk[n], h) * jnp.exp(b_last - bcw[i] + g_c[n][:, h:h + 1] - m_new))
            m_ref[i:i + 1, :] = m_new
        c_upd = [_bdot_tn(_head(dvb[n], h), kws[i]) for i, (n, h) in enumerate(chains)]
        hs = []
        for i, (n, h) in enumerate(chains):
            n_row = n_ref[i:i + 1, :]
            inter = jnp.exp(bcw[i] + m_prev[i] - m_t[i])
            num = inter * q_c[i] + w_v[i]
            den = (inter[:, 0:1] * jnp.sum(_head(dq[n], h) * n_row, axis=1, keepdims=True)
                   + jnp.sum(w_mat[i], axis=1, keepdims=True))
            hs.append(num / jnp.maximum(jnp.abs(den), jnp.exp(-m_t[i])))
            ct_ref[i] = c_scales[i] * cts[i] + c_upd[i]
            n_ref[i:i + 1, :] = c_scales[i] * n_row + jnp.sum(kws[i], axis=0, keepdims=True)
        for n in seqs:
            od_ref[n, rows, :] = jnp.concatenate(hs[N_HEADS * n:N_HEADS * (n + 1)], axis=1)
        return carry

    lax.fori_loop(0, tt // L, body, 0)

    @pl.when(t == pl.num_programs(0) - 1)
    def _():
        sc_ref[...] = st_ref[...]
        dc_ref[...] = ct_ref[...]
        dn_ref[...] = n_ref[...]
        dm_ref[...] = m_ref[...]


def _od_in_kernel(x_ref, w_ref, wg_ref, cos_ref, sin_ref, h_ref, g_ref):
    xb = x_ref[...].astype(BF16)
    cos, sin = cos_ref[...], sin_ref[...]
    proj = lambda lo, hi: jnp.dot(xb, w_ref[:, lo * MIX_W:hi * MIX_W], preferred_element_type=F32)
    h_ref[:, 2 * MIX_W:] = proj(2, 8)
    h_ref[:, :MIX_W] = _rope(proj(0, 1), cos, sin, HEAD_W // 2)
    h_ref[:, MIX_W:2 * MIX_W] = _rope(proj(1, 2), cos, sin, HEAD_W // 2) * (HEAD_W ** -0.5)
    g_ref[...] = jnp.dot(xb, wg_ref[...], preferred_element_type=F32)


def _od_in_prompt(x, w_in, w_gate, cos_t, sin_t, tm):
    M, K = x.shape
    N = w_in.shape[1]
    tm = min(tm, cos_t.shape[0])
    n_tab = cos_t.shape[0] // tm
    tab = pl.BlockSpec((tm, LANES), lambda i: (i % n_tab, 0))
    return pl.pallas_call(
        _od_in_kernel,
        out_shape=(jax.ShapeDtypeStruct((M, N), F32), jax.ShapeDtypeStruct((M, LANES), F32)),
        grid=(M // tm,),
        in_specs=[pl.BlockSpec((tm, K), lambda i: (i, 0)),
                  pl.BlockSpec((K, N), lambda i: (0, 0)),
                  pl.BlockSpec((K, LANES), lambda i: (0, 0)), tab, tab],
        out_specs=(pl.BlockSpec((tm, N), lambda i: (i, 0)), pl.BlockSpec((tm, LANES), lambda i: (i, 0))),
        compiler_params=_cparams(1),
        name="od_in_prompt",
    )(x, w_in, w_gate, cos_t, sin_t)


def _od_prompt(h3, gates, b_if, tt):
    B, T, _ = h3.shape
    tt = min(tt, T)
    L = math.gcd(T, OD_CHUNK)
    gates_r = gates[:, :, :2 * N_HEADS].reshape(B, T // L, L, 2 * N_HEADS).transpose(0, 1, 3, 2)
    b_pad = jnp.zeros((1, LANES), F32).at[0, :2 * N_HEADS].set(b_if)
    n_chain = B * N_HEADS
    n_row = -(-n_chain // SUBLANES) * SUBLANES
    col = lambda c: pl.BlockSpec((B, tt, MIX_W), lambda t: (0, t, c))
    mat_state = pl.BlockSpec((n_chain, HEAD_W, HEAD_W), lambda t: (0, 0, 0))
    row_state = pl.BlockSpec((n_row, LANES), lambda t: (0, 0))
    out_blk = pl.BlockSpec((B, tt, MIX_W), lambda t: (0, t, 0))
    mat_shape = jax.ShapeDtypeStruct((n_chain, HEAD_W, HEAD_W), F32)
    row_shape = jax.ShapeDtypeStruct((n_row, LANES), F32)
    o_c, o_d, sct, dct, dn, dm = pl.pallas_call(
        functools.partial(_od_kernel, tt=tt, L=L, n_seq=B),
        out_shape=(jax.ShapeDtypeStruct((B, T, MIX_W), F32), jax.ShapeDtypeStruct((B, T, MIX_W), F32),
                   mat_shape, mat_shape, row_shape, row_shape),
        grid=(T // tt,),
        in_specs=[col(0), col(1), col(2), col(4), col(5), col(6),
                  pl.BlockSpec((B, tt, LANES), lambda t: (0, t, 0)),
                  pl.BlockSpec((B, tt // L, 2 * N_HEADS, L), lambda t: (0, t, 0, 0)),
                  pl.BlockSpec((1, LANES), lambda t: (0, 0)),
                  pl.BlockSpec((2 * N_HEADS, 1), lambda t: (0, 0))],
        out_specs=(out_blk, out_blk, mat_state, mat_state, row_state, row_state),
        scratch_shapes=[pltpu.VMEM(mat_shape.shape, F32), pltpu.VMEM(mat_shape.shape, F32),
                        pltpu.VMEM(row_shape.shape, F32), pltpu.VMEM(row_shape.shape, F32)],
        compiler_params=_cparams(1),
        name="od_prompt",
    )(h3, h3, h3, h3, h3, h3, gates, gates_r, b_pad, b_if.reshape(-1, 1))
    per_seq = lambda s: s.reshape((B, N_HEADS) + s.shape[1:])
    return (o_c, o_d, per_seq(sct), per_seq(dct), per_seq(dn[:n_chain]), per_seq(dm[:n_chain, 0]))


def _xattn_kernel(x_ref, wq_ref, mk_ref, mv_ref, wo_ref, g_ref, b_ref, o_ref):
    x = x_ref[...]
    q = _bdot(x, wq_ref[...])
    qb = (q * (DH_X ** -0.5)).astype(BF16)
    heads = range(N_HEADS)
    cols = [slice(h * DH_X, (h + 1) * DH_X) for h in heads]
    s = [lax.dot_general(qb[:, cols[h]], mk_ref[0, :, cols[h]], NT_DIMS, preferred_element_type=F32) for h in heads]
    p = [jnp.exp(s[h] - jnp.max(s[h], axis=1, keepdims=True)) for h in heads]
    l = [jnp.sum(p[h], axis=1, keepdims=True) for h in heads]
    pv = [jnp.dot(p[h].astype(BF16), mv_ref[0, :, cols[h]], preferred_element_type=F32) for h in heads]
    outs = [(pv[h] / l[h]).astype(BF16) for h in heads]
    y = jnp.dot(jnp.concatenate(outs, axis=1), wo_ref[...], preferred_element_type=F32)
    o_ref[...] = _layernorm(DN_ALPHA * x + y, g_ref[...], b_ref[...])


def _xattn_prompt(x, wq, mk, mv, wo, g, b, T, tm):
    M = x.shape[0]
    tm = min(tm, T)
    n_mem = mk.shape[1]
    per_b = T // tm
    full = lambda shape: pl.BlockSpec(shape, lambda i: (0,) * len(shape))
    mem = pl.BlockSpec((1, n_mem, D_MODEL), lambda i: (i // per_b, 0, 0))
    wq, wq_spec = _wspec(wq, (D_MODEL, D_MODEL), lambda i: (0, 0))
    wo, wo_spec = _wspec(wo, (D_MODEL, D_MODEL), lambda i: (0, 0))
    return pl.pallas_call(
        _xattn_kernel,
        out_shape=jax.ShapeDtypeStruct((M, D_MODEL), F32),
        grid=(M // tm,),
        in_specs=[pl.BlockSpec((tm, D_MODEL), lambda i: (i, 0)), wq_spec, mem, mem,
                  wo_spec, full((1, D_MODEL)), full((1, D_MODEL))],
        out_specs=pl.BlockSpec((tm, D_MODEL), lambda i: (i, 0)),
        compiler_params=_cparams(1),
        name="xattn_prompt",
    )(x, wq, mk, mv, wo, g.reshape(1, -1), b.reshape(1, -1))


def _ffn_kernel(x_ref, wu_ref, wg_ref, cw_ref, cb_ref, wd_ref, g_ref, b_ref, o_ref, tail_ref,
                xb_ref, acc_ref, stage_ref, carry_ref, *, tm, per_b, single):
    i, j = pl.program_id(0), pl.program_id(1)

    @pl.when(j == 0)
    def _():
        xb_ref[...] = x_ref[...].astype(BF16)

    @pl.when(i % per_b == 0)
    def _():
        carry_ref[j] = jnp.zeros(carry_ref.shape[1:], F32)

    xb = xb_ref[...]
    stage_ref[0:SUBLANES, :] = carry_ref[j]
    stage_ref[SUBLANES:, :] = jnp.dot(xb, wu_ref[...], preferred_element_type=F32)
    last = stage_ref[tm:tm + SUBLANES, :]
    carry_ref[j] = last
    tail_ref[0] = last
    conv = (cb_ref[...] + cw_ref[0:1, :] * stage_ref[SUBLANES - 2:SUBLANES - 2 + tm, :]
            + cw_ref[1:2, :] * stage_ref[SUBLANES - 1:SUBLANES - 1 + tm, :]
            + cw_ref[2:3, :] * stage_ref[SUBLANES:, :])
    gate = jnp.dot(xb, wg_ref[...], preferred_element_type=F32)
    y = _bdot(jax.nn.gelu(conv) * gate, wd_ref[...])
    if single:
        o_ref[...] = _layernorm(DN_ALPHA * x_ref[...] + y, g_ref[...], b_ref[...])
        return

    @pl.when(j == 0)
    def _():
        acc_ref[...] = y

    @pl.when(j > 0)
    def _():
        acc_ref[...] = acc_ref[...] + y

    @pl.when(j == pl.num_programs(1) - 1)
    def _():
        o_ref[...] = _layernorm(DN_ALPHA * x_ref[...] + acc_ref[...], g_ref[...], b_ref[...])


def _ffn_prompt(x, w_up, conv_w, conv_b, w_down, g, b, T, tm, tf):
    M = x.shape[0]
    d_ff = _wshape(w_down)[0]
    tm = min(tm, T)
    nf = d_ff // tf
    per_b = T // tm
    wmode = dict(pipeline_mode=pl.Buffered(1)) if nf == 1 else {}
    w_up_arr, up_spec = _wspec(w_up, (D_MODEL, tf), lambda i, j: (0, j), **wmode)
    _, gate_spec = _wspec(w_up, (D_MODEL, tf), lambda i, j: (0, nf + j), **wmode)
    w_down_arr, down_spec = _wspec(w_down, (tf, D_MODEL), lambda i, j: (j, 0), **wmode)
    return pl.pallas_call(
        functools.partial(_ffn_kernel, tm=tm, per_b=per_b, single=nf == 1),
        out_shape=(jax.ShapeDtypeStruct((M, D_MODEL), F32),
                   jax.ShapeDtypeStruct((M // tm, SUBLANES, d_ff), F32)),
        grid=(M // tm, nf),
        in_specs=[pl.BlockSpec((tm, D_MODEL), lambda i, j: (i, 0)),
                  up_spec, gate_spec,
                  pl.BlockSpec((3, tf), lambda i, j: (0, j)),
                  pl.BlockSpec((1, tf), lambda i, j: (0, j)),
                  down_spec,
                  pl.BlockSpec((1, D_MODEL), lambda i, j: (0, 0)),
                  pl.BlockSpec((1, D_MODEL), lambda i, j: (0, 0))],
        out_specs=(pl.BlockSpec((tm, D_MODEL), lambda i, j: (i, 0)),
                   pl.BlockSpec((1, SUBLANES, tf), lambda i, j: (i, 0, j))),
        scratch_shapes=[pltpu.VMEM((tm, D_MODEL), BF16), pltpu.VMEM((tm, D_MODEL), F32),
                        pltpu.VMEM((tm + SUBLANES, tf), F32), pltpu.VMEM((nf, SUBLANES, tf), F32)],
        compiler_params=_cparams(2),
        name="ffn_prompt",
    )(x, w_up_arr, w_up_arr, conv_w, conv_b.reshape(1, -1), w_down_arr, g.reshape(1, -1), b.reshape(1, -1))


def _dattn_sample_kernel(*refs, n_pages, page, lam_init, rows):
    lam_ref, g_ref, q_ref, kn_ref, vn_ref = refs[1:6]
    o_ref = refs[6 + 2 * rows * n_pages]
    for r in range(rows):
        k_refs = refs[6 + r * n_pages:6 + (r + 1) * n_pages]
        v_refs = refs[6 + (rows + r) * n_pages:6 + (rows + r + 1) * n_pages]
        _dattn_sample_row(lam_ref, g_ref, q_ref.at[r], kn_ref.at[r], vn_ref.at[r], k_refs, v_refs, o_ref.at[r],
                          n_pages=n_pages, page=page, lam_init=lam_init)


def _dattn_sample_row(lam_ref, g_ref, q_ref, kn_ref, vn_ref, k_refs, v_refs, o_ref, *, n_pages, page, lam_init):
    n_rows = 2 * N_HEADS
    q = q_ref[...]
    lane = lax.broadcasted_iota(jnp.int32, (n_rows, MIX_W), 1)
    row = lax.broadcasted_iota(jnp.int32, (n_rows, MIX_W), 0)
    qbd = jnp.where(lax.shift_right_logical(lane, 6) == row, jnp.broadcast_to(q, (n_rows, MIX_W)), 0.0)
    s_new = jnp.sum(qbd * kn_ref[...], axis=1, keepdims=True)
    scores = [_bdot(qbd, k_refs[p][0]) for p in range(n_pages)]
    m = s_new
    for s in scores:
        m = jnp.maximum(m, jnp.max(s, axis=1, keepdims=True))
    p_new = jnp.exp2(s_new - m)
    l = p_new
    vn = vn_ref[...]
    accs = [p_new * vn[:, h * HEAD_W:(h + 1) * HEAD_W] for h in range(N_HEADS)]
    for p in range(n_pages):
        pr = jnp.exp2(scores[p] - m)
        l = l + jnp.sum(pr, axis=1, keepdims=True)
        for h in range(N_HEADS):
            accs[h] = accs[h] + _bdot(pr, v_refs[p][0, pl.ds(h, page, stride=N_HEADS), :])
    lam = _lambda(lam_ref, lam_init)
    outs = []
    for h in range(N_HEADS):
        o = accs[h] / l
        d = o[2 * h:2 * h + 1, :] - lam * o[2 * h + 1:2 * h + 2, :]
        outs.append(_rmsnorm(d, g_ref[...]) * (1.0 - lam_init))
    o_ref[...] = jnp.concatenate(outs, axis=1)


def _dattn_sample(q, k_new, v_new, kt_pool, v_pool, pool_base, page_table, lam_p, subln_g, lam_init):
    DB = q.shape[0]
    n_pages = page_table.shape[1]
    page = kt_pool.shape[2]
    rows = math.gcd(DB, 2)
    tok = pl.BlockSpec((rows, 1, MIX_W), lambda b, pt: (b, 0, 0))

    def k_spec(r, p):
        return pl.BlockSpec((1, MIX_W, page), lambda b, pt: (pool_base + pt[rows * b + r, p], 0, 0))

    def v_spec(r, p):
        return pl.BlockSpec((1, page * N_HEADS, HEAD_W), lambda b, pt: (pool_base + pt[rows * b + r, p], 0, 0))

    pages = [(r, p) for r in range(rows) for p in range(n_pages)]
    grid_spec = pltpu.PrefetchScalarGridSpec(
        num_scalar_prefetch=1,
        grid=(DB // rows,),
        in_specs=[pl.BlockSpec((4, DH_A), lambda b, pt: (0, 0)),
                  pl.BlockSpec((1, HEAD_W), lambda b, pt: (0, 0)), tok, tok, tok]
                 + [k_spec(r, p) for r, p in pages] + [v_spec(r, p) for r, p in pages],
        out_specs=tok,
    )
    out = pl.pallas_call(
        functools.partial(_dattn_sample_kernel, n_pages=n_pages, page=page, lam_init=lam_init, rows=rows),
        out_shape=jax.ShapeDtypeStruct((DB, 1, MIX_W), F32),
        grid_spec=grid_spec,
        compiler_params=_cparams(1),
        name="dattn_sample",
    )(page_table, lam_p, subln_g.reshape(1, -1), q.reshape(DB, 1, MIX_W), k_new.reshape(DB, 1, MIX_W),
      v_new.reshape(DB, 1, MIX_W), *([kt_pool] * len(pages)), *([v_pool] * len(pages)))
    return out.reshape(DB, MIX_W)


def _hgrn_step_kernel(h_ref, s0_ref, lb_ref, ng_ref, o_ref, s_ref, *, bt):
    def body(i, carry):
        row = h_ref[i]
        heads = range(N_HEADS)
        blk = lambda n: row[:, n * MIX_W:(n + 1) * MIX_W]
        lb = lb_ref[...]
        f = lb + (1.0 - lb) * _sigmoid(blk(4))
        q, v = _silu(blk(3)), blk(5)
        f_col = [_col(_head(f, h)) for h in heads]
        s_new = [f_col[h] * s0_ref[i, h] + (1.0 - f_col[h]) * _head(v, h) for h in heads]
        for h in heads:
            s_ref[i, h] = s_new[h]
        o = [_bdot(jnp.broadcast_to(_head(q, h), (SUBLANES, HEAD_W)), s_new[h])[0:1, :] for h in heads]
        o = jnp.concatenate([_rmsnorm(o[h], ng_ref[...]) for h in heads], axis=1)
        o_ref[i] = o * _silu(blk(6))
        return carry

    lax.fori_loop(0, bt, body, 0, unroll=2)


def _hgrn_step(h, s0, lb, norm_g, bt):
    DB, W = h.shape
    bt = min(bt, DB)
    state = pl.BlockSpec((bt, N_HEADS, HEAD_W, HEAD_W), lambda i: (i, 0, 0, 0))
    o, s = pl.pallas_call(
        functools.partial(_hgrn_step_kernel, bt=bt),
        out_shape=(jax.ShapeDtypeStruct((DB, 1, MIX_W), F32), jax.ShapeDtypeStruct(s0.shape, F32)),
        grid=(DB // bt,),
        in_specs=[pl.BlockSpec((bt, 1, W), lambda i: (i, 0, 0)), state,
                  pl.BlockSpec((1, MIX_W), lambda i: (0, 0)), pl.BlockSpec((1, HEAD_W), lambda i: (0, 0))],
        out_specs=(pl.BlockSpec((bt, 1, MIX_W), lambda i: (i, 0, 0)), state),
        compiler_params=_cparams(1),
        name="hgrn_step",
    )(h.reshape(DB, 1, W), s0, lb.reshape(1, -1), norm_g.reshape(1, -1))
    return o.reshape(DB, MIX_W), s


def _od_step_kernel(h_ref, gt_ref, bif_ref, cos_ref, sin_ref, sc0_ref, dc0_ref, dn0_ref, dm0_ref, cng_ref, dng_ref,
                    oc_ref, od_ref, sc_ref, dc_ref, dn_ref, dm_ref, *, bt):
    scale = HEAD_W ** -0.5

    def body(i, carry):
        row = h_ref[i]
        cos, sin = cos_ref[...], sin_ref[...]
        gates = gt_ref[i] + bif_ref[...]
        m_row = dm0_ref[i]
        heads = range(N_HEADS)
        blk = lambda n: row[:, n * MIX_W:(n + 1) * MIX_W]
        q = _rope(blk(0), cos, sin, HEAD_W // 2)
        k = _rope(blk(1), cos, sin, HEAD_W // 2) * scale
        cv, dk, dv, dq = blk(2), blk(5), blk(6), blk(4) * scale
        lf = _log_sigmoid(gates)
        m_new = [jnp.maximum(lf[:, N_HEADS + h:N_HEADS + h + 1] + m_row[:, h:h + 1], gates[:, h:h + 1])
                 for h in heads]
        c_scale = [jnp.exp(lf[:, N_HEADS + h:N_HEADS + h + 1] + m_row[:, h:h + 1] - m_new[h]) for h in heads]
        kw = [_head(dk, h) * jnp.exp(gates[:, h:h + 1] - m_new[h]) for h in heads]
        k_col = [_col(_head(k, h)) for h in heads]
        kw_col = [_col(kw[h]) for h in heads]
        s_new = [math.exp(_log_gamma(h)) * sc0_ref[i, h] + k_col[h] * _head(cv, h) for h in heads]
        c_new = [c_scale[h] * dc0_ref[i, h] + kw_col[h] * _head(dv, h) for h in heads]
        n_new = [c_scale[h] * dn0_ref[i, h:h + 1, :] + kw[h] for h in heads]
        for h in heads:
            sc_ref[i, h] = s_new[h]
            dc_ref[i, h] = c_new[h]
            dn_ref[i, h:h + 1, :] = n_new[h]
        rows8 = lambda x: jnp.broadcast_to(x, (SUBLANES, HEAD_W))
        o = [_bdot(rows8(_head(q, h)), s_new[h])[0:1, :] for h in heads]
        num = [_bdot(rows8(_head(dq, h)), c_new[h])[0:1, :] for h in heads]
        den = [jnp.sum(_head(dq, h) * n_new[h], axis=1, keepdims=True) for h in heads]
        hh = [num[h] / jnp.maximum(jnp.abs(den[h]), jnp.exp(-m_new[h])) for h in heads]
        oc = jnp.concatenate([_groupnorm(o[h], cng_ref[...]) for h in heads], axis=1)
        od = jnp.concatenate([_groupnorm(hh[h], dng_ref[...]) for h in heads], axis=1)
        oc_ref[i] = oc * _silu(blk(3))
        od_ref[i] = od * _sigmoid(blk(7))
        dm_ref[i] = jnp.concatenate(m_new, axis=1)
        return carry

    lax.fori_loop(0, bt, body, 0, unroll=2)


def _od_step(h, gates, b_if, cos_t, sin_t, sc0, dc0, dn0, dm0, c_norm_g, d_norm_g, bt):
    DB, W = h.shape
    bt = min(bt, DB)
    mat = pl.BlockSpec((bt, N_HEADS, HEAD_W, HEAD_W), lambda i: (i, 0, 0, 0))
    nblk = pl.BlockSpec((bt, N_HEADS, HEAD_W), lambda i: (i, 0, 0))
    mblk = pl.BlockSpec((bt, 1, N_HEADS), lambda i: (i, 0, 0))
    vec = pl.BlockSpec((1, LANES), lambda i: (0, 0))
    out = pl.BlockSpec((bt, 1, MIX_W), lambda i: (i, 0, 0))
    b_pad = jnp.zeros((1, LANES), F32).at[0, :2 * N_HEADS].set(b_if)
    oc, od, sc, dc, dn, dm = pl.pallas_call(
        functools.partial(_od_step_kernel, bt=bt),
        out_shape=(jax.ShapeDtypeStruct((DB, 1, MIX_W), F32), jax.ShapeDtypeStruct((DB, 1, MIX_W), F32),
                   jax.ShapeDtypeStruct(sc0.shape, F32), jax.ShapeDtypeStruct(dc0.shape, F32),
                   jax.ShapeDtypeStruct(dn0.shape, F32), jax.ShapeDtypeStruct((DB, 1, N_HEADS), F32)),
        grid=(DB // bt,),
        in_specs=[pl.BlockSpec((bt, 1, W), lambda i: (i, 0, 0)),
                  pl.BlockSpec((bt, 1, LANES), lambda i: (i, 0, 0)), vec, vec, vec,
                  mat, mat, nblk, mblk, vec, vec],
        out_specs=(out, out, mat, mat, nblk, mblk),
        compiler_params=_cparams(1),
        name="od_step",
    )(h.reshape(DB, 1, W), gates.reshape(DB, 1, LANES), b_pad, cos_t, sin_t, sc0, dc0, dn0,
      dm0.reshape(DB, 1, N_HEADS), c_norm_g.reshape(1, -1), d_norm_g.reshape(1, -1))
    return oc.reshape(DB, MIX_W), od.reshape(DB, MIX_W), sc, dc, dn, dm.reshape(DB, N_HEADS)


X_HALVES = DH_X // LANES
X_ROWS = N_HEADS * X_HALVES


def _xattn_sample_kernel(q_ref, mk_ref, mv_ref, o_ref, *, xb):
    rows = range(xb)
    n_cols = mk_ref.shape[1]
    row = lax.broadcasted_iota(jnp.int32, (X_ROWS, n_cols), 0)
    col = lax.broadcasted_iota(jnp.int32, (X_ROWS, n_cols), 1)
    own = (col & (X_ROWS - 1)) == row
    own4 = own[:N_HEADS]
    q = [q_ref[t] * (DH_X ** -0.5) for t in rows]
    q2 = [jnp.concatenate([q[t][:, h * DH_X + c * LANES:h * DH_X + (c + 1) * LANES]
                           for c in range(X_HALVES) for h in range(N_HEADS)], axis=0) for t in rows]
    s2 = [jnp.where(own, _bdot_nt(q2[t], mk_ref[t]), 0.0) for t in rows]
    s4 = [jnp.where(own4, s2[t][:N_HEADS] + pltpu.roll(s2[t], n_cols - N_HEADS, 1)[N_HEADS:], NEG_INF) for t in rows]
    p4 = [jnp.exp(s4[t] - jnp.max(s4[t], axis=1, keepdims=True)) for t in rows]
    p4 = [p4[t] / jnp.sum(p4[t], axis=1, keepdims=True) for t in rows]
    p2 = [jnp.concatenate([p4[t], pltpu.roll(p4[t], N_HEADS, 1)], axis=0) for t in rows]
    o2 = [_bdot(p2[t], mv_ref[t]) for t in rows]
    for t in rows:
        o_ref[t] = jnp.concatenate([o2[t][c * N_HEADS + h:c * N_HEADS + h + 1, :]
                                    for h in range(N_HEADS) for c in range(X_HALVES)], axis=1)


def _xattn_sample(q, mk, mv, base):
    DB = q.shape[0]
    n_rows = mk.shape[1]
    xb = math.gcd(DB, 4)
    tok = pl.BlockSpec((xb, 1, D_MODEL), lambda b: (b, 0, 0))
    mem = pl.BlockSpec((xb, n_rows, LANES), lambda b: (base // xb + b, 0, 0))
    assert base % xb == 0
    out = pl.pallas_call(
        functools.partial(_xattn_sample_kernel, xb=xb),
        out_shape=jax.ShapeDtypeStruct((DB, 1, D_MODEL), F32),
        grid=(DB // xb,),
        in_specs=[tok, mem, mem],
        out_specs=tok,
        compiler_params=_cparams(1),
        name="xattn_sample",
    )(q.reshape(DB, 1, D_MODEL), mk, mv)
    return out.reshape(DB, D_MODEL)


def _mem_rows(cache):
    n_l, DB, n_mem = cache.shape[:3]
    c = cache.reshape(n_l, DB, n_mem, N_HEADS, X_HALVES, LANES).transpose(0, 1, 2, 4, 3, 5)
    return c.reshape(n_l * DB, n_mem * X_ROWS, LANES)


def _ffn_sample_kernel(x_ref, up_ref, gate_ref, b0_ref, b1_ref, cw_ref, cb_ref, wd_ref, g_ref, b_ref, o_ref, acc_ref):
    j = pl.program_id(0)
    conv = (cb_ref[...] + cw_ref[0:1, :] * b0_ref[...] + cw_ref[1:2, :] * b1_ref[...]
            + cw_ref[2:3, :] * up_ref[...])
    y = _bdot(jax.nn.gelu(conv) * gate_ref[...], wd_ref[...])

    @pl.when(j == 0)
    def _():
        acc_ref[...] = y

    @pl.when(j > 0)
    def _():
        acc_ref[...] = acc_ref[...] + y

    @pl.when(j == pl.num_programs(0) - 1)
    def _():
        o_ref[...] = _layernorm(DN_ALPHA * x_ref[...] + acc_ref[...], g_ref[...], b_ref[...])


def _ffn_sample(x, ug, buf0, buf1, conv_w, conv_b, w_down, g, b, tf):
    DB = x.shape[0]
    d_ff = _wshape(w_down)[0]
    nf = d_ff // tf
    ff = lambda off: pl.BlockSpec((DB, tf), lambda j: (0, off + j))
    full = lambda shape: pl.BlockSpec(shape, lambda j: (0,) * len(shape))
    w_down, down_spec = _wspec(w_down, (tf, D_MODEL), lambda j: (j, 0))
    return pl.pallas_call(
        _ffn_sample_kernel,
        out_shape=jax.ShapeDtypeStruct((DB, D_MODEL), F32),
        grid=(nf,),
        in_specs=[full((DB, D_MODEL)), ff(0), ff(nf), ff(0), ff(0),
                  pl.BlockSpec((3, tf), lambda j: (0, j)), pl.BlockSpec((1, tf), lambda j: (0, j)),
                  down_spec, full((1, D_MODEL)), full((1, D_MODEL))],
        out_specs=full((DB, D_MODEL)),
        scratch_shapes=[pltpu.VMEM((DB, D_MODEL), F32)],
        compiler_params=_cparams(1),
        name="ffn_sample",
    )(x, ug, ug, buf0, buf1, conv_w, conv_b.reshape(1, -1), w_down, g.reshape(1, -1), b.reshape(1, -1))


def kernel(x_prompt, x_sample, cache_a_k, cache_a_v, state_b, state_c, state_d_c, state_d_n, state_d_m,
           cache_mem_k, cache_mem_v, state_conv, page_table, mem_prompt,
           ev_w_in, ev_w_out, ev_lam, ev_subln_g, ev_lb_logits, ev_b_norm_g,
           od_w_in, od_b_if, od_w_out, od_c_norm_g, od_d_norm_g,
           ln_g, ln_b, xa_wq, xa_wkv, xa_wo, ffn_w_up, ffn_conv_w, ffn_conv_b, ffn_w_down):
    B, T, _ = x_prompt.shape
    DB, t_s, _ = x_sample.shape
    assert t_s == 1, "the sample group is a single decoding step"
    assert T % CHUNK == 0
    n_pool, page = cache_a_k.shape[1], cache_a_k.shape[2]
    n_pages = page_table.shape[1]
    past = n_pages * page
    n_mem = mem_prompt.shape[1]
    d_ff = ffn_w_down.shape[1]
    tf = d_ff // 2
    N = B * T
    TM = ROWS_MATMUL
    cols = lambda w: _wshape(w)[1] // SAMPLE_COL_TILES

    pos_p = jnp.arange(T)
    pos_s = jnp.full((DB,), past, jnp.int32)
    lb_table = jnp.cumsum(jax.nn.softmax(ev_lb_logits.astype(F32), axis=0), axis=0)

    xp = x_prompt.reshape(N, D_MODEL)
    xs = x_sample.reshape(DB, D_MODEL)
    outs = {k: [] for k in ("ak_p", "av_p", "ak_s", "av_s", "sb_p", "sb_s", "sc_p", "sc_s", "dc_p", "dc_s",
                            "dn_p", "dn_s", "dm_p", "dm_s", "mk_p", "mv_p", "cv_p", "cv_s")}
    swap = lambda s: jnp.swapaxes(s, -1, -2)
    mem_k_rows, mem_v_rows = _mem_rows(cache_mem_k), _mem_rows(cache_mem_v)
    wq_all, wkv_all, wo_all = xa_wq.astype(BF16), xa_wkv.astype(BF16), xa_wo.astype(BF16)
    w_up_all, w_down_all = ffn_w_up.astype(BF16), ffn_w_down.astype(BF16)

    for l in range(DEPTH):
        j = l // 2
        if l % 2 == 0:
            lam_init = 0.8 - 0.6 * math.exp(-0.3 * l)
            w_in = ev_w_in[j].astype(BF16)
            w_out = ev_w_out[j].astype(BF16)
            cos_t, sin_t = _rope_tables(pos_p, DH_A)
            qt, kb, kt32, vt, v_rows, h_b = _ev_in_prompt(xp.reshape(B, T, D_MODEL), w_in, cos_t, sin_t, TM)
            o_a = _dattn_prompt(qt, kb, vt, ev_lam[j], ev_subln_g[j], lam_init, ATTN_BLOCK)
            o_b, st = _hgrn_prompt(h_b, lb_table[j], TIME_BLOCK_EV)
            outs["ak_p"].append(kt32.reshape(B, N_HEADS, 2, DH_A, T).transpose(0, 4, 1, 2, 3))
            outs["av_p"].append(v_rows.reshape(B, T, N_HEADS, HEAD_W))
            outs["sb_p"].append(swap(st))
            xp = _proj_ln([o_a.reshape(N, MIX_W), o_b.reshape(N, MIX_W)], w_out,
                          xp, ln_g[l, 0], ln_b[l, 0], ROWS_LIGHT, "ev_out_prompt",
                          gates=[None, ("rms", ev_b_norm_g[j], "silu", h_b.reshape(N, -1), 3)])
            hs = _matmul(xs, w_in, DB, cols(w_in), "ev_in_sample")
            cos_s, sin_s = _rope_tables(pos_s, DH_A)
            qs, ks32, kts32 = _ev_prep_sample(hs, cos_s, sin_s)
            vs32 = hs[:, 2 * MIX_W:3 * MIX_W]
            kt_pool = cache_a_k.transpose(0, 1, 3, 4, 5, 2).reshape(-1, MIX_W, page)
            v_pool = cache_a_v.reshape(-1, page * N_HEADS, HEAD_W)
            oa_s = _dattn_sample(qs, ks32, vs32, kt_pool, v_pool, j * n_pool, page_table,
                                 ev_lam[j], ev_subln_g[j], lam_init)
            ob_s, sb_s = _hgrn_step(hs, state_b[j], lb_table[j], ev_b_norm_g[j], STEP_ROWS)
            outs["ak_s"].append(kts32.reshape(N_HEADS, 2, DH_A, DB).transpose(3, 0, 1, 2)[:, None])
            outs["av_s"].append(vs32.reshape(DB, 1, N_HEADS, HEAD_W))
            outs["sb_s"].append(sb_s)
            xs = _proj_ln([oa_s, ob_s], w_out, xs, ln_g[l, 0], ln_b[l, 0], DB, "ev_out_sample")
        else:
            n_main = 8 * MIX_W
            w_in = od_w_in[j][:, :n_main].astype(BF16)
            w_gate = jnp.pad(od_w_in[j][:, n_main:], ((0, 0), (0, LANES - 2 * N_HEADS))).astype(BF16)
            w_out = od_w_out[j].astype(BF16)
            cos_t, sin_t = _rope_tables(pos_p, HEAD_W)
            h, gates = _od_in_prompt(xp, w_in, w_gate, cos_t, sin_t, TM)
            o_c, o_d, sct, dct, dn, dm = _od_prompt(h.reshape(B, T, -1), gates.reshape(B, T, -1), od_b_if[j],
                                                    TIME_BLOCK_OD)
            outs["sc_p"].append(swap(sct))
            outs["dc_p"].append(swap(dct))
            outs["dn_p"].append(dn)
            outs["dm_p"].append(dm)
            xp = _proj_ln([o_c.reshape(N, MIX_W), o_d.reshape(N, MIX_W)], w_out,
                          xp, ln_g[l, 0], ln_b[l, 0], ROWS_LIGHT, "od_out_prompt",
                          gates=[("group", od_c_norm_g[j], "silu", h, 3), ("group", od_d_norm_g[j], "sigmoid", h, 7)])
            hs = _matmul(xs, w_in, DB, cols(w_in), "od_in_sample")
            gates_s = _matmul(xs, w_gate, DB, LANES, "od_gates_sample")
            cos_s, sin_s = _rope_tables(pos_s[:1], HEAD_W)
            oc_s, od_s, sc_s, dc_s, dn_s, dm_s = _od_step(hs, gates_s, od_b_if[j], cos_s, sin_s,
                                                          state_c[j], state_d_c[j], state_d_n[j], state_d_m[j],
                                                          od_c_norm_g[j], od_d_norm_g[j], STEP_ROWS)
            outs["sc_s"].append(sc_s)
            outs["dc_s"].append(dc_s)
            outs["dn_s"].append(dn_s)
            outs["dm_s"].append(dm_s)
            xs = _proj_ln([oc_s, od_s], w_out, xs, ln_g[l, 0], ln_b[l, 0], DB, "od_out_sample")

        wq, wo = (wq_all, l), (wo_all, l)
        mkv = _matmul(mem_prompt.reshape(B * n_mem, D_MODEL), (wkv_all, l), ROWS_MATMUL, cols((wkv_all, l)), "mem_kv")
        mk, mv = mkv[:, :D_MODEL], mkv[:, D_MODEL:]
        outs["mk_p"].append(mk.reshape(B, n_mem, N_HEADS, DH_X))
        outs["mv_p"].append(mv.reshape(B, n_mem, N_HEADS, DH_X))
        xp = _xattn_prompt(xp, wq, mk.astype(BF16).reshape(B, n_mem, D_MODEL),
                           mv.astype(BF16).reshape(B, n_mem, D_MODEL), wo, ln_g[l, 1], ln_b[l, 1], T, ROWS_LIGHT)
        q_s = _matmul(xs, wq, DB, cols(wq), "xattn_q_sample")
        xo_s = _xattn_sample(q_s, mem_k_rows, mem_v_rows, l * DB)
        xs = _proj_ln([xo_s], wo, xs, ln_g[l, 1], ln_b[l, 1], DB, "xattn_out_sample")

        w_up, w_down = (w_up_all, l), (w_down_all, l)
        tm_f = min(TM, T)
        xp, tails = _ffn_prompt(xp, w_up, ffn_conv_w[l], ffn_conv_b[l], w_down, ln_g[l, 2], ln_b[l, 2], T, TM, d_ff)
        tails = tails.reshape(B, T // tm_f, SUBLANES, d_ff)
        outs["cv_p"].append(tails[:, -1, SUBLANES - 2:, :])
        ug_s = _matmul(xs, w_up, DB, cols(w_up), "ffn_up_sample")
        buf = state_conv[l]
        xs = _ffn_sample(xs, ug_s, buf[:, 0, :], buf[:, 1, :], ffn_conv_w[l], ffn_conv_b[l], w_down,
                         ln_g[l, 2], ln_b[l, 2], tf)
        outs["cv_s"].append(jnp.stack([buf[:, 1, :], ug_s[:, :d_ff]], axis=1))

    st = lambda k: jnp.stack(outs[k])
    return (xp.reshape(B, T, D_MODEL), xs.reshape(DB, 1, D_MODEL),
            st("ak_p"), st("av_p"), st("ak_s"), st("av_s"), st("sb_p"), st("sb_s"),
            st("sc_p"), st("sc_s"), st("dc_p"), st("dc_s"), st("dn_p"), st("dn_s"), st("dm_p"), st("dm_s"),
            st("mk_p"), st("mv_p"), st("cv_p"), st("cv_s"))
```
